```python
import math, functools
import jax, jax.numpy as jnp
from jax import lax
import numpy as np

D_MODEL = 1024
BATCH = 8
SEQ = 4096
DEPTH = 2

GRID_W = 64
CTX_LEN = 256
N_MOD = 6
NORM_EPS = 1e-6
D_FF = -(-8 * D_MODEL // (3 * 256)) * 256

D_MIX = D_MODEL
GLA_HEADS = 4
GLA_DV = D_MIX // 2 // GLA_HEADS
GLA_DK = GLA_DV // 2
GLA_LR = 16
GLA_TAU = 16.0
GLA_CHUNK = 64
GLA_HK = GLA_HEADS * GLA_DK
GLA_HV = GLA_HEADS * GLA_DV
GMLP_GROUPS = 4
GMLP_WIDTH = D_MIX // 2
GMLP_GC = GMLP_WIDTH // GMLP_GROUPS
GMLP_CHUNK = 128
AB_STATE_SPLITS = (GLA_HK, GLA_HV, GLA_LR, GLA_LR)
AB_OUT_SPLITS = (GLA_HK, GLA_HV, GMLP_WIDTH, GMLP_WIDTH)
AB_IN = sum(AB_STATE_SPLITS + AB_OUT_SPLITS)

SSD_INNER = 2 * D_MODEL
SSD_HEADDIM = 64
SSD_HEADS = SSD_INNER // SSD_HEADDIM
SSD_GROUPS = 4
SSD_HPG = SSD_HEADS // SSD_GROUPS
SSD_STATE = 128
SSD_CHUNK = 128
SSD_CONV = 5
SSD_GS = SSD_GROUPS * SSD_STATE
SSD_CONV_DIM = SSD_INNER + 2 * SSD_GS
SSD_IN = SSD_CONV_DIM + 2 * SSD_HEADS + SSD_INNER

kernel_name = 'hybrid_gla_gmlp_ssd_prefix_dit'


def rms_norm(x, g):
    xf = x.astype(jnp.float32)
    y = xf * lax.rsqrt(jnp.mean(xf * xf, axis=-1, keepdims=True) + NORM_EPS)
    return y.astype(x.dtype) * g


def group_rms_norm(x, g, groups):
    shp = x.shape
    xf = x.astype(jnp.float32).reshape(shp[:-1] + (groups, shp[-1] // groups))
    y = xf * lax.rsqrt(jnp.mean(xf * xf, axis=-1, keepdims=True) + NORM_EPS)
    return y.reshape(shp).astype(x.dtype) * g


def layer_norm(x, g):
    xf = x.astype(jnp.float32)
    mu = jnp.mean(xf, axis=-1, keepdims=True)
    var = jnp.mean(jnp.square(xf - mu), axis=-1, keepdims=True)
    return ((xf - mu) * lax.rsqrt(var + NORM_EPS)).astype(x.dtype) * g


def modulate(h, shift, scale):
    return h * (1 + scale) + shift


def split_cols(a, sizes):
    idx = [int(i) for i in np.cumsum(sizes)[:-1]]
    return jnp.split(a, idx, axis=-1)


def swiglu(h, w_in, w_out):
    g, u = jnp.split(h @ w_in, 2, axis=-1)
    return (jax.nn.silu(g) * u) @ w_out


def to_col_major(x):
    bsz, t, d = x.shape
    rows = t // GRID_W
    return x.reshape(bsz, rows, GRID_W, d).transpose(0, 2, 1, 3).reshape(bsz, t, d)


def to_row_major(x):
    bsz, t, d = x.shape
    rows = t // GRID_W
    return x.reshape(bsz, GRID_W, rows, d).transpose(0, 2, 1, 3).reshape(bsz, t, d)


def dwconv_centred(x, w, b):
    pad = (w.shape[0] - 1) // 2
    y = lax.conv_general_dilated(x, w.astype(x.dtype)[:, None, :], window_strides=(1,),
                                 padding=[(pad, pad)], dimension_numbers=('NWC', 'WIO', 'NWC'),
                                 feature_group_count=x.shape[-1])
    return y + b


def bidir_prefix_scan(scan_f, scan_b, ctx_f, ctx_b, lat_f, lat_b, s0, ctx_out):
    flip = lambda a: None if a is None else jnp.flip(a, axis=1)
    y_cf, s_cf = scan_f(*ctx_f, s0, ctx_out)
    y_xf, _ = scan_f(*lat_f, s_cf, True)
    y_cb, s_cb = scan_b(*[flip(a) for a in ctx_b], s0, ctx_out)
    y_xb, _ = scan_b(*[flip(a) for a in lat_b], s_cb, True)
    y_x = y_xf + flip(y_xb)
    y_c = y_cf + flip(y_cb) if ctx_out else None
    return y_x, y_c


def gla_chunked(k, v, log_a, q, s0, with_output):
    bsz, t, h, dk = k.shape
    dv = v.shape[-1]
    n = t // GLA_CHUNK
    chunks = lambda a: a.astype(jnp.float32).reshape(bsz, n, GLA_CHUNK, h, a.shape[-1])
    k, v, log_a = chunks(k), chunks(v), chunks(log_a)
    b = jnp.cumsum(log_a, axis=2)
    b_last = b[:, :, -1]
    d_state = jnp.einsum('bnlhk,bnlhv->bnhkv', k * jnp.exp(b_last[:, :, None] - b), v)

    def step(s, inp):
        decay, ds = inp
        return decay[..., None] * s + ds, s

    s_final, s_prev = lax.scan(step, s0, (jnp.moveaxis(jnp.exp(b_last), 1, 0), jnp.moveaxis(d_state, 1, 0)))
    if not with_output:
        return None, s_final
    q = chunks(q)
    s_prev = jnp.moveaxis(s_prev, 0, 1)
    q_dec = q * jnp.exp(b)
    o_inter = jnp.einsum('bnlhk,bnhkv->bnlhv', q_dec, s_prev)
    scores = jnp.einsum('bnlhk,bnshk->bnhls', q_dec, k * jnp.exp(-b))
    tri = jnp.tril(jnp.ones((GLA_CHUNK, GLA_CHUNK), dtype=bool))
    scores = jnp.where(tri, scores, 0.0)
    o_intra = jnp.einsum('bnhls,bnshv->bnlhv', scores, v)
    return (o_inter + o_intra).reshape(bsz, t, h, dv), s_final


def ssd_chunked(x, bm, dt, cm, s0, with_output, a_coef):
    bsz, t = x.shape[:2]
    n = t // SSD_CHUNK
    f = lambda z: z.astype(jnp.float32).reshape((bsz, n, SSD_CHUNK) + z.shape[2:])
    x, bm, dt = f(x), f(bm), f(dt)
    acum = jnp.cumsum(dt * a_coef, axis=2)
    a_last = acum[:, :, -1]
    xdt = x * dt[..., None]
    states = jnp.einsum('bclgn,bclgh,bclghp->bcghpn', bm, jnp.exp(a_last[:, :, None] - acum), xdt)

    def step(s, inp):
        decay, st = inp
        return decay[..., None, None] * s + st, s

    s_final, s_prev = lax.scan(step, s0, (jnp.moveaxis(jnp.exp(a_last), 1, 0), jnp.moveaxis(states, 1, 0)))
    if not with_output:
        return None, s_final
    cm = f(cm)
    s_prev = jnp.moveaxis(s_prev, 0, 1)
    y_off = jnp.einsum('bclgn,bcghpn->bclghp', cm, s_prev) * jnp.exp(acum)[..., None]
    at = jnp.moveaxis(acum, 2, -1)
    seg = at[..., :, None] - at[..., None, :]
    tri = jnp.tril(jnp.ones((SSD_CHUNK, SSD_CHUNK), dtype=bool))
    decay = jnp.exp(jnp.where(tri, seg, -jnp.inf))
    cb = jnp.einsum('bclgn,bcsgn->bcgls', cm, bm)
    y_diag = jnp.einsum('bcgls,bcghls,bcsghp->bclghp', cb, decay, xdt)
    return (y_diag + y_off).reshape((bsz, t) + x.shape[3:]), s_final


def ab_project(h, w_in, gate_w, gate_b, full):
    bsz, t, _ = h.shape
    sizes = AB_STATE_SPLITS + AB_OUT_SPLITS if full else AB_STATE_SPLITS
    parts = split_cols(h @ w_in[:, :sum(sizes)], sizes)
    heads = lambda a: a.reshape(bsz, t, GLA_HEADS, -1)
    k, v, lr_f, lr_b = parts[:4]
    la_f = jax.nn.log_sigmoid((lr_f @ gate_w[0] + gate_b[0]).astype(jnp.float32)) / GLA_TAU
    la_b = jax.nn.log_sigmoid((lr_b @ gate_w[1] + gate_b[1]).astype(jnp.float32)) / GLA_TAU
    state = (heads(k), heads(v), heads(la_f), heads(la_b))
    if not full:
        return state, None
    q, r, u, g = parts[4:]
    return state, (heads(q) * GLA_DK ** -0.5, r, u, g)


def gmlp_chunk_mix(u, v, vnorm_g, spatial_w, spatial_b):
    bsz, t, _ = u.shape
    n = t // GMLP_CHUNK
    u = jax.nn.gelu(u)
    v = layer_norm(jax.nn.gelu(v), vnorm_g).reshape(bsz, n, GMLP_CHUNK, GMLP_GROUPS, GMLP_GC)
    s = jnp.einsum('gts,bnsgc->bntgc', spatial_w, v) + spatial_b.T[:, :, None]
    return u * s.reshape(bsz, t, GMLP_WIDTH)


def mixer_gla_gmlp(hx, hc, w_in, gate_w, gate_b, gla_norm_g, vnorm_g, spatial_w, spatial_b, w_out, ctx_out):
    (kx, vx, lfx, lbx), (qx, rx, ux, gx) = ab_project(hx, w_in, gate_w, gate_b, True)
    (kc, vc, lfc, lbc), rest_c = ab_project(hc, w_in, gate_w, gate_b, ctx_out)
    qc = rest_c[0] if ctx_out else None
    s0 = jnp.zeros((hx.shape[0], GLA_HEADS, GLA_DK, GLA_DV), jnp.float32)
    ox, oc = bidir_prefix_scan(gla_chunked, gla_chunked,
                               [kc, vc, lfc, qc], [kc, vc, lbc, qc],
                               [kx, vx, lfx, qx], [kx, vx, lbx, qx], s0, ctx_out)

    def merge(o, r, u, g):
        bsz, t = o.shape[:2]
        a = group_rms_norm(o.reshape(bsz, t, GLA_HV), gla_norm_g, GLA_HEADS) * jax.nn.silu(r)
        b = gmlp_chunk_mix(u, g, vnorm_g, spatial_w, spatial_b)
        return jnp.concatenate([a.astype(b.dtype), b], axis=-1) @ w_out

    yx = merge(ox, rx, ux, gx)
    yc = merge(oc, rest_c[1], rest_c[2], rest_c[3]) if ctx_out else None
    return yx, yc


def ssd_project(h, w_in, conv_w, conv_b, dt_bias, full):
    bsz, t, _ = h.shape
    if full:
        p = h @ w_in
        xbc = jax.nn.silu(dwconv_centred(p[..., :SSD_CONV_DIM], conv_w, conv_b))
        xs, bm, cm = split_cols(xbc, (SSD_INNER, SSD_GS, SSD_GS))
        dt_raw = p[..., SSD_CONV_DIM:SSD_CONV_DIM + 2 * SSD_HEADS]
        z = p[..., SSD_CONV_DIM + 2 * SSD_HEADS:]
        cm = cm.reshape(bsz, t, SSD_GROUPS, SSD_STATE)
    else:
        nxb = SSD_INNER + SSD_GS
        xb = jax.nn.silu(dwconv_centred(h @ w_in[:, :nxb], conv_w[:, :nxb], conv_b[:nxb]))
        xs, bm = split_cols(xb, (SSD_INNER, SSD_GS))
        dt_raw = h @ w_in[:, SSD_CONV_DIM:SSD_CONV_DIM + 2 * SSD_HEADS]
        cm = None
        z = None
    dt = jax.nn.softplus(dt_raw.astype(jnp.float32).reshape(bsz, t, 2, SSD_HEADS) + dt_bias)
    dt = dt.reshape(bsz, t, 2, SSD_GROUPS, SSD_HPG)
    xs = xs.reshape(bsz, t, SSD_GROUPS, SSD_HPG, SSD_HEADDIM)
    bm = bm.reshape(bsz, t, SSD_GROUPS, SSD_STATE)
    return xs, bm, cm, dt[:, :, 0], dt[:, :, 1], z


def mixer_ssd(hx, hc, w_in, conv_w, conv_b, dt_bias, a_log, d_skip, norm_g, w_out, ctx_out):
    xx, bx, cx, dfx, dbx, zx = ssd_project(hx, w_in, conv_w, conv_b, dt_bias, True)
    xc, bc, cc, dfc, dbc, zc = ssd_project(hc, w_in, conv_w, conv_b, dt_bias, ctx_out)
    a = -jnp.exp(a_log.astype(jnp.float32)).reshape(2, SSD_GROUPS, SSD_HPG)
    s0 = jnp.zeros((hx.shape[0], SSD_GROUPS, SSD_HPG, SSD_HEADDIM, SSD_STATE), jnp.float32)
    yx, yc = bidir_prefix_scan(functools.partial(ssd_chunked, a_coef=a[0]),
                               functools.partial(ssd_chunked, a_coef=a[1]),
                               [xc, bc, dfc, cc], [xc, bc, dbc, cc],
                               [xx, bx, dfx, cx], [xx, bx, dbx, cx], s0, ctx_out)
    d_h = d_skip.reshape(SSD_GROUPS, SSD_HPG)[..., None]

    def finish(y, xs, z):
        bsz, t = y.shape[:2]
        y = (y + d_h * xs).reshape(bsz, t, SSD_INNER)
        y = group_rms_norm(y * jax.nn.silu(z), norm_g, SSD_GROUPS)
        return y.astype(z.dtype) @ w_out

    out_x = finish(yx, xx, zx)
    out_c = finish(yc, xc, zc) if ctx_out else None
    return out_x, out_c


def _fwd_setup_inputs(seed: int = 0) -> dict:
    key = jax.random.key(seed)
    ks = iter(jax.random.split(key, 40))

    def nrm(shape, scale):
        return jax.random.normal(next(ks), shape, jnp.float32) * scale

    ne, no = (DEPTH + 1) // 2, DEPTH // 2
    dt0 = jnp.exp(jax.random.uniform(next(ks), (no, 2, SSD_HEADS), jnp.float32, math.log(1e-3), math.log(1e-1)))
    return {
        'x': nrm((BATCH, SEQ, D_MODEL), 1.0),
        'c': nrm((BATCH, D_MODEL), 1.0),
        'ctx': nrm((BATCH, CTX_LEN, D_MODEL), 1.0),
        'c_ctx': nrm((D_MODEL,), 1.0),
        'mod_w': nrm((DEPTH, D_MODEL, N_MOD * D_MODEL), D_MODEL ** -0.5),
        'mod_b': nrm((DEPTH, N_MOD * D_MODEL), 0.02),
        'norm_g': 1.0 + nrm((DEPTH, 2, D_MODEL), 0.02),
        'ffn_w_in': nrm((DEPTH, D_MODEL, 2 * D_FF), D_MODEL ** -0.5),
        'ffn_w_out': nrm((DEPTH, D_FF, D_MODEL), D_FF ** -0.5),
        'ab_w_in': nrm((ne, D_MODEL, AB_IN), D_MODEL ** -0.5),
        'ab_gate_w': nrm((ne, 2, GLA_LR, GLA_HK), GLA_LR ** -0.5),
        'ab_gate_b': nrm((ne, 2, GLA_HK), 0.1),
        'ab_gla_norm_g': 1.0 + nrm((ne, GLA_HV), 0.02),
        'ab_vnorm_g': 1.0 + nrm((ne, GMLP_WIDTH), 0.02),
        'ab_spatial_w': nrm((ne, GMLP_GROUPS, GMLP_CHUNK, GMLP_CHUNK), GMLP_CHUNK ** -0.5),
        'ab_spatial_b': 1.0 + nrm((ne, GMLP_GROUPS, GMLP_CHUNK), 0.02),
        'ab_w_out': nrm((ne, D_MIX, D_MODEL), D_MIX ** -0.5),
        'ssd_w_in': nrm((no, D_MODEL, SSD_IN), D_MODEL ** -0.5),
        'ssd_conv_w': nrm((no, SSD_CONV, SSD_CONV_DIM), SSD_CONV ** -0.5),
        'ssd_conv_b': nrm((no, SSD_CONV_DIM), 0.02),
        'ssd_dt_bias': dt0 + jnp.log(-jnp.expm1(-dt0)),
        'ssd_a_log': jnp.log(jax.random.uniform(next(ks), (no, 2, SSD_HEADS), jnp.float32, 1.0, 16.0)),
        'ssd_d': 1.0 + nrm((no, SSD_HEADS), 0.02),
        'ssd_norm_g': 1.0 + nrm((no, SSD_INNER), 0.02),
        'ssd_w_out': nrm((no, SSD_INNER, D_MODEL), SSD_INNER ** -0.5),
        'final_norm_g': 1.0 + nrm((D_MODEL,), 0.02),
    }


def _fwd_reference(x, c, ctx, c_ctx, mod_w, mod_b, norm_g, ffn_w_in, ffn_w_out,
              ab_w_in, ab_gate_w, ab_gate_b, ab_gla_norm_g, ab_vnorm_g, ab_spatial_w, ab_spatial_b, ab_w_out,
              ssd_w_in, ssd_conv_w, ssd_conv_b, ssd_dt_bias, ssd_a_log, ssd_d, ssd_norm_g, ssd_w_out,
              final_norm_g):
    sc = jax.nn.silu(c)
    sc_ctx = jax.nn.silu(c_ctx)
    for i in range(DEPTH):
        ctx_out = i < DEPTH - 1
        j = i // 2
        mx = jnp.split((sc @ mod_w[i] + mod_b[i])[:, None, :], N_MOD, axis=-1)
        mc = jnp.split(sc_ctx @ mod_w[i] + mod_b[i], N_MOD, axis=-1)
        hx = modulate(rms_norm(x, norm_g[i, 0]), mx[0], mx[1])
        hc = modulate(rms_norm(ctx, norm_g[i, 0]), mc[0], mc[1])
        if i % 2 == 0:
            yx, yc = mixer_gla_gmlp(hx, hc, ab_w_in[j], ab_gate_w[j], ab_gate_b[j], ab_gla_norm_g[j],
                                    ab_vnorm_g[j], ab_spatial_w[j], ab_spatial_b[j], ab_w_out[j], ctx_out)
        else:
            yx_cm, yc = mixer_ssd(to_col_major(hx), hc, ssd_w_in[j], ssd_conv_w[j], ssd_conv_b[j],
                                  ssd_dt_bias[j], ssd_a_log[j], ssd_d[j], ssd_norm_g[j], ssd_w_out[j], ctx_out)
            yx = to_row_major(yx_cm)
        x = x + mx[2] * yx
        x = x + mx[5] * swiglu(modulate(rms_norm(x, norm_g[i, 1]), mx[3], mx[4]), ffn_w_in[i], ffn_w_out[i])
        if ctx_out:
            ctx = ctx + mc[2] * yc
            ctx = ctx + mc[5] * swiglu(modulate(rms_norm(ctx, norm_g[i, 1]), mc[3], mc[4]), ffn_w_in[i], ffn_w_out[i])
    return rms_norm(x, final_norm_g)


import jax as _jax
import jax.numpy as _jnp

TWIN_FORMAT = 'train_step'
FWD_PARAMS = ['x', 'c', 'ctx', 'c_ctx', 'mod_w', 'mod_b', 'norm_g', 'ffn_w_in', 'ffn_w_out', 'ab_w_in', 'ab_gate_w', 'ab_gate_b', 'ab_gla_norm_g', 'ab_vnorm_g', 'ab_spatial_w', 'ab_spatial_b', 'ab_w_out', 'ssd_w_in', 'ssd_conv_w', 'ssd_conv_b', 'ssd_dt_bias', 'ssd_a_log', 'ssd_d', 'ssd_norm_g', 'ssd_w_out', 'final_norm_g']
TWIN_WEIGHTS = ['c_ctx', 'mod_w', 'mod_b', 'norm_g', 'ffn_w_in', 'ffn_w_out', 'ab_w_in', 'ab_gate_w', 'ab_gate_b', 'ab_gla_norm_g', 'ab_vnorm_g', 'ab_spatial_w', 'ab_spatial_b', 'ab_w_out', 'ssd_w_in', 'ssd_conv_w', 'ssd_conv_b', 'ssd_dt_bias', 'ssd_a_log', 'ssd_d', 'ssd_norm_g', 'ssd_w_out', 'final_norm_g']
TWIN_DIFF_INPUT = 'x'
TWIN_INPUTS = ['x', 'c', 'ctx', 'c_ctx', 'mod_w', 'mod_b', 'norm_g', 'ffn_w_in', 'ffn_w_out', 'ab_w_in', 'ab_gate_w', 'ab_gate_b', 'ab_gla_norm_g', 'ab_vnorm_g', 'ab_spatial_w', 'ab_spatial_b', 'ab_w_out', 'ssd_w_in', 'ssd_conv_w', 'ssd_conv_b', 'ssd_dt_bias', 'ssd_a_log', 'ssd_d', 'ssd_norm_g', 'ssd_w_out', 'final_norm_g', 'loss_target', 'm_c_ctx', 'm_mod_w', 'm_mod_b', 'm_norm_g', 'm_ffn_w_in', 'm_ffn_w_out', 'm_ab_w_in', 'm_ab_gate_w', 'm_ab_gate_b', 'm_ab_gla_norm_g', 'm_ab_vnorm_g', 'm_ab_spatial_w', 'm_ab_spatial_b', 'm_ab_w_out', 'm_ssd_w_in', 'm_ssd_conv_w', 'm_ssd_conv_b', 'm_ssd_dt_bias', 'm_ssd_a_log', 'm_ssd_d', 'm_ssd_norm_g', 'm_ssd_w_out', 'm_final_norm_g', 'v_c_ctx', 'v_mod_w', 'v_mod_b', 'v_norm_g', 'v_ffn_w_in', 'v_ffn_w_out', 'v_ab_w_in', 'v_ab_gate_w', 'v_ab_gate_b', 'v_ab_gla_norm_g', 'v_ab_vnorm_g', 'v_ab_spatial_w', 'v_ab_spatial_b', 'v_ab_w_out', 'v_ssd_w_in', 'v_ssd_conv_w', 'v_ssd_conv_b', 'v_ssd_dt_bias', 'v_ssd_a_log', 'v_ssd_d', 'v_ssd_norm_g', 'v_ssd_w_out', 'v_final_norm_g']
TWIN_OUTPUTS = ['loss', 'grad_x', 'grad_c_ctx', 'grad_mod_w', 'grad_mod_b', 'grad_norm_g', 'grad_ffn_w_in', 'grad_ffn_w_out', 'grad_ab_w_in', 'grad_ab_gate_w', 'grad_ab_gate_b', 'grad_ab_gla_norm_g', 'grad_ab_vnorm_g', 'grad_ab_spatial_w', 'grad_ab_spatial_b', 'grad_ab_w_out', 'grad_ssd_w_in', 'grad_ssd_conv_w', 'grad_ssd_conv_b', 'grad_ssd_dt_bias', 'grad_ssd_a_log', 'grad_ssd_d', 'grad_ssd_norm_g', 'grad_ssd_w_out', 'grad_final_norm_g', 'delta_c_ctx', 'delta_mod_w', 'delta_mod_b', 'delta_norm_g', 'delta_ffn_w_in', 'delta_ffn_w_out', 'delta_ab_w_in', 'delta_ab_gate_w', 'delta_ab_gate_b', 'delta_ab_gla_norm_g', 'delta_ab_vnorm_g', 'delta_ab_spatial_w', 'delta_ab_spatial_b', 'delta_ab_w_out', 'delta_ssd_w_in', 'delta_ssd_conv_w', 'delta_ssd_conv_b', 'delta_ssd_dt_bias', 'delta_ssd_a_log', 'delta_ssd_d', 'delta_ssd_norm_g', 'delta_ssd_w_out', 'delta_final_norm_g', 'new_m_c_ctx', 'new_m_mod_w', 'new_m_mod_b', 'new_m_norm_g', 'new_m_ffn_w_in', 'new_m_ffn_w_out', 'new_m_ab_w_in', 'new_m_ab_gate_w', 'new_m_ab_gate_b', 'new_m_ab_gla_norm_g', 'new_m_ab_vnorm_g', 'new_m_ab_spatial_w', 'new_m_ab_spatial_b', 'new_m_ab_w_out', 'new_m_ssd_w_in', 'new_m_ssd_conv_w', 'new_m_ssd_conv_b', 'new_m_ssd_dt_bias', 'new_m_ssd_a_log', 'new_m_ssd_d', 'new_m_ssd_norm_g', 'new_m_ssd_w_out', 'new_m_final_norm_g', 'new_v_c_ctx', 'new_v_mod_w', 'new_v_mod_b', 'new_v_norm_g', 'new_v_ffn_w_in', 'new_v_ffn_w_out', 'new_v_ab_w_in', 'new_v_ab_gate_w', 'new_v_ab_gate_b', 'new_v_ab_gla_norm_g', 'new_v_ab_vnorm_g', 'new_v_ab_spatial_w', 'new_v_ab_spatial_b', 'new_v_ab_w_out', 'new_v_ssd_w_in', 'new_v_ssd_conv_w', 'new_v_ssd_conv_b', 'new_v_ssd_dt_bias', 'new_v_ssd_a_log', 'new_v_ssd_d', 'new_v_ssd_norm_g', 'new_v_ssd_w_out', 'new_v_final_norm_g']
TWIN_LEAF_KINDS = {'loss': 'loss', 'grad_x': 'grad_x', 'grad_c_ctx': 'grad_w', 'grad_mod_w': 'grad_w', 'grad_mod_b': 'grad_w', 'grad_norm_g': 'grad_w', 'grad_ffn_w_in': 'grad_w', 'grad_ffn_w_out': 'grad_w', 'grad_ab_w_in': 'grad_w', 'grad_ab_gate_w': 'grad_w', 'grad_ab_gate_b': 'grad_w', 'grad_ab_gla_norm_g': 'grad_w', 'grad_ab_vnorm_g': 'grad_w', 'grad_ab_spatial_w': 'grad_w', 'grad_ab_spatial_b': 'grad_w', 'grad_ab_w_out': 'grad_w', 'grad_ssd_w_in': 'grad_w', 'grad_ssd_conv_w': 'grad_w', 'grad_ssd_conv_b': 'grad_w', 'grad_ssd_dt_bias': 'grad_w', 'grad_ssd_a_log': 'grad_w', 'grad_ssd_d': 'grad_w', 'grad_ssd_norm_g': 'grad_w', 'grad_ssd_w_out': 'grad_w', 'grad_final_norm_g': 'grad_w', 'delta_c_ctx': 'delta_w', 'delta_mod_w': 'delta_w', 'delta_mod_b': 'delta_w', 'delta_norm_g': 'delta_w', 'delta_ffn_w_in': 'delta_w', 'delta_ffn_w_out': 'delta_w', 'delta_ab_w_in': 'delta_w', 'delta_ab_gate_w': 'delta_w', 'delta_ab_gate_b': 'delta_w', 'delta_ab_gla_norm_g': 'delta_w', 'delta_ab_vnorm_g': 'delta_w', 'delta_ab_spatial_w': 'delta_w', 'delta_ab_spatial_b': 'delta_w', 'delta_ab_w_out': 'delta_w', 'delta_ssd_w_in': 'delta_w', 'delta_ssd_conv_w': 'delta_w', 'delta_ssd_conv_b': 'delta_w', 'delta_ssd_dt_bias': 'delta_w', 'delta_ssd_a_log': 'delta_w', 'delta_ssd_d': 'delta_w', 'delta_ssd_norm_g': 'delta_w', 'delta_ssd_w_out': 'delta_w', 'delta_final_norm_g': 'delta_w', 'new_m_c_ctx': 'new_m', 'new_m_mod_w': 'new_m', 'new_m_mod_b': 'new_m', 'new_m_norm_g': 'new_m', 'new_m_ffn_w_in': 'new_m', 'new_m_ffn_w_out': 'new_m', 'new_m_ab_w_in': 'new_m', 'new_m_ab_gate_w': 'new_m', 'new_m_ab_gate_b': 'new_m', 'new_m_ab_gla_norm_g': 'new_m', 'new_m_ab_vnorm_g': 'new_m', 'new_m_ab_spatial_w': 'new_m', 'new_m_ab_spatial_b': 'new_m', 'new_m_ab_w_out': 'new_m', 'new_m_ssd_w_in': 'new_m', 'new_m_ssd_conv_w': 'new_m', 'new_m_ssd_conv_b': 'new_m', 'new_m_ssd_dt_bias': 'new_m', 'new_m_ssd_a_log': 'new_m', 'new_m_ssd_d': 'new_m', 'new_m_ssd_norm_g': 'new_m', 'new_m_ssd_w_out': 'new_m', 'new_m_final_norm_g': 'new_m', 'new_v_c_ctx': 'new_v', 'new_v_mod_w': 'new_v', 'new_v_mod_b': 'new_v', 'new_v_norm_g': 'new_v', 'new_v_ffn_w_in': 'new_v', 'new_v_ffn_w_out': 'new_v', 'new_v_ab_w_in': 'new_v', 'new_v_ab_gate_w': 'new_v', 'new_v_ab_gate_b': 'new_v', 'new_v_ab_gla_norm_g': 'new_v', 'new_v_ab_vnorm_g': 'new_v', 'new_v_ab_spatial_w': 'new_v', 'new_v_ab_spatial_b': 'new_v', 'new_v_ab_w_out': 'new_v', 'new_v_ssd_w_in': 'new_v', 'new_v_ssd_conv_w': 'new_v', 'new_v_ssd_conv_b': 'new_v', 'new_v_ssd_dt_bias': 'new_v', 'new_v_ssd_a_log': 'new_v', 'new_v_ssd_d': 'new_v', 'new_v_ssd_norm_g': 'new_v', 'new_v_ssd_w_out': 'new_v', 'new_v_final_norm_g': 'new_v'}


def _forward(args):
    return _fwd_reference(*[args[k] for k in FWD_PARAMS])


def _output_shape():
    def fwd():
        inp = _fwd_setup_inputs(0)
        return _fwd_reference(*[inp[k] for k in FWD_PARAMS])
    out = _jax.eval_shape(fwd)
    return out.shape, out.dtype

N_MICROBATCH = 1
ADAM_LR = 0.001
ADAM_B1 = 0.9
ADAM_B2 = 0.999
ADAM_EPS = 1e-08
ADAM_WD = 0.01
ADAM_STEP = 10
PER_EXAMPLE_BATCH_AXIS = {'x': 0, 'c': 0, 'ctx': 0, 'loss_target': 0}
SHARED_INPUTS = []
_WEIGHT_DTYPES = {'c_ctx': _jnp.float32, 'mod_w': _jnp.float32, 'mod_b': _jnp.float32, 'norm_g': _jnp.float32, 'ffn_w_in': _jnp.float32, 'ffn_w_out': _jnp.float32, 'ab_w_in': _jnp.float32, 'ab_gate_w': _jnp.float32, 'ab_gate_b': _jnp.float32, 'ab_gla_norm_g': _jnp.float32, 'ab_vnorm_g': _jnp.float32, 'ab_spatial_w': _jnp.float32, 'ab_spatial_b': _jnp.float32, 'ab_w_out': _jnp.float32, 'ssd_w_in': _jnp.float32, 'ssd_conv_w': _jnp.float32, 'ssd_conv_b': _jnp.float32, 'ssd_dt_bias': _jnp.float32, 'ssd_a_log': _jnp.float32, 'ssd_d': _jnp.float32, 'ssd_norm_g': _jnp.float32, 'ssd_w_out': _jnp.float32, 'final_norm_g': _jnp.float32}
MOMENT_SCALE = {'c_ctx': 1.893672e-02, 'mod_w': 8.183128e-02, 'mod_b': 1.399446e-01, 'norm_g': 1.190666e-01, 'ffn_w_in': 4.976861e-02, 'ffn_w_out': 8.151780e-02, 'ab_w_in': 1.251648e-01, 'ab_gate_w': 3.288187e-02, 'ab_gate_b': 5.503233e-02, 'ab_gla_norm_g': 8.865075e-02, 'ab_vnorm_g': 9.842771e-02, 'ab_spatial_w': 9.559081e-02, 'ab_spatial_b': 9.396291e-02, 'ab_w_out': 1.140410e-01, 'ssd_w_in': 5.145346e-02, 'ssd_conv_w': 4.476134e-02, 'ssd_conv_b': 4.547631e-02, 'ssd_dt_bias': 1.002602e-01, 'ssd_a_log': 1.241876e-01, 'ssd_d': 1.616031e-01, 'ssd_norm_g': 5.402334e-02, 'ssd_w_out': 7.721240e-02, 'final_norm_g': 3.249030e+01}


def _to_microbatches(a, axis):
    t = _jnp.moveaxis(a, axis, 0)
    t = t.reshape((N_MICROBATCH, t.shape[0] // N_MICROBATCH) + t.shape[1:])
    return _jnp.moveaxis(t, 1, axis + 1)


def setup_inputs(seed: int = 0) -> dict:
    inp = _fwd_setup_inputs(seed)
    key = _jax.random.fold_in(_jax.random.key(seed), 7919)
    shape, _ = _output_shape()
    out = dict(inp)
    out["loss_target"] = _jax.random.normal(_jax.random.fold_in(key, 0), shape, _jnp.float32)
    for i, name in enumerate(TWIN_WEIGHTS):
        w = inp[name].astype(_jnp.float32)
        if MOMENT_SCALE is None:
            s = _jnp.sqrt(_jnp.mean(_jnp.square(w)) + 1e-30)
        else:
            s = MOMENT_SCALE[name]
        km, kv = _jax.random.split(_jax.random.fold_in(key, i + 1))
        out[name] = w
        out["m_" + name] = s * _jax.random.normal(km, w.shape, _jnp.float32)
        out["v_" + name] = (s * s) * _jax.random.uniform(kv, w.shape, _jnp.float32, 0.5, 1.5)
    if N_MICROBATCH > 1:
        for name, axis in PER_EXAMPLE_BATCH_AXIS.items():
            out[name] = _to_microbatches(out[name], axis)
    return {'x': out['x'], 'c': out['c'], 'ctx': out['ctx'], 'c_ctx': out['c_ctx'], 'mod_w': out['mod_w'], 'mod_b': out['mod_b'], 'norm_g': out['norm_g'], 'ffn_w_in': out['ffn_w_in'], 'ffn_w_out': out['ffn_w_out'], 'ab_w_in': out['ab_w_in'], 'ab_gate_w': out['ab_gate_w'], 'ab_gate_b': out['ab_gate_b'], 'ab_gla_norm_g': out['ab_gla_norm_g'], 'ab_vnorm_g': out['ab_vnorm_g'], 'ab_spatial_w': out['ab_spatial_w'], 'ab_spatial_b': out['ab_spatial_b'], 'ab_w_out': out['ab_w_out'], 'ssd_w_in': out['ssd_w_in'], 'ssd_conv_w': out['ssd_conv_w'], 'ssd_conv_b': out['ssd_conv_b'], 'ssd_dt_bias': out['ssd_dt_bias'], 'ssd_a_log': out['ssd_a_log'], 'ssd_d': out['ssd_d'], 'ssd_norm_g': out['ssd_norm_g'], 'ssd_w_out': out['ssd_w_out'], 'final_norm_g': out['final_norm_g'], 'loss_target': out['loss_target'], 'm_c_ctx': out['m_c_ctx'], 'm_mod_w': out['m_mod_w'], 'm_mod_b': out['m_mod_b'], 'm_norm_g': out['m_norm_g'], 'm_ffn_w_in': out['m_ffn_w_in'], 'm_ffn_w_out': out['m_ffn_w_out'], 'm_ab_w_in': out['m_ab_w_in'], 'm_ab_gate_w': out['m_ab_gate_w'], 'm_ab_gate_b': out['m_ab_gate_b'], 'm_ab_gla_norm_g': out['m_ab_gla_norm_g'], 'm_ab_vnorm_g': out['m_ab_vnorm_g'], 'm_ab_spatial_w': out['m_ab_spatial_w'], 'm_ab_spatial_b': out['m_ab_spatial_b'], 'm_ab_w_out': out['m_ab_w_out'], 'm_ssd_w_in': out['m_ssd_w_in'], 'm_ssd_conv_w': out['m_ssd_conv_w'], 'm_ssd_conv_b': out['m_ssd_conv_b'], 'm_ssd_dt_bias': out['m_ssd_dt_bias'], 'm_ssd_a_log': out['m_ssd_a_log'], 'm_ssd_d': out['m_ssd_d'], 'm_ssd_norm_g': out['m_ssd_norm_g'], 'm_ssd_w_out': out['m_ssd_w_out'], 'm_final_norm_g': out['m_final_norm_g'], 'v_c_ctx': out['v_c_ctx'], 'v_mod_w': out['v_mod_w'], 'v_mod_b': out['v_mod_b'], 'v_norm_g': out['v_norm_g'], 'v_ffn_w_in': out['v_ffn_w_in'], 'v_ffn_w_out': out['v_ffn_w_out'], 'v_ab_w_in': out['v_ab_w_in'], 'v_ab_gate_w': out['v_ab_gate_w'], 'v_ab_gate_b': out['v_ab_gate_b'], 'v_ab_gla_norm_g': out['v_ab_gla_norm_g'], 'v_ab_vnorm_g': out['v_ab_vnorm_g'], 'v_ab_spatial_w': out['v_ab_spatial_w'], 'v_ab_spatial_b': out['v_ab_spatial_b'], 'v_ab_w_out': out['v_ab_w_out'], 'v_ssd_w_in': out['v_ssd_w_in'], 'v_ssd_conv_w': out['v_ssd_conv_w'], 'v_ssd_conv_b': out['v_ssd_conv_b'], 'v_ssd_dt_bias': out['v_ssd_dt_bias'], 'v_ssd_a_log': out['v_ssd_a_log'], 'v_ssd_d': out['v_ssd_d'], 'v_ssd_norm_g': out['v_ssd_norm_g'], 'v_ssd_w_out': out['v_ssd_w_out'], 'v_final_norm_g': out['v_final_norm_g']}


def _loss(weights, diff, rest, loss_target):
    with _jax.named_scope("forward"):
        args = {**rest, TWIN_DIFF_INPUT: diff, **{k: w.astype(_WEIGHT_DTYPES[k]) for k, w in weights.items()}}
        y = _forward(args)
    with _jax.named_scope("loss_head"):
        err = _jnp.square(y.astype(_jnp.float32) - loss_target)
        return 0.5 * _jnp.sum(_jnp.mean(err, axis=-1)) if err.ndim else 0.5 * err


def _adamw(w, g, m, v):
    m = ADAM_B1 * m + (1.0 - ADAM_B1) * g
    v = ADAM_B2 * v + (1.0 - ADAM_B2) * _jnp.square(g)
    m_hat = m / (1.0 - ADAM_B1 ** ADAM_STEP)
    v_hat = v / (1.0 - ADAM_B2 ** ADAM_STEP)
    delta = -ADAM_LR * (m_hat / (_jnp.sqrt(v_hat) + ADAM_EPS) + ADAM_WD * w)
    return delta, m, v


def reference(x, c, ctx, c_ctx, mod_w, mod_b, norm_g, ffn_w_in, ffn_w_out, ab_w_in, ab_gate_w, ab_gate_b, ab_gla_norm_g, ab_vnorm_g, ab_spatial_w, ab_spatial_b, ab_w_out, ssd_w_in, ssd_conv_w, ssd_conv_b, ssd_dt_bias, ssd_a_log, ssd_d, ssd_norm_g, ssd_w_out, final_norm_g, loss_target, m_c_ctx, m_mod_w, m_mod_b, m_norm_g, m_ffn_w_in, m_ffn_w_out, m_ab_w_in, m_ab_gate_w, m_ab_gate_b, m_ab_gla_norm_g, m_ab_vnorm_g, m_ab_spatial_w, m_ab_spatial_b, m_ab_w_out, m_ssd_w_in, m_ssd_conv_w, m_ssd_conv_b, m_ssd_dt_bias, m_ssd_a_log, m_ssd_d, m_ssd_norm_g, m_ssd_w_out, m_final_norm_g, v_c_ctx, v_mod_w, v_mod_b, v_norm_g, v_ffn_w_in, v_ffn_w_out, v_ab_w_in, v_ab_gate_w, v_ab_gate_b, v_ab_gla_norm_g, v_ab_vnorm_g, v_ab_spatial_w, v_ab_spatial_b, v_ab_w_out, v_ssd_w_in, v_ssd_conv_w, v_ssd_conv_b, v_ssd_dt_bias, v_ssd_a_log, v_ssd_d, v_ssd_norm_g, v_ssd_w_out, v_final_norm_g):
    given = dict(x=x, c=c, ctx=ctx, c_ctx=c_ctx, mod_w=mod_w, mod_b=mod_b, norm_g=norm_g, ffn_w_in=ffn_w_in, ffn_w_out=ffn_w_out, ab_w_in=ab_w_in, ab_gate_w=ab_gate_w, ab_gate_b=ab_gate_b, ab_gla_norm_g=ab_gla_norm_g, ab_vnorm_g=ab_vnorm_g, ab_spatial_w=ab_spatial_w, ab_spatial_b=ab_spatial_b, ab_w_out=ab_w_out, ssd_w_in=ssd_w_in, ssd_conv_w=ssd_conv_w, ssd_conv_b=ssd_conv_b, ssd_dt_bias=ssd_dt_bias, ssd_a_log=ssd_a_log, ssd_d=ssd_d, ssd_norm_g=ssd_norm_g, ssd_w_out=ssd_w_out, final_norm_g=final_norm_g, loss_target=loss_target, m_c_ctx=m_c_ctx, m_mod_w=m_mod_w, m_mod_b=m_mod_b, m_norm_g=m_norm_g, m_ffn_w_in=m_ffn_w_in, m_ffn_w_out=m_ffn_w_out, m_ab_w_in=m_ab_w_in, m_ab_gate_w=m_ab_gate_w, m_ab_gate_b=m_ab_gate_b, m_ab_gla_norm_g=m_ab_gla_norm_g, m_ab_vnorm_g=m_ab_vnorm_g, m_ab_spatial_w=m_ab_spatial_w, m_ab_spatial_b=m_ab_spatial_b, m_ab_w_out=m_ab_w_out, m_ssd_w_in=m_ssd_w_in, m_ssd_conv_w=m_ssd_conv_w, m_ssd_conv_b=m_ssd_conv_b, m_ssd_dt_bias=m_ssd_dt_bias, m_ssd_a_log=m_ssd_a_log, m_ssd_d=m_ssd_d, m_ssd_norm_g=m_ssd_norm_g, m_ssd_w_out=m_ssd_w_out, m_final_norm_g=m_final_norm_g, v_c_ctx=v_c_ctx, v_mod_w=v_mod_w, v_mod_b=v_mod_b, v_norm_g=v_norm_g, v_ffn_w_in=v_ffn_w_in, v_ffn_w_out=v_ffn_w_out, v_ab_w_in=v_ab_w_in, v_ab_gate_w=v_ab_gate_w, v_ab_gate_b=v_ab_gate_b, v_ab_gla_norm_g=v_ab_gla_norm_g, v_ab_vnorm_g=v_ab_vnorm_g, v_ab_spatial_w=v_ab_spatial_w, v_ab_spatial_b=v_ab_spatial_b, v_ab_w_out=v_ab_w_out, v_ssd_w_in=v_ssd_w_in, v_ssd_conv_w=v_ssd_conv_w, v_ssd_conv_b=v_ssd_conv_b, v_ssd_dt_bias=v_ssd_dt_bias, v_ssd_a_log=v_ssd_a_log, v_ssd_d=v_ssd_d, v_ssd_norm_g=v_ssd_norm_g, v_ssd_w_out=v_ssd_w_out, v_final_norm_g=v_final_norm_g)
    weights = {n: given[n] for n in TWIN_WEIGHTS}
    shared = {n: given[n] for n in SHARED_INPUTS}
    per_example = {n: given[n] for n in ['x', 'c', 'ctx']}
    grad_fn = _jax.value_and_grad(_loss, argnums=(0, 1))

    def one_microbatch(ex, loss_target):
        ex = dict(ex)
        diff = ex.pop(TWIN_DIFF_INPUT)
        return grad_fn(weights, diff, {**shared, **ex}, loss_target)

    if N_MICROBATCH == 1:
        loss, (grad_w, grad_x) = one_microbatch(per_example, given["loss_target"])
    else:
        def body(carry, xs):
            loss_sum, grad_sum = carry
            l_k, (gw_k, gx_k) = one_microbatch(xs[0], xs[1])
            with _jax.named_scope("update"):
                return (loss_sum + l_k, _jax.tree.map(_jnp.add, grad_sum, gw_k)), gx_k

        init = (_jnp.zeros((), _jnp.float32), _jax.tree.map(_jnp.zeros_like, weights))
        (loss, grad_w), grad_x = _jax.lax.scan(body, init, (per_example, given["loss_target"]))
    with _jax.named_scope("update"):
        delta_w, new_m, new_v = {}, {}, {}
        for n in TWIN_WEIGHTS:
            delta_w[n], new_m[n], new_v[n] = _adamw(weights[n], grad_w[n], given["m_" + n], given["v_" + n])
    return (loss, grad_x, *[grad_w[n] for n in TWIN_WEIGHTS], *[delta_w[n] for n in TWIN_WEIGHTS],
            *[new_m[n] for n in TWIN_WEIGHTS], *[new_v[n] for n in TWIN_WEIGHTS])
```

```python
import functools
import math

import jax
import jax.numpy as jnp
from jax import lax
from jax.experimental import pallas as pl
from jax.experimental.pallas import tpu as pltpu

F32 = jnp.float32
BF16 = jnp.bfloat16
MXU_DTYPE = jnp.bfloat16
HI = lax.Precision.HIGHEST

D = 1024
NDEV = 8
N_MOD = 6
EPS = 1e-6
GRID_W = 64
CTX = 256
TM = 256
D_FF = 2816
GLA_H, GLA_DK, GLA_DV, GLA_LR, GLA_TAU, GLA_L = 4, 64, 128, 16, 16.0, 64
GMLP_G, GMLP_C, GMLP_L = 4, 128, 128
SSD_H, SSD_P, SSD_G, SSD_N, SSD_L, SSD_K = 32, 64, 4, 128, 128, 5
SSD_INNER = SSD_H * SSD_P
AB_IN = 2592
SSD_IN = 5184
AB_SEGS = ((256, 768), (1056, 1568), (1568, 2080), (2080, 2592), (0, 256), (800, 1056), (768, 800))
AB_P = 2688
SSD_SEGS = ((0, 2048), (3136, 5184), (2048, 2560), (2560, 3072), (3072, 3136))
SSD_P_W = 5376
VMEM_LIMIT = 56 * 1024 * 1024

ADAM_LR, ADAM_B1, ADAM_B2, ADAM_EPS, ADAM_WD, ADAM_STEP = 0.001, 0.9, 0.999, 1e-08, 0.01, 10


def _cp(sem=None, **kw):
    return pltpu.CompilerParams(dimension_semantics=sem, vmem_limit_bytes=VMEM_LIMIT, **kw)


def _dot(a, b, dims=(((1,), (0,)), ((), ()))):
    return lax.dot_general(a.astype(MXU_DTYPE), b.astype(MXU_DTYPE), dims, preferred_element_type=F32)


def _dot_nt(a, b):
    return _dot(a, b, (((1,), (1,)), ((), ())))


def _dot_tn(a, b):
    return _dot(a, b, (((0,), (0,)), ((), ())))


def _dotx(a, b, dims=(((1,), (0,)), ((), ()))):
    return lax.dot_general(a, b, dims, precision=HI, preferred_element_type=F32)


def _rms(x):
    return x * lax.rsqrt(jnp.mean(x * x, axis=-1, keepdims=True) + EPS)


def _pick(n, prefs):
    for p in prefs:
        if n % p == 0:
            return p
    return n


def _row(arr, width=None, colblk=0, tm=TM):
    width = arr.shape[1] if width is None else width
    return (arr, pl.BlockSpec((tm, width), lambda i, c=colblk: (i, c)))


def _full_spec(p):
    nd = p.ndim
    return pl.BlockSpec(p.shape, lambda i, nd=nd: (0,) * nd)


def _rowwise(name, fn, n_blocks, ctx_blk, rows, params, outs, tm=TM):
    nr, npar = len(rows), len(params)

    def body(*refs):
        t = (pl.program_id(0) >= ctx_blk).astype(F32)
        rv = [r[...].astype(F32) for r in refs[:nr]]
        pv = [p[...] for p in refs[nr:nr + npar]]
        res = fn(t, rv, pv)
        for o_ref, o in zip(refs[nr + npar:], res):
            o_ref[...] = o.astype(o_ref.dtype)

    return pl.pallas_call(
        body, name=name, grid=(n_blocks,),
        in_specs=[s for _, s in rows] + [_full_spec(p) for p in params],
        out_specs=[pl.BlockSpec((tm, w), lambda i: (i, 0)) for w, _ in outs],
        out_shape=[jax.ShapeDtypeStruct((n_blocks * tm, w), dt) for w, dt in outs],
        compiler_params=_cp(("parallel",)),
    )(*[a for a, _ in rows], *params)


def _rowwise_vjp(name, fn, n_blocks, ctx_blk, rows, params, douts, row_grads, tm=TM):
    nr, npar, nd = len(rows), len(params), len(douts)
    adds = [a for _, _, a in row_grads if a is not None]
    na = len(adds)

    def body(*refs):
        i = pl.program_id(0)
        t = (i >= ctx_blk).astype(F32)
        rv = [r[...].astype(F32) for r in refs[:nr]]
        pv = [p[...] for p in refs[nr:nr + npar]]
        dv = [r[...].astype(F32) for r in refs[nr + npar:nr + npar + nd]]
        av = [r[...].astype(F32) for r in refs[nr + npar + nd:nr + npar + nd + na]]
        o_refs = refs[nr + npar + nd + na:]
        _, vjp = jax.vjp(lambda r, p: tuple(fn(t, r, p)), rv, pv)
        d_rows, d_params = vjp(tuple(dv))
        ai = 0
        for o_ref, (ri, _, addend) in zip(o_refs, row_grads):
            g = d_rows[ri]
            if addend is not None:
                g = g + av[ai]
                ai += 1
            o_ref[...] = g.astype(o_ref.dtype)
        p_refs = o_refs[len(row_grads):]

        @pl.when(i == 0)
        def _():
            for p_ref in p_refs:
                p_ref[...] = jnp.zeros_like(p_ref)

        for p_ref, g in zip(p_refs, d_params):
            p_ref[...] += g

    widths = [rows[ri][1].block_shape[1] for ri, _, _ in row_grads]
    res = pl.pallas_call(
        body, name=name, grid=(n_blocks,),
        in_specs=[s for _, s in rows] + [_full_spec(p) for p in params] + [s for _, s in douts] + [s for _, s in adds],
        out_specs=[pl.BlockSpec((tm, w), lambda i: (i, 0)) for w in widths] + [_full_spec(p) for p in params],
        out_shape=[jax.ShapeDtypeStruct((n_blocks * tm, w), dt) for w, (_, dt, _) in zip(widths, row_grads)]
        + [jax.ShapeDtypeStruct(p.shape, F32) for p in params],
        compiler_params=_cp(("arbitrary",)),
    )(*[a for a, _ in rows], *params, *[a for a, _ in douts], *[a for a, _ in adds])
    return res[:len(row_grads)], res[len(row_grads):]


def _mm(name, a, b, mode, out_dtype):
    if mode == "nn":
        m, kk = a.shape
        n = b.shape[1]
    elif mode == "nt":
        m, kk = a.shape
        n = b.shape[0]
    else:
        kk, m = a.shape
        n = b.shape[1]
    tm = _pick(m, (1088, 1024, 768, 512, 384, 256, 128))
    tn = _pick(n, (512, 384, 256, 128))
    if mode == "tn":
        tk = _pick(kk, (1088, 768, 512, 256))
    else:
        tk = kk if kk <= 2816 else _pick(kk, (1792, 1408, 1024, 896, 768, 512, 384, 256, 128))
    nk = kk // tk
    if mode == "nn":
        specs = [pl.BlockSpec((tm, tk), lambda i, j, k: (i, k)), pl.BlockSpec((tk, tn), lambda i, j, k: (k, j))]
        dims = (((1,), (0,)), ((), ()))
    elif mode == "nt":
        specs = [pl.BlockSpec((tm, tk), lambda i, j, k: (i, k)), pl.BlockSpec((tn, tk), lambda i, j, k: (j, k))]
        dims = (((1,), (1,)), ((), ()))
    else:
        specs = [pl.BlockSpec((tk, tm), lambda i, j, k: (k, i)), pl.BlockSpec((tk, tn), lambda i, j, k: (k, j))]
        dims = (((0,), (0,)), ((), ()))

    def body(a_ref, b_ref, o_ref, *acc):
        part = lax.dot_general(a_ref[...].astype(MXU_DTYPE), b_ref[...].astype(MXU_DTYPE), dims, preferred_element_type=F32)
        if nk == 1:
            o_ref[...] = part.astype(o_ref.dtype)
        else:
            k = pl.program_id(2)

            @pl.when(k == 0)
            def _():
                acc[0][...] = part

            @pl.when(k > 0)
            def _():
                acc[0][...] += part

            @pl.when(k == nk - 1)
            def _():
                o_ref[...] = acc[0][...].astype(o_ref.dtype)

    return pl.pallas_call(
        body, name=name, grid=(m // tm, n // tn, nk), in_specs=specs,
        out_specs=pl.BlockSpec((tm, tn), lambda i, j, k: (i, j)),
        out_shape=jax.ShapeDtypeStruct((m, n), out_dtype),
        scratch_shapes=[] if nk == 1 else [pltpu.VMEM((tm, tn), F32)],
        compiler_params=_cp(("parallel", "parallel", "arbitrary")),
    )(a, b)


def _sel_mod(modp, t):
    return modp[0:8] * (1.0 - t) + modp[8:16] * t


def _fn_prenorm(t, rows, params, *, a, b):
    (x,), (g, modp) = rows, params
    m = _sel_mod(modp, t)
    return ((_rms(x) * g) * (1.0 + m[b:b + 1]) + m[a:a + 1],)


def _fn_resid_prenorm(t, rows, params, *, gi, a, b):
    (x, y), (g, mod_a, mod_b) = rows, params
    ma, mb = _sel_mod(mod_a, t), _sel_mod(mod_b, t)
    xn = x + ma[gi:gi + 1] * y
    return xn, (_rms(xn) * g) * (1.0 + mb[b:b + 1]) + mb[a:a + 1]


def _fn_resid(t, rows, params, *, gi):
    (x, y), (mod_a,) = rows, params
    return (x + _sel_mod(mod_a, t)[gi:gi + 1] * y,)


def _fn_swiglu(t, rows, params):
    (pf,) = rows
    return (jax.nn.silu(pf[:, :D_FF]) * pf[:, D_FF:],)


def _fn_mixpost(t, rows, params):
    (o_f, o_b, r, u, g), (gla_g, vn_g, sw, sb_t) = rows, params
    o = o_f + o_b
    a = jnp.concatenate([_rms(o[:, h * GLA_DV:(h + 1) * GLA_DV]) for h in range(GLA_H)], axis=1) * gla_g * jax.nn.silu(r)
    uu, vv = jax.nn.gelu(u), jax.nn.gelu(g)
    mu = jnp.mean(vv, axis=-1, keepdims=True)
    var = jnp.mean(jnp.square(vv - mu), axis=-1, keepdims=True)
    vn = ((vv - mu) * lax.rsqrt(var + EPS)) * vn_g
    s = jnp.concatenate(
        [_dot(sw[gi * GMLP_L:(gi + 1) * GMLP_L, :], vn[:, gi * GMLP_C:(gi + 1) * GMLP_C]) + sb_t[:, gi:gi + 1]
         for gi in range(GMLP_G)], axis=1)
    return (jnp.concatenate([a, uu * s], axis=1),)


def _head_expand():
    r = lax.broadcasted_iota(jnp.int32, (SSD_H, SSD_INNER), 0)
    c = lax.broadcasted_iota(jnp.int32, (SSD_H, SSD_INNER), 1)
    return (c // SSD_P == r).astype(F32)


def _fn_ssd_finish(t, rows, params):
    (y_f, y_b, xs, z), (d_skip, norm_g) = rows, params
    d_full = _dotx(jnp.broadcast_to(d_skip, (8, SSD_H)), _head_expand())[0:1]
    y = (y_f + y_b + d_full * xs) * jax.nn.silu(z)
    gw = SSD_INNER // SSD_G
    return (jnp.concatenate([_rms(y[:, gi * gw:(gi + 1) * gw]) for gi in range(SSD_G)], axis=1) * norm_g,)


def _fn_concat(t, rows, params, *, sums, pad=0):
    out, i = [], 0
    for n in sums:
        acc = rows[i]
        for j in range(1, n):
            acc = acc + rows[i + j]
        out.append(acc)
        i += n
    if pad:
        out.append(jnp.zeros((out[0].shape[0], pad), F32))
    return (jnp.concatenate(out, axis=1),)


def _tri(n, rev):
    r = lax.broadcasted_iota(jnp.int32, (n, n), 0)
    c = lax.broadcasted_iota(jnp.int32, (n, n), 1)
    return (r <= c) if rev else (r >= c)


def _gla_chunk(S, v, k, q, tail, gw, gb, *, rev):
    L = GLA_L
    msk = _tri(L, rev)
    tri = msk.astype(F32)
    lr = tail[:, GLA_LR:2 * GLA_LR] if rev else tail[:, 0:GLA_LR]
    la = jax.nn.log_sigmoid(_dot(lr, gw) + gb) / GLA_TAU
    b = _dotx(tri, la)
    b_last = b[0:1] if rev else b[L - 1:L]
    kd = k * jnp.exp(b_last - b)
    qd = (q * GLA_DK ** -0.5) * jnp.exp(b)
    ki = k * jnp.exp(-b)
    dec = jnp.exp(_dotx(la, jnp.ones((L, GLA_DV), F32), (((0,), (0,)), ((), ()))))
    o_parts, s_parts = [], []
    for h in range(GLA_H):
        ks, vs = slice(h * GLA_DK, (h + 1) * GLA_DK), slice(h * GLA_DV, (h + 1) * GLA_DV)
        sh = S[ks, :]
        sc = jnp.where(msk, _dot_nt(qd[:, ks], ki[:, ks]), 0.0)
        o_parts.append(_dot(qd[:, ks], sh) + _dot(sc, v[:, vs]))
        s_parts.append(dec[ks, :] * sh + _dot_tn(kd[:, ks], v[:, vs]))
    return jnp.concatenate(s_parts, axis=0), jnp.concatenate(o_parts, axis=1)


def _ssd_chunk(S, x, bm, cm, tail, dtb, alog, *, rev):
    L = SSD_L
    msk = _tri(L, rev)
    tri = msk.astype(F32)
    E = _head_expand()
    raw = tail[:, SSD_H:2 * SSD_H] if rev else tail[:, 0:SSD_H]
    dt = jax.nn.softplus(raw + dtb)
    dta = dt * (-jnp.exp(alog))
    acum = _dotx(tri, dta)
    acum_t = _dotx(dta, tri, (((0,), (1,)), ((), ())))
    a_last = acum[0:1] if rev else acum[L - 1:L]
    xdt = x * _dotx(dt, E)
    xw = xdt * _dotx(jnp.exp(a_last - acum), E)
    eac = _dotx(jnp.exp(acum), E)
    dec = jnp.exp(_dotx(_dotx(dta, E), jnp.ones((L, SSD_N), F32), (((0,), (0,)), ((), ()))))
    hpg = SSD_H // SSD_G
    gw = hpg * SSD_P
    y_parts, s_parts = [], []
    for g in range(SSD_G):
        hs, ns = slice(g * gw, (g + 1) * gw), slice(g * SSD_N, (g + 1) * SSD_N)
        sg, bg, cg = S[hs, :], bm[:, ns], cm[:, ns]
        s_parts.append(dec[hs, :] * sg + _dot_tn(xw[:, hs], bg))
        y_off = _dot_nt(cg, sg) * eac[:, hs]
        cb = _dot_nt(cg, bg)
        yd = []
        for hh in range(hpg):
            h = g * hpg + hh
            seg = acum[:, h:h + 1] - acum_t[h:h + 1, :]
            m = cb * jnp.exp(jnp.where(msk, seg, -jnp.inf))
            yd.append(_dot(m, xdt[:, h * SSD_P:(h + 1) * SSD_P]))
        y_parts.append(y_off + jnp.concatenate(yd, axis=1))
    return jnp.concatenate(s_parts, axis=0), jnp.concatenate(y_parts, axis=1)


def _scan_order(n, nx, rev, backward):
    nc = n - nx

    def fwd(s):
        return (n - 1 - s) if rev else jnp.where(s < nc, s + nx, s - nc)

    return (lambda s: fwd(n - 1 - s)) if backward else fwd


def _scan_fwd(name, chunk_fn, L, n, nx, rev, rows, params, state_shape, out_w):
    order = _scan_order(n, nx, rev, False)
    nr, npar = len(rows), len(params)

    def body(*refs):
        s_scr = refs[-1]

        @pl.when(pl.program_id(0) == 0)
        def _():
            s_scr[...] = jnp.zeros_like(s_scr)

        s_in = s_scr[...]
        y_ref, st_ref = refs[nr + npar], refs[nr + npar + 1]
        st_ref[0] = s_in
        s_new, y = chunk_fn(s_in, *[r[...] for r in refs[:nr]], *[p[...] for p in refs[nr:nr + npar]], rev=rev)
        y_ref[...] = y
        s_scr[...] = s_new

    return pl.pallas_call(
        body, name=name, grid=(n,),
        in_specs=[pl.BlockSpec((L, w), lambda s, c=c: (order(s), c)) for _, w, c in rows] + [_full_spec(p) for p in params],
        out_specs=[pl.BlockSpec((L, out_w), lambda s: (order(s), 0)),
                   pl.BlockSpec((1,) + state_shape, lambda s: (order(s), 0, 0))],
        out_shape=[jax.ShapeDtypeStruct((n * L, out_w), F32), jax.ShapeDtypeStruct((n,) + state_shape, F32)],
        scratch_shapes=[pltpu.VMEM(state_shape, F32)],
        compiler_params=_cp(("arbitrary",)),
    )(*[a for a, _, _ in rows], *params)


def _scan_bwd(name, chunk_fn, L, n, nx, rev, rows, params, states, dy, state_shape, out_w):
    order = _scan_order(n, nx, rev, True)
    nr, npar = len(rows), len(params)

    def body(*refs):
        i = pl.program_id(0)
        ds_scr = refs[-1]
        rv = [r[...] for r in refs[:nr]]
        pv = [p[...] for p in refs[nr:nr + npar]]
        st_ref, dy_ref = refs[nr + npar], refs[nr + npar + 1]
        o_refs = refs[nr + npar + 2:-1]
        p_refs = o_refs[nr:]

        @pl.when(i == 0)
        def _():
            ds_scr[...] = jnp.zeros_like(ds_scr)
            for p_ref in p_refs:
                p_ref[...] = jnp.zeros_like(p_ref)

        _, vjp = jax.vjp(functools.partial(chunk_fn, rev=rev), st_ref[0], *rv, *pv)
        grads = vjp((ds_scr[...], dy_ref[...].astype(F32)))
        ds_scr[...] = grads[0]
        for o_ref, g in zip(o_refs[:nr], grads[1:1 + nr]):
            o_ref[...] = g
        for p_ref, g in zip(p_refs, grads[1 + nr:]):
            p_ref[...] += g

    res = pl.pallas_call(
        body, name=name, grid=(n,),
        in_specs=[pl.BlockSpec((L, w), lambda s, c=c: (order(s), c)) for _, w, c in rows] + [_full_spec(p) for p in params]
        + [pl.BlockSpec((1,) + state_shape, lambda s: (order(s), 0, 0)), pl.BlockSpec((L, out_w), lambda s: (order(s), 0))],
        out_specs=[pl.BlockSpec((L, w), lambda s: (order(s), 0)) for _, w, _ in rows] + [_full_spec(p) for p in params],
        out_shape=[jax.ShapeDtypeStruct((n * L, w), F32) for _, w, _ in rows] + [jax.ShapeDtypeStruct(p.shape, F32) for p in params],
        scratch_shapes=[pltpu.VMEM(state_shape, F32)],
        compiler_params=_cp(("arbitrary",)),
    )(*[a for a, _, _ in rows], *params, states, dy)
    return res[:nr], res[nr:]


CONV_W = 1024
CONV_COLBLK = (0, 1, 4)


def _conv_specs(nb, src_blk):
    halo = TM // 8
    return [pl.BlockSpec((TM, CONV_W), lambda j, i: (i, src_blk(j))),
            pl.BlockSpec((8, CONV_W), lambda j, i: (jnp.maximum(i * halo - 1, 0), src_blk(j))),
            pl.BlockSpec((8, CONV_W), lambda j, i: (jnp.minimum(i * halo + halo, nb * halo - 1), src_blk(j)))]


def _conv_ext(i, nb, cur, prev, nxt):
    has_prev = jnp.logical_and(i > 0, i < nb - 1)
    has_next = i < nb - 2
    return jnp.concatenate([jnp.where(has_prev, prev, 0.0), cur, jnp.where(has_next, nxt, 0.0)], axis=0)


def _conv_taps(ext, w, flip):
    acc = None
    for j in range(SSD_K):
        wj = w[SSD_K - 1 - j:SSD_K - j, :] if flip else w[j:j + 1, :]
        term = wj * ext[6 + j:6 + j + TM, :]
        acc = term if acc is None else acc + term
    return acc


def _conv(name, src, w8, b1, nb, *, permuted_src, act, flip, out_dtype):
    src_blk = (lambda j: jnp.where(j == 2, CONV_COLBLK[2], j)) if permuted_src else (lambda j: j)

    def body(cur, prev, nxt, w_ref, b_ref, o_ref):
        ext = _conv_ext(pl.program_id(1), nb, cur[...].astype(F32), prev[...].astype(F32), nxt[...].astype(F32))
        acc = _conv_taps(ext, w_ref[...], flip)
        if act:
            acc = jax.nn.silu(acc + b_ref[...])
        o_ref[...] = acc.astype(o_ref.dtype)

    return pl.pallas_call(
        body, name=name, grid=(3, nb),
        in_specs=_conv_specs(nb, src_blk) + [pl.BlockSpec((8, CONV_W), lambda j, i: (0, j)), pl.BlockSpec((1, CONV_W), lambda j, i: (0, j))],
        out_specs=pl.BlockSpec((TM, CONV_W), lambda j, i: (i, j)),
        out_shape=jax.ShapeDtypeStruct((nb * TM, 3 * CONV_W), out_dtype),
        compiler_params=_cp(("parallel", "parallel")),
    )(src, src, src, w8, b1)


def _conv_bwd_pre(name, p1, w8, b1, dxbc_parts, nb):
    src_blk = lambda j: jnp.where(j == 2, CONV_COLBLK[2], j)
    xs_parts, bc_parts = dxbc_parts
    n_x, n_bc = len(xs_parts), len(bc_parts)

    def body(*refs):
        cur, prev, nxt, w_ref, b_ref = refs[:5]
        d_refs = refs[5:5 + n_x + n_bc]
        da_ref, dw_ref, db_ref = refs[5 + n_x + n_bc:]
        j, i = pl.program_id(0), pl.program_id(1)
        ext = _conv_ext(i, nb, cur[...], prev[...], nxt[...])
        acc = _conv_taps(ext, w_ref[...], False) + b_ref[...]
        dx = d_refs[0][...]
        for r in d_refs[1:n_x]:
            dx = dx + r[...]
        dbc = jnp.concatenate([d_refs[n_x][...] + d_refs[n_x + 1][...], d_refs[n_x + 2][...] + d_refs[n_x + 3][...]], axis=1)
        dy = jnp.where(j == 2, dbc, dx)
        sg = jax.nn.sigmoid(acc)
        da = dy * (sg + acc * sg * (1.0 - sg))
        da_ref[...] = da

        @pl.when(i == 0)
        def _():
            dw_ref[...] = jnp.zeros_like(dw_ref)
            db_ref[...] = jnp.zeros_like(db_ref)

        rows = [jnp.sum(da * ext[6 + t:6 + t + TM, :], axis=0, keepdims=True) for t in range(SSD_K)]
        dw_ref[...] += jnp.concatenate(rows + [jnp.zeros((8 - SSD_K, CONV_W), F32)], axis=0)
        db_ref[...] += jnp.sum(da, axis=0, keepdims=True)

    x_specs = [pl.BlockSpec((TM, CONV_W), lambda j, i: (i, jnp.minimum(j, 1))) for _ in xs_parts]
    bc_specs = [pl.BlockSpec((TM, 512), lambda j, i: (i, 0)) for _ in bc_parts]
    return pl.pallas_call(
        body, name=name, grid=(3, nb),
        in_specs=_conv_specs(nb, src_blk) + [pl.BlockSpec((8, CONV_W), lambda j, i: (0, j)), pl.BlockSpec((1, CONV_W), lambda j, i: (0, j))]
        + x_specs + bc_specs,
        out_specs=[pl.BlockSpec((TM, CONV_W), lambda j, i: (i, j)), pl.BlockSpec((8, CONV_W), lambda j, i: (0, j)),
                   pl.BlockSpec((1, CONV_W), lambda j, i: (0, j))],
        out_shape=[jax.ShapeDtypeStruct((nb * TM, 3 * CONV_W), F32), jax.ShapeDtypeStruct((8, 3 * CONV_W), F32),
                   jax.ShapeDtypeStruct((1, 3 * CONV_W), F32)],
        compiler_params=_cp(("arbitrary", "arbitrary")),
    )(p1, p1, p1, w8, b1, *xs_parts, *bc_parts)


def _grid_rows(val, a, kb):
    return jnp.concatenate([val[:, t * D:(t + 1) * D] for t in range(kb)], axis=0)


def _perm(name, xc, a, nb):
    n = xc.shape[0]
    b = (n - CTX) // a
    kb = TM // a
    view = xc.reshape(n // b, b * D)

    def body(v_ref, c_ref, o_ref):
        i = pl.program_id(0)

        @pl.when(i < nb - 1)
        def _():
            o_ref[...] = _grid_rows(v_ref[...], a, kb)

        @pl.when(i == nb - 1)
        def _():
            o_ref[...] = c_ref[...]

    return pl.pallas_call(
        body, name=name, grid=(nb,),
        in_specs=[pl.BlockSpec((a, kb * D), lambda i: (0, jnp.minimum(i, nb - 2))), pl.BlockSpec((TM, D), lambda i: (nb - 1, 0))],
        out_specs=pl.BlockSpec((TM, D), lambda i: (i, 0)),
        out_shape=jax.ShapeDtypeStruct((n, D), xc.dtype),
        compiler_params=_cp(("parallel",)),
    )(view, xc)


def _loss_head(x, f, target, modp, g_final, nb, rows_r):
    t_tok = target.shape[0]
    kb = TM // rows_r
    tview = target.reshape(rows_r, (t_tok // rows_r) * D)

    def fn(x_, f_, tgt, modp_, g_, is_ctx):
        xn = x_ + _sel_mod(modp_, is_ctx)[5:6] * f_
        err = _rms(xn) * g_ - tgt
        return 0.5 * jnp.sum(jnp.mean(err * err, axis=-1)) * (1.0 - is_ctx)

    def body(x_ref, f_ref, t_ref, m_ref, g_ref, l_ref, dx_ref, df_ref, dm_ref, dg_ref):
        i = pl.program_id(0)
        is_ctx = (i == nb - 1).astype(F32)
        tgt = _grid_rows(t_ref[...], rows_r, kb)
        l, vjp = jax.vjp(lambda a_, b_, c_, d_: fn(a_, b_, tgt, c_, d_, is_ctx), x_ref[...], f_ref[...], m_ref[...], g_ref[...])
        dx, df, dm, dg = vjp(jnp.ones((), F32))

        @pl.when(i == 0)
        def _():
            l_ref[...] = jnp.zeros_like(l_ref)
            dm_ref[...] = jnp.zeros_like(dm_ref)
            dg_ref[...] = jnp.zeros_like(dg_ref)

        l_ref[...] += jnp.reshape(l, (1, 1))
        dx_ref[...] = dx
        df_ref[...] = df.astype(df_ref.dtype)
        dm_ref[...] += dm
        dg_ref[...] += dg

    rowspec = pl.BlockSpec((TM, D), lambda i: (i, 0))
    return pl.pallas_call(
        body, name="loss_head", grid=(nb,),
        in_specs=[rowspec, rowspec, pl.BlockSpec((rows_r, kb * D), lambda i: (0, jnp.minimum(i, nb - 2))),
                  _full_spec(modp), _full_spec(g_final)],
        out_specs=[pl.BlockSpec((1, 1), lambda i: (0, 0)), rowspec, rowspec, _full_spec(modp), _full_spec(g_final)],
        out_shape=[jax.ShapeDtypeStruct((1, 1), F32), jax.ShapeDtypeStruct(x.shape, F32), jax.ShapeDtypeStruct(x.shape, MXU_DTYPE),
                   jax.ShapeDtypeStruct(modp.shape, F32), jax.ShapeDtypeStruct(g_final.shape, F32)],
        compiler_params=_cp(("arbitrary",)),
    )(x, f, tview, modp, g_final)


def _repack(name, shards, segs, wp):
    nd, kk, ws = shards.shape
    tr = 128
    used = sum(e - s for s, e in segs)

    def body(a_ref, o_ref):
        full = jnp.concatenate([a_ref[d].astype(F32) for d in range(nd)], axis=1)
        parts = [full[:, s:e] for s, e in segs]
        if wp > used:
            parts.append(jnp.zeros((tr, wp - used), F32))
        o_ref[...] = jnp.concatenate(parts, axis=1).astype(o_ref.dtype)

    return pl.pallas_call(
        body, name=name, grid=(kk // tr,),
        in_specs=[pl.BlockSpec((nd, tr, ws), lambda i: (0, i, 0))],
        out_specs=pl.BlockSpec((tr, wp), lambda i: (i, 0)),
        out_shape=jax.ShapeDtypeStruct((kk, wp), MXU_DTYPE),
        compiler_params=_cp(("parallel",)),
    )(shards)


def _unpack(name, dw, segs, ws, out_dtype):
    kk, wp = dw.shape
    tr = 128
    order = sorted(range(len(segs)), key=lambda i: segs[i][0])
    offs, o = [], 0
    for s, e in segs:
        offs.append(o)
        o += e - s

    def body(a_ref, o_ref):
        a = a_ref[...].astype(F32)
        full = jnp.concatenate([a[:, offs[i]:offs[i] + segs[i][1] - segs[i][0]] for i in order], axis=1)
        for d in range(NDEV):
            o_ref[d] = full[:, d * ws:(d + 1) * ws].astype(o_ref.dtype)

    return pl.pallas_call(
        body, name=name, grid=(kk // tr,),
        in_specs=[pl.BlockSpec((tr, wp), lambda i: (i, 0))],
        out_specs=pl.BlockSpec((NDEV, tr, ws), lambda i: (0, i, 0)),
        out_shape=jax.ShapeDtypeStruct((NDEV, kk, ws), out_dtype),
        compiler_params=_cp(("parallel",)),
    )(dw)


def _adam_math(w, g, m, v):
    m = ADAM_B1 * m + (1.0 - ADAM_B1) * g
    v = ADAM_B2 * v + (1.0 - ADAM_B2) * jnp.square(g)
    m_hat = m / (1.0 - ADAM_B1 ** ADAM_STEP)
    v_hat = v / (1.0 - ADAM_B2 ** ADAM_STEP)
    delta = -ADAM_LR * (m_hat / (jnp.sqrt(v_hat) + ADAM_EPS) + ADAM_WD * w)
    return delta, m, v


def _adam(name, w, parts, m, v):
    r, c = w.shape
    npart = parts.shape[0]
    tr = _pick(r, (256, 128, 64, 32, 16, 8)) if r * c * 4 > (1 << 20) else r

    def body(w_ref, p_ref, m_ref, v_ref, g_ref, d_ref, nm_ref, nv_ref):
        g = p_ref[0].astype(F32)
        for s in range(1, npart):
            g = g + p_ref[s].astype(F32)
        delta, nm, nv = _adam_math(w_ref[...], g, m_ref[...], v_ref[...])
        g_ref[...], d_ref[...], nm_ref[...], nv_ref[...] = g, delta, nm, nv

    spec = pl.BlockSpec((tr, c), lambda i: (i, 0))
    return pl.pallas_call(
        body, name=name, grid=(r // tr,),
        in_specs=[spec, pl.BlockSpec((npart, tr, c), lambda i: (0, i, 0)), spec, spec],
        out_specs=[spec] * 4, out_shape=[jax.ShapeDtypeStruct((r, c), F32)] * 4,
        compiler_params=_cp(("parallel",)),
    )(w, parts, m, v)


def _mod_fwd(c_all, mod_w):
    nl, _, ws = mod_w.shape

    def body(c_ref, w_ref, o_ref):
        o_ref[0] = _dot(jax.nn.silu(c_ref[...]), w_ref[0])

    return pl.pallas_call(
        body, name="mod_fwd", grid=(nl,),
        in_specs=[_full_spec(c_all), pl.BlockSpec((1, D, ws), lambda i: (i, 0, 0))],
        out_specs=pl.BlockSpec((1, 16, ws), lambda i: (i, 0, 0)),
        out_shape=jax.ShapeDtypeStruct((nl, 16, ws), F32),
        compiler_params=_cp(("parallel",)),
    )(c_all, mod_w)


def _mod_bwd(c_all, mod_w, dm):
    nl, _, ws = mod_w.shape

    def body(c_ref, w_ref, d_ref, dw_ref, dc_ref):
        dw_ref[0] = _dot_tn(jax.nn.silu(c_ref[...]), d_ref[0])
        dc_ref[0] = _dot_nt(d_ref[0], w_ref[0])

    return pl.pallas_call(
        body, name="mod_bwd", grid=(nl,),
        in_specs=[_full_spec(c_all), pl.BlockSpec((1, D, ws), lambda i: (i, 0, 0)), pl.BlockSpec((1, 16, ws), lambda i: (i, 0, 0))],
        out_specs=[pl.BlockSpec((1, D, ws), lambda i: (i, 0, 0)), pl.BlockSpec((1, 16, D), lambda i: (i, 0, 0))],
        out_shape=[jax.ShapeDtypeStruct((nl, D, ws), F32), jax.ShapeDtypeStruct((nl, 16, D), F32)],
        compiler_params=_cp(("parallel",)),
    )(c_all, mod_w, dm)


def _sum_parts(name, parts):
    npart, r, c = parts.shape

    def body(p_ref, o_ref):
        g = p_ref[0].astype(F32)
        for s in range(1, npart):
            g = g + p_ref[s].astype(F32)
        o_ref[...] = g

    return pl.pallas_call(body, name=name, out_shape=jax.ShapeDtypeStruct((r, c), F32), compiler_params=_cp())(parts)


MESH = pl.DeviceIdType.MESH
ANY = pl.BlockSpec(memory_space=pl.ANY)
N_PEERS = NDEV - 1


def _mesh_pos():
    return lax.axis_index("x"), lax.axis_index("y"), lax.axis_index("c")


def _slot(px, py, pc):
    return 4 * px + 2 * py + pc


def _two_level_gather(x_refs, o_refs, send_sems, recv_sems, local_sems):
    x, y, c = _mesh_pos()
    me, sibling = (x, y, c), (x, y, 1 - c)
    chips = [(1 - x, y), (x, 1 - y), (1 - x, 1 - y)]
    n = len(x_refs)

    def copy(a, k, block, to, src=None):
        dst = o_refs[a].at[_slot(*block)]
        return pltpu.make_async_remote_copy(src_ref=dst if src is None else src, dst_ref=dst, send_sem=send_sems.at[a, k],
                                            recv_sem=recv_sems.at[a, k], device_id=to, device_id_type=MESH)

    mine = [pltpu.make_async_copy(x_refs[a], o_refs[a].at[_slot(*me)], local_sems.at[a]) for a in range(n)]
    for cp in mine:
        cp.start()
    first = []
    for a in range(n):
        first.append(copy(a, 0, me, sibling, src=x_refs[a]))
        first += [copy(a, 1 + j, me, (*chip, c), src=x_refs[a]) for j, chip in enumerate(chips)]
    for cp in first:
        cp.start()
    passed = []
    for j, chip in enumerate(chips):
        for a in range(n):
            copy(a, 1 + j, (*chip, c), me).wait_recv()
            fwd = copy(a, 4 + j, (*chip, c), sibling)
            fwd.start()
            passed.append(fwd)
    for a in range(n):
        copy(a, 0, sibling, me).wait_recv()
        for j, chip in enumerate(chips):
            copy(a, 4 + j, (*chip, 1 - c), me).wait_recv()
    for cp in first + passed:
        cp.wait_send()
    for cp in mine:
        cp.wait()


def _ag_small(name, x):
    r, c = x.shape

    def body(x_ref, o_ref, send_sems, recv_sems, local_sems):
        _two_level_gather([x_ref], [o_ref], send_sems, recv_sems, local_sems)

    return pl.pallas_call(
        body, name=name, out_shape=jax.ShapeDtypeStruct((NDEV, r, c), x.dtype),
        in_specs=[pl.BlockSpec(memory_space=pltpu.VMEM)], out_specs=pl.BlockSpec(memory_space=pltpu.VMEM),
        scratch_shapes=[pltpu.SemaphoreType.DMA((1, N_PEERS)), pltpu.SemaphoreType.DMA((1, N_PEERS)), pltpu.SemaphoreType.DMA((1,))],
        compiler_params=pltpu.CompilerParams(vmem_limit_bytes=VMEM_LIMIT),
    )(x)


def _ag_big(name, shards):
    n = len(shards)

    def body(*refs):
        _two_level_gather(refs[:n], refs[n:2 * n], *refs[2 * n:])

    return pl.pallas_call(
        body, name=name, out_shape=[jax.ShapeDtypeStruct((NDEV,) + s.shape, s.dtype) for s in shards],
        in_specs=[ANY] * n, out_specs=[ANY] * n,
        scratch_shapes=[pltpu.SemaphoreType.DMA((n, N_PEERS)), pltpu.SemaphoreType.DMA((n, N_PEERS)), pltpu.SemaphoreType.DMA((n,))],
    )(*shards)


def _rs_exchange(name, parts):
    n = len(parts)

    def body(*refs):
        p_refs, r_refs = refs[:n], refs[n:2 * n]
        send_sems, recv_sems, local_sems = refs[2 * n:]
        x, y, c = _mesh_pos()
        me = _slot(x, y, c)
        local = [pltpu.make_async_copy(p_refs[a].at[me], r_refs[a].at[me], local_sems.at[a]) for a in range(n)]
        for cp in local:
            cp.start()
        sends, recvs = [], []
        for k in range(1, NDEV):
            peer = ((1 - x) if k & 4 else x, (1 - y) if k & 2 else y, (1 - c) if k & 1 else c)
            ps = _slot(*peer)
            for a in range(n):
                sends.append(pltpu.make_async_remote_copy(
                    src_ref=p_refs[a].at[ps], dst_ref=r_refs[a].at[me], send_sem=send_sems.at[a, k - 1],
                    recv_sem=recv_sems.at[a, k - 1], device_id=peer, device_id_type=MESH))
                recvs.append(pltpu.make_async_remote_copy(
                    src_ref=p_refs[a].at[ps], dst_ref=r_refs[a].at[ps], send_sem=send_sems.at[a, k - 1],
                    recv_sem=recv_sems.at[a, k - 1], device_id=peer, device_id_type=MESH))
        for cp in sends:
            cp.start()
        for cp in recvs:
            cp.wait_recv()
        for cp in sends:
            cp.wait_send()
        for cp in local:
            cp.wait()

    return pl.pallas_call(
        body, name=name, out_shape=[jax.ShapeDtypeStruct(p.shape, p.dtype) for p in parts],
        in_specs=[ANY] * n, out_specs=[ANY] * n,
        scratch_shapes=[pltpu.SemaphoreType.DMA((n, N_PEERS)), pltpu.SemaphoreType.DMA((n, N_PEERS)), pltpu.SemaphoreType.DMA((n,))],
    )(*parts)


def _ffn_fwd(tag, h, w_in, w_out, nb, cb):
    pf = _mm(tag + "_ffn_in", h, w_in, "nn", F32)
    (act,) = _rowwise(tag + "_swiglu", _fn_swiglu, nb, cb, [_row(pf)], [], [(D_FF, MXU_DTYPE)])
    return pf, act, _mm(tag + "_ffn_out", act, w_out, "nn", F32)


def _ffn_bwd(tag, h, pf, act, df, w_in, w_out, nb, cb):
    dw_out = _mm(tag + "_ffn_out_dw", act, df, "tn", MXU_DTYPE)
    dact = _mm(tag + "_ffn_out_dx", df, w_out, "nt", MXU_DTYPE)
    (dpf,), _ = _rowwise_vjp(tag + "_swiglu_bwd", _fn_swiglu, nb, cb, [_row(pf)], [], [_row(dact)], [(0, MXU_DTYPE, None)])
    dw_in = _mm(tag + "_ffn_in_dw", h, dpf, "tn", MXU_DTYPE)
    dh = _mm(tag + "_ffn_in_dx", dpf, w_in, "nt", MXU_DTYPE)
    return dw_out, dw_in, dh


def _local_step(x, ctx, target, mod, W, P):
    T = x.shape[0]
    N = T + CTX
    nb, cb = N // TM, N // TM - 1
    R = T // GRID_W
    mod0, mod1 = mod[0], mod[1]
    ng = P["norm_g"]
    g00, g01, g10, g11 = ng[0, 0][None], ng[0, 1][None], ng[1, 0][None], ng[1, 1][None]
    pre = functools.partial(_fn_prenorm, a=0, b=1)
    rpre = functools.partial(_fn_resid_prenorm, gi=2, a=3, b=4)
    res5 = functools.partial(_fn_resid, gi=5)
    dirs = (("f", False), ("b", True))

    xc0 = jnp.concatenate([x, ctx], axis=0)
    (h0,) = _rowwise("l0_prenorm", pre, nb, cb, [_row(xc0)], [g00, mod0], [(D, MXU_DTYPE)])
    p0 = _mm("l0_in", h0, W["ab_in"], "nn", F32)
    gla_rows = [(p0, 512, 0), (p0, 256, 8), (p0, 256, 9), (p0, 128, 20)]
    n64, nx64 = N // GLA_L, T // GLA_L
    gla_par = {d: [P["ab_gate_w"][int(r)], P["ab_gate_b"][int(r)][None]] for d, r in dirs}
    gla_state = (GLA_H * GLA_DK, GLA_DV)
    o, st0 = {}, {}
    for d, rev in dirs:
        o[d], st0[d] = _scan_fwd("gla_fwd_" + d, _gla_chunk, GLA_L, n64, nx64, rev, gla_rows, gla_par[d], gla_state, GLA_H * GLA_DV)
    n128, cb128 = N // GMLP_L, T // GMLP_L
    mix_rows = [_row(o["f"], tm=GMLP_L), _row(o["b"], tm=GMLP_L)] + [_row(p0, 512, j, tm=GMLP_L) for j in (1, 2, 3)]
    mix_par = [P["ab_gla_norm_g"], P["ab_vnorm_g"], P["ab_spatial_w"].reshape(GMLP_G * GMLP_L, GMLP_L), P["ab_spatial_b"].T]
    (cat0,) = _rowwise("l0_mix", _fn_mixpost, n128, cb128, mix_rows, mix_par, [(D, MXU_DTYPE)], tm=GMLP_L)
    y0 = _mm("l0_out", cat0, W["ab_out"], "nn", F32)
    x1, h1 = _rowwise("l0_ffn_prenorm", rpre, nb, cb, [_row(xc0), _row(y0)], [g01, mod0, mod0], [(D, F32), (D, MXU_DTYPE)])
    pf0, act0, f0 = _ffn_fwd("l0", h1, W["ffn_in"][0], W["ffn_out"][0], nb, cb)
    (x2,) = _rowwise("l0_resid", res5, nb, cb, [_row(x1), _row(f0)], [mod0], [(D, F32)])
    x2p = _perm("to_col_major", x2, R, nb)

    (h2,) = _rowwise("l1_prenorm", pre, nb, cb, [_row(x2p)], [g10, mod1], [(D, MXU_DTYPE)])
    p1 = _mm("l1_in", h2, W["ssd_in"], "nn", F32)
    conv_w8 = jnp.concatenate([P["ssd_conv_w"], jnp.zeros((8 - SSD_K, 3 * CONV_W), F32)], axis=0)
    xbc = _conv("l1_conv", p1, conv_w8, P["ssd_conv_b"], nb, permuted_src=True, act=True, flip=False, out_dtype=F32)
    ssd_rows = [(xbc, SSD_INNER, 0), (xbc, 512, 4), (xbc, 512, 5), (p1, 128, 40)]
    ns, nxs = N // SSD_L, T // SSD_L
    ssd_par = {d: [P["ssd_dt_bias"][int(r)][None], P["ssd_a_log"][int(r)][None]] for d, r in dirs}
    ssd_state = (SSD_INNER, SSD_N)
    ys, st1 = {}, {}
    for d, rev in dirs:
        ys[d], st1[d] = _scan_fwd("ssd_fwd_" + d, _ssd_chunk, SSD_L, ns, nxs, rev, ssd_rows, ssd_par[d], ssd_state, SSD_INNER)
    fin_rows = [_row(ys["f"]), _row(ys["b"]), _row(xbc, SSD_INNER, 0), _row(p1, SSD_INNER, 1)]
    fin_par = [P["ssd_d"], P["ssd_norm_g"]]
    (yn,) = _rowwise("l1_finish", _fn_ssd_finish, nb, cb, fin_rows, fin_par, [(SSD_INNER, MXU_DTYPE)])
    y1 = _mm("l1_out", yn, W["ssd_out"], "nn", F32)
    x3, h3 = _rowwise("l1_ffn_prenorm", rpre, nb, cb, [_row(x2p), _row(y1)], [g11, mod1, mod1], [(D, F32), (D, MXU_DTYPE)])
    pf1, act1, f1 = _ffn_fwd("l1", h3, W["ffn_in"][1], W["ffn_out"][1], nb, cb)
    loss, dx3, df1, dm1_j, d_final_g = _loss_head(x3, f1, target, mod1, P["final_norm_g"], nb, R)

    dW, dP = {}, {"final_norm_g": d_final_g}
    dwo1, dwi1, dh3 = _ffn_bwd("l1", h3, pf1, act1, df1, W["ffn_in"][1], W["ffn_out"][1], nb, cb)
    (dx2p_a, dy1), (dg11, dm1_a, dm1_b) = _rowwise_vjp(
        "l1_ffn_prenorm_bwd", rpre, nb, cb, [_row(x2p), _row(y1)], [g11, mod1, mod1], [_row(dx3), _row(dh3)],
        [(0, F32, None), (1, MXU_DTYPE, None)])
    dW["ssd_out"] = _mm("l1_out_dw", yn, dy1, "tn", MXU_DTYPE)
    dyn = _mm("l1_out_dx", dy1, W["ssd_out"], "nt", MXU_DTYPE)
    (dys, dxs, dz), (dP["ssd_d"], dP["ssd_norm_g"]) = _rowwise_vjp(
        "l1_finish_bwd", _fn_ssd_finish, nb, cb, fin_rows, fin_par, [_row(dyn)],
        [(0, F32, None), (2, F32, None), (3, MXU_DTYPE, None)])
    dxp, dbp, dcp, dtl, ddtb, dalog = [], [], [], [], [], []
    for d, rev in dirs:
        (dx_, db_, dc_, dt_), (ddtb_, dalog_) = _scan_bwd("ssd_bwd_" + d, _ssd_chunk, SSD_L, ns, nxs, rev, ssd_rows, ssd_par[d],
                                                          st1[d], dys, ssd_state, SSD_INNER)
        dxp.append(dx_); dbp.append(db_); dcp.append(dc_); dtl.append(dt_); ddtb.append(ddtb_); dalog.append(dalog_)
    dP["ssd_dt_bias"] = jnp.concatenate(ddtb, axis=0)
    dP["ssd_a_log"] = jnp.concatenate(dalog, axis=0)
    dacc, dcw8, dP["ssd_conv_b"] = _conv_bwd_pre("l1_conv_bwd", p1, conv_w8, P["ssd_conv_b"],
                                                  (dxp + [dxs], [dbp[0], dbp[1], dcp[0], dcp[1]]), nb)
    dP["ssd_conv_w"] = dcw8[:SSD_K]
    dpc = _conv("l1_conv_dx", dacc, conv_w8, jnp.zeros((1, 3 * CONV_W), F32), nb, permuted_src=False, act=False, flip=True,
                out_dtype=MXU_DTYPE)
    cat1 = functools.partial(_fn_concat, sums=(1, 1, 1, 2), pad=SSD_P_W - 5248)
    (dp1,) = _rowwise("l1_dp", cat1, nb, cb, [_row(dpc, SSD_INNER, 0), _row(dz), _row(dpc, 1024, 2), _row(dtl[0]), _row(dtl[1])],
                      [], [(SSD_P_W, MXU_DTYPE)])
    dW["ssd_in"] = _mm("l1_in_dw", h2, dp1, "tn", F32)
    dh2 = _mm("l1_in_dx", dp1, W["ssd_in"], "nt", MXU_DTYPE)
    (dx2p,), (dg10, dm1_f) = _rowwise_vjp("l1_prenorm_bwd", pre, nb, cb, [_row(x2p)], [g10, mod1], [_row(dh2)],
                                          [(0, F32, _row(dx2p_a))])
    dx2 = _perm("to_row_major", dx2p, GRID_W, nb)

    (dx1_a, df0), (dm0_e,) = _rowwise_vjp("l0_resid_bwd", res5, nb, cb, [_row(x1), _row(f0)], [mod0], [_row(dx2)],
                                          [(0, F32, None), (1, MXU_DTYPE, None)])
    dwo0, dwi0, dh1 = _ffn_bwd("l0", h1, pf0, act0, df0, W["ffn_in"][0], W["ffn_out"][0], nb, cb)
    (dxc0_a, dy0), (dg01, dm0_a, dm0_b) = _rowwise_vjp(
        "l0_ffn_prenorm_bwd", rpre, nb, cb, [_row(xc0), _row(y0)], [g01, mod0, mod0], [_row(dx1_a), _row(dh1)],
        [(0, F32, None), (1, MXU_DTYPE, None)])
    dW["ab_out"] = _mm("l0_out_dw", cat0, dy0, "tn", MXU_DTYPE)
    dcat0 = _mm("l0_out_dx", dy0, W["ab_out"], "nt", MXU_DTYPE)
    (do, dr, du, dgm), (dP["ab_gla_norm_g"], dP["ab_vnorm_g"], dsw, dsb_t) = _rowwise_vjp(
        "l0_mix_bwd", _fn_mixpost, n128, cb128, mix_rows, mix_par, [_row(dcat0, tm=GMLP_L)],
        [(0, F32, None), (2, MXU_DTYPE, None), (3, MXU_DTYPE, None), (4, MXU_DTYPE, None)], tm=GMLP_L)
    dP["ab_spatial_w"] = dsw.reshape(GMLP_G, GMLP_L, GMLP_L)
    dP["ab_spatial_b"] = dsb_t.T
    gl, dgw, dgb = [], [], []
    for d, rev in dirs:
        g4, (dgw_, dgb_) = _scan_bwd("gla_bwd_" + d, _gla_chunk, GLA_L, n64, nx64, rev, gla_rows, gla_par[d], st0[d], do,
                                     gla_state, GLA_H * GLA_DV)
        gl.append(g4); dgw.append(dgw_[None]); dgb.append(dgb_)
    dP["ab_gate_w"] = jnp.concatenate(dgw, axis=0)
    dP["ab_gate_b"] = jnp.concatenate(dgb, axis=0)
    cat0f = functools.partial(_fn_concat, sums=(2, 1, 1, 1, 2, 2, 2))
    (dp0,) = _rowwise("l0_dp", cat0f, nb, cb,
                      [_row(gl[0][0]), _row(gl[1][0]), _row(dr), _row(du), _row(dgm), _row(gl[0][1]), _row(gl[1][1]),
                       _row(gl[0][2]), _row(gl[1][2]), _row(gl[0][3]), _row(gl[1][3])], [], [(AB_P, MXU_DTYPE)])
    dW["ab_in"] = _mm("l0_in_dw", h0, dp0, "tn", F32)
    dh0 = _mm("l0_in_dx", dp0, W["ab_in"], "nt", MXU_DTYPE)
    (dxc0,), (dg00, dm0_s) = _rowwise_vjp("l0_prenorm_bwd", pre, nb, cb, [_row(xc0)], [g00, mod0], [_row(dh0)],
                                          [(0, F32, _row(dxc0_a))])
    dW["ffn_in"] = (dwi0, dwi1)
    dW["ffn_out"] = (dwo0, dwo1)
    dP["norm_g"] = jnp.concatenate([dg00, dg01, dg10, dg11], axis=0).reshape(2, 2, D)
    dmod = jnp.stack([dm0_s + dm0_a + dm0_b + dm0_e, dm1_f + dm1_a + dm1_b + dm1_j])
    return loss, dxc0[:T], dmod, dW, dP


WEIGHTS = ("c_ctx", "mod_w", "mod_b", "norm_g", "ffn_w_in", "ffn_w_out", "ab_w_in", "ab_gate_w", "ab_gate_b", "ab_gla_norm_g",
           "ab_vnorm_g", "ab_spatial_w", "ab_spatial_b", "ab_w_out", "ssd_w_in", "ssd_conv_w", "ssd_conv_b", "ssd_dt_bias",
           "ssd_a_log", "ssd_d", "ssd_norm_g", "ssd_w_out", "final_norm_g")
SMALL_SHARDED = ("norm_g", "ab_gate_w", "ab_gate_b", "ssd_conv_w", "ssd_conv_b", "ssd_norm_g")
SMALL = ("c_ctx", "mod_b", "norm_g", "ab_gate_w", "ab_gate_b", "ab_gla_norm_g", "ab_vnorm_g", "ab_spatial_w", "ab_spatial_b",
         "ssd_conv_w", "ssd_conv_b", "ssd_dt_bias", "ssd_a_log", "ssd_d", "ssd_norm_g", "final_norm_g")
LANES = 1024


def _pack(arrs, rows_multiple=8):
    flat = jnp.concatenate([a.reshape(-1).astype(F32) for a in arrs])
    rows = -(-flat.shape[0] // LANES)
    rows = -(-rows // rows_multiple) * rows_multiple
    return jnp.pad(flat, (0, rows * LANES - flat.shape[0])).reshape(rows, LANES)


def _unpack_flat(buf, shapes):
    lead = buf.shape[:-2]
    flat = buf.reshape(lead + (-1,))
    out, o = [], 0
    for s in shapes:
        n = math.prod(s)
        out.append(flat[..., o:o + n].reshape(lead + tuple(s)))
        o += n
    return out


def _unshard(g):
    g = jnp.moveaxis(g, 0, -2)
    return g.reshape(g.shape[:-2] + (g.shape[-2] * g.shape[-1],))


def _my_shard(full, me, ws):
    return lax.dynamic_slice_in_dim(full, me * ws, ws, axis=full.ndim - 1)


def _silu_vjp(cvec, dsc):
    def body(c_ref, d_ref, o_ref):
        _, vjp = jax.vjp(jax.nn.silu, c_ref[...])
        o_ref[...] = vjp(d_ref[...])[0]

    return pl.pallas_call(body, name="c_ctx_bwd", out_shape=jax.ShapeDtypeStruct(cvec.shape, F32), compiler_params=_cp())(cvec, dsc)


def kernel(x, c, ctx, c_ctx, mod_w, mod_b, norm_g, ffn_w_in, ffn_w_out, ab_w_in, ab_gate_w, ab_gate_b, ab_gla_norm_g, ab_vnorm_g, ab_spatial_w, ab_spatial_b, ab_w_out, ssd_w_in, ssd_conv_w, ssd_conv_b, ssd_dt_bias, ssd_a_log, ssd_d, ssd_norm_g, ssd_w_out, final_norm_g, loss_target, m_c_ctx, m_mod_w, m_mod_b, m_norm_g, m_ffn_w_in, m_ffn_w_out, m_ab_w_in, m_ab_gate_w, m_ab_gate_b, m_ab_gla_norm_g, m_ab_vnorm_g, m_ab_spatial_w, m_ab_spatial_b, m_ab_w_out, m_ssd_w_in, m_ssd_conv_w, m_ssd_conv_b, m_ssd_dt_bias, m_ssd_a_log, m_ssd_d, m_ssd_norm_g, m_ssd_w_out, m_final_norm_g, v_c_ctx, v_mod_w, v_mod_b, v_norm_g, v_ffn_w_in, v_ffn_w_out, v_ab_w_in, v_ab_gate_w, v_ab_gate_b, v_ab_gla_norm_g, v_ab_vnorm_g, v_ab_spatial_w, v_ab_spatial_b, v_ab_w_out, v_ssd_w_in, v_ssd_conv_w, v_ssd_conv_b, v_ssd_dt_bias, v_ssd_a_log, v_ssd_d, v_ssd_norm_g, v_ssd_w_out, v_final_norm_g):
    a = dict(locals())
    me = _slot(*_mesh_pos())
    ws_mod = mod_w.shape[-1]

    fwd_small = [c] + [a[k] for k in SMALL_SHARDED]
    g_small = _ag_small("gather_small", _pack(fwd_small))
    parts = _unpack_flat(g_small, [t.shape for t in fwd_small])
    c_rows = parts[0].reshape(NDEV, D)
    full = {k: _unshard(p) for k, p in zip(SMALL_SHARDED, parts[1:])}
    c_all = jnp.concatenate([c_rows, c_ctx[None], jnp.zeros((7, D), F32)], axis=0)
    m_all = _ag_small("gather_mod", _mod_fwd(c_all, mod_w).reshape(2 * 16, ws_mod)).reshape(NDEV, 2, 16, ws_mod)
    m_mine = lax.dynamic_index_in_dim(m_all, me, axis=2, keepdims=False)
    mx = jnp.moveaxis(m_mine, 0, 1).reshape(2, N_MOD, D) + mod_b.reshape(2, N_MOD, D)
    mc = jnp.moveaxis(m_all[:, :, 8, :], 0, 1).reshape(2, N_MOD, D) + mod_b.reshape(2, N_MOD, D)
    pad2 = jnp.zeros((2, 2, D), F32)
    mod = jnp.concatenate([mx, pad2, mc, pad2], axis=1)

    big = [ffn_w_in[0], ffn_w_in[1], ffn_w_out[0], ffn_w_out[1], ab_w_in[0], ab_w_out[0], ssd_w_in[0], ssd_w_out[0]]
    gath = _ag_big("gather_weights", [w.astype(MXU_DTYPE) for w in big])
    ffn_cols = NDEV * ffn_w_in.shape[-1]
    ffn_segs = ((0, ffn_cols),)
    W = {
        "ffn_in": (_repack("repack_ffn0", gath[0], ffn_segs, ffn_cols), _repack("repack_ffn1", gath[1], ffn_segs, ffn_cols)),
        "ffn_out": (gath[2].reshape(-1, D), gath[3].reshape(-1, D)),
        "ab_in": _repack("repack_ab", gath[4], AB_SEGS, AB_P), "ab_out": gath[5].reshape(-1, D),
        "ssd_in": _repack("repack_ssd", gath[6], SSD_SEGS, SSD_P_W), "ssd_out": gath[7].reshape(-1, D),
    }
    P = {
        "norm_g": full["norm_g"], "ab_gate_w": full["ab_gate_w"][0], "ab_gate_b": full["ab_gate_b"][0],
        "ab_gla_norm_g": ab_gla_norm_g, "ab_vnorm_g": ab_vnorm_g, "ab_spatial_w": ab_spatial_w[0], "ab_spatial_b": ab_spatial_b[0],
        "ssd_conv_w": full["ssd_conv_w"][0], "ssd_conv_b": full["ssd_conv_b"], "ssd_dt_bias": ssd_dt_bias[0],
        "ssd_a_log": ssd_a_log[0], "ssd_d": ssd_d, "ssd_norm_g": full["ssd_norm_g"], "final_norm_g": final_norm_g[None],
    }

    loss, grad_x, dmod, dW, dP = _local_step(x[0], ctx[0], loss_target[0], mod, W, P)

    ws_ffn, ws_ab, ws_ssd = ffn_w_in.shape[-1], ab_w_in.shape[-1], ssd_w_in.shape[-1]
    rs_parts = [
        _unpack("unpack_ffn0", dW["ffn_in"][0], ffn_segs, ws_ffn, MXU_DTYPE), _unpack("unpack_ffn1", dW["ffn_in"][1], ffn_segs, ws_ffn, MXU_DTYPE),
        dW["ffn_out"][0].reshape(NDEV, -1, D), dW["ffn_out"][1].reshape(NDEV, -1, D),
        _unpack("unpack_ab", dW["ab_in"], AB_SEGS, ws_ab, MXU_DTYPE), dW["ab_out"].reshape(NDEV, -1, D),
        _unpack("unpack_ssd", dW["ssd_in"], SSD_SEGS, ws_ssd, MXU_DTYPE), dW["ssd_out"].reshape(NDEV, -1, D),
    ]
    recv = _rs_exchange("scatter_grads", rs_parts)

    dmx, dmc = dmod[:, 0:N_MOD].reshape(2, N_MOD * D), dmod[:, 8:8 + N_MOD].reshape(2, N_MOD * D)
    small_names = ("ab_gate_w", "ab_gate_b", "ab_gla_norm_g", "ab_vnorm_g", "ab_spatial_w", "ab_spatial_b", "norm_g", "ssd_conv_w",
                   "ssd_conv_b", "ssd_dt_bias", "ssd_a_log", "ssd_d", "ssd_norm_g", "final_norm_g")
    bwd_small = [dP[k] for k in small_names] + [dmc, dmx]
    shapes = [t.shape for t in bwd_small]
    g_bwd = _ag_small("gather_small_grads", _pack(bwd_small))
    summed = _unpack_flat(_sum_parts("sum_small_grads", g_bwd), shapes)
    gfull = dict(zip(small_names, summed[:-2]))
    dmc_sum, dmx_sum = summed[-2], summed[-1]
    dmx_all = _unpack_flat(g_bwd, shapes)[-1]
    dmx_sh = jnp.moveaxis(_my_shard(dmx_all, me, ws_mod), 0, 1)
    dm = jnp.concatenate([dmx_sh, _my_shard(dmc_sum, me, ws_mod)[:, None, :], jnp.zeros((2, 7, ws_mod), F32)], axis=1)
    d_mod_w, dsc = _mod_bwd(c_all, mod_w, dm)
    dsc_ctx = (dsc[0, 8] + dsc[1, 8])[None]
    dsc_all = _ag_small("gather_c_ctx_grad", jnp.concatenate([dsc_ctx, jnp.zeros((7, D), F32)], axis=0))
    d_c_ctx = _silu_vjp(c_ctx[None], _sum_parts("sum_c_ctx_grad", dsc_all)[0:1])[0]

    g_small_w = {
        "c_ctx": d_c_ctx, "mod_b": dmx_sum + dmc_sum, "norm_g": gfull["norm_g"], "ab_gate_w": gfull["ab_gate_w"][None],
        "ab_gate_b": gfull["ab_gate_b"][None], "ab_gla_norm_g": gfull["ab_gla_norm_g"], "ab_vnorm_g": gfull["ab_vnorm_g"],
        "ab_spatial_w": gfull["ab_spatial_w"][None], "ab_spatial_b": gfull["ab_spatial_b"][None], "ssd_conv_w": gfull["ssd_conv_w"][None],
        "ssd_conv_b": gfull["ssd_conv_b"], "ssd_dt_bias": gfull["ssd_dt_bias"][None], "ssd_a_log": gfull["ssd_a_log"][None],
        "ssd_d": gfull["ssd_d"], "ssd_norm_g": gfull["ssd_norm_g"], "final_norm_g": gfull["final_norm_g"][0],
    }
    for k in SMALL_SHARDED:
        g_small_w[k] = _my_shard(g_small_w[k], me, a[k].shape[-1])
    res = _adam("adam_small", _pack([a[k] for k in SMALL]), _pack([g_small_w[k] for k in SMALL])[None],
                _pack([a["m_" + k] for k in SMALL]), _pack([a["v_" + k] for k in SMALL]))
    out = {k: vals for k, vals in zip(SMALL, zip(*[_unpack_flat(r, [a[k].shape for k in SMALL]) for r in res]))}

    def adam_big(name, w2d, parts3d, m2d, v2d, shape):
        return tuple(r.reshape(shape) for r in _adam(name, w2d, parts3d, m2d, v2d))

    out["mod_w"] = adam_big("adam_mod_w", mod_w.reshape(-1, ws_mod), d_mod_w.reshape(1, -1, ws_mod), m_mod_w.reshape(-1, ws_mod),
                            v_mod_w.reshape(-1, ws_mod), mod_w.shape)
    for k, idx in (("ffn_w_in", (0, 1)), ("ffn_w_out", (2, 3))):
        per = [adam_big("adam_%s%d" % (k, i), a[k][i], recv[j], a["m_" + k][i], a["v_" + k][i], a[k].shape[1:]) for i, j in enumerate(idx)]
        out[k] = tuple(jnp.stack(t) for t in zip(*per))
    for k, j in (("ab_w_in", 4), ("ab_w_out", 5), ("ssd_w_in", 6), ("ssd_w_out", 7)):
        out[k] = adam_big("adam_" + k, a[k][0], recv[j], a["m_" + k][0], a["v_" + k][0], a[k].shape)

    loss_all = lax.psum(loss[0, 0], ("x", "y", "c"))
    return (loss_all, grad_x[None], *[out[k][0] for k in WEIGHTS], *[out[k][1] for k in WEIGHTS],
            *[out[k][2] for k in WEIGHTS], *[out[k][3] for k in WEIGHTS])
```

```python
import functools
import math

import jax
import jax.numpy as jnp
from jax import lax
from jax.experimental import pallas as pl
from jax.experimental.pallas import tpu as pltpu

F32 = jnp.float32
BF16 = jnp.bfloat16
MXU_DTYPE = jnp.bfloat16
HI = lax.Precision.HIGHEST

D = 1024
NDEV = 8
N_MOD = 6
EPS = 1e-6
GRID_W = 64
CTX = 256
TM = 256
D_FF = 2816
GLA_H, GLA_DK, GLA_DV, GLA_LR, GLA_TAU, GLA_L = 4, 64, 128, 16, 16.0, 64
GMLP_G, GMLP_C, GMLP_L = 4, 128, 128
SSD_H, SSD_P, SSD_G, SSD_N, SSD_L, SSD_K = 32, 64, 4, 128, 128, 5
SSD_INNER = SSD_H * SSD_P
AB_IN = 2592
SSD_IN = 5184
AB_SEGS = ((256, 768), (1056, 1568), (1568, 2080), (2080, 2592), (0, 256), (800, 1056), (768, 800))
AB_P = 2688
SSD_SEGS = ((0, 2048), (3136, 5184), (2048, 2560), (2560, 3072), (3072, 3136))
SSD_P_W = 5376
VMEM_LIMIT = 56 * 1024 * 1024

ADAM_LR, ADAM_B1, ADAM_B2, ADAM_EPS, ADAM_WD, ADAM_STEP = 0.001, 0.9, 0.999, 1e-08, 0.01, 10


def _cp(sem=None, **kw):
    return pltpu.CompilerParams(dimension_semantics=sem, vmem_limit_bytes=VMEM_LIMIT, **kw)


def _dot(a, b, dims=(((1,), (0,)), ((), ()))):
    return lax.dot_general(a.astype(MXU_DTYPE), b.astype(MXU_DTYPE), dims, preferred_element_type=F32)


def _dot_nt(a, b):
    return _dot(a, b, (((1,), (1,)), ((), ())))


def _dot_tn(a, b):
    return _dot(a, b, (((0,), (0,)), ((), ())))


def _dotx(a, b, dims=(((1,), (0,)), ((), ()))):
    return lax.dot_general(a, b, dims, precision=HI, preferred_element_type=F32)


def _rms(x):
    return x * lax.rsqrt(jnp.mean(x * x, axis=-1, keepdims=True) + EPS)


def _pick(n, prefs):
    for p in prefs:
        if n % p == 0:
            return p
    return n


def _row(arr, width=None, colblk=0, tm=TM):
    width = arr.shape[1] if width is None else width
    return (arr, pl.BlockSpec((tm, width), lambda i, c=colblk: (i, c)))


def _full_spec(p):
    nd = p.ndim
    return pl.BlockSpec(p.shape, lambda i, nd=nd: (0,) * nd)


def _rowwise(name, fn, n_blocks, ctx_blk, rows, params, outs, tm=TM):
    nr, npar = len(rows), len(params)

    def body(*refs):
        t = (pl.program_id(0) >= ctx_blk).astype(F32)
        rv = [r[...].astype(F32) for r in refs[:nr]]
        pv = [p[...] for p in refs[nr:nr + npar]]
        res = fn(t, rv, pv)
        for o_ref, o in zip(refs[nr + npar:], res):
            o_ref[...] = o.astype(o_ref.dtype)

    return pl.pallas_call(
        body, name=name, grid=(n_blocks,),
        in_specs=[s for _, s in rows] + [_full_spec(p) for p in params],
        out_specs=[pl.BlockSpec((tm, w), lambda i: (i, 0)) for w, _ in outs],
        out_shape=[jax.ShapeDtypeStruct((n_blocks * tm, w), dt) for w, dt in outs],
        compiler_params=_cp(("parallel",)),
    )(*[a for a, _ in rows], *params)


def _rowwise_vjp(name, fn, n_blocks, ctx_blk, rows, params, douts, row_grads, tm=TM):
    nr, npar, nd = len(rows), len(params), len(douts)
    adds = [a for _, _, a in row_grads if a is not None]
    na = len(adds)

    def body(*refs):
        i = pl.program_id(0)
        t = (i >= ctx_blk).astype(F32)
        rv = [r[...].astype(F32) for r in refs[:nr]]
        pv = [p[...] for p in refs[nr:nr + npar]]
        dv = [r[...].astype(F32) for r in refs[nr + npar:nr + npar + nd]]
        av = [r[...].astype(F32) for r in refs[nr + npar + nd:nr + npar + nd + na]]
        o_refs = refs[nr + npar + nd + na:]
        _, vjp = jax.vjp(lambda r, p: tuple(fn(t, r, p)), rv, pv)
        d_rows, d_params = vjp(tuple(dv))
        ai = 0
        for o_ref, (ri, _, addend) in zip(o_refs, row_grads):
            g = d_rows[ri]
            if addend is not None:
                g = g + av[ai]
                ai += 1
            o_ref[...] = g.astype(o_ref.dtype)
        p_refs = o_refs[len(row_grads):]

        @pl.when(i == 0)
        def _():
            for p_ref in p_refs:
                p_ref[...] = jnp.zeros_like(p_ref)

        for p_ref, g in zip(p_refs, d_params):
            p_ref[...] += g

    widths = [rows[ri][1].block_shape[1] for ri, _, _ in row_grads]
    res = pl.pallas_call(
        body, name=name, grid=(n_blocks,),
        in_specs=[s for _, s in rows] + [_full_spec(p) for p in params] + [s for _, s in douts] + [s for _, s in adds],
        out_specs=[pl.BlockSpec((tm, w), lambda i: (i, 0)) for w in widths] + [_full_spec(p) for p in params],
        out_shape=[jax.ShapeDtypeStruct((n_blocks * tm, w), dt) for w, (_, dt, _) in zip(widths, row_grads)]
        + [jax.ShapeDtypeStruct(p.shape, F32) for p in params],
        compiler_params=_cp(("arbitrary",)),
    )(*[a for a, _ in rows], *params, *[a for a, _ in douts], *[a for a, _ in adds])
    return res[:len(row_grads)], res[len(row_grads):]


def _mm(name, a, b, mode, out_dtype):
    if mode == "nn":
        m, kk = a.shape
        n = b.shape[1]
    elif mode == "nt":
        m, kk = a.shape
        n = b.shape[0]
    else:
        kk, m = a.shape
        n = b.shape[1]
    tm = _pick(m, (1088, 1024, 768, 512, 384, 256, 128))
    tn = _pick(n, (512, 384, 256, 128))
    if mode == "tn":
        tk = _pick(kk, (1088, 768, 512, 256))
    else:
        tk = kk if kk <= 2816 else _pick(kk, (1792, 1408, 1024, 896, 768, 512, 384, 256, 128))
    nk = kk // tk
    if mode == "nn":
        specs = [pl.BlockSpec((tm, tk), lambda i, j, k: (i, k)), pl.BlockSpec((tk, tn), lambda i, j, k: (k, j))]
        dims = (((1,), (0,)), ((), ()))
    elif mode == "nt":
        specs = [pl.BlockSpec((tm, tk), lambda i, j, k: (i, k)), pl.BlockSpec((tn, tk), lambda i, j, k: (j, k))]
        dims = (((1,), (1,)), ((), ()))
    else:
        specs = [pl.BlockSpec((tk, tm), lambda i, j, k: (k, i)), pl.BlockSpec((tk, tn), lambda i, j, k: (k, j))]
        dims = (((0,), (0,)), ((), ()))

    def body(a_ref, b_ref, o_ref, *acc):
        part = lax.dot_general(a_ref[...].astype(MXU_DTYPE), b_ref[...].astype(MXU_DTYPE), dims, preferred_element_type=F32)
        if nk == 1:
            o_ref[...] = part.astype(o_ref.dtype)
        else:
            k = pl.program_id(2)

            @pl.when(k == 0)
            def _():
                acc[0][...] = part

            @pl.when(k > 0)
            def _():
                acc[0][...] += part

            @pl.when(k == nk - 1)
            def _():
                o_ref[...] = acc[0][...].astype(o_ref.dtype)

    return pl.pallas_call(
        body, name=name, grid=(m // tm, n // tn, nk), in_specs=specs,
        out_specs=pl.BlockSpec((tm, tn), lambda i, j, k: (i, j)),
        out_shape=jax.ShapeDtypeStruct((m, n), out_dtype),
        scratch_shapes=[] if nk == 1 else [pltpu.VMEM((tm, tn), F32)],
        compiler_params=_cp(("parallel", "parallel", "arbitrary")),
    )(a, b)


def _sel_mod(modp, t):
    return modp[0:8] * (1.0 - t) + modp[8:16] * t


def _fn_prenorm(t, rows, params, *, a, b):
    (x,), (g, modp) = rows, params
    m = _sel_mod(modp, t)
    return ((_rms(x) * g) * (1.0 + m[b:b + 1]) + m[a:a + 1],)


def _fn_resid_prenorm(t, rows, params, *, gi, a, b):
    (x, y), (g, mod_a, mod_b) = rows, params
    ma, mb = _sel_mod(mod_a, t), _sel_mod(mod_b, t)
    xn = x + ma[gi:gi + 1] * y
    return xn, (_rms(xn) * g) * (1.0 + mb[b:b + 1]) + mb[a:a + 1]


def _fn_resid(t, rows, params, *, gi):
    (x, y), (mod_a,) = rows, params
    return (x + _sel_mod(mod_a, t)[gi:gi + 1] * y,)


def _fn_swiglu(t, rows, params):
    (pf,) = rows
    return (jax.nn.silu(pf[:, :D_FF]) * pf[:, D_FF:],)


def _fn_mixpost(t, rows, params):
    (o_f, o_b, r, u, g), (gla_g, vn_g, sw, sb_t) = rows, params
    o = o_f + o_b
    a = jnp.concatenate([_rms(o[:, h * GLA_DV:(h + 1) * GLA_DV]) for h in range(GLA_H)], axis=1) * gla_g * jax.nn.silu(r)
    uu, vv = jax.nn.gelu(u), jax.nn.gelu(g)
    mu = jnp.mean(vv, axis=-1, keepdims=True)
    var = jnp.mean(jnp.square(vv - mu), axis=-1, keepdims=True)
    vn = ((vv - mu) * lax.rsqrt(var + EPS)) * vn_g
    s = jnp.concatenate(
        [_dot(sw[gi * GMLP_L:(gi + 1) * GMLP_L, :], vn[:, gi * GMLP_C:(gi + 1) * GMLP_C]) + sb_t[:, gi:gi + 1]
         for gi in range(GMLP_G)], axis=1)
    return (jnp.concatenate([a, uu * s], axis=1),)


def _head_expand():
    r = lax.broadcasted_iota(jnp.int32, (SSD_H, SSD_INNER), 0)
    c = lax.broadcasted_iota(jnp.int32, (SSD_H, SSD_INNER), 1)
    return (c // SSD_P == r).astype(F32)


def _fn_ssd_finish(t, rows, params):
    (y_f, y_b, xs, z), (d_skip, norm_g) = rows, params
    d_full = _dotx(jnp.broadcast_to(d_skip, (8, SSD_H)), _head_expand())[0:1]
    y = (y_f + y_b + d_full * xs) * jax.nn.silu(z)
    gw = SSD_INNER // SSD_G
    return (jnp.concatenate([_rms(y[:, gi * gw:(gi + 1) * gw]) for gi in range(SSD_G)], axis=1) * norm_g,)


def _fn_concat(t, rows, params, *, sums, pad=0):
    out, i = [], 0
    for n in sums:
        acc = rows[i]
        for j in range(1, n):
            acc = acc + rows[i + j]
        out.append(acc)
        i += n
    if pad:
        out.append(jnp.zeros((out[0].shape[0], pad), F32))
    return (jnp.concatenate(out, axis=1),)


def _tri(n, rev):
    r = lax.broadcasted_iota(jnp.int32, (n, n), 0)
    c = lax.broadcasted_iota(jnp.int32, (n, n), 1)
    return (r <= c) if rev else (r >= c)


def _gla_chunk(S, v, k, q, tail, gw, gb, *, rev):
    L = GLA_L
    msk = _tri(L, rev)
    tri = msk.astype(F32)
    lr = tail[:, GLA_LR:2 * GLA_LR] if rev else tail[:, 0:GLA_LR]
    la = jax.nn.log_sigmoid(_dot(lr, gw) + gb) / GLA_TAU
    b = _dotx(tri, la)
    b_last = b[0:1] if rev else b[L - 1:L]
    kd = k * jnp.exp(b_last - b)
    qd = (q * GLA_DK ** -0.5) * jnp.exp(b)
    ki = k * jnp.exp(-b)
    dec = jnp.exp(_dotx(la, jnp.ones((L, GLA_DV), F32), (((0,), (0,)), ((), ()))))
    o_parts, s_parts = [], []
    for h in range(GLA_H):
        ks, vs = slice(h * GLA_DK, (h + 1) * GLA_DK), slice(h * GLA_DV, (h + 1) * GLA_DV)
        sh = S[ks, :]
        sc = jnp.where(msk, _dot_nt(qd[:, ks], ki[:, ks]), 0.0)
        o_parts.append(_dot(qd[:, ks], sh) + _dot(sc, v[:, vs]))
        s_parts.append(dec[ks, :] * sh + _dot_tn(kd[:, ks], v[:, vs]))
    return jnp.concatenate(s_parts, axis=0), jnp.concatenate(o_parts, axis=1)


def _ssd_chunk(S, x, bm, cm, tail, dtb, alog, *, rev):
    L = SSD_L
    msk = _tri(L, rev)
    tri = msk.astype(F32)
    r = lax.broadcasted_iota(jnp.int32, (L, L), 0)
    c = lax.broadcasted_iota(jnp.int32, (L, L), 1)
    eye = (r == c).astype(F32)
    raw = tail[:, SSD_H:2 * SSD_H] if rev else tail[:, 0:SSD_H]
    dt = jax.nn.softplus(raw + dtb)
    dta = dt * (-jnp.exp(alog))
    acum = _dotx(tri, dta)
    a_last = acum[0:1] if rev else acum[L - 1:L]
    wst = dt * jnp.exp(a_last - acum)
    eac = jnp.exp(acum)
    tr = _dotx(jnp.concatenate([acum, dt, wst], axis=1), eye, (((0,), (0,)), ((), ())))
    acum_t, dt_t, wst_t = tr[0:SSD_H], tr[SSD_H:2 * SSD_H], tr[2 * SSD_H:3 * SSD_H]
    decrow = jnp.exp(_dotx(jnp.broadcast_to(a_last, (8, SSD_H)), _head_expand())[0:1])
    lane = lax.broadcasted_iota(jnp.int32, (1, 2 * SSD_P), 1)
    m0 = (lane < SSD_P).astype(F32)
    m1 = 1.0 - m0
    pairs_per_group = SSD_H // SSD_G // 2
    y_parts, s_parts = [], []
    for g in range(SSD_G):
        ns = slice(g * SSD_N, (g + 1) * SSD_N)
        bg, cg = bm[:, ns], cm[:, ns]
        cb = _dot_nt(cg, bg)
        bgt = _dotx(bg, eye, (((0,), (0,)), ((), ())))
        for jj in range(pairs_per_group):
            j = g * pairs_per_group + jj
            ls = slice(j * 2 * SSD_P, (j + 1) * 2 * SSD_P)
            xp, sp = x[:, ls], S[:, ls]
            xm = jnp.concatenate([xp * m0, xp * m1], axis=0)
            sm = jnp.concatenate([sp * m0, sp * m1], axis=0)
            lhs, bw = [], []
            for h in (2 * j, 2 * j + 1):
                seg = acum[:, h:h + 1] - acum_t[h:h + 1, :]
                lhs.append(cb * jnp.exp(jnp.where(msk, seg, -jnp.inf)) * dt_t[h:h + 1, :])
                bw.append(bgt * wst_t[h:h + 1, :])
            lhs += [cg * eac[:, h:h + 1] for h in (2 * j, 2 * j + 1)]
            y_parts.append(_dot(jnp.concatenate(lhs, axis=1), jnp.concatenate([xm, sm], axis=0)))
            s_parts.append(sp * decrow[:, ls] + _dot(jnp.concatenate(bw, axis=1), xm))
    return jnp.concatenate(s_parts, axis=1), jnp.concatenate(y_parts, axis=1)


def _scan_order(n, nx, rev, backward):
    nc = n - nx

    def fwd(s):
        return (n - 1 - s) if rev else jnp.where(s < nc, s + nx, s - nc)

    return (lambda s: fwd(n - 1 - s)) if backward else fwd


def _scan_fwd(name, chunk_fn, L, n, nx, rev, rows, params, state_shape, out_w):
    order = _scan_order(n, nx, rev, False)
    nr, npar = len(rows), len(params)

    def body(*refs):
        s_scr = refs[-1]

        @pl.when(pl.program_id(0) == 0)
        def _():
            s_scr[...] = jnp.zeros_like(s_scr)

        s_in = s_scr[...]
        y_ref, st_ref = refs[nr + npar], refs[nr + npar + 1]
        st_ref[0] = s_in
        s_new, y = chunk_fn(s_in, *[r[...] for r in refs[:nr]], *[p[...] for p in refs[nr:nr + npar]], rev=rev)
        y_ref[...] = y
        s_scr[...] = s_new

    return pl.pallas_call(
        body, name=name, grid=(n,),
        in_specs=[pl.BlockSpec((L, w), lambda s, c=c: (order(s), c)) for _, w, c in rows] + [_full_spec(p) for p in params],
        out_specs=[pl.BlockSpec((L, out_w), lambda s: (order(s), 0)),
                   pl.BlockSpec((1,) + state_shape, lambda s: (order(s), 0, 0))],
        out_shape=[jax.ShapeDtypeStruct((n * L, out_w), F32), jax.ShapeDtypeStruct((n,) + state_shape, F32)],
        scratch_shapes=[pltpu.VMEM(state_shape, F32)],
        compiler_params=_cp(("arbitrary",)),
    )(*[a for a, _, _ in rows], *params)


def _scan_bwd(name, chunk_fn, L, n, nx, rev, rows, params, states, dy, state_shape, out_w):
    order = _scan_order(n, nx, rev, True)
    nr, npar = len(rows), len(params)

    def body(*refs):
        i = pl.program_id(0)
        ds_scr = refs[-1]
        rv = [r[...] for r in refs[:nr]]
        pv = [p[...] for p in refs[nr:nr + npar]]
        st_ref, dy_ref = refs[nr + npar], refs[nr + npar + 1]
        o_refs = refs[nr + npar + 2:-1]
        p_refs = o_refs[nr:]

        @pl.when(i == 0)
        def _():
            ds_scr[...] = jnp.zeros_like(ds_scr)
            for p_ref in p_refs:
                p_ref[...] = jnp.zeros_like(p_ref)

        _, vjp = jax.vjp(functools.partial(chunk_fn, rev=rev), st_ref[0], *rv, *pv)
        grads = vjp((ds_scr[...], dy_ref[...].astype(F32)))
        ds_scr[...] = grads[0]
        for o_ref, g in zip(o_refs[:nr], grads[1:1 + nr]):
            o_ref[...] = g
        for p_ref, g in zip(p_refs, grads[1 + nr:]):
            p_ref[...] += g

    res = pl.pallas_call(
        body, name=name, grid=(n,),
        in_specs=[pl.BlockSpec((L, w), lambda s, c=c: (order(s), c)) for _, w, c in rows] + [_full_spec(p) for p in params]
        + [pl.BlockSpec((1,) + state_shape, lambda s: (order(s), 0, 0)), pl.BlockSpec((L, out_w), lambda s: (order(s), 0))],
        out_specs=[pl.BlockSpec((L, w), lambda s: (order(s), 0)) for _, w, _ in rows] + [_full_spec(p) for p in params],
        out_shape=[jax.ShapeDtypeStruct((n * L, w), F32) for _, w, _ in rows] + [jax.ShapeDtypeStruct(p.shape, F32) for p in params],
        scratch_shapes=[pltpu.VMEM(state_shape, F32)],
        compiler_params=_cp(("arbitrary",)),
    )(*[a for a, _, _ in rows], *params, states, dy)
    return res[:nr], res[nr:]


CONV_W = 1024
CONV_COLBLK = (0, 1, 4)


def _conv_specs(nb, src_blk):
    halo = TM // 8
    return [pl.BlockSpec((TM, CONV_W), lambda j, i: (i, src_blk(j))),
            pl.BlockSpec((8, CONV_W), lambda j, i: (jnp.maximum(i * halo - 1, 0), src_blk(j))),
            pl.BlockSpec((8, CONV_W), lambda j, i: (jnp.minimum(i * halo + halo, nb * halo - 1), src_blk(j)))]


def _conv_ext(i, nb, cur, prev, nxt):
    has_prev = jnp.logical_and(i > 0, i < nb - 1)
    has_next = i < nb - 2
    return jnp.concatenate([jnp.where(has_prev, prev, 0.0), cur, jnp.where(has_next, nxt, 0.0)], axis=0)


def _conv_taps(ext, w, flip):
    acc = None
    for j in range(SSD_K):
        wj = w[SSD_K - 1 - j:SSD_K - j, :] if flip else w[j:j + 1, :]
        term = wj * ext[6 + j:6 + j + TM, :]
        acc = term if acc is None else acc + term
    return acc


def _conv(name, src, w8, b1, nb, *, permuted_src, act, flip, out_dtype):
    src_blk = (lambda j: jnp.where(j == 2, CONV_COLBLK[2], j)) if permuted_src else (lambda j: j)

    def body(cur, prev, nxt, w_ref, b_ref, o_ref):
        ext = _conv_ext(pl.program_id(1), nb, cur[...].astype(F32), prev[...].astype(F32), nxt[...].astype(F32))
        acc = _conv_taps(ext, w_ref[...], flip)
        if act:
            acc = jax.nn.silu(acc + b_ref[...])
        o_ref[...] = acc.astype(o_ref.dtype)

    return pl.pallas_call(
        body, name=name, grid=(3, nb),
        in_specs=_conv_specs(nb, src_blk) + [pl.BlockSpec((8, CONV_W), lambda j, i: (0, j)), pl.BlockSpec((1, CONV_W), lambda j, i: (0, j))],
        out_specs=pl.BlockSpec((TM, CONV_W), lambda j, i: (i, j)),
        out_shape=jax.ShapeDtypeStruct((nb * TM, 3 * CONV_W), out_dtype),
        compiler_params=_cp(("parallel", "parallel")),
    )(src, src, src, w8, b1)


def _conv_bwd_pre(name, p1, w8, b1, dxbc_parts, nb):
    src_blk = lambda j: jnp.where(j == 2, CONV_COLBLK[2], j)
    xs_parts, bc_parts = dxbc_parts
    n_x, n_bc = len(xs_parts), len(bc_parts)

    def body(*refs):
        cur, prev, nxt, w_ref, b_ref = refs[:5]
        d_refs = refs[5:5 + n_x + n_bc]
        da_ref, dw_ref, db_ref = refs[5 + n_x + n_bc:]
        j, i = pl.program_id(0), pl.program_id(1)
        ext = _conv_ext(i, nb, cur[...], prev[...], nxt[...])
        acc = _conv_taps(ext, w_ref[...], False) + b_ref[...]
        dx = d_refs[0][...]
        for r in d_refs[1:n_x]:
            dx = dx + r[...]
        dbc = jnp.concatenate([d_refs[n_x][...] + d_refs[n_x + 1][...], d_refs[n_x + 2][...] + d_refs[n_x + 3][...]], axis=1)
        dy = jnp.where(j == 2, dbc, dx)
        sg = jax.nn.sigmoid(acc)
        da = dy * (sg + acc * sg * (1.0 - sg))
        da_ref[...] = da

        @pl.when(i == 0)
        def _():
            dw_ref[...] = jnp.zeros_like(dw_ref)
            db_ref[...] = jnp.zeros_like(db_ref)

        rows = [jnp.sum(da * ext[6 + t:6 + t + TM, :], axis=0, keepdims=True) for t in range(SSD_K)]
        dw_ref[...] += jnp.concatenate(rows + [jnp.zeros((8 - SSD_K, CONV_W), F32)], axis=0)
        db_ref[...] += jnp.sum(da, axis=0, keepdims=True)

    x_specs = [pl.BlockSpec((TM, CONV_W), lambda j, i: (i, jnp.minimum(j, 1))) for _ in xs_parts]
    bc_specs = [pl.BlockSpec((TM, 512), lambda j, i: (i, 0)) for _ in bc_parts]
    return pl.pallas_call(
        body, name=name, grid=(3, nb),
        in_specs=_conv_specs(nb, src_blk) + [pl.BlockSpec((8, CONV_W), lambda j, i: (0, j)), pl.BlockSpec((1, CONV_W), lambda j, i: (0, j))]
        + x_specs + bc_specs,
        out_specs=[pl.BlockSpec((TM, CONV_W), lambda j, i: (i, j)), pl.BlockSpec((8, CONV_W), lambda j, i: (0, j)),
                   pl.BlockSpec((1, CONV_W), lambda j, i: (0, j))],
        out_shape=[jax.ShapeDtypeStruct((nb * TM, 3 * CONV_W), F32), jax.ShapeDtypeStruct((8, 3 * CONV_W), F32),
                   jax.ShapeDtypeStruct((1, 3 * CONV_W), F32)],
        compiler_params=_cp(("arbitrary", "arbitrary")),
    )(p1, p1, p1, w8, b1, *xs_parts, *bc_parts)


def _grid_rows(val, a, kb):
    return jnp.concatenate([val[:, t * D:(t + 1) * D] for t in range(kb)], axis=0)


def _perm(name, xc, a, nb):
    n = xc.shape[0]
    b = (n - CTX) // a
    kb = TM // a
    view = xc.reshape(n // b, b * D)

    def body(v_ref, c_ref, o_ref):
        i = pl.program_id(0)

        @pl.when(i < nb - 1)
        def _():
            o_ref[...] = _grid_rows(v_ref[...], a, kb)

        @pl.when(i == nb - 1)
        def _():
            o_ref[...] = c_ref[...]

    return pl.pallas_call(
        body, name=name, grid=(nb,),
        in_specs=[pl.BlockSpec((a, kb * D), lambda i: (0, jnp.minimum(i, nb - 2))), pl.BlockSpec((TM, D), lambda i: (nb - 1, 0))],
        out_specs=pl.BlockSpec((TM, D), lambda i: (i, 0)),
        out_shape=jax.ShapeDtypeStruct((n, D), xc.dtype),
        compiler_params=_cp(("parallel",)),
    )(view, xc)


def _loss_head(x, f, target, modp, g_final, nb, rows_r):
    t_tok = target.shape[0]
    kb = TM // rows_r
    tview = target.reshape(rows_r, (t_tok // rows_r) * D)

    def fn(x_, f_, tgt, modp_, g_, is_ctx):
        xn = x_ + _sel_mod(modp_, is_ctx)[5:6] * f_
        err = _rms(xn) * g_ - tgt
        return 0.5 * jnp.sum(jnp.mean(err * err, axis=-1)) * (1.0 - is_ctx)

    def body(x_ref, f_ref, t_ref, m_ref, g_ref, l_ref, dx_ref, df_ref, dm_ref, dg_ref):
        i = pl.program_id(0)
        is_ctx = (i == nb - 1).astype(F32)
        tgt = _grid_rows(t_ref[...], rows_r, kb)
        l, vjp = jax.vjp(lambda a_, b_, c_, d_: fn(a_, b_, tgt, c_, d_, is_ctx), x_ref[...], f_ref[...], m_ref[...], g_ref[...])
        dx, df, dm, dg = vjp(jnp.ones((), F32))

        @pl.when(i == 0)
        def _():
            l_ref[...] = jnp.zeros_like(l_ref)
            dm_ref[...] = jnp.zeros_like(dm_ref)
            dg_ref[...] = jnp.zeros_like(dg_ref)

        l_ref[...] += jnp.reshape(l, (1, 1))
        dx_ref[...] = dx
        df_ref[...] = df.astype(df_ref.dtype)
        dm_ref[...] += dm
        dg_ref[...] += dg

    rowspec = pl.BlockSpec((TM, D), lambda i: (i, 0))
    return pl.pallas_call(
        body, name="loss_head", grid=(nb,),
        in_specs=[rowspec, rowspec, pl.BlockSpec((rows_r, kb * D), lambda i: (0, jnp.minimum(i, nb - 2))),
                  _full_spec(modp), _full_spec(g_final)],
        out_specs=[pl.BlockSpec((1, 1), lambda i: (0, 0)), rowspec, rowspec, _full_spec(modp), _full_spec(g_final)],
        out_shape=[jax.ShapeDtypeStruct((1, 1), F32), jax.ShapeDtypeStruct(x.shape, F32), jax.ShapeDtypeStruct(x.shape, MXU_DTYPE),
                   jax.ShapeDtypeStruct(modp.shape, F32), jax.ShapeDtypeStruct(g_final.shape, F32)],
        compiler_params=_cp(("arbitrary",)),
    )(x, f, tview, modp, g_final)


def _repack(name, shards, segs, wp):
    nd, kk, ws = shards.shape
    tr = 128
    used = sum(e - s for s, e in segs)

    def body(a_ref, o_ref):
        full = jnp.concatenate([a_ref[d].astype(F32) for d in range(nd)], axis=1)
        parts = [full[:, s:e] for s, e in segs]
        if wp > used:
            parts.append(jnp.zeros((tr, wp - used), F32))
        o_ref[...] = jnp.concatenate(parts, axis=1).astype(o_ref.dtype)

    return pl.pallas_call(
        body, name=name, grid=(kk // tr,),
        in_specs=[pl.BlockSpec((nd, tr, ws), lambda i: (0, i, 0))],
        out_specs=pl.BlockSpec((tr, wp), lambda i: (i, 0)),
        out_shape=jax.ShapeDtypeStruct((kk, wp), MXU_DTYPE),
        compiler_params=_cp(("parallel",)),
    )(shards)


def _unpack(name, dw, segs, ws, out_dtype):
    kk, wp = dw.shape
    tr = 128
    order = sorted(range(len(segs)), key=lambda i: segs[i][0])
    offs, o = [], 0
    for s, e in segs:
        offs.append(o)
        o += e - s

    def body(a_ref, o_ref):
        a = a_ref[...].astype(F32)
        full = jnp.concatenate([a[:, offs[i]:offs[i] + segs[i][1] - segs[i][0]] for i in order], axis=1)
        for d in range(NDEV):
            o_ref[d] = full[:, d * ws:(d + 1) * ws].astype(o_ref.dtype)

    return pl.pallas_call(
        body, name=name, grid=(kk // tr,),
        in_specs=[pl.BlockSpec((tr, wp), lambda i: (i, 0))],
        out_specs=pl.BlockSpec((NDEV, tr, ws), lambda i: (0, i, 0)),
        out_shape=jax.ShapeDtypeStruct((NDEV, kk, ws), out_dtype),
        compiler_params=_cp(("parallel",)),
    )(dw)


def _adam_math(w, g, m, v):
    m = ADAM_B1 * m + (1.0 - ADAM_B1) * g
    v = ADAM_B2 * v + (1.0 - ADAM_B2) * jnp.square(g)
    m_hat = m / (1.0 - ADAM_B1 ** ADAM_STEP)
    v_hat = v / (1.0 - ADAM_B2 ** ADAM_STEP)
    delta = -ADAM_LR * (m_hat / (jnp.sqrt(v_hat) + ADAM_EPS) + ADAM_WD * w)
    return delta, m, v


def _adam(name, w, parts, m, v):
    r, c = w.shape
    npart = parts.shape[0]
    tr = _pick(r, (256, 128, 64, 32, 16, 8)) if r * c * 4 > (1 << 20) else r

    def body(w_ref, p_ref, m_ref, v_ref, g_ref, d_ref, nm_ref, nv_ref):
        g = p_ref[0].astype(F32)
        for s in range(1, npart):
            g = g + p_ref[s].astype(F32)
        delta, nm, nv = _adam_math(w_ref[...], g, m_ref[...], v_ref[...])
        g_ref[...], d_ref[...], nm_ref[...], nv_ref[...] = g, delta, nm, nv

    spec = pl.BlockSpec((tr, c), lambda i: (i, 0))
    return pl.pallas_call(
        body, name=name, grid=(r // tr,),
        in_specs=[spec, pl.BlockSpec((npart, tr, c), lambda i: (0, i, 0)), spec, spec],
        out_specs=[spec] * 4, out_shape=[jax.ShapeDtypeStruct((r, c), F32)] * 4,
        compiler_params=_cp(("parallel",)),
    )(w, parts, m, v)


def _mod_fwd(c_all, mod_w):
    nl, _, ws = mod_w.shape

    def body(c_ref, w_ref, o_ref):
        o_ref[0] = _dot(jax.nn.silu(c_ref[...]), w_ref[0])

    return pl.pallas_call(
        body, name="mod_fwd", grid=(nl,),
        in_specs=[_full_spec(c_all), pl.BlockSpec((1, D, ws), lambda i: (i, 0, 0))],
        out_specs=pl.BlockSpec((1, 16, ws), lambda i: (i, 0, 0)),
        out_shape=jax.ShapeDtypeStruct((nl, 16, ws), F32),
        compiler_params=_cp(("parallel",)),
    )(c_all, mod_w)


def _mod_bwd(c_all, mod_w, dm):
    nl, _, ws = mod_w.shape

    def body(c_ref, w_ref, d_ref, dw_ref, dc_ref):
        dw_ref[0] = _dot_tn(jax.nn.silu(c_ref[...]), d_ref[0])
        dc_ref[0] = _dot_nt(d_ref[0], w_ref[0])

    return pl.pallas_call(
        body, name="mod_bwd", grid=(nl,),
        in_specs=[_full_spec(c_all), pl.BlockSpec((1, D, ws), lambda i: (i, 0, 0)), pl.BlockSpec((1, 16, ws), lambda i: (i, 0, 0))],
        out_specs=[pl.BlockSpec((1, D, ws), lambda i: (i, 0, 0)), pl.BlockSpec((1, 16, D), lambda i: (i, 0, 0))],
        out_shape=[jax.ShapeDtypeStruct((nl, D, ws), F32), jax.ShapeDtypeStruct((nl, 16, D), F32)],
        compiler_params=_cp(("parallel",)),
    )(c_all, mod_w, dm)


def _sum_parts(name, parts):
    npart, r, c = parts.shape

    def body(p_ref, o_ref):
        g = p_ref[0].astype(F32)
        for s in range(1, npart):
            g = g + p_ref[s].astype(F32)
        o_ref[...] = g

    return pl.pallas_call(body, name=name, out_shape=jax.ShapeDtypeStruct((r, c), F32), compiler_params=_cp())(parts)


MESH = pl.DeviceIdType.MESH
ANY = pl.BlockSpec(memory_space=pl.ANY)
N_PEERS = NDEV - 1


def _mesh_pos():
    return lax.axis_index("x"), lax.axis_index("y"), lax.axis_index("c")


def _slot(px, py, pc):
    return 4 * px + 2 * py + pc


def _two_level_gather(x_refs, o_refs, send_sems, recv_sems, local_sems):
    x, y, c = _mesh_pos()
    me, sibling = (x, y, c), (x, y, 1 - c)
    chips = [(1 - x, y), (x, 1 - y), (1 - x, 1 - y)]
    n = len(x_refs)

    def copy(a, k, block, to, src=None):
        dst = o_refs[a].at[_slot(*block)]
        return pltpu.make_async_remote_copy(src_ref=dst if src is None else src, dst_ref=dst, send_sem=send_sems.at[a, k],
                                            recv_sem=recv_sems.at[a, k], device_id=to, device_id_type=MESH)

    mine = [pltpu.make_async_copy(x_refs[a], o_refs[a].at[_slot(*me)], local_sems.at[a]) for a in range(n)]
    for cp in mine:
        cp.start()
    first = []
    for a in range(n):
        first.append(copy(a, 0, me, sibling, src=x_refs[a]))
        first += [copy(a, 1 + j, me, (*chip, c), src=x_refs[a]) for j, chip in enumerate(chips)]
    for cp in first:
        cp.start()
    passed = []
    for j, chip in enumerate(chips):
        for a in range(n):
            copy(a, 1 + j, (*chip, c), me).wait_recv()
            fwd = copy(a, 4 + j, (*chip, c), sibling)
            fwd.start()
            passed.append(fwd)
    for a in range(n):
        copy(a, 0, sibling, me).wait_recv()
        for j, chip in enumerate(chips):
            copy(a, 4 + j, (*chip, 1 - c), me).wait_recv()
    for cp in first + passed:
        cp.wait_send()
    for cp in mine:
        cp.wait()


def _ag_small(name, x):
    r, c = x.shape

    def body(x_ref, o_ref, send_sems, recv_sems, local_sems):
        _two_level_gather([x_ref], [o_ref], send_sems, recv_sems, local_sems)

    return pl.pallas_call(
        body, name=name, out_shape=jax.ShapeDtypeStruct((NDEV, r, c), x.dtype),
        in_specs=[pl.BlockSpec(memory_space=pltpu.VMEM)], out_specs=pl.BlockSpec(memory_space=pltpu.VMEM),
        scratch_shapes=[pltpu.SemaphoreType.DMA((1, N_PEERS)), pltpu.SemaphoreType.DMA((1, N_PEERS)), pltpu.SemaphoreType.DMA((1,))],
        compiler_params=pltpu.CompilerParams(vmem_limit_bytes=VMEM_LIMIT),
    )(x)


def _ag_big(name, shards):
    n = len(shards)

    def body(*refs):
        _two_level_gather(refs[:n], refs[n:2 * n], *refs[2 * n:])

    return pl.pallas_call(
        body, name=name, out_shape=[jax.ShapeDtypeStruct((NDEV,) + s.shape, s.dtype) for s in shards],
        in_specs=[ANY] * n, out_specs=[ANY] * n,
        scratch_shapes=[pltpu.SemaphoreType.DMA((n, N_PEERS)), pltpu.SemaphoreType.DMA((n, N_PEERS)), pltpu.SemaphoreType.DMA((n,))],
    )(*shards)


def _rs_exchange(name, parts):
    n = len(parts)

    def body(*refs):
        p_refs, r_refs = refs[:n], refs[n:2 * n]
        send_sems, recv_sems, local_sems = refs[2 * n:]
        x, y, c = _mesh_pos()
        me = _slot(x, y, c)
        local = [pltpu.make_async_copy(p_refs[a].at[me], r_refs[a].at[me], local_sems.at[a]) for a in range(n)]
        for cp in local:
            cp.start()
        sends, recvs = [], []
        for k in range(1, NDEV):
            peer = ((1 - x) if k & 4 else x, (1 - y) if k & 2 else y, (1 - c) if k & 1 else c)
            ps = _slot(*peer)
            for a in range(n):
                sends.append(pltpu.make_async_remote_copy(
                    src_ref=p_refs[a].at[ps], dst_ref=r_refs[a].at[me], send_sem=send_sems.at[a, k - 1],
                    recv_sem=recv_sems.at[a, k - 1], device_id=peer, device_id_type=MESH))
                recvs.append(pltpu.make_async_remote_copy(
                    src_ref=p_refs[a].at[ps], dst_ref=r_refs[a].at[ps], send_sem=send_sems.at[a, k - 1],
                    recv_sem=recv_sems.at[a, k - 1], device_id=peer, device_id_type=MESH))
        for cp in sends:
            cp.start()
        for cp in recvs:
            cp.wait_recv()
        for cp in sends:
            cp.wait_send()
        for cp in local:
            cp.wait()

    return pl.pallas_call(
        body, name=name, out_shape=[jax.ShapeDtypeStruct(p.shape, p.dtype) for p in parts],
        in_specs=[ANY] * n, out_specs=[ANY] * n,
        scratch_shapes=[pltpu.SemaphoreType.DMA((n, N_PEERS)), pltpu.SemaphoreType.DMA((n, N_PEERS)), pltpu.SemaphoreType.DMA((n,))],
    )(*parts)


def _ffn_fwd(tag, h, w_in, w_out, nb, cb):
    pf = _mm(tag + "_ffn_in", h, w_in, "nn", F32)
    (act,) = _rowwise(tag + "_swiglu", _fn_swiglu, nb, cb, [_row(pf)], [], [(D_FF, MXU_DTYPE)])
    return pf, act, _mm(tag + "_ffn_out", act, w_out, "nn", F32)


def _ffn_bwd(tag, h, pf, act, df, w_in, w_out, nb, cb):
    dw_out = _mm(tag + "_ffn_out_dw", act, df, "tn", MXU_DTYPE)
    dact = _mm(tag + "_ffn_out_dx", df, w_out, "nt", MXU_DTYPE)
    (dpf,), _ = _rowwise_vjp(tag + "_swiglu_bwd", _fn_swiglu, nb, cb, [_row(pf)], [], [_row(dact)], [(0, MXU_DTYPE, None)])
    dw_in = _mm(tag + "_ffn_in_dw", h, dpf, "tn", MXU_DTYPE)
    dh = _mm(tag + "_ffn_in_dx", dpf, w_in, "nt", MXU_DTYPE)
    return dw_out, dw_in, dh


def _local_step(x, ctx, target, mod, W, P):
    T = x.shape[0]
    N = T + CTX
    nb, cb = N // TM, N // TM - 1
    R = T // GRID_W
    mod0, mod1 = mod[0], mod[1]
    ng = P["norm_g"]
    g00, g01, g10, g11 = ng[0, 0][None], ng[0, 1][None], ng[1, 0][None], ng[1, 1][None]
    pre = functools.partial(_fn_prenorm, a=0, b=1)
    rpre = functools.partial(_fn_resid_prenorm, gi=2, a=3, b=4)
    res5 = functools.partial(_fn_resid, gi=5)
    dirs = (("f", False), ("b", True))

    xc0 = jnp.concatenate([x, ctx], axis=0)
    (h0,) = _rowwise("l0_prenorm", pre, nb, cb, [_row(xc0)], [g00, mod0], [(D, MXU_DTYPE)])
    p0 = _mm("l0_in", h0, W["ab_in"], "nn", F32)
    gla_rows = [(p0, 512, 0), (p0, 256, 8), (p0, 256, 9), (p0, 128, 20)]
    n64, nx64 = N // GLA_L, T // GLA_L
    gla_par = {d: [P["ab_gate_w"][int(r)], P["ab_gate_b"][int(r)][None]] for d, r in dirs}
    gla_state = (GLA_H * GLA_DK, GLA_DV)
    o, st0 = {}, {}
    for d, rev in dirs:
        o[d], st0[d] = _scan_fwd("gla_fwd_" + d, _gla_chunk, GLA_L, n64, nx64, rev, gla_rows, gla_par[d], gla_state, GLA_H * GLA_DV)
    n128, cb128 = N // GMLP_L, T // GMLP_L
    mix_rows = [_row(o["f"], tm=GMLP_L), _row(o["b"], tm=GMLP_L)] + [_row(p0, 512, j, tm=GMLP_L) for j in (1, 2, 3)]
    mix_par = [P["ab_gla_norm_g"], P["ab_vnorm_g"], P["ab_spatial_w"].reshape(GMLP_G * GMLP_L, GMLP_L), P["ab_spatial_b"].T]
    (cat0,) = _rowwise("l0_mix", _fn_mixpost, n128, cb128, mix_rows, mix_par, [(D, MXU_DTYPE)], tm=GMLP_L)
    y0 = _mm("l0_out", cat0, W["ab_out"], "nn", F32)
    x1, h1 = _rowwise("l0_ffn_prenorm", rpre, nb, cb, [_row(xc0), _row(y0)], [g01, mod0, mod0], [(D, F32), (D, MXU_DTYPE)])
    pf0, act0, f0 = _ffn_fwd("l0", h1, W["ffn_in"][0], W["ffn_out"][0], nb, cb)
    (x2,) = _rowwise("l0_resid", res5, nb, cb, [_row(x1), _row(f0)], [mod0], [(D, F32)])
    x2p = _perm("to_col_major", x2, R, nb)

    (h2,) = _rowwise("l1_prenorm", pre, nb, cb, [_row(x2p)], [g10, mod1], [(D, MXU_DTYPE)])
    p1 = _mm("l1_in", h2, W["ssd_in"], "nn", F32)
    conv_w8 = jnp.concatenate([P["ssd_conv_w"], jnp.zeros((8 - SSD_K, 3 * CONV_W), F32)], axis=0)
    xbc = _conv("l1_conv", p1, conv_w8, P["ssd_conv_b"], nb, permuted_src=True, act=True, flip=False, out_dtype=F32)
    ssd_rows = [(xbc, SSD_INNER, 0), (xbc, 512, 4), (xbc, 512, 5), (p1, 128, 40)]
    ns, nxs = N // SSD_L, T // SSD_L
    ssd_par = {d: [P["ssd_dt_bias"][int(r)][None], P["ssd_a_log"][int(r)][None]] for d, r in dirs}
    ssd_state = (SSD_N, SSD_INNER)
    ys, st1 = {}, {}
    for d, rev in dirs:
        ys[d], st1[d] = _scan_fwd("ssd_fwd_" + d, _ssd_chunk, SSD_L, ns, nxs, rev, ssd_rows, ssd_par[d], ssd_state, SSD_INNER)
    fin_rows = [_row(ys["f"]), _row(ys["b"]), _row(xbc, SSD_INNER, 0), _row(p1, SSD_INNER, 1)]
    fin_par = [P["ssd_d"], P["ssd_norm_g"]]
    (yn,) = _rowwise("l1_finish", _fn_ssd_finish, nb, cb, fin_rows, fin_par, [(SSD_INNER, MXU_DTYPE)])
    y1 = _mm("l1_out", yn, W["ssd_out"], "nn", F32)
    x3, h3 = _rowwise("l1_ffn_prenorm", rpre, nb, cb, [_row(x2p), _row(y1)], [g11, mod1, mod1], [(D, F32), (D, MXU_DTYPE)])
    pf1, act1, f1 = _ffn_fwd("l1", h3, W["ffn_in"][1], W["ffn_out"][1], nb, cb)
    loss, dx3, df1, dm1_j, d_final_g = _loss_head(x3, f1, target, mod1, P["final_norm_g"], nb, R)

    dW, dP = {}, {"final_norm_g": d_final_g}
    dwo1, dwi1, dh3 = _ffn_bwd("l1", h3, pf1, act1, df1, W["ffn_in"][1], W["ffn_out"][1], nb, cb)
    (dx2p_a, dy1), (dg11, dm1_a, dm1_b) = _rowwise_vjp(
        "l1_ffn_prenorm_bwd", rpre, nb, cb, [_row(x2p), _row(y1)], [g11, mod1, mod1], [_row(dx3), _row(dh3)],
        [(0, F32, None), (1, MXU_DTYPE, None)])
    dW["ssd_out"] = _mm("l1_out_dw", yn, dy1, "tn", MXU_DTYPE)
    dyn = _mm("l1_out_dx", dy1, W["ssd_out"], "nt", MXU_DTYPE)
    (dys, dxs, dz), (dP["ssd_d"], dP["ssd_norm_g"]) = _rowwise_vjp(
        "l1_finish_bwd", _fn_ssd_finish, nb, cb, fin_rows, fin_par, [_row(dyn)],
        [(0, F32, None), (2, F32, None), (3, MXU_DTYPE, None)])
    dxp, dbp, dcp, dtl, ddtb, dalog = [], [], [], [], [], []
    for d, rev in dirs:
        (dx_, db_, dc_, dt_), (ddtb_, dalog_) = _scan_bwd("ssd_bwd_" + d, _ssd_chunk, SSD_L, ns, nxs, rev, ssd_rows, ssd_par[d],
                                                          st1[d], dys, ssd_state, SSD_INNER)
        dxp.append(dx_); dbp.append(db_); dcp.append(dc_); dtl.append(dt_); ddtb.append(ddtb_); dalog.append(dalog_)
    dP["ssd_dt_bias"] = jnp.concatenate(ddtb, axis=0)
    dP["ssd_a_log"] = jnp.concatenate(dalog, axis=0)
    dacc, dcw8, dP["ssd_conv_b"] = _conv_bwd_pre("l1_conv_bwd", p1, conv_w8, P["ssd_conv_b"],
                                                  (dxp + [dxs], [dbp[0], dbp[1], dcp[0], dcp[1]]), nb)
    dP["ssd_conv_w"] = dcw8[:SSD_K]
    dpc = _conv("l1_conv_dx", dacc, conv_w8, jnp.zeros((1, 3 * CONV_W), F32), nb, permuted_src=False, act=False, flip=True,
                out_dtype=MXU_DTYPE)
    cat1 = functools.partial(_fn_concat, sums=(1, 1, 1, 2), pad=SSD_P_W - 5248)
    (dp1,) = _rowwise("l1_dp", cat1, nb, cb, [_row(dpc, SSD_INNER, 0), _row(dz), _row(dpc, 1024, 2), _row(dtl[0]), _row(dtl[1])],
                      [], [(SSD_P_W, MXU_DTYPE)])
    dW["ssd_in"] = _mm("l1_in_dw", h2, dp1, "tn", F32)
    dh2 = _mm("l1_in_dx", dp1, W["ssd_in"], "nt", MXU_DTYPE)
    (dx2p,), (dg10, dm1_f) = _rowwise_vjp("l1_prenorm_bwd", pre, nb, cb, [_row(x2p)], [g10, mod1], [_row(dh2)],
                                          [(0, F32, _row(dx2p_a))])
    dx2 = _perm("to_row_major", dx2p, GRID_W, nb)

    (dx1_a, df0), (dm0_e,) = _rowwise_vjp("l0_resid_bwd", res5, nb, cb, [_row(x1), _row(f0)], [mod0], [_row(dx2)],
                                          [(0, F32, None), (1, MXU_DTYPE, None)])
    dwo0, dwi0, dh1 = _ffn_bwd("l0", h1, pf0, act0, df0, W["ffn_in"][0], W["ffn_out"][0], nb, cb)
    (dxc0_a, dy0), (dg01, dm0_a, dm0_b) = _rowwise_vjp(
        "l0_ffn_prenorm_bwd", rpre, nb, cb, [_row(xc0), _row(y0)], [g01, mod0, mod0], [_row(dx1_a), _row(dh1)],
        [(0, F32, None), (1, MXU_DTYPE, None)])
    dW["ab_out"] = _mm("l0_out_dw", cat0, dy0, "tn", MXU_DTYPE)
    dcat0 = _mm("l0_out_dx", dy0, W["ab_out"], "nt", MXU_DTYPE)
    (do, dr, du, dgm), (dP["ab_gla_norm_g"], dP["ab_vnorm_g"], dsw, dsb_t) = _rowwise_vjp(
        "l0_mix_bwd", _fn_mixpost, n128, cb128, mix_rows, mix_par, [_row(dcat0, tm=GMLP_L)],
        [(0, F32, None), (2, MXU_DTYPE, None), (3, MXU_DTYPE, None), (4, MXU_DTYPE, None)], tm=GMLP_L)
    dP["ab_spatial_w"] = dsw.reshape(GMLP_G, GMLP_L, GMLP_L)
    dP["ab_spatial_b"] = dsb_t.T
    gl, dgw, dgb = [], [], []
    for d, rev in dirs:
        g4, (dgw_, dgb_) = _scan_bwd("gla_bwd_" + d, _gla_chunk, GLA_L, n64, nx64, rev, gla_rows, gla_par[d], st0[d], do,
                                     gla_state, GLA_H * GLA_DV)
        gl.append(g4); dgw.append(dgw_[None]); dgb.append(dgb_)
    dP["ab_gate_w"] = jnp.concatenate(dgw, axis=0)
    dP["ab_gate_b"] = jnp.concatenate(dgb, axis=0)
    cat0f = functools.partial(_fn_concat, sums=(2, 1, 1, 1, 2, 2, 2))
    (dp0,) = _rowwise("l0_dp", cat0f, nb, cb,
                      [_row(gl[0][0]), _row(gl[1][0]), _row(dr), _row(du), _row(dgm), _row(gl[0][1]), _row(gl[1][1]),
                       _row(gl[0][2]), _row(gl[1][2]), _row(gl[0][3]), _row(gl[1][3])], [], [(AB_P, MXU_DTYPE)])
    dW["ab_in"] = _mm("l0_in_dw", h0, dp0, "tn", F32)
    dh0 = _mm("l0_in_dx", dp0, W["ab_in"], "nt", MXU_DTYPE)
    (dxc0,), (dg00, dm0_s) = _rowwise_vjp("l0_prenorm_bwd", pre, nb, cb, [_row(xc0)], [g00, mod0], [_row(dh0)],
                                          [(0, F32, _row(dxc0_a))])
    dW["ffn_in"] = (dwi0, dwi1)
    dW["ffn_out"] = (dwo0, dwo1)
    dP["norm_g"] = jnp.concatenate([dg00, dg01, dg10, dg11], axis=0).reshape(2, 2, D)
    dmod = jnp.stack([dm0_s + dm0_a + dm0_b + dm0_e, dm1_f + dm1_a + dm1_b + dm1_j])
    return loss, dxc0[:T], dmod, dW, dP


WEIGHTS = ("c_ctx", "mod_w", "mod_b", "norm_g", "ffn_w_in", "ffn_w_out", "ab_w_in", "ab_gate_w", "ab_gate_b", "ab_gla_norm_g",
           "ab_vnorm_g", "ab_spatial_w", "ab_spatial_b", "ab_w_out", "ssd_w_in", "ssd_conv_w", "ssd_conv_b", "ssd_dt_bias",
           "ssd_a_log", "ssd_d", "ssd_norm_g", "ssd_w_out", "final_norm_g")
SMALL_SHARDED = ("norm_g", "ab_gate_w", "ab_gate_b", "ssd_conv_w", "ssd_conv_b", "ssd_norm_g")
SMALL = ("c_ctx", "mod_b", "norm_g", "ab_gate_w", "ab_gate_b", "ab_gla_norm_g", "ab_vnorm_g", "ab_spatial_w", "ab_spatial_b",
         "ssd_conv_w", "ssd_conv_b", "ssd_dt_bias", "ssd_a_log", "ssd_d", "ssd_norm_g", "final_norm_g")
LANES = 1024


def _pack(arrs, rows_multiple=8):
    flat = jnp.concatenate([a.reshape(-1).astype(F32) for a in arrs])
    rows = -(-flat.shape[0] // LANES)
    rows = -(-rows // rows_multiple) * rows_multiple
    return jnp.pad(flat, (0, rows * LANES - flat.shape[0])).reshape(rows, LANES)


def _unpack_flat(buf, shapes):
    lead = buf.shape[:-2]
    flat = buf.reshape(lead + (-1,))
    out, o = [], 0
    for s in shapes:
        n = math.prod(s)
        out.append(flat[..., o:o + n].reshape(lead + tuple(s)))
        o += n
    return out


def _unshard(g):
    g = jnp.moveaxis(g, 0, -2)
    return g.reshape(g.shape[:-2] + (g.shape[-2] * g.shape[-1],))


def _my_shard(full, me, ws):
    return lax.dynamic_slice_in_dim(full, me * ws, ws, axis=full.ndim - 1)


def _silu_vjp(cvec, dsc):
    def body(c_ref, d_ref, o_ref):
        _, vjp = jax.vjp(jax.nn.silu, c_ref[...])
        o_ref[...] = vjp(d_ref[...])[0]

    return pl.pallas_call(body, name="c_ctx_bwd", out_shape=jax.ShapeDtypeStruct(cvec.shape, F32), compiler_params=_cp())(cvec, dsc)


def kernel(x, c, ctx, c_ctx, mod_w, mod_b, norm_g, ffn_w_in, ffn_w_out, ab_w_in, ab_gate_w, ab_gate_b, ab_gla_norm_g, ab_vnorm_g, ab_spatial_w, ab_spatial_b, ab_w_out, ssd_w_in, ssd_conv_w, ssd_conv_b, ssd_dt_bias, ssd_a_log, ssd_d, ssd_norm_g, ssd_w_out, final_norm_g, loss_target, m_c_ctx, m_mod_w, m_mod_b, m_norm_g, m_ffn_w_in, m_ffn_w_out, m_ab_w_in, m_ab_gate_w, m_ab_gate_b, m_ab_gla_norm_g, m_ab_vnorm_g, m_ab_spatial_w, m_ab_spatial_b, m_ab_w_out, m_ssd_w_in, m_ssd_conv_w, m_ssd_conv_b, m_ssd_dt_bias, m_ssd_a_log, m_ssd_d, m_ssd_norm_g, m_ssd_w_out, m_final_norm_g, v_c_ctx, v_mod_w, v_mod_b, v_norm_g, v_ffn_w_in, v_ffn_w_out, v_ab_w_in, v_ab_gate_w, v_ab_gate_b, v_ab_gla_norm_g, v_ab_vnorm_g, v_ab_spatial_w, v_ab_spatial_b, v_ab_w_out, v_ssd_w_in, v_ssd_conv_w, v_ssd_conv_b, v_ssd_dt_bias, v_ssd_a_log, v_ssd_d, v_ssd_norm_g, v_ssd_w_out, v_final_norm_g):
    a = dict(locals())
    me = _slot(*_mesh_pos())
    ws_mod = mod_w.shape[-1]

    fwd_small = [c] + [a[k] for k in SMALL_SHARDED]
    g_small = _ag_small("gather_small", _pack(fwd_small))
    parts = _unpack_flat(g_small, [t.shape for t in fwd_small])
    c_rows = parts[0].reshape(NDEV, D)
    full = {k: _unshard(p) for k, p in zip(SMALL_SHARDED, parts[1:])}
    c_all = jnp.concatenate([c_rows, c_ctx[None], jnp.zeros((7, D), F32)], axis=0)
    m_all = _ag_small("gather_mod", _mod_fwd(c_all, mod_w).reshape(2 * 16, ws_mod)).reshape(NDEV, 2, 16, ws_mod)
    m_mine = lax.dynamic_index_in_dim(m_all, me, axis=2, keepdims=False)
    mx = jnp.moveaxis(m_mine, 0, 1).reshape(2, N_MOD, D) + mod_b.reshape(2, N_MOD, D)
    mc = jnp.moveaxis(m_all[:, :, 8, :], 0, 1).reshape(2, N_MOD, D) + mod_b.reshape(2, N_MOD, D)
    pad2 = jnp.zeros((2, 2, D), F32)
    mod = jnp.concatenate([mx, pad2, mc, pad2], axis=1)

    big = [ffn_w_in[0], ffn_w_in[1], ffn_w_out[0], ffn_w_out[1], ab_w_in[0], ab_w_out[0], ssd_w_in[0], ssd_w_out[0]]
    gath = _ag_big("gather_weights", [w.astype(MXU_DTYPE) for w in big])
    ffn_cols = NDEV * ffn_w_in.shape[-1]
    ffn_segs = ((0, ffn_cols),)
    W = {
        "ffn_in": (_repack("repack_ffn0", gath[0], ffn_segs, ffn_cols), _repack("repack_ffn1", gath[1], ffn_segs, ffn_cols)),
        "ffn_out": (gath[2].reshape(-1, D), gath[3].reshape(-1, D)),
        "ab_in": _repack("repack_ab", gath[4], AB_SEGS, AB_P), "ab_out": gath[5].reshape(-1, D),
        "ssd_in": _repack("repack_ssd", gath[6], SSD_SEGS, SSD_P_W), "ssd_out": gath[7].reshape(-1, D),
    }
    P = {
        "norm_g": full["norm_g"], "ab_gate_w": full["ab_gate_w"][0], "ab_gate_b": full["ab_gate_b"][0],
        "ab_gla_norm_g": ab_gla_norm_g, "ab_vnorm_g": ab_vnorm_g, "ab_spatial_w": ab_spatial_w[0], "ab_spatial_b": ab_spatial_b[0],
        "ssd_conv_w": full["ssd_conv_w"][0], "ssd_conv_b": full["ssd_conv_b"], "ssd_dt_bias": ssd_dt_bias[0],
        "ssd_a_log": ssd_a_log[0], "ssd_d": ssd_d, "ssd_norm_g": full["ssd_norm_g"], "final_norm_g": final_norm_g[None],
    }

    loss, grad_x, dmod, dW, dP = _local_step(x[0], ctx[0], loss_target[0], mod, W, P)

    ws_ffn, ws_ab, ws_ssd = ffn_w_in.shape[-1], ab_w_in.shape[-1], ssd_w_in.shape[-1]
    rs_parts = [
        _unpack("unpack_ffn0", dW["ffn_in"][0], ffn_segs, ws_ffn, MXU_DTYPE), _unpack("unpack_ffn1", dW["ffn_in"][1], ffn_segs, ws_ffn, MXU_DTYPE),
        dW["ffn_out"][0].reshape(NDEV, -1, D), dW["ffn_out"][1].reshape(NDEV, -1, D),
        _unpack("unpack_ab", dW["ab_in"], AB_SEGS, ws_ab, MXU_DTYPE), dW["ab_out"].reshape(NDEV, -1, D),
        _unpack("unpack_ssd", dW["ssd_in"], SSD_SEGS, ws_ssd, MXU_DTYPE), dW["ssd_out"].reshape(NDEV, -1, D),
    ]
    recv = _rs_exchange("scatter_grads", rs_parts)

    dmx, dmc = dmod[:, 0:N_MOD].reshape(2, N_MOD * D), dmod[:, 8:8 + N_MOD].reshape(2, N_MOD * D)
    small_names = ("ab_gate_w", "ab_gate_b", "ab_gla_norm_g", "ab_vnorm_g", "ab_spatial_w", "ab_spatial_b", "norm_g", "ssd_conv_w",
                   "ssd_conv_b", "ssd_dt_bias", "ssd_a_log", "ssd_d", "ssd_norm_g", "final_norm_g")
    bwd_small = [dP[k] for k in small_names] + [dmc, dmx]
    shapes = [t.shape for t in bwd_small]
    g_bwd = _ag_small("gather_small_grads", _pack(bwd_small))
    summed = _unpack_flat(_sum_parts("sum_small_grads", g_bwd), shapes)
    gfull = dict(zip(small_names, summed[:-2]))
    dmc_sum, dmx_sum = summed[-2], summed[-1]
    dmx_all = _unpack_flat(g_bwd, shapes)[-1]
    dmx_sh = jnp.moveaxis(_my_shard(dmx_all, me, ws_mod), 0, 1)
    dm = jnp.concatenate([dmx_sh, _my_shard(dmc_sum, me, ws_mod)[:, None, :], jnp.zeros((2, 7, ws_mod), F32)], axis=1)
    d_mod_w, dsc = _mod_bwd(c_all, mod_w, dm)
    dsc_ctx = (dsc[0, 8] + dsc[1, 8])[None]
    dsc_all = _ag_small("gather_c_ctx_grad", jnp.concatenate([dsc_ctx, jnp.zeros((7, D), F32)], axis=0))
    d_c_ctx = _silu_vjp(c_ctx[None], _sum_parts("sum_c_ctx_grad", dsc_all)[0:1])[0]

    g_small_w = {
        "c_ctx": d_c_ctx, "mod_b": dmx_sum + dmc_sum, "norm_g": gfull["norm_g"], "ab_gate_w": gfull["ab_gate_w"][None],
        "ab_gate_b": gfull["ab_gate_b"][None], "ab_gla_norm_g": gfull["ab_gla_norm_g"], "ab_vnorm_g": gfull["ab_vnorm_g"],
        "ab_spatial_w": gfull["ab_spatial_w"][None], "ab_spatial_b": gfull["ab_spatial_b"][None], "ssd_conv_w": gfull["ssd_conv_w"][None],
        "ssd_conv_b": gfull["ssd_conv_b"], "ssd_dt_bias": gfull["ssd_dt_bias"][None], "ssd_a_log": gfull["ssd_a_log"][None],
        "ssd_d": gfull["ssd_d"], "ssd_norm_g": gfull["ssd_norm_g"], "final_norm_g": gfull["final_norm_g"][0],
    }
    for k in SMALL_SHARDED:
        g_small_w[k] = _my_shard(g_small_w[k], me, a[k].shape[-1])
    res = _adam("adam_small", _pack([a[k] for k in SMALL]), _pack([g_small_w[k] for k in SMALL])[None],
                _pack([a["m_" + k] for k in SMALL]), _pack([a["v_" + k] for k in SMALL]))
    out = {k: vals for k, vals in zip(SMALL, zip(*[_unpack_flat(r, [a[k].shape for k in SMALL]) for r in res]))}

    def adam_big(name, w2d, parts3d, m2d, v2d, shape):
        return tuple(r.reshape(shape) for r in _adam(name, w2d, parts3d, m2d, v2d))

    out["mod_w"] = adam_big("adam_mod_w", mod_w.reshape(-1, ws_mod), d_mod_w.reshape(1, -1, ws_mod), m_mod_w.reshape(-1, ws_mod),
                            v_mod_w.reshape(-1, ws_mod), mod_w.shape)
    for k, idx in (("ffn_w_in", (0, 1)), ("ffn_w_out", (2, 3))):
        per = [adam_big("adam_%s%d" % (k, i), a[k][i], recv[j], a["m_" + k][i], a["v_" + k][i], a[k].shape[1:]) for i, j in enumerate(idx)]
        out[k] = tuple(jnp.stack(t) for t in zip(*per))
    for k, j in (("ab_w_in", 4), ("ab_w_out", 5), ("ssd_w_in", 6), ("ssd_w_out", 7)):
        out[k] = adam_big("adam_" + k, a[k][0], recv[j], a["m_" + k][0], a["v_" + k][0], a[k].shape)

    loss_all = lax.psum(loss[0, 0], ("x", "y", "c"))
    return (loss_all, grad_x[None], *[out[k][0] for k in WEIGHTS], *[out[k][1] for k in WEIGHTS],
            *[out[k][2] for k in WEIGHTS], *[out[k][3] for k in WEIGHTS])
```

```python
import functools
import math

import jax
import jax.numpy as jnp
from jax import lax
from jax.experimental import pallas as pl
from jax.experimental.pallas import tpu as pltpu

F32 = jnp.float32
BF16 = jnp.bfloat16
MXU_DTYPE = jnp.bfloat16
HI = lax.Precision.HIGHEST

D = 1024
NDEV = 8
N_MOD = 6
EPS = 1e-6
GRID_W = 64
CTX = 256
TM = 256
D_FF = 2816
GLA_H, GLA_DK, GLA_DV, GLA_LR, GLA_TAU, GLA_L = 4, 64, 128, 16, 16.0, 64
GMLP_G, GMLP_C, GMLP_L = 4, 128, 128
SSD_H, SSD_P, SSD_G, SSD_N, SSD_L, SSD_K = 32, 64, 4, 128, 128, 5
SSD_INNER = SSD_H * SSD_P
AB_IN = 2592
SSD_IN = 5184
AB_SEGS = ((256, 768), (1056, 1568), (1568, 2080), (2080, 2592), (0, 256), (800, 1056), (768, 800))
AB_P = 2688
SSD_SEGS = ((0, 2048), (3136, 5184), (2048, 2560), (2560, 3072), (3072, 3136))
SSD_P_W = 5376
VMEM_LIMIT = 56 * 1024 * 1024

ADAM_LR, ADAM_B1, ADAM_B2, ADAM_EPS, ADAM_WD, ADAM_STEP = 0.001, 0.9, 0.999, 1e-08, 0.01, 10


def _cp(sem=None, **kw):
    return pltpu.CompilerParams(dimension_semantics=sem, vmem_limit_bytes=VMEM_LIMIT, **kw)


def _dot(a, b, dims=(((1,), (0,)), ((), ()))):
    return lax.dot_general(a.astype(MXU_DTYPE), b.astype(MXU_DTYPE), dims, preferred_element_type=F32)


def _dot_nt(a, b):
    return _dot(a, b, (((1,), (1,)), ((), ())))


def _dot_tn(a, b):
    return _dot(a, b, (((0,), (0,)), ((), ())))


def _dotx(a, b, dims=(((1,), (0,)), ((), ()))):
    return lax.dot_general(a, b, dims, precision=HI, preferred_element_type=F32)


def _rms(x):
    return x * lax.rsqrt(jnp.mean(x * x, axis=-1, keepdims=True) + EPS)


def _pick(n, prefs):
    for p in prefs:
        if n % p == 0:
            return p
    return n


def _row(arr, width=None, colblk=0, tm=TM):
    width = arr.shape[1] if width is None else width
    return (arr, pl.BlockSpec((tm, width), lambda i, c=colblk: (i, c)))


def _full_spec(p):
    nd = p.ndim
    return pl.BlockSpec(p.shape, lambda i, nd=nd: (0,) * nd)


def _rowwise(name, fn, n_blocks, ctx_blk, rows, params, outs, tm=TM):
    nr, npar = len(rows), len(params)

    def body(*refs):
        t = (pl.program_id(0) >= ctx_blk).astype(F32)
        rv = [r[...].astype(F32) for r in refs[:nr]]
        pv = [p[...] for p in refs[nr:nr + npar]]
        res = fn(t, rv, pv)
        for o_ref, o in zip(refs[nr + npar:], res):
            o_ref[...] = o.astype(o_ref.dtype)

    return pl.pallas_call(
        body, name=name, grid=(n_blocks,),
        in_specs=[s for _, s in rows] + [_full_spec(p) for p in params],
        out_specs=[pl.BlockSpec((tm, w), lambda i: (i, 0)) for w, _ in outs],
        out_shape=[jax.ShapeDtypeStruct((n_blocks * tm, w), dt) for w, dt in outs],
        compiler_params=_cp(("parallel",)),
    )(*[a for a, _ in rows], *params)


def _rowwise_vjp(name, fn, n_blocks, ctx_blk, rows, params, douts, row_grads, tm=TM):
    nr, npar, nd = len(rows), len(params), len(douts)
    adds = [a for _, _, a in row_grads if a is not None]
    na = len(adds)

    def body(*refs):
        i = pl.program_id(0)
        t = (i >= ctx_blk).astype(F32)
        rv = [r[...].astype(F32) for r in refs[:nr]]
        pv = [p[...] for p in refs[nr:nr + npar]]
        dv = [r[...].astype(F32) for r in refs[nr + npar:nr + npar + nd]]
        av = [r[...].astype(F32) for r in refs[nr + npar + nd:nr + npar + nd + na]]
        o_refs = refs[nr + npar + nd + na:]
        _, vjp = jax.vjp(lambda r, p: tuple(fn(t, r, p)), rv, pv)
        d_rows, d_params = vjp(tuple(dv))
        ai = 0
        for o_ref, (ri, _, addend) in zip(o_refs, row_grads):
            g = d_rows[ri]
            if addend is not None:
                g = g + av[ai]
                ai += 1
            o_ref[...] = g.astype(o_ref.dtype)
        p_refs = o_refs[len(row_grads):]

        @pl.when(i == 0)
        def _():
            for p_ref in p_refs:
                p_ref[...] = jnp.zeros_like(p_ref)

        for p_ref, g in zip(p_refs, d_params):
            p_ref[...] += g

    widths = [rows[ri][1].block_shape[1] for ri, _, _ in row_grads]
    res = pl.pallas_call(
        body, name=name, grid=(n_blocks,),
        in_specs=[s for _, s in rows] + [_full_spec(p) for p in params] + [s for _, s in douts] + [s for _, s in adds],
        out_specs=[pl.BlockSpec((tm, w), lambda i: (i, 0)) for w in widths] + [_full_spec(p) for p in params],
        out_shape=[jax.ShapeDtypeStruct((n_blocks * tm, w), dt) for w, (_, dt, _) in zip(widths, row_grads)]
        + [jax.ShapeDtypeStruct(p.shape, F32) for p in params],
        compiler_params=_cp(("arbitrary",)),
    )(*[a for a, _ in rows], *params, *[a for a, _ in douts], *[a for a, _ in adds])
    return res[:len(row_grads)], res[len(row_grads):]


def _mm(name, a, b, mode, out_dtype):
    if mode == "nn":
        m, kk = a.shape
        n = b.shape[1]
    elif mode == "nt":
        m, kk = a.shape
        n = b.shape[0]
    else:
        kk, m = a.shape
        n = b.shape[1]
    tm = _pick(m, (1088, 1024, 768, 512, 384, 256, 128))
    tn = _pick(n, (512, 384, 256, 128))
    if mode == "tn":
        tk = _pick(kk, (1088, 768, 512, 256))
    else:
        tk = kk if kk <= 2816 else _pick(kk, (1792, 1408, 1024, 896, 768, 512, 384, 256, 128))
    nk = kk // tk
    if mode == "nn":
        specs = [pl.BlockSpec((tm, tk), lambda i, j, k: (i, k)), pl.BlockSpec((tk, tn), lambda i, j, k: (k, j))]
        dims = (((1,), (0,)), ((), ()))
    elif mode == "nt":
        specs = [pl.BlockSpec((tm, tk), lambda i, j, k: (i, k)), pl.BlockSpec((tn, tk), lambda i, j, k: (j, k))]
        dims = (((1,), (1,)), ((), ()))
    else:
        specs = [pl.BlockSpec((tk, tm), lambda i, j, k: (k, i)), pl.BlockSpec((tk, tn), lambda i, j, k: (k, j))]
        dims = (((0,), (0,)), ((), ()))

    def body(a_ref, b_ref, o_ref, *acc):
        part = lax.dot_general(a_ref[...].astype(MXU_DTYPE), b_ref[...].astype(MXU_DTYPE), dims, preferred_element_type=F32)
        if nk == 1:
            o_ref[...] = part.astype(o_ref.dtype)
        else:
            k = pl.program_id(2)

            @pl.when(k == 0)
            def _():
                acc[0][...] = part

            @pl.when(k > 0)
            def _():
                acc[0][...] += part

            @pl.when(k == nk - 1)
            def _():
                o_ref[...] = acc[0][...].astype(o_ref.dtype)

    return pl.pallas_call(
        body, name=name, grid=(m // tm, n // tn, nk), in_specs=specs,
        out_specs=pl.BlockSpec((tm, tn), lambda i, j, k: (i, j)),
        out_shape=jax.ShapeDtypeStruct((m, n), out_dtype),
        scratch_shapes=[] if nk == 1 else [pltpu.VMEM((tm, tn), F32)],
        compiler_params=_cp(("parallel", "parallel", "arbitrary")),
    )(a, b)


def _sel_mod(modp, t):
    return modp[0:8] * (1.0 - t) + modp[8:16] * t


def _fn_prenorm(t, rows, params, *, a, b):
    (x,), (g, modp) = rows, params
    m = _sel_mod(modp, t)
    return ((_rms(x) * g) * (1.0 + m[b:b + 1]) + m[a:a + 1],)


def _fn_resid_prenorm(t, rows, params, *, gi, a, b):
    (x, y), (g, mod_a, mod_b) = rows, params
    ma, mb = _sel_mod(mod_a, t), _sel_mod(mod_b, t)
    xn = x + ma[gi:gi + 1] * y
    return xn, (_rms(xn) * g) * (1.0 + mb[b:b + 1]) + mb[a:a + 1]


def _fn_resid(t, rows, params, *, gi):
    (x, y), (mod_a,) = rows, params
    return (x + _sel_mod(mod_a, t)[gi:gi + 1] * y,)


def _fn_swiglu(t, rows, params):
    (pf,) = rows
    return (jax.nn.silu(pf[:, :D_FF]) * pf[:, D_FF:],)


def _fn_mixpost(t, rows, params):
    (o_f, o_b, r, u, g), (gla_g, vn_g, sw, sb_t) = rows, params
    o = o_f + o_b
    a = jnp.concatenate([_rms(o[:, h * GLA_DV:(h + 1) * GLA_DV]) for h in range(GLA_H)], axis=1) * gla_g * jax.nn.silu(r)
    uu, vv = jax.nn.gelu(u), jax.nn.gelu(g)
    mu = jnp.mean(vv, axis=-1, keepdims=True)
    var = jnp.mean(jnp.square(vv - mu), axis=-1, keepdims=True)
    vn = ((vv - mu) * lax.rsqrt(var + EPS)) * vn_g
    s = jnp.concatenate(
        [_dot(sw[gi * GMLP_L:(gi + 1) * GMLP_L, :], vn[:, gi * GMLP_C:(gi + 1) * GMLP_C]) + sb_t[:, gi:gi + 1]
         for gi in range(GMLP_G)], axis=1)
    return (jnp.concatenate([a, uu * s], axis=1),)


def _head_expand():
    r = lax.broadcasted_iota(jnp.int32, (SSD_H, SSD_INNER), 0)
    c = lax.broadcasted_iota(jnp.int32, (SSD_H, SSD_INNER), 1)
    return (c // SSD_P == r).astype(F32)


def _fn_ssd_finish(t, rows, params):
    (y_f, y_b, xs, z), (d_skip, norm_g) = rows, params
    d_full = _dotx(jnp.broadcast_to(d_skip, (8, SSD_H)), _head_expand())[0:1]
    y = (y_f + y_b + d_full * xs) * jax.nn.silu(z)
    gw = SSD_INNER // SSD_G
    return (jnp.concatenate([_rms(y[:, gi * gw:(gi + 1) * gw]) for gi in range(SSD_G)], axis=1) * norm_g,)


def _fn_concat(t, rows, params, *, sums, pad=0):
    out, i = [], 0
    for n in sums:
        acc = rows[i]
        for j in range(1, n):
            acc = acc + rows[i + j]
        out.append(acc)
        i += n
    if pad:
        out.append(jnp.zeros((out[0].shape[0], pad), F32))
    return (jnp.concatenate(out, axis=1),)


def _tri(n, rev):
    r = lax.broadcasted_iota(jnp.int32, (n, n), 0)
    c = lax.broadcasted_iota(jnp.int32, (n, n), 1)
    return (r <= c) if rev else (r >= c)


def _gla_chunk(S, v, k, q, tail, gw, gb, *, rev):
    L = GLA_L
    msk = _tri(L, rev)
    tri = msk.astype(F32)
    lr = tail[:, GLA_LR:2 * GLA_LR] if rev else tail[:, 0:GLA_LR]
    la = jax.nn.log_sigmoid(_dot(lr, gw) + gb) / GLA_TAU
    b = _dotx(tri, la)
    b_last = b[0:1] if rev else b[L - 1:L]
    kd = k * jnp.exp(b_last - b)
    qd = (q * GLA_DK ** -0.5) * jnp.exp(b)
    ki = k * jnp.exp(-b)
    dec = jnp.exp(_dotx(la, jnp.ones((L, GLA_DV), F32), (((0,), (0,)), ((), ()))))
    o_parts, s_parts = [], []
    for h in range(GLA_H):
        ks, vs = slice(h * GLA_DK, (h + 1) * GLA_DK), slice(h * GLA_DV, (h + 1) * GLA_DV)
        sh = S[ks, :]
        sc = jnp.where(msk, _dot_nt(qd[:, ks], ki[:, ks]), 0.0)
        o_parts.append(_dot(qd[:, ks], sh) + _dot(sc, v[:, vs]))
        s_parts.append(dec[ks, :] * sh + _dot_tn(kd[:, ks], v[:, vs]))
    return jnp.concatenate(s_parts, axis=0), jnp.concatenate(o_parts, axis=1)


def _ssd_chunk(S, x, bm, cm, tail, dtb, alog, *, rev):
    L = SSD_L
    msk = _tri(L, rev)
    tri = msk.astype(F32)
    r = lax.broadcasted_iota(jnp.int32, (L, L), 0)
    c = lax.broadcasted_iota(jnp.int32, (L, L), 1)
    eye = (r == c).astype(F32)
    raw = tail[:, SSD_H:2 * SSD_H] if rev else tail[:, 0:SSD_H]
    dt = jax.nn.softplus(raw + dtb)
    dta = dt * (-jnp.exp(alog))
    acum = _dotx(tri, dta)
    a_last = acum[0:1] if rev else acum[L - 1:L]
    wst = dt * jnp.exp(a_last - acum)
    eac = jnp.exp(acum)
    tr = _dotx(jnp.concatenate([acum, dt, wst], axis=1), eye, (((0,), (0,)), ((), ())))
    acum_t, dt_t, wst_t = tr[0:SSD_H], tr[SSD_H:2 * SSD_H], tr[2 * SSD_H:3 * SSD_H]
    decrow = jnp.exp(_dotx(jnp.broadcast_to(a_last, (8, SSD_H)), _head_expand())[0:1])
    lane = lax.broadcasted_iota(jnp.int32, (1, 2 * SSD_P), 1)
    m0 = (lane < SSD_P).astype(F32)
    m1 = 1.0 - m0
    pairs_per_group = SSD_H // SSD_G // 2
    y_parts, s_parts = [], []
    for g in range(SSD_G):
        ns = slice(g * SSD_N, (g + 1) * SSD_N)
        bg, cg = bm[:, ns], cm[:, ns]
        cb = _dot_nt(cg, bg)
        bgt = _dotx(bg, eye, (((0,), (0,)), ((), ())))
        for jj in range(pairs_per_group):
            j = g * pairs_per_group + jj
            ls = slice(j * 2 * SSD_P, (j + 1) * 2 * SSD_P)
            xp, sp = x[:, ls], S[:, ls]
            xm = jnp.concatenate([xp * m0, xp * m1], axis=0)
            sm = jnp.concatenate([sp * m0, sp * m1], axis=0)
            lhs, bw = [], []
            for h in (2 * j, 2 * j + 1):
                seg = acum[:, h:h + 1] - acum_t[h:h + 1, :]
                lhs.append(cb * jnp.exp(jnp.where(msk, seg, -jnp.inf)) * dt_t[h:h + 1, :])
                bw.append(bgt * wst_t[h:h + 1, :])
            lhs += [cg * eac[:, h:h + 1] for h in (2 * j, 2 * j + 1)]
            y_parts.append(_dot(jnp.concatenate(lhs, axis=1), jnp.concatenate([xm, sm], axis=0)))
            s_parts.append(sp * decrow[:, ls] + _dot(jnp.concatenate(bw, axis=1), xm))
    return jnp.concatenate(s_parts, axis=1), jnp.concatenate(y_parts, axis=1)


def _scan_order(n, nx, rev, backward):
    nc = n - nx

    def fwd(s):
        return (n - 1 - s) if rev else jnp.where(s < nc, s + nx, s - nc)

    return (lambda s: fwd(n - 1 - s)) if backward else fwd


def _scan_fwd(name, chunk_fn, L, n, nx, rev, rows, params, state_shape, out_w):
    order = _scan_order(n, nx, rev, False)
    nr, npar = len(rows), len(params)

    def body(*refs):
        s_scr = refs[-1]

        @pl.when(pl.program_id(0) == 0)
        def _():
            s_scr[...] = jnp.zeros_like(s_scr)

        s_in = s_scr[...]
        y_ref, st_ref = refs[nr + npar], refs[nr + npar + 1]
        st_ref[0] = s_in
        s_new, y = chunk_fn(s_in, *[r[...] for r in refs[:nr]], *[p[...] for p in refs[nr:nr + npar]], rev=rev)
        y_ref[...] = y
        s_scr[...] = s_new

    return pl.pallas_call(
        body, name=name, grid=(n,),
        in_specs=[pl.BlockSpec((L, w), lambda s, c=c: (order(s), c)) for _, w, c in rows] + [_full_spec(p) for p in params],
        out_specs=[pl.BlockSpec((L, out_w), lambda s: (order(s), 0)),
                   pl.BlockSpec((1,) + state_shape, lambda s: (order(s), 0, 0))],
        out_shape=[jax.ShapeDtypeStruct((n * L, out_w), F32), jax.ShapeDtypeStruct((n,) + state_shape, F32)],
        scratch_shapes=[pltpu.VMEM(state_shape, F32)],
        compiler_params=_cp(("arbitrary",)),
    )(*[a for a, _, _ in rows], *params)


def _scan_bwd(name, chunk_fn, L, n, nx, rev, rows, params, states, dy, state_shape, out_w):
    order = _scan_order(n, nx, rev, True)
    nr, npar = len(rows), len(params)

    def body(*refs):
        i = pl.program_id(0)
        ds_scr = refs[-1]
        rv = [r[...] for r in refs[:nr]]
        pv = [p[...] for p in refs[nr:nr + npar]]
        st_ref, dy_ref = refs[nr + npar], refs[nr + npar + 1]
        o_refs = refs[nr + npar + 2:-1]
        p_refs = o_refs[nr:]

        @pl.when(i == 0)
        def _():
            ds_scr[...] = jnp.zeros_like(ds_scr)
            for p_ref in p_refs:
                p_ref[...] = jnp.zeros_like(p_ref)

        _, vjp = jax.vjp(functools.partial(chunk_fn, rev=rev), st_ref[0], *rv, *pv)
        grads = vjp((ds_scr[...], dy_ref[...].astype(F32)))
        ds_scr[...] = grads[0]
        for o_ref, g in zip(o_refs[:nr], grads[1:1 + nr]):
            o_ref[...] = g
        for p_ref, g in zip(p_refs, grads[1 + nr:]):
            p_ref[...] += g

    res = pl.pallas_call(
        body, name=name, grid=(n,),
        in_specs=[pl.BlockSpec((L, w), lambda s, c=c: (order(s), c)) for _, w, c in rows] + [_full_spec(p) for p in params]
        + [pl.BlockSpec((1,) + state_shape, lambda s: (order(s), 0, 0)), pl.BlockSpec((L, out_w), lambda s: (order(s), 0))],
        out_specs=[pl.BlockSpec((L, w), lambda s: (order(s), 0)) for _, w, _ in rows] + [_full_spec(p) for p in params],
        out_shape=[jax.ShapeDtypeStruct((n * L, w), F32) for _, w, _ in rows] + [jax.ShapeDtypeStruct(p.shape, F32) for p in params],
        scratch_shapes=[pltpu.VMEM(state_shape, F32)],
        compiler_params=_cp(("arbitrary",)),
    )(*[a for a, _, _ in rows], *params, states, dy)
    return res[:nr], res[nr:]


CONV_W = 1024
CONV_COLBLK = (0, 1, 4)


def _conv_specs(nb, src_blk):
    halo = TM // 8
    return [pl.BlockSpec((TM, CONV_W), lambda j, i: (i, src_blk(j))),
            pl.BlockSpec((8, CONV_W), lambda j, i: (jnp.maximum(i * halo - 1, 0), src_blk(j))),
            pl.BlockSpec((8, CONV_W), lambda j, i: (jnp.minimum(i * halo + halo, nb * halo - 1), src_blk(j)))]


def _conv_ext(i, nb, cur, prev, nxt):
    has_prev = jnp.logical_and(i > 0, i < nb - 1)
    has_next = i < nb - 2
    return jnp.concatenate([jnp.where(has_prev, prev, 0.0), cur, jnp.where(has_next, nxt, 0.0)], axis=0)


def _conv_taps(ext, w, flip):
    acc = None
    for j in range(SSD_K):
        wj = w[SSD_K - 1 - j:SSD_K - j, :] if flip else w[j:j + 1, :]
        term = wj * ext[6 + j:6 + j + TM, :]
        acc = term if acc is None else acc + term
    return acc


def _conv(name, src, w8, b1, nb, *, permuted_src, act, flip, out_dtype):
    src_blk = (lambda j: jnp.where(j == 2, CONV_COLBLK[2], j)) if permuted_src else (lambda j: j)

    def body(cur, prev, nxt, w_ref, b_ref, o_ref):
        ext = _conv_ext(pl.program_id(1), nb, cur[...].astype(F32), prev[...].astype(F32), nxt[...].astype(F32))
        acc = _conv_taps(ext, w_ref[...], flip)
        if act:
            acc = jax.nn.silu(acc + b_ref[...])
        o_ref[...] = acc.astype(o_ref.dtype)

    return pl.pallas_call(
        body, name=name, grid=(3, nb),
        in_specs=_conv_specs(nb, src_blk) + [pl.BlockSpec((8, CONV_W), lambda j, i: (0, j)), pl.BlockSpec((1, CONV_W), lambda j, i: (0, j))],
        out_specs=pl.BlockSpec((TM, CONV_W), lambda j, i: (i, j)),
        out_shape=jax.ShapeDtypeStruct((nb * TM, 3 * CONV_W), out_dtype),
        compiler_params=_cp(("parallel", "parallel")),
    )(src, src, src, w8, b1)


def _conv_bwd_pre(name, p1, w8, b1, dxbc_parts, nb):
    src_blk = lambda j: jnp.where(j == 2, CONV_COLBLK[2], j)
    xs_parts, bc_parts = dxbc_parts
    n_x, n_bc = len(xs_parts), len(bc_parts)

    def body(*refs):
        cur, prev, nxt, w_ref, b_ref = refs[:5]
        d_refs = refs[5:5 + n_x + n_bc]
        da_ref, dw_ref, db_ref = refs[5 + n_x + n_bc:]
        j, i = pl.program_id(0), pl.program_id(1)
        ext = _conv_ext(i, nb, cur[...], prev[...], nxt[...])
        acc = _conv_taps(ext, w_ref[...], False) + b_ref[...]
        dx = d_refs[0][...]
        for r in d_refs[1:n_x]:
            dx = dx + r[...]
        dbc = jnp.concatenate([d_refs[n_x][...] + d_refs[n_x + 1][...], d_refs[n_x + 2][...] + d_refs[n_x + 3][...]], axis=1)
        dy = jnp.where(j == 2, dbc, dx)
        sg = jax.nn.sigmoid(acc)
        da = dy * (sg + acc * sg * (1.0 - sg))
        da_ref[...] = da

        @pl.when(i == 0)
        def _():
            dw_ref[...] = jnp.zeros_like(dw_ref)
            db_ref[...] = jnp.zeros_like(db_ref)

        rows = [jnp.sum(da * ext[6 + t:6 + t + TM, :], axis=0, keepdims=True) for t in range(SSD_K)]
        dw_ref[...] += jnp.concatenate(rows + [jnp.zeros((8 - SSD_K, CONV_W), F32)], axis=0)
        db_ref[...] += jnp.sum(da, axis=0, keepdims=True)

    x_specs = [pl.BlockSpec((TM, CONV_W), lambda j, i: (i, jnp.minimum(j, 1))) for _ in xs_parts]
    bc_specs = [pl.BlockSpec((TM, 512), lambda j, i: (i, 0)) for _ in bc_parts]
    return pl.pallas_call(
        body, name=name, grid=(3, nb),
        in_specs=_conv_specs(nb, src_blk) + [pl.BlockSpec((8, CONV_W), lambda j, i: (0, j)), pl.BlockSpec((1, CONV_W), lambda j, i: (0, j))]
        + x_specs + bc_specs,
        out_specs=[pl.BlockSpec((TM, CONV_W), lambda j, i: (i, j)), pl.BlockSpec((8, CONV_W), lambda j, i: (0, j)),
                   pl.BlockSpec((1, CONV_W), lambda j, i: (0, j))],
        out_shape=[jax.ShapeDtypeStruct((nb * TM, 3 * CONV_W), F32), jax.ShapeDtypeStruct((8, 3 * CONV_W), F32),
                   jax.ShapeDtypeStruct((1, 3 * CONV_W), F32)],
        compiler_params=_cp(("arbitrary", "arbitrary")),
    )(p1, p1, p1, w8, b1, *xs_parts, *bc_parts)


def _grid_rows(val, a, kb):
    return jnp.concatenate([val[:, t * D:(t + 1) * D] for t in range(kb)], axis=0)


def _perm(name, xc, a, nb):
    n = xc.shape[0]
    b = (n - CTX) // a
    kb = TM // a
    view = xc.reshape(n // b, b * D)

    def body(v_ref, c_ref, o_ref):
        i = pl.program_id(0)

        @pl.when(i < nb - 1)
        def _():
            o_ref[...] = _grid_rows(v_ref[...], a, kb)

        @pl.when(i == nb - 1)
        def _():
            o_ref[...] = c_ref[...]

    return pl.pallas_call(
        body, name=name, grid=(nb,),
        in_specs=[pl.BlockSpec((a, kb * D), lambda i: (0, jnp.minimum(i, nb - 2))), pl.BlockSpec((TM, D), lambda i: (nb - 1, 0))],
        out_specs=pl.BlockSpec((TM, D), lambda i: (i, 0)),
        out_shape=jax.ShapeDtypeStruct((n, D), xc.dtype),
        compiler_params=_cp(("parallel",)),
    )(view, xc)


def _loss_head(x, f, target, modp, g_final, nb, rows_r):
    t_tok = target.shape[0]
    kb = TM // rows_r
    tview = target.reshape(rows_r, (t_tok // rows_r) * D)

    def fn(x_, f_, tgt, modp_, g_, is_ctx):
        xn = x_ + _sel_mod(modp_, is_ctx)[5:6] * f_
        err = _rms(xn) * g_ - tgt
        return 0.5 * jnp.sum(jnp.mean(err * err, axis=-1)) * (1.0 - is_ctx)

    def body(x_ref, f_ref, t_ref, m_ref, g_ref, l_ref, dx_ref, df_ref, dm_ref, dg_ref):
        i = pl.program_id(0)
        is_ctx = (i == nb - 1).astype(F32)
        tgt = _grid_rows(t_ref[...], rows_r, kb)
        l, vjp = jax.vjp(lambda a_, b_, c_, d_: fn(a_, b_, tgt, c_, d_, is_ctx), x_ref[...], f_ref[...], m_ref[...], g_ref[...])
        dx, df, dm, dg = vjp(jnp.ones((), F32))

        @pl.when(i == 0)
        def _():
            l_ref[...] = jnp.zeros_like(l_ref)
            dm_ref[...] = jnp.zeros_like(dm_ref)
            dg_ref[...] = jnp.zeros_like(dg_ref)

        l_ref[...] += jnp.reshape(l, (1, 1))
        dx_ref[...] = dx
        df_ref[...] = df.astype(df_ref.dtype)
        dm_ref[...] += dm
        dg_ref[...] += dg

    rowspec = pl.BlockSpec((TM, D), lambda i: (i, 0))
    return pl.pallas_call(
        body, name="loss_head", grid=(nb,),
        in_specs=[rowspec, rowspec, pl.BlockSpec((rows_r, kb * D), lambda i: (0, jnp.minimum(i, nb - 2))),
                  _full_spec(modp), _full_spec(g_final)],
        out_specs=[pl.BlockSpec((1, 1), lambda i: (0, 0)), rowspec, rowspec, _full_spec(modp), _full_spec(g_final)],
        out_shape=[jax.ShapeDtypeStruct((1, 1), F32), jax.ShapeDtypeStruct(x.shape, F32), jax.ShapeDtypeStruct(x.shape, MXU_DTYPE),
                   jax.ShapeDtypeStruct(modp.shape, F32), jax.ShapeDtypeStruct(g_final.shape, F32)],
        compiler_params=_cp(("arbitrary",)),
    )(x, f, tview, modp, g_final)


def _repack(name, shards, segs, wp):
    nd, kk, ws = shards.shape
    tr = 128
    used = sum(e - s for s, e in segs)

    def body(a_ref, o_ref):
        full = jnp.concatenate([a_ref[d].astype(F32) for d in range(nd)], axis=1)
        parts = [full[:, s:e] for s, e in segs]
        if wp > used:
            parts.append(jnp.zeros((tr, wp - used), F32))
        o_ref[...] = jnp.concatenate(parts, axis=1).astype(o_ref.dtype)

    return pl.pallas_call(
        body, name=name, grid=(kk // tr,),
        in_specs=[pl.BlockSpec((nd, tr, ws), lambda i: (0, i, 0))],
        out_specs=pl.BlockSpec((tr, wp), lambda i: (i, 0)),
        out_shape=jax.ShapeDtypeStruct((kk, wp), MXU_DTYPE),
        compiler_params=_cp(("parallel",)),
    )(shards)


def _unpack(name, dw, segs, ws, out_dtype):
    kk, wp = dw.shape
    tr = 128
    order = sorted(range(len(segs)), key=lambda i: segs[i][0])
    offs, o = [], 0
    for s, e in segs:
        offs.append(o)
        o += e - s

    def body(a_ref, o_ref):
        a = a_ref[...].astype(F32)
        full = jnp.concatenate([a[:, offs[i]:offs[i] + segs[i][1] - segs[i][0]] for i in order], axis=1)
        for d in range(NDEV):
            o_ref[d] = full[:, d * ws:(d + 1) * ws].astype(o_ref.dtype)

    return pl.pallas_call(
        body, name=name, grid=(kk // tr,),
        in_specs=[pl.BlockSpec((tr, wp), lambda i: (i, 0))],
        out_specs=pl.BlockSpec((NDEV, tr, ws), lambda i: (0, i, 0)),
        out_shape=jax.ShapeDtypeStruct((NDEV, kk, ws), out_dtype),
        compiler_params=_cp(("parallel",)),
    )(dw)


def _adam_math(w, g, m, v):
    m = ADAM_B1 * m + (1.0 - ADAM_B1) * g
    v = ADAM_B2 * v + (1.0 - ADAM_B2) * jnp.square(g)
    m_hat = m / (1.0 - ADAM_B1 ** ADAM_STEP)
    v_hat = v / (1.0 - ADAM_B2 ** ADAM_STEP)
    delta = -ADAM_LR * (m_hat / (jnp.sqrt(v_hat) + ADAM_EPS) + ADAM_WD * w)
    return delta, m, v


def _adam(name, w, parts, m, v):
    r, c = w.shape
    npart = parts.shape[0]
    tr = _pick(r, (256, 128, 64, 32, 16, 8)) if r * c * 4 > (1 << 20) else r

    def body(w_ref, p_ref, m_ref, v_ref, g_ref, d_ref, nm_ref, nv_ref):
        g = p_ref[0].astype(F32)
        for s in range(1, npart):
            g = g + p_ref[s].astype(F32)
        delta, nm, nv = _adam_math(w_ref[...], g, m_ref[...], v_ref[...])
        g_ref[...], d_ref[...], nm_ref[...], nv_ref[...] = g, delta, nm, nv

    spec = pl.BlockSpec((tr, c), lambda i: (i, 0))
    return pl.pallas_call(
        body, name=name, grid=(r // tr,),
        in_specs=[spec, pl.BlockSpec((npart, tr, c), lambda i: (0, i, 0)), spec, spec],
        out_specs=[spec] * 4, out_shape=[jax.ShapeDtypeStruct((r, c), F32)] * 4,
        compiler_params=_cp(("parallel",)),
    )(w, parts, m, v)


def _mod_fwd(c_all, mod_w):
    nl, _, ws = mod_w.shape

    def body(c_ref, w_ref, o_ref):
        o_ref[0] = _dot(jax.nn.silu(c_ref[...]), w_ref[0])

    return pl.pallas_call(
        body, name="mod_fwd", grid=(nl,),
        in_specs=[_full_spec(c_all), pl.BlockSpec((1, D, ws), lambda i: (i, 0, 0))],
        out_specs=pl.BlockSpec((1, 16, ws), lambda i: (i, 0, 0)),
        out_shape=jax.ShapeDtypeStruct((nl, 16, ws), F32),
        compiler_params=_cp(("parallel",)),
    )(c_all, mod_w)


def _mod_bwd(c_all, mod_w, dm):
    nl, _, ws = mod_w.shape

    def body(c_ref, w_ref, d_ref, dw_ref, dc_ref):
        dw_ref[0] = _dot_tn(jax.nn.silu(c_ref[...]), d_ref[0])
        dc_ref[0] = _dot_nt(d_ref[0], w_ref[0])

    return pl.pallas_call(
        body, name="mod_bwd", grid=(nl,),
        in_specs=[_full_spec(c_all), pl.BlockSpec((1, D, ws), lambda i: (i, 0, 0)), pl.BlockSpec((1, 16, ws), lambda i: (i, 0, 0))],
        out_specs=[pl.BlockSpec((1, D, ws), lambda i: (i, 0, 0)), pl.BlockSpec((1, 16, D), lambda i: (i, 0, 0))],
        out_shape=[jax.ShapeDtypeStruct((nl, D, ws), F32), jax.ShapeDtypeStruct((nl, 16, D), F32)],
        compiler_params=_cp(("parallel",)),
    )(c_all, mod_w, dm)


def _sum_parts(name, parts):
    npart, r, c = parts.shape

    def body(p_ref, o_ref):
        g = p_ref[0].astype(F32)
        for s in range(1, npart):
            g = g + p_ref[s].astype(F32)
        o_ref[...] = g

    return pl.pallas_call(body, name=name, out_shape=jax.ShapeDtypeStruct((r, c), F32), compiler_params=_cp())(parts)


MESH = pl.DeviceIdType.MESH
ANY = pl.BlockSpec(memory_space=pl.ANY)
N_PEERS = NDEV - 1


def _mesh_pos():
    return lax.axis_index("x"), lax.axis_index("y"), lax.axis_index("c")


def _slot(px, py, pc):
    return 4 * px + 2 * py + pc


def _two_level_gather(x_refs, o_refs, send_sems, recv_sems, local_sems):
    x, y, c = _mesh_pos()
    me, sibling = (x, y, c), (x, y, 1 - c)
    chips = [(1 - x, y), (x, 1 - y), (1 - x, 1 - y)]
    n = len(x_refs)

    def copy(a, k, block, to, src=None):
        dst = o_refs[a].at[_slot(*block)]
        return pltpu.make_async_remote_copy(src_ref=dst if src is None else src, dst_ref=dst, send_sem=send_sems.at[a, k],
                                            recv_sem=recv_sems.at[a, k], device_id=to, device_id_type=MESH)

    mine = [pltpu.make_async_copy(x_refs[a], o_refs[a].at[_slot(*me)], local_sems.at[a]) for a in range(n)]
    for cp in mine:
        cp.start()
    first = []
    for a in range(n):
        first.append(copy(a, 0, me, sibling, src=x_refs[a]))
        first += [copy(a, 1 + j, me, (*chip, c), src=x_refs[a]) for j, chip in enumerate(chips)]
    for cp in first:
        cp.start()
    passed = []
    for j, chip in enumerate(chips):
        for a in range(n):
            copy(a, 1 + j, (*chip, c), me).wait_recv()
            fwd = copy(a, 4 + j, (*chip, c), sibling)
            fwd.start()
            passed.append(fwd)
    for a in range(n):
        copy(a, 0, sibling, me).wait_recv()
        for j, chip in enumerate(chips):
            copy(a, 4 + j, (*chip, 1 - c), me).wait_recv()
    for cp in first + passed:
        cp.wait_send()
    for cp in mine:
        cp.wait()


def _ag_small(name, x):
    r, c = x.shape

    def body(x_ref, o_ref, send_sems, recv_sems, local_sems):
        _two_level_gather([x_ref], [o_ref], send_sems, recv_sems, local_sems)

    return pl.pallas_call(
        body, name=name, out_shape=jax.ShapeDtypeStruct((NDEV, r, c), x.dtype),
        in_specs=[pl.BlockSpec(memory_space=pltpu.VMEM)], out_specs=pl.BlockSpec(memory_space=pltpu.VMEM),
        scratch_shapes=[pltpu.SemaphoreType.DMA((1, N_PEERS)), pltpu.SemaphoreType.DMA((1, N_PEERS)), pltpu.SemaphoreType.DMA((1,))],
        compiler_params=pltpu.CompilerParams(vmem_limit_bytes=VMEM_LIMIT),
    )(x)


def _ag_big(name, shards):
    n = len(shards)

    def body(*refs):
        _two_level_gather(refs[:n], refs[n:2 * n], *refs[2 * n:])

    return pl.pallas_call(
        body, name=name, out_shape=[jax.ShapeDtypeStruct((NDEV,) + s.shape, s.dtype) for s in shards],
        in_specs=[ANY] * n, out_specs=[ANY] * n,
        scratch_shapes=[pltpu.SemaphoreType.DMA((n, N_PEERS)), pltpu.SemaphoreType.DMA((n, N_PEERS)), pltpu.SemaphoreType.DMA((n,))],
    )(*shards)


HBM = pl.BlockSpec(memory_space=pltpu.HBM)
SEM = pl.BlockSpec(memory_space=pltpu.SEMAPHORE)
EFFECT = pltpu.SideEffectType.DATAFLOW_SIDE_EFFECTING


def _peers(x, y, c):
    return [(k - 1, ((1 - x) if k & 4 else x, (1 - y) if k & 2 else y, (1 - c) if k & 1 else c)) for k in range(1, NDEV)]


def _xchg_copy(src_refs, land_refs, send_sems, recv_sems, a, k, peer, me, scatter):
    src = src_refs[a].at[_slot(*peer)] if scatter else src_refs[a]
    return pltpu.make_async_remote_copy(src_ref=src, dst_ref=land_refs[a].at[me], send_sem=send_sems.at[a * N_PEERS + k],
                                        recv_sem=recv_sems.at[a * N_PEERS + k], device_id=peer, device_id_type=MESH)


def _xchg_start(name, srcs, lands, dep, scatter):
    n = len(srcs)

    def body(*refs):
        src_refs, land_refs = refs[:n], refs[n:2 * n]
        send_sems, recv_sems, token = refs[2 * n + 1], refs[2 * n + 2], refs[-1]
        x, y, c = _mesh_pos()
        me = _slot(x, y, c)
        for k, peer in _peers(x, y, c):
            for a in range(n):
                _xchg_copy(src_refs, land_refs, send_sems, recv_sems, a, k, peer, me, scatter).start()
        token[...] = jnp.zeros_like(token)

    res = pl.pallas_call(
        body, name=name,
        out_shape=(pltpu.SemaphoreType.DMA((n * N_PEERS,)), pltpu.SemaphoreType.DMA((n * N_PEERS,)),
                   *[pltpu.HBM(s.shape, s.dtype) for s in srcs], *[pltpu.HBM(s.shape, s.dtype) for s in lands],
                   jax.ShapeDtypeStruct((8, 128), F32)),
        in_specs=[HBM] * (2 * n) + [ANY],
        out_specs=(SEM, SEM, *([HBM] * (2 * n)), pl.BlockSpec(memory_space=pltpu.VMEM)),
        input_output_aliases={i: 2 + i for i in range(2 * n)},
        compiler_params=pltpu.CompilerParams(has_side_effects=EFFECT),
    )(*[pltpu.with_memory_space_constraint(s, pltpu.HBM) for s in srcs],
      *[pltpu.with_memory_space_constraint(s, pltpu.HBM) for s in lands], dep)
    return res[0], res[1], res[2:2 + n], res[2 + n:2 + 2 * n], res[-1]


def _xchg_wait(name, send_sems, recv_sems, srcs, lands, after, scatter):
    n = len(srcs)

    def body(*refs):
        src_refs, land_refs = refs[:n], refs[n:2 * n]
        s_sems, r_sems = refs[2 * n], refs[2 * n + 1]
        x, y, c = _mesh_pos()
        me = _slot(x, y, c)
        for k, peer in _peers(x, y, c):
            for a in range(n):
                cp = _xchg_copy(src_refs, land_refs, s_sems, r_sems, a, k, peer, me, scatter)
                cp.wait_send()
                cp.wait_recv()

    res = pl.pallas_call(
        body, name=name,
        out_shape=[pltpu.HBM(s.shape, s.dtype) for s in srcs] + [pltpu.HBM(s.shape, s.dtype) for s in lands],
        in_specs=[HBM] * (2 * n) + [SEM, SEM, ANY], out_specs=[HBM] * (2 * n),
        input_output_aliases={i: i for i in range(2 * n)},
        compiler_params=pltpu.CompilerParams(has_side_effects=EFFECT),
    )(*srcs, *lands, send_sems, recv_sems, after)
    return res[n:]


def _landing(own, me):
    return lax.dynamic_update_slice_in_dim(lax.empty((NDEV,) + own.shape, own.dtype), own[None], me, axis=0)


STAGES = ("l0_mixer", "l0_ffn", "l1_mixer", "l1_ffn")
STAGE_LAYOUT = {"l0_mixer": (AB_SEGS, AB_P), "l1_mixer": (SSD_SEGS, SSD_P_W)}


class _Exchange:
    def __init__(self, shards, me):
        self.shards, self.me = shards, me
        self.pending, self.pending_grads, self.recv = None, None, {}

    def _layout(self, stage):
        ws = self.shards[stage][0].shape[-1]
        return STAGE_LAYOUT.get(stage, (((0, NDEV * ws),), NDEV * ws)) + (ws,)

    def _start_gather(self, stage, dep):
        srcs = list(self.shards[stage])
        lands = [_landing(s, self.me) for s in srcs]
        return _xchg_start("gather_start_" + stage, srcs, lands, dep, False)

    def get(self, stage, dep, thread):
        i = STAGES.index(stage)
        if i == 0:
            g_in, g_out = _ag_big("gather_" + stage, list(self.shards[stage]))
        else:
            ss, rs, srcs, lands, _ = self.pending
            g_in, g_out = _xchg_wait("gather_wait_" + stage, ss, rs, srcs, lands, dep, False)
            self.pending = None
        if i + 1 < len(STAGES):
            self.pending = self._start_gather(STAGES[i + 1], g_out)
            thread = thread + self.pending[4][0, 0]
        segs, wp, _ = self._layout(stage)
        return _repack("repack_" + stage, g_in, segs, wp), g_out.reshape(-1, D), thread

    def put(self, stage, d_in, d_out, thread):
        segs, _, ws = self._layout(stage)
        parts = [_unpack("unpack_" + stage, d_in, segs, ws, MXU_DTYPE), d_out.reshape(NDEV, -1, D)]
        dep = parts[0]
        if self.pending_grads is not None:
            dep = self.finish(parts[0])[0]
        lands = [_landing(lax.dynamic_index_in_dim(p, self.me, 0, keepdims=False), self.me) for p in parts]
        self.pending_grads = (stage,) + _xchg_start("scatter_start_" + stage, parts, lands, dep, True)
        return thread + self.pending_grads[5][0, 0]

    def finish(self, after):
        stage, ss, rs, srcs, lands, _ = self.pending_grads
        self.recv[stage] = _xchg_wait("scatter_wait_" + stage, ss, rs, srcs, lands, after, True)
        self.pending_grads = None
        return self.recv[stage]


def _ffn_fwd(tag, h, w_in, w_out, nb, cb):
    pf = _mm(tag + "_ffn_in", h, w_in, "nn", F32)
    (act,) = _rowwise(tag + "_swiglu", _fn_swiglu, nb, cb, [_row(pf)], [], [(D_FF, MXU_DTYPE)])
    return pf, act, _mm(tag + "_ffn_out", act, w_out, "nn", F32)


def _ffn_bwd(tag, h, pf, act, df, w_in, w_out, nb, cb):
    dw_out = _mm(tag + "_ffn_out_dw", act, df, "tn", MXU_DTYPE)
    dact = _mm(tag + "_ffn_out_dx", df, w_out, "nt", MXU_DTYPE)
    (dpf,), _ = _rowwise_vjp(tag + "_swiglu_bwd", _fn_swiglu, nb, cb, [_row(pf)], [], [_row(dact)], [(0, MXU_DTYPE, None)])
    dw_in = _mm(tag + "_ffn_in_dw", h, dpf, "tn", MXU_DTYPE)
    dh = _mm(tag + "_ffn_in_dx", dpf, w_in, "nt", MXU_DTYPE)
    return dw_out, dw_in, dh


def _local_step(x, ctx, target, mod, P, comm):
    T = x.shape[0]
    N = T + CTX
    nb, cb = N // TM, N // TM - 1
    R = T // GRID_W
    mod0, mod1 = mod[0], mod[1]
    ng = P["norm_g"]
    g00, g01, g10, g11 = ng[0, 0][None], ng[0, 1][None], ng[1, 0][None], ng[1, 1][None]
    pre = functools.partial(_fn_prenorm, a=0, b=1)
    rpre = functools.partial(_fn_resid_prenorm, gi=2, a=3, b=4)
    res5 = functools.partial(_fn_resid, gi=5)
    dirs = (("f", False), ("b", True))

    xc0 = jnp.concatenate([x, ctx], axis=0)
    w_ab_in, w_ab_out, g00 = comm.get("l0_mixer", None, g00)
    (h0,) = _rowwise("l0_prenorm", pre, nb, cb, [_row(xc0)], [g00, mod0], [(D, MXU_DTYPE)])
    p0 = _mm("l0_in", h0, w_ab_in, "nn", F32)
    gla_rows = [(p0, 512, 0), (p0, 256, 8), (p0, 256, 9), (p0, 128, 20)]
    n64, nx64 = N // GLA_L, T // GLA_L
    gla_par = {d: [P["ab_gate_w"][int(r)], P["ab_gate_b"][int(r)][None]] for d, r in dirs}
    gla_state = (GLA_H * GLA_DK, GLA_DV)
    o, st0 = {}, {}
    for d, rev in dirs:
        o[d], st0[d] = _scan_fwd("gla_fwd_" + d, _gla_chunk, GLA_L, n64, nx64, rev, gla_rows, gla_par[d], gla_state, GLA_H * GLA_DV)
    n128, cb128 = N // GMLP_L, T // GMLP_L
    mix_rows = [_row(o["f"], tm=GMLP_L), _row(o["b"], tm=GMLP_L)] + [_row(p0, 512, j, tm=GMLP_L) for j in (1, 2, 3)]
    mix_par = [P["ab_gla_norm_g"], P["ab_vnorm_g"], P["ab_spatial_w"].reshape(GMLP_G * GMLP_L, GMLP_L), P["ab_spatial_b"].T]
    (cat0,) = _rowwise("l0_mix", _fn_mixpost, n128, cb128, mix_rows, mix_par, [(D, MXU_DTYPE)], tm=GMLP_L)
    y0 = _mm("l0_out", cat0, w_ab_out, "nn", F32)
    w_fi0, w_fo0, g01 = comm.get("l0_ffn", y0, g01)
    x1, h1 = _rowwise("l0_ffn_prenorm", rpre, nb, cb, [_row(xc0), _row(y0)], [g01, mod0, mod0], [(D, F32), (D, MXU_DTYPE)])
    pf0, act0, f0 = _ffn_fwd("l0", h1, w_fi0, w_fo0, nb, cb)
    (x2,) = _rowwise("l0_resid", res5, nb, cb, [_row(x1), _row(f0)], [mod0], [(D, F32)])
    x2p = _perm("to_col_major", x2, R, nb)

    w_ssd_in, w_ssd_out, g10 = comm.get("l1_mixer", x2p, g10)
    (h2,) = _rowwise("l1_prenorm", pre, nb, cb, [_row(x2p)], [g10, mod1], [(D, MXU_DTYPE)])
    p1 = _mm("l1_in", h2, w_ssd_in, "nn", F32)
    conv_w8 = jnp.concatenate([P["ssd_conv_w"], jnp.zeros((8 - SSD_K, 3 * CONV_W), F32)], axis=0)
    xbc = _conv("l1_conv", p1, conv_w8, P["ssd_conv_b"], nb, permuted_src=True, act=True, flip=False, out_dtype=F32)
    ssd_rows = [(xbc, SSD_INNER, 0), (xbc, 512, 4), (xbc, 512, 5), (p1, 128, 40)]
    ns, nxs = N // SSD_L, T // SSD_L
    ssd_par = {d: [P["ssd_dt_bias"][int(r)][None], P["ssd_a_log"][int(r)][None]] for d, r in dirs}
    ssd_state = (SSD_N, SSD_INNER)
    ys, st1 = {}, {}
    for d, rev in dirs:
        ys[d], st1[d] = _scan_fwd("ssd_fwd_" + d, _ssd_chunk, SSD_L, ns, nxs, rev, ssd_rows, ssd_par[d], ssd_state, SSD_INNER)
    fin_rows = [_row(ys["f"]), _row(ys["b"]), _row(xbc, SSD_INNER, 0), _row(p1, SSD_INNER, 1)]
    fin_par = [P["ssd_d"], P["ssd_norm_g"]]
    (yn,) = _rowwise("l1_finish", _fn_ssd_finish, nb, cb, fin_rows, fin_par, [(SSD_INNER, MXU_DTYPE)])
    y1 = _mm("l1_out", yn, w_ssd_out, "nn", F32)
    w_fi1, w_fo1, g11 = comm.get("l1_ffn", y1, g11)
    x3, h3 = _rowwise("l1_ffn_prenorm", rpre, nb, cb, [_row(x2p), _row(y1)], [g11, mod1, mod1], [(D, F32), (D, MXU_DTYPE)])
    pf1, act1, f1 = _ffn_fwd("l1", h3, w_fi1, w_fo1, nb, cb)
    loss, dx3, df1, dm1_j, d_final_g = _loss_head(x3, f1, target, mod1, P["final_norm_g"], nb, R)

    dP = {"final_norm_g": d_final_g}
    dwo1, dwi1, dh3 = _ffn_bwd("l1", h3, pf1, act1, df1, w_fi1, w_fo1, nb, cb)
    g11 = comm.put("l1_ffn", dwi1, dwo1, g11)
    (dx2p_a, dy1), (dg11, dm1_a, dm1_b) = _rowwise_vjp(
        "l1_ffn_prenorm_bwd", rpre, nb, cb, [_row(x2p), _row(y1)], [g11, mod1, mod1], [_row(dx3), _row(dh3)],
        [(0, F32, None), (1, MXU_DTYPE, None)])
    d_ssd_out = _mm("l1_out_dw", yn, dy1, "tn", MXU_DTYPE)
    dyn = _mm("l1_out_dx", dy1, w_ssd_out, "nt", MXU_DTYPE)
    (dys, dxs, dz), (dP["ssd_d"], dP["ssd_norm_g"]) = _rowwise_vjp(
        "l1_finish_bwd", _fn_ssd_finish, nb, cb, fin_rows, fin_par, [_row(dyn)],
        [(0, F32, None), (2, F32, None), (3, MXU_DTYPE, None)])
    dxp, dbp, dcp, dtl, ddtb, dalog = [], [], [], [], [], []
    for d, rev in dirs:
        (dx_, db_, dc_, dt_), (ddtb_, dalog_) = _scan_bwd("ssd_bwd_" + d, _ssd_chunk, SSD_L, ns, nxs, rev, ssd_rows, ssd_par[d],
                                                          st1[d], dys, ssd_state, SSD_INNER)
        dxp.append(dx_); dbp.append(db_); dcp.append(dc_); dtl.append(dt_); ddtb.append(ddtb_); dalog.append(dalog_)
    dP["ssd_dt_bias"] = jnp.concatenate(ddtb, axis=0)
    dP["ssd_a_log"] = jnp.concatenate(dalog, axis=0)
    dacc, dcw8, dP["ssd_conv_b"] = _conv_bwd_pre("l1_conv_bwd", p1, conv_w8, P["ssd_conv_b"],
                                                  (dxp + [dxs], [dbp[0], dbp[1], dcp[0], dcp[1]]), nb)
    dP["ssd_conv_w"] = dcw8[:SSD_K]
    dpc = _conv("l1_conv_dx", dacc, conv_w8, jnp.zeros((1, 3 * CONV_W), F32), nb, permuted_src=False, act=False, flip=True,
                out_dtype=MXU_DTYPE)
    cat1 = functools.partial(_fn_concat, sums=(1, 1, 1, 2), pad=SSD_P_W - 5248)
    (dp1,) = _rowwise("l1_dp", cat1, nb, cb, [_row(dpc, SSD_INNER, 0), _row(dz), _row(dpc, 1024, 2), _row(dtl[0]), _row(dtl[1])],
                      [], [(SSD_P_W, MXU_DTYPE)])
    g10 = comm.put("l1_mixer", _mm("l1_in_dw", h2, dp1, "tn", F32), d_ssd_out, g10)
    dh2 = _mm("l1_in_dx", dp1, w_ssd_in, "nt", MXU_DTYPE)
    (dx2p,), (dg10, dm1_f) = _rowwise_vjp("l1_prenorm_bwd", pre, nb, cb, [_row(x2p)], [g10, mod1], [_row(dh2)],
                                          [(0, F32, _row(dx2p_a))])
    dx2 = _perm("to_row_major", dx2p, GRID_W, nb)

    (dx1_a, df0), (dm0_e,) = _rowwise_vjp("l0_resid_bwd", res5, nb, cb, [_row(x1), _row(f0)], [mod0], [_row(dx2)],
                                          [(0, F32, None), (1, MXU_DTYPE, None)])
    dwo0, dwi0, dh1 = _ffn_bwd("l0", h1, pf0, act0, df0, w_fi0, w_fo0, nb, cb)
    g01 = comm.put("l0_ffn", dwi0, dwo0, g01)
    (dxc0_a, dy0), (dg01, dm0_a, dm0_b) = _rowwise_vjp(
        "l0_ffn_prenorm_bwd", rpre, nb, cb, [_row(xc0), _row(y0)], [g01, mod0, mod0], [_row(dx1_a), _row(dh1)],
        [(0, F32, None), (1, MXU_DTYPE, None)])
    d_ab_out = _mm("l0_out_dw", cat0, dy0, "tn", MXU_DTYPE)
    dcat0 = _mm("l0_out_dx", dy0, w_ab_out, "nt", MXU_DTYPE)
    (do, dr, du, dgm), (dP["ab_gla_norm_g"], dP["ab_vnorm_g"], dsw, dsb_t) = _rowwise_vjp(
        "l0_mix_bwd", _fn_mixpost, n128, cb128, mix_rows, mix_par, [_row(dcat0, tm=GMLP_L)],
        [(0, F32, None), (2, MXU_DTYPE, None), (3, MXU_DTYPE, None), (4, MXU_DTYPE, None)], tm=GMLP_L)
    dP["ab_spatial_w"] = dsw.reshape(GMLP_G, GMLP_L, GMLP_L)
    dP["ab_spatial_b"] = dsb_t.T
    gl, dgw, dgb = [], [], []
    for d, rev in dirs:
        g4, (dgw_, dgb_) = _scan_bwd("gla_bwd_" + d, _gla_chunk, GLA_L, n64, nx64, rev, gla_rows, gla_par[d], st0[d], do,
                                     gla_state, GLA_H * GLA_DV)
        gl.append(g4); dgw.append(dgw_[None]); dgb.append(dgb_)
    dP["ab_gate_w"] = jnp.concatenate(dgw, axis=0)
    dP["ab_gate_b"] = jnp.concatenate(dgb, axis=0)
    cat0f = functools.partial(_fn_concat, sums=(2, 1, 1, 1, 2, 2, 2))
    (dp0,) = _rowwise("l0_dp", cat0f, nb, cb,
                      [_row(gl[0][0]), _row(gl[1][0]), _row(dr), _row(du), _row(dgm), _row(gl[0][1]), _row(gl[1][1]),
                       _row(gl[0][2]), _row(gl[1][2]), _row(gl[0][3]), _row(gl[1][3])], [], [(AB_P, MXU_DTYPE)])
    g00 = comm.put("l0_mixer", _mm("l0_in_dw", h0, dp0, "tn", F32), d_ab_out, g00)
    dh0 = _mm("l0_in_dx", dp0, w_ab_in, "nt", MXU_DTYPE)
    (dxc0,), (dg00, dm0_s) = _rowwise_vjp("l0_prenorm_bwd", pre, nb, cb, [_row(xc0)], [g00, mod0], [_row(dh0)],
                                          [(0, F32, _row(dxc0_a))])
    dP["norm_g"] = jnp.concatenate([dg00, dg01, dg10, dg11], axis=0).reshape(2, 2, D)
    dmod = jnp.stack([dm0_s + dm0_a + dm0_b + dm0_e, dm1_f + dm1_a + dm1_b + dm1_j])
    return loss, dxc0[:T], dmod, dP


WEIGHTS = ("c_ctx", "mod_w", "mod_b", "norm_g", "ffn_w_in", "ffn_w_out", "ab_w_in", "ab_gate_w", "ab_gate_b", "ab_gla_norm_g",
           "ab_vnorm_g", "ab_spatial_w", "ab_spatial_b", "ab_w_out", "ssd_w_in", "ssd_conv_w", "ssd_conv_b", "ssd_dt_bias",
           "ssd_a_log", "ssd_d", "ssd_norm_g", "ssd_w_out", "final_norm_g")
SMALL_SHARDED = ("norm_g", "ab_gate_w", "ab_gate_b", "ssd_conv_w", "ssd_conv_b", "ssd_norm_g")
SMALL = ("c_ctx", "mod_b", "norm_g", "ab_gate_w", "ab_gate_b", "ab_gla_norm_g", "ab_vnorm_g", "ab_spatial_w", "ab_spatial_b",
         "ssd_conv_w", "ssd_conv_b", "ssd_dt_bias", "ssd_a_log", "ssd_d", "ssd_norm_g", "final_norm_g")
LANES = 1024


def _pack(arrs, rows_multiple=8):
    flat = jnp.concatenate([a.reshape(-1).astype(F32) for a in arrs])
    rows = -(-flat.shape[0] // LANES)
    rows = -(-rows // rows_multiple) * rows_multiple
    return jnp.pad(flat, (0, rows * LANES - flat.shape[0])).reshape(rows, LANES)


def _unpack_flat(buf, shapes):
    lead = buf.shape[:-2]
    flat = buf.reshape(lead + (-1,))
    out, o = [], 0
    for s in shapes:
        n = math.prod(s)
        out.append(flat[..., o:o + n].reshape(lead + tuple(s)))
        o += n
    return out


def _unshard(g):
    g = jnp.moveaxis(g, 0, -2)
    return g.reshape(g.shape[:-2] + (g.shape[-2] * g.shape[-1],))


def _my_shard(full, me, ws):
    return lax.dynamic_slice_in_dim(full, me * ws, ws, axis=full.ndim - 1)


def _silu_vjp(cvec, dsc):
    def body(c_ref, d_ref, o_ref):
        _, vjp = jax.vjp(jax.nn.silu, c_ref[...])
        o_ref[...] = vjp(d_ref[...])[0]

    return pl.pallas_call(body, name="c_ctx_bwd", out_shape=jax.ShapeDtypeStruct(cvec.shape, F32), compiler_params=_cp())(cvec, dsc)


def kernel(x, c, ctx, c_ctx, mod_w, mod_b, norm_g, ffn_w_in, ffn_w_out, ab_w_in, ab_gate_w, ab_gate_b, ab_gla_norm_g, ab_vnorm_g, ab_spatial_w, ab_spatial_b, ab_w_out, ssd_w_in, ssd_conv_w, ssd_conv_b, ssd_dt_bias, ssd_a_log, ssd_d, ssd_norm_g, ssd_w_out, final_norm_g, loss_target, m_c_ctx, m_mod_w, m_mod_b, m_norm_g, m_ffn_w_in, m_ffn_w_out, m_ab_w_in, m_ab_gate_w, m_ab_gate_b, m_ab_gla_norm_g, m_ab_vnorm_g, m_ab_spatial_w, m_ab_spatial_b, m_ab_w_out, m_ssd_w_in, m_ssd_conv_w, m_ssd_conv_b, m_ssd_dt_bias, m_ssd_a_log, m_ssd_d, m_ssd_norm_g, m_ssd_w_out, m_final_norm_g, v_c_ctx, v_mod_w, v_mod_b, v_norm_g, v_ffn_w_in, v_ffn_w_out, v_ab_w_in, v_ab_gate_w, v_ab_gate_b, v_ab_gla_norm_g, v_ab_vnorm_g, v_ab_spatial_w, v_ab_spatial_b, v_ab_w_out, v_ssd_w_in, v_ssd_conv_w, v_ssd_conv_b, v_ssd_dt_bias, v_ssd_a_log, v_ssd_d, v_ssd_norm_g, v_ssd_w_out, v_final_norm_g):
    a = dict(locals())
    me = _slot(*_mesh_pos())
    ws_mod = mod_w.shape[-1]

    fwd_small = [c] + [a[k] for k in SMALL_SHARDED]
    g_small = _ag_small("gather_small", _pack(fwd_small))
    parts = _unpack_flat(g_small, [t.shape for t in fwd_small])
    c_rows = parts[0].reshape(NDEV, D)
    full = {k: _unshard(p) for k, p in zip(SMALL_SHARDED, parts[1:])}
    c_all = jnp.concatenate([c_rows, c_ctx[None], jnp.zeros((7, D), F32)], axis=0)
    m_all = _ag_small("gather_mod", _mod_fwd(c_all, mod_w).reshape(2 * 16, ws_mod)).reshape(NDEV, 2, 16, ws_mod)
    m_mine = lax.dynamic_index_in_dim(m_all, me, axis=2, keepdims=False)
    mx = jnp.moveaxis(m_mine, 0, 1).reshape(2, N_MOD, D) + mod_b.reshape(2, N_MOD, D)
    mc = jnp.moveaxis(m_all[:, :, 8, :], 0, 1).reshape(2, N_MOD, D) + mod_b.reshape(2, N_MOD, D)
    pad2 = jnp.zeros((2, 2, D), F32)
    mod = jnp.concatenate([mx, pad2, mc, pad2], axis=1)

    big = {"l0_mixer": (ab_w_in[0], ab_w_out[0]), "l0_ffn": (ffn_w_in[0], ffn_w_out[0]),
           "l1_mixer": (ssd_w_in[0], ssd_w_out[0]), "l1_ffn": (ffn_w_in[1], ffn_w_out[1])}
    comm = _Exchange({k: tuple(w.astype(MXU_DTYPE) for w in v) for k, v in big.items()}, me)
    P = {
        "norm_g": full["norm_g"], "ab_gate_w": full["ab_gate_w"][0], "ab_gate_b": full["ab_gate_b"][0],
        "ab_gla_norm_g": ab_gla_norm_g, "ab_vnorm_g": ab_vnorm_g, "ab_spatial_w": ab_spatial_w[0], "ab_spatial_b": ab_spatial_b[0],
        "ssd_conv_w": full["ssd_conv_w"][0], "ssd_conv_b": full["ssd_conv_b"], "ssd_dt_bias": ssd_dt_bias[0],
        "ssd_a_log": ssd_a_log[0], "ssd_d": ssd_d, "ssd_norm_g": full["ssd_norm_g"], "final_norm_g": final_norm_g[None],
    }

    loss, grad_x, dmod, dP = _local_step(x[0], ctx[0], loss_target[0], mod, P, comm)

    dmx, dmc = dmod[:, 0:N_MOD].reshape(2, N_MOD * D), dmod[:, 8:8 + N_MOD].reshape(2, N_MOD * D)
    small_names = ("ab_gate_w", "ab_gate_b", "ab_gla_norm_g", "ab_vnorm_g", "ab_spatial_w", "ab_spatial_b", "norm_g", "ssd_conv_w",
                   "ssd_conv_b", "ssd_dt_bias", "ssd_a_log", "ssd_d", "ssd_norm_g", "final_norm_g")
    bwd_small = [dP[k] for k in small_names] + [dmc, dmx]
    shapes = [t.shape for t in bwd_small]
    g_bwd = _ag_small("gather_small_grads", _pack(bwd_small))
    summed = _unpack_flat(_sum_parts("sum_small_grads", g_bwd), shapes)
    gfull = dict(zip(small_names, summed[:-2]))
    dmc_sum, dmx_sum = summed[-2], summed[-1]
    dmx_all = _unpack_flat(g_bwd, shapes)[-1]
    dmx_sh = jnp.moveaxis(_my_shard(dmx_all, me, ws_mod), 0, 1)
    dm = jnp.concatenate([dmx_sh, _my_shard(dmc_sum, me, ws_mod)[:, None, :], jnp.zeros((2, 7, ws_mod), F32)], axis=1)
    d_mod_w, dsc = _mod_bwd(c_all, mod_w, dm)
    dsc_ctx = (dsc[0, 8] + dsc[1, 8])[None]
    dsc_all = _ag_small("gather_c_ctx_grad", jnp.concatenate([dsc_ctx, jnp.zeros((7, D), F32)], axis=0))
    d_c_ctx = _silu_vjp(c_ctx[None], _sum_parts("sum_c_ctx_grad", dsc_all)[0:1])[0]

    g_small_w = {
        "c_ctx": d_c_ctx, "mod_b": dmx_sum + dmc_sum, "norm_g": gfull["norm_g"], "ab_gate_w": gfull["ab_gate_w"][None],
        "ab_gate_b": gfull["ab_gate_b"][None], "ab_gla_norm_g": gfull["ab_gla_norm_g"], "ab_vnorm_g": gfull["ab_vnorm_g"],
        "ab_spatial_w": gfull["ab_spatial_w"][None], "ab_spatial_b": gfull["ab_spatial_b"][None], "ssd_conv_w": gfull["ssd_conv_w"][None],
        "ssd_conv_b": gfull["ssd_conv_b"], "ssd_dt_bias": gfull["ssd_dt_bias"][None], "ssd_a_log": gfull["ssd_a_log"][None],
        "ssd_d": gfull["ssd_d"], "ssd_norm_g": gfull["ssd_norm_g"], "final_norm_g": gfull["final_norm_g"][0],
    }
    for k in SMALL_SHARDED:
        g_small_w[k] = _my_shard(g_small_w[k], me, a[k].shape[-1])
    res = _adam("adam_small", _pack([a[k] for k in SMALL]), _pack([g_small_w[k] for k in SMALL])[None],
                _pack([a["m_" + k] for k in SMALL]), _pack([a["v_" + k] for k in SMALL]))
    out = {k: vals for k, vals in zip(SMALL, zip(*[_unpack_flat(r, [a[k].shape for k in SMALL]) for r in res]))}

    def adam_big(name, w2d, parts3d, m2d, v2d, shape):
        return tuple(r.reshape(shape) for r in _adam(name, w2d, parts3d, m2d, v2d))

    out["mod_w"] = adam_big("adam_mod_w", mod_w.reshape(-1, ws_mod), d_mod_w.reshape(1, -1, ws_mod), m_mod_w.reshape(-1, ws_mod),
                            v_mod_w.reshape(-1, ws_mod), mod_w.shape)
    ffn = {}
    for i, stage in enumerate(("l0_ffn", "l1_ffn")):
        for j, k in enumerate(("ffn_w_in", "ffn_w_out")):
            ffn[k, i] = adam_big("adam_%s%d" % (k, i), a[k][i], comm.recv[stage][j], a["m_" + k][i], a["v_" + k][i], a[k].shape[1:])
    for k in ("ffn_w_in", "ffn_w_out"):
        out[k] = tuple(jnp.stack(t) for t in zip(ffn[k, 0], ffn[k, 1]))
    for j, k in enumerate(("ssd_w_in", "ssd_w_out")):
        out[k] = adam_big("adam_" + k, a[k][0], comm.recv["l1_mixer"][j], a["m_" + k][0], a["v_" + k][0], a[k].shape)
    recv_ab = comm.finish(out["ssd_w_out"][3])
    for j, k in enumerate(("ab_w_in", "ab_w_out")):
        out[k] = adam_big("adam_" + k, a[k][0], recv_ab[j], a["m_" + k][0], a["v_" + k][0], a[k].shape)

    loss_all = lax.psum(loss[0, 0], ("x", "y", "c"))
    return (loss_all, grad_x[None], *[out[k][0] for k in WEIGHTS], *[out[k][1] for k in WEIGHTS],
            *[out[k][2] for k in WEIGHTS], *[out[k][3] for k in WEIGHTS])
```

```python
import functools
import math

import jax
import jax.numpy as jnp
from jax import lax
from jax.experimental import pallas as pl
from jax.experimental.pallas import tpu as pltpu

F32 = jnp.float32
BF16 = jnp.bfloat16
MXU_DTYPE = jnp.bfloat16
HI = lax.Precision.HIGHEST

D = 1024
NDEV = 8
N_MOD = 6
EPS = 1e-6
GRID_W = 64
CTX = 256
TM = 256
D_FF = 2816
GLA_H, GLA_DK, GLA_DV, GLA_LR, GLA_TAU, GLA_L = 4, 64, 128, 16, 16.0, 64
GMLP_G, GMLP_C, GMLP_L = 4, 128, 128
SSD_H, SSD_P, SSD_G, SSD_N, SSD_L, SSD_K = 32, 64, 4, 128, 128, 5
SSD_INNER = SSD_H * SSD_P
AB_IN = 2592
SSD_IN = 5184
AB_SEGS = ((256, 768), (1056, 1568), (1568, 2080), (2080, 2592), (0, 256), (800, 1056), (768, 800))
AB_P = 2688
SSD_SEGS = ((0, 2048), (3136, 5184), (2048, 2560), (2560, 3072), (3072, 3136))
SSD_P_W = 5376
VMEM_LIMIT = 56 * 1024 * 1024

ADAM_LR, ADAM_B1, ADAM_B2, ADAM_EPS, ADAM_WD, ADAM_STEP = 0.001, 0.9, 0.999, 1e-08, 0.01, 10


def _cp(sem=None, **kw):
    return pltpu.CompilerParams(dimension_semantics=sem, vmem_limit_bytes=VMEM_LIMIT, **kw)


def _dot(a, b, dims=(((1,), (0,)), ((), ()))):
    return lax.dot_general(a.astype(MXU_DTYPE), b.astype(MXU_DTYPE), dims, preferred_element_type=F32)


def _dot_nt(a, b):
    return _dot(a, b, (((1,), (1,)), ((), ())))


def _dot_tn(a, b):
    return _dot(a, b, (((0,), (0,)), ((), ())))


def _dotx(a, b, dims=(((1,), (0,)), ((), ()))):
    return lax.dot_general(a, b, dims, precision=HI, preferred_element_type=F32)


def _rms(x):
    return x * lax.rsqrt(jnp.mean(x * x, axis=-1, keepdims=True) + EPS)


def _pick(n, prefs):
    for p in prefs:
        if n % p == 0:
            return p
    return n


def _row(arr, width=None, colblk=0, tm=TM):
    width = arr.shape[1] if width is None else width
    return (arr, pl.BlockSpec((tm, width), lambda i, c=colblk: (i, c)))


def _full_spec(p):
    nd = p.ndim
    return pl.BlockSpec(p.shape, lambda i, nd=nd: (0,) * nd)


def _rowwise(name, fn, n_blocks, ctx_blk, rows, params, outs, tm=TM):
    nr, npar = len(rows), len(params)

    def body(*refs):
        t = (pl.program_id(0) >= ctx_blk).astype(F32)
        rv = [r[...].astype(F32) for r in refs[:nr]]
        pv = [p[...] for p in refs[nr:nr + npar]]
        res = fn(t, rv, pv)
        for o_ref, o in zip(refs[nr + npar:], res):
            o_ref[...] = o.astype(o_ref.dtype)

    return pl.pallas_call(
        body, name=name, grid=(n_blocks,),
        in_specs=[s for _, s in rows] + [_full_spec(p) for p in params],
        out_specs=[pl.BlockSpec((tm, w), lambda i: (i, 0)) for w, _ in outs],
        out_shape=[jax.ShapeDtypeStruct((n_blocks * tm, w), dt) for w, dt in outs],
        compiler_params=_cp(("parallel",)),
    )(*[a for a, _ in rows], *params)


def _rowwise_vjp(name, fn, n_blocks, ctx_blk, rows, params, douts, row_grads, tm=TM):
    nr, npar, nd = len(rows), len(params), len(douts)
    adds = [a for _, _, a in row_grads if a is not None]
    na = len(adds)

    def body(*refs):
        i = pl.program_id(0)
        t = (i >= ctx_blk).astype(F32)
        rv = [r[...].astype(F32) for r in refs[:nr]]
        pv = [p[...] for p in refs[nr:nr + npar]]
        dv = [r[...].astype(F32) for r in refs[nr + npar:nr + npar + nd]]
        av = [r[...].astype(F32) for r in refs[nr + npar + nd:nr + npar + nd + na]]
        o_refs = refs[nr + npar + nd + na:]
        _, vjp = jax.vjp(lambda r, p: tuple(fn(t, r, p)), rv, pv)
        d_rows, d_params = vjp(tuple(dv))
        ai = 0
        for o_ref, (ri, _, addend) in zip(o_refs, row_grads):
            g = d_rows[ri]
            if addend is not None:
                g = g + av[ai]
                ai += 1
            o_ref[...] = g.astype(o_ref.dtype)
        p_refs = o_refs[len(row_grads):]

        @pl.when(i == 0)
        def _():
            for p_ref in p_refs:
                p_ref[...] = jnp.zeros_like(p_ref)

        for p_ref, g in zip(p_refs, d_params):
            p_ref[...] += g

    widths = [rows[ri][1].block_shape[1] for ri, _, _ in row_grads]
    res = pl.pallas_call(
        body, name=name, grid=(n_blocks,),
        in_specs=[s for _, s in rows] + [_full_spec(p) for p in params] + [s for _, s in douts] + [s for _, s in adds],
        out_specs=[pl.BlockSpec((tm, w), lambda i: (i, 0)) for w in widths] + [_full_spec(p) for p in params],
        out_shape=[jax.ShapeDtypeStruct((n_blocks * tm, w), dt) for w, (_, dt, _) in zip(widths, row_grads)]
        + [jax.ShapeDtypeStruct(p.shape, F32) for p in params],
        compiler_params=_cp(("arbitrary",)),
    )(*[a for a, _ in rows], *params, *[a for a, _ in douts], *[a for a, _ in adds])
    return res[:len(row_grads)], res[len(row_grads):]


def _mm(name, a, b, mode, out_dtype):
    if mode == "nn":
        m, kk = a.shape
        n = b.shape[1]
    elif mode == "nt":
        m, kk = a.shape
        n = b.shape[0]
    else:
        kk, m = a.shape
        n = b.shape[1]
    tm = _pick(m, (1088, 1024, 768, 512, 384, 256, 128))
    tn = _pick(n, (512, 384, 256, 128))
    if mode == "tn":
        tk = _pick(kk, (1088, 768, 512, 256))
    else:
        tk = kk if kk <= 2816 else _pick(kk, (1792, 1408, 1024, 896, 768, 512, 384, 256, 128))
    nk = kk // tk
    if mode == "nn":
        specs = [pl.BlockSpec((tm, tk), lambda i, j, k: (i, k)), pl.BlockSpec((tk, tn), lambda i, j, k: (k, j))]
        dims = (((1,), (0,)), ((), ()))
    elif mode == "nt":
        specs = [pl.BlockSpec((tm, tk), lambda i, j, k: (i, k)), pl.BlockSpec((tn, tk), lambda i, j, k: (j, k))]
        dims = (((1,), (1,)), ((), ()))
    else:
        specs = [pl.BlockSpec((tk, tm), lambda i, j, k: (k, i)), pl.BlockSpec((tk, tn), lambda i, j, k: (k, j))]
        dims = (((0,), (0,)), ((), ()))

    def body(a_ref, b_ref, o_ref, *acc):
        part = lax.dot_general(a_ref[...].astype(MXU_DTYPE), b_ref[...].astype(MXU_DTYPE), dims, preferred_element_type=F32)
        if nk == 1:
            o_ref[...] = part.astype(o_ref.dtype)
        else:
            k = pl.program_id(2)

            @pl.when(k == 0)
            def _():
                acc[0][...] = part

            @pl.when(k > 0)
            def _():
                acc[0][...] += part

            @pl.when(k == nk - 1)
            def _():
                o_ref[...] = acc[0][...].astype(o_ref.dtype)

    return pl.pallas_call(
        body, name=name, grid=(m // tm, n // tn, nk), in_specs=specs,
        out_specs=pl.BlockSpec((tm, tn), lambda i, j, k: (i, j)),
        out_shape=jax.ShapeDtypeStruct((m, n), out_dtype),
        scratch_shapes=[] if nk == 1 else [pltpu.VMEM((tm, tn), F32)],
        compiler_params=_cp(("parallel", "parallel", "arbitrary")),
    )(a, b)


def _sel_mod(modp, t):
    return modp[0:8] * (1.0 - t) + modp[8:16] * t


def _fn_prenorm(t, rows, params, *, a, b):
    (x,), (g, modp) = rows, params
    m = _sel_mod(modp, t)
    return ((_rms(x) * g) * (1.0 + m[b:b + 1]) + m[a:a + 1],)


def _fn_resid_prenorm(t, rows, params, *, gi, a, b):
    (x, y), (g, mod_a, mod_b) = rows, params
    ma, mb = _sel_mod(mod_a, t), _sel_mod(mod_b, t)
    xn = x + ma[gi:gi + 1] * y
    return xn, (_rms(xn) * g) * (1.0 + mb[b:b + 1]) + mb[a:a + 1]


def _fn_resid(t, rows, params, *, gi):
    (x, y), (mod_a,) = rows, params
    return (x + _sel_mod(mod_a, t)[gi:gi + 1] * y,)


def _fn_swiglu(t, rows, params):
    (pf,) = rows
    return (jax.nn.silu(pf[:, :D_FF]) * pf[:, D_FF:],)


def _fn_mixpost(t, rows, params):
    (o_f, o_b, r, u, g), (gla_g, vn_g, sw, sb_t) = rows, params
    o = o_f + o_b
    a = jnp.concatenate([_rms(o[:, h * GLA_DV:(h + 1) * GLA_DV]) for h in range(GLA_H)], axis=1) * gla_g * jax.nn.silu(r)
    uu, vv = jax.nn.gelu(u), jax.nn.gelu(g)
    mu = jnp.mean(vv, axis=-1, keepdims=True)
    var = jnp.mean(jnp.square(vv - mu), axis=-1, keepdims=True)
    vn = ((vv - mu) * lax.rsqrt(var + EPS)) * vn_g
    s = jnp.concatenate(
        [_dot(sw[gi * GMLP_L:(gi + 1) * GMLP_L, :], vn[:, gi * GMLP_C:(gi + 1) * GMLP_C]) + sb_t[:, gi:gi + 1]
         for gi in range(GMLP_G)], axis=1)
    return (jnp.concatenate([a, uu * s], axis=1),)


def _head_expand():
    r = lax.broadcasted_iota(jnp.int32, (SSD_H, SSD_INNER), 0)
    c = lax.broadcasted_iota(jnp.int32, (SSD_H, SSD_INNER), 1)
    return (c // SSD_P == r).astype(F32)


def _fn_ssd_finish(t, rows, params):
    (y_f, y_b, xs, z), (d_skip, norm_g) = rows, params
    d_full = _dotx(jnp.broadcast_to(d_skip, (8, SSD_H)), _head_expand())[0:1]
    y = (y_f + y_b + d_full * xs) * jax.nn.silu(z)
    gw = SSD_INNER // SSD_G
    return (jnp.concatenate([_rms(y[:, gi * gw:(gi + 1) * gw]) for gi in range(SSD_G)], axis=1) * norm_g,)


def _fn_concat(t, rows, params, *, sums, pad=0):
    out, i = [], 0
    for n in sums:
        acc = rows[i]
        for j in range(1, n):
            acc = acc + rows[i + j]
        out.append(acc)
        i += n
    if pad:
        out.append(jnp.zeros((out[0].shape[0], pad), F32))
    return (jnp.concatenate(out, axis=1),)


def _tri(n, rev):
    r = lax.broadcasted_iota(jnp.int32, (n, n), 0)
    c = lax.broadcasted_iota(jnp.int32, (n, n), 1)
    return (r <= c) if rev else (r >= c)


def _gla_chunk(S, v, k, q, tail, gw, gb, *, rev):
    L = GLA_L
    msk = _tri(L, rev)
    tri = msk.astype(F32)
    lr = tail[:, GLA_LR:2 * GLA_LR] if rev else tail[:, 0:GLA_LR]
    la = jax.nn.log_sigmoid(_dot(lr, gw) + gb) / GLA_TAU
    b = _dotx(tri, la)
    b_last = b[0:1] if rev else b[L - 1:L]
    kd = k * jnp.exp(b_last - b)
    qd = (q * GLA_DK ** -0.5) * jnp.exp(b)
    ki = k * jnp.exp(-b)
    dec = jnp.exp(b_last)
    o_parts, s_parts = [], []
    for h in range(GLA_H):
        ks, vs = slice(h * GLA_DK, (h + 1) * GLA_DK), slice(h * GLA_DV, (h + 1) * GLA_DV)
        sh = S[:, ks]
        sc = jnp.where(msk, _dot_nt(qd[:, ks], ki[:, ks]), 0.0)
        o_parts.append(_dot_nt(qd[:, ks], sh) + _dot(sc, v[:, vs]))
        s_parts.append(dec[:, ks] * sh + _dot_tn(v[:, vs], kd[:, ks]))
    return jnp.concatenate(s_parts, axis=1), jnp.concatenate(o_parts, axis=1)


def _ssd_chunk(S, x, bm, cm, tail, dtb, alog, *, rev):
    L = SSD_L
    msk = _tri(L, rev)
    tri = msk.astype(F32)
    r = lax.broadcasted_iota(jnp.int32, (L, L), 0)
    c = lax.broadcasted_iota(jnp.int32, (L, L), 1)
    eye = (r == c).astype(F32)
    raw = tail[:, SSD_H:2 * SSD_H] if rev else tail[:, 0:SSD_H]
    dt = jax.nn.softplus(raw + dtb)
    dta = dt * (-jnp.exp(alog))
    acum = _dotx(tri, dta)
    a_last = acum[0:1] if rev else acum[L - 1:L]
    wst = dt * jnp.exp(a_last - acum)
    eac = jnp.exp(acum)
    tr = _dotx(jnp.concatenate([acum, dt, wst], axis=1), eye, (((0,), (0,)), ((), ())))
    acum_t, dt_t, wst_t = tr[0:SSD_H], tr[SSD_H:2 * SSD_H], tr[2 * SSD_H:3 * SSD_H]
    decrow = jnp.exp(_dotx(jnp.broadcast_to(a_last, (8, SSD_H)), _head_expand())[0:1])
    lane = lax.broadcasted_iota(jnp.int32, (1, 2 * SSD_P), 1)
    m0 = (lane < SSD_P).astype(F32)
    m1 = 1.0 - m0
    pairs_per_group = SSD_H // SSD_G // 2
    y_parts, s_parts = [], []
    for g in range(SSD_G):
        ns = slice(g * SSD_N, (g + 1) * SSD_N)
        bg, cg = bm[:, ns], cm[:, ns]
        cb = _dot_nt(cg, bg)
        bgt = _dotx(bg, eye, (((0,), (0,)), ((), ())))
        for jj in range(pairs_per_group):
            j = g * pairs_per_group + jj
            ls = slice(j * 2 * SSD_P, (j + 1) * 2 * SSD_P)
            xp, sp = x[:, ls], S[:, ls]
            xm = jnp.concatenate([xp * m0, xp * m1], axis=0)
            sm = jnp.concatenate([sp * m0, sp * m1], axis=0)
            lhs, bw = [], []
            for h in (2 * j, 2 * j + 1):
                seg = acum[:, h:h + 1] - acum_t[h:h + 1, :]
                lhs.append(cb * jnp.exp(jnp.where(msk, seg, -jnp.inf)) * dt_t[h:h + 1, :])
                bw.append(bgt * wst_t[h:h + 1, :])
            lhs += [cg * eac[:, h:h + 1] for h in (2 * j, 2 * j + 1)]
            y_parts.append(_dot(jnp.concatenate(lhs, axis=1), jnp.concatenate([xm, sm], axis=0)))
            s_parts.append(sp * decrow[:, ls] + _dot(jnp.concatenate(bw, axis=1), xm))
    return jnp.concatenate(s_parts, axis=1), jnp.concatenate(y_parts, axis=1)


def _multi_chunk(chunk_fn, L, subs, nr):
    def fn(S, *args, rev):
        rows, params = args[:nr], args[nr:]
        ys = [None] * subs
        for j in (range(subs - 1, -1, -1) if rev else range(subs)):
            S, ys[j] = chunk_fn(S, *[r[j * L:(j + 1) * L] for r in rows], *params, rev=rev)
        return S, jnp.concatenate(ys, axis=0)

    return fn


def _scan_order(n, nx, rev, backward):
    nc = n - nx

    def fwd(s):
        return (n - 1 - s) if rev else jnp.where(s < nc, s + nx, s - nc)

    return (lambda s: fwd(n - 1 - s)) if backward else fwd


def _scan_fwd(name, chunk_fn, L, n, nx, rev, rows, params, state_shape, out_w):
    order = _scan_order(n, nx, rev, False)
    nr, npar = len(rows), len(params)

    def body(*refs):
        s_scr = refs[-1]

        @pl.when(pl.program_id(0) == 0)
        def _():
            s_scr[...] = jnp.zeros_like(s_scr)

        s_in = s_scr[...]
        y_ref, st_ref = refs[nr + npar], refs[nr + npar + 1]
        st_ref[0] = s_in
        s_new, y = chunk_fn(s_in, *[r[...] for r in refs[:nr]], *[p[...] for p in refs[nr:nr + npar]], rev=rev)
        y_ref[...] = y
        s_scr[...] = s_new

    return pl.pallas_call(
        body, name=name, grid=(n,),
        in_specs=[pl.BlockSpec((L, w), lambda s, c=c: (order(s), c)) for _, w, c in rows] + [_full_spec(p) for p in params],
        out_specs=[pl.BlockSpec((L, out_w), lambda s: (order(s), 0)),
                   pl.BlockSpec((1,) + state_shape, lambda s: (order(s), 0, 0))],
        out_shape=[jax.ShapeDtypeStruct((n * L, out_w), F32), jax.ShapeDtypeStruct((n,) + state_shape, F32)],
        scratch_shapes=[pltpu.VMEM(state_shape, F32)],
        compiler_params=_cp(("arbitrary",)),
    )(*[a for a, _, _ in rows], *params)


def _scan_bwd(name, chunk_fn, L, n, nx, rev, rows, params, states, dy, state_shape, out_w):
    order = _scan_order(n, nx, rev, True)
    nr, npar = len(rows), len(params)

    def body(*refs):
        i = pl.program_id(0)
        ds_scr = refs[-1]
        rv = [r[...] for r in refs[:nr]]
        pv = [p[...] for p in refs[nr:nr + npar]]
        st_ref, dy_ref = refs[nr + npar], refs[nr + npar + 1]
        o_refs = refs[nr + npar + 2:-1]
        p_refs = o_refs[nr:]

        @pl.when(i == 0)
        def _():
            ds_scr[...] = jnp.zeros_like(ds_scr)
            for p_ref in p_refs:
                p_ref[...] = jnp.zeros_like(p_ref)

        _, vjp = jax.vjp(functools.partial(chunk_fn, rev=rev), st_ref[0], *rv, *pv)
        grads = vjp((ds_scr[...], dy_ref[...].astype(F32)))
        ds_scr[...] = grads[0]
        for o_ref, g in zip(o_refs[:nr], grads[1:1 + nr]):
            o_ref[...] = g
        for p_ref, g in zip(p_refs, grads[1 + nr:]):
            p_ref[...] += g

    res = pl.pallas_call(
        body, name=name, grid=(n,),
        in_specs=[pl.BlockSpec((L, w), lambda s, c=c: (order(s), c)) for _, w, c in rows] + [_full_spec(p) for p in params]
        + [pl.BlockSpec((1,) + state_shape, lambda s: (order(s), 0, 0)), pl.BlockSpec((L, out_w), lambda s: (order(s), 0))],
        out_specs=[pl.BlockSpec((L, w), lambda s: (order(s), 0)) for _, w, _ in rows] + [_full_spec(p) for p in params],
        out_shape=[jax.ShapeDtypeStruct((n * L, w), F32) for _, w, _ in rows] + [jax.ShapeDtypeStruct(p.shape, F32) for p in params],
        scratch_shapes=[pltpu.VMEM(state_shape, F32)],
        compiler_params=_cp(("arbitrary",)),
    )(*[a for a, _, _ in rows], *params, states, dy)
    return res[:nr], res[nr:]


CONV_W = 1024
CONV_COLBLK = (0, 1, 4)


def _conv_specs(nb, src_blk):
    halo = TM // 8
    return [pl.BlockSpec((TM, CONV_W), lambda j, i: (i, src_blk(j))),
            pl.BlockSpec((8, CONV_W), lambda j, i: (jnp.maximum(i * halo - 1, 0), src_blk(j))),
            pl.BlockSpec((8, CONV_W), lambda j, i: (jnp.minimum(i * halo + halo, nb * halo - 1), src_blk(j)))]


def _conv_ext(i, nb, cur, prev, nxt):
    has_prev = jnp.logical_and(i > 0, i < nb - 1)
    has_next = i < nb - 2
    return jnp.concatenate([jnp.where(has_prev, prev, 0.0), cur, jnp.where(has_next, nxt, 0.0)], axis=0)


def _conv_taps(ext, w, flip):
    acc = None
    for j in range(SSD_K):
        wj = w[SSD_K - 1 - j:SSD_K - j, :] if flip else w[j:j + 1, :]
        term = wj * ext[6 + j:6 + j + TM, :]
        acc = term if acc is None else acc + term
    return acc


def _conv(name, src, w8, b1, nb, *, permuted_src, act, flip, out_dtype):
    src_blk = (lambda j: jnp.where(j == 2, CONV_COLBLK[2], j)) if permuted_src else (lambda j: j)

    def body(cur, prev, nxt, w_ref, b_ref, o_ref):
        ext = _conv_ext(pl.program_id(1), nb, cur[...].astype(F32), prev[...].astype(F32), nxt[...].astype(F32))
        acc = _conv_taps(ext, w_ref[...], flip)
        if act:
            acc = jax.nn.silu(acc + b_ref[...])
        o_ref[...] = acc.astype(o_ref.dtype)

    return pl.pallas_call(
        body, name=name, grid=(3, nb),
        in_specs=_conv_specs(nb, src_blk) + [pl.BlockSpec((8, CONV_W), lambda j, i: (0, j)), pl.BlockSpec((1, CONV_W), lambda j, i: (0, j))],
        out_specs=pl.BlockSpec((TM, CONV_W), lambda j, i: (i, j)),
        out_shape=jax.ShapeDtypeStruct((nb * TM, 3 * CONV_W), out_dtype),
        compiler_params=_cp(("parallel", "parallel")),
    )(src, src, src, w8, b1)


def _conv_bwd_pre(name, p1, w8, b1, dxbc_parts, nb):
    src_blk = lambda j: jnp.where(j == 2, CONV_COLBLK[2], j)
    xs_parts, bc_parts = dxbc_parts
    n_x, n_bc = len(xs_parts), len(bc_parts)

    def body(*refs):
        cur, prev, nxt, w_ref, b_ref = refs[:5]
        d_refs = refs[5:5 + n_x + n_bc]
        da_ref, dw_ref, db_ref = refs[5 + n_x + n_bc:]
        j, i = pl.program_id(0), pl.program_id(1)
        ext = _conv_ext(i, nb, cur[...], prev[...], nxt[...])
        acc = _conv_taps(ext, w_ref[...], False) + b_ref[...]
        dx = d_refs[0][...]
        for r in d_refs[1:n_x]:
            dx = dx + r[...]
        dbc = jnp.concatenate([d_refs[n_x][...] + d_refs[n_x + 1][...], d_refs[n_x + 2][...] + d_refs[n_x + 3][...]], axis=1)
        dy = jnp.where(j == 2, dbc, dx)
        sg = jax.nn.sigmoid(acc)
        da = dy * (sg + acc * sg * (1.0 - sg))
        da_ref[...] = da

        @pl.when(i == 0)
        def _():
            dw_ref[...] = jnp.zeros_like(dw_ref)
            db_ref[...] = jnp.zeros_like(db_ref)

        rows = [jnp.sum(da * ext[6 + t:6 + t + TM, :], axis=0, keepdims=True) for t in range(SSD_K)]
        dw_ref[...] += jnp.concatenate(rows + [jnp.zeros((8 - SSD_K, CONV_W), F32)], axis=0)
        db_ref[...] += jnp.sum(da, axis=0, keepdims=True)

    x_specs = [pl.BlockSpec((TM, CONV_W), lambda j, i: (i, jnp.minimum(j, 1))) for _ in xs_parts]
    bc_specs = [pl.BlockSpec((TM, 512), lambda j, i: (i, 0)) for _ in bc_parts]
    return pl.pallas_call(
        body, name=name, grid=(3, nb),
        in_specs=_conv_specs(nb, src_blk) + [pl.BlockSpec((8, CONV_W), lambda j, i: (0, j)), pl.BlockSpec((1, CONV_W), lambda j, i: (0, j))]
        + x_specs + bc_specs,
        out_specs=[pl.BlockSpec((TM, CONV_W), lambda j, i: (i, j)), pl.BlockSpec((8, CONV_W), lambda j, i: (0, j)),
                   pl.BlockSpec((1, CONV_W), lambda j, i: (0, j))],
        out_shape=[jax.ShapeDtypeStruct((nb * TM, 3 * CONV_W), F32), jax.ShapeDtypeStruct((8, 3 * CONV_W), F32),
                   jax.ShapeDtypeStruct((1, 3 * CONV_W), F32)],
        compiler_params=_cp(("arbitrary", "arbitrary")),
    )(p1, p1, p1, w8, b1, *xs_parts, *bc_parts)


def _grid_rows(val, a, kb):
    return jnp.concatenate([val[:, t * D:(t + 1) * D] for t in range(kb)], axis=0)


def _perm(name, xc, a, nb):
    n = xc.shape[0]
    b = (n - CTX) // a
    kb = TM // a
    view = xc.reshape(n // b, b * D)

    def body(v_ref, c_ref, o_ref):
        i = pl.program_id(0)

        @pl.when(i < nb - 1)
        def _():
            o_ref[...] = _grid_rows(v_ref[...], a, kb)

        @pl.when(i == nb - 1)
        def _():
            o_ref[...] = c_ref[...]

    return pl.pallas_call(
        body, name=name, grid=(nb,),
        in_specs=[pl.BlockSpec((a, kb * D), lambda i: (0, jnp.minimum(i, nb - 2))), pl.BlockSpec((TM, D), lambda i: (nb - 1, 0))],
        out_specs=pl.BlockSpec((TM, D), lambda i: (i, 0)),
        out_shape=jax.ShapeDtypeStruct((n, D), xc.dtype),
        compiler_params=_cp(("parallel",)),
    )(view, xc)


def _loss_head(x, f, target, modp, g_final, nb, rows_r):
    t_tok = target.shape[0]
    kb = TM // rows_r
    tview = target.reshape(rows_r, (t_tok // rows_r) * D)

    def fn(x_, f_, tgt, modp_, g_, is_ctx):
        xn = x_ + _sel_mod(modp_, is_ctx)[5:6] * f_
        err = _rms(xn) * g_ - tgt
        return 0.5 * jnp.sum(jnp.mean(err * err, axis=-1)) * (1.0 - is_ctx)

    def body(x_ref, f_ref, t_ref, m_ref, g_ref, l_ref, dx_ref, df_ref, dm_ref, dg_ref):
        i = pl.program_id(0)
        is_ctx = (i == nb - 1).astype(F32)
        tgt = _grid_rows(t_ref[...], rows_r, kb)
        l, vjp = jax.vjp(lambda a_, b_, c_, d_: fn(a_, b_, tgt, c_, d_, is_ctx), x_ref[...], f_ref[...], m_ref[...], g_ref[...])
        dx, df, dm, dg = vjp(jnp.ones((), F32))

        @pl.when(i == 0)
        def _():
            l_ref[...] = jnp.zeros_like(l_ref)
            dm_ref[...] = jnp.zeros_like(dm_ref)
            dg_ref[...] = jnp.zeros_like(dg_ref)

        l_ref[...] += jnp.reshape(l, (1, 1))
        dx_ref[...] = dx
        df_ref[...] = df.astype(df_ref.dtype)
        dm_ref[...] += dm
        dg_ref[...] += dg

    rowspec = pl.BlockSpec((TM, D), lambda i: (i, 0))
    return pl.pallas_call(
        body, name="loss_head", grid=(nb,),
        in_specs=[rowspec, rowspec, pl.BlockSpec((rows_r, kb * D), lambda i: (0, jnp.minimum(i, nb - 2))),
                  _full_spec(modp), _full_spec(g_final)],
        out_specs=[pl.BlockSpec((1, 1), lambda i: (0, 0)), rowspec, rowspec, _full_spec(modp), _full_spec(g_final)],
        out_shape=[jax.ShapeDtypeStruct((1, 1), F32), jax.ShapeDtypeStruct(x.shape, F32), jax.ShapeDtypeStruct(x.shape, MXU_DTYPE),
                   jax.ShapeDtypeStruct(modp.shape, F32), jax.ShapeDtypeStruct(g_final.shape, F32)],
        compiler_params=_cp(("arbitrary",)),
    )(x, f, tview, modp, g_final)


def _repack(name, shards, segs, wp):
    nd, kk, ws = shards.shape
    tr = 128
    used = sum(e - s for s, e in segs)

    def body(a_ref, o_ref):
        full = jnp.concatenate([a_ref[d].astype(F32) for d in range(nd)], axis=1)
        parts = [full[:, s:e] for s, e in segs]
        if wp > used:
            parts.append(jnp.zeros((tr, wp - used), F32))
        o_ref[...] = jnp.concatenate(parts, axis=1).astype(o_ref.dtype)

    return pl.pallas_call(
        body, name=name, grid=(kk // tr,),
        in_specs=[pl.BlockSpec((nd, tr, ws), lambda i: (0, i, 0))],
        out_specs=pl.BlockSpec((tr, wp), lambda i: (i, 0)),
        out_shape=jax.ShapeDtypeStruct((kk, wp), MXU_DTYPE),
        compiler_params=_cp(("parallel",)),
    )(shards)


def _unpack(name, dw, segs, ws, out_dtype):
    kk, wp = dw.shape
    tr = 128
    order = sorted(range(len(segs)), key=lambda i: segs[i][0])
    offs, o = [], 0
    for s, e in segs:
        offs.append(o)
        o += e - s

    def body(a_ref, o_ref):
        a = a_ref[...].astype(F32)
        full = jnp.concatenate([a[:, offs[i]:offs[i] + segs[i][1] - segs[i][0]] for i in order], axis=1)
        for d in range(NDEV):
            o_ref[d] = full[:, d * ws:(d + 1) * ws].astype(o_ref.dtype)

    return pl.pallas_call(
        body, name=name, grid=(kk // tr,),
        in_specs=[pl.BlockSpec((tr, wp), lambda i: (i, 0))],
        out_specs=pl.BlockSpec((NDEV, tr, ws), lambda i: (0, i, 0)),
        out_shape=jax.ShapeDtypeStruct((NDEV, kk, ws), out_dtype),
        compiler_params=_cp(("parallel",)),
    )(dw)


def _adam_math(w, g, m, v):
    m = ADAM_B1 * m + (1.0 - ADAM_B1) * g
    v = ADAM_B2 * v + (1.0 - ADAM_B2) * jnp.square(g)
    m_hat = m / (1.0 - ADAM_B1 ** ADAM_STEP)
    v_hat = v / (1.0 - ADAM_B2 ** ADAM_STEP)
    delta = -ADAM_LR * (m_hat / (jnp.sqrt(v_hat) + ADAM_EPS) + ADAM_WD * w)
    return delta, m, v


def _adam(name, w, parts, m, v, after):
    r, c = w.shape
    npart = parts.shape[0]
    tr = _pick(r, (256, 128, 64, 32, 16, 8)) if r * c * 4 > (1 << 20) else r

    def body(w_ref, p_ref, m_ref, v_ref, after_ref, g_ref, d_ref, nm_ref, nv_ref):
        g = p_ref[0].astype(F32)
        for s in range(1, npart):
            g = g + p_ref[s].astype(F32)
        delta, nm, nv = _adam_math(w_ref[...], g, m_ref[...], v_ref[...])
        g_ref[...], d_ref[...], nm_ref[...], nv_ref[...] = g, delta, nm, nv

    spec = pl.BlockSpec((tr, c), lambda i: (i, 0))
    return pl.pallas_call(
        body, name=name, grid=(r // tr,),
        in_specs=[spec, pl.BlockSpec((npart, tr, c), lambda i: (0, i, 0)), spec, spec, ANY],
        out_specs=[spec] * 4, out_shape=[jax.ShapeDtypeStruct((r, c), F32)] * 4,
        compiler_params=_cp(("parallel",)),
    )(w, parts, m, v, after)


def _mod_fwd(c_all, mod_w):
    nl, _, ws = mod_w.shape

    def body(c_ref, w_ref, o_ref):
        o_ref[0] = _dot(jax.nn.silu(c_ref[...]), w_ref[0])

    return pl.pallas_call(
        body, name="mod_fwd", grid=(nl,),
        in_specs=[_full_spec(c_all), pl.BlockSpec((1, D, ws), lambda i: (i, 0, 0))],
        out_specs=pl.BlockSpec((1, 16, ws), lambda i: (i, 0, 0)),
        out_shape=jax.ShapeDtypeStruct((nl, 16, ws), F32),
        compiler_params=_cp(("parallel",)),
    )(c_all, mod_w)


def _mod_bwd(c_all, mod_w, dm):
    nl, _, ws = mod_w.shape

    def body(c_ref, w_ref, d_ref, dw_ref, dc_ref):
        dw_ref[0] = _dot_tn(jax.nn.silu(c_ref[...]), d_ref[0])
        dc_ref[0] = _dot_nt(d_ref[0], w_ref[0])

    return pl.pallas_call(
        body, name="mod_bwd", grid=(nl,),
        in_specs=[_full_spec(c_all), pl.BlockSpec((1, D, ws), lambda i: (i, 0, 0)), pl.BlockSpec((1, 16, ws), lambda i: (i, 0, 0))],
        out_specs=[pl.BlockSpec((1, D, ws), lambda i: (i, 0, 0)), pl.BlockSpec((1, 16, D), lambda i: (i, 0, 0))],
        out_shape=[jax.ShapeDtypeStruct((nl, D, ws), F32), jax.ShapeDtypeStruct((nl, 16, D), F32)],
        compiler_params=_cp(("parallel",)),
    )(c_all, mod_w, dm)


def _sum_parts(name, parts):
    npart, r, c = parts.shape

    def body(p_ref, o_ref):
        g = p_ref[0].astype(F32)
        for s in range(1, npart):
            g = g + p_ref[s].astype(F32)
        o_ref[...] = g

    return pl.pallas_call(body, name=name, out_shape=jax.ShapeDtypeStruct((r, c), F32), compiler_params=_cp())(parts)


MESH = pl.DeviceIdType.MESH
ANY = pl.BlockSpec(memory_space=pl.ANY)
N_PEERS = NDEV - 1


def _mesh_pos():
    return lax.axis_index("x"), lax.axis_index("y"), lax.axis_index("c")


def _slot(px, py, pc):
    return 4 * px + 2 * py + pc


def _two_level_gather(x_refs, o_refs, send_sems, recv_sems, local_sems):
    x, y, c = _mesh_pos()
    me, sibling = (x, y, c), (x, y, 1 - c)
    chips = [(1 - x, y), (x, 1 - y), (1 - x, 1 - y)]
    n = len(x_refs)

    def copy(a, k, block, to, src=None):
        dst = o_refs[a].at[_slot(*block)]
        return pltpu.make_async_remote_copy(src_ref=dst if src is None else src, dst_ref=dst, send_sem=send_sems.at[a, k],
                                            recv_sem=recv_sems.at[a, k], device_id=to, device_id_type=MESH)

    mine = [pltpu.make_async_copy(x_refs[a], o_refs[a].at[_slot(*me)], local_sems.at[a]) for a in range(n)]
    for cp in mine:
        cp.start()
    first = []
    for a in range(n):
        first.append(copy(a, 0, me, sibling, src=x_refs[a]))
        first += [copy(a, 1 + j, me, (*chip, c), src=x_refs[a]) for j, chip in enumerate(chips)]
    for cp in first:
        cp.start()
    passed = []
    for j, chip in enumerate(chips):
        for a in range(n):
            copy(a, 1 + j, (*chip, c), me).wait_recv()
            fwd = copy(a, 4 + j, (*chip, c), sibling)
            fwd.start()
            passed.append(fwd)
    for a in range(n):
        copy(a, 0, sibling, me).wait_recv()
        for j, chip in enumerate(chips):
            copy(a, 4 + j, (*chip, 1 - c), me).wait_recv()
    for cp in first + passed:
        cp.wait_send()
    for cp in mine:
        cp.wait()


def _ag_small(name, x):
    r, c = x.shape

    def body(x_ref, o_ref, send_sems, recv_sems, local_sems):
        _two_level_gather([x_ref], [o_ref], send_sems, recv_sems, local_sems)

    return pl.pallas_call(
        body, name=name, out_shape=jax.ShapeDtypeStruct((NDEV, r, c), x.dtype),
        in_specs=[pl.BlockSpec(memory_space=pltpu.VMEM)], out_specs=pl.BlockSpec(memory_space=pltpu.VMEM),
        scratch_shapes=[pltpu.SemaphoreType.DMA((1, N_PEERS)), pltpu.SemaphoreType.DMA((1, N_PEERS)), pltpu.SemaphoreType.DMA((1,))],
        compiler_params=pltpu.CompilerParams(vmem_limit_bytes=VMEM_LIMIT),
    )(x)


def _ag_big(name, shards):
    n = len(shards)

    def body(*refs):
        _two_level_gather(refs[:n], refs[n:2 * n], *refs[2 * n:])

    return pl.pallas_call(
        body, name=name, out_shape=[jax.ShapeDtypeStruct((NDEV,) + s.shape, s.dtype) for s in shards],
        in_specs=[ANY] * n, out_specs=[ANY] * n,
        scratch_shapes=[pltpu.SemaphoreType.DMA((n, N_PEERS)), pltpu.SemaphoreType.DMA((n, N_PEERS)), pltpu.SemaphoreType.DMA((n,))],
    )(*shards)


HBM = pl.BlockSpec(memory_space=pltpu.HBM)
SEM = pl.BlockSpec(memory_space=pltpu.SEMAPHORE)
EFFECT = pltpu.SideEffectType.DATAFLOW_SIDE_EFFECTING


def _peers(x, y, c):
    return [(k - 1, ((1 - x) if k & 4 else x, (1 - y) if k & 2 else y, (1 - c) if k & 1 else c)) for k in range(1, NDEV)]


def _xchg_copy(src_refs, land_refs, send_sems, recv_sems, a, k, peer, me, scatter):
    src = src_refs[a].at[_slot(*peer)] if scatter else src_refs[a]
    return pltpu.make_async_remote_copy(src_ref=src, dst_ref=land_refs[a].at[me], send_sem=send_sems.at[a * N_PEERS + k],
                                        recv_sem=recv_sems.at[a * N_PEERS + k], device_id=peer, device_id_type=MESH)


def _xchg_start(name, srcs, lands, deps, scatter):
    n, nd = len(srcs), len(deps)

    def body(*refs):
        src_refs, land_refs = refs[:n], refs[n:2 * n]
        send_sems, recv_sems, token = refs[2 * n + nd], refs[2 * n + nd + 1], refs[-1]
        x, y, c = _mesh_pos()
        me = _slot(x, y, c)
        for k, peer in _peers(x, y, c):
            for a in range(n):
                _xchg_copy(src_refs, land_refs, send_sems, recv_sems, a, k, peer, me, scatter).start()
        token[...] = jnp.zeros_like(token)

    res = pl.pallas_call(
        body, name=name,
        out_shape=(pltpu.SemaphoreType.DMA((n * N_PEERS,)), pltpu.SemaphoreType.DMA((n * N_PEERS,)),
                   *[pltpu.HBM(s.shape, s.dtype) for s in srcs], *[pltpu.HBM(s.shape, s.dtype) for s in lands],
                   jax.ShapeDtypeStruct((8, 128), F32)),
        in_specs=[HBM] * (2 * n) + [ANY] * nd,
        out_specs=(SEM, SEM, *([HBM] * (2 * n)), pl.BlockSpec(memory_space=pltpu.VMEM)),
        input_output_aliases={i: 2 + i for i in range(2 * n)},
        compiler_params=pltpu.CompilerParams(has_side_effects=EFFECT),
    )(*[pltpu.with_memory_space_constraint(s, pltpu.HBM) for s in srcs],
      *[pltpu.with_memory_space_constraint(s, pltpu.HBM) for s in lands], *deps)
    return res[0], res[1], res[2:2 + n], res[2 + n:2 + 2 * n], res[-1]


def _xchg_wait(name, send_sems, recv_sems, srcs, lands, after, scatter):
    n = len(srcs)

    def body(*refs):
        src_refs, land_refs = refs[:n], refs[n:2 * n]
        s_sems, r_sems = refs[2 * n], refs[2 * n + 1]
        x, y, c = _mesh_pos()
        me = _slot(x, y, c)
        for k, peer in _peers(x, y, c):
            for a in range(n):
                cp = _xchg_copy(src_refs, land_refs, s_sems, r_sems, a, k, peer, me, scatter)
                cp.wait_send()
                cp.wait_recv()

    res = pl.pallas_call(
        body, name=name,
        out_shape=[pltpu.HBM(s.shape, s.dtype) for s in srcs] + [pltpu.HBM(s.shape, s.dtype) for s in lands],
        in_specs=[HBM] * (2 * n) + [SEM, SEM, ANY], out_specs=[HBM] * (2 * n),
        input_output_aliases={i: i for i in range(2 * n)},
        compiler_params=pltpu.CompilerParams(has_side_effects=EFFECT),
    )(*srcs, *lands, send_sems, recv_sems, after)
    return res[n:]


def _landing(own, me):
    return lax.dynamic_update_slice_in_dim(lax.empty((NDEV,) + own.shape, own.dtype), own[None], me, axis=0)


STAGES = ("l0_mixer", "l0_ffn", "l1_mixer", "l1_ffn")
STAGE_LAYOUT = {"l0_mixer": (AB_SEGS, AB_P), "l1_mixer": (SSD_SEGS, SSD_P_W)}


class _Exchange:
    def __init__(self, shards, me):
        self.shards, self.me = shards, me
        self.pending, self.pending_grads, self.recv = None, None, {}

    def _layout(self, stage):
        ws = self.shards[stage][0].shape[-1]
        return STAGE_LAYOUT.get(stage, (((0, NDEV * ws),), NDEV * ws)) + (ws,)

    def _start_gather(self, stage, deps):
        srcs = list(self.shards[stage])
        lands = [_landing(s, self.me) for s in srcs]
        return _xchg_start("gather_start_" + stage, srcs, lands, deps, False)

    def get(self, stage, dep, thread):
        i = STAGES.index(stage)
        if i == 0:
            g_in, g_out = _ag_big("gather_" + stage, list(self.shards[stage]))
            deps = [g_out, dep]
        else:
            ss, rs, srcs, lands, _ = self.pending
            g_in, g_out = _xchg_wait("gather_wait_" + stage, ss, rs, srcs, lands, dep, False)
            self.pending, deps = None, [g_out]
        if i + 1 < len(STAGES):
            self.pending = self._start_gather(STAGES[i + 1], deps)
            thread = thread + self.pending[4][0, 0]
        segs, wp, _ = self._layout(stage)
        return _repack("repack_" + stage, g_in, segs, wp), g_out.reshape(-1, D), thread

    def put(self, stage, d_in, d_out, thread):
        segs, _, ws = self._layout(stage)
        parts = [_unpack("unpack_" + stage, d_in, segs, ws, MXU_DTYPE), d_out.reshape(NDEV, -1, D)]
        deps = [parts[0]]
        if self.pending_grads is not None:
            deps = [self.finish(parts[0])[0]]
        self.staged = (stage, parts)
        return thread if stage == STAGES[0] else thread + self.start_last(deps)[0, 0]

    def start_last(self, deps):
        stage, parts = self.staged
        lands = [_landing(lax.dynamic_index_in_dim(p, self.me, 0, keepdims=False), self.me) for p in parts]
        self.pending_grads = (stage,) + _xchg_start("scatter_start_" + stage, parts, lands, deps, True)
        return self.pending_grads[5]

    def finish(self, after):
        stage, ss, rs, srcs, lands, _ = self.pending_grads
        self.recv[stage] = _xchg_wait("scatter_wait_" + stage, ss, rs, srcs, lands, after, True)
        self.pending_grads = None
        return self.recv[stage]


def _ffn_fwd(tag, h, w_in, w_out, nb, cb):
    pf = _mm(tag + "_ffn_in", h, w_in, "nn", F32)
    (act,) = _rowwise(tag + "_swiglu", _fn_swiglu, nb, cb, [_row(pf)], [], [(D_FF, MXU_DTYPE)])
    return pf, act, _mm(tag + "_ffn_out", act, w_out, "nn", F32)


def _ffn_bwd(tag, h, pf, act, df, w_in, w_out, nb, cb):
    dw_out = _mm(tag + "_ffn_out_dw", act, df, "tn", MXU_DTYPE)
    dact = _mm(tag + "_ffn_out_dx", df, w_out, "nt", MXU_DTYPE)
    (dpf,), _ = _rowwise_vjp(tag + "_swiglu_bwd", _fn_swiglu, nb, cb, [_row(pf)], [], [_row(dact)], [(0, MXU_DTYPE, None)])
    dw_in = _mm(tag + "_ffn_in_dw", h, dpf, "tn", MXU_DTYPE)
    dh = _mm(tag + "_ffn_in_dx", dpf, w_in, "nt", MXU_DTYPE)
    return dw_out, dw_in, dh


def _local_step(x, ctx, target, mod, P, comm):
    T = x.shape[0]
    N = T + CTX
    nb, cb = N // TM, N // TM - 1
    R = T // GRID_W
    mod0, mod1 = mod[0], mod[1]
    ng = P["norm_g"]
    g00, g01, g10, g11 = ng[0, 0][None], ng[0, 1][None], ng[1, 0][None], ng[1, 1][None]
    pre = functools.partial(_fn_prenorm, a=0, b=1)
    rpre = functools.partial(_fn_resid_prenorm, gi=2, a=3, b=4)
    res5 = functools.partial(_fn_resid, gi=5)
    dirs = (("f", False), ("b", True))

    xc0 = jnp.concatenate([x, ctx], axis=0)
    w_ab_in, w_ab_out, g00 = comm.get("l0_mixer", mod, g00)
    (h0,) = _rowwise("l0_prenorm", pre, nb, cb, [_row(xc0)], [g00, mod0], [(D, MXU_DTYPE)])
    p0 = _mm("l0_in", h0, w_ab_in, "nn", F32)
    gla_rows = [(p0, 512, 0), (p0, 256, 8), (p0, 256, 9), (p0, 128, 20)]
    gla_blk = _multi_chunk(_gla_chunk, GLA_L, TM // GLA_L, len(gla_rows))
    gla_par = {d: [P["ab_gate_w"][int(r)], P["ab_gate_b"][int(r)][None]] for d, r in dirs}
    gla_state = (GLA_DV, GLA_H * GLA_DK)
    o, st0 = {}, {}
    for d, rev in dirs:
        o[d], st0[d] = _scan_fwd("gla_fwd_" + d, gla_blk, TM, nb, cb, rev, gla_rows, gla_par[d], gla_state, GLA_H * GLA_DV)
    n128, cb128 = N // GMLP_L, T // GMLP_L
    mix_rows = [_row(o["f"], tm=GMLP_L), _row(o["b"], tm=GMLP_L)] + [_row(p0, 512, j, tm=GMLP_L) for j in (1, 2, 3)]
    mix_par = [P["ab_gla_norm_g"], P["ab_vnorm_g"], P["ab_spatial_w"].reshape(GMLP_G * GMLP_L, GMLP_L), P["ab_spatial_b"].T]
    (cat0,) = _rowwise("l0_mix", _fn_mixpost, n128, cb128, mix_rows, mix_par, [(D, MXU_DTYPE)], tm=GMLP_L)
    y0 = _mm("l0_out", cat0, w_ab_out, "nn", F32)
    w_fi0, w_fo0, g01 = comm.get("l0_ffn", y0, g01)
    x1, h1 = _rowwise("l0_ffn_prenorm", rpre, nb, cb, [_row(xc0), _row(y0)], [g01, mod0, mod0], [(D, F32), (D, MXU_DTYPE)])
    pf0, act0, f0 = _ffn_fwd("l0", h1, w_fi0, w_fo0, nb, cb)
    (x2,) = _rowwise("l0_resid", res5, nb, cb, [_row(x1), _row(f0)], [mod0], [(D, F32)])
    x2p = _perm("to_col_major", x2, R, nb)

    w_ssd_in, w_ssd_out, g10 = comm.get("l1_mixer", x2p, g10)
    (h2,) = _rowwise("l1_prenorm", pre, nb, cb, [_row(x2p)], [g10, mod1], [(D, MXU_DTYPE)])
    p1 = _mm("l1_in", h2, w_ssd_in, "nn", F32)
    conv_w8 = jnp.concatenate([P["ssd_conv_w"], jnp.zeros((8 - SSD_K, 3 * CONV_W), F32)], axis=0)
    xbc = _conv("l1_conv", p1, conv_w8, P["ssd_conv_b"], nb, permuted_src=True, act=True, flip=False, out_dtype=F32)
    ssd_rows = [(xbc, SSD_INNER, 0), (xbc, 512, 4), (xbc, 512, 5), (p1, 128, 40)]
    ssd_blk = _multi_chunk(_ssd_chunk, SSD_L, TM // SSD_L, len(ssd_rows))
    ssd_par = {d: [P["ssd_dt_bias"][int(r)][None], P["ssd_a_log"][int(r)][None]] for d, r in dirs}
    ssd_state = (SSD_N, SSD_INNER)
    ys, st1 = {}, {}
    for d, rev in dirs:
        ys[d], st1[d] = _scan_fwd("ssd_fwd_" + d, ssd_blk, TM, nb, cb, rev, ssd_rows, ssd_par[d], ssd_state, SSD_INNER)
    fin_rows = [_row(ys["f"]), _row(ys["b"]), _row(xbc, SSD_INNER, 0), _row(p1, SSD_INNER, 1)]
    fin_par = [P["ssd_d"], P["ssd_norm_g"]]
    (yn,) = _rowwise("l1_finish", _fn_ssd_finish, nb, cb, fin_rows, fin_par, [(SSD_INNER, MXU_DTYPE)])
    y1 = _mm("l1_out", yn, w_ssd_out, "nn", F32)
    w_fi1, w_fo1, g11 = comm.get("l1_ffn", y1, g11)
    x3, h3 = _rowwise("l1_ffn_prenorm", rpre, nb, cb, [_row(x2p), _row(y1)], [g11, mod1, mod1], [(D, F32), (D, MXU_DTYPE)])
    pf1, act1, f1 = _ffn_fwd("l1", h3, w_fi1, w_fo1, nb, cb)
    loss, dx3, df1, dm1_j, d_final_g = _loss_head(x3, f1, target, mod1, P["final_norm_g"], nb, R)

    dP = {"final_norm_g": d_final_g}
    dwo1, dwi1, dh3 = _ffn_bwd("l1", h3, pf1, act1, df1, w_fi1, w_fo1, nb, cb)
    g11 = comm.put("l1_ffn", dwi1, dwo1, g11)
    (dx2p_a, dy1), (dg11, dm1_a, dm1_b) = _rowwise_vjp(
        "l1_ffn_prenorm_bwd", rpre, nb, cb, [_row(x2p), _row(y1)], [g11, mod1, mod1], [_row(dx3), _row(dh3)],
        [(0, F32, None), (1, MXU_DTYPE, None)])
    d_ssd_out = _mm("l1_out_dw", yn, dy1, "tn", MXU_DTYPE)
    dyn = _mm("l1_out_dx", dy1, w_ssd_out, "nt", MXU_DTYPE)
    (dys, dxs, dz), (dP["ssd_d"], dP["ssd_norm_g"]) = _rowwise_vjp(
        "l1_finish_bwd", _fn_ssd_finish, nb, cb, fin_rows, fin_par, [_row(dyn)],
        [(0, F32, None), (2, F32, None), (3, MXU_DTYPE, None)])
    dxp, dbp, dcp, dtl, ddtb, dalog = [], [], [], [], [], []
    for d, rev in dirs:
        (dx_, db_, dc_, dt_), (ddtb_, dalog_) = _scan_bwd("ssd_bwd_" + d, ssd_blk, TM, nb, cb, rev, ssd_rows, ssd_par[d],
                                                          st1[d], dys, ssd_state, SSD_INNER)
        dxp.append(dx_); dbp.append(db_); dcp.append(dc_); dtl.append(dt_); ddtb.append(ddtb_); dalog.append(dalog_)
    dP["ssd_dt_bias"] = jnp.concatenate(ddtb, axis=0)
    dP["ssd_a_log"] = jnp.concatenate(dalog, axis=0)
    dacc, dcw8, dP["ssd_conv_b"] = _conv_bwd_pre("l1_conv_bwd", p1, conv_w8, P["ssd_conv_b"],
                                                  (dxp + [dxs], [dbp[0], dbp[1], dcp[0], dcp[1]]), nb)
    dP["ssd_conv_w"] = dcw8[:SSD_K]
    dpc = _conv("l1_conv_dx", dacc, conv_w8, jnp.zeros((1, 3 * CONV_W), F32), nb, permuted_src=False, act=False, flip=True,
                out_dtype=MXU_DTYPE)
    cat1 = functools.partial(_fn_concat, sums=(1, 1, 1, 2), pad=SSD_P_W - 5248)
    (dp1,) = _rowwise("l1_dp", cat1, nb, cb, [_row(dpc, SSD_INNER, 0), _row(dz), _row(dpc, 1024, 2), _row(dtl[0]), _row(dtl[1])],
                      [], [(SSD_P_W, MXU_DTYPE)])
    g10 = comm.put("l1_mixer", _mm("l1_in_dw", h2, dp1, "tn", F32), d_ssd_out, g10)
    dh2 = _mm("l1_in_dx", dp1, w_ssd_in, "nt", MXU_DTYPE)
    (dx2p,), (dg10, dm1_f) = _rowwise_vjp("l1_prenorm_bwd", pre, nb, cb, [_row(x2p)], [g10, mod1], [_row(dh2)],
                                          [(0, F32, _row(dx2p_a))])
    dx2 = _perm("to_row_major", dx2p, GRID_W, nb)

    (dx1_a, df0), (dm0_e,) = _rowwise_vjp("l0_resid_bwd", res5, nb, cb, [_row(x1), _row(f0)], [mod0], [_row(dx2)],
                                          [(0, F32, None), (1, MXU_DTYPE, None)])
    dwo0, dwi0, dh1 = _ffn_bwd("l0", h1, pf0, act0, df0, w_fi0, w_fo0, nb, cb)
    g01 = comm.put("l0_ffn", dwi0, dwo0, g01)
    (dxc0_a, dy0), (dg01, dm0_a, dm0_b) = _rowwise_vjp(
        "l0_ffn_prenorm_bwd", rpre, nb, cb, [_row(xc0), _row(y0)], [g01, mod0, mod0], [_row(dx1_a), _row(dh1)],
        [(0, F32, None), (1, MXU_DTYPE, None)])
    d_ab_out = _mm("l0_out_dw", cat0, dy0, "tn", MXU_DTYPE)
    dcat0 = _mm("l0_out_dx", dy0, w_ab_out, "nt", MXU_DTYPE)
    (do, dr, du, dgm), (dP["ab_gla_norm_g"], dP["ab_vnorm_g"], dsw, dsb_t) = _rowwise_vjp(
        "l0_mix_bwd", _fn_mixpost, n128, cb128, mix_rows, mix_par, [_row(dcat0, tm=GMLP_L)],
        [(0, F32, None), (2, MXU_DTYPE, None), (3, MXU_DTYPE, None), (4, MXU_DTYPE, None)], tm=GMLP_L)
    dP["ab_spatial_w"] = dsw.reshape(GMLP_G, GMLP_L, GMLP_L)
    dP["ab_spatial_b"] = dsb_t.T
    gl, dgw, dgb = [], [], []
    for d, rev in dirs:
        g4, (dgw_, dgb_) = _scan_bwd("gla_bwd_" + d, gla_blk, TM, nb, cb, rev, gla_rows, gla_par[d], st0[d], do,
                                     gla_state, GLA_H * GLA_DV)
        gl.append(g4); dgw.append(dgw_[None]); dgb.append(dgb_)
    dP["ab_gate_w"] = jnp.concatenate(dgw, axis=0)
    dP["ab_gate_b"] = jnp.concatenate(dgb, axis=0)
    cat0f = functools.partial(_fn_concat, sums=(2, 1, 1, 1, 2, 2, 2))
    (dp0,) = _rowwise("l0_dp", cat0f, nb, cb,
                      [_row(gl[0][0]), _row(gl[1][0]), _row(dr), _row(du), _row(dgm), _row(gl[0][1]), _row(gl[1][1]),
                       _row(gl[0][2]), _row(gl[1][2]), _row(gl[0][3]), _row(gl[1][3])], [], [(AB_P, MXU_DTYPE)])
    g00 = comm.put("l0_mixer", _mm("l0_in_dw", h0, dp0, "tn", F32), d_ab_out, g00)
    dh0 = _mm("l0_in_dx", dp0, w_ab_in, "nt", MXU_DTYPE)
    (dxc0,), (dg00, dm0_s) = _rowwise_vjp("l0_prenorm_bwd", pre, nb, cb, [_row(xc0)], [g00, mod0], [_row(dh0)],
                                          [(0, F32, _row(dxc0_a))])
    dP["norm_g"] = jnp.concatenate([dg00, dg01, dg10, dg11], axis=0).reshape(2, 2, D)
    dmod = jnp.stack([dm0_s + dm0_a + dm0_b + dm0_e, dm1_f + dm1_a + dm1_b + dm1_j])
    return loss, dxc0[:T], dmod, dP


WEIGHTS = ("c_ctx", "mod_w", "mod_b", "norm_g", "ffn_w_in", "ffn_w_out", "ab_w_in", "ab_gate_w", "ab_gate_b", "ab_gla_norm_g",
           "ab_vnorm_g", "ab_spatial_w", "ab_spatial_b", "ab_w_out", "ssd_w_in", "ssd_conv_w", "ssd_conv_b", "ssd_dt_bias",
           "ssd_a_log", "ssd_d", "ssd_norm_g", "ssd_w_out", "final_norm_g")
SMALL_SHARDED = ("norm_g", "ab_gate_w", "ab_gate_b", "ssd_conv_w", "ssd_conv_b", "ssd_norm_g")
SMALL = ("c_ctx", "mod_b", "norm_g", "ab_gate_w", "ab_gate_b", "ab_gla_norm_g", "ab_vnorm_g", "ab_spatial_w", "ab_spatial_b",
         "ssd_conv_w", "ssd_conv_b", "ssd_dt_bias", "ssd_a_log", "ssd_d", "ssd_norm_g", "final_norm_g")
LANES = 1024


def _pack(arrs, rows_multiple=8):
    flat = jnp.concatenate([a.reshape(-1).astype(F32) for a in arrs])
    rows = -(-flat.shape[0] // LANES)
    rows = -(-rows // rows_multiple) * rows_multiple
    return jnp.pad(flat, (0, rows * LANES - flat.shape[0])).reshape(rows, LANES)


def _unpack_flat(buf, shapes):
    lead = buf.shape[:-2]
    flat = buf.reshape(lead + (-1,))
    out, o = [], 0
    for s in shapes:
        n = math.prod(s)
        out.append(flat[..., o:o + n].reshape(lead + tuple(s)))
        o += n
    return out


def _unshard(g):
    g = jnp.moveaxis(g, 0, -2)
    return g.reshape(g.shape[:-2] + (g.shape[-2] * g.shape[-1],))


def _my_shard(full, me, ws):
    return lax.dynamic_slice_in_dim(full, me * ws, ws, axis=full.ndim - 1)


def _silu_vjp(cvec, dsc):
    def body(c_ref, d_ref, o_ref):
        _, vjp = jax.vjp(jax.nn.silu, c_ref[...])
        o_ref[...] = vjp(d_ref[...])[0]

    return pl.pallas_call(body, name="c_ctx_bwd", out_shape=jax.ShapeDtypeStruct(cvec.shape, F32), compiler_params=_cp())(cvec, dsc)


def kernel(x, c, ctx, c_ctx, mod_w, mod_b, norm_g, ffn_w_in, ffn_w_out, ab_w_in, ab_gate_w, ab_gate_b, ab_gla_norm_g, ab_vnorm_g, ab_spatial_w, ab_spatial_b, ab_w_out, ssd_w_in, ssd_conv_w, ssd_conv_b, ssd_dt_bias, ssd_a_log, ssd_d, ssd_norm_g, ssd_w_out, final_norm_g, loss_target, m_c_ctx, m_mod_w, m_mod_b, m_norm_g, m_ffn_w_in, m_ffn_w_out, m_ab_w_in, m_ab_gate_w, m_ab_gate_b, m_ab_gla_norm_g, m_ab_vnorm_g, m_ab_spatial_w, m_ab_spatial_b, m_ab_w_out, m_ssd_w_in, m_ssd_conv_w, m_ssd_conv_b, m_ssd_dt_bias, m_ssd_a_log, m_ssd_d, m_ssd_norm_g, m_ssd_w_out, m_final_norm_g, v_c_ctx, v_mod_w, v_mod_b, v_norm_g, v_ffn_w_in, v_ffn_w_out, v_ab_w_in, v_ab_gate_w, v_ab_gate_b, v_ab_gla_norm_g, v_ab_vnorm_g, v_ab_spatial_w, v_ab_spatial_b, v_ab_w_out, v_ssd_w_in, v_ssd_conv_w, v_ssd_conv_b, v_ssd_dt_bias, v_ssd_a_log, v_ssd_d, v_ssd_norm_g, v_ssd_w_out, v_final_norm_g):
    a = dict(locals())
    me = _slot(*_mesh_pos())
    ws_mod = mod_w.shape[-1]

    fwd_small = [c] + [a[k] for k in SMALL_SHARDED]
    g_small = _ag_small("gather_small", _pack(fwd_small))
    parts = _unpack_flat(g_small, [t.shape for t in fwd_small])
    c_rows = parts[0].reshape(NDEV, D)
    full = {k: _unshard(p) for k, p in zip(SMALL_SHARDED, parts[1:])}
    c_all = jnp.concatenate([c_rows, c_ctx[None], jnp.zeros((7, D), F32)], axis=0)
    m_all = _ag_small("gather_mod", _mod_fwd(c_all, mod_w).reshape(2 * 16, ws_mod)).reshape(NDEV, 2, 16, ws_mod)
    m_mine = lax.dynamic_index_in_dim(m_all, me, axis=2, keepdims=False)
    mx = jnp.moveaxis(m_mine, 0, 1).reshape(2, N_MOD, D) + mod_b.reshape(2, N_MOD, D)
    mc = jnp.moveaxis(m_all[:, :, 8, :], 0, 1).reshape(2, N_MOD, D) + mod_b.reshape(2, N_MOD, D)
    pad2 = jnp.zeros((2, 2, D), F32)
    mod = jnp.concatenate([mx, pad2, mc, pad2], axis=1)

    big = {"l0_mixer": (ab_w_in[0], ab_w_out[0]), "l0_ffn": (ffn_w_in[0], ffn_w_out[0]),
           "l1_mixer": (ssd_w_in[0], ssd_w_out[0]), "l1_ffn": (ffn_w_in[1], ffn_w_out[1])}
    comm = _Exchange({k: tuple(w.astype(MXU_DTYPE) for w in v) for k, v in big.items()}, me)
    P = {
        "norm_g": full["norm_g"], "ab_gate_w": full["ab_gate_w"][0], "ab_gate_b": full["ab_gate_b"][0],
        "ab_gla_norm_g": ab_gla_norm_g, "ab_vnorm_g": ab_vnorm_g, "ab_spatial_w": ab_spatial_w[0], "ab_spatial_b": ab_spatial_b[0],
        "ssd_conv_w": full["ssd_conv_w"][0], "ssd_conv_b": full["ssd_conv_b"], "ssd_dt_bias": ssd_dt_bias[0],
        "ssd_a_log": ssd_a_log[0], "ssd_d": ssd_d, "ssd_norm_g": full["ssd_norm_g"], "final_norm_g": final_norm_g[None],
    }

    loss, grad_x, dmod, dP = _local_step(x[0], ctx[0], loss_target[0], mod, P, comm)

    dmx, dmc = dmod[:, 0:N_MOD].reshape(2, N_MOD * D), dmod[:, 8:8 + N_MOD].reshape(2, N_MOD * D)
    small_names = ("ab_gate_w", "ab_gate_b", "ab_gla_norm_g", "ab_vnorm_g", "ab_spatial_w", "ab_spatial_b", "norm_g", "ssd_conv_w",
                   "ssd_conv_b", "ssd_dt_bias", "ssd_a_log", "ssd_d", "ssd_norm_g", "final_norm_g")
    bwd_small = [dP[k] for k in small_names] + [dmc, dmx]
    shapes = [t.shape for t in bwd_small]
    g_bwd = _ag_small("gather_small_grads", _pack(bwd_small))
    summed = _unpack_flat(_sum_parts("sum_small_grads", g_bwd), shapes)
    gfull = dict(zip(small_names, summed[:-2]))
    dmc_sum, dmx_sum = summed[-2], summed[-1]
    dmx_all = _unpack_flat(g_bwd, shapes)[-1]
    dmx_sh = jnp.moveaxis(_my_shard(dmx_all, me, ws_mod), 0, 1)
    dm = jnp.concatenate([dmx_sh, _my_shard(dmc_sum, me, ws_mod)[:, None, :], jnp.zeros((2, 7, ws_mod), F32)], axis=1)
    d_mod_w, dsc = _mod_bwd(c_all, mod_w, dm)
    dsc_ctx = (dsc[0, 8] + dsc[1, 8])[None]
    dsc_all = _ag_small("gather_c_ctx_grad", jnp.concatenate([dsc_ctx, jnp.zeros((7, D), F32)], axis=0))
    d_c_ctx = _silu_vjp(c_ctx[None], _sum_parts("sum_c_ctx_grad", dsc_all)[0:1])[0]

    g_small_w = {
        "c_ctx": d_c_ctx, "mod_b": dmx_sum + dmc_sum, "norm_g": gfull["norm_g"], "ab_gate_w": gfull["ab_gate_w"][None],
        "ab_gate_b": gfull["ab_gate_b"][None], "ab_gla_norm_g": gfull["ab_gla_norm_g"], "ab_vnorm_g": gfull["ab_vnorm_g"],
        "ab_spatial_w": gfull["ab_spatial_w"][None], "ab_spatial_b": gfull["ab_spatial_b"][None], "ssd_conv_w": gfull["ssd_conv_w"][None],
        "ssd_conv_b": gfull["ssd_conv_b"], "ssd_dt_bias": gfull["ssd_dt_bias"][None], "ssd_a_log": gfull["ssd_a_log"][None],
        "ssd_d": gfull["ssd_d"], "ssd_norm_g": gfull["ssd_norm_g"], "final_norm_g": gfull["final_norm_g"][0],
    }
    for k in SMALL_SHARDED:
        g_small_w[k] = _my_shard(g_small_w[k], me, a[k].shape[-1])
    token = comm.start_last([d_c_ctx])
    res = _adam("adam_small", _pack([a[k] for k in SMALL]), _pack([g_small_w[k] for k in SMALL])[None],
                _pack([a["m_" + k] for k in SMALL]), _pack([a["v_" + k] for k in SMALL]), token)
    out = {k: vals for k, vals in zip(SMALL, zip(*[_unpack_flat(r, [a[k].shape for k in SMALL]) for r in res]))}

    def adam_big(name, w2d, parts3d, m2d, v2d, shape):
        return tuple(r.reshape(shape) for r in _adam(name, w2d, parts3d, m2d, v2d, token))

    out["mod_w"] = adam_big("adam_mod_w", mod_w.reshape(-1, ws_mod), d_mod_w.reshape(1, -1, ws_mod), m_mod_w.reshape(-1, ws_mod),
                            v_mod_w.reshape(-1, ws_mod), mod_w.shape)
    ffn = {}
    for i, stage in enumerate(("l0_ffn", "l1_ffn")):
        for j, k in enumerate(("ffn_w_in", "ffn_w_out")):
            ffn[k, i] = adam_big("adam_%s%d" % (k, i), a[k][i], comm.recv[stage][j], a["m_" + k][i], a["v_" + k][i], a[k].shape[1:])
    for k in ("ffn_w_in", "ffn_w_out"):
        out[k] = tuple(jnp.stack(t) for t in zip(ffn[k, 0], ffn[k, 1]))
    for j, k in enumerate(("ssd_w_in", "ssd_w_out")):
        out[k] = adam_big("adam_" + k, a[k][0], comm.recv["l1_mixer"][j], a["m_" + k][0], a["v_" + k][0], a[k].shape)
    recv_ab = comm.finish(out["ssd_w_out"][3])
    for j, k in enumerate(("ab_w_in", "ab_w_out")):
        out[k] = adam_big("adam_" + k, a[k][0], recv_ab[j], a["m_" + k][0], a["v_" + k][0], a[k].shape)

    loss_all = lax.psum(loss[0, 0], ("x", "y", "c"))
    return (loss_all, grad_x[None], *[out[k][0] for k in WEIGHTS], *[out[k][1] for k in WEIGHTS],
            *[out[k][2] for k in WEIGHTS], *[out[k][3] for k in WEIGHTS])
```

```python
import functools
import math

import jax
import jax.numpy as jnp
from jax import lax
from jax.experimental import pallas as pl
from jax.experimental.pallas import tpu as pltpu

F32 = jnp.float32
BF16 = jnp.bfloat16
MXU_DTYPE = jnp.bfloat16
HI = lax.Precision.HIGHEST

D = 1024
NDEV = 8
N_MOD = 6
EPS = 1e-6
GRID_W = 64
CTX = 256
TM = 256
D_FF = 2816
GLA_H, GLA_DK, GLA_DV, GLA_LR, GLA_TAU, GLA_L = 4, 64, 128, 16, 16.0, 64
GMLP_G, GMLP_C, GMLP_L = 4, 128, 128
SSD_H, SSD_P, SSD_G, SSD_N, SSD_L, SSD_K = 32, 64, 4, 128, 128, 5
SSD_INNER = SSD_H * SSD_P
AB_IN = 2592
SSD_IN = 5184
AB_SEGS = ((256, 768), (1056, 1568), (1568, 2080), (2080, 2592), (0, 256), (800, 1056), (768, 800))
AB_P = 2688
SSD_SEGS = ((0, 2048), (3136, 5184), (2048, 2560), (2560, 3072), (3072, 3136))
SSD_P_W = 5376
VMEM_LIMIT = 56 * 1024 * 1024
TN_MAX_N = 5632
TN_ACC_BYTES = 12 * 1024 * 1024

ADAM_LR, ADAM_B1, ADAM_B2, ADAM_EPS, ADAM_WD, ADAM_STEP = 0.001, 0.9, 0.999, 1e-08, 0.01, 10


def _cp(sem=None, **kw):
    return pltpu.CompilerParams(dimension_semantics=sem, vmem_limit_bytes=VMEM_LIMIT, **kw)


def _dot(a, b, dims=(((1,), (0,)), ((), ()))):
    return lax.dot_general(a.astype(MXU_DTYPE), b.astype(MXU_DTYPE), dims, preferred_element_type=F32)


def _dot_nt(a, b):
    return _dot(a, b, (((1,), (1,)), ((), ())))


def _dot_tn(a, b):
    return _dot(a, b, (((0,), (0,)), ((), ())))


def _dotx(a, b, dims=(((1,), (0,)), ((), ()))):
    return lax.dot_general(a, b, dims, precision=HI, preferred_element_type=F32)


def _rms(x):
    return x * lax.rsqrt(jnp.mean(x * x, axis=-1, keepdims=True) + EPS)


def _pick(n, prefs):
    for p in prefs:
        if n % p == 0:
            return p
    return n


def _row(arr, width=None, colblk=0, tm=TM):
    width = arr.shape[1] if width is None else width
    return (arr, pl.BlockSpec((tm, width), lambda i, c=colblk: (i, c)))


def _full_spec(p):
    nd = p.ndim
    return pl.BlockSpec(p.shape, lambda i, nd=nd: (0,) * nd)


def _rowwise(name, fn, n_blocks, ctx_blk, rows, params, outs, tm=TM):
    nr, npar = len(rows), len(params)

    def body(*refs):
        t = (pl.program_id(0) >= ctx_blk).astype(F32)
        rv = [r[...].astype(F32) for r in refs[:nr]]
        pv = [p[...] for p in refs[nr:nr + npar]]
        res = fn(t, rv, pv)
        for o_ref, o in zip(refs[nr + npar:], res):
            o_ref[...] = o.astype(o_ref.dtype)

    return pl.pallas_call(
        body, name=name, grid=(n_blocks,),
        in_specs=[s for _, s in rows] + [_full_spec(p) for p in params],
        out_specs=[pl.BlockSpec((tm, w), lambda i: (i, 0)) for w, _ in outs],
        out_shape=[jax.ShapeDtypeStruct((n_blocks * tm, w), dt) for w, dt in outs],
        compiler_params=_cp(("parallel",)),
    )(*[a for a, _ in rows], *params)


def _rowwise_vjp(name, fn, n_blocks, ctx_blk, rows, params, douts, row_grads, tm=TM):
    nr, npar, nd = len(rows), len(params), len(douts)
    adds = [a for _, _, a in row_grads if a is not None]
    na = len(adds)

    def body(*refs):
        i = pl.program_id(0)
        t = (i >= ctx_blk).astype(F32)
        rv = [r[...].astype(F32) for r in refs[:nr]]
        pv = [p[...] for p in refs[nr:nr + npar]]
        dv = [r[...].astype(F32) for r in refs[nr + npar:nr + npar + nd]]
        av = [r[...].astype(F32) for r in refs[nr + npar + nd:nr + npar + nd + na]]
        o_refs = refs[nr + npar + nd + na:]
        _, vjp = jax.vjp(lambda r, p: tuple(fn(t, r, p)), rv, pv)
        d_rows, d_params = vjp(tuple(dv))
        ai = 0
        for o_ref, (ri, _, addend) in zip(o_refs, row_grads):
            g = d_rows[ri]
            if addend is not None:
                g = g + av[ai]
                ai += 1
            o_ref[...] = g.astype(o_ref.dtype)
        p_refs = o_refs[len(row_grads):]

        @pl.when(i == 0)
        def _():
            for p_ref in p_refs:
                p_ref[...] = jnp.zeros_like(p_ref)

        for p_ref, g in zip(p_refs, d_params):
            p_ref[...] += g

    widths = [rows[ri][1].block_shape[1] for ri, _, _ in row_grads]
    res = pl.pallas_call(
        body, name=name, grid=(n_blocks,),
        in_specs=[s for _, s in rows] + [_full_spec(p) for p in params] + [s for _, s in douts] + [s for _, s in adds],
        out_specs=[pl.BlockSpec((tm, w), lambda i: (i, 0)) for w in widths] + [_full_spec(p) for p in params],
        out_shape=[jax.ShapeDtypeStruct((n_blocks * tm, w), dt) for w, (_, dt, _) in zip(widths, row_grads)]
        + [jax.ShapeDtypeStruct(p.shape, F32) for p in params],
        compiler_params=_cp(("arbitrary",)),
    )(*[a for a, _ in rows], *params, *[a for a, _ in douts], *[a for a, _ in adds])
    return res[:len(row_grads)], res[len(row_grads):]


def _mm(name, a, b, mode, out_dtype):
    if mode == "nn":
        m, kk = a.shape
        n = b.shape[1]
    elif mode == "nt":
        m, kk = a.shape
        n = b.shape[0]
    else:
        kk, m = a.shape
        n = b.shape[1]
    if mode == "tn":
        tn = n if n <= TN_MAX_N else _pick(n, (512, 384, 256, 128))
        tm = next((t for t in (2048, 1408, 1024, 512, 256, 128) if m % t == 0 and t * tn * 4 <= TN_ACC_BYTES), 128)
        tk = _pick(kk, (544, 512, 256, 128))
    else:
        tm = _pick(m, (1088, 1024, 768, 512, 384, 256, 128))
        tn = _pick(n, (512, 384, 256, 128))
        tk = kk if kk <= 2816 else _pick(kk, (1792, 1408, 1024, 896, 768, 512, 384, 256, 128))
    nk = kk // tk
    in_place = out_dtype == F32
    if mode == "nn":
        specs = [pl.BlockSpec((tm, tk), lambda i, j, k: (i, k)), pl.BlockSpec((tk, tn), lambda i, j, k: (k, j))]
        dims = (((1,), (0,)), ((), ()))
    elif mode == "nt":
        specs = [pl.BlockSpec((tm, tk), lambda i, j, k: (i, k)), pl.BlockSpec((tn, tk), lambda i, j, k: (j, k))]
        dims = (((1,), (1,)), ((), ()))
    else:
        specs = [pl.BlockSpec((tk, tm), lambda i, j, k: (k, i)), pl.BlockSpec((tk, tn), lambda i, j, k: (k, j))]
        dims = (((0,), (0,)), ((), ()))

    def body(a_ref, b_ref, o_ref, *scratch):
        part = lax.dot_general(a_ref[...].astype(MXU_DTYPE), b_ref[...].astype(MXU_DTYPE), dims, preferred_element_type=F32)
        if nk == 1:
            o_ref[...] = part.astype(o_ref.dtype)
        else:
            k = pl.program_id(2)
            acc = o_ref if in_place else scratch[0]

            @pl.when(k == 0)
            def _():
                acc[...] = part

            @pl.when(k > 0)
            def _():
                acc[...] += part

            if not in_place:
                @pl.when(k == nk - 1)
                def _():
                    o_ref[...] = acc[...].astype(o_ref.dtype)

    return pl.pallas_call(
        body, name=name, grid=(m // tm, n // tn, nk), in_specs=specs,
        out_specs=pl.BlockSpec((tm, tn), lambda i, j, k: (i, j)),
        out_shape=jax.ShapeDtypeStruct((m, n), out_dtype),
        scratch_shapes=[] if nk == 1 or in_place else [pltpu.VMEM((tm, tn), F32)],
        compiler_params=_cp(("parallel", "parallel", "arbitrary")),
    )(a, b)


def _sel_mod(modp, t):
    return modp[0:8] * (1.0 - t) + modp[8:16] * t


def _fn_prenorm(t, rows, params, *, a, b):
    (x,), (g, modp) = rows, params
    m = _sel_mod(modp, t)
    return ((_rms(x) * g) * (1.0 + m[b:b + 1]) + m[a:a + 1],)


def _fn_resid_prenorm(t, rows, params, *, gi, a, b):
    (x, y), (g, mod_a, mod_b) = rows, params
    ma, mb = _sel_mod(mod_a, t), _sel_mod(mod_b, t)
    xn = x + ma[gi:gi + 1] * y
    return xn, (_rms(xn) * g) * (1.0 + mb[b:b + 1]) + mb[a:a + 1]


def _fn_resid(t, rows, params, *, gi):
    (x, y), (mod_a,) = rows, params
    return (x + _sel_mod(mod_a, t)[gi:gi + 1] * y,)


def _fn_swiglu(t, rows, params):
    (pf,) = rows
    return (jax.nn.silu(pf[:, :D_FF]) * pf[:, D_FF:],)


def _fn_mixpost(t, rows, params):
    (o_f, o_b, r, u, g), (gla_g, vn_g, sw, sb_t) = rows, params
    o = o_f + o_b
    a = jnp.concatenate([_rms(o[:, h * GLA_DV:(h + 1) * GLA_DV]) for h in range(GLA_H)], axis=1) * gla_g * jax.nn.silu(r)
    uu, vv = jax.nn.gelu(u), jax.nn.gelu(g)
    mu = jnp.mean(vv, axis=-1, keepdims=True)
    var = jnp.mean(jnp.square(vv - mu), axis=-1, keepdims=True)
    vn = ((vv - mu) * lax.rsqrt(var + EPS)) * vn_g
    s = jnp.concatenate(
        [_dot(sw[gi * GMLP_L:(gi + 1) * GMLP_L, :], vn[:, gi * GMLP_C:(gi + 1) * GMLP_C]) + sb_t[:, gi:gi + 1]
         for gi in range(GMLP_G)], axis=1)
    return (jnp.concatenate([a, uu * s], axis=1),)


def _head_expand():
    r = lax.broadcasted_iota(jnp.int32, (SSD_H, SSD_INNER), 0)
    c = lax.broadcasted_iota(jnp.int32, (SSD_H, SSD_INNER), 1)
    return (c // SSD_P == r).astype(F32)


def _fn_ssd_finish(t, rows, params):
    (y_f, y_b, xs, z), (d_skip, norm_g) = rows, params
    d_full = _dotx(jnp.broadcast_to(d_skip, (8, SSD_H)), _head_expand())[0:1]
    y = (y_f + y_b + d_full * xs) * jax.nn.silu(z)
    gw = SSD_INNER // SSD_G
    return (jnp.concatenate([_rms(y[:, gi * gw:(gi + 1) * gw]) for gi in range(SSD_G)], axis=1) * norm_g,)


def _fn_concat(t, rows, params, *, sums, pad=0):
    out, i = [], 0
    for n in sums:
        acc = rows[i]
        for j in range(1, n):
            acc = acc + rows[i + j]
        out.append(acc)
        i += n
    if pad:
        out.append(jnp.zeros((out[0].shape[0], pad), F32))
    return (jnp.concatenate(out, axis=1),)


def _tri(n, rev):
    r = lax.broadcasted_iota(jnp.int32, (n, n), 0)
    c = lax.broadcasted_iota(jnp.int32, (n, n), 1)
    return (r <= c) if rev else (r >= c)


def _gla_chunk(S, v, k, q, tail, gw, gb, *, rev):
    L = GLA_L
    msk = _tri(L, rev)
    tri = msk.astype(F32)
    lr = tail[:, GLA_LR:2 * GLA_LR] if rev else tail[:, 0:GLA_LR]
    la = jax.nn.log_sigmoid(_dot(lr, gw) + gb) / GLA_TAU
    b = _dotx(tri, la)
    b_last = b[0:1] if rev else b[L - 1:L]
    kd = k * jnp.exp(b_last - b)
    qd = (q * GLA_DK ** -0.5) * jnp.exp(b)
    ki = k * jnp.exp(-b)
    dec = jnp.exp(b_last)
    o_parts, s_parts = [], []
    for h in range(GLA_H):
        ks, vs = slice(h * GLA_DK, (h + 1) * GLA_DK), slice(h * GLA_DV, (h + 1) * GLA_DV)
        sh = S[:, ks]
        sc = jnp.where(msk, _dot_nt(qd[:, ks], ki[:, ks]), 0.0)
        o_parts.append(_dot_nt(qd[:, ks], sh) + _dot(sc, v[:, vs]))
        s_parts.append(dec[:, ks] * sh + _dot_tn(v[:, vs], kd[:, ks]))
    return jnp.concatenate(s_parts, axis=1), jnp.concatenate(o_parts, axis=1)


def _ssd_chunk(S, x, bm, cm, tail, dtb, alog, *, rev):
    L = SSD_L
    msk = _tri(L, rev)
    tri = msk.astype(F32)
    raw = tail[:, SSD_H:2 * SSD_H] if rev else tail[:, 0:SSD_H]
    dt = jax.nn.softplus(raw + dtb)
    dta = dt * (-jnp.exp(alog))
    acum = _dotx(tri, dta)
    a_last = acum[0:1] if rev else acum[L - 1:L]
    wst = dt * jnp.exp(a_last - acum)
    eac = jnp.exp(acum)
    tr = jnp.concatenate([acum, dt, wst, jnp.zeros((L, L - 3 * SSD_H), F32)], axis=1).T
    acum_t, dt_t, wst_t = tr[0:SSD_H], tr[SSD_H:2 * SSD_H], tr[2 * SSD_H:3 * SSD_H]
    decrow = jnp.exp(_dotx(jnp.broadcast_to(a_last, (8, SSD_H)), _head_expand())[0:1])
    lane = lax.broadcasted_iota(jnp.int32, (1, 2 * SSD_P), 1)
    m0 = (lane < SSD_P).astype(F32)
    m1 = 1.0 - m0
    pairs_per_group = SSD_H // SSD_G // 2
    y_parts, s_parts = [], []
    for g in range(SSD_G):
        ns = slice(g * SSD_N, (g + 1) * SSD_N)
        bg, cg = bm[:, ns], cm[:, ns]
        cb = _dot_nt(cg, bg)
        bgt = bg.T
        for jj in range(pairs_per_group):
            j = g * pairs_per_group + jj
            ls = slice(j * 2 * SSD_P, (j + 1) * 2 * SSD_P)
            xp, sp = x[:, ls], S[:, ls]
            xm = jnp.concatenate([xp * m0, xp * m1], axis=0)
            sm = jnp.concatenate([sp * m0, sp * m1], axis=0)
            lhs, bw = [], []
            for h in (2 * j, 2 * j + 1):
                seg = acum[:, h:h + 1] - acum_t[h:h + 1, :]
                lhs.append(cb * jnp.exp(jnp.where(msk, seg, -jnp.inf)) * dt_t[h:h + 1, :])
                bw.append(bgt * wst_t[h:h + 1, :])
            lhs += [cg * eac[:, h:h + 1] for h in (2 * j, 2 * j + 1)]
            y_parts.append(_dot(jnp.concatenate(lhs, axis=1), jnp.concatenate([xm, sm], axis=0)))
            s_parts.append(sp * decrow[:, ls] + _dot(jnp.concatenate(bw, axis=1), xm))
    return jnp.concatenate(s_parts, axis=1), jnp.concatenate(y_parts, axis=1)


def _multi_chunk(chunk_fn, L, subs, nr):
    def fn(S, *args, rev):
        rows, params = args[:nr], args[nr:]
        ys = [None] * subs
        for j in (range(subs - 1, -1, -1) if rev else range(subs)):
            S, ys[j] = chunk_fn(S, *[r[j * L:(j + 1) * L] for r in rows], *params, rev=rev)
        return S, jnp.concatenate(ys, axis=0)

    return fn


def _scan_order(n, nx, rev, backward):
    nc = n - nx

    def fwd(s):
        return (n - 1 - s) if rev else jnp.where(s < nc, s + nx, s - nc)

    return (lambda s: fwd(n - 1 - s)) if backward else fwd


def _scan_fwd(name, chunk_fn, L, n, nx, rev, rows, params, state_shape, out_w):
    order = _scan_order(n, nx, rev, False)
    nr, npar = len(rows), len(params)

    def body(*refs):
        s_scr = refs[-1]

        @pl.when(pl.program_id(0) == 0)
        def _():
            s_scr[...] = jnp.zeros_like(s_scr)

        s_in = s_scr[...]
        y_ref, st_ref = refs[nr + npar], refs[nr + npar + 1]
        st_ref[0] = s_in
        s_new, y = chunk_fn(s_in, *[r[...] for r in refs[:nr]], *[p[...] for p in refs[nr:nr + npar]], rev=rev)
        y_ref[...] = y
        s_scr[...] = s_new

    return pl.pallas_call(
        body, name=name, grid=(n,),
        in_specs=[pl.BlockSpec((L, w), lambda s, c=c: (order(s), c)) for _, w, c in rows] + [_full_spec(p) for p in params],
        out_specs=[pl.BlockSpec((L, out_w), lambda s: (order(s), 0)),
                   pl.BlockSpec((1,) + state_shape, lambda s: (order(s), 0, 0))],
        out_shape=[jax.ShapeDtypeStruct((n * L, out_w), F32), jax.ShapeDtypeStruct((n,) + state_shape, F32)],
        scratch_shapes=[pltpu.VMEM(state_shape, F32)],
        compiler_params=_cp(("arbitrary",)),
    )(*[a for a, _, _ in rows], *params)


def _scan_bwd(name, chunk_fn, L, n, nx, rev, rows, params, states, dy, state_shape, out_w):
    order = _scan_order(n, nx, rev, True)
    nr, npar = len(rows), len(params)

    def body(*refs):
        i = pl.program_id(0)
        ds_scr = refs[-1]
        rv = [r[...] for r in refs[:nr]]
        pv = [p[...] for p in refs[nr:nr + npar]]
        st_ref, dy_ref = refs[nr + npar], refs[nr + npar + 1]
        o_refs = refs[nr + npar + 2:-1]
        p_refs = o_refs[nr:]

        @pl.when(i == 0)
        def _():
            ds_scr[...] = jnp.zeros_like(ds_scr)
            for p_ref in p_refs:
                p_ref[...] = jnp.zeros_like(p_ref)

        _, vjp = jax.vjp(functools.partial(chunk_fn, rev=rev), st_ref[0], *rv, *pv)
        grads = vjp((ds_scr[...], dy_ref[...].astype(F32)))
        ds_scr[...] = grads[0]
        for o_ref, g in zip(o_refs[:nr], grads[1:1 + nr]):
            o_ref[...] = g
        for p_ref, g in zip(p_refs, grads[1 + nr:]):
            p_ref[...] += g

    res = pl.pallas_call(
        body, name=name, grid=(n,),
        in_specs=[pl.BlockSpec((L, w), lambda s, c=c: (order(s), c)) for _, w, c in rows] + [_full_spec(p) for p in params]
        + [pl.BlockSpec((1,) + state_shape, lambda s: (order(s), 0, 0)), pl.BlockSpec((L, out_w), lambda s: (order(s), 0))],
        out_specs=[pl.BlockSpec((L, w), lambda s: (order(s), 0)) for _, w, _ in rows] + [_full_spec(p) for p in params],
        out_shape=[jax.ShapeDtypeStruct((n * L, w), F32) for _, w, _ in rows] + [jax.ShapeDtypeStruct(p.shape, F32) for p in params],
        scratch_shapes=[pltpu.VMEM(state_shape, F32)],
        compiler_params=_cp(("arbitrary",)),
    )(*[a for a, _, _ in rows], *params, states, dy)
    return res[:nr], res[nr:]


CONV_W = 1024
CONV_COLBLK = (0, 1, 4)


def _conv_specs(nb, src_blk):
    halo = TM // 8
    return [pl.BlockSpec((TM, CONV_W), lambda j, i: (i, src_blk(j))),
            pl.BlockSpec((8, CONV_W), lambda j, i: (jnp.maximum(i * halo - 1, 0), src_blk(j))),
            pl.BlockSpec((8, CONV_W), lambda j, i: (jnp.minimum(i * halo + halo, nb * halo - 1), src_blk(j)))]


def _conv_ext(i, nb, cur, prev, nxt):
    has_prev = jnp.logical_and(i > 0, i < nb - 1)
    has_next = i < nb - 2
    return jnp.concatenate([jnp.where(has_prev, prev, 0.0), cur, jnp.where(has_next, nxt, 0.0)], axis=0)


def _conv_taps(ext, w, flip):
    acc = None
    for j in range(SSD_K):
        wj = w[SSD_K - 1 - j:SSD_K - j, :] if flip else w[j:j + 1, :]
        term = wj * ext[6 + j:6 + j + TM, :]
        acc = term if acc is None else acc + term
    return acc


def _conv(name, src, w8, b1, nb, *, permuted_src, act, flip, out_dtype):
    src_blk = (lambda j: jnp.where(j == 2, CONV_COLBLK[2], j)) if permuted_src else (lambda j: j)

    def body(cur, prev, nxt, w_ref, b_ref, o_ref):
        ext = _conv_ext(pl.program_id(1), nb, cur[...].astype(F32), prev[...].astype(F32), nxt[...].astype(F32))
        acc = _conv_taps(ext, w_ref[...], flip)
        if act:
            acc = jax.nn.silu(acc + b_ref[...])
        o_ref[...] = acc.astype(o_ref.dtype)

    return pl.pallas_call(
        body, name=name, grid=(3, nb),
        in_specs=_conv_specs(nb, src_blk) + [pl.BlockSpec((8, CONV_W), lambda j, i: (0, j)), pl.BlockSpec((1, CONV_W), lambda j, i: (0, j))],
        out_specs=pl.BlockSpec((TM, CONV_W), lambda j, i: (i, j)),
        out_shape=jax.ShapeDtypeStruct((nb * TM, 3 * CONV_W), out_dtype),
        compiler_params=_cp(("parallel", "parallel")),
    )(src, src, src, w8, b1)


def _conv_bwd_pre(name, p1, w8, b1, dxbc_parts, nb):
    src_blk = lambda j: jnp.where(j == 2, CONV_COLBLK[2], j)
    xs_parts, bc_parts = dxbc_parts
    n_x, n_bc = len(xs_parts), len(bc_parts)

    def body(*refs):
        cur, prev, nxt, w_ref, b_ref = refs[:5]
        d_refs = refs[5:5 + n_x + n_bc]
        da_ref, dw_ref, db_ref = refs[5 + n_x + n_bc:]
        j, i = pl.program_id(0), pl.program_id(1)
        ext = _conv_ext(i, nb, cur[...], prev[...], nxt[...])
        acc = _conv_taps(ext, w_ref[...], False) + b_ref[...]
        dx = d_refs[0][...]
        for r in d_refs[1:n_x]:
            dx = dx + r[...]
        dbc = jnp.concatenate([d_refs[n_x][...] + d_refs[n_x + 1][...], d_refs[n_x + 2][...] + d_refs[n_x + 3][...]], axis=1)
        dy = jnp.where(j == 2, dbc, dx)
        sg = jax.nn.sigmoid(acc)
        da = dy * (sg + acc * sg * (1.0 - sg))
        da_ref[...] = da

        @pl.when(i == 0)
        def _():
            dw_ref[...] = jnp.zeros_like(dw_ref)
            db_ref[...] = jnp.zeros_like(db_ref)

        rows = [jnp.sum(da * ext[6 + t:6 + t + TM, :], axis=0, keepdims=True) for t in range(SSD_K)]
        dw_ref[...] += jnp.concatenate(rows + [jnp.zeros((8 - SSD_K, CONV_W), F32)], axis=0)
        db_ref[...] += jnp.sum(da, axis=0, keepdims=True)

    x_specs = [pl.BlockSpec((TM, CONV_W), lambda j, i: (i, jnp.minimum(j, 1))) for _ in xs_parts]
    bc_specs = [pl.BlockSpec((TM, 512), lambda j, i: (i, 0)) for _ in bc_parts]
    return pl.pallas_call(
        body, name=name, grid=(3, nb),
        in_specs=_conv_specs(nb, src_blk) + [pl.BlockSpec((8, CONV_W), lambda j, i: (0, j)), pl.BlockSpec((1, CONV_W), lambda j, i: (0, j))]
        + x_specs + bc_specs,
        out_specs=[pl.BlockSpec((TM, CONV_W), lambda j, i: (i, j)), pl.BlockSpec((8, CONV_W), lambda j, i: (0, j)),
                   pl.BlockSpec((1, CONV_W), lambda j, i: (0, j))],
        out_shape=[jax.ShapeDtypeStruct((nb * TM, 3 * CONV_W), F32), jax.ShapeDtypeStruct((8, 3 * CONV_W), F32),
                   jax.ShapeDtypeStruct((1, 3 * CONV_W), F32)],
        compiler_params=_cp(("arbitrary", "arbitrary")),
    )(p1, p1, p1, w8, b1, *xs_parts, *bc_parts)


def _grid_rows(val, a, kb):
    return jnp.concatenate([val[:, t * D:(t + 1) * D] for t in range(kb)], axis=0)


def _perm(name, xc, a, nb):
    n = xc.shape[0]
    b = (n - CTX) // a
    kb = TM // a
    view = xc.reshape(n // b, b * D)

    def body(v_ref, c_ref, o_ref):
        i = pl.program_id(0)

        @pl.when(i < nb - 1)
        def _():
            o_ref[...] = _grid_rows(v_ref[...], a, kb)

        @pl.when(i == nb - 1)
        def _():
            o_ref[...] = c_ref[...]

    return pl.pallas_call(
        body, name=name, grid=(nb,),
        in_specs=[pl.BlockSpec((a, kb * D), lambda i: (0, jnp.minimum(i, nb - 2))), pl.BlockSpec((TM, D), lambda i: (nb - 1, 0))],
        out_specs=pl.BlockSpec((TM, D), lambda i: (i, 0)),
        out_shape=jax.ShapeDtypeStruct((n, D), xc.dtype),
        compiler_params=_cp(("parallel",)),
    )(view, xc)


def _loss_head(x, f, target, modp, g_final, nb, rows_r):
    t_tok = target.shape[0]
    kb = TM // rows_r
    tview = target.reshape(rows_r, (t_tok // rows_r) * D)

    def fn(x_, f_, tgt, modp_, g_, is_ctx):
        xn = x_ + _sel_mod(modp_, is_ctx)[5:6] * f_
        err = _rms(xn) * g_ - tgt
        return 0.5 * jnp.sum(jnp.mean(err * err, axis=-1)) * (1.0 - is_ctx)

    def body(x_ref, f_ref, t_ref, m_ref, g_ref, l_ref, dx_ref, df_ref, dm_ref, dg_ref):
        i = pl.program_id(0)
        is_ctx = (i == nb - 1).astype(F32)
        tgt = _grid_rows(t_ref[...], rows_r, kb)
        l, vjp = jax.vjp(lambda a_, b_, c_, d_: fn(a_, b_, tgt, c_, d_, is_ctx), x_ref[...], f_ref[...], m_ref[...], g_ref[...])
        dx, df, dm, dg = vjp(jnp.ones((), F32))

        @pl.when(i == 0)
        def _():
            l_ref[...] = jnp.zeros_like(l_ref)
            dm_ref[...] = jnp.zeros_like(dm_ref)
            dg_ref[...] = jnp.zeros_like(dg_ref)

        l_ref[...] += jnp.reshape(l, (1, 1))
        dx_ref[...] = dx
        df_ref[...] = df.astype(df_ref.dtype)
        dm_ref[...] += dm
        dg_ref[...] += dg

    rowspec = pl.BlockSpec((TM, D), lambda i: (i, 0))
    return pl.pallas_call(
        body, name="loss_head", grid=(nb,),
        in_specs=[rowspec, rowspec, pl.BlockSpec((rows_r, kb * D), lambda i: (0, jnp.minimum(i, nb - 2))),
                  _full_spec(modp), _full_spec(g_final)],
        out_specs=[pl.BlockSpec((1, 1), lambda i: (0, 0)), rowspec, rowspec, _full_spec(modp), _full_spec(g_final)],
        out_shape=[jax.ShapeDtypeStruct((1, 1), F32), jax.ShapeDtypeStruct(x.shape, F32), jax.ShapeDtypeStruct(x.shape, MXU_DTYPE),
                   jax.ShapeDtypeStruct(modp.shape, F32), jax.ShapeDtypeStruct(g_final.shape, F32)],
        compiler_params=_cp(("arbitrary",)),
    )(x, f, tview, modp, g_final)


def _repack(name, shards, segs, wp):
    nd, kk, ws = shards.shape
    tr = 128
    used = sum(e - s for s, e in segs)

    def body(a_ref, o_ref):
        full = jnp.concatenate([a_ref[d].astype(F32) for d in range(nd)], axis=1)
        parts = [full[:, s:e] for s, e in segs]
        if wp > used:
            parts.append(jnp.zeros((tr, wp - used), F32))
        o_ref[...] = jnp.concatenate(parts, axis=1).astype(o_ref.dtype)

    return pl.pallas_call(
        body, name=name, grid=(kk // tr,),
        in_specs=[pl.BlockSpec((nd, tr, ws), lambda i: (0, i, 0))],
        out_specs=pl.BlockSpec((tr, wp), lambda i: (i, 0)),
        out_shape=jax.ShapeDtypeStruct((kk, wp), MXU_DTYPE),
        compiler_params=_cp(("parallel",)),
    )(shards)


def _unpack(name, dw, segs, ws, out_dtype):
    kk, wp = dw.shape
    tr = 128
    order = sorted(range(len(segs)), key=lambda i: segs[i][0])
    offs, o = [], 0
    for s, e in segs:
        offs.append(o)
        o += e - s

    def body(a_ref, o_ref):
        a = a_ref[...].astype(F32)
        full = jnp.concatenate([a[:, offs[i]:offs[i] + segs[i][1] - segs[i][0]] for i in order], axis=1)
        for d in range(NDEV):
            o_ref[d] = full[:, d * ws:(d + 1) * ws].astype(o_ref.dtype)

    return pl.pallas_call(
        body, name=name, grid=(kk // tr,),
        in_specs=[pl.BlockSpec((tr, wp), lambda i: (i, 0))],
        out_specs=pl.BlockSpec((NDEV, tr, ws), lambda i: (0, i, 0)),
        out_shape=jax.ShapeDtypeStruct((NDEV, kk, ws), out_dtype),
        compiler_params=_cp(("parallel",)),
    )(dw)


def _adam_math(w, g, m, v):
    m = ADAM_B1 * m + (1.0 - ADAM_B1) * g
    v = ADAM_B2 * v + (1.0 - ADAM_B2) * jnp.square(g)
    m_hat = m / (1.0 - ADAM_B1 ** ADAM_STEP)
    v_hat = v / (1.0 - ADAM_B2 ** ADAM_STEP)
    delta = -ADAM_LR * (m_hat / (jnp.sqrt(v_hat) + ADAM_EPS) + ADAM_WD * w)
    return delta, m, v


def _adam(name, w, parts, m, v, after):
    r, c = w.shape
    npart = parts.shape[0]
    tr = _pick(r, (256, 128, 64, 32, 16, 8)) if r * c * 4 > (1 << 20) else r

    def body(w_ref, p_ref, m_ref, v_ref, after_ref, g_ref, d_ref, nm_ref, nv_ref):
        g = p_ref[0].astype(F32)
        for s in range(1, npart):
            g = g + p_ref[s].astype(F32)
        delta, nm, nv = _adam_math(w_ref[...], g, m_ref[...], v_ref[...])
        g_ref[...], d_ref[...], nm_ref[...], nv_ref[...] = g, delta, nm, nv

    spec = pl.BlockSpec((tr, c), lambda i: (i, 0))
    return pl.pallas_call(
        body, name=name, grid=(r // tr,),
        in_specs=[spec, pl.BlockSpec((npart, tr, c), lambda i: (0, i, 0)), spec, spec, ANY],
        out_specs=[spec] * 4, out_shape=[jax.ShapeDtypeStruct((r, c), F32)] * 4,
        compiler_params=_cp(("parallel",)),
    )(w, parts, m, v, after)


def _mod_fwd(c_all, mod_w):
    nl, _, ws = mod_w.shape

    def body(c_ref, w_ref, o_ref):
        o_ref[0] = _dot(jax.nn.silu(c_ref[...]), w_ref[0])

    return pl.pallas_call(
        body, name="mod_fwd", grid=(nl,),
        in_specs=[_full_spec(c_all), pl.BlockSpec((1, D, ws), lambda i: (i, 0, 0))],
        out_specs=pl.BlockSpec((1, 16, ws), lambda i: (i, 0, 0)),
        out_shape=jax.ShapeDtypeStruct((nl, 16, ws), F32),
        compiler_params=_cp(("parallel",)),
    )(c_all, mod_w)


def _mod_bwd(c_all, mod_w, dm):
    nl, _, ws = mod_w.shape

    def body(c_ref, w_ref, d_ref, dw_ref, dc_ref):
        dw_ref[0] = _dot_tn(jax.nn.silu(c_ref[...]), d_ref[0])
        dc_ref[0] = _dot_nt(d_ref[0], w_ref[0])

    return pl.pallas_call(
        body, name="mod_bwd", grid=(nl,),
        in_specs=[_full_spec(c_all), pl.BlockSpec((1, D, ws), lambda i: (i, 0, 0)), pl.BlockSpec((1, 16, ws), lambda i: (i, 0, 0))],
        out_specs=[pl.BlockSpec((1, D, ws), lambda i: (i, 0, 0)), pl.BlockSpec((1, 16, D), lambda i: (i, 0, 0))],
        out_shape=[jax.ShapeDtypeStruct((nl, D, ws), F32), jax.ShapeDtypeStruct((nl, 16, D), F32)],
        compiler_params=_cp(("parallel",)),
    )(c_all, mod_w, dm)


def _sum_parts(name, parts):
    npart, r, c = parts.shape

    def body(p_ref, o_ref):
        g = p_ref[0].astype(F32)
        for s in range(1, npart):
            g = g + p_ref[s].astype(F32)
        o_ref[...] = g

    return pl.pallas_call(body, name=name, out_shape=jax.ShapeDtypeStruct((r, c), F32), compiler_params=_cp())(parts)


MESH = pl.DeviceIdType.MESH
ANY = pl.BlockSpec(memory_space=pl.ANY)
N_PEERS = NDEV - 1


def _mesh_pos():
    return lax.axis_index("x"), lax.axis_index("y"), lax.axis_index("c")


def _slot(px, py, pc):
    return 4 * px + 2 * py + pc


def _two_level_gather(x_refs, o_refs, send_sems, recv_sems, local_sems):
    x, y, c = _mesh_pos()
    me, sibling = (x, y, c), (x, y, 1 - c)
    chips = [(1 - x, y), (x, 1 - y), (1 - x, 1 - y)]
    n = len(x_refs)

    def copy(a, k, block, to, src=None):
        dst = o_refs[a].at[_slot(*block)]
        return pltpu.make_async_remote_copy(src_ref=dst if src is None else src, dst_ref=dst, send_sem=send_sems.at[a, k],
                                            recv_sem=recv_sems.at[a, k], device_id=to, device_id_type=MESH)

    mine = [pltpu.make_async_copy(x_refs[a], o_refs[a].at[_slot(*me)], local_sems.at[a]) for a in range(n)]
    for cp in mine:
        cp.start()
    first = []
    for a in range(n):
        first.append(copy(a, 0, me, sibling, src=x_refs[a]))
        first += [copy(a, 1 + j, me, (*chip, c), src=x_refs[a]) for j, chip in enumerate(chips)]
    for cp in first:
        cp.start()
    passed = []
    for j, chip in enumerate(chips):
        for a in range(n):
            copy(a, 1 + j, (*chip, c), me).wait_recv()
            fwd = copy(a, 4 + j, (*chip, c), sibling)
            fwd.start()
            passed.append(fwd)
    for a in range(n):
        copy(a, 0, sibling, me).wait_recv()
        for j, chip in enumerate(chips):
            copy(a, 4 + j, (*chip, 1 - c), me).wait_recv()
    for cp in first + passed:
        cp.wait_send()
    for cp in mine:
        cp.wait()


def _ag_small(name, x):
    r, c = x.shape

    def body(x_ref, o_ref, send_sems, recv_sems, local_sems):
        _two_level_gather([x_ref], [o_ref], send_sems, recv_sems, local_sems)

    return pl.pallas_call(
        body, name=name, out_shape=jax.ShapeDtypeStruct((NDEV, r, c), x.dtype),
        in_specs=[pl.BlockSpec(memory_space=pltpu.VMEM)], out_specs=pl.BlockSpec(memory_space=pltpu.VMEM),
        scratch_shapes=[pltpu.SemaphoreType.DMA((1, N_PEERS)), pltpu.SemaphoreType.DMA((1, N_PEERS)), pltpu.SemaphoreType.DMA((1,))],
        compiler_params=pltpu.CompilerParams(vmem_limit_bytes=VMEM_LIMIT),
    )(x)


def _ag_big(name, shards):
    n = len(shards)

    def body(*refs):
        _two_level_gather(refs[:n], refs[n:2 * n], *refs[2 * n:])

    return pl.pallas_call(
        body, name=name, out_shape=[jax.ShapeDtypeStruct((NDEV,) + s.shape, s.dtype) for s in shards],
        in_specs=[ANY] * n, out_specs=[ANY] * n,
        scratch_shapes=[pltpu.SemaphoreType.DMA((n, N_PEERS)), pltpu.SemaphoreType.DMA((n, N_PEERS)), pltpu.SemaphoreType.DMA((n,))],
    )(*shards)


HBM = pl.BlockSpec(memory_space=pltpu.HBM)
SEM = pl.BlockSpec(memory_space=pltpu.SEMAPHORE)
EFFECT = pltpu.SideEffectType.DATAFLOW_SIDE_EFFECTING


def _peers(x, y, c):
    return [(k - 1, ((1 - x) if k & 4 else x, (1 - y) if k & 2 else y, (1 - c) if k & 1 else c)) for k in range(1, NDEV)]


def _xchg_copy(src_refs, land_refs, send_sems, recv_sems, a, k, peer, me, scatter):
    src = src_refs[a].at[_slot(*peer)] if scatter else src_refs[a]
    return pltpu.make_async_remote_copy(src_ref=src, dst_ref=land_refs[a].at[me], send_sem=send_sems.at[a * N_PEERS + k],
                                        recv_sem=recv_sems.at[a * N_PEERS + k], device_id=peer, device_id_type=MESH)


def _xchg_start(name, srcs, lands, deps, scatter):
    n, nd = len(srcs), len(deps)

    def body(*refs):
        src_refs, land_refs = refs[:n], refs[n:2 * n]
        send_sems, recv_sems, token = refs[2 * n + nd], refs[2 * n + nd + 1], refs[-1]
        x, y, c = _mesh_pos()
        me = _slot(x, y, c)
        for k, peer in _peers(x, y, c):
            for a in range(n):
                _xchg_copy(src_refs, land_refs, send_sems, recv_sems, a, k, peer, me, scatter).start()
        token[...] = jnp.zeros_like(token)

    res = pl.pallas_call(
        body, name=name,
        out_shape=(pltpu.SemaphoreType.DMA((n * N_PEERS,)), pltpu.SemaphoreType.DMA((n * N_PEERS,)),
                   *[pltpu.HBM(s.shape, s.dtype) for s in srcs], *[pltpu.HBM(s.shape, s.dtype) for s in lands],
                   jax.ShapeDtypeStruct((8, 128), F32)),
        in_specs=[HBM] * (2 * n) + [ANY] * nd,
        out_specs=(SEM, SEM, *([HBM] * (2 * n)), pl.BlockSpec(memory_space=pltpu.VMEM)),
        input_output_aliases={i: 2 + i for i in range(2 * n)},
        compiler_params=pltpu.CompilerParams(has_side_effects=EFFECT),
    )(*[pltpu.with_memory_space_constraint(s, pltpu.HBM) for s in srcs],
      *[pltpu.with_memory_space_constraint(s, pltpu.HBM) for s in lands], *deps)
    return res[0], res[1], res[2:2 + n], res[2 + n:2 + 2 * n], res[-1]


def _xchg_wait(name, send_sems, recv_sems, srcs, lands, after, scatter):
    n = len(srcs)

    def body(*refs):
        src_refs, land_refs = refs[:n], refs[n:2 * n]
        s_sems, r_sems = refs[2 * n], refs[2 * n + 1]
        x, y, c = _mesh_pos()
        me = _slot(x, y, c)
        for k, peer in _peers(x, y, c):
            for a in range(n):
                cp = _xchg_copy(src_refs, land_refs, s_sems, r_sems, a, k, peer, me, scatter)
                cp.wait_send()
                cp.wait_recv()

    res = pl.pallas_call(
        body, name=name,
        out_shape=[pltpu.HBM(s.shape, s.dtype) for s in srcs] + [pltpu.HBM(s.shape, s.dtype) for s in lands],
        in_specs=[HBM] * (2 * n) + [SEM, SEM, ANY], out_specs=[HBM] * (2 * n),
        input_output_aliases={i: i for i in range(2 * n)},
        compiler_params=pltpu.CompilerParams(has_side_effects=EFFECT),
    )(*srcs, *lands, send_sems, recv_sems, after)
    return res[n:]


def _landing(own, me):
    return lax.dynamic_update_slice_in_dim(lax.empty((NDEV,) + own.shape, own.dtype), own[None], me, axis=0)


STAGES = ("l0_mixer", "l0_ffn", "l1_mixer", "l1_ffn")
STAGE_LAYOUT = {"l0_mixer": (AB_SEGS, AB_P), "l1_mixer": (SSD_SEGS, SSD_P_W)}


class _Exchange:
    def __init__(self, shards, me):
        self.shards, self.me = shards, me
        self.pending, self.pending_grads, self.recv = None, None, {}

    def _layout(self, stage):
        ws = self.shards[stage][0].shape[-1]
        return STAGE_LAYOUT.get(stage, (((0, NDEV * ws),), NDEV * ws)) + (ws,)

    def _start_gather(self, stage, deps):
        srcs = list(self.shards[stage])
        lands = [_landing(s, self.me) for s in srcs]
        return _xchg_start("gather_start_" + stage, srcs, lands, deps, False)

    def get(self, stage, dep, thread):
        i = STAGES.index(stage)
        if i == 0:
            g_in, g_out = _ag_big("gather_" + stage, list(self.shards[stage]))
            deps = [g_out, dep]
        else:
            ss, rs, srcs, lands, _ = self.pending
            g_in, g_out = _xchg_wait("gather_wait_" + stage, ss, rs, srcs, lands, dep, False)
            self.pending, deps = None, [g_out]
        if i + 1 < len(STAGES):
            self.pending = self._start_gather(STAGES[i + 1], deps)
            thread = thread + self.pending[4][0, 0]
        segs, wp, _ = self._layout(stage)
        return _repack("repack_" + stage, g_in, segs, wp), g_out.reshape(-1, D), thread

    def put(self, stage, d_in, d_out, thread):
        segs, _, ws = self._layout(stage)
        parts = [_unpack("unpack_" + stage, d_in, segs, ws, MXU_DTYPE), d_out.reshape(NDEV, -1, D)]
        deps = [parts[0]]
        if self.pending_grads is not None:
            deps = [self.finish(parts[0])[0]]
        self.staged = (stage, parts)
        return thread if stage == STAGES[0] else thread + self.start_last(deps)[0, 0]

    def start_last(self, deps):
        stage, parts = self.staged
        lands = [_landing(lax.dynamic_index_in_dim(p, self.me, 0, keepdims=False), self.me) for p in parts]
        self.pending_grads = (stage,) + _xchg_start("scatter_start_" + stage, parts, lands, deps, True)
        return self.pending_grads[5]

    def finish(self, after):
        stage, ss, rs, srcs, lands, _ = self.pending_grads
        self.recv[stage] = _xchg_wait("scatter_wait_" + stage, ss, rs, srcs, lands, after, True)
        self.pending_grads = None
        return self.recv[stage]


def _ffn_fwd(tag, h, w_in, w_out, nb, cb):
    pf = _mm(tag + "_ffn_in", h, w_in, "nn", F32)
    (act,) = _rowwise(tag + "_swiglu", _fn_swiglu, nb, cb, [_row(pf)], [], [(D_FF, MXU_DTYPE)])
    return pf, act, _mm(tag + "_ffn_out", act, w_out, "nn", F32)


def _ffn_bwd(tag, h, pf, act, df, w_in, w_out, nb, cb):
    dw_out = _mm(tag + "_ffn_out_dw", act, df, "tn", MXU_DTYPE)
    dact = _mm(tag + "_ffn_out_dx", df, w_out, "nt", MXU_DTYPE)
    (dpf,), _ = _rowwise_vjp(tag + "_swiglu_bwd", _fn_swiglu, nb, cb, [_row(pf)], [], [_row(dact)], [(0, MXU_DTYPE, None)])
    dw_in = _mm(tag + "_ffn_in_dw", h, dpf, "tn", MXU_DTYPE)
    dh = _mm(tag + "_ffn_in_dx", dpf, w_in, "nt", MXU_DTYPE)
    return dw_out, dw_in, dh


def _local_step(x, ctx, target, mod, P, comm):
    T = x.shape[0]
    N = T + CTX
    nb, cb = N // TM, N // TM - 1
    R = T // GRID_W
    mod0, mod1 = mod[0], mod[1]
    ng = P["norm_g"]
    g00, g01, g10, g11 = ng[0, 0][None], ng[0, 1][None], ng[1, 0][None], ng[1, 1][None]
    pre = functools.partial(_fn_prenorm, a=0, b=1)
    rpre = functools.partial(_fn_resid_prenorm, gi=2, a=3, b=4)
    res5 = functools.partial(_fn_resid, gi=5)
    dirs = (("f", False), ("b", True))

    xc0 = jnp.concatenate([x, ctx], axis=0)
    w_ab_in, w_ab_out, g00 = comm.get("l0_mixer", mod, g00)
    (h0,) = _rowwise("l0_prenorm", pre, nb, cb, [_row(xc0)], [g00, mod0], [(D, MXU_DTYPE)])
    p0 = _mm("l0_in", h0, w_ab_in, "nn", F32)
    gla_rows = [(p0, 512, 0), (p0, 256, 8), (p0, 256, 9), (p0, 128, 20)]
    gla_blk = _multi_chunk(_gla_chunk, GLA_L, TM // GLA_L, len(gla_rows))
    gla_par = {d: [P["ab_gate_w"][int(r)], P["ab_gate_b"][int(r)][None]] for d, r in dirs}
    gla_state = (GLA_DV, GLA_H * GLA_DK)
    o, st0 = {}, {}
    for d, rev in dirs:
        o[d], st0[d] = _scan_fwd("gla_fwd_" + d, gla_blk, TM, nb, cb, rev, gla_rows, gla_par[d], gla_state, GLA_H * GLA_DV)
    n128, cb128 = N // GMLP_L, T // GMLP_L
    mix_rows = [_row(o["f"], tm=GMLP_L), _row(o["b"], tm=GMLP_L)] + [_row(p0, 512, j, tm=GMLP_L) for j in (1, 2, 3)]
    mix_par = [P["ab_gla_norm_g"], P["ab_vnorm_g"], P["ab_spatial_w"].reshape(GMLP_G * GMLP_L, GMLP_L), P["ab_spatial_b"].T]
    (cat0,) = _rowwise("l0_mix", _fn_mixpost, n128, cb128, mix_rows, mix_par, [(D, MXU_DTYPE)], tm=GMLP_L)
    y0 = _mm("l0_out", cat0, w_ab_out, "nn", F32)
    w_fi0, w_fo0, g01 = comm.get("l0_ffn", y0, g01)
    x1, h1 = _rowwise("l0_ffn_prenorm", rpre, nb, cb, [_row(xc0), _row(y0)], [g01, mod0, mod0], [(D, F32), (D, MXU_DTYPE)])
    pf0, act0, f0 = _ffn_fwd("l0", h1, w_fi0, w_fo0, nb, cb)
    (x2,) = _rowwise("l0_resid", res5, nb, cb, [_row(x1), _row(f0)], [mod0], [(D, F32)])
    x2p = _perm("to_col_major", x2, R, nb)

    w_ssd_in, w_ssd_out, g10 = comm.get("l1_mixer", x2p, g10)
    (h2,) = _rowwise("l1_prenorm", pre, nb, cb, [_row(x2p)], [g10, mod1], [(D, MXU_DTYPE)])
    p1 = _mm("l1_in", h2, w_ssd_in, "nn", F32)
    conv_w8 = jnp.concatenate([P["ssd_conv_w"], jnp.zeros((8 - SSD_K, 3 * CONV_W), F32)], axis=0)
    xbc = _conv("l1_conv", p1, conv_w8, P["ssd_conv_b"], nb, permuted_src=True, act=True, flip=False, out_dtype=F32)
    ssd_rows = [(xbc, SSD_INNER, 0), (xbc, 512, 4), (xbc, 512, 5), (p1, 128, 40)]
    ssd_blk = _multi_chunk(_ssd_chunk, SSD_L, TM // SSD_L, len(ssd_rows))
    ssd_par = {d: [P["ssd_dt_bias"][int(r)][None], P["ssd_a_log"][int(r)][None]] for d, r in dirs}
    ssd_state = (SSD_N, SSD_INNER)
    ys, st1 = {}, {}
    for d, rev in dirs:
        ys[d], st1[d] = _scan_fwd("ssd_fwd_" + d, ssd_blk, TM, nb, cb, rev, ssd_rows, ssd_par[d], ssd_state, SSD_INNER)
    fin_rows = [_row(ys["f"]), _row(ys["b"]), _row(xbc, SSD_INNER, 0), _row(p1, SSD_INNER, 1)]
    fin_par = [P["ssd_d"], P["ssd_norm_g"]]
    (yn,) = _rowwise("l1_finish", _fn_ssd_finish, nb, cb, fin_rows, fin_par, [(SSD_INNER, MXU_DTYPE)])
    y1 = _mm("l1_out", yn, w_ssd_out, "nn", F32)
    w_fi1, w_fo1, g11 = comm.get("l1_ffn", y1, g11)
    x3, h3 = _rowwise("l1_ffn_prenorm", rpre, nb, cb, [_row(x2p), _row(y1)], [g11, mod1, mod1], [(D, F32), (D, MXU_DTYPE)])
    pf1, act1, f1 = _ffn_fwd("l1", h3, w_fi1, w_fo1, nb, cb)
    loss, dx3, df1, dm1_j, d_final_g = _loss_head(x3, f1, target, mod1, P["final_norm_g"], nb, R)

    dP = {"final_norm_g": d_final_g}
    dwo1, dwi1, dh3 = _ffn_bwd("l1", h3, pf1, act1, df1, w_fi1, w_fo1, nb, cb)
    g11 = comm.put("l1_ffn", dwi1, dwo1, g11)
    (dx2p_a, dy1), (dg11, dm1_a, dm1_b) = _rowwise_vjp(
        "l1_ffn_prenorm_bwd", rpre, nb, cb, [_row(x2p), _row(y1)], [g11, mod1, mod1], [_row(dx3), _row(dh3)],
        [(0, F32, None), (1, MXU_DTYPE, None)])
    d_ssd_out = _mm("l1_out_dw", yn, dy1, "tn", MXU_DTYPE)
    dyn = _mm("l1_out_dx", dy1, w_ssd_out, "nt", MXU_DTYPE)
    (dys, dxs, dz), (dP["ssd_d"], dP["ssd_norm_g"]) = _rowwise_vjp(
        "l1_finish_bwd", _fn_ssd_finish, nb, cb, fin_rows, fin_par, [_row(dyn)],
        [(0, F32, None), (2, F32, None), (3, MXU_DTYPE, None)])
    dxp, dbp, dcp, dtl, ddtb, dalog = [], [], [], [], [], []
    for d, rev in dirs:
        (dx_, db_, dc_, dt_), (ddtb_, dalog_) = _scan_bwd("ssd_bwd_" + d, ssd_blk, TM, nb, cb, rev, ssd_rows, ssd_par[d],
                                                          st1[d], dys, ssd_state, SSD_INNER)
        dxp.append(dx_); dbp.append(db_); dcp.append(dc_); dtl.append(dt_); ddtb.append(ddtb_); dalog.append(dalog_)
    dP["ssd_dt_bias"] = jnp.concatenate(ddtb, axis=0)
    dP["ssd_a_log"] = jnp.concatenate(dalog, axis=0)
    dacc, dcw8, dP["ssd_conv_b"] = _conv_bwd_pre("l1_conv_bwd", p1, conv_w8, P["ssd_conv_b"],
                                                  (dxp + [dxs], [dbp[0], dbp[1], dcp[0], dcp[1]]), nb)
    dP["ssd_conv_w"] = dcw8[:SSD_K]
    dpc = _conv("l1_conv_dx", dacc, conv_w8, jnp.zeros((1, 3 * CONV_W), F32), nb, permuted_src=False, act=False, flip=True,
                out_dtype=MXU_DTYPE)
    cat1 = functools.partial(_fn_concat, sums=(1, 1, 1, 2), pad=SSD_P_W - 5248)
    (dp1,) = _rowwise("l1_dp", cat1, nb, cb, [_row(dpc, SSD_INNER, 0), _row(dz), _row(dpc, 1024, 2), _row(dtl[0]), _row(dtl[1])],
                      [], [(SSD_P_W, MXU_DTYPE)])
    g10 = comm.put("l1_mixer", _mm("l1_in_dw", h2, dp1, "tn", F32), d_ssd_out, g10)
    dh2 = _mm("l1_in_dx", dp1, w_ssd_in, "nt", MXU_DTYPE)
    (dx2p,), (dg10, dm1_f) = _rowwise_vjp("l1_prenorm_bwd", pre, nb, cb, [_row(x2p)], [g10, mod1], [_row(dh2)],
                                          [(0, F32, _row(dx2p_a))])
    dx2 = _perm("to_row_major", dx2p, GRID_W, nb)

    (dx1_a, df0), (dm0_e,) = _rowwise_vjp("l0_resid_bwd", res5, nb, cb, [_row(x1), _row(f0)], [mod0], [_row(dx2)],
                                          [(0, F32, None), (1, MXU_DTYPE, None)])
    dwo0, dwi0, dh1 = _ffn_bwd("l0", h1, pf0, act0, df0, w_fi0, w_fo0, nb, cb)
    g01 = comm.put("l0_ffn", dwi0, dwo0, g01)
    (dxc0_a, dy0), (dg01, dm0_a, dm0_b) = _rowwise_vjp(
        "l0_ffn_prenorm_bwd", rpre, nb, cb, [_row(xc0), _row(y0)], [g01, mod0, mod0], [_row(dx1_a), _row(dh1)],
        [(0, F32, None), (1, MXU_DTYPE, None)])
    d_ab_out = _mm("l0_out_dw", cat0, dy0, "tn", MXU_DTYPE)
    dcat0 = _mm("l0_out_dx", dy0, w_ab_out, "nt", MXU_DTYPE)
    (do, dr, du, dgm), (dP["ab_gla_norm_g"], dP["ab_vnorm_g"], dsw, dsb_t) = _rowwise_vjp(
        "l0_mix_bwd", _fn_mixpost, n128, cb128, mix_rows, mix_par, [_row(dcat0, tm=GMLP_L)],
        [(0, F32, None), (2, MXU_DTYPE, None), (3, MXU_DTYPE, None), (4, MXU_DTYPE, None)], tm=GMLP_L)
    dP["ab_spatial_w"] = dsw.reshape(GMLP_G, GMLP_L, GMLP_L)
    dP["ab_spatial_b"] = dsb_t.T
    gl, dgw, dgb = [], [], []
    for d, rev in dirs:
        g4, (dgw_, dgb_) = _scan_bwd("gla_bwd_" + d, gla_blk, TM, nb, cb, rev, gla_rows, gla_par[d], st0[d], do,
                                     gla_state, GLA_H * GLA_DV)
        gl.append(g4); dgw.append(dgw_[None]); dgb.append(dgb_)
    dP["ab_gate_w"] = jnp.concatenate(dgw, axis=0)
    dP["ab_gate_b"] = jnp.concatenate(dgb, axis=0)
    cat0f = functools.partial(_fn_concat, sums=(2, 1, 1, 1, 2, 2, 2))
    (dp0,) = _rowwise("l0_dp", cat0f, nb, cb,
                      [_row(gl[0][0]), _row(gl[1][0]), _row(dr), _row(du), _row(dgm), _row(gl[0][1]), _row(gl[1][1]),
                       _row(gl[0][2]), _row(gl[1][2]), _row(gl[0][3]), _row(gl[1][3])], [], [(AB_P, MXU_DTYPE)])
    g00 = comm.put("l0_mixer", _mm("l0_in_dw", h0, dp0, "tn", F32), d_ab_out, g00)
    dh0 = _mm("l0_in_dx", dp0, w_ab_in, "nt", MXU_DTYPE)
    (dxc0,), (dg00, dm0_s) = _rowwise_vjp("l0_prenorm_bwd", pre, nb, cb, [_row(xc0)], [g00, mod0], [_row(dh0)],
                                          [(0, F32, _row(dxc0_a))])
    dP["norm_g"] = jnp.concatenate([dg00, dg01, dg10, dg11], axis=0).reshape(2, 2, D)
    dmod = jnp.stack([dm0_s + dm0_a + dm0_b + dm0_e, dm1_f + dm1_a + dm1_b + dm1_j])
    return loss, dxc0[:T], dmod, dP


WEIGHTS = ("c_ctx", "mod_w", "mod_b", "norm_g", "ffn_w_in", "ffn_w_out", "ab_w_in", "ab_gate_w", "ab_gate_b", "ab_gla_norm_g",
           "ab_vnorm_g", "ab_spatial_w", "ab_spatial_b", "ab_w_out", "ssd_w_in", "ssd_conv_w", "ssd_conv_b", "ssd_dt_bias",
           "ssd_a_log", "ssd_d", "ssd_norm_g", "ssd_w_out", "final_norm_g")
SMALL_SHARDED = ("norm_g", "ab_gate_w", "ab_gate_b", "ssd_conv_w", "ssd_conv_b", "ssd_norm_g")
SMALL = ("c_ctx", "mod_b", "norm_g", "ab_gate_w", "ab_gate_b", "ab_gla_norm_g", "ab_vnorm_g", "ab_spatial_w", "ab_spatial_b",
         "ssd_conv_w", "ssd_conv_b", "ssd_dt_bias", "ssd_a_log", "ssd_d", "ssd_norm_g", "final_norm_g")
LANES = 1024


def _pack(arrs, rows_multiple=8):
    flat = jnp.concatenate([a.reshape(-1).astype(F32) for a in arrs])
    rows = -(-flat.shape[0] // LANES)
    rows = -(-rows // rows_multiple) * rows_multiple
    return jnp.pad(flat, (0, rows * LANES - flat.shape[0])).reshape(rows, LANES)


def _unpack_flat(buf, shapes):
    lead = buf.shape[:-2]
    flat = buf.reshape(lead + (-1,))
    out, o = [], 0
    for s in shapes:
        n = math.prod(s)
        out.append(flat[..., o:o + n].reshape(lead + tuple(s)))
        o += n
    return out


def _unshard(g):
    g = jnp.moveaxis(g, 0, -2)
    return g.reshape(g.shape[:-2] + (g.shape[-2] * g.shape[-1],))


def _my_shard(full, me, ws):
    return lax.dynamic_slice_in_dim(full, me * ws, ws, axis=full.ndim - 1)


def _silu_vjp(cvec, dsc):
    def body(c_ref, d_ref, o_ref):
        _, vjp = jax.vjp(jax.nn.silu, c_ref[...])
        o_ref[...] = vjp(d_ref[...])[0]

    return pl.pallas_call(body, name="c_ctx_bwd", out_shape=jax.ShapeDtypeStruct(cvec.shape, F32), compiler_params=_cp())(cvec, dsc)


def kernel(x, c, ctx, c_ctx, mod_w, mod_b, norm_g, ffn_w_in, ffn_w_out, ab_w_in, ab_gate_w, ab_gate_b, ab_gla_norm_g, ab_vnorm_g, ab_spatial_w, ab_spatial_b, ab_w_out, ssd_w_in, ssd_conv_w, ssd_conv_b, ssd_dt_bias, ssd_a_log, ssd_d, ssd_norm_g, ssd_w_out, final_norm_g, loss_target, m_c_ctx, m_mod_w, m_mod_b, m_norm_g, m_ffn_w_in, m_ffn_w_out, m_ab_w_in, m_ab_gate_w, m_ab_gate_b, m_ab_gla_norm_g, m_ab_vnorm_g, m_ab_spatial_w, m_ab_spatial_b, m_ab_w_out, m_ssd_w_in, m_ssd_conv_w, m_ssd_conv_b, m_ssd_dt_bias, m_ssd_a_log, m_ssd_d, m_ssd_norm_g, m_ssd_w_out, m_final_norm_g, v_c_ctx, v_mod_w, v_mod_b, v_norm_g, v_ffn_w_in, v_ffn_w_out, v_ab_w_in, v_ab_gate_w, v_ab_gate_b, v_ab_gla_norm_g, v_ab_vnorm_g, v_ab_spatial_w, v_ab_spatial_b, v_ab_w_out, v_ssd_w_in, v_ssd_conv_w, v_ssd_conv_b, v_ssd_dt_bias, v_ssd_a_log, v_ssd_d, v_ssd_norm_g, v_ssd_w_out, v_final_norm_g):
    a = dict(locals())
    me = _slot(*_mesh_pos())
    ws_mod = mod_w.shape[-1]

    fwd_small = [c] + [a[k] for k in SMALL_SHARDED]
    g_small = _ag_small("gather_small", _pack(fwd_small))
    parts = _unpack_flat(g_small, [t.shape for t in fwd_small])
    c_rows = parts[0].reshape(NDEV, D)
    full = {k: _unshard(p) for k, p in zip(SMALL_SHARDED, parts[1:])}
    c_all = jnp.concatenate([c_rows, c_ctx[None], jnp.zeros((7, D), F32)], axis=0)
    m_all = _ag_small("gather_mod", _mod_fwd(c_all, mod_w).reshape(2 * 16, ws_mod)).reshape(NDEV, 2, 16, ws_mod)
    m_mine = lax.dynamic_index_in_dim(m_all, me, axis=2, keepdims=False)
    mx = jnp.moveaxis(m_mine, 0, 1).reshape(2, N_MOD, D) + mod_b.reshape(2, N_MOD, D)
    mc = jnp.moveaxis(m_all[:, :, 8, :], 0, 1).reshape(2, N_MOD, D) + mod_b.reshape(2, N_MOD, D)
    pad2 = jnp.zeros((2, 2, D), F32)
    mod = jnp.concatenate([mx, pad2, mc, pad2], axis=1)

    big = {"l0_mixer": (ab_w_in[0], ab_w_out[0]), "l0_ffn": (ffn_w_in[0], ffn_w_out[0]),
           "l1_mixer": (ssd_w_in[0], ssd_w_out[0]), "l1_ffn": (ffn_w_in[1], ffn_w_out[1])}
    comm = _Exchange({k: tuple(w.astype(MXU_DTYPE) for w in v) for k, v in big.items()}, me)
    P = {
        "norm_g": full["norm_g"], "ab_gate_w": full["ab_gate_w"][0], "ab_gate_b": full["ab_gate_b"][0],
        "ab_gla_norm_g": ab_gla_norm_g, "ab_vnorm_g": ab_vnorm_g, "ab_spatial_w": ab_spatial_w[0], "ab_spatial_b": ab_spatial_b[0],
        "ssd_conv_w": full["ssd_conv_w"][0], "ssd_conv_b": full["ssd_conv_b"], "ssd_dt_bias": ssd_dt_bias[0],
        "ssd_a_log": ssd_a_log[0], "ssd_d": ssd_d, "ssd_norm_g": full["ssd_norm_g"], "final_norm_g": final_norm_g[None],
    }

    loss, grad_x, dmod, dP = _local_step(x[0], ctx[0], loss_target[0], mod, P, comm)

    dmx, dmc = dmod[:, 0:N_MOD].reshape(2, N_MOD * D), dmod[:, 8:8 + N_MOD].reshape(2, N_MOD * D)
    small_names = ("ab_gate_w", "ab_gate_b", "ab_gla_norm_g", "ab_vnorm_g", "ab_spatial_w", "ab_spatial_b", "norm_g", "ssd_conv_w",
                   "ssd_conv_b", "ssd_dt_bias", "ssd_a_log", "ssd_d", "ssd_norm_g", "final_norm_g")
    bwd_small = [dP[k] for k in small_names] + [dmc, dmx]
    shapes = [t.shape for t in bwd_small]
    g_bwd = _ag_small("gather_small_grads", _pack(bwd_small))
    summed = _unpack_flat(_sum_parts("sum_small_grads", g_bwd), shapes)
    gfull = dict(zip(small_names, summed[:-2]))
    dmc_sum, dmx_sum = summed[-2], summed[-1]
    dmx_all = _unpack_flat(g_bwd, shapes)[-1]
    dmx_sh = jnp.moveaxis(_my_shard(dmx_all, me, ws_mod), 0, 1)
    dm = jnp.concatenate([dmx_sh, _my_shard(dmc_sum, me, ws_mod)[:, None, :], jnp.zeros((2, 7, ws_mod), F32)], axis=1)
    d_mod_w, dsc = _mod_bwd(c_all, mod_w, dm)
    dsc_ctx = (dsc[0, 8] + dsc[1, 8])[None]
    dsc_all = _ag_small("gather_c_ctx_grad", jnp.concatenate([dsc_ctx, jnp.zeros((7, D), F32)], axis=0))
    d_c_ctx = _silu_vjp(c_ctx[None], _sum_parts("sum_c_ctx_grad", dsc_all)[0:1])[0]

    g_small_w = {
        "c_ctx": d_c_ctx, "mod_b": dmx_sum + dmc_sum, "norm_g": gfull["norm_g"], "ab_gate_w": gfull["ab_gate_w"][None],
        "ab_gate_b": gfull["ab_gate_b"][None], "ab_gla_norm_g": gfull["ab_gla_norm_g"], "ab_vnorm_g": gfull["ab_vnorm_g"],
        "ab_spatial_w": gfull["ab_spatial_w"][None], "ab_spatial_b": gfull["ab_spatial_b"][None], "ssd_conv_w": gfull["ssd_conv_w"][None],
        "ssd_conv_b": gfull["ssd_conv_b"], "ssd_dt_bias": gfull["ssd_dt_bias"][None], "ssd_a_log": gfull["ssd_a_log"][None],
        "ssd_d": gfull["ssd_d"], "ssd_norm_g": gfull["ssd_norm_g"], "final_norm_g": gfull["final_norm_g"][0],
    }
    for k in SMALL_SHARDED:
        g_small_w[k] = _my_shard(g_small_w[k], me, a[k].shape[-1])
    token = comm.start_last([d_c_ctx])
    res = _adam("adam_small", _pack([a[k] for k in SMALL]), _pack([g_small_w[k] for k in SMALL])[None],
                _pack([a["m_" + k] for k in SMALL]), _pack([a["v_" + k] for k in SMALL]), token)
    out = {k: vals for k, vals in zip(SMALL, zip(*[_unpack_flat(r, [a[k].shape for k in SMALL]) for r in res]))}

    def adam_big(name, w2d, parts3d, m2d, v2d, shape):
        return tuple(r.reshape(shape) for r in _adam(name, w2d, parts3d, m2d, v2d, token))

    out["mod_w"] = adam_big("adam_mod_w", mod_w.reshape(-1, ws_mod), d_mod_w.reshape(1, -1, ws_mod), m_mod_w.reshape(-1, ws_mod),
                            v_mod_w.reshape(-1, ws_mod), mod_w.shape)
    ffn = {}
    for i, stage in enumerate(("l0_ffn", "l1_ffn")):
        for j, k in enumerate(("ffn_w_in", "ffn_w_out")):
            ffn[k, i] = adam_big("adam_%s%d" % (k, i), a[k][i], comm.recv[stage][j], a["m_" + k][i], a["v_" + k][i], a[k].shape[1:])
    for k in ("ffn_w_in", "ffn_w_out"):
        out[k] = tuple(jnp.stack(t) for t in zip(ffn[k, 0], ffn[k, 1]))
    for j, k in enumerate(("ssd_w_in", "ssd_w_out")):
        out[k] = adam_big("adam_" + k, a[k][0], comm.recv["l1_mixer"][j], a["m_" + k][0], a["v_" + k][0], a[k].shape)
    recv_ab = comm.finish(out["ssd_w_out"][3])
    for j, k in enumerate(("ab_w_in", "ab_w_out")):
        out[k] = adam_big("adam_" + k, a[k][0], recv_ab[j], a["m_" + k][0], a["v_" + k][0], a[k].shape)

    loss_all = lax.psum(loss[0, 0], ("x", "y", "c"))
    return (loss_all, grad_x[None], *[out[k][0] for k in WEIGHTS], *[out[k][1] for k in WEIGHTS],
            *[out[k][2] for k in WEIGHTS], *[out[k][3] for k in WEIGHTS])
```

```python
import functools
import math

import jax
import jax.numpy as jnp
from jax import lax
from jax.experimental import pallas as pl
from jax.experimental.pallas import tpu as pltpu

F32 = jnp.float32
BF16 = jnp.bfloat16
MXU_DTYPE = jnp.bfloat16
HI = lax.Precision.HIGHEST

D = 1024
NDEV = 8
N_MOD = 6
EPS = 1e-6
GRID_W = 64
CTX = 256
TM = 256
D_FF = 2816
GLA_H, GLA_DK, GLA_DV, GLA_LR, GLA_TAU, GLA_L = 4, 64, 128, 16, 16.0, 64
GMLP_G, GMLP_C, GMLP_L = 4, 128, 128
SSD_H, SSD_P, SSD_G, SSD_N, SSD_L, SSD_K = 32, 64, 4, 128, 128, 5
SSD_INNER = SSD_H * SSD_P
AB_IN = 2592
SSD_IN = 5184
AB_SEGS = ((256, 768), (1056, 1568), (1568, 2080), (2080, 2592), (0, 256), (800, 1056), (768, 800))
AB_P = 2688
SSD_SEGS = ((0, 2048), (3136, 5184), (2048, 2560), (2560, 3072), (3072, 3136))
SSD_P_W = 5376
VMEM_LIMIT = 56 * 1024 * 1024
TN_ACC_BYTES = 12 * 1024 * 1024

ADAM_LR, ADAM_B1, ADAM_B2, ADAM_EPS, ADAM_WD, ADAM_STEP = 0.001, 0.9, 0.999, 1e-08, 0.01, 10


def _cp(sem=None, **kw):
    return pltpu.CompilerParams(dimension_semantics=sem, vmem_limit_bytes=VMEM_LIMIT, **kw)


def _dot(a, b, dims=(((1,), (0,)), ((), ()))):
    return lax.dot_general(a.astype(MXU_DTYPE), b.astype(MXU_DTYPE), dims, preferred_element_type=F32)


def _dot_nt(a, b):
    return _dot(a, b, (((1,), (1,)), ((), ())))


def _dot_tn(a, b):
    return _dot(a, b, (((0,), (0,)), ((), ())))


def _dotx(a, b, dims=(((1,), (0,)), ((), ()))):
    return lax.dot_general(a, b, dims, precision=HI, preferred_element_type=F32)


def _rms(x):
    return x * lax.rsqrt(jnp.mean(x * x, axis=-1, keepdims=True) + EPS)


def _pick(n, prefs):
    for p in prefs:
        if n % p == 0:
            return p
    return n


def _row(arr, width=None, colblk=0, tm=TM):
    width = arr.shape[1] if width is None else width
    return (arr, pl.BlockSpec((tm, width), lambda i, c=colblk: (i, c)))


def _full_spec(p):
    nd = p.ndim
    return pl.BlockSpec(p.shape, lambda i, nd=nd: (0,) * nd)


def _rowwise(name, fn, n_blocks, ctx_blk, rows, params, outs, tm=TM):
    nr, npar = len(rows), len(params)

    def body(*refs):
        t = (pl.program_id(0) >= ctx_blk).astype(F32)
        rv = [r[...].astype(F32) for r in refs[:nr]]
        pv = [p[...] for p in refs[nr:nr + npar]]
        res = fn(t, rv, pv)
        for o_ref, o in zip(refs[nr + npar:], res):
            o_ref[...] = o.astype(o_ref.dtype)

    return pl.pallas_call(
        body, name=name, grid=(n_blocks,),
        in_specs=[s for _, s in rows] + [_full_spec(p) for p in params],
        out_specs=[pl.BlockSpec((tm, w), lambda i: (i, 0)) for w, _ in outs],
        out_shape=[jax.ShapeDtypeStruct((n_blocks * tm, w), dt) for w, dt in outs],
        compiler_params=_cp(("parallel",)),
    )(*[a for a, _ in rows], *params)


def _rowwise_vjp(name, fn, n_blocks, ctx_blk, rows, params, douts, row_grads, tm=TM):
    nr, npar, nd = len(rows), len(params), len(douts)
    adds = [a for _, _, a in row_grads if a is not None]
    na = len(adds)

    def body(*refs):
        i = pl.program_id(0)
        t = (i >= ctx_blk).astype(F32)
        rv = [r[...].astype(F32) for r in refs[:nr]]
        pv = [p[...] for p in refs[nr:nr + npar]]
        dv = [r[...].astype(F32) for r in refs[nr + npar:nr + npar + nd]]
        av = [r[...].astype(F32) for r in refs[nr + npar + nd:nr + npar + nd + na]]
        o_refs = refs[nr + npar + nd + na:]
        _, vjp = jax.vjp(lambda r, p: tuple(fn(t, r, p)), rv, pv)
        d_rows, d_params = vjp(tuple(dv))
        ai = 0
        for o_ref, (ri, _, addend) in zip(o_refs, row_grads):
            g = d_rows[ri]
            if addend is not None:
                g = g + av[ai]
                ai += 1
            o_ref[...] = g.astype(o_ref.dtype)
        p_refs = o_refs[len(row_grads):]

        @pl.when(i == 0)
        def _():
            for p_ref in p_refs:
                p_ref[...] = jnp.zeros_like(p_ref)

        for p_ref, g in zip(p_refs, d_params):
            p_ref[...] += g

    widths = [rows[ri][1].block_shape[1] for ri, _, _ in row_grads]
    res = pl.pallas_call(
        body, name=name, grid=(n_blocks,),
        in_specs=[s for _, s in rows] + [_full_spec(p) for p in params] + [s for _, s in douts] + [s for _, s in adds],
        out_specs=[pl.BlockSpec((tm, w), lambda i: (i, 0)) for w in widths] + [_full_spec(p) for p in params],
        out_shape=[jax.ShapeDtypeStruct((n_blocks * tm, w), dt) for w, (_, dt, _) in zip(widths, row_grads)]
        + [jax.ShapeDtypeStruct(p.shape, F32) for p in params],
        compiler_params=_cp(("arbitrary",)),
    )(*[a for a, _ in rows], *params, *[a for a, _ in douts], *[a for a, _ in adds])
    return res[:len(row_grads)], res[len(row_grads):]


def _mm(name, a, b, mode, out_dtype):
    if mode == "nn":
        m, kk = a.shape
        n = b.shape[1]
    elif mode == "nt":
        m, kk = a.shape
        n = b.shape[0]
    else:
        kk, m = a.shape
        n = b.shape[1]
    if mode == "tn":
        tm = _pick(m, (2048, 1408, 1024, 512, 256, 128))
        tn = next(n // d for d in range(1, n // 128 + 1) if n % d == 0 and (n // d) % 128 == 0 and tm * (n // d) * 4 <= TN_ACC_BYTES)
        tk = _pick(kk, (544, 512, 256, 128))
    else:
        tm = _pick(m, (1088, 1024, 768, 512, 384, 256, 128))
        tn = _pick(n, (1024, 512, 384, 256, 128) if mode == "nt" else (512, 384, 256, 128))
        tk = kk if kk <= 2816 else _pick(kk, (1792, 1408, 1024, 896, 768, 512, 384, 256, 128))
    nk = kk // tk
    in_place = out_dtype == F32
    if mode == "nn":
        specs = [pl.BlockSpec((tm, tk), lambda i, j, k: (i, k)), pl.BlockSpec((tk, tn), lambda i, j, k: (k, j))]
        dims = (((1,), (0,)), ((), ()))
    elif mode == "nt":
        specs = [pl.BlockSpec((tm, tk), lambda i, j, k: (i, k)), pl.BlockSpec((tn, tk), lambda i, j, k: (j, k))]
        dims = (((1,), (1,)), ((), ()))
    else:
        specs = [pl.BlockSpec((tk, tm), lambda i, j, k: (k, i)), pl.BlockSpec((tk, tn), lambda i, j, k: (k, j))]
        dims = (((0,), (0,)), ((), ()))

    def body(a_ref, b_ref, o_ref, *scratch):
        part = lax.dot_general(a_ref[...].astype(MXU_DTYPE), b_ref[...].astype(MXU_DTYPE), dims, preferred_element_type=F32)
        if nk == 1:
            o_ref[...] = part.astype(o_ref.dtype)
        else:
            k = pl.program_id(2)
            acc = o_ref if in_place else scratch[0]

            @pl.when(k == 0)
            def _():
                acc[...] = part

            @pl.when(k > 0)
            def _():
                acc[...] += part

            if not in_place:
                @pl.when(k == nk - 1)
                def _():
                    o_ref[...] = acc[...].astype(o_ref.dtype)

    return pl.pallas_call(
        body, name=name, grid=(m // tm, n // tn, nk), in_specs=specs,
        out_specs=pl.BlockSpec((tm, tn), lambda i, j, k: (i, j)),
        out_shape=jax.ShapeDtypeStruct((m, n), out_dtype),
        scratch_shapes=[] if nk == 1 or in_place else [pltpu.VMEM((tm, tn), F32)],
        compiler_params=_cp(("parallel", "parallel", "arbitrary")),
    )(a, b)


def _sel_mod(modp, t):
    return modp[0:8] * (1.0 - t) + modp[8:16] * t


def _fn_prenorm(t, rows, params, *, a, b):
    (x,), (g, modp) = rows, params
    m = _sel_mod(modp, t)
    return ((_rms(x) * g) * (1.0 + m[b:b + 1]) + m[a:a + 1],)


def _fn_resid_prenorm(t, rows, params, *, gi, a, b):
    (x, y), (g, mod_a, mod_b) = rows, params
    ma, mb = _sel_mod(mod_a, t), _sel_mod(mod_b, t)
    xn = x + ma[gi:gi + 1] * y
    return xn, (_rms(xn) * g) * (1.0 + mb[b:b + 1]) + mb[a:a + 1]


def _fn_resid(t, rows, params, *, gi):
    (x, y), (mod_a,) = rows, params
    return (x + _sel_mod(mod_a, t)[gi:gi + 1] * y,)


def _fn_swiglu(t, rows, params):
    (pf,) = rows
    return (jax.nn.silu(pf[:, :D_FF]) * pf[:, D_FF:],)


def _fn_mixpost(t, rows, params):
    (o_f, o_b, r, u, g), (gla_g, vn_g, sw, sb_t) = rows, params
    o = o_f + o_b
    a = jnp.concatenate([_rms(o[:, h * GLA_DV:(h + 1) * GLA_DV]) for h in range(GLA_H)], axis=1) * gla_g * jax.nn.silu(r)
    uu, vv = jax.nn.gelu(u), jax.nn.gelu(g)
    mu = jnp.mean(vv, axis=-1, keepdims=True)
    var = jnp.mean(jnp.square(vv - mu), axis=-1, keepdims=True)
    vn = ((vv - mu) * lax.rsqrt(var + EPS)) * vn_g
    s = jnp.concatenate(
        [_dot(sw[gi * GMLP_L:(gi + 1) * GMLP_L, :], vn[:, gi * GMLP_C:(gi + 1) * GMLP_C]) + sb_t[:, gi:gi + 1]
         for gi in range(GMLP_G)], axis=1)
    return (jnp.concatenate([a, uu * s], axis=1),)


def _head_expand():
    r = lax.broadcasted_iota(jnp.int32, (SSD_H, SSD_INNER), 0)
    c = lax.broadcasted_iota(jnp.int32, (SSD_H, SSD_INNER), 1)
    return (c // SSD_P == r).astype(F32)


def _fn_ssd_finish(t, rows, params):
    (y_f, y_b, xs, z), (d_skip, norm_g) = rows, params
    d_full = _dotx(jnp.broadcast_to(d_skip, (8, SSD_H)), _head_expand())[0:1]
    y = (y_f + y_b + d_full * xs) * jax.nn.silu(z)
    gw = SSD_INNER // SSD_G
    return (jnp.concatenate([_rms(y[:, gi * gw:(gi + 1) * gw]) for gi in range(SSD_G)], axis=1) * norm_g,)


def _fn_concat(t, rows, params, *, sums, pad=0):
    out, i = [], 0
    for n in sums:
        acc = rows[i]
        for j in range(1, n):
            acc = acc + rows[i + j]
        out.append(acc)
        i += n
    if pad:
        out.append(jnp.zeros((out[0].shape[0], pad), F32))
    return (jnp.concatenate(out, axis=1),)


def _tri(n, rev):
    r = lax.broadcasted_iota(jnp.int32, (n, n), 0)
    c = lax.broadcasted_iota(jnp.int32, (n, n), 1)
    return (r <= c) if rev else (r >= c)


def _gla_chunk(S, v, k, q, tail, gw, gb, *, rev):
    L = GLA_L
    msk = _tri(L, rev)
    tri = msk.astype(F32)
    lr = tail[:, GLA_LR:2 * GLA_LR] if rev else tail[:, 0:GLA_LR]
    la = jax.nn.log_sigmoid(_dot(lr, gw) + gb) / GLA_TAU
    b = _dotx(tri, la)
    b_last = b[0:1] if rev else b[L - 1:L]
    kd = k * jnp.exp(b_last - b)
    qd = (q * GLA_DK ** -0.5) * jnp.exp(b)
    ki = k * jnp.exp(-b)
    dec = jnp.exp(b_last)
    o_parts, s_parts = [], []
    for h in range(GLA_H):
        ks, vs = slice(h * GLA_DK, (h + 1) * GLA_DK), slice(h * GLA_DV, (h + 1) * GLA_DV)
        sh = S[:, ks]
        sc = jnp.where(msk, _dot_nt(qd[:, ks], ki[:, ks]), 0.0)
        o_parts.append(_dot_nt(qd[:, ks], sh) + _dot(sc, v[:, vs]))
        s_parts.append(dec[:, ks] * sh + _dot_tn(v[:, vs], kd[:, ks]))
    return jnp.concatenate(s_parts, axis=1), jnp.concatenate(o_parts, axis=1)


def _ssd_chunk(S, x, bm, cm, tail, dtb, alog, *, rev):
    L = SSD_L
    msk = _tri(L, rev)
    tri = msk.astype(F32)
    raw = tail[:, SSD_H:2 * SSD_H] if rev else tail[:, 0:SSD_H]
    dt = jax.nn.softplus(raw + dtb)
    dta = dt * (-jnp.exp(alog))
    acum = _dotx(tri, dta)
    a_last = acum[0:1] if rev else acum[L - 1:L]
    wst = dt * jnp.exp(a_last - acum)
    eac = jnp.exp(acum)
    tr = jnp.concatenate([acum, dt, wst, jnp.zeros((L, L - 3 * SSD_H), F32)], axis=1).T
    acum_t, dt_t, wst_t = tr[0:SSD_H], tr[SSD_H:2 * SSD_H], tr[2 * SSD_H:3 * SSD_H]
    decrow = jnp.exp(_dotx(jnp.broadcast_to(a_last, (8, SSD_H)), _head_expand())[0:1])
    lane = lax.broadcasted_iota(jnp.int32, (1, 2 * SSD_P), 1)
    m0 = (lane < SSD_P).astype(F32)
    m1 = 1.0 - m0
    pairs_per_group = SSD_H // SSD_G // 2
    y_parts, s_parts = [], []
    for g in range(SSD_G):
        ns = slice(g * SSD_N, (g + 1) * SSD_N)
        bg, cg = bm[:, ns], cm[:, ns]
        cb = _dot_nt(cg, bg)
        bgt = bg.T
        for jj in range(pairs_per_group):
            j = g * pairs_per_group + jj
            ls = slice(j * 2 * SSD_P, (j + 1) * 2 * SSD_P)
            xp, sp = x[:, ls], S[:, ls]
            xm = jnp.concatenate([xp * m0, xp * m1], axis=0)
            sm = jnp.concatenate([sp * m0, sp * m1], axis=0)
            lhs, bw = [], []
            for h in (2 * j, 2 * j + 1):
                seg = acum[:, h:h + 1] - acum_t[h:h + 1, :]
                lhs.append(cb * jnp.exp(jnp.where(msk, seg, -jnp.inf)) * dt_t[h:h + 1, :])
                bw.append(bgt * wst_t[h:h + 1, :])
            lhs += [cg * eac[:, h:h + 1] for h in (2 * j, 2 * j + 1)]
            y_parts.append(_dot(jnp.concatenate(lhs, axis=1), jnp.concatenate([xm, sm], axis=0)))
            s_parts.append(sp * decrow[:, ls] + _dot(jnp.concatenate(bw, axis=1), xm))
    return jnp.concatenate(s_parts, axis=1), jnp.concatenate(y_parts, axis=1)


def _multi_chunk(chunk_fn, L, subs, nr):
    def fn(S, *args, rev):
        rows, params = args[:nr], args[nr:]
        ys = [None] * subs
        for j in (range(subs - 1, -1, -1) if rev else range(subs)):
            S, ys[j] = chunk_fn(S, *[r[j * L:(j + 1) * L] for r in rows], *params, rev=rev)
        return S, jnp.concatenate(ys, axis=0)

    return fn


def _scan_order(n, nx, rev, backward):
    nc = n - nx

    def fwd(s):
        return (n - 1 - s) if rev else jnp.where(s < nc, s + nx, s - nc)

    return (lambda s: fwd(n - 1 - s)) if backward else fwd


def _scan_fwd(name, chunk_fn, L, n, nx, rev, rows, params, state_shape, out_w):
    order = _scan_order(n, nx, rev, False)
    nr, npar = len(rows), len(params)

    def body(*refs):
        s_scr = refs[-1]

        @pl.when(pl.program_id(0) == 0)
        def _():
            s_scr[...] = jnp.zeros_like(s_scr)

        s_in = s_scr[...]
        y_ref, st_ref = refs[nr + npar], refs[nr + npar + 1]
        st_ref[0] = s_in
        s_new, y = chunk_fn(s_in, *[r[...] for r in refs[:nr]], *[p[...] for p in refs[nr:nr + npar]], rev=rev)
        y_ref[...] = y
        s_scr[...] = s_new

    return pl.pallas_call(
        body, name=name, grid=(n,),
        in_specs=[pl.BlockSpec((L, w), lambda s, c=c: (order(s), c)) for _, w, c in rows] + [_full_spec(p) for p in params],
        out_specs=[pl.BlockSpec((L, out_w), lambda s: (order(s), 0)),
                   pl.BlockSpec((1,) + state_shape, lambda s: (order(s), 0, 0))],
        out_shape=[jax.ShapeDtypeStruct((n * L, out_w), F32), jax.ShapeDtypeStruct((n,) + state_shape, F32)],
        scratch_shapes=[pltpu.VMEM(state_shape, F32)],
        compiler_params=_cp(("arbitrary",)),
    )(*[a for a, _, _ in rows], *params)


def _scan_bwd(name, chunk_fn, L, n, nx, rev, rows, params, states, dy, state_shape, out_w):
    order = _scan_order(n, nx, rev, True)
    nr, npar = len(rows), len(params)

    def body(*refs):
        i = pl.program_id(0)
        ds_scr = refs[-1]
        rv = [r[...] for r in refs[:nr]]
        pv = [p[...] for p in refs[nr:nr + npar]]
        st_ref, dy_ref = refs[nr + npar], refs[nr + npar + 1]
        o_refs = refs[nr + npar + 2:-1]
        p_refs = o_refs[nr:]

        @pl.when(i == 0)
        def _():
            ds_scr[...] = jnp.zeros_like(ds_scr)
            for p_ref in p_refs:
                p_ref[...] = jnp.zeros_like(p_ref)

        _, vjp = jax.vjp(functools.partial(chunk_fn, rev=rev), st_ref[0], *rv, *pv)
        grads = vjp((ds_scr[...], dy_ref[...].astype(F32)))
        ds_scr[...] = grads[0]
        for o_ref, g in zip(o_refs[:nr], grads[1:1 + nr]):
            o_ref[...] = g
        for p_ref, g in zip(p_refs, grads[1 + nr:]):
            p_ref[...] += g

    res = pl.pallas_call(
        body, name=name, grid=(n,),
        in_specs=[pl.BlockSpec((L, w), lambda s, c=c: (order(s), c)) for _, w, c in rows] + [_full_spec(p) for p in params]
        + [pl.BlockSpec((1,) + state_shape, lambda s: (order(s), 0, 0)), pl.BlockSpec((L, out_w), lambda s: (order(s), 0))],
        out_specs=[pl.BlockSpec((L, w), lambda s: (order(s), 0)) for _, w, _ in rows] + [_full_spec(p) for p in params],
        out_shape=[jax.ShapeDtypeStruct((n * L, w), F32) for _, w, _ in rows] + [jax.ShapeDtypeStruct(p.shape, F32) for p in params],
        scratch_shapes=[pltpu.VMEM(state_shape, F32)],
        compiler_params=_cp(("arbitrary",)),
    )(*[a for a, _, _ in rows], *params, states, dy)
    return res[:nr], res[nr:]


CONV_W = 1024
CONV_COLBLK = (0, 1, 4)


def _conv_specs(nb, src_blk):
    halo = TM // 8
    return [pl.BlockSpec((TM, CONV_W), lambda j, i: (i, src_blk(j))),
            pl.BlockSpec((8, CONV_W), lambda j, i: (jnp.maximum(i * halo - 1, 0), src_blk(j))),
            pl.BlockSpec((8, CONV_W), lambda j, i: (jnp.minimum(i * halo + halo, nb * halo - 1), src_blk(j)))]


def _conv_ext(i, nb, cur, prev, nxt):
    has_prev = jnp.logical_and(i > 0, i < nb - 1)
    has_next = i < nb - 2
    return jnp.concatenate([jnp.where(has_prev, prev, 0.0), cur, jnp.where(has_next, nxt, 0.0)], axis=0)


def _conv_taps(ext, w, flip):
    acc = None
    for j in range(SSD_K):
        wj = w[SSD_K - 1 - j:SSD_K - j, :] if flip else w[j:j + 1, :]
        term = wj * ext[6 + j:6 + j + TM, :]
        acc = term if acc is None else acc + term
    return acc


def _conv(name, src, w8, b1, nb, *, permuted_src, act, flip, out_dtype):
    src_blk = (lambda j: jnp.where(j == 2, CONV_COLBLK[2], j)) if permuted_src else (lambda j: j)

    def body(cur, prev, nxt, w_ref, b_ref, o_ref):
        ext = _conv_ext(pl.program_id(1), nb, cur[...].astype(F32), prev[...].astype(F32), nxt[...].astype(F32))
        acc = _conv_taps(ext, w_ref[...], flip)
        if act:
            acc = jax.nn.silu(acc + b_ref[...])
        o_ref[...] = acc.astype(o_ref.dtype)

    return pl.pallas_call(
        body, name=name, grid=(3, nb),
        in_specs=_conv_specs(nb, src_blk) + [pl.BlockSpec((8, CONV_W), lambda j, i: (0, j)), pl.BlockSpec((1, CONV_W), lambda j, i: (0, j))],
        out_specs=pl.BlockSpec((TM, CONV_W), lambda j, i: (i, j)),
        out_shape=jax.ShapeDtypeStruct((nb * TM, 3 * CONV_W), out_dtype),
        compiler_params=_cp(("parallel", "parallel")),
    )(src, src, src, w8, b1)


def _conv_bwd_pre(name, p1, w8, b1, dxbc_parts, nb):
    src_blk = lambda j: jnp.where(j == 2, CONV_COLBLK[2], j)
    xs_parts, bc_parts = dxbc_parts
    n_x, n_bc = len(xs_parts), len(bc_parts)

    def body(*refs):
        cur, prev, nxt, w_ref, b_ref = refs[:5]
        d_refs = refs[5:5 + n_x + n_bc]
        da_ref, dw_ref, db_ref = refs[5 + n_x + n_bc:]
        j, i = pl.program_id(0), pl.program_id(1)
        ext = _conv_ext(i, nb, cur[...], prev[...], nxt[...])
        acc = _conv_taps(ext, w_ref[...], False) + b_ref[...]
        dx = d_refs[0][...]
        for r in d_refs[1:n_x]:
            dx = dx + r[...]
        dbc = jnp.concatenate([d_refs[n_x][...] + d_refs[n_x + 1][...], d_refs[n_x + 2][...] + d_refs[n_x + 3][...]], axis=1)
        dy = jnp.where(j == 2, dbc, dx)
        sg = jax.nn.sigmoid(acc)
        da = dy * (sg + acc * sg * (1.0 - sg))
        da_ref[...] = da

        @pl.when(i == 0)
        def _():
            dw_ref[...] = jnp.zeros_like(dw_ref)
            db_ref[...] = jnp.zeros_like(db_ref)

        rows = [jnp.sum(da * ext[6 + t:6 + t + TM, :], axis=0, keepdims=True) for t in range(SSD_K)]
        dw_ref[...] += jnp.concatenate(rows + [jnp.zeros((8 - SSD_K, CONV_W), F32)], axis=0)
        db_ref[...] += jnp.sum(da, axis=0, keepdims=True)

    x_specs = [pl.BlockSpec((TM, CONV_W), lambda j, i: (i, jnp.minimum(j, 1))) for _ in xs_parts]
    bc_specs = [pl.BlockSpec((TM, 512), lambda j, i: (i, 0)) for _ in bc_parts]
    return pl.pallas_call(
        body, name=name, grid=(3, nb),
        in_specs=_conv_specs(nb, src_blk) + [pl.BlockSpec((8, CONV_W), lambda j, i: (0, j)), pl.BlockSpec((1, CONV_W), lambda j, i: (0, j))]
        + x_specs + bc_specs,
        out_specs=[pl.BlockSpec((TM, CONV_W), lambda j, i: (i, j)), pl.BlockSpec((8, CONV_W), lambda j, i: (0, j)),
                   pl.BlockSpec((1, CONV_W), lambda j, i: (0, j))],
        out_shape=[jax.ShapeDtypeStruct((nb * TM, 3 * CONV_W), F32), jax.ShapeDtypeStruct((8, 3 * CONV_W), F32),
                   jax.ShapeDtypeStruct((1, 3 * CONV_W), F32)],
        compiler_params=_cp(("arbitrary", "arbitrary")),
    )(p1, p1, p1, w8, b1, *xs_parts, *bc_parts)


def _grid_rows(val, a, kb):
    return jnp.concatenate([val[:, t * D:(t + 1) * D] for t in range(kb)], axis=0)


def _perm(name, xc, a, nb):
    n = xc.shape[0]
    b = (n - CTX) // a
    kb = TM // a
    view = xc.reshape(n // b, b * D)

    def body(v_ref, c_ref, o_ref):
        i = pl.program_id(0)

        @pl.when(i < nb - 1)
        def _():
            o_ref[...] = _grid_rows(v_ref[...], a, kb)

        @pl.when(i == nb - 1)
        def _():
            o_ref[...] = c_ref[...]

    return pl.pallas_call(
        body, name=name, grid=(nb,),
        in_specs=[pl.BlockSpec((a, kb * D), lambda i: (0, jnp.minimum(i, nb - 2))), pl.BlockSpec((TM, D), lambda i: (nb - 1, 0))],
        out_specs=pl.BlockSpec((TM, D), lambda i: (i, 0)),
        out_shape=jax.ShapeDtypeStruct((n, D), xc.dtype),
        compiler_params=_cp(("parallel",)),
    )(view, xc)


def _loss_head(x, f, target, modp, g_final, nb, rows_r):
    t_tok = target.shape[0]
    kb = TM // rows_r
    tview = target.reshape(rows_r, (t_tok // rows_r) * D)

    def fn(x_, f_, tgt, modp_, g_, is_ctx):
        xn = x_ + _sel_mod(modp_, is_ctx)[5:6] * f_
        err = _rms(xn) * g_ - tgt
        return 0.5 * jnp.sum(jnp.mean(err * err, axis=-1)) * (1.0 - is_ctx)

    def body(x_ref, f_ref, t_ref, m_ref, g_ref, l_ref, dx_ref, df_ref, dm_ref, dg_ref):
        i = pl.program_id(0)
        is_ctx = (i == nb - 1).astype(F32)
        tgt = _grid_rows(t_ref[...], rows_r, kb)
        l, vjp = jax.vjp(lambda a_, b_, c_, d_: fn(a_, b_, tgt, c_, d_, is_ctx), x_ref[...], f_ref[...], m_ref[...], g_ref[...])
        dx, df, dm, dg = vjp(jnp.ones((), F32))

        @pl.when(i == 0)
        def _():
            l_ref[...] = jnp.zeros_like(l_ref)
            dm_ref[...] = jnp.zeros_like(dm_ref)
            dg_ref[...] = jnp.zeros_like(dg_ref)

        l_ref[...] += jnp.reshape(l, (1, 1))
        dx_ref[...] = dx
        df_ref[...] = df.astype(df_ref.dtype)
        dm_ref[...] += dm
        dg_ref[...] += dg

    rowspec = pl.BlockSpec((TM, D), lambda i: (i, 0))
    return pl.pallas_call(
        body, name="loss_head", grid=(nb,),
        in_specs=[rowspec, rowspec, pl.BlockSpec((rows_r, kb * D), lambda i: (0, jnp.minimum(i, nb - 2))),
                  _full_spec(modp), _full_spec(g_final)],
        out_specs=[pl.BlockSpec((1, 1), lambda i: (0, 0)), rowspec, rowspec, _full_spec(modp), _full_spec(g_final)],
        out_shape=[jax.ShapeDtypeStruct((1, 1), F32), jax.ShapeDtypeStruct(x.shape, F32), jax.ShapeDtypeStruct(x.shape, MXU_DTYPE),
                   jax.ShapeDtypeStruct(modp.shape, F32), jax.ShapeDtypeStruct(g_final.shape, F32)],
        compiler_params=_cp(("arbitrary",)),
    )(x, f, tview, modp, g_final)


def _repack(name, shards, segs, wp):
    nd, kk, ws = shards.shape
    tr = 128
    used = sum(e - s for s, e in segs)

    def body(a_ref, o_ref):
        full = jnp.concatenate([a_ref[d].astype(F32) for d in range(nd)], axis=1)
        parts = [full[:, s:e] for s, e in segs]
        if wp > used:
            parts.append(jnp.zeros((tr, wp - used), F32))
        o_ref[...] = jnp.concatenate(parts, axis=1).astype(o_ref.dtype)

    return pl.pallas_call(
        body, name=name, grid=(kk // tr,),
        in_specs=[pl.BlockSpec((nd, tr, ws), lambda i: (0, i, 0))],
        out_specs=pl.BlockSpec((tr, wp), lambda i: (i, 0)),
        out_shape=jax.ShapeDtypeStruct((kk, wp), MXU_DTYPE),
        compiler_params=_cp(("parallel",)),
    )(shards)


def _unpack(name, dw, segs, ws, out_dtype):
    kk, wp = dw.shape
    tr = 128
    order = sorted(range(len(segs)), key=lambda i: segs[i][0])
    offs, o = [], 0
    for s, e in segs:
        offs.append(o)
        o += e - s

    def body(a_ref, o_ref):
        a = a_ref[...].astype(F32)
        full = jnp.concatenate([a[:, offs[i]:offs[i] + segs[i][1] - segs[i][0]] for i in order], axis=1)
        for d in range(NDEV):
            o_ref[d] = full[:, d * ws:(d + 1) * ws].astype(o_ref.dtype)

    return pl.pallas_call(
        body, name=name, grid=(kk // tr,),
        in_specs=[pl.BlockSpec((tr, wp), lambda i: (i, 0))],
        out_specs=pl.BlockSpec((NDEV, tr, ws), lambda i: (0, i, 0)),
        out_shape=jax.ShapeDtypeStruct((NDEV, kk, ws), out_dtype),
        compiler_params=_cp(("parallel",)),
    )(dw)


def _adam_math(w, g, m, v):
    m = ADAM_B1 * m + (1.0 - ADAM_B1) * g
    v = ADAM_B2 * v + (1.0 - ADAM_B2) * jnp.square(g)
    m_hat = m / (1.0 - ADAM_B1 ** ADAM_STEP)
    v_hat = v / (1.0 - ADAM_B2 ** ADAM_STEP)
    delta = -ADAM_LR * (m_hat / (jnp.sqrt(v_hat) + ADAM_EPS) + ADAM_WD * w)
    return delta, m, v


def _adam(name, w, parts, m, v, after):
    r, c = w.shape
    npart = parts.shape[0]
    tr = _pick(r, (256, 128, 64, 32, 16, 8)) if r * c * 4 > (1 << 20) else r

    def body(w_ref, p_ref, m_ref, v_ref, after_ref, g_ref, d_ref, nm_ref, nv_ref):
        g = p_ref[0].astype(F32)
        for s in range(1, npart):
            g = g + p_ref[s].astype(F32)
        delta, nm, nv = _adam_math(w_ref[...], g, m_ref[...], v_ref[...])
        g_ref[...], d_ref[...], nm_ref[...], nv_ref[...] = g, delta, nm, nv

    spec = pl.BlockSpec((tr, c), lambda i: (i, 0))
    return pl.pallas_call(
        body, name=name, grid=(r // tr,),
        in_specs=[spec, pl.BlockSpec((npart, tr, c), lambda i: (0, i, 0)), spec, spec, ANY],
        out_specs=[spec] * 4, out_shape=[jax.ShapeDtypeStruct((r, c), F32)] * 4,
        compiler_params=_cp(("parallel",)),
    )(w, parts, m, v, after)


def _mod_fwd(c_all, mod_w):
    nl, _, ws = mod_w.shape

    def body(c_ref, w_ref, o_ref):
        o_ref[0] = _dot(jax.nn.silu(c_ref[...]), w_ref[0])

    return pl.pallas_call(
        body, name="mod_fwd", grid=(nl,),
        in_specs=[_full_spec(c_all), pl.BlockSpec((1, D, ws), lambda i: (i, 0, 0))],
        out_specs=pl.BlockSpec((1, 16, ws), lambda i: (i, 0, 0)),
        out_shape=jax.ShapeDtypeStruct((nl, 16, ws), F32),
        compiler_params=_cp(("parallel",)),
    )(c_all, mod_w)


def _mod_bwd(c_all, mod_w, dm):
    nl, _, ws = mod_w.shape

    def body(c_ref, w_ref, d_ref, dw_ref, dc_ref):
        dw_ref[0] = _dot_tn(jax.nn.silu(c_ref[...]), d_ref[0])
        dc_ref[0] = _dot_nt(d_ref[0], w_ref[0])

    return pl.pallas_call(
        body, name="mod_bwd", grid=(nl,),
        in_specs=[_full_spec(c_all), pl.BlockSpec((1, D, ws), lambda i: (i, 0, 0)), pl.BlockSpec((1, 16, ws), lambda i: (i, 0, 0))],
        out_specs=[pl.BlockSpec((1, D, ws), lambda i: (i, 0, 0)), pl.BlockSpec((1, 16, D), lambda i: (i, 0, 0))],
        out_shape=[jax.ShapeDtypeStruct((nl, D, ws), F32), jax.ShapeDtypeStruct((nl, 16, D), F32)],
        compiler_params=_cp(("parallel",)),
    )(c_all, mod_w, dm)


def _sum_parts(name, parts):
    npart, r, c = parts.shape

    def body(p_ref, o_ref):
        g = p_ref[0].astype(F32)
        for s in range(1, npart):
            g = g + p_ref[s].astype(F32)
        o_ref[...] = g

    return pl.pallas_call(body, name=name, out_shape=jax.ShapeDtypeStruct((r, c), F32), compiler_params=_cp())(parts)


MESH = pl.DeviceIdType.MESH
ANY = pl.BlockSpec(memory_space=pl.ANY)
N_PEERS = NDEV - 1


def _mesh_pos():
    return lax.axis_index("x"), lax.axis_index("y"), lax.axis_index("c")


def _slot(px, py, pc):
    return 4 * px + 2 * py + pc


def _two_level_gather(x_refs, o_refs, send_sems, recv_sems, local_sems):
    x, y, c = _mesh_pos()
    me, sibling = (x, y, c), (x, y, 1 - c)
    chips = [(1 - x, y), (x, 1 - y), (1 - x, 1 - y)]
    n = len(x_refs)

    def copy(a, k, block, to, src=None):
        dst = o_refs[a].at[_slot(*block)]
        return pltpu.make_async_remote_copy(src_ref=dst if src is None else src, dst_ref=dst, send_sem=send_sems.at[a, k],
                                            recv_sem=recv_sems.at[a, k], device_id=to, device_id_type=MESH)

    mine = [pltpu.make_async_copy(x_refs[a], o_refs[a].at[_slot(*me)], local_sems.at[a]) for a in range(n)]
    for cp in mine:
        cp.start()
    first = []
    for a in range(n):
        first.append(copy(a, 0, me, sibling, src=x_refs[a]))
        first += [copy(a, 1 + j, me, (*chip, c), src=x_refs[a]) for j, chip in enumerate(chips)]
    for cp in first:
        cp.start()
    passed = []
    for j, chip in enumerate(chips):
        for a in range(n):
            copy(a, 1 + j, (*chip, c), me).wait_recv()
            fwd = copy(a, 4 + j, (*chip, c), sibling)
            fwd.start()
            passed.append(fwd)
    for a in range(n):
        copy(a, 0, sibling, me).wait_recv()
        for j, chip in enumerate(chips):
            copy(a, 4 + j, (*chip, 1 - c), me).wait_recv()
    for cp in first + passed:
        cp.wait_send()
    for cp in mine:
        cp.wait()


def _ag_small(name, x):
    r, c = x.shape

    def body(x_ref, o_ref, send_sems, recv_sems, local_sems):
        _two_level_gather([x_ref], [o_ref], send_sems, recv_sems, local_sems)

    return pl.pallas_call(
        body, name=name, out_shape=jax.ShapeDtypeStruct((NDEV, r, c), x.dtype),
        in_specs=[pl.BlockSpec(memory_space=pltpu.VMEM)], out_specs=pl.BlockSpec(memory_space=pltpu.VMEM),
        scratch_shapes=[pltpu.SemaphoreType.DMA((1, N_PEERS)), pltpu.SemaphoreType.DMA((1, N_PEERS)), pltpu.SemaphoreType.DMA((1,))],
        compiler_params=pltpu.CompilerParams(vmem_limit_bytes=VMEM_LIMIT),
    )(x)


def _ag_big(name, shards):
    n = len(shards)

    def body(*refs):
        _two_level_gather(refs[:n], refs[n:2 * n], *refs[2 * n:])

    return pl.pallas_call(
        body, name=name, out_shape=[jax.ShapeDtypeStruct((NDEV,) + s.shape, s.dtype) for s in shards],
        in_specs=[ANY] * n, out_specs=[ANY] * n,
        scratch_shapes=[pltpu.SemaphoreType.DMA((n, N_PEERS)), pltpu.SemaphoreType.DMA((n, N_PEERS)), pltpu.SemaphoreType.DMA((n,))],
    )(*shards)


HBM = pl.BlockSpec(memory_space=pltpu.HBM)
SEM = pl.BlockSpec(memory_space=pltpu.SEMAPHORE)
EFFECT = pltpu.SideEffectType.DATAFLOW_SIDE_EFFECTING


def _peers(x, y, c):
    return [(k - 1, ((1 - x) if k & 4 else x, (1 - y) if k & 2 else y, (1 - c) if k & 1 else c)) for k in range(1, NDEV)]


def _xchg_copy(src_refs, land_refs, send_sems, recv_sems, a, k, peer, me, scatter):
    src = src_refs[a].at[_slot(*peer)] if scatter else src_refs[a]
    return pltpu.make_async_remote_copy(src_ref=src, dst_ref=land_refs[a].at[me], send_sem=send_sems.at[a * N_PEERS + k],
                                        recv_sem=recv_sems.at[a * N_PEERS + k], device_id=peer, device_id_type=MESH)


def _xchg_start(name, srcs, lands, deps, scatter):
    n, nd = len(srcs), len(deps)

    def body(*refs):
        src_refs, land_refs = refs[:n], refs[n:2 * n]
        send_sems, recv_sems, token = refs[2 * n + nd], refs[2 * n + nd + 1], refs[-1]
        x, y, c = _mesh_pos()
        me = _slot(x, y, c)
        for k, peer in _peers(x, y, c):
            for a in range(n):
                _xchg_copy(src_refs, land_refs, send_sems, recv_sems, a, k, peer, me, scatter).start()
        token[...] = jnp.zeros_like(token)

    res = pl.pallas_call(
        body, name=name,
        out_shape=(pltpu.SemaphoreType.DMA((n * N_PEERS,)), pltpu.SemaphoreType.DMA((n * N_PEERS,)),
                   *[pltpu.HBM(s.shape, s.dtype) for s in srcs], *[pltpu.HBM(s.shape, s.dtype) for s in lands],
                   jax.ShapeDtypeStruct((8, 128), F32)),
        in_specs=[HBM] * (2 * n) + [ANY] * nd,
        out_specs=(SEM, SEM, *([HBM] * (2 * n)), pl.BlockSpec(memory_space=pltpu.VMEM)),
        input_output_aliases={i: 2 + i for i in range(2 * n)},
        compiler_params=pltpu.CompilerParams(has_side_effects=EFFECT),
    )(*[pltpu.with_memory_space_constraint(s, pltpu.HBM) for s in srcs],
      *[pltpu.with_memory_space_constraint(s, pltpu.HBM) for s in lands], *deps)
    return res[0], res[1], res[2:2 + n], res[2 + n:2 + 2 * n], res[-1]


def _xchg_wait(name, send_sems, recv_sems, srcs, lands, after, scatter):
    n = len(srcs)

    def body(*refs):
        src_refs, land_refs = refs[:n], refs[n:2 * n]
        s_sems, r_sems = refs[2 * n], refs[2 * n + 1]
        x, y, c = _mesh_pos()
        me = _slot(x, y, c)
        for k, peer in _peers(x, y, c):
            for a in range(n):
                cp = _xchg_copy(src_refs, land_refs, s_sems, r_sems, a, k, peer, me, scatter)
                cp.wait_send()
                cp.wait_recv()

    res = pl.pallas_call(
        body, name=name,
        out_shape=[pltpu.HBM(s.shape, s.dtype) for s in srcs] + [pltpu.HBM(s.shape, s.dtype) for s in lands],
        in_specs=[HBM] * (2 * n) + [SEM, SEM, ANY], out_specs=[HBM] * (2 * n),
        input_output_aliases={i: i for i in range(2 * n)},
        compiler_params=pltpu.CompilerParams(has_side_effects=EFFECT),
    )(*srcs, *lands, send_sems, recv_sems, after)
    return res[n:]


def _landing(name, src, me, scatter):
    r, c = src.shape[-2:]
    tr = r if r * c * src.dtype.itemsize <= (2 << 20) else _pick(r, (256, 128, 64, 32, 16))

    def body(me_ref, s_ref, o_ref):
        o_ref[...] = s_ref[...].reshape(o_ref.shape)

    src_spec = pl.BlockSpec((1, tr, c), lambda i, me_ref: (me_ref[0], i, 0)) if scatter else pl.BlockSpec((tr, c), lambda i, me_ref: (i, 0))
    return pl.pallas_call(
        body, name=name, out_shape=jax.ShapeDtypeStruct((NDEV, r, c), src.dtype),
        grid_spec=pltpu.PrefetchScalarGridSpec(
            num_scalar_prefetch=1, grid=(r // tr,), in_specs=[src_spec],
            out_specs=pl.BlockSpec((1, tr, c), lambda i, me_ref: (me_ref[0], i, 0))),
        compiler_params=_cp(("parallel",)),
    )(jnp.reshape(me, (1,)).astype(jnp.int32), src)


STAGES = ("l0_mixer", "l0_ffn", "l1_mixer", "l1_ffn")
STAGE_LAYOUT = {"l0_mixer": (AB_SEGS, AB_P), "l1_mixer": (SSD_SEGS, SSD_P_W)}


class _Exchange:
    def __init__(self, shards, me):
        self.shards, self.me = shards, me
        self.pending, self.pending_grads, self.recv = None, None, {}

    def _layout(self, stage):
        ws = self.shards[stage][0].shape[-1]
        return STAGE_LAYOUT.get(stage, (((0, NDEV * ws),), NDEV * ws)) + (ws,)

    def _start_gather(self, stage, deps):
        srcs = list(self.shards[stage])
        lands = [_landing("own_%s_%d" % (stage, a), s, self.me, False) for a, s in enumerate(srcs)]
        return _xchg_start("gather_start_" + stage, srcs, lands, deps, False)

    def get(self, stage, dep, thread):
        i = STAGES.index(stage)
        if i == 0:
            g_in, g_out = _ag_big("gather_" + stage, list(self.shards[stage]))
            deps = [g_out, dep]
        else:
            ss, rs, srcs, lands, _ = self.pending
            g_in, g_out = _xchg_wait("gather_wait_" + stage, ss, rs, srcs, lands, dep, False)
            self.pending, deps = None, [g_out]
        if i + 1 < len(STAGES):
            self.pending = self._start_gather(STAGES[i + 1], deps)
            thread = thread + self.pending[4][0, 0]
        segs, wp, _ = self._layout(stage)
        return _repack("repack_" + stage, g_in, segs, wp), g_out.reshape(-1, D), thread

    def put(self, stage, d_in, d_out, thread):
        segs, _, ws = self._layout(stage)
        parts = [_unpack("unpack_" + stage, d_in, segs, ws, MXU_DTYPE), d_out.reshape(NDEV, -1, D)]
        deps = [parts[0]]
        if self.pending_grads is not None:
            deps = [self.finish(parts[0])[0]]
        self.staged = (stage, parts)
        return thread if stage == STAGES[0] else thread + self.start_last(deps)[0, 0]

    def start_last(self, deps):
        stage, parts = self.staged
        lands = [_landing("own_grad_%s_%d" % (stage, a), p, self.me, True) for a, p in enumerate(parts)]
        self.pending_grads = (stage,) + _xchg_start("scatter_start_" + stage, parts, lands, deps, True)
        return self.pending_grads[5]

    def finish(self, after):
        stage, ss, rs, srcs, lands, _ = self.pending_grads
        self.recv[stage] = _xchg_wait("scatter_wait_" + stage, ss, rs, srcs, lands, after, True)
        self.pending_grads = None
        return self.recv[stage]


def _ffn_fwd(tag, h, w_in, w_out, nb, cb):
    pf = _mm(tag + "_ffn_in", h, w_in, "nn", MXU_DTYPE)
    (act,) = _rowwise(tag + "_swiglu", _fn_swiglu, nb, cb, [_row(pf)], [], [(D_FF, MXU_DTYPE)])
    return pf, act, _mm(tag + "_ffn_out", act, w_out, "nn", F32)


def _ffn_bwd(tag, h, pf, act, df, w_in, w_out, nb, cb):
    dw_out = _mm(tag + "_ffn_out_dw", act, df, "tn", MXU_DTYPE)
    dact = _mm(tag + "_ffn_out_dx", df, w_out, "nt", MXU_DTYPE)
    (dpf,), _ = _rowwise_vjp(tag + "_swiglu_bwd", _fn_swiglu, nb, cb, [_row(pf)], [], [_row(dact)], [(0, MXU_DTYPE, None)])
    dw_in = _mm(tag + "_ffn_in_dw", h, dpf, "tn", MXU_DTYPE)
    dh = _mm(tag + "_ffn_in_dx", dpf, w_in, "nt", MXU_DTYPE)
    return dw_out, dw_in, dh


def _local_step(x, ctx, target, mod, P, comm):
    T = x.shape[0]
    N = T + CTX
    nb, cb = N // TM, N // TM - 1
    R = T // GRID_W
    mod0, mod1 = mod[0], mod[1]
    ng = P["norm_g"]
    g00, g01, g10, g11 = ng[0, 0][None], ng[0, 1][None], ng[1, 0][None], ng[1, 1][None]
    pre = functools.partial(_fn_prenorm, a=0, b=1)
    rpre = functools.partial(_fn_resid_prenorm, gi=2, a=3, b=4)
    res5 = functools.partial(_fn_resid, gi=5)
    dirs = (("f", False), ("b", True))

    xc0 = jnp.concatenate([x, ctx], axis=0)
    w_ab_in, w_ab_out, g00 = comm.get("l0_mixer", mod, g00)
    (h0,) = _rowwise("l0_prenorm", pre, nb, cb, [_row(xc0)], [g00, mod0], [(D, MXU_DTYPE)])
    p0 = _mm("l0_in", h0, w_ab_in, "nn", F32)
    gla_rows = [(p0, 512, 0), (p0, 256, 8), (p0, 256, 9), (p0, 128, 20)]
    gla_blk = _multi_chunk(_gla_chunk, GLA_L, TM // GLA_L, len(gla_rows))
    gla_par = {d: [P["ab_gate_w"][int(r)], P["ab_gate_b"][int(r)][None]] for d, r in dirs}
    gla_state = (GLA_DV, GLA_H * GLA_DK)
    o, st0 = {}, {}
    for d, rev in dirs:
        o[d], st0[d] = _scan_fwd("gla_fwd_" + d, gla_blk, TM, nb, cb, rev, gla_rows, gla_par[d], gla_state, GLA_H * GLA_DV)
    n128, cb128 = N // GMLP_L, T // GMLP_L
    mix_rows = [_row(o["f"], tm=GMLP_L), _row(o["b"], tm=GMLP_L)] + [_row(p0, 512, j, tm=GMLP_L) for j in (1, 2, 3)]
    mix_par = [P["ab_gla_norm_g"], P["ab_vnorm_g"], P["ab_spatial_w"].reshape(GMLP_G * GMLP_L, GMLP_L), P["ab_spatial_b"].T]
    (cat0,) = _rowwise("l0_mix", _fn_mixpost, n128, cb128, mix_rows, mix_par, [(D, MXU_DTYPE)], tm=GMLP_L)
    y0 = _mm("l0_out", cat0, w_ab_out, "nn", F32)
    w_fi0, w_fo0, g01 = comm.get("l0_ffn", y0, g01)
    x1, h1 = _rowwise("l0_ffn_prenorm", rpre, nb, cb, [_row(xc0), _row(y0)], [g01, mod0, mod0], [(D, F32), (D, MXU_DTYPE)])
    pf0, act0, f0 = _ffn_fwd("l0", h1, w_fi0, w_fo0, nb, cb)
    (x2,) = _rowwise("l0_resid", res5, nb, cb, [_row(x1), _row(f0)], [mod0], [(D, F32)])
    x2p = _perm("to_col_major", x2, R, nb)

    w_ssd_in, w_ssd_out, g10 = comm.get("l1_mixer", x2p, g10)
    (h2,) = _rowwise("l1_prenorm", pre, nb, cb, [_row(x2p)], [g10, mod1], [(D, MXU_DTYPE)])
    p1 = _mm("l1_in", h2, w_ssd_in, "nn", F32)
    conv_w8 = jnp.concatenate([P["ssd_conv_w"], jnp.zeros((8 - SSD_K, 3 * CONV_W), F32)], axis=0)
    xbc = _conv("l1_conv", p1, conv_w8, P["ssd_conv_b"], nb, permuted_src=True, act=True, flip=False, out_dtype=F32)
    ssd_rows = [(xbc, SSD_INNER, 0), (xbc, 512, 4), (xbc, 512, 5), (p1, 128, 40)]
    ssd_blk = _multi_chunk(_ssd_chunk, SSD_L, TM // SSD_L, len(ssd_rows))
    ssd_par = {d: [P["ssd_dt_bias"][int(r)][None], P["ssd_a_log"][int(r)][None]] for d, r in dirs}
    ssd_state = (SSD_N, SSD_INNER)
    ys, st1 = {}, {}
    for d, rev in dirs:
        ys[d], st1[d] = _scan_fwd("ssd_fwd_" + d, ssd_blk, TM, nb, cb, rev, ssd_rows, ssd_par[d], ssd_state, SSD_INNER)
    fin_rows = [_row(ys["f"]), _row(ys["b"]), _row(xbc, SSD_INNER, 0), _row(p1, SSD_INNER, 1)]
    fin_par = [P["ssd_d"], P["ssd_norm_g"]]
    (yn,) = _rowwise("l1_finish", _fn_ssd_finish, nb, cb, fin_rows, fin_par, [(SSD_INNER, MXU_DTYPE)])
    y1 = _mm("l1_out", yn, w_ssd_out, "nn", F32)
    w_fi1, w_fo1, g11 = comm.get("l1_ffn", y1, g11)
    x3, h3 = _rowwise("l1_ffn_prenorm", rpre, nb, cb, [_row(x2p), _row(y1)], [g11, mod1, mod1], [(D, F32), (D, MXU_DTYPE)])
    pf1, act1, f1 = _ffn_fwd("l1", h3, w_fi1, w_fo1, nb, cb)
    loss, dx3, df1, dm1_j, d_final_g = _loss_head(x3, f1, target, mod1, P["final_norm_g"], nb, R)

    dP = {"final_norm_g": d_final_g}
    dwo1, dwi1, dh3 = _ffn_bwd("l1", h3, pf1, act1, df1, w_fi1, w_fo1, nb, cb)
    g11 = comm.put("l1_ffn", dwi1, dwo1, g11)
    (dx2p_a, dy1), (dg11, dm1_a, dm1_b) = _rowwise_vjp(
        "l1_ffn_prenorm_bwd", rpre, nb, cb, [_row(x2p), _row(y1)], [g11, mod1, mod1], [_row(dx3), _row(dh3)],
        [(0, F32, None), (1, MXU_DTYPE, None)])
    d_ssd_out = _mm("l1_out_dw", yn, dy1, "tn", MXU_DTYPE)
    dyn = _mm("l1_out_dx", dy1, w_ssd_out, "nt", MXU_DTYPE)
    (dys, dxs, dz), (dP["ssd_d"], dP["ssd_norm_g"]) = _rowwise_vjp(
        "l1_finish_bwd", _fn_ssd_finish, nb, cb, fin_rows, fin_par, [_row(dyn)],
        [(0, F32, None), (2, F32, None), (3, MXU_DTYPE, None)])
    dxp, dbp, dcp, dtl, ddtb, dalog = [], [], [], [], [], []
    for d, rev in dirs:
        (dx_, db_, dc_, dt_), (ddtb_, dalog_) = _scan_bwd("ssd_bwd_" + d, ssd_blk, TM, nb, cb, rev, ssd_rows, ssd_par[d],
                                                          st1[d], dys, ssd_state, SSD_INNER)
        dxp.append(dx_); dbp.append(db_); dcp.append(dc_); dtl.append(dt_); ddtb.append(ddtb_); dalog.append(dalog_)
    dP["ssd_dt_bias"] = jnp.concatenate(ddtb, axis=0)
    dP["ssd_a_log"] = jnp.concatenate(dalog, axis=0)
    dacc, dcw8, dP["ssd_conv_b"] = _conv_bwd_pre("l1_conv_bwd", p1, conv_w8, P["ssd_conv_b"],
                                                  (dxp + [dxs], [dbp[0], dbp[1], dcp[0], dcp[1]]), nb)
    dP["ssd_conv_w"] = dcw8[:SSD_K]
    dpc = _conv("l1_conv_dx", dacc, conv_w8, jnp.zeros((1, 3 * CONV_W), F32), nb, permuted_src=False, act=False, flip=True,
                out_dtype=MXU_DTYPE)
    cat1 = functools.partial(_fn_concat, sums=(1, 1, 1, 2), pad=SSD_P_W - 5248)
    (dp1,) = _rowwise("l1_dp", cat1, nb, cb, [_row(dpc, SSD_INNER, 0), _row(dz), _row(dpc, 1024, 2), _row(dtl[0]), _row(dtl[1])],
                      [], [(SSD_P_W, MXU_DTYPE)])
    g10 = comm.put("l1_mixer", _mm("l1_in_dw", h2, dp1, "tn", F32), d_ssd_out, g10)
    dh2 = _mm("l1_in_dx", dp1, w_ssd_in, "nt", MXU_DTYPE)
    (dx2p,), (dg10, dm1_f) = _rowwise_vjp("l1_prenorm_bwd", pre, nb, cb, [_row(x2p)], [g10, mod1], [_row(dh2)],
                                          [(0, F32, _row(dx2p_a))])
    dx2 = _perm("to_row_major", dx2p, GRID_W, nb)

    (dx1_a, df0), (dm0_e,) = _rowwise_vjp("l0_resid_bwd", res5, nb, cb, [_row(x1), _row(f0)], [mod0], [_row(dx2)],
                                          [(0, F32, None), (1, MXU_DTYPE, None)])
    dwo0, dwi0, dh1 = _ffn_bwd("l0", h1, pf0, act0, df0, w_fi0, w_fo0, nb, cb)
    g01 = comm.put("l0_ffn", dwi0, dwo0, g01)
    (dxc0_a, dy0), (dg01, dm0_a, dm0_b) = _rowwise_vjp(
        "l0_ffn_prenorm_bwd", rpre, nb, cb, [_row(xc0), _row(y0)], [g01, mod0, mod0], [_row(dx1_a), _row(dh1)],
        [(0, F32, None), (1, MXU_DTYPE, None)])
    d_ab_out = _mm("l0_out_dw", cat0, dy0, "tn", MXU_DTYPE)
    dcat0 = _mm("l0_out_dx", dy0, w_ab_out, "nt", MXU_DTYPE)
    (do, dr, du, dgm), (dP["ab_gla_norm_g"], dP["ab_vnorm_g"], dsw, dsb_t) = _rowwise_vjp(
        "l0_mix_bwd", _fn_mixpost, n128, cb128, mix_rows, mix_par, [_row(dcat0, tm=GMLP_L)],
        [(0, F32, None), (2, MXU_DTYPE, None), (3, MXU_DTYPE, None), (4, MXU_DTYPE, None)], tm=GMLP_L)
    dP["ab_spatial_w"] = dsw.reshape(GMLP_G, GMLP_L, GMLP_L)
    dP["ab_spatial_b"] = dsb_t.T
    gl, dgw, dgb = [], [], []
    for d, rev in dirs:
        g4, (dgw_, dgb_) = _scan_bwd("gla_bwd_" + d, gla_blk, TM, nb, cb, rev, gla_rows, gla_par[d], st0[d], do,
                                     gla_state, GLA_H * GLA_DV)
        gl.append(g4); dgw.append(dgw_[None]); dgb.append(dgb_)
    dP["ab_gate_w"] = jnp.concatenate(dgw, axis=0)
    dP["ab_gate_b"] = jnp.concatenate(dgb, axis=0)
    cat0f = functools.partial(_fn_concat, sums=(2, 1, 1, 1, 2, 2, 2))
    (dp0,) = _rowwise("l0_dp", cat0f, nb, cb,
                      [_row(gl[0][0]), _row(gl[1][0]), _row(dr), _row(du), _row(dgm), _row(gl[0][1]), _row(gl[1][1]),
                       _row(gl[0][2]), _row(gl[1][2]), _row(gl[0][3]), _row(gl[1][3])], [], [(AB_P, MXU_DTYPE)])
    g00 = comm.put("l0_mixer", _mm("l0_in_dw", h0, dp0, "tn", F32), d_ab_out, g00)
    dh0 = _mm("l0_in_dx", dp0, w_ab_in, "nt", MXU_DTYPE)
    (dxc0,), (dg00, dm0_s) = _rowwise_vjp("l0_prenorm_bwd", pre, nb, cb, [_row(xc0)], [g00, mod0], [_row(dh0)],
                                          [(0, F32, _row(dxc0_a))])
    dP["norm_g"] = jnp.concatenate([dg00, dg01, dg10, dg11], axis=0).reshape(2, 2, D)
    dmod = jnp.stack([dm0_s + dm0_a + dm0_b + dm0_e, dm1_f + dm1_a + dm1_b + dm1_j])
    return loss, dxc0[:T], dmod, dP


WEIGHTS = ("c_ctx", "mod_w", "mod_b", "norm_g", "ffn_w_in", "ffn_w_out", "ab_w_in", "ab_gate_w", "ab_gate_b", "ab_gla_norm_g",
           "ab_vnorm_g", "ab_spatial_w", "ab_spatial_b", "ab_w_out", "ssd_w_in", "ssd_conv_w", "ssd_conv_b", "ssd_dt_bias",
           "ssd_a_log", "ssd_d", "ssd_norm_g", "ssd_w_out", "final_norm_g")
SMALL_SHARDED = ("norm_g", "ab_gate_w", "ab_gate_b", "ssd_conv_w", "ssd_conv_b", "ssd_norm_g")
SMALL = ("c_ctx", "mod_b", "norm_g", "ab_gate_w", "ab_gate_b", "ab_gla_norm_g", "ab_vnorm_g", "ab_spatial_w", "ab_spatial_b",
         "ssd_conv_w", "ssd_conv_b", "ssd_dt_bias", "ssd_a_log", "ssd_d", "ssd_norm_g", "final_norm_g")
LANES = 1024


def _pack(arrs, rows_multiple=8):
    flat = jnp.concatenate([a.reshape(-1).astype(F32) for a in arrs])
    rows = -(-flat.shape[0] // LANES)
    rows = -(-rows // rows_multiple) * rows_multiple
    return jnp.pad(flat, (0, rows * LANES - flat.shape[0])).reshape(rows, LANES)


def _unpack_flat(buf, shapes):
    lead = buf.shape[:-2]
    flat = buf.reshape(lead + (-1,))
    out, o = [], 0
    for s in shapes:
        n = math.prod(s)
        out.append(flat[..., o:o + n].reshape(lead + tuple(s)))
        o += n
    return out


def _unshard(g):
    g = jnp.moveaxis(g, 0, -2)
    return g.reshape(g.shape[:-2] + (g.shape[-2] * g.shape[-1],))


def _my_shard(full, me, ws):
    return lax.dynamic_slice_in_dim(full, me * ws, ws, axis=full.ndim - 1)


def _silu_vjp(cvec, dsc):
    def body(c_ref, d_ref, o_ref):
        _, vjp = jax.vjp(jax.nn.silu, c_ref[...])
        o_ref[...] = vjp(d_ref[...])[0]

    return pl.pallas_call(body, name="c_ctx_bwd", out_shape=jax.ShapeDtypeStruct(cvec.shape, F32), compiler_params=_cp())(cvec, dsc)


def kernel(x, c, ctx, c_ctx, mod_w, mod_b, norm_g, ffn_w_in, ffn_w_out, ab_w_in, ab_gate_w, ab_gate_b, ab_gla_norm_g, ab_vnorm_g, ab_spatial_w, ab_spatial_b, ab_w_out, ssd_w_in, ssd_conv_w, ssd_conv_b, ssd_dt_bias, ssd_a_log, ssd_d, ssd_norm_g, ssd_w_out, final_norm_g, loss_target, m_c_ctx, m_mod_w, m_mod_b, m_norm_g, m_ffn_w_in, m_ffn_w_out, m_ab_w_in, m_ab_gate_w, m_ab_gate_b, m_ab_gla_norm_g, m_ab_vnorm_g, m_ab_spatial_w, m_ab_spatial_b, m_ab_w_out, m_ssd_w_in, m_ssd_conv_w, m_ssd_conv_b, m_ssd_dt_bias, m_ssd_a_log, m_ssd_d, m_ssd_norm_g, m_ssd_w_out, m_final_norm_g, v_c_ctx, v_mod_w, v_mod_b, v_norm_g, v_ffn_w_in, v_ffn_w_out, v_ab_w_in, v_ab_gate_w, v_ab_gate_b, v_ab_gla_norm_g, v_ab_vnorm_g, v_ab_spatial_w, v_ab_spatial_b, v_ab_w_out, v_ssd_w_in, v_ssd_conv_w, v_ssd_conv_b, v_ssd_dt_bias, v_ssd_a_log, v_ssd_d, v_ssd_norm_g, v_ssd_w_out, v_final_norm_g):
    a = dict(locals())
    me = _slot(*_mesh_pos())
    ws_mod = mod_w.shape[-1]

    fwd_small = [c] + [a[k] for k in SMALL_SHARDED]
    g_small = _ag_small("gather_small", _pack(fwd_small))
    parts = _unpack_flat(g_small, [t.shape for t in fwd_small])
    c_rows = parts[0].reshape(NDEV, D)
    full = {k: _unshard(p) for k, p in zip(SMALL_SHARDED, parts[1:])}
    c_all = jnp.concatenate([c_rows, c_ctx[None], jnp.zeros((7, D), F32)], axis=0)
    m_all = _ag_small("gather_mod", _mod_fwd(c_all, mod_w).reshape(2 * 16, ws_mod)).reshape(NDEV, 2, 16, ws_mod)
    m_mine = lax.dynamic_index_in_dim(m_all, me, axis=2, keepdims=False)
    mx = jnp.moveaxis(m_mine, 0, 1).reshape(2, N_MOD, D) + mod_b.reshape(2, N_MOD, D)
    mc = jnp.moveaxis(m_all[:, :, 8, :], 0, 1).reshape(2, N_MOD, D) + mod_b.reshape(2, N_MOD, D)
    pad2 = jnp.zeros((2, 2, D), F32)
    mod = jnp.concatenate([mx, pad2, mc, pad2], axis=1)

    big = {"l0_mixer": (ab_w_in[0], ab_w_out[0]), "l0_ffn": (ffn_w_in[0], ffn_w_out[0]),
           "l1_mixer": (ssd_w_in[0], ssd_w_out[0]), "l1_ffn": (ffn_w_in[1], ffn_w_out[1])}
    comm = _Exchange({k: tuple(w.astype(MXU_DTYPE) for w in v) for k, v in big.items()}, me)
    P = {
        "norm_g": full["norm_g"], "ab_gate_w": full["ab_gate_w"][0], "ab_gate_b": full["ab_gate_b"][0],
        "ab_gla_norm_g": ab_gla_norm_g, "ab_vnorm_g": ab_vnorm_g, "ab_spatial_w": ab_spatial_w[0], "ab_spatial_b": ab_spatial_b[0],
        "ssd_conv_w": full["ssd_conv_w"][0], "ssd_conv_b": full["ssd_conv_b"], "ssd_dt_bias": ssd_dt_bias[0],
        "ssd_a_log": ssd_a_log[0], "ssd_d": ssd_d, "ssd_norm_g": full["ssd_norm_g"], "final_norm_g": final_norm_g[None],
    }

    loss, grad_x, dmod, dP = _local_step(x[0], ctx[0], loss_target[0], mod, P, comm)

    dmx, dmc = dmod[:, 0:N_MOD].reshape(2, N_MOD * D), dmod[:, 8:8 + N_MOD].reshape(2, N_MOD * D)
    small_names = ("ab_gate_w", "ab_gate_b", "ab_gla_norm_g", "ab_vnorm_g", "ab_spatial_w", "ab_spatial_b", "norm_g", "ssd_conv_w",
                   "ssd_conv_b", "ssd_dt_bias", "ssd_a_log", "ssd_d", "ssd_norm_g", "final_norm_g")
    bwd_small = [dP[k] for k in small_names] + [dmc, dmx]
    shapes = [t.shape for t in bwd_small]
    g_bwd = _ag_small("gather_small_grads", _pack(bwd_small))
    summed = _unpack_flat(_sum_parts("sum_small_grads", g_bwd), shapes)
    gfull = dict(zip(small_names, summed[:-2]))
    dmc_sum, dmx_sum = summed[-2], summed[-1]
    dmx_all = _unpack_flat(g_bwd, shapes)[-1]
    dmx_sh = jnp.moveaxis(_my_shard(dmx_all, me, ws_mod), 0, 1)
    dm = jnp.concatenate([dmx_sh, _my_shard(dmc_sum, me, ws_mod)[:, None, :], jnp.zeros((2, 7, ws_mod), F32)], axis=1)
    d_mod_w, dsc = _mod_bwd(c_all, mod_w, dm)
    dsc_ctx = (dsc[0, 8] + dsc[1, 8])[None]
    dsc_all = _ag_small("gather_c_ctx_grad", jnp.concatenate([dsc_ctx, jnp.zeros((7, D), F32)], axis=0))
    d_c_ctx = _silu_vjp(c_ctx[None], _sum_parts("sum_c_ctx_grad", dsc_all)[0:1])[0]

    g_small_w = {
        "c_ctx": d_c_ctx, "mod_b": dmx_sum + dmc_sum, "norm_g": gfull["norm_g"], "ab_gate_w": gfull["ab_gate_w"][None],
        "ab_gate_b": gfull["ab_gate_b"][None], "ab_gla_norm_g": gfull["ab_gla_norm_g"], "ab_vnorm_g": gfull["ab_vnorm_g"],
        "ab_spatial_w": gfull["ab_spatial_w"][None], "ab_spatial_b": gfull["ab_spatial_b"][None], "ssd_conv_w": gfull["ssd_conv_w"][None],
        "ssd_conv_b": gfull["ssd_conv_b"], "ssd_dt_bias": gfull["ssd_dt_bias"][None], "ssd_a_log": gfull["ssd_a_log"][None],
        "ssd_d": gfull["ssd_d"], "ssd_norm_g": gfull["ssd_norm_g"], "final_norm_g": gfull["final_norm_g"][0],
    }
    for k in SMALL_SHARDED:
        g_small_w[k] = _my_shard(g_small_w[k], me, a[k].shape[-1])
    token = comm.start_last([d_c_ctx])
    res = _adam("adam_small", _pack([a[k] for k in SMALL]), _pack([g_small_w[k] for k in SMALL])[None],
                _pack([a["m_" + k] for k in SMALL]), _pack([a["v_" + k] for k in SMALL]), token)
    out = {k: vals for k, vals in zip(SMALL, zip(*[_unpack_flat(r, [a[k].shape for k in SMALL]) for r in res]))}

    def adam_big(name, w2d, parts3d, m2d, v2d, shape):
        return tuple(r.reshape(shape) for r in _adam(name, w2d, parts3d, m2d, v2d, token))

    out["mod_w"] = adam_big("adam_mod_w", mod_w.reshape(-1, ws_mod), d_mod_w.reshape(1, -1, ws_mod), m_mod_w.reshape(-1, ws_mod),
                            v_mod_w.reshape(-1, ws_mod), mod_w.shape)
    ffn = {}
    for i, stage in enumerate(("l0_ffn", "l1_ffn")):
        for j, k in enumerate(("ffn_w_in", "ffn_w_out")):
            ffn[k, i] = adam_big("adam_%s%d" % (k, i), a[k][i], comm.recv[stage][j], a["m_" + k][i], a["v_" + k][i], a[k].shape[1:])
    for k in ("ffn_w_in", "ffn_w_out"):
        out[k] = tuple(jnp.stack(t) for t in zip(ffn[k, 0], ffn[k, 1]))
    for j, k in enumerate(("ssd_w_in", "ssd_w_out")):
        out[k] = adam_big("adam_" + k, a[k][0], comm.recv["l1_mixer"][j], a["m_" + k][0], a["v_" + k][0], a[k].shape)
    recv_ab = comm.finish(out["ssd_w_out"][3])
    for j, k in enumerate(("ab_w_in", "ab_w_out")):
        out[k] = adam_big("adam_" + k, a[k][0], recv_ab[j], a["m_" + k][0], a["v_" + k][0], a[k].shape)

    loss_all = lax.psum(loss[0, 0], ("x", "y", "c"))
    return (loss_all, grad_x[None], *[out[k][0] for k in WEIGHTS], *[out[k][1] for k in WEIGHTS],
            *[out[k][2] for k in WEIGHTS], *[out[k][3] for k in WEIGHTS])
```

```python
import functools
import math

import jax
import jax.numpy as jnp
from jax import lax
from jax.experimental import pallas as pl
from jax.experimental.pallas import tpu as pltpu

F32 = jnp.float32
BF16 = jnp.bfloat16
MXU_DTYPE = jnp.bfloat16
HI = lax.Precision.HIGHEST

D = 1024
NDEV = 8
N_MOD = 6
EPS = 1e-6
GRID_W = 64
CTX = 256
TM = 256
D_FF = 2816
GLA_H, GLA_DK, GLA_DV, GLA_LR, GLA_TAU, GLA_L = 4, 64, 128, 16, 16.0, 64
GMLP_G, GMLP_C, GMLP_L = 4, 128, 128
SSD_H, SSD_P, SSD_G, SSD_N, SSD_L, SSD_K = 32, 64, 4, 128, 128, 5
SSD_INNER = SSD_H * SSD_P
AB_IN = 2592
SSD_IN = 5184
AB_SEGS = ((256, 768), (1056, 1568), (1568, 2080), (2080, 2592), (0, 256), (800, 1056), (768, 800))
AB_P = 2688
SSD_SEGS = ((0, 2048), (3136, 5184), (2048, 2560), (2560, 3072), (3072, 3136))
SSD_P_W = 5376
VMEM_LIMIT = 56 * 1024 * 1024

ADAM_LR, ADAM_B1, ADAM_B2, ADAM_EPS, ADAM_WD, ADAM_STEP = 0.001, 0.9, 0.999, 1e-08, 0.01, 10


def _cp(sem=None, **kw):
    return pltpu.CompilerParams(dimension_semantics=sem, vmem_limit_bytes=VMEM_LIMIT, **kw)


def _dot(a, b, dims=(((1,), (0,)), ((), ()))):
    return lax.dot_general(a.astype(MXU_DTYPE), b.astype(MXU_DTYPE), dims, preferred_element_type=F32)


def _dot_nt(a, b):
    return _dot(a, b, (((1,), (1,)), ((), ())))


def _dot_tn(a, b):
    return _dot(a, b, (((0,), (0,)), ((), ())))


def _dotx(a, b, dims=(((1,), (0,)), ((), ()))):
    return lax.dot_general(a, b, dims, precision=HI, preferred_element_type=F32)


def _rms(x):
    return x * lax.rsqrt(jnp.mean(x * x, axis=-1, keepdims=True) + EPS)


def _pick(n, prefs):
    for p in prefs:
        if n % p == 0:
            return p
    return n


def _row(arr, width=None, colblk=0, tm=TM):
    width = arr.shape[1] if width is None else width
    return (arr, pl.BlockSpec((tm, width), lambda i, c=colblk: (i, c)))


def _full_spec(p):
    nd = p.ndim
    return pl.BlockSpec(p.shape, lambda i, nd=nd: (0,) * nd)


def _rowwise(name, fn, n_blocks, ctx_blk, rows, params, outs, tm=TM):
    nr, npar = len(rows), len(params)

    def body(*refs):
        t = (pl.program_id(0) >= ctx_blk).astype(F32)
        rv = [r[...].astype(F32) for r in refs[:nr]]
        pv = [p[...] for p in refs[nr:nr + npar]]
        res = fn(t, rv, pv)
        for o_ref, o in zip(refs[nr + npar:], res):
            o_ref[...] = o.astype(o_ref.dtype)

    return pl.pallas_call(
        body, name=name, grid=(n_blocks,),
        in_specs=[s for _, s in rows] + [_full_spec(p) for p in params],
        out_specs=[pl.BlockSpec((tm, w), lambda i: (i, 0)) for w, _ in outs],
        out_shape=[jax.ShapeDtypeStruct((n_blocks * tm, w), dt) for w, dt in outs],
        compiler_params=_cp(("parallel",)),
    )(*[a for a, _ in rows], *params)


def _rowwise_vjp(name, fn, n_blocks, ctx_blk, rows, params, douts, row_grads, tm=TM):
    nr, npar, nd = len(rows), len(params), len(douts)
    adds = [a for _, _, a in row_grads if a is not None]
    na = len(adds)

    def body(*refs):
        i = pl.program_id(0)
        t = (i >= ctx_blk).astype(F32)
        rv = [r[...].astype(F32) for r in refs[:nr]]
        pv = [p[...] for p in refs[nr:nr + npar]]
        dv = [r[...].astype(F32) for r in refs[nr + npar:nr + npar + nd]]
        av = [r[...].astype(F32) for r in refs[nr + npar + nd:nr + npar + nd + na]]
        o_refs = refs[nr + npar + nd + na:]
        _, vjp = jax.vjp(lambda r, p: tuple(fn(t, r, p)), rv, pv)
        d_rows, d_params = vjp(tuple(dv))
        ai = 0
        for o_ref, (ri, _, addend) in zip(o_refs, row_grads):
            g = d_rows[ri]
            if addend is not None:
                g = g + av[ai]
                ai += 1
            o_ref[...] = g.astype(o_ref.dtype)
        p_refs = o_refs[len(row_grads):]

        @pl.when(i == 0)
        def _():
            for p_ref in p_refs:
                p_ref[...] = jnp.zeros_like(p_ref)

        for p_ref, g in zip(p_refs, d_params):
            p_ref[...] += g

    widths = [rows[ri][1].block_shape[1] for ri, _, _ in row_grads]
    res = pl.pallas_call(
        body, name=name, grid=(n_blocks,),
        in_specs=[s for _, s in rows] + [_full_spec(p) for p in params] + [s for _, s in douts] + [s for _, s in adds],
        out_specs=[pl.BlockSpec((tm, w), lambda i: (i, 0)) for w in widths] + [_full_spec(p) for p in params],
        out_shape=[jax.ShapeDtypeStruct((n_blocks * tm, w), dt) for w, (_, dt, _) in zip(widths, row_grads)]
        + [jax.ShapeDtypeStruct(p.shape, F32) for p in params],
        compiler_params=_cp(("arbitrary",)),
    )(*[a for a, _ in rows], *params, *[a for a, _ in douts], *[a for a, _ in adds])
    return res[:len(row_grads)], res[len(row_grads):]


def _mm(name, a, b, mode, out_dtype):
    if mode == "nn":
        m, kk = a.shape
        n = b.shape[1]
    elif mode == "nt":
        m, kk = a.shape
        n = b.shape[0]
    else:
        kk, m = a.shape
        n = b.shape[1]
    if mode == "tn":
        tm = _pick(m, (1024, 1408, 512, 256, 128))
        tn = _pick(n, (512, 384, 256, 128))
        tk = kk
    else:
        tm = _pick(m, (1088, 1024, 768, 512, 384, 256, 128))
        tn = _pick(n, (1024, 512, 384, 256, 128) if mode == "nt" else (512, 384, 256, 128))
        tk = kk if kk <= 2816 else _pick(kk, (2816, 1792, 1024, 768, 512, 256, 128))
    nk = kk // tk
    in_place = out_dtype == F32
    if mode == "nn":
        specs = [pl.BlockSpec((tm, tk), lambda i, j, k: (i, k)), pl.BlockSpec((tk, tn), lambda i, j, k: (k, j))]
        dims = (((1,), (0,)), ((), ()))
    elif mode == "nt":
        specs = [pl.BlockSpec((tm, tk), lambda i, j, k: (i, k)), pl.BlockSpec((tn, tk), lambda i, j, k: (j, k))]
        dims = (((1,), (1,)), ((), ()))
    else:
        specs = [pl.BlockSpec((tk, tm), lambda i, j, k: (k, i)), pl.BlockSpec((tk, tn), lambda i, j, k: (k, j))]
        dims = (((0,), (0,)), ((), ()))

    def body(a_ref, b_ref, o_ref, *scratch):
        part = lax.dot_general(a_ref[...].astype(MXU_DTYPE), b_ref[...].astype(MXU_DTYPE), dims, preferred_element_type=F32)
        if nk == 1:
            o_ref[...] = part.astype(o_ref.dtype)
        else:
            k = pl.program_id(2)
            acc = o_ref if in_place else scratch[0]

            @pl.when(k == 0)
            def _():
                acc[...] = part

            @pl.when(k > 0)
            def _():
                acc[...] += part

            if not in_place:
                @pl.when(k == nk - 1)
                def _():
                    o_ref[...] = acc[...].astype(o_ref.dtype)

    return pl.pallas_call(
        body, name=name, grid=(m // tm, n // tn, nk), in_specs=specs,
        out_specs=pl.BlockSpec((tm, tn), lambda i, j, k: (i, j)),
        out_shape=jax.ShapeDtypeStruct((m, n), out_dtype),
        scratch_shapes=[] if nk == 1 or in_place else [pltpu.VMEM((tm, tn), F32)],
        compiler_params=_cp(("parallel", "parallel", "arbitrary")),
    )(a, b)


def _sel_mod(modp, t):
    return modp[0:8] * (1.0 - t) + modp[8:16] * t


def _fn_prenorm(t, rows, params, *, a, b):
    (x,), (g, modp) = rows, params
    m = _sel_mod(modp, t)
    return ((_rms(x) * g) * (1.0 + m[b:b + 1]) + m[a:a + 1],)


def _fn_resid_prenorm(t, rows, params, *, gi, a, b):
    (x, y), (g, mod_a, mod_b) = rows, params
    ma, mb = _sel_mod(mod_a, t), _sel_mod(mod_b, t)
    xn = x + ma[gi:gi + 1] * y
    return xn, (_rms(xn) * g) * (1.0 + mb[b:b + 1]) + mb[a:a + 1]


def _fn_resid(t, rows, params, *, gi):
    (x, y), (mod_a,) = rows, params
    return (x + _sel_mod(mod_a, t)[gi:gi + 1] * y,)


def _fn_swiglu(t, rows, params):
    (pf,) = rows
    return (jax.nn.silu(pf[:, :D_FF]) * pf[:, D_FF:],)


def _fn_mixpost(t, rows, params):
    (o, r, u, g), (gla_g, vn_g, sw, sb_t) = rows, params
    a =jnp.concatenate([_rms(o[:, h * GLA_DV:(h + 1) * GLA_DV]) for h in range(GLA_H)], axis=1) * gla_g * jax.nn.silu(r)
    uu, vv = jax.nn.gelu(u), jax.nn.gelu(g)
    mu = jnp.mean(vv, axis=-1, keepdims=True)
    var = jnp.mean(jnp.square(vv - mu), axis=-1, keepdims=True)
    vn = ((vv - mu) * lax.rsqrt(var + EPS)) * vn_g
    s = jnp.concatenate(
        [_dot(sw[gi * GMLP_L:(gi + 1) * GMLP_L, :], vn[:, gi * GMLP_C:(gi + 1) * GMLP_C]) + sb_t[:, gi:gi + 1]
         for gi in range(GMLP_G)], axis=1)
    return (jnp.concatenate([a, uu * s], axis=1),)


def _head_expand():
    r = lax.broadcasted_iota(jnp.int32, (SSD_H, SSD_INNER), 0)
    c = lax.broadcasted_iota(jnp.int32, (SSD_H, SSD_INNER), 1)
    return (c // SSD_P == r).astype(F32)


def _fn_ssd_finish(t, rows, params):
    (y2, xs, z), (d_skip, norm_g) = rows, params
    d_full = _dotx(jnp.broadcast_to(d_skip, (8, SSD_H)), _head_expand())[0:1]
    y = (y2 + d_full * xs) * jax.nn.silu(z)
    gw = SSD_INNER // SSD_G
    return (jnp.concatenate([_rms(y[:, gi * gw:(gi + 1) * gw]) for gi in range(SSD_G)], axis=1) * norm_g,)


def _fn_concat(t, rows, params, *, sums, pad=0):
    out, i = [], 0
    for n in sums:
        acc = rows[i]
        for j in range(1, n):
            acc = acc + rows[i + j]
        out.append(acc)
        i += n
    if pad:
        out.append(jnp.zeros((out[0].shape[0], pad), F32))
    return (jnp.concatenate(out, axis=1),)


def _tri(n, rev):
    r = lax.broadcasted_iota(jnp.int32, (n, n), 0)
    c = lax.broadcasted_iota(jnp.int32, (n, n), 1)
    return (r <= c) if rev else (r >= c)


def _gla_chunk(S, v, k, q, tail, gw, gb, *, rev):
    L = GLA_L
    msk = _tri(L, rev)
    tri = msk.astype(F32)
    lr = tail[:, GLA_LR:2 * GLA_LR] if rev else tail[:, 0:GLA_LR]
    la = jax.nn.log_sigmoid(_dot(lr, gw) + gb) / GLA_TAU
    b = _dotx(tri, la)
    b_last = b[0:1] if rev else b[L - 1:L]
    kd = k * jnp.exp(b_last - b)
    qd = (q * GLA_DK ** -0.5) * jnp.exp(b)
    ki = k * jnp.exp(-b)
    dec = jnp.exp(b_last)
    o_parts, s_parts = [], []
    for h in range(GLA_H):
        ks, vs = slice(h * GLA_DK, (h + 1) * GLA_DK), slice(h * GLA_DV, (h + 1) * GLA_DV)
        sh = S[:, ks]
        sc = jnp.where(msk, _dot_nt(qd[:, ks], ki[:, ks]), 0.0)
        o_parts.append(_dot_nt(qd[:, ks], sh) + _dot(sc, v[:, vs]))
        s_parts.append(dec[:, ks] * sh + _dot_tn(v[:, vs], kd[:, ks]))
    return jnp.concatenate(s_parts, axis=1), jnp.concatenate(o_parts, axis=1)


def _ssd_chunk(S, x, bm, cm, tail, dtb, alog, *, rev):
    L = SSD_L
    msk = _tri(L, rev)
    tri = msk.astype(F32)
    raw = tail[:, SSD_H:2 * SSD_H] if rev else tail[:, 0:SSD_H]
    dt = jax.nn.softplus(raw + dtb)
    dta = dt * (-jnp.exp(alog))
    acum = _dotx(tri, dta)
    a_last = acum[0:1] if rev else acum[L - 1:L]
    wst = dt * jnp.exp(a_last - acum)
    eac = jnp.exp(acum)
    tr = jnp.concatenate([acum, dt, wst, jnp.zeros((L, L - 3 * SSD_H), F32)], axis=1).T
    acum_t, dt_t, wst_t = tr[0:SSD_H], tr[SSD_H:2 * SSD_H], tr[2 * SSD_H:3 * SSD_H]
    decrow = jnp.exp(_dotx(jnp.broadcast_to(a_last, (8, SSD_H)), _head_expand())[0:1])
    lane = lax.broadcasted_iota(jnp.int32, (1, 2 * SSD_P), 1)
    m0 = (lane < SSD_P).astype(F32)
    m1 = 1.0 - m0
    pairs_per_group = SSD_H // SSD_G // 2
    y_parts, s_parts = [], []
    for g in range(SSD_G):
        ns = slice(g * SSD_N, (g + 1) * SSD_N)
        bg, cg = bm[:, ns], cm[:, ns]
        cb = _dot_nt(cg, bg)
        bgt = bg.T
        for jj in range(pairs_per_group):
            j = g * pairs_per_group + jj
            ls = slice(j * 2 * SSD_P, (j + 1) * 2 * SSD_P)
            xp, sp = x[:, ls], S[:, ls]
            xm = jnp.concatenate([xp * m0, xp * m1], axis=0)
            sm = jnp.concatenate([sp * m0, sp * m1], axis=0)
            lhs, bw = [], []
            for h in (2 * j, 2 * j + 1):
                seg = acum[:, h:h + 1] - acum_t[h:h + 1, :]
                lhs.append(cb * jnp.exp(jnp.where(msk, seg, -jnp.inf)) * dt_t[h:h + 1, :])
                bw.append(bgt * wst_t[h:h + 1, :])
            lhs += [cg * eac[:, h:h + 1] for h in (2 * j, 2 * j + 1)]
            y_parts.append(_dot(jnp.concatenate(lhs, axis=1), jnp.concatenate([xm, sm], axis=0)))
            s_parts.append(sp * decrow[:, ls] + _dot(jnp.concatenate(bw, axis=1), xm))
    return jnp.concatenate(s_parts, axis=1), jnp.concatenate(y_parts, axis=1)


def _multi_chunk(chunk_fn, L, subs, nr):
    def fn(S, *args, rev):
        rows, params = args[:nr], args[nr:]
        ys = [None] * subs
        for j in (range(subs - 1, -1, -1) if rev else range(subs)):
            S, ys[j] = chunk_fn(S, *[r[j * L:(j + 1) * L] for r in rows], *params, rev=rev)
        return S, jnp.concatenate(ys, axis=0)

    return fn


def _scan_order(n, nx, rev, backward):
    nc = n - nx

    def fwd(s):
        return (n - 1 - s) if rev else jnp.where(s < nc, s + nx, s - nc)

    return (lambda s: fwd(n - 1 - s)) if backward else fwd


def _scan_fwd(name, chunk_fn, L, n, nx, rev, rows, params, state_shape, out_w, addend=None):
    order = _scan_order(n, nx, rev, False)
    nr, npar = len(rows), len(params)
    adds = [] if addend is None else [addend]

    def body(*refs):
        s_scr = refs[-1]

        @pl.when(pl.program_id(0) == 0)
        def _():
            s_scr[...] = jnp.zeros_like(s_scr)

        s_in = s_scr[...]
        y_ref, st_ref = refs[nr + npar + len(adds)], refs[nr + npar + len(adds) + 1]
        st_ref[0] = s_in
        s_new, y = chunk_fn(s_in, *[r[...] for r in refs[:nr]], *[p[...] for p in refs[nr:nr + npar]], rev=rev)
        y_ref[...] = y + refs[nr + npar][...] if adds else y
        s_scr[...] = s_new

    return pl.pallas_call(
        body, name=name, grid=(n,),
        in_specs=[pl.BlockSpec((L, w), lambda s, c=c: (order(s), c)) for _, w, c in rows] + [_full_spec(p) for p in params]
        + [pl.BlockSpec((L, out_w), lambda s: (order(s), 0)) for _ in adds],
        out_specs=[pl.BlockSpec((L, out_w), lambda s: (order(s), 0)),
                   pl.BlockSpec((1,) + state_shape, lambda s: (order(s), 0, 0))],
        out_shape=[jax.ShapeDtypeStruct((n * L, out_w), F32), jax.ShapeDtypeStruct((n,) + state_shape, F32)],
        scratch_shapes=[pltpu.VMEM(state_shape, F32)],
        compiler_params=_cp(("arbitrary",)),
    )(*[a for a, _, _ in rows], *params, *adds)


def _scan_bwd(name, chunk_fn, L, n, nx, rev, rows, params, states, dy, state_shape, out_w, addends=None):
    order = _scan_order(n, nx, rev, True)
    nr, npar = len(rows), len(params)
    adds = [] if addends is None else list(addends)

    def body(*refs):
        i = pl.program_id(0)
        ds_scr = refs[-1]
        rv = [r[...] for r in refs[:nr]]
        pv = [p[...] for p in refs[nr:nr + npar]]
        st_ref, dy_ref = refs[nr + npar], refs[nr + npar + 1]
        a_refs = refs[nr + npar + 2:nr + npar + 2 + len(adds)]
        o_refs = refs[nr + npar + 2 + len(adds):-1]
        p_refs = o_refs[nr:]

        @pl.when(i == 0)
        def _():
            ds_scr[...] = jnp.zeros_like(ds_scr)
            for p_ref in p_refs:
                p_ref[...] = jnp.zeros_like(p_ref)

        _, vjp = jax.vjp(functools.partial(chunk_fn, rev=rev), st_ref[0], *rv, *pv)
        grads = vjp((ds_scr[...], dy_ref[...].astype(F32)))
        ds_scr[...] = grads[0]
        for j, (o_ref, g) in enumerate(zip(o_refs[:nr], grads[1:1 + nr])):
            o_ref[...] = g + a_refs[j][...] if adds else g
        for p_ref, g in zip(p_refs, grads[1 + nr:]):
            p_ref[...] += g

    row_specs = [pl.BlockSpec((L, w), lambda s: (order(s), 0)) for _, w, _ in rows]
    res = pl.pallas_call(
        body, name=name, grid=(n,),
        in_specs=[pl.BlockSpec((L, w), lambda s, c=c: (order(s), c)) for _, w, c in rows] + [_full_spec(p) for p in params]
        + [pl.BlockSpec((1,) + state_shape, lambda s: (order(s), 0, 0)), pl.BlockSpec((L, out_w), lambda s: (order(s), 0))]
        + row_specs[:len(adds)],
        out_specs=row_specs + [_full_spec(p) for p in params],
        out_shape=[jax.ShapeDtypeStruct((n * L, w), F32) for _, w, _ in rows] + [jax.ShapeDtypeStruct(p.shape, F32) for p in params],
        scratch_shapes=[pltpu.VMEM(state_shape, F32)],
        compiler_params=_cp(("arbitrary",)),
    )(*[a for a, _, _ in rows], *params, states, dy, *adds)
    return res[:nr], res[nr:]


CONV_W = 1024
CONV_COLBLK = (0, 1, 4)


def _conv_specs(nb, src_blk):
    halo = TM // 8
    return [pl.BlockSpec((TM, CONV_W), lambda j, i: (i, src_blk(j))),
            pl.BlockSpec((8, CONV_W), lambda j, i: (jnp.maximum(i * halo - 1, 0), src_blk(j))),
            pl.BlockSpec((8, CONV_W), lambda j, i: (jnp.minimum(i * halo + halo, nb * halo - 1), src_blk(j)))]


def _conv_ext(i, nb, cur, prev, nxt):
    has_prev = jnp.logical_and(i > 0, i < nb - 1)
    has_next = i < nb - 2
    return jnp.concatenate([jnp.where(has_prev, prev, 0.0), cur, jnp.where(has_next, nxt, 0.0)], axis=0)


def _conv_taps(ext, w, flip):
    acc = None
    for j in range(SSD_K):
        wj = w[SSD_K - 1 - j:SSD_K - j, :] if flip else w[j:j + 1, :]
        term = wj * ext[6 + j:6 + j + TM, :]
        acc = term if acc is None else acc + term
    return acc


def _conv(name, src, w8, b1, nb, *, permuted_src, act, flip, out_dtype):
    src_blk = (lambda j: jnp.where(j == 2, CONV_COLBLK[2], j)) if permuted_src else (lambda j: j)

    def body(cur, prev, nxt, w_ref, b_ref, o_ref):
        ext = _conv_ext(pl.program_id(1), nb, cur[...].astype(F32), prev[...].astype(F32), nxt[...].astype(F32))
        acc = _conv_taps(ext, w_ref[...], flip)
        if act:
            acc = jax.nn.silu(acc + b_ref[...])
        o_ref[...] = acc.astype(o_ref.dtype)

    return pl.pallas_call(
        body, name=name, grid=(3, nb),
        in_specs=_conv_specs(nb, src_blk) + [pl.BlockSpec((8, CONV_W), lambda j, i: (0, j)), pl.BlockSpec((1, CONV_W), lambda j, i: (0, j))],
        out_specs=pl.BlockSpec((TM, CONV_W), lambda j, i: (i, j)),
        out_shape=jax.ShapeDtypeStruct((nb * TM, 3 * CONV_W), out_dtype),
        compiler_params=_cp(("parallel", "parallel")),
    )(src, src, src, w8, b1)


def _conv_bwd_pre(name, p1, w8, b1, dxbc_parts, nb):
    src_blk = lambda j: jnp.where(j == 2, CONV_COLBLK[2], j)
    xs_parts, bc_parts = dxbc_parts
    n_x, n_bc = len(xs_parts), len(bc_parts)

    def body(*refs):
        cur, prev, nxt, w_ref, b_ref = refs[:5]
        d_refs = refs[5:5 + n_x + n_bc]
        da_ref, dw_ref, db_ref = refs[5 + n_x + n_bc:]
        j, i = pl.program_id(0), pl.program_id(1)
        ext = _conv_ext(i, nb, cur[...], prev[...], nxt[...])
        acc = _conv_taps(ext, w_ref[...], False) + b_ref[...]
        dx = d_refs[0][...]
        for r in d_refs[1:n_x]:
            dx = dx + r[...]
        dbc = jnp.concatenate([d_refs[n_x][...], d_refs[n_x + 1][...]], axis=1)
        dy = jnp.where(j == 2, dbc, dx)
        sg = jax.nn.sigmoid(acc)
        da = dy * (sg + acc * sg * (1.0 - sg))
        da_ref[...] = da

        @pl.when(i == 0)
        def _():
            dw_ref[...] = jnp.zeros_like(dw_ref)
            db_ref[...] = jnp.zeros_like(db_ref)

        rows = [jnp.sum(da * ext[6 + t:6 + t + TM, :], axis=0, keepdims=True) for t in range(SSD_K)]
        dw_ref[...] += jnp.concatenate(rows + [jnp.zeros((8 - SSD_K, CONV_W), F32)], axis=0)
        db_ref[...] += jnp.sum(da, axis=0, keepdims=True)

    x_specs = [pl.BlockSpec((TM, CONV_W), lambda j, i: (i, jnp.minimum(j, 1))) for _ in xs_parts]
    bc_specs = [pl.BlockSpec((TM, 512), lambda j, i: (i, 0)) for _ in bc_parts]
    return pl.pallas_call(
        body, name=name, grid=(3, nb),
        in_specs=_conv_specs(nb, src_blk) + [pl.BlockSpec((8, CONV_W), lambda j, i: (0, j)), pl.BlockSpec((1, CONV_W), lambda j, i: (0, j))]
        + x_specs + bc_specs,
        out_specs=[pl.BlockSpec((TM, CONV_W), lambda j, i: (i, j)), pl.BlockSpec((8, CONV_W), lambda j, i: (0, j)),
                   pl.BlockSpec((1, CONV_W), lambda j, i: (0, j))],
        out_shape=[jax.ShapeDtypeStruct((nb * TM, 3 * CONV_W), F32), jax.ShapeDtypeStruct((8, 3 * CONV_W), F32),
                   jax.ShapeDtypeStruct((1, 3 * CONV_W), F32)],
        compiler_params=_cp(("arbitrary", "arbitrary")),
    )(p1, p1, p1, w8, b1, *xs_parts, *bc_parts)


def _grid_rows(val, a, kb):
    return jnp.concatenate([val[:, t * D:(t + 1) * D] for t in range(kb)], axis=0)


def _perm(name, xc, a, nb):
    n = xc.shape[0]
    b = (n - CTX) // a
    kb = TM // a
    view = xc.reshape(n // b, b * D)

    def body(v_ref, c_ref, o_ref):
        i = pl.program_id(0)

        @pl.when(i < nb - 1)
        def _():
            o_ref[...] = _grid_rows(v_ref[...], a, kb)

        @pl.when(i == nb - 1)
        def _():
            o_ref[...] = c_ref[...]

    return pl.pallas_call(
        body, name=name, grid=(nb,),
        in_specs=[pl.BlockSpec((a, kb * D), lambda i: (0, jnp.minimum(i, nb - 2))), pl.BlockSpec((TM, D), lambda i: (nb - 1, 0))],
        out_specs=pl.BlockSpec((TM, D), lambda i: (i, 0)),
        out_shape=jax.ShapeDtypeStruct((n, D), xc.dtype),
        compiler_params=_cp(("parallel",)),
    )(view, xc)


def _loss_head(x, f, target, modp, g_final, nb, rows_r):
    t_tok = target.shape[0]
    kb = TM // rows_r
    tview = target.reshape(rows_r, (t_tok // rows_r) * D)

    def fn(x_, f_, tgt, modp_, g_, is_ctx):
        xn = x_ + _sel_mod(modp_, is_ctx)[5:6] * f_
        err = _rms(xn) * g_ - tgt
        return 0.5 * jnp.sum(jnp.mean(err * err, axis=-1)) * (1.0 - is_ctx)

    def body(x_ref, f_ref, t_ref, m_ref, g_ref, l_ref, dx_ref, df_ref, dm_ref, dg_ref):
        i = pl.program_id(0)
        is_ctx = (i == nb - 1).astype(F32)
        tgt = _grid_rows(t_ref[...], rows_r, kb)
        l, vjp = jax.vjp(lambda a_, b_, c_, d_: fn(a_, b_, tgt, c_, d_, is_ctx), x_ref[...], f_ref[...], m_ref[...], g_ref[...])
        dx, df, dm, dg = vjp(jnp.ones((), F32))

        @pl.when(i == 0)
        def _():
            l_ref[...] = jnp.zeros_like(l_ref)
            dm_ref[...] = jnp.zeros_like(dm_ref)
            dg_ref[...] = jnp.zeros_like(dg_ref)

        l_ref[...] += jnp.reshape(l, (1, 1))
        dx_ref[...] = dx
        df_ref[...] = df.astype(df_ref.dtype)
        dm_ref[...] += dm
        dg_ref[...] += dg

    rowspec = pl.BlockSpec((TM, D), lambda i: (i, 0))
    return pl.pallas_call(
        body, name="loss_head", grid=(nb,),
        in_specs=[rowspec, rowspec, pl.BlockSpec((rows_r, kb * D), lambda i: (0, jnp.minimum(i, nb - 2))),
                  _full_spec(modp), _full_spec(g_final)],
        out_specs=[pl.BlockSpec((1, 1), lambda i: (0, 0)), rowspec, rowspec, _full_spec(modp), _full_spec(g_final)],
        out_shape=[jax.ShapeDtypeStruct((1, 1), F32), jax.ShapeDtypeStruct(x.shape, F32), jax.ShapeDtypeStruct(x.shape, MXU_DTYPE),
                   jax.ShapeDtypeStruct(modp.shape, F32), jax.ShapeDtypeStruct(g_final.shape, F32)],
        compiler_params=_cp(("arbitrary",)),
    )(x, f, tview, modp, g_final)


def _repack(name, shards, segs, wp):
    nd, kk, ws = shards.shape
    tr = 128
    used = sum(e - s for s, e in segs)

    def body(a_ref, o_ref):
        full = jnp.concatenate([a_ref[d].astype(F32) for d in range(nd)], axis=1)
        parts = [full[:, s:e] for s, e in segs]
        if wp > used:
            parts.append(jnp.zeros((tr, wp - used), F32))
        o_ref[...] = jnp.concatenate(parts, axis=1).astype(o_ref.dtype)

    return pl.pallas_call(
        body, name=name, grid=(kk // tr,),
        in_specs=[pl.BlockSpec((nd, tr, ws), lambda i: (0, i, 0))],
        out_specs=pl.BlockSpec((tr, wp), lambda i: (i, 0)),
        out_shape=jax.ShapeDtypeStruct((kk, wp), MXU_DTYPE),
        compiler_params=_cp(("parallel",)),
    )(shards)


def _unpack(name, dw, segs, ws, out_dtype):
    kk, wp = dw.shape
    tr = 128
    order = sorted(range(len(segs)), key=lambda i: segs[i][0])
    offs, o = [], 0
    for s, e in segs:
        offs.append(o)
        o += e - s

    def body(a_ref, o_ref):
        a = a_ref[...].astype(F32)
        full = jnp.concatenate([a[:, offs[i]:offs[i] + segs[i][1] - segs[i][0]] for i in order], axis=1)
        for d in range(NDEV):
            o_ref[d] = full[:, d * ws:(d + 1) * ws].astype(o_ref.dtype)

    return pl.pallas_call(
        body, name=name, grid=(kk // tr,),
        in_specs=[pl.BlockSpec((tr, wp), lambda i: (i, 0))],
        out_specs=pl.BlockSpec((NDEV, tr, ws), lambda i: (0, i, 0)),
        out_shape=jax.ShapeDtypeStruct((NDEV, kk, ws), out_dtype),
        compiler_params=_cp(("parallel",)),
    )(dw)


def _adam_math(w, g, m, v):
    m = ADAM_B1 * m + (1.0 - ADAM_B1) * g
    v = ADAM_B2 * v + (1.0 - ADAM_B2) * jnp.square(g)
    m_hat = m / (1.0 - ADAM_B1 ** ADAM_STEP)
    v_hat = v / (1.0 - ADAM_B2 ** ADAM_STEP)
    delta = -ADAM_LR * (m_hat / (jnp.sqrt(v_hat) + ADAM_EPS) + ADAM_WD * w)
    return delta, m, v


def _adam(name, w, parts, m, v, after):
    r, c = w.shape
    nsec, npart = len(parts), parts[0].shape[0]
    rs = r // nsec
    tr = _pick(rs, (256, 128, 64, 32, 16, 8)) if rs * c * 4 > (1 << 20) else rs
    tiles = rs // tr

    def body(w_ref, *refs):
        m_ref, v_ref, _, g_ref, d_ref, nm_ref, nv_ref = refs[nsec:]
        sec = pl.program_id(0) // tiles
        g = None
        for a, p_ref in enumerate(refs[:nsec]):
            ga = p_ref[0].astype(F32)
            for s in range(1, npart):
                ga = ga + p_ref[s].astype(F32)
            g = ga if g is None else jnp.where(sec == a, ga, g)
        delta, nm, nv = _adam_math(w_ref[...], g, m_ref[...], v_ref[...])
        g_ref[...], d_ref[...], nm_ref[...], nv_ref[...] = g, delta, nm, nv

    spec = pl.BlockSpec((tr, c), lambda i: (i, 0))
    part_specs = [pl.BlockSpec((npart, tr, c), lambda i, a=a: (0, jnp.clip(i - a * tiles, 0, tiles - 1), 0)) for a in range(nsec)]
    return pl.pallas_call(
        body, name=name, grid=(r // tr,),
        in_specs=[spec] + part_specs + [spec, spec, ANY],
        out_specs=[spec] * 4, out_shape=[jax.ShapeDtypeStruct((r, c), F32)] * 4,
        compiler_params=_cp(("parallel",)),
    )(w, *parts, m, v, after)


def _mod_fwd(c_all, mod_w):
    nl, _, ws = mod_w.shape

    def body(c_ref, w_ref, o_ref):
        o_ref[0] = _dot(jax.nn.silu(c_ref[...]), w_ref[0])

    return pl.pallas_call(
        body, name="mod_fwd", grid=(nl,),
        in_specs=[_full_spec(c_all), pl.BlockSpec((1, D, ws), lambda i: (i, 0, 0))],
        out_specs=pl.BlockSpec((1, 16, ws), lambda i: (i, 0, 0)),
        out_shape=jax.ShapeDtypeStruct((nl, 16, ws), F32),
        compiler_params=_cp(("parallel",)),
    )(c_all, mod_w)


def _mod_bwd(c_all, mod_w, dm):
    nl, _, ws = mod_w.shape

    def body(c_ref, w_ref, d_ref, dw_ref, dc_ref):
        dw_ref[0] = _dot_tn(jax.nn.silu(c_ref[...]), d_ref[0])
        dc_ref[0] = _dot_nt(d_ref[0], w_ref[0])

    return pl.pallas_call(
        body, name="mod_bwd", grid=(nl,),
        in_specs=[_full_spec(c_all), pl.BlockSpec((1, D, ws), lambda i: (i, 0, 0)), pl.BlockSpec((1, 16, ws), lambda i: (i, 0, 0))],
        out_specs=[pl.BlockSpec((1, D, ws), lambda i: (i, 0, 0)), pl.BlockSpec((1, 16, D), lambda i: (i, 0, 0))],
        out_shape=[jax.ShapeDtypeStruct((nl, D, ws), F32), jax.ShapeDtypeStruct((nl, 16, D), F32)],
        compiler_params=_cp(("parallel",)),
    )(c_all, mod_w, dm)


def _sum_parts(name, parts):
    npart, r, c = parts.shape

    def body(p_ref, o_ref):
        g = p_ref[0].astype(F32)
        for s in range(1, npart):
            g = g + p_ref[s].astype(F32)
        o_ref[...] = g

    return pl.pallas_call(body, name=name, out_shape=jax.ShapeDtypeStruct((r, c), F32), compiler_params=_cp())(parts)


MESH = pl.DeviceIdType.MESH
ANY = pl.BlockSpec(memory_space=pl.ANY)
N_PEERS = NDEV - 1


def _mesh_pos():
    return lax.axis_index("x"), lax.axis_index("y"), lax.axis_index("c")


def _slot(px, py, pc):
    return 4 * px + 2 * py + pc


def _two_level_gather(x_refs, o_refs, send_sems, recv_sems, local_sems):
    x, y, c = _mesh_pos()
    me, sibling = (x, y, c), (x, y, 1 - c)
    chips = [(1 - x, y), (x, 1 - y), (1 - x, 1 - y)]
    n = len(x_refs)

    def copy(a, k, block, to, src=None):
        dst = o_refs[a].at[_slot(*block)]
        return pltpu.make_async_remote_copy(src_ref=dst if src is None else src, dst_ref=dst, send_sem=send_sems.at[a, k],
                                            recv_sem=recv_sems.at[a, k], device_id=to, device_id_type=MESH)

    mine = [pltpu.make_async_copy(x_refs[a], o_refs[a].at[_slot(*me)], local_sems.at[a]) for a in range(n)]
    for cp in mine:
        cp.start()
    first = []
    for a in range(n):
        first.append(copy(a, 0, me, sibling, src=x_refs[a]))
        first += [copy(a, 1 + j, me, (*chip, c), src=x_refs[a]) for j, chip in enumerate(chips)]
    for cp in first:
        cp.start()
    passed = []
    for j, chip in enumerate(chips):
        for a in range(n):
            copy(a, 1 + j, (*chip, c), me).wait_recv()
            fwd = copy(a, 4 + j, (*chip, c), sibling)
            fwd.start()
            passed.append(fwd)
    for a in range(n):
        copy(a, 0, sibling, me).wait_recv()
        for j, chip in enumerate(chips):
            copy(a, 4 + j, (*chip, 1 - c), me).wait_recv()
    for cp in first + passed:
        cp.wait_send()
    for cp in mine:
        cp.wait()


def _ag_small(name, x):
    r, c = x.shape

    def body(x_ref, o_ref, send_sems, recv_sems, local_sems):
        _two_level_gather([x_ref], [o_ref], send_sems, recv_sems, local_sems)

    return pl.pallas_call(
        body, name=name, out_shape=jax.ShapeDtypeStruct((NDEV, r, c), x.dtype),
        in_specs=[pl.BlockSpec(memory_space=pltpu.VMEM)], out_specs=pl.BlockSpec(memory_space=pltpu.VMEM),
        scratch_shapes=[pltpu.SemaphoreType.DMA((1, N_PEERS)), pltpu.SemaphoreType.DMA((1, N_PEERS)), pltpu.SemaphoreType.DMA((1,))],
        compiler_params=pltpu.CompilerParams(vmem_limit_bytes=VMEM_LIMIT),
    )(x)


def _ag_big(name, shards):
    n = len(shards)

    def body(*refs):
        _two_level_gather(refs[:n], refs[n:2 * n], *refs[2 * n:])

    return pl.pallas_call(
        body, name=name, out_shape=[jax.ShapeDtypeStruct((NDEV,) + s.shape, s.dtype) for s in shards],
        in_specs=[ANY] * n, out_specs=[ANY] * n,
        scratch_shapes=[pltpu.SemaphoreType.DMA((n, N_PEERS)), pltpu.SemaphoreType.DMA((n, N_PEERS)), pltpu.SemaphoreType.DMA((n,))],
    )(*shards)


HBM = pl.BlockSpec(memory_space=pltpu.HBM)
SEM = pl.BlockSpec(memory_space=pltpu.SEMAPHORE)
EFFECT = pltpu.SideEffectType.DATAFLOW_SIDE_EFFECTING


def _peers(x, y, c):
    return [(k - 1, ((1 - x) if k & 4 else x, (1 - y) if k & 2 else y, (1 - c) if k & 1 else c)) for k in range(1, NDEV)]


def _xchg_copy(src_refs, land_refs, send_sems, recv_sems, a, k, peer, me, scatter):
    src = src_refs[a].at[_slot(*peer)] if scatter else src_refs[a]
    return pltpu.make_async_remote_copy(src_ref=src, dst_ref=land_refs[a].at[me], send_sem=send_sems.at[a * N_PEERS + k],
                                        recv_sem=recv_sems.at[a * N_PEERS + k], device_id=peer, device_id_type=MESH)


def _xchg_start(name, srcs, lands, deps, scatter):
    n, nd = len(srcs), len(deps)

    def body(*refs):
        src_refs, land_refs = refs[:n], refs[n:2 * n]
        send_sems, recv_sems, token = refs[2 * n + nd], refs[2 * n + nd + 1], refs[-1]
        x, y, c = _mesh_pos()
        me = _slot(x, y, c)
        for k, peer in _peers(x, y, c):
            for a in range(n):
                _xchg_copy(src_refs, land_refs, send_sems, recv_sems, a, k, peer, me, scatter).start()
        token[...] = jnp.zeros_like(token)

    res = pl.pallas_call(
        body, name=name,
        out_shape=(pltpu.SemaphoreType.DMA((n * N_PEERS,)), pltpu.SemaphoreType.DMA((n * N_PEERS,)),
                   *[pltpu.HBM(s.shape, s.dtype) for s in srcs], *[pltpu.HBM(s.shape, s.dtype) for s in lands],
                   jax.ShapeDtypeStruct((8, 128), F32)),
        in_specs=[HBM] * (2 * n) + [ANY] * nd,
        out_specs=(SEM, SEM, *([HBM] * (2 * n)), pl.BlockSpec(memory_space=pltpu.VMEM)),
        input_output_aliases={i: 2 + i for i in range(2 * n)},
        compiler_params=pltpu.CompilerParams(has_side_effects=EFFECT),
    )(*[pltpu.with_memory_space_constraint(s, pltpu.HBM) for s in srcs],
      *[pltpu.with_memory_space_constraint(s, pltpu.HBM) for s in lands], *deps)
    return res[0], res[1], res[2:2 + n], res[2 + n:2 + 2 * n], res[-1]


def _xchg_wait(name, send_sems, recv_sems, srcs, lands, after, scatter):
    n = len(srcs)

    def body(*refs):
        src_refs, land_refs = refs[:n], refs[n:2 * n]
        s_sems, r_sems = refs[2 * n], refs[2 * n + 1]
        x, y, c = _mesh_pos()
        me = _slot(x, y, c)
        for k, peer in _peers(x, y, c):
            for a in range(n):
                cp = _xchg_copy(src_refs, land_refs, s_sems, r_sems, a, k, peer, me, scatter)
                cp.wait_send()
                cp.wait_recv()

    res = pl.pallas_call(
        body, name=name,
        out_shape=[pltpu.HBM(s.shape, s.dtype) for s in srcs] + [pltpu.HBM(s.shape, s.dtype) for s in lands],
        in_specs=[HBM] * (2 * n) + [SEM, SEM, ANY], out_specs=[HBM] * (2 * n),
        input_output_aliases={i: i for i in range(2 * n)},
        compiler_params=pltpu.CompilerParams(has_side_effects=EFFECT),
    )(*srcs, *lands, send_sems, recv_sems, after)
    return res[n:]


def _landing(name, src, me, scatter):
    r, c = src.shape[-2:]
    tr = r if r * c * src.dtype.itemsize <= (2 << 20) else _pick(r, (256, 128, 64, 32, 16))

    def body(me_ref, s_ref, o_ref):
        o_ref[...] = s_ref[...].reshape(o_ref.shape)

    src_spec = pl.BlockSpec((1, tr, c), lambda i, me_ref: (me_ref[0], i, 0)) if scatter else pl.BlockSpec((tr, c), lambda i, me_ref: (i, 0))
    return pl.pallas_call(
        body, name=name, out_shape=jax.ShapeDtypeStruct((NDEV, r, c), src.dtype),
        grid_spec=pltpu.PrefetchScalarGridSpec(
            num_scalar_prefetch=1, grid=(r // tr,), in_specs=[src_spec],
            out_specs=pl.BlockSpec((1, tr, c), lambda i, me_ref: (me_ref[0], i, 0))),
        compiler_params=_cp(("parallel",)),
    )(jnp.reshape(me, (1,)).astype(jnp.int32), src)


STAGES = ("l0_mixer", "l0_ffn", "l1_mixer", "l1_ffn")
STAGE_LAYOUT = {"l0_mixer": (AB_SEGS, AB_P), "l1_mixer": (SSD_SEGS, SSD_P_W)}


class _Exchange:
    def __init__(self, shards, me):
        self.shards, self.me = shards, me
        self.pending, self.pending_grads, self.recv = None, None, {}

    def _layout(self, stage):
        ws = self.shards[stage][0].shape[-1]
        return STAGE_LAYOUT.get(stage, (((0, NDEV * ws),), NDEV * ws)) + (ws,)

    def _start_gather(self, stage, deps):
        srcs = list(self.shards[stage])
        lands = [_landing("own_%s_%d" % (stage, a), s, self.me, False) for a, s in enumerate(srcs)]
        return _xchg_start("gather_start_" + stage, srcs, lands, deps, False)

    def get(self, stage, dep, thread):
        i = STAGES.index(stage)
        if i == 0:
            g_in, g_out = _ag_big("gather_" + stage, list(self.shards[stage]))
            deps = [g_out, dep]
        else:
            ss, rs, srcs, lands, _ = self.pending
            g_in, g_out = _xchg_wait("gather_wait_" + stage, ss, rs, srcs, lands, dep, False)
            self.pending, deps = None, [g_out]
        if i + 1 < len(STAGES):
            self.pending = self._start_gather(STAGES[i + 1], deps)
            thread = thread + self.pending[4][0, 0]
        segs, wp, _ = self._layout(stage)
        return _repack("repack_" + stage, g_in, segs, wp), g_out.reshape(-1, D), thread

    def put(self, stage, d_in, d_out, thread):
        segs, _, ws = self._layout(stage)
        parts = [_unpack("unpack_" + stage, d_in, segs, ws, MXU_DTYPE), d_out.reshape(NDEV, -1, D)]
        deps = [parts[0]]
        if self.pending_grads is not None:
            deps = [self.finish(parts[0])[0]]
        self.staged = (stage, parts)
        return thread if stage == STAGES[0] else thread + self.start_last(deps)[0, 0]

    def start_last(self, deps):
        stage, parts = self.staged
        lands = [_landing("own_grad_%s_%d" % (stage, a), p, self.me, True) for a, p in enumerate(parts)]
        self.pending_grads = (stage,) + _xchg_start("scatter_start_" + stage, parts, lands, deps, True)
        return self.pending_grads[5]

    def finish(self, after):
        stage, ss, rs, srcs, lands, _ = self.pending_grads
        self.recv[stage] = _xchg_wait("scatter_wait_" + stage, ss, rs, srcs, lands, after, True)
        self.pending_grads = None
        return self.recv[stage]


def _ffn_fwd(tag, h, w_in, w_out, nb, cb):
    pf = _mm(tag + "_ffn_in", h, w_in, "nn", MXU_DTYPE)
    (act,) = _rowwise(tag + "_swiglu", _fn_swiglu, nb, cb, [_row(pf)], [], [(D_FF, MXU_DTYPE)])
    return pf, act, _mm(tag + "_ffn_out", act, w_out, "nn", F32)


def _ffn_bwd(tag, h, pf, act, df, w_in, w_out, nb, cb):
    dw_out = _mm(tag + "_ffn_out_dw", act, df, "tn", MXU_DTYPE)
    dact = _mm(tag + "_ffn_out_dx", df, w_out, "nt", MXU_DTYPE)
    (dpf,), _ = _rowwise_vjp(tag + "_swiglu_bwd", _fn_swiglu, nb, cb, [_row(pf)], [], [_row(dact)], [(0, MXU_DTYPE, None)])
    dw_in = _mm(tag + "_ffn_in_dw", h, dpf, "tn", MXU_DTYPE)
    dh = _mm(tag + "_ffn_in_dx", dpf, w_in, "nt", MXU_DTYPE)
    return dw_out, dw_in, dh


def _local_step(x, ctx, target, mod, P, comm):
    T = x.shape[0]
    N = T + CTX
    nb, cb = N // TM, N // TM - 1
    R = T // GRID_W
    mod0, mod1 = mod[0], mod[1]
    ng = P["norm_g"]
    g00, g01, g10, g11 = ng[0, 0][None], ng[0, 1][None], ng[1, 0][None], ng[1, 1][None]
    pre = functools.partial(_fn_prenorm, a=0, b=1)
    rpre = functools.partial(_fn_resid_prenorm, gi=2, a=3, b=4)
    res5 = functools.partial(_fn_resid, gi=5)
    dirs = (("f", False), ("b", True))

    xc0 = jnp.concatenate([x, ctx], axis=0)
    w_ab_in, w_ab_out, g00 = comm.get("l0_mixer", mod, g00)
    (h0,) = _rowwise("l0_prenorm", pre, nb, cb, [_row(xc0)], [g00, mod0], [(D, MXU_DTYPE)])
    p0 = _mm("l0_in", h0, w_ab_in, "nn", F32)
    gla_rows = [(p0, 512, 0), (p0, 256, 8), (p0, 256, 9), (p0, 128, 20)]
    gla_blk = _multi_chunk(_gla_chunk, GLA_L, TM // GLA_L, len(gla_rows))
    gla_par = {d: [P["ab_gate_w"][int(r)], P["ab_gate_b"][int(r)][None]] for d, r in dirs}
    gla_state = (GLA_DV, GLA_H * GLA_DK)
    o, st0 = None, {}
    for d, rev in dirs:
        o, st0[d] = _scan_fwd("gla_fwd_" + d, gla_blk, TM, nb, cb, rev, gla_rows, gla_par[d], gla_state, GLA_H * GLA_DV, o)
    n128, cb128 = N // GMLP_L, T // GMLP_L
    mix_rows = [_row(o, tm=GMLP_L)] + [_row(p0, 512, j, tm=GMLP_L) for j in (1, 2, 3)]
    mix_par = [P["ab_gla_norm_g"], P["ab_vnorm_g"], P["ab_spatial_w"].reshape(GMLP_G * GMLP_L, GMLP_L), P["ab_spatial_b"].T]
    (cat0,) = _rowwise("l0_mix", _fn_mixpost, n128, cb128, mix_rows, mix_par, [(D, MXU_DTYPE)], tm=GMLP_L)
    y0 = _mm("l0_out", cat0, w_ab_out, "nn", F32)
    w_fi0, w_fo0, g01 = comm.get("l0_ffn", y0, g01)
    x1, h1 = _rowwise("l0_ffn_prenorm", rpre, nb, cb, [_row(xc0), _row(y0)], [g01, mod0, mod0], [(D, F32), (D, MXU_DTYPE)])
    pf0, act0, f0 = _ffn_fwd("l0", h1, w_fi0, w_fo0, nb, cb)
    (x2,) = _rowwise("l0_resid", res5, nb, cb, [_row(x1), _row(f0)], [mod0], [(D, F32)])
    x2p = _perm("to_col_major", x2, R, nb)

    w_ssd_in, w_ssd_out, g10 = comm.get("l1_mixer", x2p, g10)
    (h2,) = _rowwise("l1_prenorm", pre, nb, cb, [_row(x2p)], [g10, mod1], [(D, MXU_DTYPE)])
    p1 = _mm("l1_in", h2, w_ssd_in, "nn", F32)
    conv_w8 = jnp.concatenate([P["ssd_conv_w"], jnp.zeros((8 - SSD_K, 3 * CONV_W), F32)], axis=0)
    xbc = _conv("l1_conv", p1, conv_w8, P["ssd_conv_b"], nb, permuted_src=True, act=True, flip=False, out_dtype=F32)
    ssd_rows = [(xbc, SSD_INNER, 0), (xbc, 512, 4), (xbc, 512, 5), (p1, 128, 40)]
    ssd_blk = _multi_chunk(_ssd_chunk, SSD_L, TM // SSD_L, len(ssd_rows))
    ssd_par = {d: [P["ssd_dt_bias"][int(r)][None], P["ssd_a_log"][int(r)][None]] for d, r in dirs}
    ssd_state = (SSD_N, SSD_INNER)
    ys, st1 = None, {}
    for d, rev in dirs:
        ys, st1[d] = _scan_fwd("ssd_fwd_" + d, ssd_blk, TM, nb, cb, rev, ssd_rows, ssd_par[d], ssd_state, SSD_INNER, ys)
    fin_rows = [_row(ys), _row(xbc, SSD_INNER, 0), _row(p1, SSD_INNER, 1)]
    fin_par = [P["ssd_d"], P["ssd_norm_g"]]
    (yn,) = _rowwise("l1_finish", _fn_ssd_finish, nb, cb, fin_rows, fin_par, [(SSD_INNER, MXU_DTYPE)])
    y1 = _mm("l1_out", yn, w_ssd_out, "nn", F32)
    w_fi1, w_fo1, g11 = comm.get("l1_ffn", y1, g11)
    x3, h3 = _rowwise("l1_ffn_prenorm", rpre, nb, cb, [_row(x2p), _row(y1)], [g11, mod1, mod1], [(D, F32), (D, MXU_DTYPE)])
    pf1, act1, f1 = _ffn_fwd("l1", h3, w_fi1, w_fo1, nb, cb)
    loss, dx3, df1, dm1_j, d_final_g = _loss_head(x3, f1, target, mod1, P["final_norm_g"], nb, R)

    dP = {"final_norm_g": d_final_g}
    dwo1, dwi1, dh3 = _ffn_bwd("l1", h3, pf1, act1, df1, w_fi1, w_fo1, nb, cb)
    g11 = comm.put("l1_ffn", dwi1, dwo1, g11)
    (dx2p_a, dy1), (dg11, dm1_a, dm1_b) = _rowwise_vjp(
        "l1_ffn_prenorm_bwd", rpre, nb, cb, [_row(x2p), _row(y1)], [g11, mod1, mod1], [_row(dx3), _row(dh3)],
        [(0, F32, None), (1, MXU_DTYPE, None)])
    d_ssd_out = _mm("l1_out_dw", yn, dy1, "tn", MXU_DTYPE)
    dyn = _mm("l1_out_dx", dy1, w_ssd_out, "nt", MXU_DTYPE)
    (dys, dxs, dz), (dP["ssd_d"], dP["ssd_norm_g"]) = _rowwise_vjp(
        "l1_finish_bwd", _fn_ssd_finish, nb, cb, fin_rows, fin_par, [_row(dyn)],
        [(0, F32, None), (1, F32, None), (2, MXU_DTYPE, None)])
    dssd, ddtb, dalog = None, [], []
    for d, rev in dirs:
        dssd, (ddtb_, dalog_) = _scan_bwd("ssd_bwd_" + d, ssd_blk, TM, nb, cb, rev, ssd_rows, ssd_par[d], st1[d], dys, ssd_state,
                                          SSD_INNER, dssd)
        ddtb.append(ddtb_); dalog.append(dalog_)
    dx_s, db_s, dc_s, dtl = dssd
    dP["ssd_dt_bias"] = jnp.concatenate(ddtb, axis=0)
    dP["ssd_a_log"] = jnp.concatenate(dalog, axis=0)
    dacc, dcw8, dP["ssd_conv_b"] = _conv_bwd_pre("l1_conv_bwd", p1, conv_w8, P["ssd_conv_b"], ([dx_s, dxs], [db_s, dc_s]), nb)
    dP["ssd_conv_w"] = dcw8[:SSD_K]
    dpc = _conv("l1_conv_dx", dacc, conv_w8, jnp.zeros((1, 3 * CONV_W), F32), nb, permuted_src=False, act=False, flip=True,
                out_dtype=MXU_DTYPE)
    cat1 = functools.partial(_fn_concat, sums=(1, 1, 1, 1), pad=SSD_P_W - 5248)
    (dp1,) = _rowwise("l1_dp", cat1, nb, cb, [_row(dpc, SSD_INNER, 0), _row(dz), _row(dpc, 1024, 2), _row(dtl)],
                      [], [(SSD_P_W, MXU_DTYPE)])
    g10 = comm.put("l1_mixer", _mm("l1_in_dw", h2, dp1, "tn", F32), d_ssd_out, g10)
    dh2 = _mm("l1_in_dx", dp1, w_ssd_in, "nt", MXU_DTYPE)
    (dx2p,), (dg10, dm1_f) = _rowwise_vjp("l1_prenorm_bwd", pre, nb, cb, [_row(x2p)], [g10, mod1], [_row(dh2)],
                                          [(0, F32, _row(dx2p_a))])
    dx2 = _perm("to_row_major", dx2p, GRID_W, nb)

    (dx1_a, df0), (dm0_e,) = _rowwise_vjp("l0_resid_bwd", res5, nb, cb, [_row(x1), _row(f0)], [mod0], [_row(dx2)],
                                          [(0, F32, None), (1, MXU_DTYPE, None)])
    dwo0, dwi0, dh1 = _ffn_bwd("l0", h1, pf0, act0, df0, w_fi0, w_fo0, nb, cb)
    g01 = comm.put("l0_ffn", dwi0, dwo0, g01)
    (dxc0_a, dy0), (dg01, dm0_a, dm0_b) = _rowwise_vjp(
        "l0_ffn_prenorm_bwd", rpre, nb, cb, [_row(xc0), _row(y0)], [g01, mod0, mod0], [_row(dx1_a), _row(dh1)],
        [(0, F32, None), (1, MXU_DTYPE, None)])
    d_ab_out = _mm("l0_out_dw", cat0, dy0, "tn", MXU_DTYPE)
    dcat0 = _mm("l0_out_dx", dy0, w_ab_out, "nt", MXU_DTYPE)
    (do, dr, du, dgm), (dP["ab_gla_norm_g"], dP["ab_vnorm_g"], dsw, dsb_t) = _rowwise_vjp(
        "l0_mix_bwd", _fn_mixpost, n128, cb128, mix_rows, mix_par, [_row(dcat0, tm=GMLP_L)],
        [(0, F32, None), (1, MXU_DTYPE, None), (2, MXU_DTYPE, None), (3, MXU_DTYPE, None)], tm=GMLP_L)
    dP["ab_spatial_w"] = dsw.reshape(GMLP_G, GMLP_L, GMLP_L)
    dP["ab_spatial_b"] = dsb_t.T
    gl, dgw, dgb = None, [], []
    for d, rev in dirs:
        gl, (dgw_, dgb_) = _scan_bwd("gla_bwd_" + d, gla_blk, TM, nb, cb, rev, gla_rows, gla_par[d], st0[d], do,
                                     gla_state, GLA_H * GLA_DV, gl)
        dgw.append(dgw_[None]); dgb.append(dgb_)
    dP["ab_gate_w"] = jnp.concatenate(dgw, axis=0)
    dP["ab_gate_b"] = jnp.concatenate(dgb, axis=0)
    cat0f = functools.partial(_fn_concat, sums=(1,) * 7)
    (dp0,) = _rowwise("l0_dp", cat0f, nb, cb, [_row(gl[0]), _row(dr), _row(du), _row(dgm), _row(gl[1]), _row(gl[2]), _row(gl[3])],
                      [], [(AB_P, MXU_DTYPE)])
    g00 = comm.put("l0_mixer", _mm("l0_in_dw", h0, dp0, "tn", F32), d_ab_out, g00)
    dh0 = _mm("l0_in_dx", dp0, w_ab_in, "nt", MXU_DTYPE)
    (dxc0,), (dg00, dm0_s) = _rowwise_vjp("l0_prenorm_bwd", pre, nb, cb, [_row(xc0)], [g00, mod0], [_row(dh0)],
                                          [(0, F32, _row(dxc0_a))])
    dP["norm_g"] = jnp.concatenate([dg00, dg01, dg10, dg11], axis=0).reshape(2, 2, D)
    dmod = jnp.stack([dm0_s + dm0_a + dm0_b + dm0_e, dm1_f + dm1_a + dm1_b + dm1_j])
    return loss, dxc0[:T], dmod, dP


WEIGHTS = ("c_ctx", "mod_w", "mod_b", "norm_g", "ffn_w_in", "ffn_w_out", "ab_w_in", "ab_gate_w", "ab_gate_b", "ab_gla_norm_g",
           "ab_vnorm_g", "ab_spatial_w", "ab_spatial_b", "ab_w_out", "ssd_w_in", "ssd_conv_w", "ssd_conv_b", "ssd_dt_bias",
           "ssd_a_log", "ssd_d", "ssd_norm_g", "ssd_w_out", "final_norm_g")
SMALL_SHARDED = ("norm_g", "ab_gate_w", "ab_gate_b", "ssd_conv_w", "ssd_conv_b", "ssd_norm_g")
SMALL = ("c_ctx", "mod_b", "norm_g", "ab_gate_w", "ab_gate_b", "ab_gla_norm_g", "ab_vnorm_g", "ab_spatial_w", "ab_spatial_b",
         "ssd_conv_w", "ssd_conv_b", "ssd_dt_bias", "ssd_a_log", "ssd_d", "ssd_norm_g", "final_norm_g")
LANES = 1024


def _pack(arrs, rows_multiple=8):
    flat = jnp.concatenate([a.reshape(-1).astype(F32) for a in arrs])
    rows = -(-flat.shape[0] // LANES)
    rows = -(-rows // rows_multiple) * rows_multiple
    return jnp.pad(flat, (0, rows * LANES - flat.shape[0])).reshape(rows, LANES)


def _unpack_flat(buf, shapes):
    lead = buf.shape[:-2]
    flat = buf.reshape(lead + (-1,))
    out, o = [], 0
    for s in shapes:
        n = math.prod(s)
        out.append(flat[..., o:o + n].reshape(lead + tuple(s)))
        o += n
    return out


def _unshard(g):
    g = jnp.moveaxis(g, 0, -2)
    return g.reshape(g.shape[:-2] + (g.shape[-2] * g.shape[-1],))


def _my_shard(full, me, ws):
    return lax.dynamic_slice_in_dim(full, me * ws, ws, axis=full.ndim - 1)


def _silu_vjp(cvec, dsc):
    def body(c_ref, d_ref, o_ref):
        _, vjp = jax.vjp(jax.nn.silu, c_ref[...])
        o_ref[...] = vjp(d_ref[...])[0]

    return pl.pallas_call(body, name="c_ctx_bwd", out_shape=jax.ShapeDtypeStruct(cvec.shape, F32), compiler_params=_cp())(cvec, dsc)


def kernel(x, c, ctx, c_ctx, mod_w, mod_b, norm_g, ffn_w_in, ffn_w_out, ab_w_in, ab_gate_w, ab_gate_b, ab_gla_norm_g, ab_vnorm_g, ab_spatial_w, ab_spatial_b, ab_w_out, ssd_w_in, ssd_conv_w, ssd_conv_b, ssd_dt_bias, ssd_a_log, ssd_d, ssd_norm_g, ssd_w_out, final_norm_g, loss_target, m_c_ctx, m_mod_w, m_mod_b, m_norm_g, m_ffn_w_in, m_ffn_w_out, m_ab_w_in, m_ab_gate_w, m_ab_gate_b, m_ab_gla_norm_g, m_ab_vnorm_g, m_ab_spatial_w, m_ab_spatial_b, m_ab_w_out, m_ssd_w_in, m_ssd_conv_w, m_ssd_conv_b, m_ssd_dt_bias, m_ssd_a_log, m_ssd_d, m_ssd_norm_g, m_ssd_w_out, m_final_norm_g, v_c_ctx, v_mod_w, v_mod_b, v_norm_g, v_ffn_w_in, v_ffn_w_out, v_ab_w_in, v_ab_gate_w, v_ab_gate_b, v_ab_gla_norm_g, v_ab_vnorm_g, v_ab_spatial_w, v_ab_spatial_b, v_ab_w_out, v_ssd_w_in, v_ssd_conv_w, v_ssd_conv_b, v_ssd_dt_bias, v_ssd_a_log, v_ssd_d, v_ssd_norm_g, v_ssd_w_out, v_final_norm_g):
    a = dict(locals())
    me = _slot(*_mesh_pos())
    ws_mod = mod_w.shape[-1]

    fwd_small = [c] + [a[k] for k in SMALL_SHARDED]
    g_small = _ag_small("gather_small", _pack(fwd_small))
    parts = _unpack_flat(g_small, [t.shape for t in fwd_small])
    c_rows = parts[0].reshape(NDEV, D)
    full = {k: _unshard(p) for k, p in zip(SMALL_SHARDED, parts[1:])}
    c_all = jnp.concatenate([c_rows, c_ctx[None], jnp.zeros((7, D), F32)], axis=0)
    m_all = _ag_small("gather_mod", _mod_fwd(c_all, mod_w).reshape(2 * 16, ws_mod)).reshape(NDEV, 2, 16, ws_mod)
    m_mine = lax.dynamic_index_in_dim(m_all, me, axis=2, keepdims=False)
    mx = jnp.moveaxis(m_mine, 0, 1).reshape(2, N_MOD, D) + mod_b.reshape(2, N_MOD, D)
    mc = jnp.moveaxis(m_all[:, :, 8, :], 0, 1).reshape(2, N_MOD, D) + mod_b.reshape(2, N_MOD, D)
    pad2 = jnp.zeros((2, 2, D), F32)
    mod = jnp.concatenate([mx, pad2, mc, pad2], axis=1)

    big = {"l0_mixer": (ab_w_in[0], ab_w_out[0]), "l0_ffn": (ffn_w_in[0], ffn_w_out[0]),
           "l1_mixer": (ssd_w_in[0], ssd_w_out[0]), "l1_ffn": (ffn_w_in[1], ffn_w_out[1])}
    comm = _Exchange({k: tuple(w.astype(MXU_DTYPE) for w in v) for k, v in big.items()}, me)
    P = {
        "norm_g": full["norm_g"], "ab_gate_w": full["ab_gate_w"][0], "ab_gate_b": full["ab_gate_b"][0],
        "ab_gla_norm_g": ab_gla_norm_g, "ab_vnorm_g": ab_vnorm_g, "ab_spatial_w": ab_spatial_w[0], "ab_spatial_b": ab_spatial_b[0],
        "ssd_conv_w": full["ssd_conv_w"][0], "ssd_conv_b": full["ssd_conv_b"], "ssd_dt_bias": ssd_dt_bias[0],
        "ssd_a_log": ssd_a_log[0], "ssd_d": ssd_d, "ssd_norm_g": full["ssd_norm_g"], "final_norm_g": final_norm_g[None],
    }

    loss, grad_x, dmod, dP = _local_step(x[0], ctx[0], loss_target[0], mod, P, comm)

    dmx, dmc = dmod[:, 0:N_MOD].reshape(2, N_MOD * D), dmod[:, 8:8 + N_MOD].reshape(2, N_MOD * D)
    small_names = ("ab_gate_w", "ab_gate_b", "ab_gla_norm_g", "ab_vnorm_g", "ab_spatial_w", "ab_spatial_b", "norm_g", "ssd_conv_w",
                   "ssd_conv_b", "ssd_dt_bias", "ssd_a_log", "ssd_d", "ssd_norm_g", "final_norm_g")
    bwd_small = [dP[k] for k in small_names] + [dmc, dmx]
    shapes = [t.shape for t in bwd_small]
    g_bwd = _ag_small("gather_small_grads", _pack(bwd_small))
    summed = _unpack_flat(_sum_parts("sum_small_grads", g_bwd), shapes)
    gfull = dict(zip(small_names, summed[:-2]))
    dmc_sum, dmx_sum = summed[-2], summed[-1]
    dmx_all = _unpack_flat(g_bwd, shapes)[-1]
    dmx_sh = jnp.moveaxis(_my_shard(dmx_all, me, ws_mod), 0, 1)
    dm = jnp.concatenate([dmx_sh, _my_shard(dmc_sum, me, ws_mod)[:, None, :], jnp.zeros((2, 7, ws_mod), F32)], axis=1)
    d_mod_w, dsc = _mod_bwd(c_all, mod_w, dm)
    dsc_ctx = (dsc[0, 8] + dsc[1, 8])[None]
    dsc_all = _ag_small("gather_c_ctx_grad", jnp.concatenate([dsc_ctx, jnp.zeros((7, D), F32)], axis=0))
    d_c_ctx = _silu_vjp(c_ctx[None], _sum_parts("sum_c_ctx_grad", dsc_all)[0:1])[0]

    g_small_w = {
        "c_ctx": d_c_ctx, "mod_b": dmx_sum + dmc_sum, "norm_g": gfull["norm_g"], "ab_gate_w": gfull["ab_gate_w"][None],
        "ab_gate_b": gfull["ab_gate_b"][None], "ab_gla_norm_g": gfull["ab_gla_norm_g"], "ab_vnorm_g": gfull["ab_vnorm_g"],
        "ab_spatial_w": gfull["ab_spatial_w"][None], "ab_spatial_b": gfull["ab_spatial_b"][None], "ssd_conv_w": gfull["ssd_conv_w"][None],
        "ssd_conv_b": gfull["ssd_conv_b"], "ssd_dt_bias": gfull["ssd_dt_bias"][None], "ssd_a_log": gfull["ssd_a_log"][None],
        "ssd_d": gfull["ssd_d"], "ssd_norm_g": gfull["ssd_norm_g"], "final_norm_g": gfull["final_norm_g"][0],
    }
    for k in SMALL_SHARDED:
        g_small_w[k] = _my_shard(g_small_w[k], me, a[k].shape[-1])
    token = comm.start_last([d_c_ctx])
    res = _adam("adam_small", _pack([a[k] for k in SMALL]), [_pack([g_small_w[k] for k in SMALL])[None]],
                _pack([a["m_" + k] for k in SMALL]), _pack([a["v_" + k] for k in SMALL]), token)
    out = {k: vals for k, vals in zip(SMALL, zip(*[_unpack_flat(r, [a[k].shape for k in SMALL]) for r in res]))}

    def adam_big(name, w2d, parts, m2d, v2d, shape):
        return tuple(r.reshape(shape) for r in _adam(name, w2d, parts, m2d, v2d, token))

    def flat2(t):
        return t.reshape(-1, t.shape[-1])

    out["mod_w"] = adam_big("adam_mod_w", flat2(mod_w), [d_mod_w.reshape(1, -1, ws_mod)], flat2(m_mod_w), flat2(v_mod_w), mod_w.shape)

    for j, k in enumerate(("ffn_w_in", "ffn_w_out")):
        out[k] = adam_big("adam_" + k, flat2(a[k]), [comm.recv["l0_ffn"][j], comm.recv["l1_ffn"][j]], flat2(a["m_" + k]),
                          flat2(a["v_" + k]), a[k].shape)
    for j, k in enumerate(("ssd_w_in", "ssd_w_out")):
        out[k] = adam_big("adam_" + k, a[k][0], [comm.recv["l1_mixer"][j]], a["m_" + k][0], a["v_" + k][0], a[k].shape)
    recv_ab = comm.finish(out["ssd_w_out"][3])
    for j, k in enumerate(("ab_w_in", "ab_w_out")):
        out[k] = adam_big("adam_" + k, a[k][0], [recv_ab[j]], a["m_" + k][0], a["v_" + k][0], a[k].shape)

    loss_all = lax.psum(loss[0, 0], ("x", "y", "c"))
    return (loss_all, grad_x[None], *[out[k][0] for k in WEIGHTS], *[out[k][1] for k in WEIGHTS],
            *[out[k][2] for k in WEIGHTS], *[out[k][3] for k in WEIGHTS])
```

```python
import functools
import math

import jax
import jax.numpy as jnp
from jax import lax
from jax.experimental import pallas as pl
from jax.experimental.pallas import tpu as pltpu

F32 = jnp.float32
BF16 = jnp.bfloat16
MXU_DTYPE = jnp.bfloat16
HI = lax.Precision.HIGHEST

D = 1024
NDEV = 8
N_MOD = 6
EPS = 1e-6
GRID_W = 64
CTX = 256
TM = 256
D_FF = 2816
GLA_H, GLA_DK, GLA_DV, GLA_LR, GLA_TAU, GLA_L = 4, 64, 128, 16, 16.0, 64
GMLP_G, GMLP_C, GMLP_L = 4, 128, 128
SSD_H, SSD_P, SSD_G, SSD_N, SSD_L, SSD_K = 32, 64, 4, 128, 128, 5
SSD_INNER = SSD_H * SSD_P
AB_IN = 2592
SSD_IN = 5184
AB_SEGS = ((256, 768), (1056, 1568), (1568, 2080), (2080, 2592), (0, 256), (800, 1056), (768, 800))
AB_P = 2816
SSD_SEGS = ((0, 2048), (3136, 5184), (2048, 2560), (2560, 3072), (3072, 3136))
SSD_P_W = 5376
VMEM_LIMIT = 56 * 1024 * 1024

ADAM_LR, ADAM_B1, ADAM_B2, ADAM_EPS, ADAM_WD, ADAM_STEP = 0.001, 0.9, 0.999, 1e-08, 0.01, 10


def _cp(sem=None, **kw):
    return pltpu.CompilerParams(dimension_semantics=sem, vmem_limit_bytes=VMEM_LIMIT, **kw)


def _dot(a, b, dims=(((1,), (0,)), ((), ()))):
    return lax.dot_general(a.astype(MXU_DTYPE), b.astype(MXU_DTYPE), dims, preferred_element_type=F32)


def _dot_nt(a, b):
    return _dot(a, b, (((1,), (1,)), ((), ())))


def _dot_tn(a, b):
    return _dot(a, b, (((0,), (0,)), ((), ())))


def _dotx(a, b, dims=(((1,), (0,)), ((), ()))):
    return lax.dot_general(a, b, dims, precision=HI, preferred_element_type=F32)


def _rms(x):
    return x * lax.rsqrt(jnp.mean(x * x, axis=-1, keepdims=True) + EPS)


def _pick(n, prefs):
    for p in prefs:
        if n % p == 0:
            return p
    return n


def _row(arr, width=None, colblk=0, tm=TM):
    width = arr.shape[1] if width is None else width
    return (arr, pl.BlockSpec((tm, width), lambda i, c=colblk: (i, c)))


def _full_spec(p):
    nd = p.ndim
    return pl.BlockSpec(p.shape, lambda i, nd=nd: (0,) * nd)


def _rowwise(name, fn, n_blocks, ctx_blk, rows, params, outs, tm=TM):
    nr, npar = len(rows), len(params)

    def body(*refs):
        t = (pl.program_id(0) >= ctx_blk).astype(F32)
        rv = [r[...].astype(F32) for r in refs[:nr]]
        pv = [p[...] for p in refs[nr:nr + npar]]
        res = fn(t, rv, pv)
        for o_ref, o in zip(refs[nr + npar:], res):
            o_ref[...] = o.astype(o_ref.dtype)

    return pl.pallas_call(
        body, name=name, grid=(n_blocks,),
        in_specs=[s for _, s in rows] + [_full_spec(p) for p in params],
        out_specs=[pl.BlockSpec((tm, w), lambda i: (i, 0)) for w, _ in outs],
        out_shape=[jax.ShapeDtypeStruct((n_blocks * tm, w), dt) for w, dt in outs],
        compiler_params=_cp(("parallel",)),
    )(*[a for a, _ in rows], *params)


def _rowwise_vjp(name, fn, n_blocks, ctx_blk, rows, params, douts, row_grads, tm=TM):
    nr, npar, nd = len(rows), len(params), len(douts)
    adds = [a for _, _, a in row_grads if a is not None]
    na = len(adds)

    def body(*refs):
        i = pl.program_id(0)
        t = (i >= ctx_blk).astype(F32)
        rv = [r[...].astype(F32) for r in refs[:nr]]
        pv = [p[...] for p in refs[nr:nr + npar]]
        dv = [r[...].astype(F32) for r in refs[nr + npar:nr + npar + nd]]
        av = [r[...].astype(F32) for r in refs[nr + npar + nd:nr + npar + nd + na]]
        o_refs = refs[nr + npar + nd + na:]
        _, vjp = jax.vjp(lambda r, p: tuple(fn(t, r, p)), rv, pv)
        d_rows, d_params = vjp(tuple(dv))
        ai = 0
        for o_ref, (ri, _, addend) in zip(o_refs, row_grads):
            g = d_rows[ri]
            if addend is not None:
                g = g + av[ai]
                ai += 1
            o_ref[...] = g.astype(o_ref.dtype)
        p_refs = o_refs[len(row_grads):]

        @pl.when(i == 0)
        def _():
            for p_ref in p_refs:
                p_ref[...] = jnp.zeros_like(p_ref)

        for p_ref, g in zip(p_refs, d_params):
            p_ref[...] += g

    widths = [rows[ri][1].block_shape[1] for ri, _, _ in row_grads]
    res = pl.pallas_call(
        body, name=name, grid=(n_blocks,),
        in_specs=[s for _, s in rows] + [_full_spec(p) for p in params] + [s for _, s in douts] + [s for _, s in adds],
        out_specs=[pl.BlockSpec((tm, w), lambda i: (i, 0)) for w in widths] + [_full_spec(p) for p in params],
        out_shape=[jax.ShapeDtypeStruct((n_blocks * tm, w), dt) for w, (_, dt, _) in zip(widths, row_grads)]
        + [jax.ShapeDtypeStruct(p.shape, F32) for p in params],
        compiler_params=_cp(("arbitrary",)),
    )(*[a for a, _ in rows], *params, *[a for a, _ in douts], *[a for a, _ in adds])
    return res[:len(row_grads)], res[len(row_grads):]


def _mm(name, a, b, mode, out_dtype):
    if mode == "nn":
        m, kk = a.shape
        n = b.shape[1]
    elif mode == "nt":
        m, kk = a.shape
        n = b.shape[0]
    else:
        kk, m = a.shape
        n = b.shape[1]
    if mode == "tn":
        tm = _pick(m, (1024, 1408, 512, 256, 128))
        tn = _pick(n, (768, 512, 256, 128))
        tk = kk
    else:
        tm = _pick(m, (1088, 1024, 768, 512, 384, 256, 128))
        tn = n if n <= 2816 and mode == "nn" else _pick(n, (1024, 768, 512, 256, 128))
        tk = kk if kk <= 2816 else _pick(kk, (2816, 1792, 1024, 768, 512, 256, 128))
    nk = kk // tk
    in_place = out_dtype == F32
    if mode == "nn":
        specs = [pl.BlockSpec((tm, tk), lambda i, j, k: (i, k)), pl.BlockSpec((tk, tn), lambda i, j, k: (k, j))]
        dims = (((1,), (0,)), ((), ()))
    elif mode == "nt":
        specs = [pl.BlockSpec((tm, tk), lambda i, j, k: (i, k)), pl.BlockSpec((tn, tk), lambda i, j, k: (j, k))]
        dims = (((1,), (1,)), ((), ()))
    else:
        specs = [pl.BlockSpec((tk, tm), lambda i, j, k: (k, i)), pl.BlockSpec((tk, tn), lambda i, j, k: (k, j))]
        dims = (((0,), (0,)), ((), ()))

    def body(a_ref, b_ref, o_ref, *scratch):
        part = lax.dot_general(a_ref[...].astype(MXU_DTYPE), b_ref[...].astype(MXU_DTYPE), dims, preferred_element_type=F32)
        if nk == 1:
            o_ref[...] = part.astype(o_ref.dtype)
        else:
            k = pl.program_id(2)
            acc = o_ref if in_place else scratch[0]

            @pl.when(k == 0)
            def _():
                acc[...] = part

            @pl.when(k > 0)
            def _():
                acc[...] += part

            if not in_place:
                @pl.when(k == nk - 1)
                def _():
                    o_ref[...] = acc[...].astype(o_ref.dtype)

    return pl.pallas_call(
        body, name=name, grid=(m // tm, n // tn, nk), in_specs=specs,
        out_specs=pl.BlockSpec((tm, tn), lambda i, j, k: (i, j)),
        out_shape=jax.ShapeDtypeStruct((m, n), out_dtype),
        scratch_shapes=[] if nk == 1 or in_place else [pltpu.VMEM((tm, tn), F32)],
        compiler_params=_cp(("parallel", "parallel", "arbitrary")),
    )(a, b)


def _sel_mod(modp, t):
    return modp[0:8] * (1.0 - t) + modp[8:16] * t


def _fn_prenorm(t, rows, params, *, a, b):
    (x,), (g, modp) = rows, params
    m = _sel_mod(modp, t)
    return ((_rms(x) * g) * (1.0 + m[b:b + 1]) + m[a:a + 1],)


def _fn_resid_prenorm(t, rows, params, *, gi, a, b):
    (x, y), (g, mod_a, mod_b) = rows, params
    ma, mb = _sel_mod(mod_a, t), _sel_mod(mod_b, t)
    xn = x + ma[gi:gi + 1] * y
    return xn, (_rms(xn) * g) * (1.0 + mb[b:b + 1]) + mb[a:a + 1]


def _fn_resid(t, rows, params, *, gi):
    (x, y), (mod_a,) = rows, params
    return (x + _sel_mod(mod_a, t)[gi:gi + 1] * y,)


def _fn_swiglu(t, rows, params):
    (pf,) = rows
    return (jax.nn.silu(pf[:, :D_FF]) * pf[:, D_FF:],)


def _fn_mixpost(t, rows, params):
    (o, r, u, g), (gla_g, vn_g, sw, sb_t) = rows, params
    a =jnp.concatenate([_rms(o[:, h * GLA_DV:(h + 1) * GLA_DV]) for h in range(GLA_H)], axis=1) * gla_g * jax.nn.silu(r)
    uu, vv = jax.nn.gelu(u), jax.nn.gelu(g)
    mu = jnp.mean(vv, axis=-1, keepdims=True)
    var = jnp.mean(jnp.square(vv - mu), axis=-1, keepdims=True)
    vn = ((vv - mu) * lax.rsqrt(var + EPS)) * vn_g
    s = jnp.concatenate(
        [_dot(sw[gi * GMLP_L:(gi + 1) * GMLP_L, :], vn[:, gi * GMLP_C:(gi + 1) * GMLP_C]) + sb_t[:, gi:gi + 1]
         for gi in range(GMLP_G)], axis=1)
    return (jnp.concatenate([a, uu * s], axis=1),)


def _head_expand():
    r = lax.broadcasted_iota(jnp.int32, (SSD_H, SSD_INNER), 0)
    c = lax.broadcasted_iota(jnp.int32, (SSD_H, SSD_INNER), 1)
    return (c // SSD_P == r).astype(F32)


def _fn_ssd_finish(t, rows, params):
    (y2, xs, z), (d_skip, norm_g) = rows, params
    d_full = _dotx(jnp.broadcast_to(d_skip, (8, SSD_H)), _head_expand())[0:1]
    y = (y2 + d_full * xs) * jax.nn.silu(z)
    gw = SSD_INNER // SSD_G
    return (jnp.concatenate([_rms(y[:, gi * gw:(gi + 1) * gw]) for gi in range(SSD_G)], axis=1) * norm_g,)


def _fn_concat(t, rows, params, *, sums, pad=0):
    out, i = [], 0
    for n in sums:
        acc = rows[i]
        for j in range(1, n):
            acc = acc + rows[i + j]
        out.append(acc)
        i += n
    if pad:
        out.append(jnp.zeros((out[0].shape[0], pad), F32))
    return (jnp.concatenate(out, axis=1),)


def _tri(n, rev):
    r = lax.broadcasted_iota(jnp.int32, (n, n), 0)
    c = lax.broadcasted_iota(jnp.int32, (n, n), 1)
    return (r <= c) if rev else (r >= c)


def _gla_chunk(S, v, k, q, tail, gw, gb, *, rev):
    L, H = GLA_L, GLA_H
    msk = _tri(L, rev)
    tri = msk.astype(F32)
    lr = tail[:, GLA_LR:2 * GLA_LR] if rev else tail[:, 0:GLA_LR]
    la = jax.nn.log_sigmoid(_dot(lr, gw) + gb) / GLA_TAU
    b = _dotx(tri, la)
    b_last = b[0:1] if rev else b[L - 1:L]
    kd = k * jnp.exp(b_last - b)
    qd = (q * GLA_DK ** -0.5) * jnp.exp(b)
    ki = k * jnp.exp(-b)

    def same_head(shape, rows_per_head, cols_per_head):
        r = lax.broadcasted_iota(jnp.int32, shape, 0) // rows_per_head
        c = lax.broadcasted_iota(jnp.int32, shape, 1) // cols_per_head
        return r == c

    k_blk = jnp.where(same_head((H * L, H * GLA_DK), L, GLA_DK), jnp.concatenate([ki] * H, axis=0), 0.0)
    v_blk = jnp.where(same_head((H * L, H * GLA_DV), L, GLA_DV), jnp.concatenate([v] * H, axis=0), 0.0)
    row = lax.broadcasted_iota(jnp.int32, (L, H * L), 0)
    src = lax.broadcasted_iota(jnp.int32, (L, H * L), 1) % L
    sc = jnp.where((row <= src) if rev else (row >= src), _dot_nt(qd, k_blk), 0.0)
    o = _dot_nt(qd, S) + _dot(sc, v_blk)
    s_new = S * jnp.exp(b_last) + jnp.where(same_head(S.shape, GLA_DV, GLA_DK), _dot_tn(v, kd), 0.0)
    return s_new, o


def _ssd_chunk(S, x, bm, cm, tail, dtb, alog, *, rev):
    L = SSD_L
    msk = _tri(L, rev)
    tri = msk.astype(F32)
    raw = tail[:, SSD_H:2 * SSD_H] if rev else tail[:, 0:SSD_H]
    dt = jax.nn.softplus(raw + dtb)
    dta = dt * (-jnp.exp(alog))
    acum = _dotx(tri, dta)
    a_last = acum[0:1] if rev else acum[L - 1:L]
    wst = dt * jnp.exp(a_last - acum)
    eac = jnp.exp(acum)
    tr = jnp.concatenate([acum, dt, wst, jnp.zeros((L, L - 3 * SSD_H), F32)], axis=1).T
    acum_t, dt_t, wst_t = tr[0:SSD_H], tr[SSD_H:2 * SSD_H], tr[2 * SSD_H:3 * SSD_H]
    decrow = jnp.exp(_dotx(jnp.broadcast_to(a_last, (8, SSD_H)), _head_expand())[0:1])
    lane = lax.broadcasted_iota(jnp.int32, (1, 2 * SSD_P), 1)
    m0 = (lane < SSD_P).astype(F32)
    m1 = 1.0 - m0
    pairs_per_group = SSD_H // SSD_G // 2
    y_parts, s_parts = [], []
    for g in range(SSD_G):
        ns = slice(g * SSD_N, (g + 1) * SSD_N)
        bg, cg = bm[:, ns], cm[:, ns]
        cb = _dot_nt(cg, bg)
        bgt = bg.T
        for jj in range(pairs_per_group):
            j = g * pairs_per_group + jj
            ls = slice(j * 2 * SSD_P, (j + 1) * 2 * SSD_P)
            xp, sp = x[:, ls], S[:, ls]
            xm = jnp.concatenate([xp * m0, xp * m1], axis=0)
            sm = jnp.concatenate([sp * m0, sp * m1], axis=0)
            lhs, bw = [], []
            for h in (2 * j, 2 * j + 1):
                seg = acum[:, h:h + 1] - acum_t[h:h + 1, :]
                lhs.append(cb * jnp.exp(jnp.where(msk, seg, -jnp.inf)) * dt_t[h:h + 1, :])
                bw.append(bgt * wst_t[h:h + 1, :])
            lhs += [cg * eac[:, h:h + 1] for h in (2 * j, 2 * j + 1)]
            y_parts.append(_dot(jnp.concatenate(lhs, axis=1), jnp.concatenate([xm, sm], axis=0)))
            s_parts.append(sp * decrow[:, ls] + _dot(jnp.concatenate(bw, axis=1), xm))
    return jnp.concatenate(s_parts, axis=1), jnp.concatenate(y_parts, axis=1)


def _multi_chunk(chunk_fn, L, subs, nr):
    def fn(S, *args, rev):
        rows, params = args[:nr], args[nr:]
        ys = [None] * subs
        for j in (range(subs - 1, -1, -1) if rev else range(subs)):
            S, ys[j] = chunk_fn(S, *[r[j * L:(j + 1) * L] for r in rows], *params, rev=rev)
        return S, jnp.concatenate(ys, axis=0)

    return fn


def _scan_order(n, nx, rev, backward):
    nc = n - nx

    def fwd(s):
        return (n - 1 - s) if rev else jnp.where(s < nc, s + nx, s - nc)

    return (lambda s: fwd(n - 1 - s)) if backward else fwd


def _scan_fwd(name, chunk_fn, L, n, nx, rev, rows, params, state_shape, out_w, addend=None):
    order = _scan_order(n, nx, rev, False)
    nr, npar = len(rows), len(params)
    adds = [] if addend is None else [addend]

    def body(*refs):
        s_scr = refs[-1]

        @pl.when(pl.program_id(0) == 0)
        def _():
            s_scr[...] = jnp.zeros_like(s_scr)

        s_in = s_scr[...]
        y_ref, st_ref = refs[nr + npar + len(adds)], refs[nr + npar + len(adds) + 1]
        st_ref[0] = s_in
        s_new, y = chunk_fn(s_in, *[r[...] for r in refs[:nr]], *[p[...] for p in refs[nr:nr + npar]], rev=rev)
        y_ref[...] = y + refs[nr + npar][...] if adds else y
        s_scr[...] = s_new

    return pl.pallas_call(
        body, name=name, grid=(n,),
        in_specs=[pl.BlockSpec((L, w), lambda s, c=c: (order(s), c)) for _, w, c in rows] + [_full_spec(p) for p in params]
        + [pl.BlockSpec((L, out_w), lambda s: (order(s), 0)) for _ in adds],
        out_specs=[pl.BlockSpec((L, out_w), lambda s: (order(s), 0)),
                   pl.BlockSpec((1,) + state_shape, lambda s: (order(s), 0, 0))],
        out_shape=[jax.ShapeDtypeStruct((n * L, out_w), F32), jax.ShapeDtypeStruct((n,) + state_shape, F32)],
        scratch_shapes=[pltpu.VMEM(state_shape, F32)],
        compiler_params=_cp(("arbitrary",)),
    )(*[a for a, _, _ in rows], *params, *adds)


def _scan_bwd(name, chunk_fn, L, n, nx, rev, rows, params, states, dy, state_shape, out_w, addends=None):
    order = _scan_order(n, nx, rev, True)
    nr, npar = len(rows), len(params)
    adds = [] if addends is None else list(addends)

    def body(*refs):
        i = pl.program_id(0)
        ds_scr = refs[-1]
        rv = [r[...] for r in refs[:nr]]
        pv = [p[...] for p in refs[nr:nr + npar]]
        st_ref, dy_ref = refs[nr + npar], refs[nr + npar + 1]
        a_refs = refs[nr + npar + 2:nr + npar + 2 + len(adds)]
        o_refs = refs[nr + npar + 2 + len(adds):-1]
        p_refs = o_refs[nr:]

        @pl.when(i == 0)
        def _():
            ds_scr[...] = jnp.zeros_like(ds_scr)
            for p_ref in p_refs:
                p_ref[...] = jnp.zeros_like(p_ref)

        _, vjp = jax.vjp(functools.partial(chunk_fn, rev=rev), st_ref[0], *rv, *pv)
        grads = vjp((ds_scr[...], dy_ref[...].astype(F32)))
        ds_scr[...] = grads[0]
        for j, (o_ref, g) in enumerate(zip(o_refs[:nr], grads[1:1 + nr])):
            o_ref[...] = g + a_refs[j][...] if adds else g
        for p_ref, g in zip(p_refs, grads[1 + nr:]):
            p_ref[...] += g

    row_specs = [pl.BlockSpec((L, w), lambda s: (order(s), 0)) for _, w, _ in rows]
    res = pl.pallas_call(
        body, name=name, grid=(n,),
        in_specs=[pl.BlockSpec((L, w), lambda s, c=c: (order(s), c)) for _, w, c in rows] + [_full_spec(p) for p in params]
        + [pl.BlockSpec((1,) + state_shape, lambda s: (order(s), 0, 0)), pl.BlockSpec((L, out_w), lambda s: (order(s), 0))]
        + row_specs[:len(adds)],
        out_specs=row_specs + [_full_spec(p) for p in params],
        out_shape=[jax.ShapeDtypeStruct((n * L, w), F32) for _, w, _ in rows] + [jax.ShapeDtypeStruct(p.shape, F32) for p in params],
        scratch_shapes=[pltpu.VMEM(state_shape, F32)],
        compiler_params=_cp(("arbitrary",)),
    )(*[a for a, _, _ in rows], *params, states, dy, *adds)
    return res[:nr], res[nr:]


CONV_W = 1024
CONV_COLBLK = (0, 1, 4)


def _conv_specs(nb, src_blk):
    halo = TM // 8
    return [pl.BlockSpec((TM, CONV_W), lambda j, i: (i, src_blk(j))),
            pl.BlockSpec((8, CONV_W), lambda j, i: (jnp.maximum(i * halo - 1, 0), src_blk(j))),
            pl.BlockSpec((8, CONV_W), lambda j, i: (jnp.minimum(i * halo + halo, nb * halo - 1), src_blk(j)))]


def _conv_ext(i, nb, cur, prev, nxt):
    has_prev = jnp.logical_and(i > 0, i < nb - 1)
    has_next = i < nb - 2
    return jnp.concatenate([jnp.where(has_prev, prev, 0.0), cur, jnp.where(has_next, nxt, 0.0)], axis=0)


def _conv_taps(ext, w, flip):
    acc = None
    for j in range(SSD_K):
        wj = w[SSD_K - 1 - j:SSD_K - j, :] if flip else w[j:j + 1, :]
        term = wj * ext[6 + j:6 + j + TM, :]
        acc = term if acc is None else acc + term
    return acc


def _conv(name, src, w8, b1, nb, *, permuted_src, act, flip, out_dtype):
    src_blk = (lambda j: jnp.where(j == 2, CONV_COLBLK[2], j)) if permuted_src else (lambda j: j)

    def body(cur, prev, nxt, w_ref, b_ref, o_ref):
        ext = _conv_ext(pl.program_id(1), nb, cur[...].astype(F32), prev[...].astype(F32), nxt[...].astype(F32))
        acc = _conv_taps(ext, w_ref[...], flip)
        if act:
            acc = jax.nn.silu(acc + b_ref[...])
        o_ref[...] = acc.astype(o_ref.dtype)

    return pl.pallas_call(
        body, name=name, grid=(3, nb),
        in_specs=_conv_specs(nb, src_blk) + [pl.BlockSpec((8, CONV_W), lambda j, i: (0, j)), pl.BlockSpec((1, CONV_W), lambda j, i: (0, j))],
        out_specs=pl.BlockSpec((TM, CONV_W), lambda j, i: (i, j)),
        out_shape=jax.ShapeDtypeStruct((nb * TM, 3 * CONV_W), out_dtype),
        compiler_params=_cp(("parallel", "parallel")),
    )(src, src, src, w8, b1)


def _conv_bwd_pre(name, p1, w8, b1, dxbc_parts, nb):
    src_blk = lambda j: jnp.where(j == 2, CONV_COLBLK[2], j)
    xs_parts, bc_parts = dxbc_parts
    n_x, n_bc = len(xs_parts), len(bc_parts)

    def body(*refs):
        cur, prev, nxt, w_ref, b_ref = refs[:5]
        d_refs = refs[5:5 + n_x + n_bc]
        da_ref, dw_ref, db_ref = refs[5 + n_x + n_bc:]
        j, i = pl.program_id(0), pl.program_id(1)
        ext = _conv_ext(i, nb, cur[...], prev[...], nxt[...])
        acc = _conv_taps(ext, w_ref[...], False) + b_ref[...]
        dx = d_refs[0][...]
        for r in d_refs[1:n_x]:
            dx = dx + r[...]
        dbc = jnp.concatenate([d_refs[n_x][...], d_refs[n_x + 1][...]], axis=1)
        dy = jnp.where(j == 2, dbc, dx)
        sg = jax.nn.sigmoid(acc)
        da = dy * (sg + acc * sg * (1.0 - sg))
        da_ref[...] = da

        @pl.when(i == 0)
        def _():
            dw_ref[...] = jnp.zeros_like(dw_ref)
            db_ref[...] = jnp.zeros_like(db_ref)

        rows = [jnp.sum(da * ext[6 + t:6 + t + TM, :], axis=0, keepdims=True) for t in range(SSD_K)]
        dw_ref[...] += jnp.concatenate(rows + [jnp.zeros((8 - SSD_K, CONV_W), F32)], axis=0)
        db_ref[...] += jnp.sum(da, axis=0, keepdims=True)

    x_specs = [pl.BlockSpec((TM, CONV_W), lambda j, i: (i, jnp.minimum(j, 1))) for _ in xs_parts]
    bc_specs = [pl.BlockSpec((TM, 512), lambda j, i: (i, 0)) for _ in bc_parts]
    return pl.pallas_call(
        body, name=name, grid=(3, nb),
        in_specs=_conv_specs(nb, src_blk) + [pl.BlockSpec((8, CONV_W), lambda j, i: (0, j)), pl.BlockSpec((1, CONV_W), lambda j, i: (0, j))]
        + x_specs + bc_specs,
        out_specs=[pl.BlockSpec((TM, CONV_W), lambda j, i: (i, j)), pl.BlockSpec((8, CONV_W), lambda j, i: (0, j)),
                   pl.BlockSpec((1, CONV_W), lambda j, i: (0, j))],
        out_shape=[jax.ShapeDtypeStruct((nb * TM, 3 * CONV_W), F32), jax.ShapeDtypeStruct((8, 3 * CONV_W), F32),
                   jax.ShapeDtypeStruct((1, 3 * CONV_W), F32)],
        compiler_params=_cp(("arbitrary", "arbitrary")),
    )(p1, p1, p1, w8, b1, *xs_parts, *bc_parts)


def _grid_block(a):
    nbv = TM // a
    blk_b = max(nbv, 8)
    return nbv, blk_b, blk_b // nbv


def _grid_spec(a, nb):
    _, blk_b, per = _grid_block(a)
    return pl.BlockSpec((a, blk_b, D), lambda i: (0, jnp.minimum(i, nb - 2) // per, 0))


def _grid_rows(v_ref, a, i):
    nbv, _, per = _grid_block(a)

    def pick(ph):
        return jnp.concatenate([v_ref[:, ph * nbv + t, :] for t in range(nbv)], axis=0)

    out = pick(0)
    for ph in range(1, per):
        out = jnp.where(i % per == ph, pick(ph), out)
    return out


def _perm(name, xc, a, nb):
    n = xc.shape[0]
    b = (n - CTX) // a
    view = xc.reshape(n // b, b, D)

    def body(v_ref, c_ref, o_ref):
        i = pl.program_id(0)

        @pl.when(i < nb - 1)
        def _():
            o_ref[...] = _grid_rows(v_ref, a, i)

        @pl.when(i == nb - 1)
        def _():
            o_ref[...] = c_ref[...]

    return pl.pallas_call(
        body, name=name, grid=(nb,),
        in_specs=[_grid_spec(a, nb), pl.BlockSpec((TM, D), lambda i: (nb - 1, 0))],
        out_specs=pl.BlockSpec((TM, D), lambda i: (i, 0)),
        out_shape=jax.ShapeDtypeStruct((n, D), xc.dtype),
        compiler_params=_cp(("parallel",)),
    )(view, xc)


def _loss_head(x, f, target, modp, g_final, nb, rows_r):
    tview = target.reshape(rows_r, target.shape[0] // rows_r, D)

    def fn(x_, f_, tgt, modp_, g_, is_ctx):
        xn = x_ + _sel_mod(modp_, is_ctx)[5:6] * f_
        err = _rms(xn) * g_ - tgt
        return 0.5 * jnp.sum(jnp.mean(err * err, axis=-1)) * (1.0 - is_ctx)

    def body(x_ref, f_ref, t_ref, m_ref, g_ref, l_ref, dx_ref, df_ref, dm_ref, dg_ref):
        i = pl.program_id(0)
        is_ctx = (i == nb - 1).astype(F32)
        tgt = _grid_rows(t_ref, rows_r, i)
        l, vjp = jax.vjp(lambda a_, b_, c_, d_: fn(a_, b_, tgt, c_, d_, is_ctx), x_ref[...], f_ref[...], m_ref[...], g_ref[...])
        dx, df, dm, dg = vjp(jnp.ones((), F32))

        @pl.when(i == 0)
        def _():
            l_ref[...] = jnp.zeros_like(l_ref)
            dm_ref[...] = jnp.zeros_like(dm_ref)
            dg_ref[...] = jnp.zeros_like(dg_ref)

        l_ref[...] += jnp.reshape(l, (1, 1))
        dx_ref[...] = dx
        df_ref[...] = df.astype(df_ref.dtype)
        dm_ref[...] += dm
        dg_ref[...] += dg

    rowspec = pl.BlockSpec((TM, D), lambda i: (i, 0))
    return pl.pallas_call(
        body, name="loss_head", grid=(nb,),
        in_specs=[rowspec, rowspec, _grid_spec(rows_r, nb), _full_spec(modp), _full_spec(g_final)],
        out_specs=[pl.BlockSpec((1, 1), lambda i: (0, 0)), rowspec, rowspec, _full_spec(modp), _full_spec(g_final)],
        out_shape=[jax.ShapeDtypeStruct((1, 1), F32), jax.ShapeDtypeStruct(x.shape, F32), jax.ShapeDtypeStruct(x.shape, MXU_DTYPE),
                   jax.ShapeDtypeStruct(modp.shape, F32), jax.ShapeDtypeStruct(g_final.shape, F32)],
        compiler_params=_cp(("arbitrary",)),
    )(x, f, tview, modp, g_final)


def _repack(name, shards, segs, wp):
    nd, kk, ws = shards.shape
    tr = 128
    used = sum(e - s for s, e in segs)

    def body(a_ref, o_ref):
        full = jnp.concatenate([a_ref[d].astype(F32) for d in range(nd)], axis=1)
        parts = [full[:, s:e] for s, e in segs]
        if wp > used:
            parts.append(jnp.zeros((tr, wp - used), F32))
        o_ref[...] = jnp.concatenate(parts, axis=1).astype(o_ref.dtype)

    return pl.pallas_call(
        body, name=name, grid=(kk // tr,),
        in_specs=[pl.BlockSpec((nd, tr, ws), lambda i: (0, i, 0))],
        out_specs=pl.BlockSpec((tr, wp), lambda i: (i, 0)),
        out_shape=jax.ShapeDtypeStruct((kk, wp), MXU_DTYPE),
        compiler_params=_cp(("parallel",)),
    )(shards)


def _unpack(name, dw, segs, ws, out_dtype):
    kk, wp = dw.shape
    tr = 128
    order = sorted(range(len(segs)), key=lambda i: segs[i][0])
    offs, o = [], 0
    for s, e in segs:
        offs.append(o)
        o += e - s

    def body(a_ref, o_ref):
        a = a_ref[...].astype(F32)
        full = jnp.concatenate([a[:, offs[i]:offs[i] + segs[i][1] - segs[i][0]] for i in order], axis=1)
        for d in range(NDEV):
            o_ref[d] = full[:, d * ws:(d + 1) * ws].astype(o_ref.dtype)

    return pl.pallas_call(
        body, name=name, grid=(kk // tr,),
        in_specs=[pl.BlockSpec((tr, wp), lambda i: (i, 0))],
        out_specs=pl.BlockSpec((NDEV, tr, ws), lambda i: (0, i, 0)),
        out_shape=jax.ShapeDtypeStruct((NDEV, kk, ws), out_dtype),
        compiler_params=_cp(("parallel",)),
    )(dw)


def _adam_math(w, g, m, v):
    m = ADAM_B1 * m + (1.0 - ADAM_B1) * g
    v = ADAM_B2 * v + (1.0 - ADAM_B2) * jnp.square(g)
    m_hat = m / (1.0 - ADAM_B1 ** ADAM_STEP)
    v_hat = v / (1.0 - ADAM_B2 ** ADAM_STEP)
    delta = -ADAM_LR * (m_hat / (jnp.sqrt(v_hat) + ADAM_EPS) + ADAM_WD * w)
    return delta, m, v


def _adam(name, w, parts, m, v, after):
    r, c = w.shape
    nsec, npart = len(parts), parts[0].shape[0]
    rs = r // nsec
    tr = _pick(rs, (256, 128, 64, 32, 16, 8)) if rs * c * 4 > (1 << 20) else rs
    tiles = rs // tr

    def body(w_ref, *refs):
        m_ref, v_ref, _, g_ref, d_ref, nm_ref, nv_ref = refs[nsec:]
        sec = pl.program_id(0) // tiles
        g = None
        for a, p_ref in enumerate(refs[:nsec]):
            ga = p_ref[0].astype(F32)
            for s in range(1, npart):
                ga = ga + p_ref[s].astype(F32)
            g = ga if g is None else jnp.where(sec == a, ga, g)
        delta, nm, nv = _adam_math(w_ref[...], g, m_ref[...], v_ref[...])
        g_ref[...], d_ref[...], nm_ref[...], nv_ref[...] = g, delta, nm, nv

    spec = pl.BlockSpec((tr, c), lambda i: (i, 0))
    part_specs = [pl.BlockSpec((npart, tr, c), lambda i, a=a: (0, jnp.clip(i - a * tiles, 0, tiles - 1), 0)) for a in range(nsec)]
    return pl.pallas_call(
        body, name=name, grid=(r // tr,),
        in_specs=[spec] + part_specs + [spec, spec, ANY],
        out_specs=[spec] * 4, out_shape=[jax.ShapeDtypeStruct((r, c), F32)] * 4,
        compiler_params=_cp(("parallel",)),
    )(w, *parts, m, v, after)


def _mod_fwd(c_all, mod_w):
    nl, _, ws = mod_w.shape

    def body(c_ref, w_ref, o_ref):
        o_ref[0] = _dot(jax.nn.silu(c_ref[...]), w_ref[0])

    return pl.pallas_call(
        body, name="mod_fwd", grid=(nl,),
        in_specs=[_full_spec(c_all), pl.BlockSpec((1, D, ws), lambda i: (i, 0, 0))],
        out_specs=pl.BlockSpec((1, 16, ws), lambda i: (i, 0, 0)),
        out_shape=jax.ShapeDtypeStruct((nl, 16, ws), F32),
        compiler_params=_cp(("parallel",)),
    )(c_all, mod_w)


def _mod_bwd(c_all, mod_w, dm):
    nl, _, ws = mod_w.shape

    def body(c_ref, w_ref, d_ref, dw_ref, dc_ref):
        dw_ref[0] = _dot_tn(jax.nn.silu(c_ref[...]), d_ref[0])
        dc_ref[0] = _dot_nt(d_ref[0], w_ref[0])

    return pl.pallas_call(
        body, name="mod_bwd", grid=(nl,),
        in_specs=[_full_spec(c_all), pl.BlockSpec((1, D, ws), lambda i: (i, 0, 0)), pl.BlockSpec((1, 16, ws), lambda i: (i, 0, 0))],
        out_specs=[pl.BlockSpec((1, D, ws), lambda i: (i, 0, 0)), pl.BlockSpec((1, 16, D), lambda i: (i, 0, 0))],
        out_shape=[jax.ShapeDtypeStruct((nl, D, ws), F32), jax.ShapeDtypeStruct((nl, 16, D), F32)],
        compiler_params=_cp(("parallel",)),
    )(c_all, mod_w, dm)


def _sum_parts(name, parts):
    npart, r, c = parts.shape

    def body(p_ref, o_ref):
        g = p_ref[0].astype(F32)
        for s in range(1, npart):
            g = g + p_ref[s].astype(F32)
        o_ref[...] = g

    return pl.pallas_call(body, name=name, out_shape=jax.ShapeDtypeStruct((r, c), F32), compiler_params=_cp())(parts)


MESH = pl.DeviceIdType.MESH
ANY = pl.BlockSpec(memory_space=pl.ANY)
N_PEERS = NDEV - 1


def _mesh_pos():
    return lax.axis_index("x"), lax.axis_index("y"), lax.axis_index("c")


def _slot(px, py, pc):
    return 4 * px + 2 * py + pc


def _two_level_gather(x_refs, o_refs, send_sems, recv_sems, local_sems):
    x, y, c = _mesh_pos()
    me, sibling = (x, y, c), (x, y, 1 - c)
    chips = [(1 - x, y), (x, 1 - y), (1 - x, 1 - y)]
    n = len(x_refs)

    def copy(a, k, block, to, src=None):
        dst = o_refs[a].at[_slot(*block)]
        return pltpu.make_async_remote_copy(src_ref=dst if src is None else src, dst_ref=dst, send_sem=send_sems.at[a, k],
                                            recv_sem=recv_sems.at[a, k], device_id=to, device_id_type=MESH)

    mine = [pltpu.make_async_copy(x_refs[a], o_refs[a].at[_slot(*me)], local_sems.at[a]) for a in range(n)]
    for cp in mine:
        cp.start()
    first = []
    for a in range(n):
        first.append(copy(a, 0, me, sibling, src=x_refs[a]))
        first += [copy(a, 1 + j, me, (*chip, c), src=x_refs[a]) for j, chip in enumerate(chips)]
    for cp in first:
        cp.start()
    passed = []
    for j, chip in enumerate(chips):
        for a in range(n):
            copy(a, 1 + j, (*chip, c), me).wait_recv()
            fwd = copy(a, 4 + j, (*chip, c), sibling)
            fwd.start()
            passed.append(fwd)
    for a in range(n):
        copy(a, 0, sibling, me).wait_recv()
        for j, chip in enumerate(chips):
            copy(a, 4 + j, (*chip, 1 - c), me).wait_recv()
    for cp in first + passed:
        cp.wait_send()
    for cp in mine:
        cp.wait()


def _ag_small(name, x):
    r, c = x.shape

    def body(x_ref, o_ref, send_sems, recv_sems, local_sems):
        _two_level_gather([x_ref], [o_ref], send_sems, recv_sems, local_sems)

    return pl.pallas_call(
        body, name=name, out_shape=jax.ShapeDtypeStruct((NDEV, r, c), x.dtype),
        in_specs=[pl.BlockSpec(memory_space=pltpu.VMEM)], out_specs=pl.BlockSpec(memory_space=pltpu.VMEM),
        scratch_shapes=[pltpu.SemaphoreType.DMA((1, N_PEERS)), pltpu.SemaphoreType.DMA((1, N_PEERS)), pltpu.SemaphoreType.DMA((1,))],
        compiler_params=pltpu.CompilerParams(vmem_limit_bytes=VMEM_LIMIT),
    )(x)


def _ag_big(name, shards):
    n = len(shards)

    def body(*refs):
        _two_level_gather(refs[:n], refs[n:2 * n], *refs[2 * n:])

    return pl.pallas_call(
        body, name=name, out_shape=[jax.ShapeDtypeStruct((NDEV,) + s.shape, s.dtype) for s in shards],
        in_specs=[ANY] * n, out_specs=[ANY] * n,
        scratch_shapes=[pltpu.SemaphoreType.DMA((n, N_PEERS)), pltpu.SemaphoreType.DMA((n, N_PEERS)), pltpu.SemaphoreType.DMA((n,))],
    )(*shards)


HBM = pl.BlockSpec(memory_space=pltpu.HBM)
SEM = pl.BlockSpec(memory_space=pltpu.SEMAPHORE)
EFFECT = pltpu.SideEffectType.DATAFLOW_SIDE_EFFECTING


def _peers(x, y, c):
    return [(k - 1, ((1 - x) if k & 4 else x, (1 - y) if k & 2 else y, (1 - c) if k & 1 else c)) for k in range(1, NDEV)]


def _xchg_copy(src_refs, land_refs, send_sems, recv_sems, a, k, peer, me, scatter):
    src = src_refs[a].at[_slot(*peer)] if scatter else src_refs[a]
    return pltpu.make_async_remote_copy(src_ref=src, dst_ref=land_refs[a].at[me], send_sem=send_sems.at[a * N_PEERS + k],
                                        recv_sem=recv_sems.at[a * N_PEERS + k], device_id=peer, device_id_type=MESH)


def _xchg_start(name, srcs, lands, deps, scatter):
    n, nd = len(srcs), len(deps)

    def body(*refs):
        src_refs, land_refs = refs[:n], refs[n:2 * n]
        send_sems, recv_sems, token = refs[2 * n + nd], refs[2 * n + nd + 1], refs[-1]
        x, y, c = _mesh_pos()
        me = _slot(x, y, c)
        for k, peer in _peers(x, y, c):
            for a in range(n):
                _xchg_copy(src_refs, land_refs, send_sems, recv_sems, a, k, peer, me, scatter).start()
        token[...] = jnp.zeros_like(token)

    res = pl.pallas_call(
        body, name=name,
        out_shape=(pltpu.SemaphoreType.DMA((n * N_PEERS,)), pltpu.SemaphoreType.DMA((n * N_PEERS,)),
                   *[pltpu.HBM(s.shape, s.dtype) for s in srcs], *[pltpu.HBM(s.shape, s.dtype) for s in lands],
                   jax.ShapeDtypeStruct((8, 128), F32)),
        in_specs=[HBM] * (2 * n) + [ANY] * nd,
        out_specs=(SEM, SEM, *([HBM] * (2 * n)), pl.BlockSpec(memory_space=pltpu.VMEM)),
        input_output_aliases={i: 2 + i for i in range(2 * n)},
        compiler_params=pltpu.CompilerParams(has_side_effects=EFFECT),
    )(*[pltpu.with_memory_space_constraint(s, pltpu.HBM) for s in srcs],
      *[pltpu.with_memory_space_constraint(s, pltpu.HBM) for s in lands], *deps)
    return res[0], res[1], res[2:2 + n], res[2 + n:2 + 2 * n], res[-1]


def _xchg_wait(name, send_sems, recv_sems, srcs, lands, after, scatter):
    n = len(srcs)

    def body(*refs):
        src_refs, land_refs = refs[:n], refs[n:2 * n]
        s_sems, r_sems = refs[2 * n], refs[2 * n + 1]
        x, y, c = _mesh_pos()
        me = _slot(x, y, c)
        for k, peer in _peers(x, y, c):
            for a in range(n):
                cp = _xchg_copy(src_refs, land_refs, s_sems, r_sems, a, k, peer, me, scatter)
                cp.wait_send()
                cp.wait_recv()

    res = pl.pallas_call(
        body, name=name,
        out_shape=[pltpu.HBM(s.shape, s.dtype) for s in srcs] + [pltpu.HBM(s.shape, s.dtype) for s in lands],
        in_specs=[HBM] * (2 * n) + [SEM, SEM, ANY], out_specs=[HBM] * (2 * n),
        input_output_aliases={i: i for i in range(2 * n)},
        compiler_params=pltpu.CompilerParams(has_side_effects=EFFECT),
    )(*srcs, *lands, send_sems, recv_sems, after)
    return res[n:]


def _landing(name, src, me, scatter):
    r, c = src.shape[-2:]
    tr = r if r * c * src.dtype.itemsize <= (2 << 20) else _pick(r, (256, 128, 64, 32, 16))

    def body(me_ref, s_ref, o_ref):
        o_ref[...] = s_ref[...].reshape(o_ref.shape)

    src_spec = pl.BlockSpec((1, tr, c), lambda i, me_ref: (me_ref[0], i, 0)) if scatter else pl.BlockSpec((tr, c), lambda i, me_ref: (i, 0))
    return pl.pallas_call(
        body, name=name, out_shape=jax.ShapeDtypeStruct((NDEV, r, c), src.dtype),
        grid_spec=pltpu.PrefetchScalarGridSpec(
            num_scalar_prefetch=1, grid=(r // tr,), in_specs=[src_spec],
            out_specs=pl.BlockSpec((1, tr, c), lambda i, me_ref: (me_ref[0], i, 0))),
        compiler_params=_cp(("parallel",)),
    )(jnp.reshape(me, (1,)).astype(jnp.int32), src)


STAGES = ("l0_mixer", "l0_ffn", "l1_mixer", "l1_ffn")
STAGE_LAYOUT = {"l0_mixer": (AB_SEGS, AB_P), "l1_mixer": (SSD_SEGS, SSD_P_W)}


class _Exchange:
    def __init__(self, shards, me):
        self.shards, self.me = shards, me
        self.pending, self.pending_grads, self.recv = None, None, {}

    def _layout(self, stage):
        ws = self.shards[stage][0].shape[-1]
        return STAGE_LAYOUT.get(stage, (((0, NDEV * ws),), NDEV * ws)) + (ws,)

    def _start_gather(self, stage, deps):
        srcs = list(self.shards[stage])
        lands = [_landing("own_%s_%d" % (stage, a), s, self.me, False) for a, s in enumerate(srcs)]
        return _xchg_start("gather_start_" + stage, srcs, lands, deps, False)

    def get(self, stage, dep, thread):
        i = STAGES.index(stage)
        if i == 0:
            g_in, g_out = _ag_big("gather_" + stage, list(self.shards[stage]))
            deps = [g_out, dep]
        else:
            ss, rs, srcs, lands, _ = self.pending
            g_in, g_out = _xchg_wait("gather_wait_" + stage, ss, rs, srcs, lands, dep, False)
            self.pending, deps = None, [g_out]
        if i + 1 < len(STAGES):
            self.pending = self._start_gather(STAGES[i + 1], deps)
            thread = thread + self.pending[4][0, 0]
        segs, wp, _ = self._layout(stage)
        return _repack("repack_" + stage, g_in, segs, wp), g_out.reshape(-1, D), thread

    def put(self, stage, d_in, d_out, thread):
        segs, _, ws = self._layout(stage)
        parts = [_unpack("unpack_" + stage, d_in, segs, ws, MXU_DTYPE), d_out.reshape(NDEV, -1, D)]
        deps = [parts[0]]
        if self.pending_grads is not None:
            deps = [self.finish(parts[0])[0]]
        self.staged = (stage, parts)
        return thread if stage == STAGES[0] else thread + self.start_last(deps)[0, 0]

    def start_last(self, deps):
        stage, parts = self.staged
        lands = [_landing("own_grad_%s_%d" % (stage, a), p, self.me, True) for a, p in enumerate(parts)]
        self.pending_grads = (stage,) + _xchg_start("scatter_start_" + stage, parts, lands, deps, True)
        return self.pending_grads[5]

    def finish(self, after):
        stage, ss, rs, srcs, lands, _ = self.pending_grads
        self.recv[stage] = _xchg_wait("scatter_wait_" + stage, ss, rs, srcs, lands, after, True)
        self.pending_grads = None
        return self.recv[stage]


def _ffn_fwd(tag, h, w_in, w_out, nb, cb):
    pf = _mm(tag + "_ffn_in", h, w_in, "nn", MXU_DTYPE)
    (act,) = _rowwise(tag + "_swiglu", _fn_swiglu, nb, cb, [_row(pf)], [], [(D_FF, MXU_DTYPE)])
    return pf, act, _mm(tag + "_ffn_out", act, w_out, "nn", F32)


def _ffn_bwd(tag, h, pf, act, df, w_in, w_out, nb, cb):
    dw_out = _mm(tag + "_ffn_out_dw", act, df, "tn", MXU_DTYPE)
    dact = _mm(tag + "_ffn_out_dx", df, w_out, "nt", MXU_DTYPE)
    (dpf,), _ = _rowwise_vjp(tag + "_swiglu_bwd", _fn_swiglu, nb, cb, [_row(pf)], [], [_row(dact)], [(0, MXU_DTYPE, None)])
    dw_in = _mm(tag + "_ffn_in_dw", h, dpf, "tn", MXU_DTYPE)
    dh = _mm(tag + "_ffn_in_dx", dpf, w_in, "nt", MXU_DTYPE)
    return dw_out, dw_in, dh


def _local_step(x, ctx, target, mod, P, comm):
    T = x.shape[0]
    N = T + CTX
    nb, cb = N // TM, N // TM - 1
    R = T // GRID_W
    mod0, mod1 = mod[0], mod[1]
    ng = P["norm_g"]
    g00, g01, g10, g11 = ng[0, 0][None], ng[0, 1][None], ng[1, 0][None], ng[1, 1][None]
    pre = functools.partial(_fn_prenorm, a=0, b=1)
    rpre = functools.partial(_fn_resid_prenorm, gi=2, a=3, b=4)
    res5 = functools.partial(_fn_resid, gi=5)
    dirs = (("f", False), ("b", True))

    xc0 = jnp.concatenate([x, ctx], axis=0)
    w_ab_in, w_ab_out, g00 = comm.get("l0_mixer", mod, g00)
    (h0,) = _rowwise("l0_prenorm", pre, nb, cb, [_row(xc0)], [g00, mod0], [(D, MXU_DTYPE)])
    p0 = _mm("l0_in", h0, w_ab_in, "nn", F32)
    gla_rows = [(p0, 512, 0), (p0, 256, 8), (p0, 256, 9), (p0, 128, 20)]
    gla_blk = _multi_chunk(_gla_chunk, GLA_L, TM // GLA_L, len(gla_rows))
    gla_par = {d: [P["ab_gate_w"][int(r)], P["ab_gate_b"][int(r)][None]] for d, r in dirs}
    gla_state = (GLA_H * GLA_DV, GLA_H * GLA_DK)
    o, st0 = None, {}
    for d, rev in dirs:
        o, st0[d] = _scan_fwd("gla_fwd_" + d, gla_blk, TM, nb, cb, rev, gla_rows, gla_par[d], gla_state, GLA_H * GLA_DV, o)
    n128, cb128 = N // GMLP_L, T // GMLP_L
    mix_rows = [_row(o, tm=GMLP_L)] + [_row(p0, 512, j, tm=GMLP_L) for j in (1, 2, 3)]
    mix_par = [P["ab_gla_norm_g"], P["ab_vnorm_g"], P["ab_spatial_w"].reshape(GMLP_G * GMLP_L, GMLP_L), P["ab_spatial_b"].T]
    (cat0,) = _rowwise("l0_mix", _fn_mixpost, n128, cb128, mix_rows, mix_par, [(D, MXU_DTYPE)], tm=GMLP_L)
    y0 = _mm("l0_out", cat0, w_ab_out, "nn", F32)
    w_fi0, w_fo0, g01 = comm.get("l0_ffn", y0, g01)
    x1, h1 = _rowwise("l0_ffn_prenorm", rpre, nb, cb, [_row(xc0), _row(y0)], [g01, mod0, mod0], [(D, F32), (D, MXU_DTYPE)])
    pf0, act0, f0 = _ffn_fwd("l0", h1, w_fi0, w_fo0, nb, cb)
    (x2,) = _rowwise("l0_resid", res5, nb, cb, [_row(x1), _row(f0)], [mod0], [(D, F32)])
    x2p = _perm("to_col_major", x2, R, nb)

    w_ssd_in, w_ssd_out, g10 = comm.get("l1_mixer", x2p, g10)
    (h2,) = _rowwise("l1_prenorm", pre, nb, cb, [_row(x2p)], [g10, mod1], [(D, MXU_DTYPE)])
    p1 = _mm("l1_in", h2, w_ssd_in, "nn", F32)
    conv_w8 = jnp.concatenate([P["ssd_conv_w"], jnp.zeros((8 - SSD_K, 3 * CONV_W), F32)], axis=0)
    xbc = _conv("l1_conv", p1, conv_w8, P["ssd_conv_b"], nb, permuted_src=True, act=True, flip=False, out_dtype=F32)
    ssd_rows = [(xbc, SSD_INNER, 0), (xbc, 512, 4), (xbc, 512, 5), (p1, 128, 40)]
    ssd_blk = _multi_chunk(_ssd_chunk, SSD_L, TM // SSD_L, len(ssd_rows))
    ssd_par = {d: [P["ssd_dt_bias"][int(r)][None], P["ssd_a_log"][int(r)][None]] for d, r in dirs}
    ssd_state = (SSD_N, SSD_INNER)
    ys, st1 = None, {}
    for d, rev in dirs:
        ys, st1[d] = _scan_fwd("ssd_fwd_" + d, ssd_blk, TM, nb, cb, rev, ssd_rows, ssd_par[d], ssd_state, SSD_INNER, ys)
    fin_rows = [_row(ys), _row(xbc, SSD_INNER, 0), _row(p1, SSD_INNER, 1)]
    fin_par = [P["ssd_d"], P["ssd_norm_g"]]
    (yn,) = _rowwise("l1_finish", _fn_ssd_finish, nb, cb, fin_rows, fin_par, [(SSD_INNER, MXU_DTYPE)])
    y1 = _mm("l1_out", yn, w_ssd_out, "nn", F32)
    w_fi1, w_fo1, g11 = comm.get("l1_ffn", y1, g11)
    x3, h3 = _rowwise("l1_ffn_prenorm", rpre, nb, cb, [_row(x2p), _row(y1)], [g11, mod1, mod1], [(D, F32), (D, MXU_DTYPE)])
    pf1, act1, f1 = _ffn_fwd("l1", h3, w_fi1, w_fo1, nb, cb)
    loss, dx3, df1, dm1_j, d_final_g = _loss_head(x3, f1, target, mod1, P["final_norm_g"], nb, R)

    dP = {"final_norm_g": d_final_g}
    dwo1, dwi1, dh3 = _ffn_bwd("l1", h3, pf1, act1, df1, w_fi1, w_fo1, nb, cb)
    g11 = comm.put("l1_ffn", dwi1, dwo1, g11)
    (dx2p_a, dy1), (dg11, dm1_a, dm1_b) = _rowwise_vjp(
        "l1_ffn_prenorm_bwd", rpre, nb, cb, [_row(x2p), _row(y1)], [g11, mod1, mod1], [_row(dx3), _row(dh3)],
        [(0, F32, None), (1, MXU_DTYPE, None)])
    d_ssd_out = _mm("l1_out_dw", yn, dy1, "tn", MXU_DTYPE)
    dyn = _mm("l1_out_dx", dy1, w_ssd_out, "nt", MXU_DTYPE)
    (dys, dxs, dz), (dP["ssd_d"], dP["ssd_norm_g"]) = _rowwise_vjp(
        "l1_finish_bwd", _fn_ssd_finish, nb, cb, fin_rows, fin_par, [_row(dyn)],
        [(0, F32, None), (1, F32, None), (2, MXU_DTYPE, None)])
    dssd, ddtb, dalog = None, [], []
    for d, rev in dirs:
        dssd, (ddtb_, dalog_) = _scan_bwd("ssd_bwd_" + d, ssd_blk, TM, nb, cb, rev, ssd_rows, ssd_par[d], st1[d], dys, ssd_state,
                                          SSD_INNER, dssd)
        ddtb.append(ddtb_); dalog.append(dalog_)
    dx_s, db_s, dc_s, dtl = dssd
    dP["ssd_dt_bias"] = jnp.concatenate(ddtb, axis=0)
    dP["ssd_a_log"] = jnp.concatenate(dalog, axis=0)
    dacc, dcw8, dP["ssd_conv_b"] = _conv_bwd_pre("l1_conv_bwd", p1, conv_w8, P["ssd_conv_b"], ([dx_s, dxs], [db_s, dc_s]), nb)
    dP["ssd_conv_w"] = dcw8[:SSD_K]
    dpc = _conv("l1_conv_dx", dacc, conv_w8, jnp.zeros((1, 3 * CONV_W), F32), nb, permuted_src=False, act=False, flip=True,
                out_dtype=MXU_DTYPE)
    cat1 = functools.partial(_fn_concat, sums=(1, 1, 1, 1), pad=SSD_P_W - 5248)
    (dp1,) = _rowwise("l1_dp", cat1, nb, cb, [_row(dpc, SSD_INNER, 0), _row(dz), _row(dpc, 1024, 2), _row(dtl)],
                      [], [(SSD_P_W, MXU_DTYPE)])
    g10 = comm.put("l1_mixer", _mm("l1_in_dw", h2, dp1, "tn", F32), d_ssd_out, g10)
    dh2 = _mm("l1_in_dx", dp1, w_ssd_in, "nt", MXU_DTYPE)
    (dx2p,), (dg10, dm1_f) = _rowwise_vjp("l1_prenorm_bwd", pre, nb, cb, [_row(x2p)], [g10, mod1], [_row(dh2)],
                                          [(0, F32, _row(dx2p_a))])
    dx2 = _perm("to_row_major", dx2p, GRID_W, nb)

    (dx1_a, df0), (dm0_e,) = _rowwise_vjp("l0_resid_bwd", res5, nb, cb, [_row(x1), _row(f0)], [mod0], [_row(dx2)],
                                          [(0, F32, None), (1, MXU_DTYPE, None)])
    dwo0, dwi0, dh1 = _ffn_bwd("l0", h1, pf0, act0, df0, w_fi0, w_fo0, nb, cb)
    g01 = comm.put("l0_ffn", dwi0, dwo0, g01)
    (dxc0_a, dy0), (dg01, dm0_a, dm0_b) = _rowwise_vjp(
        "l0_ffn_prenorm_bwd", rpre, nb, cb, [_row(xc0), _row(y0)], [g01, mod0, mod0], [_row(dx1_a), _row(dh1)],
        [(0, F32, None), (1, MXU_DTYPE, None)])
    d_ab_out = _mm("l0_out_dw", cat0, dy0, "tn", MXU_DTYPE)
    dcat0 = _mm("l0_out_dx", dy0, w_ab_out, "nt", MXU_DTYPE)
    (do, dr, du, dgm), (dP["ab_gla_norm_g"], dP["ab_vnorm_g"], dsw, dsb_t) = _rowwise_vjp(
        "l0_mix_bwd", _fn_mixpost, n128, cb128, mix_rows, mix_par, [_row(dcat0, tm=GMLP_L)],
        [(0, F32, None), (1, MXU_DTYPE, None), (2, MXU_DTYPE, None), (3, MXU_DTYPE, None)], tm=GMLP_L)
    dP["ab_spatial_w"] = dsw.reshape(GMLP_G, GMLP_L, GMLP_L)
    dP["ab_spatial_b"] = dsb_t.T
    gl, dgw, dgb = None, [], []
    for d, rev in dirs:
        gl, (dgw_, dgb_) = _scan_bwd("gla_bwd_" + d, gla_blk, TM, nb, cb, rev, gla_rows, gla_par[d], st0[d], do,
                                     gla_state, GLA_H * GLA_DV, gl)
        dgw.append(dgw_[None]); dgb.append(dgb_)
    dP["ab_gate_w"] = jnp.concatenate(dgw, axis=0)
    dP["ab_gate_b"] = jnp.concatenate(dgb, axis=0)
    cat0f = functools.partial(_fn_concat, sums=(1,) * 7, pad=AB_P - 2688)
    (dp0,) = _rowwise("l0_dp", cat0f, nb, cb, [_row(gl[0]), _row(dr), _row(du), _row(dgm), _row(gl[1]), _row(gl[2]), _row(gl[3])],
                      [], [(AB_P, MXU_DTYPE)])
    g00 = comm.put("l0_mixer", _mm("l0_in_dw", h0, dp0, "tn", F32), d_ab_out, g00)
    dh0 = _mm("l0_in_dx", dp0, w_ab_in, "nt", MXU_DTYPE)
    (dxc0,), (dg00, dm0_s) = _rowwise_vjp("l0_prenorm_bwd", pre, nb, cb, [_row(xc0)], [g00, mod0], [_row(dh0)],
                                          [(0, F32, _row(dxc0_a))])
    dP["norm_g"] = jnp.concatenate([dg00, dg01, dg10, dg11], axis=0).reshape(2, 2, D)
    dmod = jnp.stack([dm0_s + dm0_a + dm0_b + dm0_e, dm1_f + dm1_a + dm1_b + dm1_j])
    return loss, dxc0[:T], dmod, dP


WEIGHTS = ("c_ctx", "mod_w", "mod_b", "norm_g", "ffn_w_in", "ffn_w_out", "ab_w_in", "ab_gate_w", "ab_gate_b", "ab_gla_norm_g",
           "ab_vnorm_g", "ab_spatial_w", "ab_spatial_b", "ab_w_out", "ssd_w_in", "ssd_conv_w", "ssd_conv_b", "ssd_dt_bias",
           "ssd_a_log", "ssd_d", "ssd_norm_g", "ssd_w_out", "final_norm_g")
SMALL_SHARDED = ("norm_g", "ab_gate_w", "ab_gate_b", "ssd_conv_w", "ssd_conv_b", "ssd_norm_g")
SMALL = ("c_ctx", "mod_b", "norm_g", "ab_gate_w", "ab_gate_b", "ab_gla_norm_g", "ab_vnorm_g", "ab_spatial_w", "ab_spatial_b",
         "ssd_conv_w", "ssd_conv_b", "ssd_dt_bias", "ssd_a_log", "ssd_d", "ssd_norm_g", "final_norm_g")
LANES = 1024


def _pack(arrs, rows_multiple=8):
    flat = jnp.concatenate([a.reshape(-1).astype(F32) for a in arrs])
    rows = -(-flat.shape[0] // LANES)
    rows = -(-rows // rows_multiple) * rows_multiple
    return jnp.pad(flat, (0, rows * LANES - flat.shape[0])).reshape(rows, LANES)


def _unpack_flat(buf, shapes):
    lead = buf.shape[:-2]
    flat = buf.reshape(lead + (-1,))
    out, o = [], 0
    for s in shapes:
        n = math.prod(s)
        out.append(flat[..., o:o + n].reshape(lead + tuple(s)))
        o += n
    return out


def _unshard(g):
    g = jnp.moveaxis(g, 0, -2)
    return g.reshape(g.shape[:-2] + (g.shape[-2] * g.shape[-1],))


def _my_shard(full, me, ws):
    return lax.dynamic_slice_in_dim(full, me * ws, ws, axis=full.ndim - 1)


def _silu_vjp(cvec, dsc):
    def body(c_ref, d_ref, o_ref):
        _, vjp = jax.vjp(jax.nn.silu, c_ref[...])
        o_ref[...] = vjp(d_ref[...])[0]

    return pl.pallas_call(body, name="c_ctx_bwd", out_shape=jax.ShapeDtypeStruct(cvec.shape, F32), compiler_params=_cp())(cvec, dsc)


def kernel(x, c, ctx, c_ctx, mod_w, mod_b, norm_g, ffn_w_in, ffn_w_out, ab_w_in, ab_gate_w, ab_gate_b, ab_gla_norm_g, ab_vnorm_g, ab_spatial_w, ab_spatial_b, ab_w_out, ssd_w_in, ssd_conv_w, ssd_conv_b, ssd_dt_bias, ssd_a_log, ssd_d, ssd_norm_g, ssd_w_out, final_norm_g, loss_target, m_c_ctx, m_mod_w, m_mod_b, m_norm_g, m_ffn_w_in, m_ffn_w_out, m_ab_w_in, m_ab_gate_w, m_ab_gate_b, m_ab_gla_norm_g, m_ab_vnorm_g, m_ab_spatial_w, m_ab_spatial_b, m_ab_w_out, m_ssd_w_in, m_ssd_conv_w, m_ssd_conv_b, m_ssd_dt_bias, m_ssd_a_log, m_ssd_d, m_ssd_norm_g, m_ssd_w_out, m_final_norm_g, v_c_ctx, v_mod_w, v_mod_b, v_norm_g, v_ffn_w_in, v_ffn_w_out, v_ab_w_in, v_ab_gate_w, v_ab_gate_b, v_ab_gla_norm_g, v_ab_vnorm_g, v_ab_spatial_w, v_ab_spatial_b, v_ab_w_out, v_ssd_w_in, v_ssd_conv_w, v_ssd_conv_b, v_ssd_dt_bias, v_ssd_a_log, v_ssd_d, v_ssd_norm_g, v_ssd_w_out, v_final_norm_g):
    a = dict(locals())
    me = _slot(*_mesh_pos())
    ws_mod = mod_w.shape[-1]

    fwd_small = [c] + [a[k] for k in SMALL_SHARDED]
    g_small = _ag_small("gather_small", _pack(fwd_small))
    parts = _unpack_flat(g_small, [t.shape for t in fwd_small])
    c_rows = parts[0].reshape(NDEV, D)
    full = {k: _unshard(p) for k, p in zip(SMALL_SHARDED, parts[1:])}
    c_all = jnp.concatenate([c_rows, c_ctx[None], jnp.zeros((7, D), F32)], axis=0)
    m_all = _ag_small("gather_mod", _mod_fwd(c_all, mod_w).reshape(2 * 16, ws_mod)).reshape(NDEV, 2, 16, ws_mod)
    m_mine = lax.dynamic_index_in_dim(m_all, me, axis=2, keepdims=False)
    mx = jnp.moveaxis(m_mine, 0, 1).reshape(2, N_MOD, D) + mod_b.reshape(2, N_MOD, D)
    mc = jnp.moveaxis(m_all[:, :, 8, :], 0, 1).reshape(2, N_MOD, D) + mod_b.reshape(2, N_MOD, D)
    pad2 = jnp.zeros((2, 2, D), F32)
    mod = jnp.concatenate([mx, pad2, mc, pad2], axis=1)

    big = {"l0_mixer": (ab_w_in[0], ab_w_out[0]), "l0_ffn": (ffn_w_in[0], ffn_w_out[0]),
           "l1_mixer": (ssd_w_in[0], ssd_w_out[0]), "l1_ffn": (ffn_w_in[1], ffn_w_out[1])}
    comm = _Exchange({k: tuple(w.astype(MXU_DTYPE) for w in v) for k, v in big.items()}, me)
    P = {
        "norm_g": full["norm_g"], "ab_gate_w": full["ab_gate_w"][0], "ab_gate_b": full["ab_gate_b"][0],
        "ab_gla_norm_g": ab_gla_norm_g, "ab_vnorm_g": ab_vnorm_g, "ab_spatial_w": ab_spatial_w[0], "ab_spatial_b": ab_spatial_b[0],
        "ssd_conv_w": full["ssd_conv_w"][0], "ssd_conv_b": full["ssd_conv_b"], "ssd_dt_bias": ssd_dt_bias[0],
        "ssd_a_log": ssd_a_log[0], "ssd_d": ssd_d, "ssd_norm_g": full["ssd_norm_g"], "final_norm_g": final_norm_g[None],
    }

    loss, grad_x, dmod, dP = _local_step(x[0], ctx[0], loss_target[0], mod, P, comm)

    dmx, dmc = dmod[:, 0:N_MOD].reshape(2, N_MOD * D), dmod[:, 8:8 + N_MOD].reshape(2, N_MOD * D)
    small_names = ("ab_gate_w", "ab_gate_b", "ab_gla_norm_g", "ab_vnorm_g", "ab_spatial_w", "ab_spatial_b", "norm_g", "ssd_conv_w",
                   "ssd_conv_b", "ssd_dt_bias", "ssd_a_log", "ssd_d", "ssd_norm_g", "final_norm_g")
    bwd_small = [dP[k] for k in small_names] + [dmc, dmx]
    shapes = [t.shape for t in bwd_small]
    g_bwd = _ag_small("gather_small_grads", _pack(bwd_small))
    summed = _unpack_flat(_sum_parts("sum_small_grads", g_bwd), shapes)
    gfull = dict(zip(small_names, summed[:-2]))
    dmc_sum, dmx_sum = summed[-2], summed[-1]
    dmx_all = _unpack_flat(g_bwd, shapes)[-1]
    dmx_sh = jnp.moveaxis(_my_shard(dmx_all, me, ws_mod), 0, 1)
    dm = jnp.concatenate([dmx_sh, _my_shard(dmc_sum, me, ws_mod)[:, None, :], jnp.zeros((2, 7, ws_mod), F32)], axis=1)
    d_mod_w, dsc = _mod_bwd(c_all, mod_w, dm)
    dsc_ctx = (dsc[0, 8] + dsc[1, 8])[None]
    dsc_all = _ag_small("gather_c_ctx_grad", jnp.concatenate([dsc_ctx, jnp.zeros((7, D), F32)], axis=0))
    d_c_ctx = _silu_vjp(c_ctx[None], _sum_parts("sum_c_ctx_grad", dsc_all)[0:1])[0]

    g_small_w = {
        "c_ctx": d_c_ctx, "mod_b": dmx_sum + dmc_sum, "norm_g": gfull["norm_g"], "ab_gate_w": gfull["ab_gate_w"][None],
        "ab_gate_b": gfull["ab_gate_b"][None], "ab_gla_norm_g": gfull["ab_gla_norm_g"], "ab_vnorm_g": gfull["ab_vnorm_g"],
        "ab_spatial_w": gfull["ab_spatial_w"][None], "ab_spatial_b": gfull["ab_spatial_b"][None], "ssd_conv_w": gfull["ssd_conv_w"][None],
        "ssd_conv_b": gfull["ssd_conv_b"], "ssd_dt_bias": gfull["ssd_dt_bias"][None], "ssd_a_log": gfull["ssd_a_log"][None],
        "ssd_d": gfull["ssd_d"], "ssd_norm_g": gfull["ssd_norm_g"], "final_norm_g": gfull["final_norm_g"][0],
    }
    for k in SMALL_SHARDED:
        g_small_w[k] = _my_shard(g_small_w[k], me, a[k].shape[-1])
    token = comm.start_last([d_c_ctx])
    res = _adam("adam_small", _pack([a[k] for k in SMALL]), [_pack([g_small_w[k] for k in SMALL])[None]],
                _pack([a["m_" + k] for k in SMALL]), _pack([a["v_" + k] for k in SMALL]), token)
    out = {k: vals for k, vals in zip(SMALL, zip(*[_unpack_flat(r, [a[k].shape for k in SMALL]) for r in res]))}

    def adam_big(name, w2d, parts, m2d, v2d, shape):
        return tuple(r.reshape(shape) for r in _adam(name, w2d, parts, m2d, v2d, token))

    def flat2(t):
        return t.reshape(-1, t.shape[-1])

    out["mod_w"] = adam_big("adam_mod_w", flat2(mod_w), [d_mod_w.reshape(1, -1, ws_mod)], flat2(m_mod_w), flat2(v_mod_w), mod_w.shape)

    for j, k in enumerate(("ffn_w_in", "ffn_w_out")):
        out[k] = adam_big("adam_" + k, flat2(a[k]), [comm.recv["l0_ffn"][j], comm.recv["l1_ffn"][j]], flat2(a["m_" + k]),
                          flat2(a["v_" + k]), a[k].shape)
    for j, k in enumerate(("ssd_w_in", "ssd_w_out")):
        out[k] = adam_big("adam_" + k, a[k][0], [comm.recv["l1_mixer"][j]], a["m_" + k][0], a["v_" + k][0], a[k].shape)
    recv_ab = comm.finish(out["ssd_w_out"][3])
    for j, k in enumerate(("ab_w_in", "ab_w_out")):
        out[k] = adam_big("adam_" + k, a[k][0], [recv_ab[j]], a["m_" + k][0], a["v_" + k][0], a[k].shape)

    loss_all = lax.psum(loss[0, 0], ("x", "y", "c"))
    return (loss_all, grad_x[None], *[out[k][0] for k in WEIGHTS], *[out[k][1] for k in WEIGHTS],
            *[out[k][2] for k in WEIGHTS], *[out[k][3] for k in WEIGHTS])
```

```python
import functools
import math

import jax
import jax.numpy as jnp
from jax import lax
from jax.experimental import pallas as pl
from jax.experimental.pallas import tpu as pltpu

F32 = jnp.float32
BF16 = jnp.bfloat16
MXU_DTYPE = jnp.bfloat16
HI = lax.Precision.HIGHEST

D = 1024
NDEV = 8
N_MOD = 6
EPS = 1e-6
GRID_W = 64
CTX = 256
TM = 256
D_FF = 2816
GLA_H, GLA_DK, GLA_DV, GLA_LR, GLA_TAU, GLA_L = 4, 64, 128, 16, 16.0, 64
GMLP_G, GMLP_C, GMLP_L = 4, 128, 128
SSD_H, SSD_P, SSD_G, SSD_N, SSD_L, SSD_K = 32, 64, 4, 128, 128, 5
SSD_INNER = SSD_H * SSD_P
AB_IN = 2592
SSD_IN = 5184
AB_SEGS = ((256, 768), (1056, 1568), (1568, 2080), (2080, 2592), (0, 256), (800, 1056), (768, 800))
AB_P = 2816
SSD_SEGS = ((0, 2048), (3136, 5184), (2048, 2560), (2560, 3072), (3072, 3136))
SSD_P_W = 5376
VMEM_LIMIT = 56 * 1024 * 1024

ADAM_LR, ADAM_B1, ADAM_B2, ADAM_EPS, ADAM_WD, ADAM_STEP = 0.001, 0.9, 0.999, 1e-08, 0.01, 10


def _cp(sem=None, **kw):
    return pltpu.CompilerParams(dimension_semantics=sem, vmem_limit_bytes=VMEM_LIMIT, **kw)


def _dot(a, b, dims=(((1,), (0,)), ((), ()))):
    return lax.dot_general(a.astype(MXU_DTYPE), b.astype(MXU_DTYPE), dims, preferred_element_type=F32)


def _dot_nt(a, b):
    return _dot(a, b, (((1,), (1,)), ((), ())))


def _dot_tn(a, b):
    return _dot(a, b, (((0,), (0,)), ((), ())))


def _dotx(a, b, dims=(((1,), (0,)), ((), ()))):
    return lax.dot_general(a, b, dims, precision=HI, preferred_element_type=F32)


def _rms(x):
    return x * lax.rsqrt(jnp.mean(x * x, axis=-1, keepdims=True) + EPS)


def _pick(n, prefs):
    for p in prefs:
        if n % p == 0:
            return p
    return n


def _row(arr, width=None, colblk=0, tm=TM, valid=None):
    width = arr.shape[1] if width is None else width
    if valid is None:
        return (arr, pl.BlockSpec((tm, width), lambda i, c=colblk: (i, c)), None)
    return (arr, pl.BlockSpec((tm, width), lambda i, c=colblk: (jnp.minimum(i, valid - 1), c)), valid)


def _load_row(ref, valid):
    v = ref[...].astype(F32)
    return v if valid is None else jnp.where(pl.program_id(0) < valid, v, 0.0)


def _full_spec(p):
    nd = p.ndim
    return pl.BlockSpec(p.shape, lambda i, nd=nd: (0,) * nd)


def _rowwise(name, fn, n_blocks, ctx_blk, rows, params, outs, tm=TM):
    nr, npar = len(rows), len(params)

    def body(*refs):
        t = (pl.program_id(0) >= ctx_blk).astype(F32)
        rv = [_load_row(r, row[2]) for r, row in zip(refs[:nr], rows)]
        pv = [p[...] for p in refs[nr:nr + npar]]
        res = fn(t, rv, pv)
        for o_ref, o in zip(refs[nr + npar:], res):
            o_ref[...] = o.astype(o_ref.dtype)

    return pl.pallas_call(
        body, name=name, grid=(n_blocks,),
        in_specs=[r[1] for r in rows] + [_full_spec(p) for p in params],
        out_specs=[pl.BlockSpec((tm, w), lambda i: (i, 0)) for w, _ in outs],
        out_shape=[jax.ShapeDtypeStruct((n_blocks * tm, w), dt) for w, dt in outs],
        compiler_params=_cp(("parallel",)),
    )(*[r[0] for r in rows], *params)


def _rowwise_vjp(name, fn, n_blocks, ctx_blk, rows, params, douts, row_grads, tm=TM):
    nr, npar, nd = len(rows), len(params), len(douts)
    adds = [a for _, _, a in row_grads if a is not None]
    na = len(adds)

    def body(*refs):
        i = pl.program_id(0)
        t = (i >= ctx_blk).astype(F32)
        rv = [_load_row(r, row[2]) for r, row in zip(refs[:nr], rows)]
        pv = [p[...] for p in refs[nr:nr + npar]]
        dv = [_load_row(r, row[2]) for r, row in zip(refs[nr + npar:nr + npar + nd], douts)]
        av = [_load_row(r, row[2]) for r, row in zip(refs[nr + npar + nd:nr + npar + nd + na], adds)]
        o_refs = refs[nr + npar + nd + na:]
        _, vjp = jax.vjp(lambda r, p: tuple(fn(t, r, p)), rv, pv)
        d_rows, d_params = vjp(tuple(dv))
        ai = 0
        for o_ref, (ri, _, addend) in zip(o_refs, row_grads):
            g = d_rows[ri]
            if addend is not None:
                g = g + av[ai]
                ai += 1
            o_ref[...] = g.astype(o_ref.dtype)
        p_refs = o_refs[len(row_grads):]

        @pl.when(i == 0)
        def _():
            for p_ref in p_refs:
                p_ref[...] = jnp.zeros_like(p_ref)

        for p_ref, g in zip(p_refs, d_params):
            p_ref[...] += g

    widths = [rows[ri][1].block_shape[1] for ri, _, _ in row_grads]
    res = pl.pallas_call(
        body, name=name, grid=(n_blocks,),
        in_specs=[r[1] for r in rows] + [_full_spec(p) for p in params] + [r[1] for r in douts] + [r[1] for r in adds],
        out_specs=[pl.BlockSpec((tm, w), lambda i: (i, 0)) for w in widths] + [_full_spec(p) for p in params],
        out_shape=[jax.ShapeDtypeStruct((n_blocks * tm, w), dt) for w, (_, dt, _) in zip(widths, row_grads)]
        + [jax.ShapeDtypeStruct(p.shape, F32) for p in params],
        compiler_params=_cp(("arbitrary",)),
    )(*[r[0] for r in rows], *params, *[r[0] for r in douts], *[r[0] for r in adds])
    return res[:len(row_grads)], res[len(row_grads):]


def _mm(name, a, b, mode, out_dtype):
    if mode == "nn":
        m, kk = a.shape
        n = b.shape[1]
    elif mode == "nt":
        m, kk = a.shape
        n = b.shape[0]
    else:
        kk, m = a.shape
        n = b.shape[1]
    if mode == "tn":
        tm = _pick(m, (1024, 1408, 512, 256, 128))
        tn = _pick(n, (768, 512, 256, 128))
        tk = kk
    else:
        tm = _pick(m, (1088, 1024, 768, 512, 384, 256, 128))
        tn = n if n <= 2816 else _pick(n, (1024, 768, 512, 256, 128))
        tk = kk if kk <= 2816 else _pick(kk, (2816, 1792, 1024, 768, 512, 256, 128))
    nk = kk // tk
    in_place = out_dtype == F32
    if mode == "nn":
        specs = [pl.BlockSpec((tm, tk), lambda i, j, k: (i, k)), pl.BlockSpec((tk, tn), lambda i, j, k: (k, j))]
        dims = (((1,), (0,)), ((), ()))
    elif mode == "nt":
        specs = [pl.BlockSpec((tm, tk), lambda i, j, k: (i, k)), pl.BlockSpec((tn, tk), lambda i, j, k: (j, k))]
        dims = (((1,), (1,)), ((), ()))
    else:
        specs = [pl.BlockSpec((tk, tm), lambda i, j, k: (k, i)), pl.BlockSpec((tk, tn), lambda i, j, k: (k, j))]
        dims = (((0,), (0,)), ((), ()))

    def body(a_ref, b_ref, o_ref, *scratch):
        part = lax.dot_general(a_ref[...].astype(MXU_DTYPE), b_ref[...].astype(MXU_DTYPE), dims, preferred_element_type=F32)
        if nk == 1:
            o_ref[...] = part.astype(o_ref.dtype)
        else:
            k = pl.program_id(2)
            acc = o_ref if in_place else scratch[0]

            @pl.when(k == 0)
            def _():
                acc[...] = part

            @pl.when(k > 0)
            def _():
                acc[...] += part

            if not in_place:
                @pl.when(k == nk - 1)
                def _():
                    o_ref[...] = acc[...].astype(o_ref.dtype)

    return pl.pallas_call(
        body, name=name, grid=(m // tm, n // tn, nk), in_specs=specs,
        out_specs=pl.BlockSpec((tm, tn), lambda i, j, k: (i, j)),
        out_shape=jax.ShapeDtypeStruct((m, n), out_dtype),
        scratch_shapes=[] if nk == 1 or in_place else [pltpu.VMEM((tm, tn), F32)],
        compiler_params=_cp(("parallel", "parallel", "arbitrary")),
    )(a, b)


def _sel_mod(modp, t):
    return modp[0:8] * (1.0 - t) + modp[8:16] * t


def _fn_prenorm(t, rows, params, *, a, b):
    (x,), (g, modp) = rows, params
    m = _sel_mod(modp, t)
    return ((_rms(x) * g) * (1.0 + m[b:b + 1]) + m[a:a + 1],)


def _fn_resid_prenorm(t, rows, params, *, gi, a, b):
    (x, y), (g, mod_a, mod_b) = rows, params
    ma, mb = _sel_mod(mod_a, t), _sel_mod(mod_b, t)
    xn = x + ma[gi:gi + 1] * y
    return xn, (_rms(xn) * g) * (1.0 + mb[b:b + 1]) + mb[a:a + 1]


def _fn_resid(t, rows, params, *, gi):
    (x, y), (mod_a,) = rows, params
    return (x + _sel_mod(mod_a, t)[gi:gi + 1] * y,)


def _fn_swiglu(t, rows, params):
    (pf,) = rows
    return (jax.nn.silu(pf[:, :D_FF]) * pf[:, D_FF:],)


def _fn_mixpost(t, rows, params):
    (o, r, u, g), (gla_g, vn_g, sw, sb_t) = rows, params
    a =jnp.concatenate([_rms(o[:, h * GLA_DV:(h + 1) * GLA_DV]) for h in range(GLA_H)], axis=1) * gla_g * jax.nn.silu(r)
    uu, vv = jax.nn.gelu(u), jax.nn.gelu(g)
    mu = jnp.mean(vv, axis=-1, keepdims=True)
    var = jnp.mean(jnp.square(vv - mu), axis=-1, keepdims=True)
    vn = ((vv - mu) * lax.rsqrt(var + EPS)) * vn_g
    s = jnp.concatenate(
        [_dot(sw[gi * GMLP_L:(gi + 1) * GMLP_L, :], vn[:, gi * GMLP_C:(gi + 1) * GMLP_C]) + sb_t[:, gi:gi + 1]
         for gi in range(GMLP_G)], axis=1)
    return (jnp.concatenate([a, uu * s], axis=1),)


def _head_expand():
    r = lax.broadcasted_iota(jnp.int32, (SSD_H, SSD_INNER), 0)
    c = lax.broadcasted_iota(jnp.int32, (SSD_H, SSD_INNER), 1)
    return (c // SSD_P == r).astype(F32)


def _fn_ssd_finish(t, rows, params):
    (y2, xs, z), (d_skip, norm_g) = rows, params
    d_full = _dotx(jnp.broadcast_to(d_skip, (8, SSD_H)), _head_expand())[0:1]
    y = (y2 + d_full * xs) * jax.nn.silu(z)
    gw = SSD_INNER // SSD_G
    return (jnp.concatenate([_rms(y[:, gi * gw:(gi + 1) * gw]) for gi in range(SSD_G)], axis=1) * norm_g,)


def _fn_concat(t, rows, params, *, sums, pad=0):
    out, i = [], 0
    for n in sums:
        acc = rows[i]
        for j in range(1, n):
            acc = acc + rows[i + j]
        out.append(acc)
        i += n
    if pad:
        out.append(jnp.zeros((out[0].shape[0], pad), F32))
    return (jnp.concatenate(out, axis=1),)


def _tri(n, rev):
    r = lax.broadcasted_iota(jnp.int32, (n, n), 0)
    c = lax.broadcasted_iota(jnp.int32, (n, n), 1)
    return (r <= c) if rev else (r >= c)


def _gla_chunk(S, v, k, q, tail, gw, gb, *, rev):
    L, H = GLA_L, GLA_H
    msk = _tri(L, rev)
    tri = msk.astype(F32)
    lr = tail[:, GLA_LR:2 * GLA_LR] if rev else tail[:, 0:GLA_LR]
    la = jax.nn.log_sigmoid(_dot(lr, gw) + gb) / GLA_TAU
    b = _dotx(tri, la)
    b_last = b[0:1] if rev else b[L - 1:L]
    kd = k * jnp.exp(b_last - b)
    qd = (q * GLA_DK ** -0.5) * jnp.exp(b)
    ki = k * jnp.exp(-b)

    def same_head(shape, rows_per_head, cols_per_head):
        r = lax.broadcasted_iota(jnp.int32, shape, 0) // rows_per_head
        c = lax.broadcasted_iota(jnp.int32, shape, 1) // cols_per_head
        return r == c

    k_blk = jnp.where(same_head((H * L, H * GLA_DK), L, GLA_DK), jnp.concatenate([ki] * H, axis=0), 0.0)
    v_blk = jnp.where(same_head((H * L, H * GLA_DV), L, GLA_DV), jnp.concatenate([v] * H, axis=0), 0.0)
    row = lax.broadcasted_iota(jnp.int32, (L, H * L), 0)
    src = lax.broadcasted_iota(jnp.int32, (L, H * L), 1) % L
    sc = jnp.where((row <= src) if rev else (row >= src), _dot_nt(qd, k_blk), 0.0)
    o = _dot_nt(qd, S) + _dot(sc, v_blk)
    s_new = S * jnp.exp(b_last) + jnp.where(same_head(S.shape, GLA_DV, GLA_DK), _dot_tn(v, kd), 0.0)
    return s_new, o


def _ssd_chunk(S, x, bm, cm, tail, dtb, alog, *, rev):
    L = SSD_L
    msk = _tri(L, rev)
    tri = msk.astype(F32)
    raw = tail[:, SSD_H:2 * SSD_H] if rev else tail[:, 0:SSD_H]
    dt = jax.nn.softplus(raw + dtb)
    dta = dt * (-jnp.exp(alog))
    acum = _dotx(tri, dta)
    a_last = acum[0:1] if rev else acum[L - 1:L]
    wst = dt * jnp.exp(a_last - acum)
    eac = jnp.exp(acum)
    tr = jnp.concatenate([acum, dt, wst, jnp.zeros((L, L - 3 * SSD_H), F32)], axis=1).T
    acum_t, dt_t, wst_t = tr[0:SSD_H], tr[SSD_H:2 * SSD_H], tr[2 * SSD_H:3 * SSD_H]
    decrow = jnp.exp(_dotx(jnp.broadcast_to(a_last, (8, SSD_H)), _head_expand())[0:1])
    lane = lax.broadcasted_iota(jnp.int32, (1, 2 * SSD_P), 1)
    m0 = (lane < SSD_P).astype(F32)
    m1 = 1.0 - m0
    pairs_per_group = SSD_H // SSD_G // 2
    y_parts, s_parts = [], []
    for g in range(SSD_G):
        ns = slice(g * SSD_N, (g + 1) * SSD_N)
        bg, cg = bm[:, ns], cm[:, ns]
        cb = _dot_nt(cg, bg)
        bgt = bg.T
        for jj in range(pairs_per_group):
            j = g * pairs_per_group + jj
            ls = slice(j * 2 * SSD_P, (j + 1) * 2 * SSD_P)
            xp, sp = x[:, ls], S[:, ls]
            xm = jnp.concatenate([xp * m0, xp * m1], axis=0)
            sm = jnp.concatenate([sp * m0, sp * m1], axis=0)
            lhs, bw = [], []
            for h in (2 * j, 2 * j + 1):
                seg = acum[:, h:h + 1] - acum_t[h:h + 1, :]
                lhs.append(cb * jnp.exp(jnp.where(msk, seg, -jnp.inf)) * dt_t[h:h + 1, :])
                bw.append(bgt * wst_t[h:h + 1, :])
            lhs += [cg * eac[:, h:h + 1] for h in (2 * j, 2 * j + 1)]
            y_parts.append(_dot(jnp.concatenate(lhs, axis=1), jnp.concatenate([xm, sm], axis=0)))
            s_parts.append(sp * decrow[:, ls] + _dot(jnp.concatenate(bw, axis=1), xm))
    return jnp.concatenate(s_parts, axis=1), jnp.concatenate(y_parts, axis=1)


def _multi_chunk(chunk_fn, L, subs, nr):
    def fn(S, *args, rev):
        rows, params = args[:nr], args[nr:]
        ys = [None] * subs
        for j in (range(subs - 1, -1, -1) if rev else range(subs)):
            S, ys[j] = chunk_fn(S, *[r[j * L:(j + 1) * L] for r in rows], *params, rev=rev)
        return S, jnp.concatenate(ys, axis=0)

    return fn


def _scan_order(n, nx, rev, backward):
    nc = n - nx

    def fwd(s):
        return (n - 1 - s) if rev else jnp.where(s < nc, s + nx, s - nc)

    return (lambda s: fwd(n - 1 - s)) if backward else fwd


def _scan_fwd(name, chunk_fn, L, n, nx, rev, rows, params, state_shape, out_w, addend=None):
    order = _scan_order(n, nx, rev, False)
    nr, npar = len(rows), len(params)
    adds = [] if addend is None else [addend]

    def body(*refs):
        s_scr = refs[-1]

        @pl.when(pl.program_id(0) == 0)
        def _():
            s_scr[...] = jnp.zeros_like(s_scr)

        s_in = s_scr[...]
        y_ref, st_ref = refs[nr + npar + len(adds)], refs[nr + npar + len(adds) + 1]
        st_ref[0] = s_in
        s_new, y = chunk_fn(s_in, *[r[...] for r in refs[:nr]], *[p[...] for p in refs[nr:nr + npar]], rev=rev)
        y_ref[...] = y + refs[nr + npar][...] if adds else y
        s_scr[...] = s_new

    return pl.pallas_call(
        body, name=name, grid=(n,),
        in_specs=[pl.BlockSpec((L, w), lambda s, c=c: (order(s), c)) for _, w, c in rows] + [_full_spec(p) for p in params]
        + [pl.BlockSpec((L, out_w), lambda s: (order(s), 0)) for _ in adds],
        out_specs=[pl.BlockSpec((L, out_w), lambda s: (order(s), 0)),
                   pl.BlockSpec((1,) + state_shape, lambda s: (order(s), 0, 0))],
        out_shape=[jax.ShapeDtypeStruct((n * L, out_w), F32), jax.ShapeDtypeStruct((n,) + state_shape, F32)],
        scratch_shapes=[pltpu.VMEM(state_shape, F32)],
        compiler_params=_cp(("arbitrary",)),
    )(*[a for a, _, _ in rows], *params, *adds)


def _scan_bwd(name, chunk_fn, L, n, nx, rev, rows, params, states, dy, state_shape, out_w, addends=None):
    order = _scan_order(n, nx, rev, True)
    dy_blocks = dy.shape[0] // L
    nr, npar = len(rows), len(params)
    adds = [] if addends is None else list(addends)

    def body(*refs):
        i = pl.program_id(0)
        ds_scr = refs[-1]
        rv = [r[...] for r in refs[:nr]]
        pv = [p[...] for p in refs[nr:nr + npar]]
        st_ref, dy_ref = refs[nr + npar], refs[nr + npar + 1]
        a_refs = refs[nr + npar + 2:nr + npar + 2 + len(adds)]
        o_refs = refs[nr + npar + 2 + len(adds):-1]
        p_refs = o_refs[nr:]

        @pl.when(i == 0)
        def _():
            ds_scr[...] = jnp.zeros_like(ds_scr)
            for p_ref in p_refs:
                p_ref[...] = jnp.zeros_like(p_ref)

        _, vjp = jax.vjp(functools.partial(chunk_fn, rev=rev), st_ref[0], *rv, *pv)
        dy_blk = jnp.where(order(i) < dy_blocks, dy_ref[...].astype(F32), 0.0)
        grads = vjp((ds_scr[...], dy_blk))
        ds_scr[...] = grads[0]
        for j, (o_ref, g) in enumerate(zip(o_refs[:nr], grads[1:1 + nr])):
            o_ref[...] = g + a_refs[j][...] if adds else g
        for p_ref, g in zip(p_refs, grads[1 + nr:]):
            p_ref[...] += g

    row_specs = [pl.BlockSpec((L, w), lambda s: (order(s), 0)) for _, w, _ in rows]
    res = pl.pallas_call(
        body, name=name, grid=(n,),
        in_specs=[pl.BlockSpec((L, w), lambda s, c=c: (order(s), c)) for _, w, c in rows] + [_full_spec(p) for p in params]
        + [pl.BlockSpec((1,) + state_shape, lambda s: (order(s), 0, 0)),
           pl.BlockSpec((L, out_w), lambda s: (jnp.minimum(order(s), dy_blocks - 1), 0))]
        + row_specs[:len(adds)],
        out_specs=row_specs + [_full_spec(p) for p in params],
        out_shape=[jax.ShapeDtypeStruct((n * L, w), F32) for _, w, _ in rows] + [jax.ShapeDtypeStruct(p.shape, F32) for p in params],
        scratch_shapes=[pltpu.VMEM(state_shape, F32)],
        compiler_params=_cp(("arbitrary",)),
    )(*[a for a, _, _ in rows], *params, states, dy, *adds)
    return res[:nr], res[nr:]


CONV_W = 1024
CONV_COLBLK = (0, 1, 4)


def _conv_specs(nb, src_blk):
    halo = TM // 8
    return [pl.BlockSpec((TM, CONV_W), lambda j, i: (i, src_blk(j))),
            pl.BlockSpec((8, CONV_W), lambda j, i: (jnp.maximum(i * halo - 1, 0), src_blk(j))),
            pl.BlockSpec((8, CONV_W), lambda j, i: (jnp.minimum(i * halo + halo, nb * halo - 1), src_blk(j)))]


def _conv_ext(i, nb, cur, prev, nxt):
    has_prev = jnp.logical_and(i > 0, i < nb - 1)
    has_next = i < nb - 2
    return jnp.concatenate([jnp.where(has_prev, prev, 0.0), cur, jnp.where(has_next, nxt, 0.0)], axis=0)


def _conv_taps(ext, w, flip):
    acc = None
    for j in range(SSD_K):
        wj = w[SSD_K - 1 - j:SSD_K - j, :] if flip else w[j:j + 1, :]
        term = wj * ext[6 + j:6 + j + TM, :]
        acc = term if acc is None else acc + term
    return acc


def _conv(name, src, w8, b1, nb, *, permuted_src, act, flip, out_dtype):
    src_blk = (lambda j: jnp.where(j == 2, CONV_COLBLK[2], j)) if permuted_src else (lambda j: j)

    def body(cur, prev, nxt, w_ref, b_ref, o_ref):
        ext = _conv_ext(pl.program_id(1), nb, cur[...].astype(F32), prev[...].astype(F32), nxt[...].astype(F32))
        acc = _conv_taps(ext, w_ref[...], flip)
        if act:
            acc = jax.nn.silu(acc + b_ref[...])
        o_ref[...] = acc.astype(o_ref.dtype)

    return pl.pallas_call(
        body, name=name, grid=(3, nb),
        in_specs=_conv_specs(nb, src_blk) + [pl.BlockSpec((8, CONV_W), lambda j, i: (0, j)), pl.BlockSpec((1, CONV_W), lambda j, i: (0, j))],
        out_specs=pl.BlockSpec((TM, CONV_W), lambda j, i: (i, j)),
        out_shape=jax.ShapeDtypeStruct((nb * TM, 3 * CONV_W), out_dtype),
        compiler_params=_cp(("parallel", "parallel")),
    )(src, src, src, w8, b1)


def _conv_bwd_pre(name, p1, w8, b1, dxbc_parts, nb):
    src_blk = lambda j: jnp.where(j == 2, CONV_COLBLK[2], j)
    xs_parts, bc_parts = dxbc_parts
    n_x, n_bc = len(xs_parts), len(bc_parts)
    x_blocks = [p.shape[0] // TM for p in xs_parts]

    def body(*refs):
        cur, prev, nxt, w_ref, b_ref = refs[:5]
        d_refs = refs[5:5 + n_x + n_bc]
        da_ref, dw_ref, db_ref = refs[5 + n_x + n_bc:]
        j, i = pl.program_id(0), pl.program_id(1)
        ext = _conv_ext(i, nb, cur[...], prev[...], nxt[...])
        acc = _conv_taps(ext, w_ref[...], False) + b_ref[...]
        dx = d_refs[0][...]
        for r, blocks in zip(d_refs[1:n_x], x_blocks[1:]):
            dx = dx + jnp.where(i < blocks, r[...], 0.0)
        dbc = jnp.concatenate([d_refs[n_x][...], d_refs[n_x + 1][...]], axis=1)
        dy = jnp.where(j == 2, dbc, dx)
        sg = jax.nn.sigmoid(acc)
        da = dy * (sg + acc * sg * (1.0 - sg))
        da_ref[...] = da

        @pl.when(i == 0)
        def _():
            dw_ref[...] = jnp.zeros_like(dw_ref)
            db_ref[...] = jnp.zeros_like(db_ref)

        rows = [jnp.sum(da * ext[6 + t:6 + t + TM, :], axis=0, keepdims=True) for t in range(SSD_K)]
        dw_ref[...] += jnp.concatenate(rows + [jnp.zeros((8 - SSD_K, CONV_W), F32)], axis=0)
        db_ref[...] += jnp.sum(da, axis=0, keepdims=True)

    x_specs = [pl.BlockSpec((TM, CONV_W), lambda j, i, b=b: (jnp.minimum(i, b - 1), jnp.minimum(j, 1))) for b in x_blocks]
    bc_specs = [pl.BlockSpec((TM, 512), lambda j, i: (i, 0)) for _ in bc_parts]
    return pl.pallas_call(
        body, name=name, grid=(3, nb),
        in_specs=_conv_specs(nb, src_blk) + [pl.BlockSpec((8, CONV_W), lambda j, i: (0, j)), pl.BlockSpec((1, CONV_W), lambda j, i: (0, j))]
        + x_specs + bc_specs,
        out_specs=[pl.BlockSpec((TM, CONV_W), lambda j, i: (i, j)), pl.BlockSpec((8, CONV_W), lambda j, i: (0, j)),
                   pl.BlockSpec((1, CONV_W), lambda j, i: (0, j))],
        out_shape=[jax.ShapeDtypeStruct((nb * TM, 3 * CONV_W), F32), jax.ShapeDtypeStruct((8, 3 * CONV_W), F32),
                   jax.ShapeDtypeStruct((1, 3 * CONV_W), F32)],
        compiler_params=_cp(("arbitrary", "arbitrary")),
    )(p1, p1, p1, w8, b1, *xs_parts, *bc_parts)


def _grid_block(a):
    nbv = TM // a
    blk_b = max(nbv, 8)
    return nbv, blk_b, blk_b // nbv


def _grid_spec(a, nb):
    _, blk_b, per = _grid_block(a)
    return pl.BlockSpec((a, blk_b, D), lambda i: (0, jnp.minimum(i, nb - 2) // per, 0))


def _grid_rows(v_ref, a, i):
    nbv, _, per = _grid_block(a)

    def pick(ph):
        return jnp.concatenate([v_ref[:, ph * nbv + t, :] for t in range(nbv)], axis=0)

    out = pick(0)
    for ph in range(1, per):
        out = jnp.where(i % per == ph, pick(ph), out)
    return out


def _perm(name, xc, a, nb):
    n = xc.shape[0]
    b = (n - CTX) // a
    view = xc.reshape(n // b, b, D)

    def body(v_ref, c_ref, o_ref):
        i = pl.program_id(0)

        @pl.when(i < nb - 1)
        def _():
            o_ref[...] = _grid_rows(v_ref, a, i)

        @pl.when(i == nb - 1)
        def _():
            o_ref[...] = c_ref[...]

    return pl.pallas_call(
        body, name=name, grid=(nb,),
        in_specs=[_grid_spec(a, nb), pl.BlockSpec((TM, D), lambda i: (nb - 1, 0))],
        out_specs=pl.BlockSpec((TM, D), lambda i: (i, 0)),
        out_shape=jax.ShapeDtypeStruct((n, D), xc.dtype),
        compiler_params=_cp(("parallel",)),
    )(view, xc)


def _loss_head(x, f, target, modp, g_final, rows_r):
    tview = target.reshape(rows_r, target.shape[0] // rows_r, D)
    nb = x.shape[0] // TM + 1

    def fn(x_, f_, tgt, modp_, g_):
        xn = x_ + _sel_mod(modp_, 0.0)[5:6] * f_
        err = _rms(xn) * g_ - tgt
        return 0.5 * jnp.sum(jnp.mean(err * err, axis=-1))

    def body(x_ref, f_ref, t_ref, m_ref, g_ref, l_ref, dx_ref, df_ref, dm_ref, dg_ref):
        i = pl.program_id(0)
        tgt = _grid_rows(t_ref, rows_r, i)
        l, vjp = jax.vjp(lambda a_, b_, c_, d_: fn(a_, b_, tgt, c_, d_), x_ref[...], f_ref[...], m_ref[...], g_ref[...])
        dx, df, dm, dg = vjp(jnp.ones((), F32))

        @pl.when(i == 0)
        def _():
            l_ref[...] = jnp.zeros_like(l_ref)
            dm_ref[...] = jnp.zeros_like(dm_ref)
            dg_ref[...] = jnp.zeros_like(dg_ref)

        l_ref[...] += jnp.reshape(l, (1, 1))
        dx_ref[...] = dx
        df_ref[...] = df.astype(df_ref.dtype)
        dm_ref[...] += dm
        dg_ref[...] += dg

    rowspec = pl.BlockSpec((TM, D), lambda i: (i, 0))
    return pl.pallas_call(
        body, name="loss_head", grid=(nb - 1,),
        in_specs=[rowspec, rowspec, _grid_spec(rows_r, nb), _full_spec(modp), _full_spec(g_final)],
        out_specs=[pl.BlockSpec((1, 1), lambda i: (0, 0)), rowspec, rowspec, _full_spec(modp), _full_spec(g_final)],
        out_shape=[jax.ShapeDtypeStruct((1, 1), F32), jax.ShapeDtypeStruct(x.shape, F32), jax.ShapeDtypeStruct(x.shape, MXU_DTYPE),
                   jax.ShapeDtypeStruct(modp.shape, F32), jax.ShapeDtypeStruct(g_final.shape, F32)],
        compiler_params=_cp(("arbitrary",)),
    )(x, f, tview, modp, g_final)


def _repack(name, shards, segs, wp):
    nd, kk, ws = shards.shape
    tr = 128
    used = sum(e - s for s, e in segs)

    def body(a_ref, o_ref):
        full = jnp.concatenate([a_ref[d].astype(F32) for d in range(nd)], axis=1)
        parts = [full[:, s:e] for s, e in segs]
        if wp > used:
            parts.append(jnp.zeros((tr, wp - used), F32))
        o_ref[...] = jnp.concatenate(parts, axis=1).astype(o_ref.dtype)

    return pl.pallas_call(
        body, name=name, grid=(kk // tr,),
        in_specs=[pl.BlockSpec((nd, tr, ws), lambda i: (0, i, 0))],
        out_specs=pl.BlockSpec((tr, wp), lambda i: (i, 0)),
        out_shape=jax.ShapeDtypeStruct((kk, wp), MXU_DTYPE),
        compiler_params=_cp(("parallel",)),
    )(shards)


def _unpack(name, dw, segs, ws, out_dtype):
    kk, wp = dw.shape
    tr = 128
    order = sorted(range(len(segs)), key=lambda i: segs[i][0])
    offs, o = [], 0
    for s, e in segs:
        offs.append(o)
        o += e - s

    def body(a_ref, o_ref):
        a = a_ref[...].astype(F32)
        full = jnp.concatenate([a[:, offs[i]:offs[i] + segs[i][1] - segs[i][0]] for i in order], axis=1)
        for d in range(NDEV):
            o_ref[d] = full[:, d * ws:(d + 1) * ws].astype(o_ref.dtype)

    return pl.pallas_call(
        body, name=name, grid=(kk // tr,),
        in_specs=[pl.BlockSpec((tr, wp), lambda i: (i, 0))],
        out_specs=pl.BlockSpec((NDEV, tr, ws), lambda i: (0, i, 0)),
        out_shape=jax.ShapeDtypeStruct((NDEV, kk, ws), out_dtype),
        compiler_params=_cp(("parallel",)),
    )(dw)


def _adam_math(w, g, m, v):
    m = ADAM_B1 * m + (1.0 - ADAM_B1) * g
    v = ADAM_B2 * v + (1.0 - ADAM_B2) * jnp.square(g)
    m_hat = m / (1.0 - ADAM_B1 ** ADAM_STEP)
    v_hat = v / (1.0 - ADAM_B2 ** ADAM_STEP)
    delta = -ADAM_LR * (m_hat / (jnp.sqrt(v_hat) + ADAM_EPS) + ADAM_WD * w)
    return delta, m, v


def _adam(name, w, parts, m, v, after):
    r, c = w.shape
    nsec, npart = len(parts), parts[0].shape[0]
    rs = r // nsec
    tr = _pick(rs, (256, 128, 64, 32, 16, 8)) if rs * c * 4 > (1 << 20) else rs
    tiles = rs // tr

    def body(w_ref, *refs):
        m_ref, v_ref, _, g_ref, d_ref, nm_ref, nv_ref = refs[nsec:]
        sec = pl.program_id(0) // tiles
        g = None
        for a, p_ref in enumerate(refs[:nsec]):
            ga = p_ref[0].astype(F32)
            for s in range(1, npart):
                ga = ga + p_ref[s].astype(F32)
            g = ga if g is None else jnp.where(sec == a, ga, g)
        delta, nm, nv = _adam_math(w_ref[...], g, m_ref[...], v_ref[...])
        g_ref[...], d_ref[...], nm_ref[...], nv_ref[...] = g, delta, nm, nv

    spec = pl.BlockSpec((tr, c), lambda i: (i, 0))
    part_specs = [pl.BlockSpec((npart, tr, c), lambda i, a=a: (0, jnp.clip(i - a * tiles, 0, tiles - 1), 0)) for a in range(nsec)]
    return pl.pallas_call(
        body, name=name, grid=(r // tr,),
        in_specs=[spec] + part_specs + [spec, spec, ANY],
        out_specs=[spec] * 4, out_shape=[jax.ShapeDtypeStruct((r, c), F32)] * 4,
        compiler_params=_cp(("parallel",)),
    )(w, *parts, m, v, after)


def _mod_fwd(c_all, mod_w):
    nl, _, ws = mod_w.shape

    def body(c_ref, w_ref, o_ref):
        o_ref[0] = _dot(jax.nn.silu(c_ref[...]), w_ref[0])

    return pl.pallas_call(
        body, name="mod_fwd", grid=(nl,),
        in_specs=[_full_spec(c_all), pl.BlockSpec((1, D, ws), lambda i: (i, 0, 0))],
        out_specs=pl.BlockSpec((1, 16, ws), lambda i: (i, 0, 0)),
        out_shape=jax.ShapeDtypeStruct((nl, 16, ws), F32),
        compiler_params=_cp(("parallel",)),
    )(c_all, mod_w)


def _mod_bwd(c_all, mod_w, dm):
    nl, _, ws = mod_w.shape

    def body(c_ref, w_ref, d_ref, dw_ref, dc_ref):
        dw_ref[0] = _dot_tn(jax.nn.silu(c_ref[...]), d_ref[0])
        dc_ref[0] = _dot_nt(d_ref[0], w_ref[0])

    return pl.pallas_call(
        body, name="mod_bwd", grid=(nl,),
        in_specs=[_full_spec(c_all), pl.BlockSpec((1, D, ws), lambda i: (i, 0, 0)), pl.BlockSpec((1, 16, ws), lambda i: (i, 0, 0))],
        out_specs=[pl.BlockSpec((1, D, ws), lambda i: (i, 0, 0)), pl.BlockSpec((1, 16, D), lambda i: (i, 0, 0))],
        out_shape=[jax.ShapeDtypeStruct((nl, D, ws), F32), jax.ShapeDtypeStruct((nl, 16, D), F32)],
        compiler_params=_cp(("parallel",)),
    )(c_all, mod_w, dm)


def _sum_parts(name, parts):
    npart, r, c = parts.shape

    def body(p_ref, o_ref):
        g = p_ref[0].astype(F32)
        for s in range(1, npart):
            g = g + p_ref[s].astype(F32)
        o_ref[...] = g

    return pl.pallas_call(body, name=name, out_shape=jax.ShapeDtypeStruct((r, c), F32), compiler_params=_cp())(parts)


MESH = pl.DeviceIdType.MESH
ANY = pl.BlockSpec(memory_space=pl.ANY)
N_PEERS = NDEV - 1


def _mesh_pos():
    return lax.axis_index("x"), lax.axis_index("y"), lax.axis_index("c")


def _slot(px, py, pc):
    return 4 * px + 2 * py + pc


def _two_level_gather(x_refs, o_refs, send_sems, recv_sems, local_sems):
    x, y, c = _mesh_pos()
    me, sibling = (x, y, c), (x, y, 1 - c)
    chips = [(1 - x, y), (x, 1 - y), (1 - x, 1 - y)]
    n = len(x_refs)

    def copy(a, k, block, to, src=None):
        dst = o_refs[a].at[_slot(*block)]
        return pltpu.make_async_remote_copy(src_ref=dst if src is None else src, dst_ref=dst, send_sem=send_sems.at[a, k],
                                            recv_sem=recv_sems.at[a, k], device_id=to, device_id_type=MESH)

    mine = [pltpu.make_async_copy(x_refs[a], o_refs[a].at[_slot(*me)], local_sems.at[a]) for a in range(n)]
    for cp in mine:
        cp.start()
    first = []
    for a in range(n):
        first.append(copy(a, 0, me, sibling, src=x_refs[a]))
        first += [copy(a, 1 + j, me, (*chip, c), src=x_refs[a]) for j, chip in enumerate(chips)]
    for cp in first:
        cp.start()
    passed = []
    for j, chip in enumerate(chips):
        for a in range(n):
            copy(a, 1 + j, (*chip, c), me).wait_recv()
            fwd = copy(a, 4 + j, (*chip, c), sibling)
            fwd.start()
            passed.append(fwd)
    for a in range(n):
        copy(a, 0, sibling, me).wait_recv()
        for j, chip in enumerate(chips):
            copy(a, 4 + j, (*chip, 1 - c), me).wait_recv()
    for cp in first + passed:
        cp.wait_send()
    for cp in mine:
        cp.wait()


def _ag_small(name, x):
    r, c = x.shape

    def body(x_ref, o_ref, send_sems, recv_sems, local_sems):
        _two_level_gather([x_ref], [o_ref], send_sems, recv_sems, local_sems)

    return pl.pallas_call(
        body, name=name, out_shape=jax.ShapeDtypeStruct((NDEV, r, c), x.dtype),
        in_specs=[pl.BlockSpec(memory_space=pltpu.VMEM)], out_specs=pl.BlockSpec(memory_space=pltpu.VMEM),
        scratch_shapes=[pltpu.SemaphoreType.DMA((1, N_PEERS)), pltpu.SemaphoreType.DMA((1, N_PEERS)), pltpu.SemaphoreType.DMA((1,))],
        compiler_params=pltpu.CompilerParams(vmem_limit_bytes=VMEM_LIMIT),
    )(x)


def _ag_big(name, shards):
    n = len(shards)

    def body(*refs):
        _two_level_gather(refs[:n], refs[n:2 * n], *refs[2 * n:])

    return pl.pallas_call(
        body, name=name, out_shape=[jax.ShapeDtypeStruct((NDEV,) + s.shape, s.dtype) for s in shards],
        in_specs=[ANY] * n, out_specs=[ANY] * n,
        scratch_shapes=[pltpu.SemaphoreType.DMA((n, N_PEERS)), pltpu.SemaphoreType.DMA((n, N_PEERS)), pltpu.SemaphoreType.DMA((n,))],
    )(*shards)


HBM = pl.BlockSpec(memory_space=pltpu.HBM)
SEM = pl.BlockSpec(memory_space=pltpu.SEMAPHORE)
EFFECT = pltpu.SideEffectType.DATAFLOW_SIDE_EFFECTING


def _peers(x, y, c):
    return [(k - 1, ((1 - x) if k & 4 else x, (1 - y) if k & 2 else y, (1 - c) if k & 1 else c)) for k in range(1, NDEV)]


def _xchg_copy(src_refs, land_refs, send_sems, recv_sems, a, k, peer, me, scatter):
    src = src_refs[a].at[_slot(*peer)] if scatter else src_refs[a]
    return pltpu.make_async_remote_copy(src_ref=src, dst_ref=land_refs[a].at[me], send_sem=send_sems.at[a * N_PEERS + k],
                                        recv_sem=recv_sems.at[a * N_PEERS + k], device_id=peer, device_id_type=MESH)


def _xchg_start(name, srcs, lands, deps, scatter):
    n, nd = len(srcs), len(deps)

    def body(*refs):
        src_refs, land_refs = refs[:n], refs[n:2 * n]
        send_sems, recv_sems, token = refs[2 * n + nd], refs[2 * n + nd + 1], refs[-1]
        x, y, c = _mesh_pos()
        me = _slot(x, y, c)
        for k, peer in _peers(x, y, c):
            for a in range(n):
                _xchg_copy(src_refs, land_refs, send_sems, recv_sems, a, k, peer, me, scatter).start()
        token[...] = jnp.zeros_like(token)

    res = pl.pallas_call(
        body, name=name,
        out_shape=(pltpu.SemaphoreType.DMA((n * N_PEERS,)), pltpu.SemaphoreType.DMA((n * N_PEERS,)),
                   *[pltpu.HBM(s.shape, s.dtype) for s in srcs], *[pltpu.HBM(s.shape, s.dtype) for s in lands],
                   jax.ShapeDtypeStruct((8, 128), F32)),
        in_specs=[HBM] * (2 * n) + [ANY] * nd,
        out_specs=(SEM, SEM, *([HBM] * (2 * n)), pl.BlockSpec(memory_space=pltpu.VMEM)),
        input_output_aliases={i: 2 + i for i in range(2 * n)},
        compiler_params=pltpu.CompilerParams(has_side_effects=EFFECT),
    )(*[pltpu.with_memory_space_constraint(s, pltpu.HBM) for s in srcs],
      *[pltpu.with_memory_space_constraint(s, pltpu.HBM) for s in lands], *deps)
    return res[0], res[1], res[2:2 + n], res[2 + n:2 + 2 * n], res[-1]


def _xchg_wait(name, send_sems, recv_sems, srcs, lands, after, scatter):
    n = len(srcs)

    def body(*refs):
        src_refs, land_refs = refs[:n], refs[n:2 * n]
        s_sems, r_sems = refs[2 * n], refs[2 * n + 1]
        x, y, c = _mesh_pos()
        me = _slot(x, y, c)
        for k, peer in _peers(x, y, c):
            for a in range(n):
                cp = _xchg_copy(src_refs, land_refs, s_sems, r_sems, a, k, peer, me, scatter)
                cp.wait_send()
                cp.wait_recv()

    res = pl.pallas_call(
        body, name=name,
        out_shape=[pltpu.HBM(s.shape, s.dtype) for s in srcs] + [pltpu.HBM(s.shape, s.dtype) for s in lands],
        in_specs=[HBM] * (2 * n) + [SEM, SEM, ANY], out_specs=[HBM] * (2 * n),
        input_output_aliases={i: i for i in range(2 * n)},
        compiler_params=pltpu.CompilerParams(has_side_effects=EFFECT),
    )(*srcs, *lands, send_sems, recv_sems, after)
    return res[n:]


def _landing(name, src, me, scatter):
    r, c = src.shape[-2:]
    tr = r if r * c * src.dtype.itemsize <= (2 << 20) else _pick(r, (256, 128, 64, 32, 16))

    def body(me_ref, s_ref, o_ref):
        o_ref[...] = s_ref[...].reshape(o_ref.shape)

    src_spec = pl.BlockSpec((1, tr, c), lambda i, me_ref: (me_ref[0], i, 0)) if scatter else pl.BlockSpec((tr, c), lambda i, me_ref: (i, 0))
    return pl.pallas_call(
        body, name=name, out_shape=jax.ShapeDtypeStruct((NDEV, r, c), src.dtype),
        grid_spec=pltpu.PrefetchScalarGridSpec(
            num_scalar_prefetch=1, grid=(r // tr,), in_specs=[src_spec],
            out_specs=pl.BlockSpec((1, tr, c), lambda i, me_ref: (me_ref[0], i, 0))),
        compiler_params=_cp(("parallel",)),
    )(jnp.reshape(me, (1,)).astype(jnp.int32), src)


STAGES = ("l0_mixer", "l0_ffn", "l1_mixer", "l1_ffn")
STAGE_LAYOUT = {"l0_mixer": (AB_SEGS, AB_P), "l1_mixer": (SSD_SEGS, SSD_P_W)}


class _Exchange:
    def __init__(self, shards, me):
        self.shards, self.me = shards, me
        self.pending, self.pending_grads, self.recv = None, None, {}

    def _layout(self, stage):
        ws = self.shards[stage][0].shape[-1]
        return STAGE_LAYOUT.get(stage, (((0, NDEV * ws),), NDEV * ws)) + (ws,)

    def _start_gather(self, stage, deps):
        srcs = list(self.shards[stage])
        lands = [_landing("own_%s_%d" % (stage, a), s, self.me, False) for a, s in enumerate(srcs)]
        return _xchg_start("gather_start_" + stage, srcs, lands, deps, False)

    def get(self, stage, dep, thread):
        i = STAGES.index(stage)
        if i == 0:
            g_in, g_out = _ag_big("gather_" + stage, list(self.shards[stage]))
            deps = [g_out, dep]
        else:
            ss, rs, srcs, lands, _ = self.pending
            g_in, g_out = _xchg_wait("gather_wait_" + stage, ss, rs, srcs, lands, dep, False)
            self.pending, deps = None, [g_out]
        if i + 1 < len(STAGES):
            self.pending = self._start_gather(STAGES[i + 1], deps)
            thread = thread + self.pending[4][0, 0]
        segs, wp, _ = self._layout(stage)
        return _repack("repack_" + stage, g_in, segs, wp), g_out.reshape(-1, D), thread

    def put(self, stage, d_in, d_out, thread):
        segs, _, ws = self._layout(stage)
        parts = [_unpack("unpack_" + stage, d_in, segs, ws, MXU_DTYPE), d_out.reshape(NDEV, -1, D)]
        deps = [parts[0]]
        if self.pending_grads is not None:
            deps = [self.finish(parts[0])[0]]
        self.staged = (stage, parts)
        return thread if stage == STAGES[0] else thread + self.start_last(deps)[0, 0]

    def start_last(self, deps):
        stage, parts = self.staged
        lands = [_landing("own_grad_%s_%d" % (stage, a), p, self.me, True) for a, p in enumerate(parts)]
        self.pending_grads = (stage,) + _xchg_start("scatter_start_" + stage, parts, lands, deps, True)
        return self.pending_grads[5]

    def finish(self, after):
        stage, ss, rs, srcs, lands, _ = self.pending_grads
        self.recv[stage] = _xchg_wait("scatter_wait_" + stage, ss, rs, srcs, lands, after, True)
        self.pending_grads = None
        return self.recv[stage]


def _ffn_fwd(tag, h, w_in, w_out, nb, cb):
    pf = _mm(tag + "_ffn_in", h, w_in, "nn", MXU_DTYPE)
    (act,) = _rowwise(tag + "_swiglu", _fn_swiglu, nb, cb, [_row(pf)], [], [(D_FF, MXU_DTYPE)])
    return pf, act, _mm(tag + "_ffn_out", act, w_out, "nn", F32)


def _ffn_bwd(tag, h, pf, act, df, w_in, w_out, nb, cb):
    dw_out = _mm(tag + "_ffn_out_dw", act, df, "tn", MXU_DTYPE)
    dact = _mm(tag + "_ffn_out_dx", df, w_out, "nt", MXU_DTYPE)
    (dpf,), _ = _rowwise_vjp(tag + "_swiglu_bwd", _fn_swiglu, nb, cb, [_row(pf)], [], [_row(dact)], [(0, MXU_DTYPE, None)])
    dw_in = _mm(tag + "_ffn_in_dw", h, dpf, "tn", MXU_DTYPE)
    dh = _mm(tag + "_ffn_in_dx", dpf, w_in, "nt", MXU_DTYPE)
    return dw_out, dw_in, dh


def _local_step(x, ctx, target, mod, P, comm):
    T = x.shape[0]
    N = T + CTX
    nb, cb = N // TM, N // TM - 1
    R = T // GRID_W
    mod0, mod1 = mod[0], mod[1]
    ng = P["norm_g"]
    g00, g01, g10, g11 = ng[0, 0][None], ng[0, 1][None], ng[1, 0][None], ng[1, 1][None]
    pre = functools.partial(_fn_prenorm, a=0, b=1)
    rpre = functools.partial(_fn_resid_prenorm, gi=2, a=3, b=4)
    res5 = functools.partial(_fn_resid, gi=5)
    dirs = (("f", False), ("b", True))

    xc0 = jnp.concatenate([x, ctx], axis=0)
    w_ab_in, w_ab_out, g00 = comm.get("l0_mixer", mod, g00)
    (h0,) = _rowwise("l0_prenorm", pre, nb, cb, [_row(xc0)], [g00, mod0], [(D, MXU_DTYPE)])
    p0 = _mm("l0_in", h0, w_ab_in, "nn", F32)
    gla_rows = [(p0, 512, 0), (p0, 256, 8), (p0, 256, 9), (p0, 128, 20)]
    gla_blk = _multi_chunk(_gla_chunk, GLA_L, TM // GLA_L, len(gla_rows))
    gla_par = {d: [P["ab_gate_w"][int(r)], P["ab_gate_b"][int(r)][None]] for d, r in dirs}
    gla_state = (GLA_H * GLA_DV, GLA_H * GLA_DK)
    o, st0 = None, {}
    for d, rev in dirs:
        o, st0[d] = _scan_fwd("gla_fwd_" + d, gla_blk, TM, nb, cb, rev, gla_rows, gla_par[d], gla_state, GLA_H * GLA_DV, o)
    n128, cb128 = N // GMLP_L, T // GMLP_L
    mix_rows = [_row(o, tm=GMLP_L)] + [_row(p0, 512, j, tm=GMLP_L) for j in (1, 2, 3)]
    mix_par = [P["ab_gla_norm_g"], P["ab_vnorm_g"], P["ab_spatial_w"].reshape(GMLP_G * GMLP_L, GMLP_L), P["ab_spatial_b"].T]
    (cat0,) = _rowwise("l0_mix", _fn_mixpost, n128, cb128, mix_rows, mix_par, [(D, MXU_DTYPE)], tm=GMLP_L)
    y0 = _mm("l0_out", cat0, w_ab_out, "nn", F32)
    w_fi0, w_fo0, g01 = comm.get("l0_ffn", y0, g01)
    x1, h1 = _rowwise("l0_ffn_prenorm", rpre, nb, cb, [_row(xc0), _row(y0)], [g01, mod0, mod0], [(D, F32), (D, MXU_DTYPE)])
    pf0, act0, f0 = _ffn_fwd("l0", h1, w_fi0, w_fo0, nb, cb)
    (x2,) = _rowwise("l0_resid", res5, nb, cb, [_row(x1), _row(f0)], [mod0], [(D, F32)])
    x2p = _perm("to_col_major", x2, R, nb)

    w_ssd_in, w_ssd_out, g10 = comm.get("l1_mixer", x2p, g10)
    (h2,) = _rowwise("l1_prenorm", pre, nb, cb, [_row(x2p)], [g10, mod1], [(D, MXU_DTYPE)])
    p1 = _mm("l1_in", h2, w_ssd_in, "nn", F32)
    conv_w8 = jnp.concatenate([P["ssd_conv_w"], jnp.zeros((8 - SSD_K, 3 * CONV_W), F32)], axis=0)
    xbc = _conv("l1_conv", p1, conv_w8, P["ssd_conv_b"], nb, permuted_src=True, act=True, flip=False, out_dtype=F32)
    ssd_rows = [(xbc, SSD_INNER, 0), (xbc, 512, 4), (xbc, 512, 5), (p1, 128, 40)]
    ssd_blk = _multi_chunk(_ssd_chunk, SSD_L, TM // SSD_L, len(ssd_rows))
    ssd_par = {d: [P["ssd_dt_bias"][int(r)][None], P["ssd_a_log"][int(r)][None]] for d, r in dirs}
    ssd_state = (SSD_N, SSD_INNER)
    ys, st1 = None, {}
    for d, rev in dirs:
        ys, st1[d] = _scan_fwd("ssd_fwd_" + d, ssd_blk, TM, nb, cb, rev, ssd_rows, ssd_par[d], ssd_state, SSD_INNER, ys)
    fin_rows = [_row(ys), _row(xbc, SSD_INNER, 0), _row(p1, SSD_INNER, 1)]
    fin_par = [P["ssd_d"], P["ssd_norm_g"]]
    (yn,) = _rowwise("l1_finish", _fn_ssd_finish, cb, cb, fin_rows, fin_par, [(SSD_INNER, MXU_DTYPE)])
    y1 = _mm("l1_out", yn, w_ssd_out, "nn", F32)
    w_fi1, w_fo1, g11 = comm.get("l1_ffn", y1, g11)
    x3, h3 = _rowwise("l1_ffn_prenorm", rpre, cb, cb, [_row(x2p), _row(y1)], [g11, mod1, mod1], [(D, F32), (D, MXU_DTYPE)])
    pf1, act1, f1 = _ffn_fwd("l1", h3, w_fi1, w_fo1, cb, cb)
    loss, dx3, df1, dm1_j, d_final_g = _loss_head(x3, f1, target, mod1, P["final_norm_g"], R)

    dP = {"final_norm_g": d_final_g}
    dwo1, dwi1, dh3 = _ffn_bwd("l1", h3, pf1, act1, df1, w_fi1, w_fo1, cb, cb)
    g11 = comm.put("l1_ffn", dwi1, dwo1, g11)
    (dx2p_a, dy1), (dg11, dm1_a, dm1_b) = _rowwise_vjp(
        "l1_ffn_prenorm_bwd", rpre, cb, cb, [_row(x2p), _row(y1)], [g11, mod1, mod1], [_row(dx3), _row(dh3)],
        [(0, F32, None), (1, MXU_DTYPE, None)])
    d_ssd_out = _mm("l1_out_dw", yn, dy1, "tn", MXU_DTYPE)
    dyn = _mm("l1_out_dx", dy1, w_ssd_out, "nt", MXU_DTYPE)
    (dys, dxs, dz), (dP["ssd_d"], dP["ssd_norm_g"]) = _rowwise_vjp(
        "l1_finish_bwd", _fn_ssd_finish, cb, cb, fin_rows, fin_par, [_row(dyn)],
        [(0, F32, None), (1, F32, None), (2, MXU_DTYPE, None)])
    dssd, ddtb, dalog = None, [], []
    for d, rev in dirs:
        dssd, (ddtb_, dalog_) = _scan_bwd("ssd_bwd_" + d, ssd_blk, TM, nb, cb, rev, ssd_rows, ssd_par[d], st1[d], dys, ssd_state,
                                          SSD_INNER, dssd)
        ddtb.append(ddtb_); dalog.append(dalog_)
    dx_s, db_s, dc_s, dtl = dssd
    dP["ssd_dt_bias"] = jnp.concatenate(ddtb, axis=0)
    dP["ssd_a_log"] = jnp.concatenate(dalog, axis=0)
    dacc, dcw8, dP["ssd_conv_b"] = _conv_bwd_pre("l1_conv_bwd", p1, conv_w8, P["ssd_conv_b"], ([dx_s, dxs], [db_s, dc_s]), nb)
    dP["ssd_conv_w"] = dcw8[:SSD_K]
    dpc = _conv("l1_conv_dx", dacc, conv_w8, jnp.zeros((1, 3 * CONV_W), F32), nb, permuted_src=False, act=False, flip=True,
                out_dtype=MXU_DTYPE)
    cat1 = functools.partial(_fn_concat, sums=(1, 1, 1, 1), pad=SSD_P_W - 5248)
    (dp1,) = _rowwise("l1_dp", cat1, nb, cb, [_row(dpc, SSD_INNER, 0), _row(dz, valid=cb), _row(dpc, 1024, 2), _row(dtl)],
                      [], [(SSD_P_W, MXU_DTYPE)])
    g10 = comm.put("l1_mixer", _mm("l1_in_dw", h2, dp1, "tn", F32), d_ssd_out, g10)
    dh2 = _mm("l1_in_dx", dp1, w_ssd_in, "nt", MXU_DTYPE)
    (dx2p,), (dg10, dm1_f) = _rowwise_vjp("l1_prenorm_bwd", pre, nb, cb, [_row(x2p)], [g10, mod1], [_row(dh2)],
                                          [(0, F32, _row(dx2p_a, valid=cb))])
    dx2 = _perm("to_row_major", dx2p, GRID_W, nb)

    (dx1_a, df0), (dm0_e,) = _rowwise_vjp("l0_resid_bwd", res5, nb, cb, [_row(x1), _row(f0)], [mod0], [_row(dx2)],
                                          [(0, F32, None), (1, MXU_DTYPE, None)])
    dwo0, dwi0, dh1 = _ffn_bwd("l0", h1, pf0, act0, df0, w_fi0, w_fo0, nb, cb)
    g01 = comm.put("l0_ffn", dwi0, dwo0, g01)
    (dxc0_a, dy0), (dg01, dm0_a, dm0_b) = _rowwise_vjp(
        "l0_ffn_prenorm_bwd", rpre, nb, cb, [_row(xc0), _row(y0)], [g01, mod0, mod0], [_row(dx1_a), _row(dh1)],
        [(0, F32, None), (1, MXU_DTYPE, None)])
    d_ab_out = _mm("l0_out_dw", cat0, dy0, "tn", MXU_DTYPE)
    dcat0 = _mm("l0_out_dx", dy0, w_ab_out, "nt", MXU_DTYPE)
    (do, dr, du, dgm), (dP["ab_gla_norm_g"], dP["ab_vnorm_g"], dsw, dsb_t) = _rowwise_vjp(
        "l0_mix_bwd", _fn_mixpost, n128, cb128, mix_rows, mix_par, [_row(dcat0, tm=GMLP_L)],
        [(0, F32, None), (1, MXU_DTYPE, None), (2, MXU_DTYPE, None), (3, MXU_DTYPE, None)], tm=GMLP_L)
    dP["ab_spatial_w"] = dsw.reshape(GMLP_G, GMLP_L, GMLP_L)
    dP["ab_spatial_b"] = dsb_t.T
    gl, dgw, dgb = None, [], []
    for d, rev in dirs:
        gl, (dgw_, dgb_) = _scan_bwd("gla_bwd_" + d, gla_blk, TM, nb, cb, rev, gla_rows, gla_par[d], st0[d], do,
                                     gla_state, GLA_H * GLA_DV, gl)
        dgw.append(dgw_[None]); dgb.append(dgb_)
    dP["ab_gate_w"] = jnp.concatenate(dgw, axis=0)
    dP["ab_gate_b"] = jnp.concatenate(dgb, axis=0)
    cat0f = functools.partial(_fn_concat, sums=(1,) * 7, pad=AB_P - 2688)
    (dp0,) = _rowwise("l0_dp", cat0f, nb, cb, [_row(gl[0]), _row(dr), _row(du), _row(dgm), _row(gl[1]), _row(gl[2]), _row(gl[3])],
                      [], [(AB_P, MXU_DTYPE)])
    g00 = comm.put("l0_mixer", _mm("l0_in_dw", h0, dp0, "tn", F32), d_ab_out, g00)
    dh0 = _mm("l0_in_dx", dp0, w_ab_in, "nt", MXU_DTYPE)
    (dxc0,), (dg00, dm0_s) = _rowwise_vjp("l0_prenorm_bwd", pre, nb, cb, [_row(xc0)], [g00, mod0], [_row(dh0)],
                                          [(0, F32, _row(dxc0_a))])
    dP["norm_g"] = jnp.concatenate([dg00, dg01, dg10, dg11], axis=0).reshape(2, 2, D)
    dmod = jnp.stack([dm0_s + dm0_a + dm0_b + dm0_e, dm1_f + dm1_a + dm1_b + dm1_j])
    return loss, dxc0[:T], dmod, dP


WEIGHTS = ("c_ctx", "mod_w", "mod_b", "norm_g", "ffn_w_in", "ffn_w_out", "ab_w_in", "ab_gate_w", "ab_gate_b", "ab_gla_norm_g",
           "ab_vnorm_g", "ab_spatial_w", "ab_spatial_b", "ab_w_out", "ssd_w_in", "ssd_conv_w", "ssd_conv_b", "ssd_dt_bias",
           "ssd_a_log", "ssd_d", "ssd_norm_g", "ssd_w_out", "final_norm_g")
SMALL_SHARDED = ("norm_g", "ab_gate_w", "ab_gate_b", "ssd_conv_w", "ssd_conv_b", "ssd_norm_g")
SMALL = ("c_ctx", "mod_b", "norm_g", "ab_gate_w", "ab_gate_b", "ab_gla_norm_g", "ab_vnorm_g", "ab_spatial_w", "ab_spatial_b",
         "ssd_conv_w", "ssd_conv_b", "ssd_dt_bias", "ssd_a_log", "ssd_d", "ssd_norm_g", "final_norm_g")
LANES = 1024


def _pack(arrs, rows_multiple=8):
    flat = jnp.concatenate([a.reshape(-1).astype(F32) for a in arrs])
    rows = -(-flat.shape[0] // LANES)
    rows = -(-rows // rows_multiple) * rows_multiple
    return jnp.pad(flat, (0, rows * LANES - flat.shape[0])).reshape(rows, LANES)


def _unpack_flat(buf, shapes):
    lead = buf.shape[:-2]
    flat = buf.reshape(lead + (-1,))
    out, o = [], 0
    for s in shapes:
        n = math.prod(s)
        out.append(flat[..., o:o + n].reshape(lead + tuple(s)))
        o += n
    return out


def _unshard(g):
    g = jnp.moveaxis(g, 0, -2)
    return g.reshape(g.shape[:-2] + (g.shape[-2] * g.shape[-1],))


def _my_shard(full, me, ws):
    return lax.dynamic_slice_in_dim(full, me * ws, ws, axis=full.ndim - 1)


def _silu_vjp(cvec, dsc):
    def body(c_ref, d_ref, o_ref):
        _, vjp = jax.vjp(jax.nn.silu, c_ref[...])
        o_ref[...] = vjp(d_ref[...])[0]

    return pl.pallas_call(body, name="c_ctx_bwd", out_shape=jax.ShapeDtypeStruct(cvec.shape, F32), compiler_params=_cp())(cvec, dsc)


def kernel(x, c, ctx, c_ctx, mod_w, mod_b, norm_g, ffn_w_in, ffn_w_out, ab_w_in, ab_gate_w, ab_gate_b, ab_gla_norm_g, ab_vnorm_g, ab_spatial_w, ab_spatial_b, ab_w_out, ssd_w_in, ssd_conv_w, ssd_conv_b, ssd_dt_bias, ssd_a_log, ssd_d, ssd_norm_g, ssd_w_out, final_norm_g, loss_target, m_c_ctx, m_mod_w, m_mod_b, m_norm_g, m_ffn_w_in, m_ffn_w_out, m_ab_w_in, m_ab_gate_w, m_ab_gate_b, m_ab_gla_norm_g, m_ab_vnorm_g, m_ab_spatial_w, m_ab_spatial_b, m_ab_w_out, m_ssd_w_in, m_ssd_conv_w, m_ssd_conv_b, m_ssd_dt_bias, m_ssd_a_log, m_ssd_d, m_ssd_norm_g, m_ssd_w_out, m_final_norm_g, v_c_ctx, v_mod_w, v_mod_b, v_norm_g, v_ffn_w_in, v_ffn_w_out, v_ab_w_in, v_ab_gate_w, v_ab_gate_b, v_ab_gla_norm_g, v_ab_vnorm_g, v_ab_spatial_w, v_ab_spatial_b, v_ab_w_out, v_ssd_w_in, v_ssd_conv_w, v_ssd_conv_b, v_ssd_dt_bias, v_ssd_a_log, v_ssd_d, v_ssd_norm_g, v_ssd_w_out, v_final_norm_g):
    a = dict(locals())
    me = _slot(*_mesh_pos())
    ws_mod = mod_w.shape[-1]

    fwd_small = [c] + [a[k] for k in SMALL_SHARDED]
    g_small = _ag_small("gather_small", _pack(fwd_small))
    parts = _unpack_flat(g_small, [t.shape for t in fwd_small])
    c_rows = parts[0].reshape(NDEV, D)
    full = {k: _unshard(p) for k, p in zip(SMALL_SHARDED, parts[1:])}
    c_all = jnp.concatenate([c_rows, c_ctx[None], jnp.zeros((7, D), F32)], axis=0)
    m_all = _ag_small("gather_mod", _mod_fwd(c_all, mod_w).reshape(2 * 16, ws_mod)).reshape(NDEV, 2, 16, ws_mod)
    m_mine = lax.dynamic_index_in_dim(m_all, me, axis=2, keepdims=False)
    mx = jnp.moveaxis(m_mine, 0, 1).reshape(2, N_MOD, D) + mod_b.reshape(2, N_MOD, D)
    mc = jnp.moveaxis(m_all[:, :, 8, :], 0, 1).reshape(2, N_MOD, D) + mod_b.reshape(2, N_MOD, D)
    pad2 = jnp.zeros((2, 2, D), F32)
    mod = jnp.concatenate([mx, pad2, mc, pad2], axis=1)

    big = {"l0_mixer": (ab_w_in[0], ab_w_out[0]), "l0_ffn": (ffn_w_in[0], ffn_w_out[0]),
           "l1_mixer": (ssd_w_in[0], ssd_w_out[0]), "l1_ffn": (ffn_w_in[1], ffn_w_out[1])}
    comm = _Exchange({k: tuple(w.astype(MXU_DTYPE) for w in v) for k, v in big.items()}, me)
    P = {
        "norm_g": full["norm_g"], "ab_gate_w": full["ab_gate_w"][0], "ab_gate_b": full["ab_gate_b"][0],
        "ab_gla_norm_g": ab_gla_norm_g, "ab_vnorm_g": ab_vnorm_g, "ab_spatial_w": ab_spatial_w[0], "ab_spatial_b": ab_spatial_b[0],
        "ssd_conv_w": full["ssd_conv_w"][0], "ssd_conv_b": full["ssd_conv_b"], "ssd_dt_bias": ssd_dt_bias[0],
        "ssd_a_log": ssd_a_log[0], "ssd_d": ssd_d, "ssd_norm_g": full["ssd_norm_g"], "final_norm_g": final_norm_g[None],
    }

    loss, grad_x, dmod, dP = _local_step(x[0], ctx[0], loss_target[0], mod, P, comm)

    dmx, dmc = dmod[:, 0:N_MOD].reshape(2, N_MOD * D), dmod[:, 8:8 + N_MOD].reshape(2, N_MOD * D)
    small_names = ("ab_gate_w", "ab_gate_b", "ab_gla_norm_g", "ab_vnorm_g", "ab_spatial_w", "ab_spatial_b", "norm_g", "ssd_conv_w",
                   "ssd_conv_b", "ssd_dt_bias", "ssd_a_log", "ssd_d", "ssd_norm_g", "final_norm_g")
    bwd_small = [dP[k] for k in small_names] + [dmc, dmx]
    shapes = [t.shape for t in bwd_small]
    g_bwd = _ag_small("gather_small_grads", _pack(bwd_small))
    summed = _unpack_flat(_sum_parts("sum_small_grads", g_bwd), shapes)
    gfull = dict(zip(small_names, summed[:-2]))
    dmc_sum, dmx_sum = summed[-2], summed[-1]
    dmx_all = _unpack_flat(g_bwd, shapes)[-1]
    dmx_sh = jnp.moveaxis(_my_shard(dmx_all, me, ws_mod), 0, 1)
    dm = jnp.concatenate([dmx_sh, _my_shard(dmc_sum, me, ws_mod)[:, None, :], jnp.zeros((2, 7, ws_mod), F32)], axis=1)
    d_mod_w, dsc = _mod_bwd(c_all, mod_w, dm)
    dsc_ctx = (dsc[0, 8] + dsc[1, 8])[None]
    dsc_all = _ag_small("gather_c_ctx_grad", jnp.concatenate([dsc_ctx, jnp.zeros((7, D), F32)], axis=0))
    d_c_ctx = _silu_vjp(c_ctx[None], _sum_parts("sum_c_ctx_grad", dsc_all)[0:1])[0]

    g_small_w = {
        "c_ctx": d_c_ctx, "mod_b": dmx_sum + dmc_sum, "norm_g": gfull["norm_g"], "ab_gate_w": gfull["ab_gate_w"][None],
        "ab_gate_b": gfull["ab_gate_b"][None], "ab_gla_norm_g": gfull["ab_gla_norm_g"], "ab_vnorm_g": gfull["ab_vnorm_g"],
        "ab_spatial_w": gfull["ab_spatial_w"][None], "ab_spatial_b": gfull["ab_spatial_b"][None], "ssd_conv_w": gfull["ssd_conv_w"][None],
        "ssd_conv_b": gfull["ssd_conv_b"], "ssd_dt_bias": gfull["ssd_dt_bias"][None], "ssd_a_log": gfull["ssd_a_log"][None],
        "ssd_d": gfull["ssd_d"], "ssd_norm_g": gfull["ssd_norm_g"], "final_norm_g": gfull["final_norm_g"][0],
    }
    for k in SMALL_SHARDED:
        g_small_w[k] = _my_shard(g_small_w[k], me, a[k].shape[-1])
    token = comm.start_last([d_c_ctx])
    res = _adam("adam_small", _pack([a[k] for k in SMALL]), [_pack([g_small_w[k] for k in SMALL])[None]],
                _pack([a["m_" + k] for k in SMALL]), _pack([a["v_" + k] for k in SMALL]), token)
    out = {k: vals for k, vals in zip(SMALL, zip(*[_unpack_flat(r, [a[k].shape for k in SMALL]) for r in res]))}

    def adam_big(name, w2d, parts, m2d, v2d, shape):
        return tuple(r.reshape(shape) for r in _adam(name, w2d, parts, m2d, v2d, token))

    def flat2(t):
        return t.reshape(-1, t.shape[-1])

    out["mod_w"] = adam_big("adam_mod_w", flat2(mod_w), [d_mod_w.reshape(1, -1, ws_mod)], flat2(m_mod_w), flat2(v_mod_w), mod_w.shape)

    for j, k in enumerate(("ffn_w_in", "ffn_w_out")):
        out[k] = adam_big("adam_" + k, flat2(a[k]), [comm.recv["l0_ffn"][j], comm.recv["l1_ffn"][j]], flat2(a["m_" + k]),
                          flat2(a["v_" + k]), a[k].shape)
    for j, k in enumerate(("ssd_w_in", "ssd_w_out")):
        out[k] = adam_big("adam_" + k, a[k][0], [comm.recv["l1_mixer"][j]], a["m_" + k][0], a["v_" + k][0], a[k].shape)
    recv_ab = comm.finish(out["ssd_w_out"][3])
    for j, k in enumerate(("ab_w_in", "ab_w_out")):
        out[k] = adam_big("adam_" + k, a[k][0], [recv_ab[j]], a["m_" + k][0], a["v_" + k][0], a[k].shape)

    loss_all = lax.psum(loss[0, 0], ("x", "y", "c"))
    return (loss_all, grad_x[None], *[out[k][0] for k in WEIGHTS], *[out[k][1] for k in WEIGHTS],
            *[out[k][2] for k in WEIGHTS], *[out[k][3] for k in WEIGHTS])
```

```python
import functools
import math

import jax
import jax.numpy as jnp
from jax import lax
from jax.experimental import pallas as pl
from jax.experimental.pallas import tpu as pltpu

F32 = jnp.float32
BF16 = jnp.bfloat16
MXU_DTYPE = jnp.bfloat16
HI = lax.Precision.HIGHEST

D = 1024
NDEV = 8
N_MOD = 6
EPS = 1e-6
GRID_W = 64
CTX = 256
TM = 256
D_FF = 2816
GLA_H, GLA_DK, GLA_DV, GLA_LR, GLA_TAU, GLA_L = 4, 64, 128, 16, 16.0, 64
GMLP_G, GMLP_C, GMLP_L = 4, 128, 128
SSD_H, SSD_P, SSD_G, SSD_N, SSD_L, SSD_K = 32, 64, 4, 128, 128, 5
SSD_INNER = SSD_H * SSD_P
AB_IN = 2592
SSD_IN = 5184
AB_SEGS = ((256, 768), (1056, 1568), (1568, 2080), (2080, 2592), (0, 256), (800, 1056), (768, 800))
AB_P = 2816
SSD_SEGS = ((0, 2048), (3136, 5184), (2048, 2560), (2560, 3072), (3072, 3136))
SSD_P_W = 5376
VMEM_LIMIT = 56 * 1024 * 1024

ADAM_LR, ADAM_B1, ADAM_B2, ADAM_EPS, ADAM_WD, ADAM_STEP = 0.001, 0.9, 0.999, 1e-08, 0.01, 10


def _cp(sem=None, **kw):
    return pltpu.CompilerParams(dimension_semantics=sem, vmem_limit_bytes=VMEM_LIMIT, **kw)


def _dot(a, b, dims=(((1,), (0,)), ((), ()))):
    return lax.dot_general(a.astype(MXU_DTYPE), b.astype(MXU_DTYPE), dims, preferred_element_type=F32)


def _dot_nt(a, b):
    return _dot(a, b, (((1,), (1,)), ((), ())))


def _dot_tn(a, b):
    return _dot(a, b, (((0,), (0,)), ((), ())))


def _dotx(a, b, dims=(((1,), (0,)), ((), ()))):
    return lax.dot_general(a, b, dims, precision=HI, preferred_element_type=F32)


def _rms(x):
    return x * lax.rsqrt(jnp.mean(x * x, axis=-1, keepdims=True) + EPS)


def _pick(n, prefs):
    for p in prefs:
        if n % p == 0:
            return p
    return n


def _row(arr, width=None, colblk=0, tm=TM, valid=None):
    width = arr.shape[1] if width is None else width
    if valid is None:
        return ([arr], [pl.BlockSpec((tm, width), lambda i, c=colblk: (i, c))], lambda r: r[...].astype(F32), width)
    spec = pl.BlockSpec((tm, width), lambda i, c=colblk: (jnp.minimum(i, valid - 1), c))
    return ([arr], [spec], lambda r: jnp.where(pl.program_id(0) < valid, r[...].astype(F32), 0.0), width)


def _row_grid(arr, a, nb):
    n = arr.shape[0]
    b = (n - CTX) // a

    def load(v_ref, c_ref):
        i = pl.program_id(0)
        return jnp.where(i == nb - 1, c_ref[...], _grid_rows(v_ref, a, i))

    return ([arr.reshape(n // b, b, D), arr], [_grid_spec(a, nb), pl.BlockSpec((TM, D), lambda i: (nb - 1, 0))], load, D)


def _operands(rows):
    return [a for r in rows for a in r[0]], [s for r in rows for s in r[1]]


def _load_rows(refs, rows):
    vals, k = [], 0
    for r in rows:
        vals.append(r[2](*refs[k:k + len(r[0])]))
        k += len(r[0])
    return vals


def _full_spec(p):
    nd = p.ndim
    return pl.BlockSpec(p.shape, lambda i, nd=nd: (0,) * nd)


def _rowwise(name, fn, n_blocks, ctx_blk, rows, params, outs, tm=TM):
    arrs, specs = _operands(rows)
    nr, npar = len(arrs), len(params)

    def body(*refs):
        t = (pl.program_id(0) >= ctx_blk).astype(F32)
        rv = _load_rows(refs[:nr], rows)
        pv = [p[...] for p in refs[nr:nr + npar]]
        res = fn(t, rv, pv)
        for o_ref, o in zip(refs[nr + npar:], res):
            o_ref[...] = o.astype(o_ref.dtype)

    return pl.pallas_call(
        body, name=name, grid=(n_blocks,),
        in_specs=specs + [_full_spec(p) for p in params],
        out_specs=[pl.BlockSpec((tm, w), lambda i: (i, 0)) for w, _ in outs],
        out_shape=[jax.ShapeDtypeStruct((n_blocks * tm, w), dt) for w, dt in outs],
        compiler_params=_cp(("parallel",)),
    )(*arrs, *params)


def _rowwise_vjp(name, fn, n_blocks, ctx_blk, rows, params, douts, row_grads, tm=TM):
    adds = [a for _, _, a in row_grads if a is not None]
    (r_arrs, r_specs), (d_arrs, d_specs), (a_arrs, a_specs) = _operands(rows), _operands(douts), _operands(adds)
    nr, npar, nd, na = len(r_arrs), len(params), len(d_arrs), len(a_arrs)

    def body(*refs):
        i = pl.program_id(0)
        t = (i >= ctx_blk).astype(F32)
        rv = _load_rows(refs[:nr], rows)
        pv = [p[...] for p in refs[nr:nr + npar]]
        dv = _load_rows(refs[nr + npar:nr + npar + nd], douts)
        av = _load_rows(refs[nr + npar + nd:nr + npar + nd + na], adds)
        o_refs = refs[nr + npar + nd + na:]
        _, vjp = jax.vjp(lambda r, p: tuple(fn(t, r, p)), rv, pv)
        d_rows, d_params = vjp(tuple(dv))
        ai = 0
        for o_ref, (ri, _, addend) in zip(o_refs, row_grads):
            g = d_rows[ri]
            if addend is not None:
                g = g + av[ai]
                ai += 1
            o_ref[...] = g.astype(o_ref.dtype)
        p_refs = o_refs[len(row_grads):]

        @pl.when(i == 0)
        def _():
            for p_ref in p_refs:
                p_ref[...] = jnp.zeros_like(p_ref)

        for p_ref, g in zip(p_refs, d_params):
            p_ref[...] += g

    widths = [rows[ri][3] for ri, _, _ in row_grads]
    res = pl.pallas_call(
        body, name=name, grid=(n_blocks,),
        in_specs=r_specs + [_full_spec(p) for p in params] + d_specs + a_specs,
        out_specs=[pl.BlockSpec((tm, w), lambda i: (i, 0)) for w in widths] + [_full_spec(p) for p in params],
        out_shape=[jax.ShapeDtypeStruct((n_blocks * tm, w), dt) for w, (_, dt, _) in zip(widths, row_grads)]
        + [jax.ShapeDtypeStruct(p.shape, F32) for p in params],
        compiler_params=_cp(("arbitrary",)),
    )(*r_arrs, *params, *d_arrs, *a_arrs)
    return res[:len(row_grads)], res[len(row_grads):]


def _mm(name, a, b, mode, out_dtype):
    if mode == "nn":
        m, kk = a.shape
        n = b.shape[1]
    elif mode == "nt":
        m, kk = a.shape
        n = b.shape[0]
    else:
        kk, m = a.shape
        n = b.shape[1]
    if mode == "tn":
        tm = _pick(m, (1024, 1408, 512, 256, 128))
        tn = _pick(n, (768, 512, 256, 128))
        tk = kk
    else:
        tm = _pick(m, (1088, 1024, 768, 512, 384, 256, 128))
        tn = n if n <= 2816 else _pick(n, (1024, 768, 512, 256, 128))
        tk = kk if kk <= 2816 else _pick(kk, (2816, 1792, 1024, 768, 512, 256, 128))
    nk = kk // tk
    in_place = out_dtype == F32
    if mode == "nn":
        specs = [pl.BlockSpec((tm, tk), lambda i, j, k: (i, k)), pl.BlockSpec((tk, tn), lambda i, j, k: (k, j))]
        dims = (((1,), (0,)), ((), ()))
    elif mode == "nt":
        specs = [pl.BlockSpec((tm, tk), lambda i, j, k: (i, k)), pl.BlockSpec((tn, tk), lambda i, j, k: (j, k))]
        dims = (((1,), (1,)), ((), ()))
    else:
        specs = [pl.BlockSpec((tk, tm), lambda i, j, k: (k, i)), pl.BlockSpec((tk, tn), lambda i, j, k: (k, j))]
        dims = (((0,), (0,)), ((), ()))

    def body(a_ref, b_ref, o_ref, *scratch):
        part = lax.dot_general(a_ref[...].astype(MXU_DTYPE), b_ref[...].astype(MXU_DTYPE), dims, preferred_element_type=F32)
        if nk == 1:
            o_ref[...] = part.astype(o_ref.dtype)
        else:
            k = pl.program_id(2)
            acc = o_ref if in_place else scratch[0]

            @pl.when(k == 0)
            def _():
                acc[...] = part

            @pl.when(k > 0)
            def _():
                acc[...] += part

            if not in_place:
                @pl.when(k == nk - 1)
                def _():
                    o_ref[...] = acc[...].astype(o_ref.dtype)

    return pl.pallas_call(
        body, name=name, grid=(m // tm, n // tn, nk), in_specs=specs,
        out_specs=pl.BlockSpec((tm, tn), lambda i, j, k: (i, j)),
        out_shape=jax.ShapeDtypeStruct((m, n), out_dtype),
        scratch_shapes=[] if nk == 1 or in_place else [pltpu.VMEM((tm, tn), F32)],
        compiler_params=_cp(("parallel", "parallel", "arbitrary")),
    )(a, b)


def _sel_mod(modp, t):
    return modp[0:8] * (1.0 - t) + modp[8:16] * t


def _fn_prenorm(t, rows, params, *, a, b):
    (x,), (g, modp) = rows, params
    m = _sel_mod(modp, t)
    return ((_rms(x) * g) * (1.0 + m[b:b + 1]) + m[a:a + 1],)


def _fn_resid_prenorm(t, rows, params, *, gi, a, b):
    (x, y), (g, mod_a, mod_b) = rows, params
    ma, mb = _sel_mod(mod_a, t), _sel_mod(mod_b, t)
    xn = x + ma[gi:gi + 1] * y
    return xn, (_rms(xn) * g) * (1.0 + mb[b:b + 1]) + mb[a:a + 1]


def _fn_resid(t, rows, params, *, gi):
    (x, y), (mod_a,) = rows, params
    return (x + _sel_mod(mod_a, t)[gi:gi + 1] * y,)


def _fn_swiglu(t, rows, params):
    (pf,) = rows
    return (jax.nn.silu(pf[:, :D_FF]) * pf[:, D_FF:],)


def _fn_mixpost(t, rows, params):
    (o, r, u, g), (gla_g, vn_g, sw, sb_t) = rows, params
    a =jnp.concatenate([_rms(o[:, h * GLA_DV:(h + 1) * GLA_DV]) for h in range(GLA_H)], axis=1) * gla_g * jax.nn.silu(r)
    uu, vv = jax.nn.gelu(u), jax.nn.gelu(g)
    mu = jnp.mean(vv, axis=-1, keepdims=True)
    var = jnp.mean(jnp.square(vv - mu), axis=-1, keepdims=True)
    vn = ((vv - mu) * lax.rsqrt(var + EPS)) * vn_g
    s = jnp.concatenate(
        [_dot(sw[gi * GMLP_L:(gi + 1) * GMLP_L, :], vn[:, gi * GMLP_C:(gi + 1) * GMLP_C]) + sb_t[:, gi:gi + 1]
         for gi in range(GMLP_G)], axis=1)
    return (jnp.concatenate([a, uu * s], axis=1),)


def _head_expand():
    r = lax.broadcasted_iota(jnp.int32, (SSD_H, SSD_INNER), 0)
    c = lax.broadcasted_iota(jnp.int32, (SSD_H, SSD_INNER), 1)
    return (c // SSD_P == r).astype(F32)


def _fn_ssd_finish(t, rows, params):
    (y2, xs, z), (d_skip, norm_g) = rows, params
    d_full = _dotx(jnp.broadcast_to(d_skip, (8, SSD_H)), _head_expand())[0:1]
    y = (y2 + d_full * xs) * jax.nn.silu(z)
    gw = SSD_INNER // SSD_G
    return (jnp.concatenate([_rms(y[:, gi * gw:(gi + 1) * gw]) for gi in range(SSD_G)], axis=1) * norm_g,)


def _fn_concat(t, rows, params, *, sums, pad=0):
    out, i = [], 0
    for n in sums:
        acc = rows[i]
        for j in range(1, n):
            acc = acc + rows[i + j]
        out.append(acc)
        i += n
    if pad:
        out.append(jnp.zeros((out[0].shape[0], pad), F32))
    return (jnp.concatenate(out, axis=1),)


def _tri(n, rev):
    r = lax.broadcasted_iota(jnp.int32, (n, n), 0)
    c = lax.broadcasted_iota(jnp.int32, (n, n), 1)
    return (r <= c) if rev else (r >= c)


def _gla_chunk(S, v, k, q, tail, gw, gb, *, rev):
    L, H = GLA_L, GLA_H
    msk = _tri(L, rev)
    tri = msk.astype(F32)
    lr = tail[:, GLA_LR:2 * GLA_LR] if rev else tail[:, 0:GLA_LR]
    la = jax.nn.log_sigmoid(_dot(lr, gw) + gb) / GLA_TAU
    b = _dotx(tri, la)
    b_last = b[0:1] if rev else b[L - 1:L]
    kd = k * jnp.exp(b_last - b)
    qd = (q * GLA_DK ** -0.5) * jnp.exp(b)
    ki = k * jnp.exp(-b)

    def same_head(shape, rows_per_head, cols_per_head):
        r = lax.broadcasted_iota(jnp.int32, shape, 0) // rows_per_head
        c = lax.broadcasted_iota(jnp.int32, shape, 1) // cols_per_head
        return r == c

    k_blk = jnp.where(same_head((H * L, H * GLA_DK), L, GLA_DK), jnp.concatenate([ki] * H, axis=0), 0.0)
    v_blk = jnp.where(same_head((H * L, H * GLA_DV), L, GLA_DV), jnp.concatenate([v] * H, axis=0), 0.0)
    row = lax.broadcasted_iota(jnp.int32, (L, H * L), 0)
    src = lax.broadcasted_iota(jnp.int32, (L, H * L), 1) % L
    sc = jnp.where((row <= src) if rev else (row >= src), _dot_nt(qd, k_blk), 0.0)
    o = _dot_nt(qd, S) + _dot(sc, v_blk)
    s_new = S * jnp.exp(b_last) + jnp.where(same_head(S.shape, GLA_DV, GLA_DK), _dot_tn(v, kd), 0.0)
    return s_new, o


def _ssd_chunk(S, x, bm, cm, tail, dtb, alog, *, rev):
    L = SSD_L
    msk = _tri(L, rev)
    tri = msk.astype(F32)
    raw = tail[:, SSD_H:2 * SSD_H] if rev else tail[:, 0:SSD_H]
    dt = jax.nn.softplus(raw + dtb)
    dta = dt * (-jnp.exp(alog))
    acum = _dotx(tri, dta)
    a_last = acum[0:1] if rev else acum[L - 1:L]
    wst = dt * jnp.exp(a_last - acum)
    eac = jnp.exp(acum)
    tr = jnp.concatenate([acum, dt, wst, jnp.zeros((L, L - 3 * SSD_H), F32)], axis=1).T
    acum_t, dt_t, wst_t = tr[0:SSD_H], tr[SSD_H:2 * SSD_H], tr[2 * SSD_H:3 * SSD_H]
    decrow = jnp.exp(_dotx(jnp.broadcast_to(a_last, (8, SSD_H)), _head_expand())[0:1])
    lane = lax.broadcasted_iota(jnp.int32, (1, 2 * SSD_P), 1)
    m0 = (lane < SSD_P).astype(F32)
    m1 = 1.0 - m0
    pairs_per_group = SSD_H // SSD_G // 2
    y_parts, s_parts = [], []
    for g in range(SSD_G):
        ns = slice(g * SSD_N, (g + 1) * SSD_N)
        bg, cg = bm[:, ns], cm[:, ns]
        cb = _dot_nt(cg, bg)
        bgt = bg.T
        for jj in range(pairs_per_group):
            j = g * pairs_per_group + jj
            ls = slice(j * 2 * SSD_P, (j + 1) * 2 * SSD_P)
            xp, sp = x[:, ls], S[:, ls]
            xm = jnp.concatenate([xp * m0, xp * m1], axis=0)
            sm = jnp.concatenate([sp * m0, sp * m1], axis=0)
            lhs, bw = [], []
            for h in (2 * j, 2 * j + 1):
                seg = acum[:, h:h + 1] - acum_t[h:h + 1, :]
                lhs.append(cb * jnp.exp(jnp.where(msk, seg, -jnp.inf)) * dt_t[h:h + 1, :])
                bw.append(bgt * wst_t[h:h + 1, :])
            lhs += [cg * eac[:, h:h + 1] for h in (2 * j, 2 * j + 1)]
            y_parts.append(_dot(jnp.concatenate(lhs, axis=1), jnp.concatenate([xm, sm], axis=0)))
            s_parts.append(sp * decrow[:, ls] + _dot(jnp.concatenate(bw, axis=1), xm))
    return jnp.concatenate(s_parts, axis=1), jnp.concatenate(y_parts, axis=1)


def _multi_chunk(chunk_fn, L, subs, nr):
    def fn(S, *args, rev):
        rows, params = args[:nr], args[nr:]
        ys = [None] * subs
        for j in (range(subs - 1, -1, -1) if rev else range(subs)):
            S, ys[j] = chunk_fn(S, *[r[j * L:(j + 1) * L] for r in rows], *params, rev=rev)
        return S, jnp.concatenate(ys, axis=0)

    return fn


def _scan_order(n, nx, rev, backward):
    nc = n - nx

    def fwd(s):
        return (n - 1 - s) if rev else jnp.where(s < nc, s + nx, s - nc)

    return (lambda s: fwd(n - 1 - s)) if backward else fwd


def _scan_fwd(name, chunk_fn, L, n, nx, rev, rows, params, state_shape, out_w, addend=None):
    order = _scan_order(n, nx, rev, False)
    nr, npar = len(rows), len(params)
    adds = [] if addend is None else [addend]

    def body(*refs):
        s_scr = refs[-1]

        @pl.when(pl.program_id(0) == 0)
        def _():
            s_scr[...] = jnp.zeros_like(s_scr)

        s_in = s_scr[...]
        y_ref, st_ref = refs[nr + npar + len(adds)], refs[nr + npar + len(adds) + 1]
        st_ref[0] = s_in
        s_new, y = chunk_fn(s_in, *[r[...] for r in refs[:nr]], *[p[...] for p in refs[nr:nr + npar]], rev=rev)
        y_ref[...] = y + refs[nr + npar][...] if adds else y
        s_scr[...] = s_new

    return pl.pallas_call(
        body, name=name, grid=(n,),
        in_specs=[pl.BlockSpec((L, w), lambda s, c=c: (order(s), c)) for _, w, c in rows] + [_full_spec(p) for p in params]
        + [pl.BlockSpec((L, out_w), lambda s: (order(s), 0)) for _ in adds],
        out_specs=[pl.BlockSpec((L, out_w), lambda s: (order(s), 0)),
                   pl.BlockSpec((1,) + state_shape, lambda s: (order(s), 0, 0))],
        out_shape=[jax.ShapeDtypeStruct((n * L, out_w), F32), jax.ShapeDtypeStruct((n,) + state_shape, F32)],
        scratch_shapes=[pltpu.VMEM(state_shape, F32)],
        compiler_params=_cp(("arbitrary",)),
    )(*[a for a, _, _ in rows], *params, *adds)


def _scan_bwd(name, chunk_fn, L, n, nx, rev, rows, params, states, dy, state_shape, out_w, addends=None):
    order = _scan_order(n, nx, rev, True)
    dy_blocks = dy.shape[0] // L
    nr, npar = len(rows), len(params)
    adds = [] if addends is None else list(addends)

    def body(*refs):
        i = pl.program_id(0)
        ds_scr = refs[-1]
        rv = [r[...] for r in refs[:nr]]
        pv = [p[...] for p in refs[nr:nr + npar]]
        st_ref, dy_ref = refs[nr + npar], refs[nr + npar + 1]
        a_refs = refs[nr + npar + 2:nr + npar + 2 + len(adds)]
        o_refs = refs[nr + npar + 2 + len(adds):-1]
        p_refs = o_refs[nr:]

        @pl.when(i == 0)
        def _():
            ds_scr[...] = jnp.zeros_like(ds_scr)
            for p_ref in p_refs:
                p_ref[...] = jnp.zeros_like(p_ref)

        _, vjp = jax.vjp(functools.partial(chunk_fn, rev=rev), st_ref[0], *rv, *pv)
        dy_blk = jnp.where(order(i) < dy_blocks, dy_ref[...].astype(F32), 0.0)
        grads = vjp((ds_scr[...], dy_blk))
        ds_scr[...] = grads[0]
        for j, (o_ref, g) in enumerate(zip(o_refs[:nr], grads[1:1 + nr])):
            o_ref[...] = g + a_refs[j][...] if adds else g
        for p_ref, g in zip(p_refs, grads[1 + nr:]):
            p_ref[...] += g

    row_specs = [pl.BlockSpec((L, w), lambda s: (order(s), 0)) for _, w, _ in rows]
    res = pl.pallas_call(
        body, name=name, grid=(n,),
        in_specs=[pl.BlockSpec((L, w), lambda s, c=c: (order(s), c)) for _, w, c in rows] + [_full_spec(p) for p in params]
        + [pl.BlockSpec((1,) + state_shape, lambda s: (order(s), 0, 0)),
           pl.BlockSpec((L, out_w), lambda s: (jnp.minimum(order(s), dy_blocks - 1), 0))]
        + row_specs[:len(adds)],
        out_specs=row_specs + [_full_spec(p) for p in params],
        out_shape=[jax.ShapeDtypeStruct((n * L, w), F32) for _, w, _ in rows] + [jax.ShapeDtypeStruct(p.shape, F32) for p in params],
        scratch_shapes=[pltpu.VMEM(state_shape, F32)],
        compiler_params=_cp(("arbitrary",)),
    )(*[a for a, _, _ in rows], *params, states, dy, *adds)
    return res[:nr], res[nr:]


CONV_W = 1024
CONV_COLBLK = (0, 1, 4)


def _conv_specs(nb, src_blk):
    halo = TM // 8
    return [pl.BlockSpec((TM, CONV_W), lambda j, i: (i, src_blk(j))),
            pl.BlockSpec((8, CONV_W), lambda j, i: (jnp.maximum(i * halo - 1, 0), src_blk(j))),
            pl.BlockSpec((8, CONV_W), lambda j, i: (jnp.minimum(i * halo + halo, nb * halo - 1), src_blk(j)))]


def _conv_ext(i, nb, cur, prev, nxt):
    has_prev = jnp.logical_and(i > 0, i < nb - 1)
    has_next = i < nb - 2
    return jnp.concatenate([jnp.where(has_prev, prev, 0.0), cur, jnp.where(has_next, nxt, 0.0)], axis=0)


def _conv_taps(ext, w, flip):
    acc = None
    for j in range(SSD_K):
        wj = w[SSD_K - 1 - j:SSD_K - j, :] if flip else w[j:j + 1, :]
        term = wj * ext[6 + j:6 + j + TM, :]
        acc = term if acc is None else acc + term
    return acc


def _conv(name, src, w8, b1, nb, *, permuted_src, act, flip, out_dtype):
    src_blk = (lambda j: jnp.where(j == 2, CONV_COLBLK[2], j)) if permuted_src else (lambda j: j)

    def body(cur, prev, nxt, w_ref, b_ref, o_ref):
        ext = _conv_ext(pl.program_id(1), nb, cur[...].astype(F32), prev[...].astype(F32), nxt[...].astype(F32))
        acc = _conv_taps(ext, w_ref[...], flip)
        if act:
            acc = jax.nn.silu(acc + b_ref[...])
        o_ref[...] = acc.astype(o_ref.dtype)

    return pl.pallas_call(
        body, name=name, grid=(3, nb),
        in_specs=_conv_specs(nb, src_blk) + [pl.BlockSpec((8, CONV_W), lambda j, i: (0, j)), pl.BlockSpec((1, CONV_W), lambda j, i: (0, j))],
        out_specs=pl.BlockSpec((TM, CONV_W), lambda j, i: (i, j)),
        out_shape=jax.ShapeDtypeStruct((nb * TM, 3 * CONV_W), out_dtype),
        compiler_params=_cp(("parallel", "parallel")),
    )(src, src, src, w8, b1)


def _conv_bwd_pre(name, p1, w8, b1, dxbc_parts, nb):
    src_blk = lambda j: jnp.where(j == 2, CONV_COLBLK[2], j)
    xs_parts, bc_parts = dxbc_parts
    n_x, n_bc = len(xs_parts), len(bc_parts)
    x_blocks = [p.shape[0] // TM for p in xs_parts]

    def body(*refs):
        cur, prev, nxt, w_ref, b_ref = refs[:5]
        d_refs = refs[5:5 + n_x + n_bc]
        da_ref, dw_ref, db_ref = refs[5 + n_x + n_bc:]
        j, i = pl.program_id(0), pl.program_id(1)
        ext = _conv_ext(i, nb, cur[...], prev[...], nxt[...])
        acc = _conv_taps(ext, w_ref[...], False) + b_ref[...]
        dx = d_refs[0][...]
        for r, blocks in zip(d_refs[1:n_x], x_blocks[1:]):
            dx = dx + jnp.where(i < blocks, r[...], 0.0)
        dbc = jnp.concatenate([d_refs[n_x][...], d_refs[n_x + 1][...]], axis=1)
        dy = jnp.where(j == 2, dbc, dx)
        sg = jax.nn.sigmoid(acc)
        da = dy * (sg + acc * sg * (1.0 - sg))
        da_ref[...] = da

        @pl.when(i == 0)
        def _():
            dw_ref[...] = jnp.zeros_like(dw_ref)
            db_ref[...] = jnp.zeros_like(db_ref)

        rows = [jnp.sum(da * ext[6 + t:6 + t + TM, :], axis=0, keepdims=True) for t in range(SSD_K)]
        dw_ref[...] += jnp.concatenate(rows + [jnp.zeros((8 - SSD_K, CONV_W), F32)], axis=0)
        db_ref[...] += jnp.sum(da, axis=0, keepdims=True)

    x_specs = [pl.BlockSpec((TM, CONV_W), lambda j, i, b=b: (jnp.minimum(i, b - 1), jnp.minimum(j, 1))) for b in x_blocks]
    bc_specs = [pl.BlockSpec((TM, 512), lambda j, i: (i, 0)) for _ in bc_parts]
    return pl.pallas_call(
        body, name=name, grid=(3, nb),
        in_specs=_conv_specs(nb, src_blk) + [pl.BlockSpec((8, CONV_W), lambda j, i: (0, j)), pl.BlockSpec((1, CONV_W), lambda j, i: (0, j))]
        + x_specs + bc_specs,
        out_specs=[pl.BlockSpec((TM, CONV_W), lambda j, i: (i, j)), pl.BlockSpec((8, CONV_W), lambda j, i: (0, j)),
                   pl.BlockSpec((1, CONV_W), lambda j, i: (0, j))],
        out_shape=[jax.ShapeDtypeStruct((nb * TM, 3 * CONV_W), F32), jax.ShapeDtypeStruct((8, 3 * CONV_W), F32),
                   jax.ShapeDtypeStruct((1, 3 * CONV_W), F32)],
        compiler_params=_cp(("arbitrary", "arbitrary")),
    )(p1, p1, p1, w8, b1, *xs_parts, *bc_parts)


def _grid_block(a):
    nbv = TM // a
    blk_b = max(nbv, 8)
    return nbv, blk_b, blk_b // nbv


def _grid_spec(a, nb):
    _, blk_b, per = _grid_block(a)
    return pl.BlockSpec((a, blk_b, D), lambda i: (0, jnp.minimum(i, nb - 2) // per, 0))


def _grid_rows(v_ref, a, i):
    nbv, _, per = _grid_block(a)

    def pick(ph):
        return jnp.concatenate([v_ref[:, ph * nbv + t, :] for t in range(nbv)], axis=0)

    out = pick(0)
    for ph in range(1, per):
        out = jnp.where(i % per == ph, pick(ph), out)
    return out


def _loss_head(x, f, target, modp, g_final, rows_r):
    tview = target.reshape(rows_r, target.shape[0] // rows_r, D)
    nb = x.shape[0] // TM + 1

    def fn(x_, f_, tgt, modp_, g_):
        xn = x_ + _sel_mod(modp_, 0.0)[5:6] * f_
        err = _rms(xn) * g_ - tgt
        return 0.5 * jnp.sum(jnp.mean(err * err, axis=-1))

    def body(x_ref, f_ref, t_ref, m_ref, g_ref, l_ref, dx_ref, df_ref, dm_ref, dg_ref):
        i = pl.program_id(0)
        tgt = _grid_rows(t_ref, rows_r, i)
        l, vjp = jax.vjp(lambda a_, b_, c_, d_: fn(a_, b_, tgt, c_, d_), x_ref[...], f_ref[...], m_ref[...], g_ref[...])
        dx, df, dm, dg = vjp(jnp.ones((), F32))

        @pl.when(i == 0)
        def _():
            l_ref[...] = jnp.zeros_like(l_ref)
            dm_ref[...] = jnp.zeros_like(dm_ref)
            dg_ref[...] = jnp.zeros_like(dg_ref)

        l_ref[...] += jnp.reshape(l, (1, 1))
        dx_ref[...] = dx
        df_ref[...] = df.astype(df_ref.dtype)
        dm_ref[...] += dm
        dg_ref[...] += dg

    rowspec = pl.BlockSpec((TM, D), lambda i: (i, 0))
    return pl.pallas_call(
        body, name="loss_head", grid=(nb - 1,),
        in_specs=[rowspec, rowspec, _grid_spec(rows_r, nb), _full_spec(modp), _full_spec(g_final)],
        out_specs=[pl.BlockSpec((1, 1), lambda i: (0, 0)), rowspec, rowspec, _full_spec(modp), _full_spec(g_final)],
        out_shape=[jax.ShapeDtypeStruct((1, 1), F32), jax.ShapeDtypeStruct(x.shape, F32), jax.ShapeDtypeStruct(x.shape, MXU_DTYPE),
                   jax.ShapeDtypeStruct(modp.shape, F32), jax.ShapeDtypeStruct(g_final.shape, F32)],
        compiler_params=_cp(("arbitrary",)),
    )(x, f, tview, modp, g_final)


def _repack(name, shards, segs, wp):
    nd, kk, ws = shards.shape
    tr = 128
    used = sum(e - s for s, e in segs)

    def body(a_ref, o_ref):
        full = jnp.concatenate([a_ref[d].astype(F32) for d in range(nd)], axis=1)
        parts = [full[:, s:e] for s, e in segs]
        if wp > used:
            parts.append(jnp.zeros((tr, wp - used), F32))
        o_ref[...] = jnp.concatenate(parts, axis=1).astype(o_ref.dtype)

    return pl.pallas_call(
        body, name=name, grid=(kk // tr,),
        in_specs=[pl.BlockSpec((nd, tr, ws), lambda i: (0, i, 0))],
        out_specs=pl.BlockSpec((tr, wp), lambda i: (i, 0)),
        out_shape=jax.ShapeDtypeStruct((kk, wp), MXU_DTYPE),
        compiler_params=_cp(("parallel",)),
    )(shards)


def _unpack(name, dw, segs, ws, out_dtype):
    kk, wp = dw.shape
    tr = 128
    order = sorted(range(len(segs)), key=lambda i: segs[i][0])
    offs, o = [], 0
    for s, e in segs:
        offs.append(o)
        o += e - s

    def body(a_ref, o_ref):
        a = a_ref[...].astype(F32)
        full = jnp.concatenate([a[:, offs[i]:offs[i] + segs[i][1] - segs[i][0]] for i in order], axis=1)
        for d in range(NDEV):
            o_ref[d] = full[:, d * ws:(d + 1) * ws].astype(o_ref.dtype)

    return pl.pallas_call(
        body, name=name, grid=(kk // tr,),
        in_specs=[pl.BlockSpec((tr, wp), lambda i: (i, 0))],
        out_specs=pl.BlockSpec((NDEV, tr, ws), lambda i: (0, i, 0)),
        out_shape=jax.ShapeDtypeStruct((NDEV, kk, ws), out_dtype),
        compiler_params=_cp(("parallel",)),
    )(dw)


def _adam_math(w, g, m, v):
    m = ADAM_B1 * m + (1.0 - ADAM_B1) * g
    v = ADAM_B2 * v + (1.0 - ADAM_B2) * jnp.square(g)
    m_hat = m / (1.0 - ADAM_B1 ** ADAM_STEP)
    v_hat = v / (1.0 - ADAM_B2 ** ADAM_STEP)
    delta = -ADAM_LR * (m_hat / (jnp.sqrt(v_hat) + ADAM_EPS) + ADAM_WD * w)
    return delta, m, v


def _adam(name, w, parts, m, v, after):
    r, c = w.shape
    nsec, npart = len(parts), parts[0].shape[0]
    rs = r // nsec
    tr = _pick(rs, (256, 128, 64, 32, 16, 8)) if rs * c * 4 > (1 << 20) else rs
    tiles = rs // tr

    def body(w_ref, *refs):
        m_ref, v_ref, _, g_ref, d_ref, nm_ref, nv_ref = refs[nsec:]
        sec = pl.program_id(0) // tiles
        g = None
        for a, p_ref in enumerate(refs[:nsec]):
            ga = p_ref[0].astype(F32)
            for s in range(1, npart):
                ga = ga + p_ref[s].astype(F32)
            g = ga if g is None else jnp.where(sec == a, ga, g)
        delta, nm, nv = _adam_math(w_ref[...], g, m_ref[...], v_ref[...])
        g_ref[...], d_ref[...], nm_ref[...], nv_ref[...] = g, delta, nm, nv

    spec = pl.BlockSpec((tr, c), lambda i: (i, 0))
    part_specs = [pl.BlockSpec((npart, tr, c), lambda i, a=a: (0, jnp.clip(i - a * tiles, 0, tiles - 1), 0)) for a in range(nsec)]
    return pl.pallas_call(
        body, name=name, grid=(r // tr,),
        in_specs=[spec] + part_specs + [spec, spec, ANY],
        out_specs=[spec] * 4, out_shape=[jax.ShapeDtypeStruct((r, c), F32)] * 4,
        compiler_params=_cp(("parallel",)),
    )(w, *parts, m, v, after)


def _mod_fwd(c_all, mod_w):
    nl, _, ws = mod_w.shape

    def body(c_ref, w_ref, o_ref):
        o_ref[0] = _dot(jax.nn.silu(c_ref[...]), w_ref[0])

    return pl.pallas_call(
        body, name="mod_fwd", grid=(nl,),
        in_specs=[_full_spec(c_all), pl.BlockSpec((1, D, ws), lambda i: (i, 0, 0))],
        out_specs=pl.BlockSpec((1, 16, ws), lambda i: (i, 0, 0)),
        out_shape=jax.ShapeDtypeStruct((nl, 16, ws), F32),
        compiler_params=_cp(("parallel",)),
    )(c_all, mod_w)


def _mod_bwd(c_all, mod_w, dm):
    nl, _, ws = mod_w.shape

    def body(c_ref, w_ref, d_ref, dw_ref, dc_ref):
        dw_ref[0] = _dot_tn(jax.nn.silu(c_ref[...]), d_ref[0])
        dc_ref[0] = _dot_nt(d_ref[0], w_ref[0])

    return pl.pallas_call(
        body, name="mod_bwd", grid=(nl,),
        in_specs=[_full_spec(c_all), pl.BlockSpec((1, D, ws), lambda i: (i, 0, 0)), pl.BlockSpec((1, 16, ws), lambda i: (i, 0, 0))],
        out_specs=[pl.BlockSpec((1, D, ws), lambda i: (i, 0, 0)), pl.BlockSpec((1, 16, D), lambda i: (i, 0, 0))],
        out_shape=[jax.ShapeDtypeStruct((nl, D, ws), F32), jax.ShapeDtypeStruct((nl, 16, D), F32)],
        compiler_params=_cp(("parallel",)),
    )(c_all, mod_w, dm)


def _sum_parts(name, parts):
    npart, r, c = parts.shape

    def body(p_ref, o_ref):
        g = p_ref[0].astype(F32)
        for s in range(1, npart):
            g = g + p_ref[s].astype(F32)
        o_ref[...] = g

    return pl.pallas_call(body, name=name, out_shape=jax.ShapeDtypeStruct((r, c), F32), compiler_params=_cp())(parts)


MESH = pl.DeviceIdType.MESH
ANY = pl.BlockSpec(memory_space=pl.ANY)
N_PEERS = NDEV - 1


def _mesh_pos():
    return lax.axis_index("x"), lax.axis_index("y"), lax.axis_index("c")


def _slot(px, py, pc):
    return 4 * px + 2 * py + pc


def _two_level_gather(x_refs, o_refs, send_sems, recv_sems, local_sems):
    x, y, c = _mesh_pos()
    me, sibling = (x, y, c), (x, y, 1 - c)
    chips = [(1 - x, y), (x, 1 - y), (1 - x, 1 - y)]
    n = len(x_refs)

    def copy(a, k, block, to, src=None):
        dst = o_refs[a].at[_slot(*block)]
        return pltpu.make_async_remote_copy(src_ref=dst if src is None else src, dst_ref=dst, send_sem=send_sems.at[a, k],
                                            recv_sem=recv_sems.at[a, k], device_id=to, device_id_type=MESH)

    mine = [pltpu.make_async_copy(x_refs[a], o_refs[a].at[_slot(*me)], local_sems.at[a]) for a in range(n)]
    for cp in mine:
        cp.start()
    first = []
    for a in range(n):
        first.append(copy(a, 0, me, sibling, src=x_refs[a]))
        first += [copy(a, 1 + j, me, (*chip, c), src=x_refs[a]) for j, chip in enumerate(chips)]
    for cp in first:
        cp.start()
    passed = []
    for j, chip in enumerate(chips):
        for a in range(n):
            copy(a, 1 + j, (*chip, c), me).wait_recv()
            fwd = copy(a, 4 + j, (*chip, c), sibling)
            fwd.start()
            passed.append(fwd)
    for a in range(n):
        copy(a, 0, sibling, me).wait_recv()
        for j, chip in enumerate(chips):
            copy(a, 4 + j, (*chip, 1 - c), me).wait_recv()
    for cp in first + passed:
        cp.wait_send()
    for cp in mine:
        cp.wait()


def _ag_small(name, x):
    r, c = x.shape

    def body(x_ref, o_ref, send_sems, recv_sems, local_sems):
        _two_level_gather([x_ref], [o_ref], send_sems, recv_sems, local_sems)

    return pl.pallas_call(
        body, name=name, out_shape=jax.ShapeDtypeStruct((NDEV, r, c), x.dtype),
        in_specs=[pl.BlockSpec(memory_space=pltpu.VMEM)], out_specs=pl.BlockSpec(memory_space=pltpu.VMEM),
        scratch_shapes=[pltpu.SemaphoreType.DMA((1, N_PEERS)), pltpu.SemaphoreType.DMA((1, N_PEERS)), pltpu.SemaphoreType.DMA((1,))],
        compiler_params=pltpu.CompilerParams(vmem_limit_bytes=VMEM_LIMIT),
    )(x)


def _ag_big(name, shards):
    n = len(shards)

    def body(*refs):
        _two_level_gather(refs[:n], refs[n:2 * n], *refs[2 * n:])

    return pl.pallas_call(
        body, name=name, out_shape=[jax.ShapeDtypeStruct((NDEV,) + s.shape, s.dtype) for s in shards],
        in_specs=[ANY] * n, out_specs=[ANY] * n,
        scratch_shapes=[pltpu.SemaphoreType.DMA((n, N_PEERS)), pltpu.SemaphoreType.DMA((n, N_PEERS)), pltpu.SemaphoreType.DMA((n,))],
    )(*shards)


HBM = pl.BlockSpec(memory_space=pltpu.HBM)
SEM = pl.BlockSpec(memory_space=pltpu.SEMAPHORE)
EFFECT = pltpu.SideEffectType.DATAFLOW_SIDE_EFFECTING


def _peers(x, y, c):
    return [(k - 1, ((1 - x) if k & 4 else x, (1 - y) if k & 2 else y, (1 - c) if k & 1 else c)) for k in range(1, NDEV)]


def _xchg_copy(src_refs, land_refs, send_sems, recv_sems, a, k, peer, me, scatter):
    src = src_refs[a].at[_slot(*peer)] if scatter else src_refs[a]
    return pltpu.make_async_remote_copy(src_ref=src, dst_ref=land_refs[a].at[me], send_sem=send_sems.at[a * N_PEERS + k],
                                        recv_sem=recv_sems.at[a * N_PEERS + k], device_id=peer, device_id_type=MESH)


def _xchg_start(name, srcs, lands, deps, scatter):
    n, nd = len(srcs), len(deps)

    def body(*refs):
        src_refs, land_refs = refs[:n], refs[n:2 * n]
        send_sems, recv_sems, token = refs[2 * n + nd], refs[2 * n + nd + 1], refs[-1]
        x, y, c = _mesh_pos()
        me = _slot(x, y, c)
        for k, peer in _peers(x, y, c):
            for a in range(n):
                _xchg_copy(src_refs, land_refs, send_sems, recv_sems, a, k, peer, me, scatter).start()
        token[...] = jnp.zeros_like(token)

    res = pl.pallas_call(
        body, name=name,
        out_shape=(pltpu.SemaphoreType.DMA((n * N_PEERS,)), pltpu.SemaphoreType.DMA((n * N_PEERS,)),
                   *[pltpu.HBM(s.shape, s.dtype) for s in srcs], *[pltpu.HBM(s.shape, s.dtype) for s in lands],
                   jax.ShapeDtypeStruct((8, 128), F32)),
        in_specs=[HBM] * (2 * n) + [ANY] * nd,
        out_specs=(SEM, SEM, *([HBM] * (2 * n)), pl.BlockSpec(memory_space=pltpu.VMEM)),
        input_output_aliases={i: 2 + i for i in range(2 * n)},
        compiler_params=pltpu.CompilerParams(has_side_effects=EFFECT),
    )(*[pltpu.with_memory_space_constraint(s, pltpu.HBM) for s in srcs],
      *[pltpu.with_memory_space_constraint(s, pltpu.HBM) for s in lands], *deps)
    return res[0], res[1], res[2:2 + n], res[2 + n:2 + 2 * n], res[-1]


def _xchg_wait(name, send_sems, recv_sems, srcs, lands, after, scatter):
    n = len(srcs)

    def body(*refs):
        src_refs, land_refs = refs[:n], refs[n:2 * n]
        s_sems, r_sems = refs[2 * n], refs[2 * n + 1]
        x, y, c = _mesh_pos()
        me = _slot(x, y, c)
        for k, peer in _peers(x, y, c):
            for a in range(n):
                cp = _xchg_copy(src_refs, land_refs, s_sems, r_sems, a, k, peer, me, scatter)
                cp.wait_send()
                cp.wait_recv()

    res = pl.pallas_call(
        body, name=name,
        out_shape=[pltpu.HBM(s.shape, s.dtype) for s in srcs] + [pltpu.HBM(s.shape, s.dtype) for s in lands],
        in_specs=[HBM] * (2 * n) + [SEM, SEM, ANY], out_specs=[HBM] * (2 * n),
        input_output_aliases={i: i for i in range(2 * n)},
        compiler_params=pltpu.CompilerParams(has_side_effects=EFFECT),
    )(*srcs, *lands, send_sems, recv_sems, after)
    return res[n:]


def _landing(name, srcs, me, scatter):
    shapes = [s.shape[-2:] for s in srcs]

    def body(me_ref, *refs):
        for s_ref, o_ref in zip(refs[:len(srcs)], refs[len(srcs):]):
            o_ref[...] = s_ref[...].reshape(o_ref.shape)

    def slot_spec(r, c):
        return pl.BlockSpec((1, r, c), lambda i, me_ref: (me_ref[0], 0, 0))

    return pl.pallas_call(
        body, name=name, out_shape=[jax.ShapeDtypeStruct((NDEV, r, c), s.dtype) for s, (r, c) in zip(srcs, shapes)],
        grid_spec=pltpu.PrefetchScalarGridSpec(
            num_scalar_prefetch=1, grid=(1,),
            in_specs=[slot_spec(r, c) if scatter else pl.BlockSpec((r, c), lambda i, me_ref: (0, 0)) for r, c in shapes],
            out_specs=[slot_spec(r, c) for r, c in shapes]),
        compiler_params=_cp(("arbitrary",)),
    )(jnp.reshape(me, (1,)).astype(jnp.int32), *srcs)


STAGES = ("l0_mixer", "l0_ffn", "l1_mixer", "l1_ffn")
STAGE_LAYOUT = {"l0_mixer": (AB_SEGS, AB_P), "l1_mixer": (SSD_SEGS, SSD_P_W)}


class _Exchange:
    def __init__(self, shards, me):
        self.shards, self.me = shards, me
        self.pending, self.pending_grads, self.recv = None, None, {}

    def _layout(self, stage):
        ws = self.shards[stage][0].shape[-1]
        return STAGE_LAYOUT.get(stage, (((0, NDEV * ws),), NDEV * ws)) + (ws,)

    def _start_gather(self, stage, deps):
        srcs = list(self.shards[stage])
        lands = _landing("own_" + stage, srcs, self.me, False)
        return _xchg_start("gather_start_" + stage, srcs, lands, deps, False)

    def get(self, stage, dep, thread):
        i = STAGES.index(stage)
        if i == 0:
            g_in, g_out = _ag_big("gather_" + stage, list(self.shards[stage]))
            deps = [g_out, dep]
        else:
            ss, rs, srcs, lands, _ = self.pending
            g_in, g_out = _xchg_wait("gather_wait_" + stage, ss, rs, srcs, lands, dep, False)
            self.pending, deps = None, [g_out]
        if i + 1 < len(STAGES):
            self.pending = self._start_gather(STAGES[i + 1], deps)
            thread = thread + self.pending[4][0, 0]
        segs, wp, _ = self._layout(stage)
        return _repack("repack_" + stage, g_in, segs, wp), g_out.reshape(-1, D), thread

    def put(self, stage, d_in, d_out, thread):
        segs, _, ws = self._layout(stage)
        parts = [_unpack("unpack_" + stage, d_in, segs, ws, MXU_DTYPE), d_out.reshape(NDEV, -1, D)]
        deps = [parts[0]]
        if self.pending_grads is not None:
            deps = [self.finish(parts[0])[0]]
        self.staged = (stage, parts)
        return thread if stage == STAGES[0] else thread + self.start_last(deps)[0, 0]

    def start_last(self, deps):
        stage, parts = self.staged
        lands = _landing("own_grad_" + stage, parts, self.me, True)
        self.pending_grads = (stage,) + _xchg_start("scatter_start_" + stage, parts, lands, deps, True)
        return self.pending_grads[5]

    def finish(self, after):
        stage, ss, rs, srcs, lands, _ = self.pending_grads
        self.recv[stage] = _xchg_wait("scatter_wait_" + stage, ss, rs, srcs, lands, after, True)
        self.pending_grads = None
        return self.recv[stage]


def _ffn_fwd(tag, h, w_in, w_out, nb, cb):
    pf = _mm(tag + "_ffn_in", h, w_in, "nn", MXU_DTYPE)
    (act,) = _rowwise(tag + "_swiglu", _fn_swiglu, nb, cb, [_row(pf)], [], [(D_FF, MXU_DTYPE)])
    return pf, act, _mm(tag + "_ffn_out", act, w_out, "nn", F32)


def _ffn_bwd(tag, h, pf, act, df, w_in, w_out, nb, cb):
    dw_out = _mm(tag + "_ffn_out_dw", act, df, "tn", MXU_DTYPE)
    dact = _mm(tag + "_ffn_out_dx", df, w_out, "nt", MXU_DTYPE)
    (dpf,), _ = _rowwise_vjp(tag + "_swiglu_bwd", _fn_swiglu, nb, cb, [_row(pf)], [], [_row(dact)], [(0, MXU_DTYPE, None)])
    dw_in = _mm(tag + "_ffn_in_dw", h, dpf, "tn", MXU_DTYPE)
    dh = _mm(tag + "_ffn_in_dx", dpf, w_in, "nt", MXU_DTYPE)
    return dw_out, dw_in, dh


def _local_step(x, ctx, target, mod, P, comm):
    T = x.shape[0]
    N = T + CTX
    nb, cb = N // TM, N // TM - 1
    R = T // GRID_W
    mod0, mod1 = mod[0], mod[1]
    ng = P["norm_g"]
    g00, g01, g10, g11 = ng[0, 0][None], ng[0, 1][None], ng[1, 0][None], ng[1, 1][None]
    pre = functools.partial(_fn_prenorm, a=0, b=1)
    rpre = functools.partial(_fn_resid_prenorm, gi=2, a=3, b=4)
    res5 = functools.partial(_fn_resid, gi=5)
    dirs = (("f", False), ("b", True))

    xc0 = jnp.concatenate([x, ctx], axis=0)
    w_ab_in, w_ab_out, g00 = comm.get("l0_mixer", mod, g00)
    (h0,) = _rowwise("l0_prenorm", pre, nb, cb, [_row(xc0)], [g00, mod0], [(D, MXU_DTYPE)])
    p0 = _mm("l0_in", h0, w_ab_in, "nn", F32)
    gla_rows = [(p0, 512, 0), (p0, 256, 8), (p0, 256, 9), (p0, 128, 20)]
    gla_blk = _multi_chunk(_gla_chunk, GLA_L, TM // GLA_L, len(gla_rows))
    gla_par = {d: [P["ab_gate_w"][int(r)], P["ab_gate_b"][int(r)][None]] for d, r in dirs}
    gla_state = (GLA_H * GLA_DV, GLA_H * GLA_DK)
    o, st0 = None, {}
    for d, rev in dirs:
        o, st0[d] = _scan_fwd("gla_fwd_" + d, gla_blk, TM, nb, cb, rev, gla_rows, gla_par[d], gla_state, GLA_H * GLA_DV, o)
    n128, cb128 = N // GMLP_L, T // GMLP_L
    mix_rows = [_row(o, tm=GMLP_L)] + [_row(p0, 512, j, tm=GMLP_L) for j in (1, 2, 3)]
    mix_par = [P["ab_gla_norm_g"], P["ab_vnorm_g"], P["ab_spatial_w"].reshape(GMLP_G * GMLP_L, GMLP_L), P["ab_spatial_b"].T]
    (cat0,) = _rowwise("l0_mix", _fn_mixpost, n128, cb128, mix_rows, mix_par, [(D, MXU_DTYPE)], tm=GMLP_L)
    y0 = _mm("l0_out", cat0, w_ab_out, "nn", F32)
    w_fi0, w_fo0, g01 = comm.get("l0_ffn", y0, g01)
    x1, h1 = _rowwise("l0_ffn_prenorm", rpre, nb, cb, [_row(xc0), _row(y0)], [g01, mod0, mod0], [(D, F32), (D, MXU_DTYPE)])
    pf0, act0, f0 = _ffn_fwd("l0", h1, w_fi0, w_fo0, nb, cb)
    w_ssd_in, w_ssd_out, g10 = comm.get("l1_mixer", f0, g10)
    x2p, h2 = _rowwise("l0_resid_l1_prenorm", functools.partial(_fn_resid_prenorm, gi=5, a=0, b=1), nb, cb,
                       [_row_grid(x1, R, nb), _row_grid(f0, R, nb)], [g10, mod0, mod1], [(D, F32), (D, MXU_DTYPE)])
    p1 = _mm("l1_in", h2, w_ssd_in, "nn", F32)
    conv_w8 = jnp.concatenate([P["ssd_conv_w"], jnp.zeros((8 - SSD_K, 3 * CONV_W), F32)], axis=0)
    xbc = _conv("l1_conv", p1, conv_w8, P["ssd_conv_b"], nb, permuted_src=True, act=True, flip=False, out_dtype=F32)
    ssd_rows = [(xbc, SSD_INNER, 0), (xbc, 512, 4), (xbc, 512, 5), (p1, 128, 40)]
    ssd_blk = _multi_chunk(_ssd_chunk, SSD_L, TM // SSD_L, len(ssd_rows))
    ssd_par = {d: [P["ssd_dt_bias"][int(r)][None], P["ssd_a_log"][int(r)][None]] for d, r in dirs}
    ssd_state = (SSD_N, SSD_INNER)
    ys, st1 = None, {}
    for d, rev in dirs:
        ys, st1[d] = _scan_fwd("ssd_fwd_" + d, ssd_blk, TM, nb, cb, rev, ssd_rows, ssd_par[d], ssd_state, SSD_INNER, ys)
    fin_rows = [_row(ys), _row(xbc, SSD_INNER, 0), _row(p1, SSD_INNER, 1)]
    fin_par = [P["ssd_d"], P["ssd_norm_g"]]
    (yn,) = _rowwise("l1_finish", _fn_ssd_finish, cb, cb, fin_rows, fin_par, [(SSD_INNER, MXU_DTYPE)])
    y1 = _mm("l1_out", yn, w_ssd_out, "nn", F32)
    w_fi1, w_fo1, g11 = comm.get("l1_ffn", y1, g11)
    x3, h3 = _rowwise("l1_ffn_prenorm", rpre, cb, cb, [_row(x2p), _row(y1)], [g11, mod1, mod1], [(D, F32), (D, MXU_DTYPE)])
    pf1, act1, f1 = _ffn_fwd("l1", h3, w_fi1, w_fo1, cb, cb)
    loss, dx3, df1, dm1_j, d_final_g = _loss_head(x3, f1, target, mod1, P["final_norm_g"], R)

    dP = {"final_norm_g": d_final_g}
    dwo1, dwi1, dh3 = _ffn_bwd("l1", h3, pf1, act1, df1, w_fi1, w_fo1, cb, cb)
    g11 = comm.put("l1_ffn", dwi1, dwo1, g11)
    (dx2p_a, dy1), (dg11, dm1_a, dm1_b) = _rowwise_vjp(
        "l1_ffn_prenorm_bwd", rpre, cb, cb, [_row(x2p), _row(y1)], [g11, mod1, mod1], [_row(dx3), _row(dh3)],
        [(0, F32, None), (1, MXU_DTYPE, None)])
    d_ssd_out = _mm("l1_out_dw", yn, dy1, "tn", MXU_DTYPE)
    dyn = _mm("l1_out_dx", dy1, w_ssd_out, "nt", MXU_DTYPE)
    (dys, dxs, dz), (dP["ssd_d"], dP["ssd_norm_g"]) = _rowwise_vjp(
        "l1_finish_bwd", _fn_ssd_finish, cb, cb, fin_rows, fin_par, [_row(dyn)],
        [(0, F32, None), (1, F32, None), (2, MXU_DTYPE, None)])
    dssd, ddtb, dalog = None, [], []
    for d, rev in dirs:
        dssd, (ddtb_, dalog_) = _scan_bwd("ssd_bwd_" + d, ssd_blk, TM, nb, cb, rev, ssd_rows, ssd_par[d], st1[d], dys, ssd_state,
                                          SSD_INNER, dssd)
        ddtb.append(ddtb_); dalog.append(dalog_)
    dx_s, db_s, dc_s, dtl = dssd
    dP["ssd_dt_bias"] = jnp.concatenate(ddtb, axis=0)
    dP["ssd_a_log"] = jnp.concatenate(dalog, axis=0)
    dacc, dcw8, dP["ssd_conv_b"] = _conv_bwd_pre("l1_conv_bwd", p1, conv_w8, P["ssd_conv_b"], ([dx_s, dxs], [db_s, dc_s]), nb)
    dP["ssd_conv_w"] = dcw8[:SSD_K]
    dpc = _conv("l1_conv_dx", dacc, conv_w8, jnp.zeros((1, 3 * CONV_W), F32), nb, permuted_src=False, act=False, flip=True,
                out_dtype=MXU_DTYPE)
    cat1 = functools.partial(_fn_concat, sums=(1, 1, 1, 1), pad=SSD_P_W - 5248)
    (dp1,) = _rowwise("l1_dp", cat1, nb, cb, [_row(dpc, SSD_INNER, 0), _row(dz, valid=cb), _row(dpc, 1024, 2), _row(dtl)],
                      [], [(SSD_P_W, MXU_DTYPE)])
    g10 = comm.put("l1_mixer", _mm("l1_in_dw", h2, dp1, "tn", F32), d_ssd_out, g10)
    dh2 = _mm("l1_in_dx", dp1, w_ssd_in, "nt", MXU_DTYPE)
    (dx2p,), (dg10, dm1_f) = _rowwise_vjp("l1_prenorm_bwd", pre, nb, cb, [_row(x2p)], [g10, mod1], [_row(dh2)],
                                          [(0, F32, _row(dx2p_a, valid=cb))])

    (dx1_a, df0), (dm0_e,) = _rowwise_vjp("l0_resid_bwd", res5, nb, cb, [_row(x1), _row(f0)], [mod0],
                                          [_row_grid(dx2p, GRID_W, nb)], [(0, F32, None), (1, MXU_DTYPE, None)])
    dwo0, dwi0, dh1 = _ffn_bwd("l0", h1, pf0, act0, df0, w_fi0, w_fo0, nb, cb)
    g01 = comm.put("l0_ffn", dwi0, dwo0, g01)
    (dxc0_a, dy0), (dg01, dm0_a, dm0_b) = _rowwise_vjp(
        "l0_ffn_prenorm_bwd", rpre, nb, cb, [_row(xc0), _row(y0)], [g01, mod0, mod0], [_row(dx1_a), _row(dh1)],
        [(0, F32, None), (1, MXU_DTYPE, None)])
    d_ab_out = _mm("l0_out_dw", cat0, dy0, "tn", MXU_DTYPE)
    dcat0 = _mm("l0_out_dx", dy0, w_ab_out, "nt", MXU_DTYPE)
    (do, dr, du, dgm), (dP["ab_gla_norm_g"], dP["ab_vnorm_g"], dsw, dsb_t) = _rowwise_vjp(
        "l0_mix_bwd", _fn_mixpost, n128, cb128, mix_rows, mix_par, [_row(dcat0, tm=GMLP_L)],
        [(0, F32, None), (1, MXU_DTYPE, None), (2, MXU_DTYPE, None), (3, MXU_DTYPE, None)], tm=GMLP_L)
    dP["ab_spatial_w"] = dsw.reshape(GMLP_G, GMLP_L, GMLP_L)
    dP["ab_spatial_b"] = dsb_t.T
    gl, dgw, dgb = None, [], []
    for d, rev in dirs:
        gl, (dgw_, dgb_) = _scan_bwd("gla_bwd_" + d, gla_blk, TM, nb, cb, rev, gla_rows, gla_par[d], st0[d], do,
                                     gla_state, GLA_H * GLA_DV, gl)
        dgw.append(dgw_[None]); dgb.append(dgb_)
    dP["ab_gate_w"] = jnp.concatenate(dgw, axis=0)
    dP["ab_gate_b"] = jnp.concatenate(dgb, axis=0)
    cat0f = functools.partial(_fn_concat, sums=(1,) * 7, pad=AB_P - 2688)
    (dp0,) = _rowwise("l0_dp", cat0f, nb, cb, [_row(gl[0]), _row(dr), _row(du), _row(dgm), _row(gl[1]), _row(gl[2]), _row(gl[3])],
                      [], [(AB_P, MXU_DTYPE)])
    g00 = comm.put("l0_mixer", _mm("l0_in_dw", h0, dp0, "tn", F32), d_ab_out, g00)
    dh0 = _mm("l0_in_dx", dp0, w_ab_in, "nt", MXU_DTYPE)
    (dxc0,), (dg00, dm0_s) = _rowwise_vjp("l0_prenorm_bwd", pre, nb, cb, [_row(xc0)], [g00, mod0], [_row(dh0)],
                                          [(0, F32, _row(dxc0_a))])
    dP["norm_g"] = jnp.concatenate([dg00, dg01, dg10, dg11], axis=0).reshape(2, 2, D)
    dmod = jnp.stack([dm0_s + dm0_a + dm0_b + dm0_e, dm1_f + dm1_a + dm1_b + dm1_j])
    return loss, dxc0[:T], dmod, dP


WEIGHTS = ("c_ctx", "mod_w", "mod_b", "norm_g", "ffn_w_in", "ffn_w_out", "ab_w_in", "ab_gate_w", "ab_gate_b", "ab_gla_norm_g",
           "ab_vnorm_g", "ab_spatial_w", "ab_spatial_b", "ab_w_out", "ssd_w_in", "ssd_conv_w", "ssd_conv_b", "ssd_dt_bias",
           "ssd_a_log", "ssd_d", "ssd_norm_g", "ssd_w_out", "final_norm_g")
SMALL_SHARDED = ("norm_g", "ab_gate_w", "ab_gate_b", "ssd_conv_w", "ssd_conv_b", "ssd_norm_g")
SMALL = ("c_ctx", "mod_b", "norm_g", "ab_gate_w", "ab_gate_b", "ab_gla_norm_g", "ab_vnorm_g", "ab_spatial_w", "ab_spatial_b",
         "ssd_conv_w", "ssd_conv_b", "ssd_dt_bias", "ssd_a_log", "ssd_d", "ssd_norm_g", "final_norm_g")
LANES = 1024


def _pack(arrs, rows_multiple=8):
    flat = jnp.concatenate([a.reshape(-1).astype(F32) for a in arrs])
    rows = -(-flat.shape[0] // LANES)
    rows = -(-rows // rows_multiple) * rows_multiple
    return jnp.pad(flat, (0, rows * LANES - flat.shape[0])).reshape(rows, LANES)


def _unpack_flat(buf, shapes):
    lead = buf.shape[:-2]
    flat = buf.reshape(lead + (-1,))
    out, o = [], 0
    for s in shapes:
        n = math.prod(s)
        out.append(flat[..., o:o + n].reshape(lead + tuple(s)))
        o += n
    return out


def _unshard(g):
    g = jnp.moveaxis(g, 0, -2)
    return g.reshape(g.shape[:-2] + (g.shape[-2] * g.shape[-1],))


def _my_shard(full, me, ws):
    return lax.dynamic_slice_in_dim(full, me * ws, ws, axis=full.ndim - 1)


def _silu_vjp(cvec, dsc):
    def body(c_ref, d_ref, o_ref):
        _, vjp = jax.vjp(jax.nn.silu, c_ref[...])
        o_ref[...] = vjp(d_ref[...])[0]

    return pl.pallas_call(body, name="c_ctx_bwd", out_shape=jax.ShapeDtypeStruct(cvec.shape, F32), compiler_params=_cp())(cvec, dsc)


def kernel(x, c, ctx, c_ctx, mod_w, mod_b, norm_g, ffn_w_in, ffn_w_out, ab_w_in, ab_gate_w, ab_gate_b, ab_gla_norm_g, ab_vnorm_g, ab_spatial_w, ab_spatial_b, ab_w_out, ssd_w_in, ssd_conv_w, ssd_conv_b, ssd_dt_bias, ssd_a_log, ssd_d, ssd_norm_g, ssd_w_out, final_norm_g, loss_target, m_c_ctx, m_mod_w, m_mod_b, m_norm_g, m_ffn_w_in, m_ffn_w_out, m_ab_w_in, m_ab_gate_w, m_ab_gate_b, m_ab_gla_norm_g, m_ab_vnorm_g, m_ab_spatial_w, m_ab_spatial_b, m_ab_w_out, m_ssd_w_in, m_ssd_conv_w, m_ssd_conv_b, m_ssd_dt_bias, m_ssd_a_log, m_ssd_d, m_ssd_norm_g, m_ssd_w_out, m_final_norm_g, v_c_ctx, v_mod_w, v_mod_b, v_norm_g, v_ffn_w_in, v_ffn_w_out, v_ab_w_in, v_ab_gate_w, v_ab_gate_b, v_ab_gla_norm_g, v_ab_vnorm_g, v_ab_spatial_w, v_ab_spatial_b, v_ab_w_out, v_ssd_w_in, v_ssd_conv_w, v_ssd_conv_b, v_ssd_dt_bias, v_ssd_a_log, v_ssd_d, v_ssd_norm_g, v_ssd_w_out, v_final_norm_g):
    a = dict(locals())
    me = _slot(*_mesh_pos())
    ws_mod = mod_w.shape[-1]

    fwd_small = [c] + [a[k] for k in SMALL_SHARDED]
    g_small = _ag_small("gather_small", _pack(fwd_small))
    parts = _unpack_flat(g_small, [t.shape for t in fwd_small])
    c_rows = parts[0].reshape(NDEV, D)
    full = {k: _unshard(p) for k, p in zip(SMALL_SHARDED, parts[1:])}
    c_all = jnp.concatenate([c_rows, c_ctx[None], jnp.zeros((7, D), F32)], axis=0)
    m_all = _ag_small("gather_mod", _mod_fwd(c_all, mod_w).reshape(2 * 16, ws_mod)).reshape(NDEV, 2, 16, ws_mod)
    m_mine = lax.dynamic_index_in_dim(m_all, me, axis=2, keepdims=False)
    mx = jnp.moveaxis(m_mine, 0, 1).reshape(2, N_MOD, D) + mod_b.reshape(2, N_MOD, D)
    mc = jnp.moveaxis(m_all[:, :, 8, :], 0, 1).reshape(2, N_MOD, D) + mod_b.reshape(2, N_MOD, D)
    pad2 = jnp.zeros((2, 2, D), F32)
    mod = jnp.concatenate([mx, pad2, mc, pad2], axis=1)

    big = {"l0_mixer": (ab_w_in[0], ab_w_out[0]), "l0_ffn": (ffn_w_in[0], ffn_w_out[0]),
           "l1_mixer": (ssd_w_in[0], ssd_w_out[0]), "l1_ffn": (ffn_w_in[1], ffn_w_out[1])}
    comm = _Exchange({k: tuple(w.astype(MXU_DTYPE) for w in v) for k, v in big.items()}, me)
    P = {
        "norm_g": full["norm_g"], "ab_gate_w": full["ab_gate_w"][0], "ab_gate_b": full["ab_gate_b"][0],
        "ab_gla_norm_g": ab_gla_norm_g, "ab_vnorm_g": ab_vnorm_g, "ab_spatial_w": ab_spatial_w[0], "ab_spatial_b": ab_spatial_b[0],
        "ssd_conv_w": full["ssd_conv_w"][0], "ssd_conv_b": full["ssd_conv_b"], "ssd_dt_bias": ssd_dt_bias[0],
        "ssd_a_log": ssd_a_log[0], "ssd_d": ssd_d, "ssd_norm_g": full["ssd_norm_g"], "final_norm_g": final_norm_g[None],
    }

    loss, grad_x, dmod, dP = _local_step(x[0], ctx[0], loss_target[0], mod, P, comm)

    dmx, dmc = dmod[:, 0:N_MOD].reshape(2, N_MOD * D), dmod[:, 8:8 + N_MOD].reshape(2, N_MOD * D)
    small_names = ("ab_gate_w", "ab_gate_b", "ab_gla_norm_g", "ab_vnorm_g", "ab_spatial_w", "ab_spatial_b", "norm_g", "ssd_conv_w",
                   "ssd_conv_b", "ssd_dt_bias", "ssd_a_log", "ssd_d", "ssd_norm_g", "final_norm_g")
    bwd_small = [dP[k] for k in small_names] + [dmc, dmx]
    shapes = [t.shape for t in bwd_small]
    g_bwd = _ag_small("gather_small_grads", _pack(bwd_small))
    summed = _unpack_flat(_sum_parts("sum_small_grads", g_bwd), shapes)
    gfull = dict(zip(small_names, summed[:-2]))
    dmc_sum, dmx_sum = summed[-2], summed[-1]
    dmx_all = _unpack_flat(g_bwd, shapes)[-1]
    dmx_sh = jnp.moveaxis(_my_shard(dmx_all, me, ws_mod), 0, 1)
    dm = jnp.concatenate([dmx_sh, _my_shard(dmc_sum, me, ws_mod)[:, None, :], jnp.zeros((2, 7, ws_mod), F32)], axis=1)
    d_mod_w, dsc = _mod_bwd(c_all, mod_w, dm)
    dsc_ctx = (dsc[0, 8] + dsc[1, 8])[None]
    dsc_all = _ag_small("gather_c_ctx_grad", jnp.concatenate([dsc_ctx, jnp.zeros((7, D), F32)], axis=0))
    d_c_ctx = _silu_vjp(c_ctx[None], _sum_parts("sum_c_ctx_grad", dsc_all)[0:1])[0]

    g_small_w = {
        "c_ctx": d_c_ctx, "mod_b": dmx_sum + dmc_sum, "norm_g": gfull["norm_g"], "ab_gate_w": gfull["ab_gate_w"][None],
        "ab_gate_b": gfull["ab_gate_b"][None], "ab_gla_norm_g": gfull["ab_gla_norm_g"], "ab_vnorm_g": gfull["ab_vnorm_g"],
        "ab_spatial_w": gfull["ab_spatial_w"][None], "ab_spatial_b": gfull["ab_spatial_b"][None], "ssd_conv_w": gfull["ssd_conv_w"][None],
        "ssd_conv_b": gfull["ssd_conv_b"], "ssd_dt_bias": gfull["ssd_dt_bias"][None], "ssd_a_log": gfull["ssd_a_log"][None],
        "ssd_d": gfull["ssd_d"], "ssd_norm_g": gfull["ssd_norm_g"], "final_norm_g": gfull["final_norm_g"][0],
    }
    for k in SMALL_SHARDED:
        g_small_w[k] = _my_shard(g_small_w[k], me, a[k].shape[-1])
    token = comm.start_last([d_c_ctx])
    res = _adam("adam_small", _pack([a[k] for k in SMALL]), [_pack([g_small_w[k] for k in SMALL])[None]],
                _pack([a["m_" + k] for k in SMALL]), _pack([a["v_" + k] for k in SMALL]), token)
    out = {k: vals for k, vals in zip(SMALL, zip(*[_unpack_flat(r, [a[k].shape for k in SMALL]) for r in res]))}

    def adam_big(name, w2d, parts, m2d, v2d, shape):
        return tuple(r.reshape(shape) for r in _adam(name, w2d, parts, m2d, v2d, token))

    def flat2(t):
        return t.reshape(-1, t.shape[-1])

    out["mod_w"] = adam_big("adam_mod_w", flat2(mod_w), [d_mod_w.reshape(1, -1, ws_mod)], flat2(m_mod_w), flat2(v_mod_w), mod_w.shape)

    for j, k in enumerate(("ffn_w_in", "ffn_w_out")):
        out[k] = adam_big("adam_" + k, flat2(a[k]), [comm.recv["l0_ffn"][j], comm.recv["l1_ffn"][j]], flat2(a["m_" + k]),
                          flat2(a["v_" + k]), a[k].shape)
    for j, k in enumerate(("ssd_w_in", "ssd_w_out")):
        out[k] = adam_big("adam_" + k, a[k][0], [comm.recv["l1_mixer"][j]], a["m_" + k][0], a["v_" + k][0], a[k].shape)
    recv_ab = comm.finish(out["ssd_w_out"][3])
    for j, k in enumerate(("ab_w_in", "ab_w_out")):
        out[k] = adam_big("adam_" + k, a[k][0], [recv_ab[j]], a["m_" + k][0], a["v_" + k][0], a[k].shape)

    loss_all = lax.psum(loss[0, 0], ("x", "y", "c"))
    return (loss_all, grad_x[None], *[out[k][0] for k in WEIGHTS], *[out[k][1] for k in WEIGHTS],
            *[out[k][2] for k in WEIGHTS], *[out[k][3] for k in WEIGHTS])
```

```python
import functools
import math

import jax
import jax.numpy as jnp
from jax import lax
from jax.experimental import pallas as pl
from jax.experimental.pallas import tpu as pltpu

F32 = jnp.float32
BF16 = jnp.bfloat16
MXU_DTYPE = jnp.bfloat16
HI = lax.Precision.HIGHEST

D = 1024
NDEV = 8
N_MOD = 6
EPS = 1e-6
GRID_W = 64
CTX = 256
TM = 256
D_FF = 2816
GLA_H, GLA_DK, GLA_DV, GLA_LR, GLA_TAU, GLA_L = 4, 64, 128, 16, 16.0, 64
GMLP_G, GMLP_C, GMLP_L = 4, 128, 128
SSD_H, SSD_P, SSD_G, SSD_N, SSD_L, SSD_K = 32, 64, 4, 128, 128, 5
SSD_INNER = SSD_H * SSD_P
AB_IN = 2592
SSD_IN = 5184
AB_SEGS = ((256, 768), (1056, 1568), (1568, 2080), (2080, 2592), (0, 256), (800, 1056), (768, 800))
AB_P = 2816
SSD_SEGS = ((0, 2048), (3136, 5184), (2048, 2560), (2560, 3072), (3072, 3136))
SSD_P_W = 5376
VMEM_LIMIT = 56 * 1024 * 1024

ADAM_LR, ADAM_B1, ADAM_B2, ADAM_EPS, ADAM_WD, ADAM_STEP = 0.001, 0.9, 0.999, 1e-08, 0.01, 10


def _cp(sem=None, **kw):
    return pltpu.CompilerParams(dimension_semantics=sem, vmem_limit_bytes=VMEM_LIMIT, **kw)


def _dot(a, b, dims=(((1,), (0,)), ((), ()))):
    return lax.dot_general(a.astype(MXU_DTYPE), b.astype(MXU_DTYPE), dims, preferred_element_type=F32)


def _dot_nt(a, b):
    return _dot(a, b, (((1,), (1,)), ((), ())))


def _dot_tn(a, b):
    return _dot(a, b, (((0,), (0,)), ((), ())))


def _dotx(a, b, dims=(((1,), (0,)), ((), ()))):
    return lax.dot_general(a, b, dims, precision=HI, preferred_element_type=F32)


def _rms(x):
    return x * lax.rsqrt(jnp.mean(x * x, axis=-1, keepdims=True) + EPS)


def _pick(n, prefs):
    for p in prefs:
        if n % p == 0:
            return p
    return n


def _row(arr, width=None, colblk=0, tm=TM, valid=None):
    width = arr.shape[1] if width is None else width
    if valid is None:
        return ([arr], [pl.BlockSpec((tm, width), lambda i, c=colblk: (i, c))], lambda r: r[...].astype(F32), width)
    spec = pl.BlockSpec((tm, width), lambda i, c=colblk: (jnp.minimum(i, valid - 1), c))
    return ([arr], [spec], lambda r: jnp.where(pl.program_id(0) < valid, r[...].astype(F32), 0.0), width)


def _row_grid(arr, a, nb):
    n = arr.shape[0]
    b = (n - CTX) // a

    def load(v_ref, c_ref):
        i = pl.program_id(0)
        return jnp.where(i == nb - 1, c_ref[...], _grid_rows(v_ref, a, i))

    return ([arr.reshape(n // b, b, D), arr], [_grid_spec(a, nb), pl.BlockSpec((TM, D), lambda i: (nb - 1, 0))], load, D)


def _operands(rows):
    return [a for r in rows for a in r[0]], [s for r in rows for s in r[1]]


def _load_rows(refs, rows):
    vals, k = [], 0
    for r in rows:
        vals.append(r[2](*refs[k:k + len(r[0])]))
        k += len(r[0])
    return vals


def _full_spec(p):
    nd = p.ndim
    return pl.BlockSpec(p.shape, lambda i, nd=nd: (0,) * nd)


def _rowwise(name, fn, n_blocks, ctx_blk, rows, params, outs, tm=TM):
    arrs, specs = _operands(rows)
    nr, npar = len(arrs), len(params)

    def body(*refs):
        t = (pl.program_id(0) >= ctx_blk).astype(F32)
        rv = _load_rows(refs[:nr], rows)
        pv = [p[...] for p in refs[nr:nr + npar]]
        res = fn(t, rv, pv)
        for o_ref, o in zip(refs[nr + npar:], res):
            o_ref[...] = o.astype(o_ref.dtype)

    return pl.pallas_call(
        body, name=name, grid=(n_blocks,),
        in_specs=specs + [_full_spec(p) for p in params],
        out_specs=[pl.BlockSpec((tm, w), lambda i: (i, 0)) for w, _ in outs],
        out_shape=[jax.ShapeDtypeStruct((n_blocks * tm, w), dt) for w, dt in outs],
        compiler_params=_cp(("parallel",)),
    )(*arrs, *params)


def _rowwise_vjp(name, fn, n_blocks, ctx_blk, rows, params, douts, row_grads, tm=TM):
    adds = [a for _, _, a in row_grads if a is not None]
    (r_arrs, r_specs), (d_arrs, d_specs), (a_arrs, a_specs) = _operands(rows), _operands(douts), _operands(adds)
    nr, npar, nd, na = len(r_arrs), len(params), len(d_arrs), len(a_arrs)

    def body(*refs):
        i = pl.program_id(0)
        t = (i >= ctx_blk).astype(F32)
        rv = _load_rows(refs[:nr], rows)
        pv = [p[...] for p in refs[nr:nr + npar]]
        dv = _load_rows(refs[nr + npar:nr + npar + nd], douts)
        av = _load_rows(refs[nr + npar + nd:nr + npar + nd + na], adds)
        o_refs = refs[nr + npar + nd + na:]
        _, vjp = jax.vjp(lambda r, p: tuple(fn(t, r, p)), rv, pv)
        d_rows, d_params = vjp(tuple(dv))
        ai = 0
        for o_ref, (ri, _, addend) in zip(o_refs, row_grads):
            g = d_rows[ri]
            if addend is not None:
                g = g + av[ai]
                ai += 1
            o_ref[...] = g.astype(o_ref.dtype)
        p_refs = o_refs[len(row_grads):]

        @pl.when(i == 0)
        def _():
            for p_ref in p_refs:
                p_ref[...] = jnp.zeros_like(p_ref)

        for p_ref, g in zip(p_refs, d_params):
            p_ref[...] += g

    widths = [rows[ri][3] for ri, _, _ in row_grads]
    res = pl.pallas_call(
        body, name=name, grid=(n_blocks,),
        in_specs=r_specs + [_full_spec(p) for p in params] + d_specs + a_specs,
        out_specs=[pl.BlockSpec((tm, w), lambda i: (i, 0)) for w in widths] + [_full_spec(p) for p in params],
        out_shape=[jax.ShapeDtypeStruct((n_blocks * tm, w), dt) for w, (_, dt, _) in zip(widths, row_grads)]
        + [jax.ShapeDtypeStruct(p.shape, F32) for p in params],
        compiler_params=_cp(("arbitrary",)),
    )(*r_arrs, *params, *d_arrs, *a_arrs)
    return res[:len(row_grads)], res[len(row_grads):]


def _mm(name, a, b, mode, out_dtype):
    if mode == "nn":
        m, kk = a.shape
        n = b.shape[1]
    elif mode == "nt":
        m, kk = a.shape
        n = b.shape[0]
    else:
        kk, m = a.shape
        n = b.shape[1]
    if mode == "tn":
        tm = _pick(m, (1024, 1408, 512, 256, 128))
        tn = _pick(n, (768, 512, 256, 128))
        tk = kk
    else:
        tm = _pick(m, (1088, 1024, 768, 512, 384, 256, 128))
        tn = n if n <= 2816 else _pick(n, (1024, 768, 512, 256, 128))
        tk = kk if kk <= 2816 else _pick(kk, (2816, 1792, 1024, 768, 512, 256, 128))
    nk = kk // tk
    in_place = out_dtype == F32
    if mode == "nn":
        specs = [pl.BlockSpec((tm, tk), lambda i, j, k: (i, k)), pl.BlockSpec((tk, tn), lambda i, j, k: (k, j))]
        dims = (((1,), (0,)), ((), ()))
    elif mode == "nt":
        specs = [pl.BlockSpec((tm, tk), lambda i, j, k: (i, k)), pl.BlockSpec((tn, tk), lambda i, j, k: (j, k))]
        dims = (((1,), (1,)), ((), ()))
    else:
        specs = [pl.BlockSpec((tk, tm), lambda i, j, k: (k, i)), pl.BlockSpec((tk, tn), lambda i, j, k: (k, j))]
        dims = (((0,), (0,)), ((), ()))

    def body(a_ref, b_ref, o_ref, *scratch):
        part = lax.dot_general(a_ref[...].astype(MXU_DTYPE), b_ref[...].astype(MXU_DTYPE), dims, preferred_element_type=F32)
        if nk == 1:
            o_ref[...] = part.astype(o_ref.dtype)
        else:
            k = pl.program_id(2)
            acc = o_ref if in_place else scratch[0]

            @pl.when(k == 0)
            def _():
                acc[...] = part

            @pl.when(k > 0)
            def _():
                acc[...] += part

            if not in_place:
                @pl.when(k == nk - 1)
                def _():
                    o_ref[...] = acc[...].astype(o_ref.dtype)

    return pl.pallas_call(
        body, name=name, grid=(m // tm, n // tn, nk), in_specs=specs,
        out_specs=pl.BlockSpec((tm, tn), lambda i, j, k: (i, j)),
        out_shape=jax.ShapeDtypeStruct((m, n), out_dtype),
        scratch_shapes=[] if nk == 1 or in_place else [pltpu.VMEM((tm, tn), F32)],
        compiler_params=_cp(("parallel", "parallel", "arbitrary")),
    )(a, b)


def _sel_mod(modp, t):
    return modp[0:8] * (1.0 - t) + modp[8:16] * t


def _fn_prenorm(t, rows, params, *, a, b):
    (x,), (g, modp) = rows, params
    m = _sel_mod(modp, t)
    return ((_rms(x) * g) * (1.0 + m[b:b + 1]) + m[a:a + 1],)


def _fn_resid_prenorm(t, rows, params, *, gi, a, b):
    (x, y), (g, mod_a, mod_b) = rows, params
    ma, mb = _sel_mod(mod_a, t), _sel_mod(mod_b, t)
    xn = x + ma[gi:gi + 1] * y
    return xn, (_rms(xn) * g) * (1.0 + mb[b:b + 1]) + mb[a:a + 1]


def _fn_resid(t, rows, params, *, gi):
    (x, y), (mod_a,) = rows, params
    return (x + _sel_mod(mod_a, t)[gi:gi + 1] * y,)


def _fn_swiglu(t, rows, params):
    (pf,) = rows
    return (jax.nn.silu(pf[:, :D_FF]) * pf[:, D_FF:],)


def _fn_mixpost(t, rows, params):
    (o, r, u, g), (gla_g, vn_g, sw, sb_t) = rows, params
    a =jnp.concatenate([_rms(o[:, h * GLA_DV:(h + 1) * GLA_DV]) for h in range(GLA_H)], axis=1) * gla_g * jax.nn.silu(r)
    uu, vv = jax.nn.gelu(u), jax.nn.gelu(g)
    mu = jnp.mean(vv, axis=-1, keepdims=True)
    var = jnp.mean(jnp.square(vv - mu), axis=-1, keepdims=True)
    vn = ((vv - mu) * lax.rsqrt(var + EPS)) * vn_g
    s = jnp.concatenate(
        [_dot(sw[gi * GMLP_L:(gi + 1) * GMLP_L, :], vn[:, gi * GMLP_C:(gi + 1) * GMLP_C]) + sb_t[:, gi:gi + 1]
         for gi in range(GMLP_G)], axis=1)
    return (jnp.concatenate([a, uu * s], axis=1),)


def _head_expand():
    r = lax.broadcasted_iota(jnp.int32, (SSD_H, SSD_INNER), 0)
    c = lax.broadcasted_iota(jnp.int32, (SSD_H, SSD_INNER), 1)
    return (c // SSD_P == r).astype(F32)


def _fn_ssd_finish(t, rows, params):
    (y2, xs, z), (d_skip, norm_g) = rows, params
    d_full = _dotx(jnp.broadcast_to(d_skip, (8, SSD_H)), _head_expand())[0:1]
    y = (y2 + d_full * xs) * jax.nn.silu(z)
    gw = SSD_INNER // SSD_G
    return (jnp.concatenate([_rms(y[:, gi * gw:(gi + 1) * gw]) for gi in range(SSD_G)], axis=1) * norm_g,)


def _fn_concat(t, rows, params, *, sums, pad=0):
    out, i = [], 0
    for n in sums:
        acc = rows[i]
        for j in range(1, n):
            acc = acc + rows[i + j]
        out.append(acc)
        i += n
    if pad:
        out.append(jnp.zeros((out[0].shape[0], pad), F32))
    return (jnp.concatenate(out, axis=1),)


def _tri(n, rev):
    r = lax.broadcasted_iota(jnp.int32, (n, n), 0)
    c = lax.broadcasted_iota(jnp.int32, (n, n), 1)
    return (r <= c) if rev else (r >= c)


def _gla_chunk(S, v, k, q, tail, gw, gb, *, rev):
    L, H = GLA_L, GLA_H
    msk = _tri(L, rev)
    tri = msk.astype(F32)
    lr = tail[:, GLA_LR:2 * GLA_LR] if rev else tail[:, 0:GLA_LR]
    la = jax.nn.log_sigmoid(_dot(lr, gw) + gb) / GLA_TAU
    b = _dotx(tri, la)
    b_last = b[0:1] if rev else b[L - 1:L]
    kd = k * jnp.exp(b_last - b)
    qd = (q * GLA_DK ** -0.5) * jnp.exp(b)
    ki = k * jnp.exp(-b)

    def same_head(shape, rows_per_head, cols_per_head):
        r = lax.broadcasted_iota(jnp.int32, shape, 0) // rows_per_head
        c = lax.broadcasted_iota(jnp.int32, shape, 1) // cols_per_head
        return r == c

    k_blk = jnp.where(same_head((H * L, H * GLA_DK), L, GLA_DK), jnp.concatenate([ki] * H, axis=0), 0.0)
    v_blk = jnp.where(same_head((H * L, H * GLA_DV), L, GLA_DV), jnp.concatenate([v] * H, axis=0), 0.0)
    row = lax.broadcasted_iota(jnp.int32, (L, H * L), 0)
    src = lax.broadcasted_iota(jnp.int32, (L, H * L), 1) % L
    sc = jnp.where((row <= src) if rev else (row >= src), _dot_nt(qd, k_blk), 0.0)
    o = _dot_nt(qd, S) + _dot(sc, v_blk)
    s_new = S * jnp.exp(b_last) + jnp.where(same_head(S.shape, GLA_DV, GLA_DK), _dot_tn(v, kd), 0.0)
    return s_new, o


def _ssd_chunk(S, x, bm, cm, tail, dtb, alog, *, rev):
    L = SSD_L
    msk = _tri(L, rev)
    tri = msk.astype(F32)
    raw = tail[:, SSD_H:2 * SSD_H] if rev else tail[:, 0:SSD_H]
    dt = jax.nn.softplus(raw + dtb)
    dta = dt * (-jnp.exp(alog))
    acum = _dotx(tri, dta)
    a_last = acum[0:1] if rev else acum[L - 1:L]
    wst = dt * jnp.exp(a_last - acum)
    eac = jnp.exp(acum)
    tr = jnp.concatenate([acum, dt, wst, jnp.zeros((L, L - 3 * SSD_H), F32)], axis=1).T
    acum_t, dt_t, wst_t = tr[0:SSD_H], tr[SSD_H:2 * SSD_H], tr[2 * SSD_H:3 * SSD_H]
    decrow = jnp.exp(_dotx(jnp.broadcast_to(a_last, (8, SSD_H)), _head_expand())[0:1])
    lane = lax.broadcasted_iota(jnp.int32, (1, 2 * SSD_P), 1)
    m0 = (lane < SSD_P).astype(F32)
    m1 = 1.0 - m0
    pairs_per_group = SSD_H // SSD_G // 2
    y_parts, s_parts = [], []
    for g in range(SSD_G):
        ns = slice(g * SSD_N, (g + 1) * SSD_N)
        bg, cg = bm[:, ns], cm[:, ns]
        cb = _dot_nt(cg, bg)
        bgt = bg.T
        for jj in range(pairs_per_group):
            j = g * pairs_per_group + jj
            ls = slice(j * 2 * SSD_P, (j + 1) * 2 * SSD_P)
            xp, sp = x[:, ls], S[:, ls]
            xm = jnp.concatenate([xp * m0, xp * m1], axis=0)
            sm = jnp.concatenate([sp * m0, sp * m1], axis=0)
            lhs, bw = [], []
            for h in (2 * j, 2 * j + 1):
                seg = acum[:, h:h + 1] - acum_t[h:h + 1, :]
                lhs.append(cb * jnp.exp(jnp.where(msk, seg, -jnp.inf)) * dt_t[h:h + 1, :])
                bw.append(bgt * wst_t[h:h + 1, :])
            lhs += [cg * eac[:, h:h + 1] for h in (2 * j, 2 * j + 1)]
            y_parts.append(_dot(jnp.concatenate(lhs, axis=1), jnp.concatenate([xm, sm], axis=0)))
            s_parts.append(sp * decrow[:, ls] + _dot(jnp.concatenate(bw, axis=1), xm))
    return jnp.concatenate(s_parts, axis=1), jnp.concatenate(y_parts, axis=1)


def _multi_chunk(chunk_fn, L, subs, nr):
    def fn(S, *args, rev):
        rows, params = args[:nr], args[nr:]
        ys = [None] * subs
        for j in (range(subs - 1, -1, -1) if rev else range(subs)):
            S, ys[j] = chunk_fn(S, *[r[j * L:(j + 1) * L] for r in rows], *params, rev=rev)
        return S, jnp.concatenate(ys, axis=0)

    return fn


def _scan_order(n, nx, rev, backward):
    nc = n - nx

    def fwd(s):
        return (n - 1 - s) if rev else jnp.where(s < nc, s + nx, s - nc)

    return (lambda s: fwd(n - 1 - s)) if backward else fwd


def _scan_fwd(name, chunk_fn, L, n, nx, rev, rows, params, state_shape, out_w, addend=None):
    order = _scan_order(n, nx, rev, False)
    nr, npar = len(rows), len(params)
    adds = [] if addend is None else [addend]

    def body(*refs):
        s_scr = refs[-1]

        @pl.when(pl.program_id(0) == 0)
        def _():
            s_scr[...] = jnp.zeros_like(s_scr)

        s_in = s_scr[...]
        y_ref, st_ref = refs[nr + npar + len(adds)], refs[nr + npar + len(adds) + 1]
        st_ref[0] = s_in
        s_new, y = chunk_fn(s_in, *[r[...] for r in refs[:nr]], *[p[...] for p in refs[nr:nr + npar]], rev=rev)
        y_ref[...] = y + refs[nr + npar][...] if adds else y
        s_scr[...] = s_new

    return pl.pallas_call(
        body, name=name, grid=(n,),
        in_specs=[pl.BlockSpec((L, w), lambda s, c=c: (order(s), c)) for _, w, c in rows] + [_full_spec(p) for p in params]
        + [pl.BlockSpec((L, out_w), lambda s: (order(s), 0)) for _ in adds],
        out_specs=[pl.BlockSpec((L, out_w), lambda s: (order(s), 0)),
                   pl.BlockSpec((1,) + state_shape, lambda s: (order(s), 0, 0))],
        out_shape=[jax.ShapeDtypeStruct((n * L, out_w), F32), jax.ShapeDtypeStruct((n,) + state_shape, F32)],
        scratch_shapes=[pltpu.VMEM(state_shape, F32)],
        compiler_params=_cp(("arbitrary",)),
    )(*[a for a, _, _ in rows], *params, *adds)


def _scan_bwd(name, chunk_fn, L, n, nx, rev, rows, params, states, dy, state_shape, out_w, addends=None):
    order = _scan_order(n, nx, rev, True)
    dy_blocks = dy.shape[0] // L
    nr, npar = len(rows), len(params)
    adds = [] if addends is None else list(addends)

    def body(*refs):
        i = pl.program_id(0)
        ds_scr = refs[-1]
        rv = [r[...] for r in refs[:nr]]
        pv = [p[...] for p in refs[nr:nr + npar]]
        st_ref, dy_ref = refs[nr + npar], refs[nr + npar + 1]
        a_refs = refs[nr + npar + 2:nr + npar + 2 + len(adds)]
        o_refs = refs[nr + npar + 2 + len(adds):-1]
        p_refs = o_refs[nr:]

        @pl.when(i == 0)
        def _():
            ds_scr[...] = jnp.zeros_like(ds_scr)
            for p_ref in p_refs:
                p_ref[...] = jnp.zeros_like(p_ref)

        _, vjp = jax.vjp(functools.partial(chunk_fn, rev=rev), st_ref[0], *rv, *pv)
        dy_blk = jnp.where(order(i) < dy_blocks, dy_ref[...].astype(F32), 0.0)
        grads = vjp((ds_scr[...], dy_blk))
        ds_scr[...] = grads[0]
        for j, (o_ref, g) in enumerate(zip(o_refs[:nr], grads[1:1 + nr])):
            o_ref[...] = g + a_refs[j][...] if adds else g
        for p_ref, g in zip(p_refs, grads[1 + nr:]):
            p_ref[...] += g

    row_specs = [pl.BlockSpec((L, w), lambda s: (order(s), 0)) for _, w, _ in rows]
    res = pl.pallas_call(
        body, name=name, grid=(n,),
        in_specs=[pl.BlockSpec((L, w), lambda s, c=c: (order(s), c)) for _, w, c in rows] + [_full_spec(p) for p in params]
        + [pl.BlockSpec((1,) + state_shape, lambda s: (order(s), 0, 0)),
           pl.BlockSpec((L, out_w), lambda s: (jnp.minimum(order(s), dy_blocks - 1), 0))]
        + row_specs[:len(adds)],
        out_specs=row_specs + [_full_spec(p) for p in params],
        out_shape=[jax.ShapeDtypeStruct((n * L, w), F32) for _, w, _ in rows] + [jax.ShapeDtypeStruct(p.shape, F32) for p in params],
        scratch_shapes=[pltpu.VMEM(state_shape, F32)],
        compiler_params=_cp(("arbitrary",)),
    )(*[a for a, _, _ in rows], *params, states, dy, *adds)
    return res[:nr], res[nr:]


CONV_W = 1024
CONV_COLBLK = (0, 1, 4)


def _conv_specs(nb, src_blk):
    halo = TM // 8
    return [pl.BlockSpec((TM, CONV_W), lambda j, i: (i, src_blk(j))),
            pl.BlockSpec((8, CONV_W), lambda j, i: (jnp.maximum(i * halo - 1, 0), src_blk(j))),
            pl.BlockSpec((8, CONV_W), lambda j, i: (jnp.minimum(i * halo + halo, nb * halo - 1), src_blk(j)))]


def _conv_ext(i, nb, cur, prev, nxt):
    has_prev = jnp.logical_and(i > 0, i < nb - 1)
    has_next = i < nb - 2
    return jnp.concatenate([jnp.where(has_prev, prev, 0.0), cur, jnp.where(has_next, nxt, 0.0)], axis=0)


def _conv_taps(ext, w, flip):
    acc = None
    for j in range(SSD_K):
        wj = w[SSD_K - 1 - j:SSD_K - j, :] if flip else w[j:j + 1, :]
        term = wj * ext[6 + j:6 + j + TM, :]
        acc = term if acc is None else acc + term
    return acc


def _conv(name, src, w8, b1, nb, *, permuted_src, act, flip, out_dtype):
    src_blk = (lambda j: jnp.where(j == 2, CONV_COLBLK[2], j)) if permuted_src else (lambda j: j)

    def body(cur, prev, nxt, w_ref, b_ref, o_ref):
        ext = _conv_ext(pl.program_id(1), nb, cur[...].astype(F32), prev[...].astype(F32), nxt[...].astype(F32))
        acc = _conv_taps(ext, w_ref[...], flip)
        if act:
            acc = jax.nn.silu(acc + b_ref[...])
        o_ref[...] = acc.astype(o_ref.dtype)

    return pl.pallas_call(
        body, name=name, grid=(3, nb),
        in_specs=_conv_specs(nb, src_blk) + [pl.BlockSpec((8, CONV_W), lambda j, i: (0, j)), pl.BlockSpec((1, CONV_W), lambda j, i: (0, j))],
        out_specs=pl.BlockSpec((TM, CONV_W), lambda j, i: (i, j)),
        out_shape=jax.ShapeDtypeStruct((nb * TM, 3 * CONV_W), out_dtype),
        compiler_params=_cp(("parallel", "parallel")),
    )(src, src, src, w8, b1)


def _conv_bwd_pre(name, p1, w8, b1, dxbc_parts, nb):
    src_blk = lambda j: jnp.where(j == 2, CONV_COLBLK[2], j)
    xs_parts, bc_parts = dxbc_parts
    n_x, n_bc = len(xs_parts), len(bc_parts)
    x_blocks = [p.shape[0] // TM for p in xs_parts]

    def body(*refs):
        cur, prev, nxt, w_ref, b_ref = refs[:5]
        d_refs = refs[5:5 + n_x + n_bc]
        da_ref, dw_ref, db_ref = refs[5 + n_x + n_bc:]
        j, i = pl.program_id(0), pl.program_id(1)
        ext = _conv_ext(i, nb, cur[...], prev[...], nxt[...])
        acc = _conv_taps(ext, w_ref[...], False) + b_ref[...]
        dx = d_refs[0][...]
        for r, blocks in zip(d_refs[1:n_x], x_blocks[1:]):
            dx = dx + jnp.where(i < blocks, r[...], 0.0)
        dbc = jnp.concatenate([d_refs[n_x][...], d_refs[n_x + 1][...]], axis=1)
        dy = jnp.where(j == 2, dbc, dx)
        sg = jax.nn.sigmoid(acc)
        da = dy * (sg + acc * sg * (1.0 - sg))
        da_ref[...] = da

        @pl.when(i == 0)
        def _():
            dw_ref[...] = jnp.zeros_like(dw_ref)
            db_ref[...] = jnp.zeros_like(db_ref)

        rows = [jnp.sum(da * ext[6 + t:6 + t + TM, :], axis=0, keepdims=True) for t in range(SSD_K)]
        dw_ref[...] += jnp.concatenate(rows + [jnp.zeros((8 - SSD_K, CONV_W), F32)], axis=0)
        db_ref[...] += jnp.sum(da, axis=0, keepdims=True)

    x_specs = [pl.BlockSpec((TM, CONV_W), lambda j, i, b=b: (jnp.minimum(i, b - 1), jnp.minimum(j, 1))) for b in x_blocks]
    bc_specs = [pl.BlockSpec((TM, 512), lambda j, i: (i, 0)) for _ in bc_parts]
    return pl.pallas_call(
        body, name=name, grid=(3, nb),
        in_specs=_conv_specs(nb, src_blk) + [pl.BlockSpec((8, CONV_W), lambda j, i: (0, j)), pl.BlockSpec((1, CONV_W), lambda j, i: (0, j))]
        + x_specs + bc_specs,
        out_specs=[pl.BlockSpec((TM, CONV_W), lambda j, i: (i, j)), pl.BlockSpec((8, CONV_W), lambda j, i: (0, j)),
                   pl.BlockSpec((1, CONV_W), lambda j, i: (0, j))],
        out_shape=[jax.ShapeDtypeStruct((nb * TM, 3 * CONV_W), F32), jax.ShapeDtypeStruct((8, 3 * CONV_W), F32),
                   jax.ShapeDtypeStruct((1, 3 * CONV_W), F32)],
        compiler_params=_cp(("arbitrary", "arbitrary")),
    )(p1, p1, p1, w8, b1, *xs_parts, *bc_parts)


def _grid_block(a):
    nbv = TM // a
    blk_b = max(nbv, 8)
    return nbv, blk_b, blk_b // nbv


def _grid_spec(a, nb):
    _, blk_b, per = _grid_block(a)
    return pl.BlockSpec((a, blk_b, D), lambda i: (0, jnp.minimum(i, nb - 2) // per, 0))


def _grid_rows(v_ref, a, i):
    nbv, _, per = _grid_block(a)

    def pick(ph):
        return jnp.concatenate([v_ref[:, ph * nbv + t, :] for t in range(nbv)], axis=0)

    out = pick(0)
    for ph in range(1, per):
        out = jnp.where(i % per == ph, pick(ph), out)
    return out


def _loss_head(x, f, target, modp, g_final, rows_r):
    tview = target.reshape(rows_r, target.shape[0] // rows_r, D)
    nb = x.shape[0] // TM + 1

    def fn(x_, f_, tgt, modp_, g_):
        xn = x_ + _sel_mod(modp_, 0.0)[5:6] * f_
        err = _rms(xn) * g_ - tgt
        return 0.5 * jnp.sum(jnp.mean(err * err, axis=-1))

    def body(x_ref, f_ref, t_ref, m_ref, g_ref, l_ref, dx_ref, df_ref, dm_ref, dg_ref):
        i = pl.program_id(0)
        tgt = _grid_rows(t_ref, rows_r, i)
        l, vjp = jax.vjp(lambda a_, b_, c_, d_: fn(a_, b_, tgt, c_, d_), x_ref[...], f_ref[...], m_ref[...], g_ref[...])
        dx, df, dm, dg = vjp(jnp.ones((), F32))

        @pl.when(i == 0)
        def _():
            l_ref[...] = jnp.zeros_like(l_ref)
            dm_ref[...] = jnp.zeros_like(dm_ref)
            dg_ref[...] = jnp.zeros_like(dg_ref)

        l_ref[...] += jnp.reshape(l, (1, 1))
        dx_ref[...] = dx
        df_ref[...] = df.astype(df_ref.dtype)
        dm_ref[...] += dm
        dg_ref[...] += dg

    rowspec = pl.BlockSpec((TM, D), lambda i: (i, 0))
    return pl.pallas_call(
        body, name="loss_head", grid=(nb - 1,),
        in_specs=[rowspec, rowspec, _grid_spec(rows_r, nb), _full_spec(modp), _full_spec(g_final)],
        out_specs=[pl.BlockSpec((1, 1), lambda i: (0, 0)), rowspec, rowspec, _full_spec(modp), _full_spec(g_final)],
        out_shape=[jax.ShapeDtypeStruct((1, 1), F32), jax.ShapeDtypeStruct(x.shape, F32), jax.ShapeDtypeStruct(x.shape, MXU_DTYPE),
                   jax.ShapeDtypeStruct(modp.shape, F32), jax.ShapeDtypeStruct(g_final.shape, F32)],
        compiler_params=_cp(("arbitrary",)),
    )(x, f, tview, modp, g_final)


def _repack(name, shards, segs, wp):
    nd, kk, ws = shards.shape
    tr = 128
    used = sum(e - s for s, e in segs)

    def body(a_ref, o_ref):
        full = jnp.concatenate([a_ref[d].astype(F32) for d in range(nd)], axis=1)
        parts = [full[:, s:e] for s, e in segs]
        if wp > used:
            parts.append(jnp.zeros((tr, wp - used), F32))
        o_ref[...] = jnp.concatenate(parts, axis=1).astype(o_ref.dtype)

    return pl.pallas_call(
        body, name=name, grid=(kk // tr,),
        in_specs=[pl.BlockSpec((nd, tr, ws), lambda i: (0, i, 0))],
        out_specs=pl.BlockSpec((tr, wp), lambda i: (i, 0)),
        out_shape=jax.ShapeDtypeStruct((kk, wp), MXU_DTYPE),
        compiler_params=_cp(("parallel",)),
    )(shards)


def _unpack(name, dw, segs, ws, out_dtype):
    kk, wp = dw.shape
    tr = 128
    order = sorted(range(len(segs)), key=lambda i: segs[i][0])
    offs, o = [], 0
    for s, e in segs:
        offs.append(o)
        o += e - s

    def body(a_ref, o_ref):
        a = a_ref[...].astype(F32)
        full = jnp.concatenate([a[:, offs[i]:offs[i] + segs[i][1] - segs[i][0]] for i in order], axis=1)
        for d in range(NDEV):
            o_ref[d] = full[:, d * ws:(d + 1) * ws].astype(o_ref.dtype)

    return pl.pallas_call(
        body, name=name, grid=(kk // tr,),
        in_specs=[pl.BlockSpec((tr, wp), lambda i: (i, 0))],
        out_specs=pl.BlockSpec((NDEV, tr, ws), lambda i: (0, i, 0)),
        out_shape=jax.ShapeDtypeStruct((NDEV, kk, ws), out_dtype),
        compiler_params=_cp(("parallel",)),
    )(dw)


def _adam_math(w, g, m, v):
    m = ADAM_B1 * m + (1.0 - ADAM_B1) * g
    v = ADAM_B2 * v + (1.0 - ADAM_B2) * jnp.square(g)
    m_hat = m / (1.0 - ADAM_B1 ** ADAM_STEP)
    v_hat = v / (1.0 - ADAM_B2 ** ADAM_STEP)
    delta = -ADAM_LR * (m_hat / (jnp.sqrt(v_hat) + ADAM_EPS) + ADAM_WD * w)
    return delta, m, v


def _adam(name, w, parts, m, v, after):
    r, c = w.shape
    nsec, npart = len(parts), parts[0].shape[0]
    rs = r // nsec
    tr = _pick(rs, (256, 128, 64, 32, 16, 8)) if rs * c * 4 > (1 << 20) else rs
    tiles = rs // tr

    def body(w_ref, *refs):
        m_ref, v_ref, _, g_ref, d_ref, nm_ref, nv_ref = refs[nsec:]
        sec = pl.program_id(0) // tiles
        g = None
        for a, p_ref in enumerate(refs[:nsec]):
            ga = p_ref[0].astype(F32)
            for s in range(1, npart):
                ga = ga + p_ref[s].astype(F32)
            g = ga if g is None else jnp.where(sec == a, ga, g)
        delta, nm, nv = _adam_math(w_ref[...], g, m_ref[...], v_ref[...])
        g_ref[...], d_ref[...], nm_ref[...], nv_ref[...] = g, delta, nm, nv

    spec = pl.BlockSpec((tr, c), lambda i: (i, 0))
    part_specs = [pl.BlockSpec((npart, tr, c), lambda i, a=a: (0, jnp.clip(i - a * tiles, 0, tiles - 1), 0)) for a in range(nsec)]
    return pl.pallas_call(
        body, name=name, grid=(r // tr,),
        in_specs=[spec] + part_specs + [spec, spec, ANY],
        out_specs=[spec] * 4, out_shape=[jax.ShapeDtypeStruct((r, c), F32)] * 4,
        compiler_params=_cp(("parallel",)),
    )(w, *parts, m, v, after)


def _mod_fwd(c_all, mod_w):
    nl, _, ws = mod_w.shape

    def body(c_ref, w_ref, o_ref):
        o_ref[0] = _dot(jax.nn.silu(c_ref[...]), w_ref[0])

    return pl.pallas_call(
        body, name="mod_fwd", grid=(nl,),
        in_specs=[_full_spec(c_all), pl.BlockSpec((1, D, ws), lambda i: (i, 0, 0))],
        out_specs=pl.BlockSpec((1, 16, ws), lambda i: (i, 0, 0)),
        out_shape=jax.ShapeDtypeStruct((nl, 16, ws), F32),
        compiler_params=_cp(("parallel",)),
    )(c_all, mod_w)


def _mod_bwd(c_all, mod_w, dm):
    nl, _, ws = mod_w.shape

    def body(c_ref, w_ref, d_ref, dw_ref, dc_ref):
        dw_ref[0] = _dot_tn(jax.nn.silu(c_ref[...]), d_ref[0])
        dc_ref[0] = _dot_nt(d_ref[0], w_ref[0])

    return pl.pallas_call(
        body, name="mod_bwd", grid=(nl,),
        in_specs=[_full_spec(c_all), pl.BlockSpec((1, D, ws), lambda i: (i, 0, 0)), pl.BlockSpec((1, 16, ws), lambda i: (i, 0, 0))],
        out_specs=[pl.BlockSpec((1, D, ws), lambda i: (i, 0, 0)), pl.BlockSpec((1, 16, D), lambda i: (i, 0, 0))],
        out_shape=[jax.ShapeDtypeStruct((nl, D, ws), F32), jax.ShapeDtypeStruct((nl, 16, D), F32)],
        compiler_params=_cp(("parallel",)),
    )(c_all, mod_w, dm)


def _sum_parts(name, parts):
    npart, r, c = parts.shape

    def body(p_ref, o_ref):
        g = p_ref[0].astype(F32)
        for s in range(1, npart):
            g = g + p_ref[s].astype(F32)
        o_ref[...] = g

    return pl.pallas_call(body, name=name, out_shape=jax.ShapeDtypeStruct((r, c), F32), compiler_params=_cp())(parts)


MESH = pl.DeviceIdType.MESH
ANY = pl.BlockSpec(memory_space=pl.ANY)
N_PEERS = NDEV - 1


def _mesh_pos():
    return lax.axis_index("x"), lax.axis_index("y"), lax.axis_index("c")


def _slot(px, py, pc):
    return 4 * px + 2 * py + pc


def _two_level_gather(x_refs, o_refs, send_sems, recv_sems, local_sems):
    x, y, c = _mesh_pos()
    me, sibling = (x, y, c), (x, y, 1 - c)
    chips = [(1 - x, y), (x, 1 - y), (1 - x, 1 - y)]
    n = len(x_refs)

    def copy(a, k, block, to, src=None):
        dst = o_refs[a].at[_slot(*block)]
        return pltpu.make_async_remote_copy(src_ref=dst if src is None else src, dst_ref=dst, send_sem=send_sems.at[a, k],
                                            recv_sem=recv_sems.at[a, k], device_id=to, device_id_type=MESH)

    mine = [pltpu.make_async_copy(x_refs[a], o_refs[a].at[_slot(*me)], local_sems.at[a]) for a in range(n)]
    for cp in mine:
        cp.start()
    first = []
    for a in range(n):
        first.append(copy(a, 0, me, sibling, src=x_refs[a]))
        first += [copy(a, 1 + j, me, (*chip, c), src=x_refs[a]) for j, chip in enumerate(chips)]
    for cp in first:
        cp.start()
    passed = []
    for j, chip in enumerate(chips):
        for a in range(n):
            copy(a, 1 + j, (*chip, c), me).wait_recv()
            fwd = copy(a, 4 + j, (*chip, c), sibling)
            fwd.start()
            passed.append(fwd)
    for a in range(n):
        copy(a, 0, sibling, me).wait_recv()
        for j, chip in enumerate(chips):
            copy(a, 4 + j, (*chip, 1 - c), me).wait_recv()
    for cp in first + passed:
        cp.wait_send()
    for cp in mine:
        cp.wait()


def _ag_small(name, x):
    r, c = x.shape

    def body(x_ref, o_ref, send_sems, recv_sems, local_sems):
        _two_level_gather([x_ref], [o_ref], send_sems, recv_sems, local_sems)

    return pl.pallas_call(
        body, name=name, out_shape=jax.ShapeDtypeStruct((NDEV, r, c), x.dtype),
        in_specs=[pl.BlockSpec(memory_space=pltpu.VMEM)], out_specs=pl.BlockSpec(memory_space=pltpu.VMEM),
        scratch_shapes=[pltpu.SemaphoreType.DMA((1, N_PEERS)), pltpu.SemaphoreType.DMA((1, N_PEERS)), pltpu.SemaphoreType.DMA((1,))],
        compiler_params=pltpu.CompilerParams(vmem_limit_bytes=VMEM_LIMIT),
    )(x)


def _ag_big(name, shards):
    n = len(shards)

    def body(*refs):
        _two_level_gather(refs[:n], refs[n:2 * n], *refs[2 * n:])

    return pl.pallas_call(
        body, name=name, out_shape=[jax.ShapeDtypeStruct((NDEV,) + s.shape, s.dtype) for s in shards],
        in_specs=[ANY] * n, out_specs=[ANY] * n,
        scratch_shapes=[pltpu.SemaphoreType.DMA((n, N_PEERS)), pltpu.SemaphoreType.DMA((n, N_PEERS)), pltpu.SemaphoreType.DMA((n,))],
    )(*shards)


HBM = pl.BlockSpec(memory_space=pltpu.HBM)
SEM = pl.BlockSpec(memory_space=pltpu.SEMAPHORE)
EFFECT = pltpu.SideEffectType.DATAFLOW_SIDE_EFFECTING


def _peers(x, y, c):
    return [(k - 1, ((1 - x) if k & 4 else x, (1 - y) if k & 2 else y, (1 - c) if k & 1 else c)) for k in range(1, NDEV)]


def _xchg_copy(src_refs, land_refs, send_sems, recv_sems, a, k, peer, me, scatter):
    src = src_refs[a].at[_slot(*peer)] if scatter else src_refs[a]
    return pltpu.make_async_remote_copy(src_ref=src, dst_ref=land_refs[a].at[me], send_sem=send_sems.at[a * N_PEERS + k],
                                        recv_sem=recv_sems.at[a * N_PEERS + k], device_id=peer, device_id_type=MESH)


def _xchg_start(name, srcs, lands, deps, scatter):
    n, nd = len(srcs), len(deps)

    def body(*refs):
        src_refs, land_refs = refs[:n], refs[n:2 * n]
        send_sems, recv_sems, token = refs[2 * n + nd], refs[2 * n + nd + 1], refs[-1]
        x, y, c = _mesh_pos()
        me = _slot(x, y, c)
        for k, peer in _peers(x, y, c):
            for a in range(n):
                _xchg_copy(src_refs, land_refs, send_sems, recv_sems, a, k, peer, me, scatter).start()
        token[...] = jnp.zeros_like(token)

    res = pl.pallas_call(
        body, name=name,
        out_shape=(pltpu.SemaphoreType.DMA((n * N_PEERS,)), pltpu.SemaphoreType.DMA((n * N_PEERS,)),
                   *[pltpu.HBM(s.shape, s.dtype) for s in srcs], *[pltpu.HBM(s.shape, s.dtype) for s in lands],
                   jax.ShapeDtypeStruct((8, 128), F32)),
        in_specs=[HBM] * (2 * n) + [ANY] * nd,
        out_specs=(SEM, SEM, *([HBM] * (2 * n)), pl.BlockSpec(memory_space=pltpu.VMEM)),
        input_output_aliases={i: 2 + i for i in range(2 * n)},
        compiler_params=pltpu.CompilerParams(has_side_effects=EFFECT),
    )(*[pltpu.with_memory_space_constraint(s, pltpu.HBM) for s in srcs],
      *[pltpu.with_memory_space_constraint(s, pltpu.HBM) for s in lands], *deps)
    return res[0], res[1], res[2:2 + n], res[2 + n:2 + 2 * n], res[-1]


def _xchg_wait(name, send_sems, recv_sems, srcs, lands, after, scatter):
    n = len(srcs)

    def body(*refs):
        src_refs, land_refs = refs[:n], refs[n:2 * n]
        s_sems, r_sems = refs[2 * n], refs[2 * n + 1]
        x, y, c = _mesh_pos()
        me = _slot(x, y, c)
        for k, peer in _peers(x, y, c):
            for a in range(n):
                cp = _xchg_copy(src_refs, land_refs, s_sems, r_sems, a, k, peer, me, scatter)
                cp.wait_send()
                cp.wait_recv()

    res = pl.pallas_call(
        body, name=name,
        out_shape=[pltpu.HBM(s.shape, s.dtype) for s in srcs] + [pltpu.HBM(s.shape, s.dtype) for s in lands],
        in_specs=[HBM] * (2 * n) + [SEM, SEM, ANY], out_specs=[HBM] * (2 * n),
        input_output_aliases={i: i for i in range(2 * n)},
        compiler_params=pltpu.CompilerParams(has_side_effects=EFFECT),
    )(*srcs, *lands, send_sems, recv_sems, after)
    return res[n:]


def _landing(name, srcs, me, scatter):
    shapes = [s.shape[-2:] for s in srcs]

    def body(me_ref, *refs):
        for s_ref, o_ref in zip(refs[:len(srcs)], refs[len(srcs):]):
            o_ref[...] = s_ref[...].reshape(o_ref.shape)

    def slot_spec(r, c):
        return pl.BlockSpec((1, r, c), lambda i, me_ref: (me_ref[0], 0, 0))

    return pl.pallas_call(
        body, name=name, out_shape=[jax.ShapeDtypeStruct((NDEV, r, c), s.dtype) for s, (r, c) in zip(srcs, shapes)],
        grid_spec=pltpu.PrefetchScalarGridSpec(
            num_scalar_prefetch=1, grid=(1,),
            in_specs=[slot_spec(r, c) if scatter else pl.BlockSpec((r, c), lambda i, me_ref: (0, 0)) for r, c in shapes],
            out_specs=[slot_spec(r, c) for r, c in shapes]),
        compiler_params=_cp(("arbitrary",)),
    )(jnp.reshape(me, (1,)).astype(jnp.int32), *srcs)


STAGES = ("l0_mixer", "l0_ffn", "l1_mixer", "l1_ffn")
STAGE_LAYOUT = {"l0_mixer": (AB_SEGS, AB_P), "l1_mixer": (SSD_SEGS, SSD_P_W)}


class _Exchange:
    def __init__(self, shards, me):
        self.shards, self.me = shards, me
        self.pending, self.pending_grads, self.recv = None, None, {}

    def _layout(self, stage):
        ws = self.shards[stage][0].shape[-1]
        return STAGE_LAYOUT.get(stage, (((0, NDEV * ws),), NDEV * ws)) + (ws,)

    def _start_gather(self, stage, deps):
        srcs = list(self.shards[stage])
        lands = _landing("own_" + stage, srcs, self.me, False)
        return _xchg_start("gather_start_" + stage, srcs, lands, deps, False)

    def get(self, stage, dep, thread):
        i = STAGES.index(stage)
        if i == 0:
            g_in, g_out = _ag_big("gather_" + stage, list(self.shards[stage]))
            deps = [g_out, dep]
        else:
            ss, rs, srcs, lands, _ = self.pending
            g_in, g_out = _xchg_wait("gather_wait_" + stage, ss, rs, srcs, lands, dep, False)
            self.pending, deps = None, [g_out]
        if i + 1 < len(STAGES):
            self.pending = self._start_gather(STAGES[i + 1], deps)
            thread = thread + self.pending[4][0, 0]
        segs, wp, _ = self._layout(stage)
        return _repack("repack_" + stage, g_in, segs, wp), g_out.reshape(-1, D), thread

    def put(self, stage, d_in, d_out, thread):
        segs, _, ws = self._layout(stage)
        parts = [_unpack("unpack_" + stage, d_in, segs, ws, MXU_DTYPE), d_out.reshape(NDEV, -1, D)]
        deps = [parts[0]]
        if self.pending_grads is not None:
            deps = [self.finish(parts[0])[0]]
        self.staged = (stage, parts)
        return thread if stage == STAGES[0] else thread + self.start_last(deps)[0, 0]

    def start_last(self, deps):
        stage, parts = self.staged
        lands = _landing("own_grad_" + stage, parts, self.me, True)
        self.pending_grads = (stage,) + _xchg_start("scatter_start_" + stage, parts, lands, deps, True)
        return self.pending_grads[5]

    def finish(self, after):
        stage, ss, rs, srcs, lands, _ = self.pending_grads
        self.recv[stage] = _xchg_wait("scatter_wait_" + stage, ss, rs, srcs, lands, after, True)
        self.pending_grads = None
        return self.recv[stage]


def _mm_swiglu(name, h, w_in):
    m, kk = h.shape
    f = w_in.shape[1] // 2
    tm = _pick(m, (272, 256, 128))

    def body(h_ref, wg_ref, wu_ref, pf_ref, act_ref):
        a = h_ref[...].astype(MXU_DTYPE)
        g = jnp.dot(a, wg_ref[...].astype(MXU_DTYPE), preferred_element_type=F32).astype(MXU_DTYPE)
        u = jnp.dot(a, wu_ref[...].astype(MXU_DTYPE), preferred_element_type=F32).astype(MXU_DTYPE)
        pf_ref[0] = g
        pf_ref[1] = u
        act_ref[...] = (jax.nn.silu(g.astype(F32)) * u.astype(F32)).astype(act_ref.dtype)

    return pl.pallas_call(
        body, name=name, grid=(m // tm,),
        in_specs=[pl.BlockSpec((tm, kk), lambda i: (i, 0)), pl.BlockSpec((kk, f), lambda i: (0, 0)), pl.BlockSpec((kk, f), lambda i: (0, 1))],
        out_specs=[pl.BlockSpec((2, tm, f), lambda i: (0, i, 0)), pl.BlockSpec((tm, f), lambda i: (i, 0))],
        out_shape=[jax.ShapeDtypeStruct((2, m, f), MXU_DTYPE), jax.ShapeDtypeStruct((m, f), MXU_DTYPE)],
        compiler_params=_cp(("parallel",)),
    )(h, w_in, w_in)


def _row_gate_up(pf):
    f = pf.shape[2]
    return ([pf], [pl.BlockSpec((2, TM, f), lambda i: (0, i, 0))],
            lambda r: jnp.concatenate([r[0], r[1]], axis=1).astype(F32), 2 * f)


def _ffn_fwd(tag, h, w_in, w_out, nb, cb):
    pf, act = _mm_swiglu(tag + "_ffn_in", h, w_in)
    return pf, act, _mm(tag + "_ffn_out", act, w_out, "nn", F32)


def _ffn_bwd(tag, h, pf, act, df, w_in, w_out, nb, cb):
    dw_out = _mm(tag + "_ffn_out_dw", act, df, "tn", MXU_DTYPE)
    dact = _mm(tag + "_ffn_out_dx", df, w_out, "nt", MXU_DTYPE)
    (dpf,), _ = _rowwise_vjp(tag + "_swiglu_bwd", _fn_swiglu, nb, cb, [_row_gate_up(pf)], [], [_row(dact)], [(0, MXU_DTYPE, None)])
    dw_in = _mm(tag + "_ffn_in_dw", h, dpf, "tn", MXU_DTYPE)
    dh = _mm(tag + "_ffn_in_dx", dpf, w_in, "nt", MXU_DTYPE)
    return dw_out, dw_in, dh


def _local_step(x, ctx, target, mod, P, comm):
    T = x.shape[0]
    N = T + CTX
    nb, cb = N // TM, N // TM - 1
    R = T // GRID_W
    mod0, mod1 = mod[0], mod[1]
    ng = P["norm_g"]
    g00, g01, g10, g11 = ng[0, 0][None], ng[0, 1][None], ng[1, 0][None], ng[1, 1][None]
    pre = functools.partial(_fn_prenorm, a=0, b=1)
    rpre = functools.partial(_fn_resid_prenorm, gi=2, a=3, b=4)
    res5 = functools.partial(_fn_resid, gi=5)
    dirs = (("f", False), ("b", True))

    xc0 = jnp.concatenate([x, ctx], axis=0)
    w_ab_in, w_ab_out, g00 = comm.get("l0_mixer", mod, g00)
    (h0,) = _rowwise("l0_prenorm", pre, nb, cb, [_row(xc0)], [g00, mod0], [(D, MXU_DTYPE)])
    p0 = _mm("l0_in", h0, w_ab_in, "nn", F32)
    gla_rows = [(p0, 512, 0), (p0, 256, 8), (p0, 256, 9), (p0, 128, 20)]
    gla_blk = _multi_chunk(_gla_chunk, GLA_L, TM // GLA_L, len(gla_rows))
    gla_par = {d: [P["ab_gate_w"][int(r)], P["ab_gate_b"][int(r)][None]] for d, r in dirs}
    gla_state = (GLA_H * GLA_DV, GLA_H * GLA_DK)
    o, st0 = None, {}
    for d, rev in dirs:
        o, st0[d] = _scan_fwd("gla_fwd_" + d, gla_blk, TM, nb, cb, rev, gla_rows, gla_par[d], gla_state, GLA_H * GLA_DV, o)
    n128, cb128 = N // GMLP_L, T // GMLP_L
    mix_rows = [_row(o, tm=GMLP_L)] + [_row(p0, 512, j, tm=GMLP_L) for j in (1, 2, 3)]
    mix_par = [P["ab_gla_norm_g"], P["ab_vnorm_g"], P["ab_spatial_w"].reshape(GMLP_G * GMLP_L, GMLP_L), P["ab_spatial_b"].T]
    (cat0,) = _rowwise("l0_mix", _fn_mixpost, n128, cb128, mix_rows, mix_par, [(D, MXU_DTYPE)], tm=GMLP_L)
    y0 = _mm("l0_out", cat0, w_ab_out, "nn", F32)
    w_fi0, w_fo0, g01 = comm.get("l0_ffn", y0, g01)
    x1, h1 = _rowwise("l0_ffn_prenorm", rpre, nb, cb, [_row(xc0), _row(y0)], [g01, mod0, mod0], [(D, F32), (D, MXU_DTYPE)])
    pf0, act0, f0 = _ffn_fwd("l0", h1, w_fi0, w_fo0, nb, cb)
    w_ssd_in, w_ssd_out, g10 = comm.get("l1_mixer", f0, g10)
    x2p, h2 = _rowwise("l0_resid_l1_prenorm", functools.partial(_fn_resid_prenorm, gi=5, a=0, b=1), nb, cb,
                       [_row_grid(x1, R, nb), _row_grid(f0, R, nb)], [g10, mod0, mod1], [(D, F32), (D, MXU_DTYPE)])
    p1 = _mm("l1_in", h2, w_ssd_in, "nn", F32)
    conv_w8 = jnp.concatenate([P["ssd_conv_w"], jnp.zeros((8 - SSD_K, 3 * CONV_W), F32)], axis=0)
    xbc = _conv("l1_conv", p1, conv_w8, P["ssd_conv_b"], nb, permuted_src=True, act=True, flip=False, out_dtype=F32)
    ssd_rows = [(xbc, SSD_INNER, 0), (xbc, 512, 4), (xbc, 512, 5), (p1, 128, 40)]
    ssd_blk = _multi_chunk(_ssd_chunk, SSD_L, TM // SSD_L, len(ssd_rows))
    ssd_par = {d: [P["ssd_dt_bias"][int(r)][None], P["ssd_a_log"][int(r)][None]] for d, r in dirs}
    ssd_state = (SSD_N, SSD_INNER)
    ys, st1 = None, {}
    for d, rev in dirs:
        ys, st1[d] = _scan_fwd("ssd_fwd_" + d, ssd_blk, TM, nb, cb, rev, ssd_rows, ssd_par[d], ssd_state, SSD_INNER, ys)
    fin_rows = [_row(ys), _row(xbc, SSD_INNER, 0), _row(p1, SSD_INNER, 1)]
    fin_par = [P["ssd_d"], P["ssd_norm_g"]]
    (yn,) = _rowwise("l1_finish", _fn_ssd_finish, cb, cb, fin_rows, fin_par, [(SSD_INNER, MXU_DTYPE)])
    y1 = _mm("l1_out", yn, w_ssd_out, "nn", F32)
    w_fi1, w_fo1, g11 = comm.get("l1_ffn", y1, g11)
    x3, h3 = _rowwise("l1_ffn_prenorm", rpre, cb, cb, [_row(x2p), _row(y1)], [g11, mod1, mod1], [(D, F32), (D, MXU_DTYPE)])
    pf1, act1, f1 = _ffn_fwd("l1", h3, w_fi1, w_fo1, cb, cb)
    loss, dx3, df1, dm1_j, d_final_g = _loss_head(x3, f1, target, mod1, P["final_norm_g"], R)

    dP = {"final_norm_g": d_final_g}
    dwo1, dwi1, dh3 = _ffn_bwd("l1", h3, pf1, act1, df1, w_fi1, w_fo1, cb, cb)
    g11 = comm.put("l1_ffn", dwi1, dwo1, g11)
    (dx2p_a, dy1), (dg11, dm1_a, dm1_b) = _rowwise_vjp(
        "l1_ffn_prenorm_bwd", rpre, cb, cb, [_row(x2p), _row(y1)], [g11, mod1, mod1], [_row(dx3), _row(dh3)],
        [(0, F32, None), (1, MXU_DTYPE, None)])
    d_ssd_out = _mm("l1_out_dw", yn, dy1, "tn", MXU_DTYPE)
    dyn = _mm("l1_out_dx", dy1, w_ssd_out, "nt", MXU_DTYPE)
    (dys, dxs, dz), (dP["ssd_d"], dP["ssd_norm_g"]) = _rowwise_vjp(
        "l1_finish_bwd", _fn_ssd_finish, cb, cb, fin_rows, fin_par, [_row(dyn)],
        [(0, F32, None), (1, F32, None), (2, MXU_DTYPE, None)])
    dssd, ddtb, dalog = None, [], []
    for d, rev in dirs:
        dssd, (ddtb_, dalog_) = _scan_bwd("ssd_bwd_" + d, ssd_blk, TM, nb, cb, rev, ssd_rows, ssd_par[d], st1[d], dys, ssd_state,
                                          SSD_INNER, dssd)
        ddtb.append(ddtb_); dalog.append(dalog_)
    dx_s, db_s, dc_s, dtl = dssd
    dP["ssd_dt_bias"] = jnp.concatenate(ddtb, axis=0)
    dP["ssd_a_log"] = jnp.concatenate(dalog, axis=0)
    dacc, dcw8, dP["ssd_conv_b"] = _conv_bwd_pre("l1_conv_bwd", p1, conv_w8, P["ssd_conv_b"], ([dx_s, dxs], [db_s, dc_s]), nb)
    dP["ssd_conv_w"] = dcw8[:SSD_K]
    dpc = _conv("l1_conv_dx", dacc, conv_w8, jnp.zeros((1, 3 * CONV_W), F32), nb, permuted_src=False, act=False, flip=True,
                out_dtype=MXU_DTYPE)
    cat1 = functools.partial(_fn_concat, sums=(1, 1, 1, 1), pad=SSD_P_W - 5248)
    (dp1,) = _rowwise("l1_dp", cat1, nb, cb, [_row(dpc, SSD_INNER, 0), _row(dz, valid=cb), _row(dpc, 1024, 2), _row(dtl)],
                      [], [(SSD_P_W, MXU_DTYPE)])
    g10 = comm.put("l1_mixer", _mm("l1_in_dw", h2, dp1, "tn", F32), d_ssd_out, g10)
    dh2 = _mm("l1_in_dx", dp1, w_ssd_in, "nt", MXU_DTYPE)
    (dx2p,), (dg10, dm1_f) = _rowwise_vjp("l1_prenorm_bwd", pre, nb, cb, [_row(x2p)], [g10, mod1], [_row(dh2)],
                                          [(0, F32, _row(dx2p_a, valid=cb))])

    (dx1_a, df0), (dm0_e,) = _rowwise_vjp("l0_resid_bwd", res5, nb, cb, [_row(x1), _row(f0)], [mod0],
                                          [_row_grid(dx2p, GRID_W, nb)], [(0, F32, None), (1, MXU_DTYPE, None)])
    dwo0, dwi0, dh1 = _ffn_bwd("l0", h1, pf0, act0, df0, w_fi0, w_fo0, nb, cb)
    g01 = comm.put("l0_ffn", dwi0, dwo0, g01)
    (dxc0_a, dy0), (dg01, dm0_a, dm0_b) = _rowwise_vjp(
        "l0_ffn_prenorm_bwd", rpre, nb, cb, [_row(xc0), _row(y0)], [g01, mod0, mod0], [_row(dx1_a), _row(dh1)],
        [(0, F32, None), (1, MXU_DTYPE, None)])
    d_ab_out = _mm("l0_out_dw", cat0, dy0, "tn", MXU_DTYPE)
    dcat0 = _mm("l0_out_dx", dy0, w_ab_out, "nt", MXU_DTYPE)
    (do, dr, du, dgm), (dP["ab_gla_norm_g"], dP["ab_vnorm_g"], dsw, dsb_t) = _rowwise_vjp(
        "l0_mix_bwd", _fn_mixpost, n128, cb128, mix_rows, mix_par, [_row(dcat0, tm=GMLP_L)],
        [(0, F32, None), (1, MXU_DTYPE, None), (2, MXU_DTYPE, None), (3, MXU_DTYPE, None)], tm=GMLP_L)
    dP["ab_spatial_w"] = dsw.reshape(GMLP_G, GMLP_L, GMLP_L)
    dP["ab_spatial_b"] = dsb_t.T
    gl, dgw, dgb = None, [], []
    for d, rev in dirs:
        gl, (dgw_, dgb_) = _scan_bwd("gla_bwd_" + d, gla_blk, TM, nb, cb, rev, gla_rows, gla_par[d], st0[d], do,
                                     gla_state, GLA_H * GLA_DV, gl)
        dgw.append(dgw_[None]); dgb.append(dgb_)
    dP["ab_gate_w"] = jnp.concatenate(dgw, axis=0)
    dP["ab_gate_b"] = jnp.concatenate(dgb, axis=0)
    cat0f = functools.partial(_fn_concat, sums=(1,) * 7, pad=AB_P - 2688)
    (dp0,) = _rowwise("l0_dp", cat0f, nb, cb, [_row(gl[0]), _row(dr), _row(du), _row(dgm), _row(gl[1]), _row(gl[2]), _row(gl[3])],
                      [], [(AB_P, MXU_DTYPE)])
    g00 = comm.put("l0_mixer", _mm("l0_in_dw", h0, dp0, "tn", F32), d_ab_out, g00)
    dh0 = _mm("l0_in_dx", dp0, w_ab_in, "nt", MXU_DTYPE)
    (dxc0,), (dg00, dm0_s) = _rowwise_vjp("l0_prenorm_bwd", pre, nb, cb, [_row(xc0)], [g00, mod0], [_row(dh0)],
                                          [(0, F32, _row(dxc0_a))])
    dP["norm_g"] = jnp.concatenate([dg00, dg01, dg10, dg11], axis=0).reshape(2, 2, D)
    dmod = jnp.stack([dm0_s + dm0_a + dm0_b + dm0_e, dm1_f + dm1_a + dm1_b + dm1_j])
    return loss, dxc0[:T], dmod, dP


WEIGHTS = ("c_ctx", "mod_w", "mod_b", "norm_g", "ffn_w_in", "ffn_w_out", "ab_w_in", "ab_gate_w", "ab_gate_b", "ab_gla_norm_g",
           "ab_vnorm_g", "ab_spatial_w", "ab_spatial_b", "ab_w_out", "ssd_w_in", "ssd_conv_w", "ssd_conv_b", "ssd_dt_bias",
           "ssd_a_log", "ssd_d", "ssd_norm_g", "ssd_w_out", "final_norm_g")
SMALL_SHARDED = ("norm_g", "ab_gate_w", "ab_gate_b", "ssd_conv_w", "ssd_conv_b", "ssd_norm_g")
SMALL = ("c_ctx", "mod_b", "norm_g", "ab_gate_w", "ab_gate_b", "ab_gla_norm_g", "ab_vnorm_g", "ab_spatial_w", "ab_spatial_b",
         "ssd_conv_w", "ssd_conv_b", "ssd_dt_bias", "ssd_a_log", "ssd_d", "ssd_norm_g", "final_norm_g")
LANES = 1024


def _pack(arrs, rows_multiple=8):
    flat = jnp.concatenate([a.reshape(-1).astype(F32) for a in arrs])
    rows = -(-flat.shape[0] // LANES)
    rows = -(-rows // rows_multiple) * rows_multiple
    return jnp.pad(flat, (0, rows * LANES - flat.shape[0])).reshape(rows, LANES)


def _unpack_flat(buf, shapes):
    lead = buf.shape[:-2]
    flat = buf.reshape(lead + (-1,))
    out, o = [], 0
    for s in shapes:
        n = math.prod(s)
        out.append(flat[..., o:o + n].reshape(lead + tuple(s)))
        o += n
    return out


def _unshard(g):
    g = jnp.moveaxis(g, 0, -2)
    return g.reshape(g.shape[:-2] + (g.shape[-2] * g.shape[-1],))


def _my_shard(full, me, ws):
    return lax.dynamic_slice_in_dim(full, me * ws, ws, axis=full.ndim - 1)


def _silu_vjp(cvec, dsc):
    def body(c_ref, d_ref, o_ref):
        _, vjp = jax.vjp(jax.nn.silu, c_ref[...])
        o_ref[...] = vjp(d_ref[...])[0]

    return pl.pallas_call(body, name="c_ctx_bwd", out_shape=jax.ShapeDtypeStruct(cvec.shape, F32), compiler_params=_cp())(cvec, dsc)


def kernel(x, c, ctx, c_ctx, mod_w, mod_b, norm_g, ffn_w_in, ffn_w_out, ab_w_in, ab_gate_w, ab_gate_b, ab_gla_norm_g, ab_vnorm_g, ab_spatial_w, ab_spatial_b, ab_w_out, ssd_w_in, ssd_conv_w, ssd_conv_b, ssd_dt_bias, ssd_a_log, ssd_d, ssd_norm_g, ssd_w_out, final_norm_g, loss_target, m_c_ctx, m_mod_w, m_mod_b, m_norm_g, m_ffn_w_in, m_ffn_w_out, m_ab_w_in, m_ab_gate_w, m_ab_gate_b, m_ab_gla_norm_g, m_ab_vnorm_g, m_ab_spatial_w, m_ab_spatial_b, m_ab_w_out, m_ssd_w_in, m_ssd_conv_w, m_ssd_conv_b, m_ssd_dt_bias, m_ssd_a_log, m_ssd_d, m_ssd_norm_g, m_ssd_w_out, m_final_norm_g, v_c_ctx, v_mod_w, v_mod_b, v_norm_g, v_ffn_w_in, v_ffn_w_out, v_ab_w_in, v_ab_gate_w, v_ab_gate_b, v_ab_gla_norm_g, v_ab_vnorm_g, v_ab_spatial_w, v_ab_spatial_b, v_ab_w_out, v_ssd_w_in, v_ssd_conv_w, v_ssd_conv_b, v_ssd_dt_bias, v_ssd_a_log, v_ssd_d, v_ssd_norm_g, v_ssd_w_out, v_final_norm_g):
    a = dict(locals())
    me = _slot(*_mesh_pos())
    ws_mod = mod_w.shape[-1]

    fwd_small = [c] + [a[k] for k in SMALL_SHARDED]
    g_small = _ag_small("gather_small", _pack(fwd_small))
    parts = _unpack_flat(g_small, [t.shape for t in fwd_small])
    c_rows = parts[0].reshape(NDEV, D)
    full = {k: _unshard(p) for k, p in zip(SMALL_SHARDED, parts[1:])}
    c_all = jnp.concatenate([c_rows, c_ctx[None], jnp.zeros((7, D), F32)], axis=0)
    m_all = _ag_small("gather_mod", _mod_fwd(c_all, mod_w).reshape(2 * 16, ws_mod)).reshape(NDEV, 2, 16, ws_mod)
    m_mine = lax.dynamic_index_in_dim(m_all, me, axis=2, keepdims=False)
    mx = jnp.moveaxis(m_mine, 0, 1).reshape(2, N_MOD, D) + mod_b.reshape(2, N_MOD, D)
    mc = jnp.moveaxis(m_all[:, :, 8, :], 0, 1).reshape(2, N_MOD, D) + mod_b.reshape(2, N_MOD, D)
    pad2 = jnp.zeros((2, 2, D), F32)
    mod = jnp.concatenate([mx, pad2, mc, pad2], axis=1)

    big = {"l0_mixer": (ab_w_in[0], ab_w_out[0]), "l0_ffn": (ffn_w_in[0], ffn_w_out[0]),
           "l1_mixer": (ssd_w_in[0], ssd_w_out[0]), "l1_ffn": (ffn_w_in[1], ffn_w_out[1])}
    comm = _Exchange({k: tuple(w.astype(MXU_DTYPE) for w in v) for k, v in big.items()}, me)
    P = {
        "norm_g": full["norm_g"], "ab_gate_w": full["ab_gate_w"][0], "ab_gate_b": full["ab_gate_b"][0],
        "ab_gla_norm_g": ab_gla_norm_g, "ab_vnorm_g": ab_vnorm_g, "ab_spatial_w": ab_spatial_w[0], "ab_spatial_b": ab_spatial_b[0],
        "ssd_conv_w": full["ssd_conv_w"][0], "ssd_conv_b": full["ssd_conv_b"], "ssd_dt_bias": ssd_dt_bias[0],
        "ssd_a_log": ssd_a_log[0], "ssd_d": ssd_d, "ssd_norm_g": full["ssd_norm_g"], "final_norm_g": final_norm_g[None],
    }

    loss, grad_x, dmod, dP = _local_step(x[0], ctx[0], loss_target[0], mod, P, comm)

    dmx, dmc = dmod[:, 0:N_MOD].reshape(2, N_MOD * D), dmod[:, 8:8 + N_MOD].reshape(2, N_MOD * D)
    small_names = ("ab_gate_w", "ab_gate_b", "ab_gla_norm_g", "ab_vnorm_g", "ab_spatial_w", "ab_spatial_b", "norm_g", "ssd_conv_w",
                   "ssd_conv_b", "ssd_dt_bias", "ssd_a_log", "ssd_d", "ssd_norm_g", "final_norm_g")
    bwd_small = [dP[k] for k in small_names] + [dmc, dmx]
    shapes = [t.shape for t in bwd_small]
    g_bwd = _ag_small("gather_small_grads", _pack(bwd_small))
    summed = _unpack_flat(_sum_parts("sum_small_grads", g_bwd), shapes)
    gfull = dict(zip(small_names, summed[:-2]))
    dmc_sum, dmx_sum = summed[-2], summed[-1]
    dmx_all = _unpack_flat(g_bwd, shapes)[-1]
    dmx_sh = jnp.moveaxis(_my_shard(dmx_all, me, ws_mod), 0, 1)
    dm = jnp.concatenate([dmx_sh, _my_shard(dmc_sum, me, ws_mod)[:, None, :], jnp.zeros((2, 7, ws_mod), F32)], axis=1)
    d_mod_w, dsc = _mod_bwd(c_all, mod_w, dm)
    dsc_ctx = (dsc[0, 8] + dsc[1, 8])[None]
    dsc_all = _ag_small("gather_c_ctx_grad", jnp.concatenate([dsc_ctx, jnp.zeros((7, D), F32)], axis=0))
    d_c_ctx = _silu_vjp(c_ctx[None], _sum_parts("sum_c_ctx_grad", dsc_all)[0:1])[0]

    g_small_w = {
        "c_ctx": d_c_ctx, "mod_b": dmx_sum + dmc_sum, "norm_g": gfull["norm_g"], "ab_gate_w": gfull["ab_gate_w"][None],
        "ab_gate_b": gfull["ab_gate_b"][None], "ab_gla_norm_g": gfull["ab_gla_norm_g"], "ab_vnorm_g": gfull["ab_vnorm_g"],
        "ab_spatial_w": gfull["ab_spatial_w"][None], "ab_spatial_b": gfull["ab_spatial_b"][None], "ssd_conv_w": gfull["ssd_conv_w"][None],
        "ssd_conv_b": gfull["ssd_conv_b"], "ssd_dt_bias": gfull["ssd_dt_bias"][None], "ssd_a_log": gfull["ssd_a_log"][None],
        "ssd_d": gfull["ssd_d"], "ssd_norm_g": gfull["ssd_norm_g"], "final_norm_g": gfull["final_norm_g"][0],
    }
    for k in SMALL_SHARDED:
        g_small_w[k] = _my_shard(g_small_w[k], me, a[k].shape[-1])
    token = comm.start_last([d_c_ctx])
    res = _adam("adam_small", _pack([a[k] for k in SMALL]), [_pack([g_small_w[k] for k in SMALL])[None]],
                _pack([a["m_" + k] for k in SMALL]), _pack([a["v_" + k] for k in SMALL]), token)
    out = {k: vals for k, vals in zip(SMALL, zip(*[_unpack_flat(r, [a[k].shape for k in SMALL]) for r in res]))}

    def adam_big(name, w2d, parts, m2d, v2d, shape):
        return tuple(r.reshape(shape) for r in _adam(name, w2d, parts, m2d, v2d, token))

    def flat2(t):
        return t.reshape(-1, t.shape[-1])

    out["mod_w"] = adam_big("adam_mod_w", flat2(mod_w), [d_mod_w.reshape(1, -1, ws_mod)], flat2(m_mod_w), flat2(v_mod_w), mod_w.shape)

    for j, k in enumerate(("ffn_w_in", "ffn_w_out")):
        out[k] = adam_big("adam_" + k, flat2(a[k]), [comm.recv["l0_ffn"][j], comm.recv["l1_ffn"][j]], flat2(a["m_" + k]),
                          flat2(a["v_" + k]), a[k].shape)
    for j, k in enumerate(("ssd_w_in", "ssd_w_out")):
        out[k] = adam_big("adam_" + k, a[k][0], [comm.recv["l1_mixer"][j]], a["m_" + k][0], a["v_" + k][0], a[k].shape)
    recv_ab = comm.finish(out["ssd_w_out"][3])
    for j, k in enumerate(("ab_w_in", "ab_w_out")):
        out[k] = adam_big("adam_" + k, a[k][0], [recv_ab[j]], a["m_" + k][0], a["v_" + k][0], a[k].shape)

    loss_all = lax.psum(loss[0, 0], ("x", "y", "c"))
    return (loss_all, grad_x[None], *[out[k][0] for k in WEIGHTS], *[out[k][1] for k in WEIGHTS],
            *[out[k][2] for k in WEIGHTS], *[out[k][3] for k in WEIGHTS])
```

```python
import functools
import math

import jax
import jax.numpy as jnp
from jax import lax
from jax.experimental import pallas as pl
from jax.experimental.pallas import tpu as pltpu

F32 = jnp.float32
BF16 = jnp.bfloat16
MXU_DTYPE = jnp.bfloat16
HI = lax.Precision.HIGHEST

D = 1024
NDEV = 8
N_MOD = 6
EPS = 1e-6
GRID_W = 64
CTX = 256
TM = 256
D_FF = 2816
GLA_H, GLA_DK, GLA_DV, GLA_LR, GLA_TAU, GLA_L = 4, 64, 128, 16, 16.0, 64
GMLP_G, GMLP_C, GMLP_L = 4, 128, 128
SSD_H, SSD_P, SSD_G, SSD_N, SSD_L, SSD_K = 32, 64, 4, 128, 128, 5
SSD_INNER = SSD_H * SSD_P
AB_IN = 2592
SSD_IN = 5184
AB_SEGS = ((256, 768), (1056, 1568), (1568, 2080), (2080, 2592), (0, 256), (800, 1056), (768, 800))
AB_P = 2816
SSD_SEGS = ((0, 2048), (3136, 5184), (2048, 2560), (2560, 3072), (3072, 3136))
SSD_P_W = 5376
VMEM_LIMIT = 56 * 1024 * 1024

ADAM_LR, ADAM_B1, ADAM_B2, ADAM_EPS, ADAM_WD, ADAM_STEP = 0.001, 0.9, 0.999, 1e-08, 0.01, 10


def _cp(sem=None, **kw):
    return pltpu.CompilerParams(dimension_semantics=sem, vmem_limit_bytes=VMEM_LIMIT, **kw)


def _dot(a, b, dims=(((1,), (0,)), ((), ()))):
    return lax.dot_general(a.astype(MXU_DTYPE), b.astype(MXU_DTYPE), dims, preferred_element_type=F32)


def _dot_nt(a, b):
    return _dot(a, b, (((1,), (1,)), ((), ())))


def _dot_tn(a, b):
    return _dot(a, b, (((0,), (0,)), ((), ())))


def _dotx(a, b, dims=(((1,), (0,)), ((), ()))):
    return lax.dot_general(a, b, dims, precision=HI, preferred_element_type=F32)


def _rms(x):
    return x * lax.rsqrt(jnp.mean(x * x, axis=-1, keepdims=True) + EPS)


def _pick(n, prefs):
    for p in prefs:
        if n % p == 0:
            return p
    return n


def _row(arr, width=None, colblk=0, tm=TM, valid=None):
    width = arr.shape[1] if width is None else width
    if valid is None:
        return ([arr], [pl.BlockSpec((tm, width), lambda i, c=colblk: (i, c))], lambda r: r[...].astype(F32), width)
    spec = pl.BlockSpec((tm, width), lambda i, c=colblk: (jnp.minimum(i, valid - 1), c))
    return ([arr], [spec], lambda r: jnp.where(pl.program_id(0) < valid, r[...].astype(F32), 0.0), width)


def _row_grid(arr, a, nb):
    n = arr.shape[0]
    b = (n - CTX) // a

    def load(v_ref, c_ref):
        i = pl.program_id(0)
        return jnp.where(i == nb - 1, c_ref[...], _grid_rows(v_ref, a, i))

    return ([arr.reshape(n // b, b, D), arr], [_grid_spec(a, nb), pl.BlockSpec((TM, D), lambda i: (nb - 1, 0))], load, D)


def _operands(rows):
    return [a for r in rows for a in r[0]], [s for r in rows for s in r[1]]


def _load_rows(refs, rows):
    vals, k = [], 0
    for r in rows:
        vals.append(r[2](*refs[k:k + len(r[0])]))
        k += len(r[0])
    return vals


def _full_spec(p):
    nd = p.ndim
    return pl.BlockSpec(p.shape, lambda i, nd=nd: (0,) * nd)


def _rowwise(name, fn, n_blocks, ctx_blk, rows, params, outs, tm=TM):
    arrs, specs = _operands(rows)
    nr, npar = len(arrs), len(params)

    def body(*refs):
        t = (pl.program_id(0) >= ctx_blk).astype(F32)
        rv = _load_rows(refs[:nr], rows)
        pv = [p[...] for p in refs[nr:nr + npar]]
        res = fn(t, rv, pv)
        for o_ref, o in zip(refs[nr + npar:], res):
            o_ref[...] = o.astype(o_ref.dtype)

    return pl.pallas_call(
        body, name=name, grid=(n_blocks,),
        in_specs=specs + [_full_spec(p) for p in params],
        out_specs=[pl.BlockSpec((tm, w), lambda i: (i, 0)) for w, _ in outs],
        out_shape=[jax.ShapeDtypeStruct((n_blocks * tm, w), dt) for w, dt in outs],
        compiler_params=_cp(("parallel",)),
    )(*arrs, *params)


def _rowwise_vjp(name, fn, n_blocks, ctx_blk, rows, params, douts, row_grads, tm=TM):
    adds = [a for _, _, a in row_grads if a is not None]
    (r_arrs, r_specs), (d_arrs, d_specs), (a_arrs, a_specs) = _operands(rows), _operands(douts), _operands(adds)
    nr, npar, nd, na = len(r_arrs), len(params), len(d_arrs), len(a_arrs)

    def body(*refs):
        i = pl.program_id(0)
        t = (i >= ctx_blk).astype(F32)
        rv = _load_rows(refs[:nr], rows)
        pv = [p[...] for p in refs[nr:nr + npar]]
        dv = _load_rows(refs[nr + npar:nr + npar + nd], douts)
        av = _load_rows(refs[nr + npar + nd:nr + npar + nd + na], adds)
        o_refs = refs[nr + npar + nd + na:]
        _, vjp = jax.vjp(lambda r, p: tuple(fn(t, r, p)), rv, pv)
        d_rows, d_params = vjp(tuple(dv))
        ai = 0
        for o_ref, (ri, _, addend) in zip(o_refs, row_grads):
            g = d_rows[ri]
            if addend is not None:
                g = g + av[ai]
                ai += 1
            o_ref[...] = g.astype(o_ref.dtype)
        p_refs = o_refs[len(row_grads):]

        @pl.when(i == 0)
        def _():
            for p_ref in p_refs:
                p_ref[...] = jnp.zeros_like(p_ref)

        for p_ref, g in zip(p_refs, d_params):
            p_ref[...] += g

    widths = [rows[ri][3] for ri, _, _ in row_grads]
    res = pl.pallas_call(
        body, name=name, grid=(n_blocks,),
        in_specs=r_specs + [_full_spec(p) for p in params] + d_specs + a_specs,
        out_specs=[pl.BlockSpec((tm, w), lambda i: (i, 0)) for w in widths] + [_full_spec(p) for p in params],
        out_shape=[jax.ShapeDtypeStruct((n_blocks * tm, w), dt) for w, (_, dt, _) in zip(widths, row_grads)]
        + [jax.ShapeDtypeStruct(p.shape, F32) for p in params],
        compiler_params=_cp(("arbitrary",)),
    )(*r_arrs, *params, *d_arrs, *a_arrs)
    return res[:len(row_grads)], res[len(row_grads):]


def _mm(name, a, b, mode, out_dtype):
    if mode == "nn":
        m, kk = a.shape
        n = b.shape[1]
    elif mode == "nt":
        m, kk = a.shape
        n = b.shape[0]
    else:
        kk, m = a.shape
        n = b.shape[1]
    if mode == "tn":
        tm = _pick(m, (1024, 1408, 512, 256, 128))
        tn = _pick(n, (768, 512, 256, 128))
        tk = kk
    else:
        tm = _pick(m, (1088, 1024, 768, 512, 384, 256, 128))
        tn = n if n <= 2816 else _pick(n, (1024, 768, 512, 256, 128))
        tk = kk if kk <= 2816 else _pick(kk, (2816, 1792, 1024, 768, 512, 256, 128))
    nk = kk // tk
    in_place = out_dtype == F32
    if mode == "nn":
        specs = [pl.BlockSpec((tm, tk), lambda i, j, k: (i, k)), pl.BlockSpec((tk, tn), lambda i, j, k: (k, j))]
        dims = (((1,), (0,)), ((), ()))
    elif mode == "nt":
        specs = [pl.BlockSpec((tm, tk), lambda i, j, k: (i, k)), pl.BlockSpec((tn, tk), lambda i, j, k: (j, k))]
        dims = (((1,), (1,)), ((), ()))
    else:
        specs = [pl.BlockSpec((tk, tm), lambda i, j, k: (k, i)), pl.BlockSpec((tk, tn), lambda i, j, k: (k, j))]
        dims = (((0,), (0,)), ((), ()))

    def body(a_ref, b_ref, o_ref, *scratch):
        part = lax.dot_general(a_ref[...].astype(MXU_DTYPE), b_ref[...].astype(MXU_DTYPE), dims, preferred_element_type=F32)
        if nk == 1:
            o_ref[...] = part.astype(o_ref.dtype)
        else:
            k = pl.program_id(2)
            acc = o_ref if in_place else scratch[0]

            @pl.when(k == 0)
            def _():
                acc[...] = part

            @pl.when(k > 0)
            def _():
                acc[...] += part

            if not in_place:
                @pl.when(k == nk - 1)
                def _():
                    o_ref[...] = acc[...].astype(o_ref.dtype)

    return pl.pallas_call(
        body, name=name, grid=(m // tm, n // tn, nk), in_specs=specs,
        out_specs=pl.BlockSpec((tm, tn), lambda i, j, k: (i, j)),
        out_shape=jax.ShapeDtypeStruct((m, n), out_dtype),
        scratch_shapes=[] if nk == 1 or in_place else [pltpu.VMEM((tm, tn), F32)],
        compiler_params=_cp(("parallel", "parallel", "arbitrary")),
    )(a, b)


def _sel_mod(modp, t):
    return modp[0:8] * (1.0 - t) + modp[8:16] * t


def _fn_prenorm(t, rows, params, *, a, b):
    (x,), (g, modp) = rows, params
    m = _sel_mod(modp, t)
    return ((_rms(x) * g) * (1.0 + m[b:b + 1]) + m[a:a + 1],)


def _fn_resid_prenorm(t, rows, params, *, gi, a, b):
    (x, y), (g, mod_a, mod_b) = rows, params
    ma, mb = _sel_mod(mod_a, t), _sel_mod(mod_b, t)
    xn = x + ma[gi:gi + 1] * y
    return xn, (_rms(xn) * g) * (1.0 + mb[b:b + 1]) + mb[a:a + 1]


def _fn_resid(t, rows, params, *, gi):
    (x, y), (mod_a,) = rows, params
    return (x + _sel_mod(mod_a, t)[gi:gi + 1] * y,)


def _fn_swiglu(t, rows, params):
    (pf,) = rows
    return (jax.nn.silu(pf[:, :D_FF]) * pf[:, D_FF:],)


def _fn_mixpost(t, rows, params):
    (o, r, u, g), (gla_g, vn_g, sw, sb_t) = rows, params
    a =jnp.concatenate([_rms(o[:, h * GLA_DV:(h + 1) * GLA_DV]) for h in range(GLA_H)], axis=1) * gla_g * jax.nn.silu(r)
    uu, vv = jax.nn.gelu(u), jax.nn.gelu(g)
    mu = jnp.mean(vv, axis=-1, keepdims=True)
    var = jnp.mean(jnp.square(vv - mu), axis=-1, keepdims=True)
    vn = ((vv - mu) * lax.rsqrt(var + EPS)) * vn_g
    s = jnp.concatenate(
        [_dot(sw[gi * GMLP_L:(gi + 1) * GMLP_L, :], vn[:, gi * GMLP_C:(gi + 1) * GMLP_C]) + sb_t[:, gi:gi + 1]
         for gi in range(GMLP_G)], axis=1)
    return (jnp.concatenate([a, uu * s], axis=1),)


def _head_expand():
    r = lax.broadcasted_iota(jnp.int32, (SSD_H, SSD_INNER), 0)
    c = lax.broadcasted_iota(jnp.int32, (SSD_H, SSD_INNER), 1)
    return (c // SSD_P == r).astype(F32)


def _fn_ssd_finish(t, rows, params):
    (y2, xs, z), (d_skip, norm_g) = rows, params
    d_full = _dotx(jnp.broadcast_to(d_skip, (8, SSD_H)), _head_expand())[0:1]
    y = (y2 + d_full * xs) * jax.nn.silu(z)
    gw = SSD_INNER // SSD_G
    return (jnp.concatenate([_rms(y[:, gi * gw:(gi + 1) * gw]) for gi in range(SSD_G)], axis=1) * norm_g,)


def _fn_concat(t, rows, params, *, sums, pad=0):
    out, i = [], 0
    for n in sums:
        acc = rows[i]
        for j in range(1, n):
            acc = acc + rows[i + j]
        out.append(acc)
        i += n
    if pad:
        out.append(jnp.zeros((out[0].shape[0], pad), F32))
    return (jnp.concatenate(out, axis=1),)


def _tri(n, rev):
    r = lax.broadcasted_iota(jnp.int32, (n, n), 0)
    c = lax.broadcasted_iota(jnp.int32, (n, n), 1)
    return (r <= c) if rev else (r >= c)


def _gla_chunk(S, v, k, q, tail, gw, gb, *, rev):
    L, H = GLA_L, GLA_H
    msk = _tri(L, rev)
    tri = msk.astype(F32)
    lr = tail[:, GLA_LR:2 * GLA_LR] if rev else tail[:, 0:GLA_LR]
    la = jax.nn.log_sigmoid(_dot(lr, gw) + gb) / GLA_TAU
    b = _dotx(tri, la)
    b_last = b[0:1] if rev else b[L - 1:L]
    kd = k * jnp.exp(b_last - b)
    qd = (q * GLA_DK ** -0.5) * jnp.exp(b)
    ki = k * jnp.exp(-b)

    def same_head(shape, rows_per_head, cols_per_head):
        r = lax.broadcasted_iota(jnp.int32, shape, 0) // rows_per_head
        c = lax.broadcasted_iota(jnp.int32, shape, 1) // cols_per_head
        return r == c

    k_blk = jnp.where(same_head((H * L, H * GLA_DK), L, GLA_DK), jnp.concatenate([ki] * H, axis=0), 0.0)
    v_blk = jnp.where(same_head((H * L, H * GLA_DV), L, GLA_DV), jnp.concatenate([v] * H, axis=0), 0.0)
    row = lax.broadcasted_iota(jnp.int32, (L, H * L), 0)
    src = lax.broadcasted_iota(jnp.int32, (L, H * L), 1) % L
    sc = jnp.where((row <= src) if rev else (row >= src), _dot_nt(qd, k_blk), 0.0)
    o = _dot_nt(qd, S) + _dot(sc, v_blk)
    s_new = S * jnp.exp(b_last) + jnp.where(same_head(S.shape, GLA_DV, GLA_DK), _dot_tn(v, kd), 0.0)
    return s_new, o


def _ssd_chunk(S, x, bm, cm, tail, dtb, alog, *, rev):
    L = SSD_L
    msk = _tri(L, rev)
    tri = msk.astype(F32)
    raw = tail[:, SSD_H:2 * SSD_H] if rev else tail[:, 0:SSD_H]
    dt = jax.nn.softplus(raw + dtb)
    dta = dt * (-jnp.exp(alog))
    acum = _dotx(tri, dta)
    a_last = acum[0:1] if rev else acum[L - 1:L]
    wst = dt * jnp.exp(a_last - acum)
    eac = jnp.exp(acum)
    tr = jnp.concatenate([acum, dt, wst, jnp.zeros((L, L - 3 * SSD_H), F32)], axis=1).T
    acum_t, dt_t, wst_t = tr[0:SSD_H], tr[SSD_H:2 * SSD_H], tr[2 * SSD_H:3 * SSD_H]
    decrow = jnp.exp(_dotx(jnp.broadcast_to(a_last, (8, SSD_H)), _head_expand())[0:1])
    lane = lax.broadcasted_iota(jnp.int32, (1, 2 * SSD_P), 1)
    m0 = (lane < SSD_P).astype(F32)
    m1 = 1.0 - m0
    pairs_per_group = SSD_H // SSD_G // 2
    y_parts, s_parts = [], []
    for g in range(SSD_G):
        ns = slice(g * SSD_N, (g + 1) * SSD_N)
        bg, cg = bm[:, ns], cm[:, ns]
        cb = _dot_nt(cg, bg)
        bgt = bg.T
        gs = slice(g * pairs_per_group * 2 * SSD_P, (g + 1) * pairs_per_group * 2 * SSD_P)
        y_carry = _dot(cg, S[:, gs])
        for jj in range(pairs_per_group):
            j = g * pairs_per_group + jj
            ls = slice(j * 2 * SSD_P, (j + 1) * 2 * SSD_P)
            xp, sp = x[:, ls], S[:, ls]
            xm = jnp.concatenate([xp * m0, xp * m1], axis=0)
            lhs, bw = [], []
            for h in (2 * j, 2 * j + 1):
                seg = acum[:, h:h + 1] - acum_t[h:h + 1, :]
                lhs.append(cb * jnp.exp(jnp.where(msk, seg, -jnp.inf)) * dt_t[h:h + 1, :])
                bw.append(bgt * wst_t[h:h + 1, :])
            e_pair = eac[:, 2 * j:2 * j + 1] * m0 + eac[:, 2 * j + 1:2 * j + 2] * m1
            y_parts.append(_dot(jnp.concatenate(lhs, axis=1), xm) + y_carry[:, jj * 2 * SSD_P:(jj + 1) * 2 * SSD_P] * e_pair)
            s_parts.append(sp * decrow[:, ls] + _dot(jnp.concatenate(bw, axis=1), xm))
    return jnp.concatenate(s_parts, axis=1), jnp.concatenate(y_parts, axis=1)


def _multi_chunk(chunk_fn, L, subs, nr):
    def fn(S, *args, rev):
        rows, params = args[:nr], args[nr:]
        ys = [None] * subs
        for j in (range(subs - 1, -1, -1) if rev else range(subs)):
            S, ys[j] = chunk_fn(S, *[r[j * L:(j + 1) * L] for r in rows], *params, rev=rev)
        return S, jnp.concatenate(ys, axis=0)

    return fn


def _scan_order(n, nx, rev, backward):
    nc = n - nx

    def fwd(s):
        return (n - 1 - s) if rev else jnp.where(s < nc, s + nx, s - nc)

    return (lambda s: fwd(n - 1 - s)) if backward else fwd


def _scan_fwd(name, chunk_fn, L, n, nx, rev, rows, params, state_shape, out_w, addend=None):
    order = _scan_order(n, nx, rev, False)
    nr, npar = len(rows), len(params)
    adds = [] if addend is None else [addend]

    def body(*refs):
        s_scr = refs[-1]

        @pl.when(pl.program_id(0) == 0)
        def _():
            s_scr[...] = jnp.zeros_like(s_scr)

        s_in = s_scr[...]
        y_ref, st_ref = refs[nr + npar + len(adds)], refs[nr + npar + len(adds) + 1]
        st_ref[0] = s_in
        s_new, y = chunk_fn(s_in, *[r[...] for r in refs[:nr]], *[p[...] for p in refs[nr:nr + npar]], rev=rev)
        y_ref[...] = y + refs[nr + npar][...] if adds else y
        s_scr[...] = s_new

    return pl.pallas_call(
        body, name=name, grid=(n,),
        in_specs=[pl.BlockSpec((L, w), lambda s, c=c: (order(s), c)) for _, w, c in rows] + [_full_spec(p) for p in params]
        + [pl.BlockSpec((L, out_w), lambda s: (order(s), 0)) for _ in adds],
        out_specs=[pl.BlockSpec((L, out_w), lambda s: (order(s), 0)),
                   pl.BlockSpec((1,) + state_shape, lambda s: (order(s), 0, 0))],
        out_shape=[jax.ShapeDtypeStruct((n * L, out_w), F32), jax.ShapeDtypeStruct((n,) + state_shape, F32)],
        scratch_shapes=[pltpu.VMEM(state_shape, F32)],
        compiler_params=_cp(("arbitrary",)),
    )(*[a for a, _, _ in rows], *params, *adds)


def _scan_bwd(name, chunk_fn, L, n, nx, rev, rows, params, states, dy, state_shape, out_w, addends=None):
    order = _scan_order(n, nx, rev, True)
    dy_blocks = dy.shape[0] // L
    nr, npar = len(rows), len(params)
    adds = [] if addends is None else list(addends)

    def body(*refs):
        i = pl.program_id(0)
        ds_scr = refs[-1]
        rv = [r[...] for r in refs[:nr]]
        pv = [p[...] for p in refs[nr:nr + npar]]
        st_ref, dy_ref = refs[nr + npar], refs[nr + npar + 1]
        a_refs = refs[nr + npar + 2:nr + npar + 2 + len(adds)]
        o_refs = refs[nr + npar + 2 + len(adds):-1]
        p_refs = o_refs[nr:]

        @pl.when(i == 0)
        def _():
            ds_scr[...] = jnp.zeros_like(ds_scr)
            for p_ref in p_refs:
                p_ref[...] = jnp.zeros_like(p_ref)

        _, vjp = jax.vjp(functools.partial(chunk_fn, rev=rev), st_ref[0], *rv, *pv)
        dy_blk = jnp.where(order(i) < dy_blocks, dy_ref[...].astype(F32), 0.0)
        grads = vjp((ds_scr[...], dy_blk))
        ds_scr[...] = grads[0]
        for j, (o_ref, g) in enumerate(zip(o_refs[:nr], grads[1:1 + nr])):
            o_ref[...] = g + a_refs[j][...] if adds else g
        for p_ref, g in zip(p_refs, grads[1 + nr:]):
            p_ref[...] += g

    row_specs = [pl.BlockSpec((L, w), lambda s: (order(s), 0)) for _, w, _ in rows]
    res = pl.pallas_call(
        body, name=name, grid=(n,),
        in_specs=[pl.BlockSpec((L, w), lambda s, c=c: (order(s), c)) for _, w, c in rows] + [_full_spec(p) for p in params]
        + [pl.BlockSpec((1,) + state_shape, lambda s: (order(s), 0, 0)),
           pl.BlockSpec((L, out_w), lambda s: (jnp.minimum(order(s), dy_blocks - 1), 0))]
        + row_specs[:len(adds)],
        out_specs=row_specs + [_full_spec(p) for p in params],
        out_shape=[jax.ShapeDtypeStruct((n * L, w), F32) for _, w, _ in rows] + [jax.ShapeDtypeStruct(p.shape, F32) for p in params],
        scratch_shapes=[pltpu.VMEM(state_shape, F32)],
        compiler_params=_cp(("arbitrary",)),
    )(*[a for a, _, _ in rows], *params, states, dy, *adds)
    return res[:nr], res[nr:]


CONV_W = 1024
CONV_COLBLK = (0, 1, 4)


def _conv_specs(nb, src_blk):
    halo = TM // 8
    return [pl.BlockSpec((TM, CONV_W), lambda j, i: (i, src_blk(j))),
            pl.BlockSpec((8, CONV_W), lambda j, i: (jnp.maximum(i * halo - 1, 0), src_blk(j))),
            pl.BlockSpec((8, CONV_W), lambda j, i: (jnp.minimum(i * halo + halo, nb * halo - 1), src_blk(j)))]


def _conv_ext(i, nb, cur, prev, nxt):
    has_prev = jnp.logical_and(i > 0, i < nb - 1)
    has_next = i < nb - 2
    return jnp.concatenate([jnp.where(has_prev, prev, 0.0), cur, jnp.where(has_next, nxt, 0.0)], axis=0)


def _conv_taps(ext, w, flip):
    acc = None
    for j in range(SSD_K):
        wj = w[SSD_K - 1 - j:SSD_K - j, :] if flip else w[j:j + 1, :]
        term = wj * ext[6 + j:6 + j + TM, :]
        acc = term if acc is None else acc + term
    return acc


def _conv(name, src, w8, b1, nb, *, permuted_src, act, flip, out_dtype):
    src_blk = (lambda j: jnp.where(j == 2, CONV_COLBLK[2], j)) if permuted_src else (lambda j: j)

    def body(cur, prev, nxt, w_ref, b_ref, o_ref):
        ext = _conv_ext(pl.program_id(1), nb, cur[...].astype(F32), prev[...].astype(F32), nxt[...].astype(F32))
        acc = _conv_taps(ext, w_ref[...], flip)
        if act:
            acc = jax.nn.silu(acc + b_ref[...])
        o_ref[...] = acc.astype(o_ref.dtype)

    return pl.pallas_call(
        body, name=name, grid=(3, nb),
        in_specs=_conv_specs(nb, src_blk) + [pl.BlockSpec((8, CONV_W), lambda j, i: (0, j)), pl.BlockSpec((1, CONV_W), lambda j, i: (0, j))],
        out_specs=pl.BlockSpec((TM, CONV_W), lambda j, i: (i, j)),
        out_shape=jax.ShapeDtypeStruct((nb * TM, 3 * CONV_W), out_dtype),
        compiler_params=_cp(("parallel", "parallel")),
    )(src, src, src, w8, b1)


def _conv_bwd_pre(name, p1, w8, b1, dxbc_parts, nb):
    src_blk = lambda j: jnp.where(j == 2, CONV_COLBLK[2], j)
    xs_parts, bc_parts = dxbc_parts
    n_x, n_bc = len(xs_parts), len(bc_parts)
    x_blocks = [p.shape[0] // TM for p in xs_parts]

    def body(*refs):
        cur, prev, nxt, w_ref, b_ref = refs[:5]
        d_refs = refs[5:5 + n_x + n_bc]
        da_ref, dw_ref, db_ref = refs[5 + n_x + n_bc:]
        j, i = pl.program_id(0), pl.program_id(1)
        ext = _conv_ext(i, nb, cur[...], prev[...], nxt[...])
        acc = _conv_taps(ext, w_ref[...], False) + b_ref[...]
        dx = d_refs[0][...]
        for r, blocks in zip(d_refs[1:n_x], x_blocks[1:]):
            dx = dx + jnp.where(i < blocks, r[...], 0.0)
        dbc = jnp.concatenate([d_refs[n_x][...], d_refs[n_x + 1][...]], axis=1)
        dy = jnp.where(j == 2, dbc, dx)
        sg = jax.nn.sigmoid(acc)
        da = dy * (sg + acc * sg * (1.0 - sg))
        da_ref[...] = da

        @pl.when(i == 0)
        def _():
            dw_ref[...] = jnp.zeros_like(dw_ref)
            db_ref[...] = jnp.zeros_like(db_ref)

        rows = [jnp.sum(da * ext[6 + t:6 + t + TM, :], axis=0, keepdims=True) for t in range(SSD_K)]
        dw_ref[...] += jnp.concatenate(rows + [jnp.zeros((8 - SSD_K, CONV_W), F32)], axis=0)
        db_ref[...] += jnp.sum(da, axis=0, keepdims=True)

    x_specs = [pl.BlockSpec((TM, CONV_W), lambda j, i, b=b: (jnp.minimum(i, b - 1), jnp.minimum(j, 1))) for b in x_blocks]
    bc_specs = [pl.BlockSpec((TM, 512), lambda j, i: (i, 0)) for _ in bc_parts]
    return pl.pallas_call(
        body, name=name, grid=(3, nb),
        in_specs=_conv_specs(nb, src_blk) + [pl.BlockSpec((8, CONV_W), lambda j, i: (0, j)), pl.BlockSpec((1, CONV_W), lambda j, i: (0, j))]
        + x_specs + bc_specs,
        out_specs=[pl.BlockSpec((TM, CONV_W), lambda j, i: (i, j)), pl.BlockSpec((8, CONV_W), lambda j, i: (0, j)),
                   pl.BlockSpec((1, CONV_W), lambda j, i: (0, j))],
        out_shape=[jax.ShapeDtypeStruct((nb * TM, 3 * CONV_W), F32), jax.ShapeDtypeStruct((8, 3 * CONV_W), F32),
                   jax.ShapeDtypeStruct((1, 3 * CONV_W), F32)],
        compiler_params=_cp(("arbitrary", "arbitrary")),
    )(p1, p1, p1, w8, b1, *xs_parts, *bc_parts)


def _grid_block(a):
    nbv = TM // a
    blk_b = max(nbv, 8)
    return nbv, blk_b, blk_b // nbv


def _grid_spec(a, nb):
    _, blk_b, per = _grid_block(a)
    return pl.BlockSpec((a, blk_b, D), lambda i: (0, jnp.minimum(i, nb - 2) // per, 0))


def _grid_rows(v_ref, a, i):
    nbv, _, per = _grid_block(a)

    def pick(ph):
        return jnp.concatenate([v_ref[:, ph * nbv + t, :] for t in range(nbv)], axis=0)

    out = pick(0)
    for ph in range(1, per):
        out = jnp.where(i % per == ph, pick(ph), out)
    return out


def _loss_head(x, f, target, modp, g_final, rows_r):
    tview = target.reshape(rows_r, target.shape[0] // rows_r, D)
    nb = x.shape[0] // TM + 1

    def fn(x_, f_, tgt, modp_, g_):
        xn = x_ + _sel_mod(modp_, 0.0)[5:6] * f_
        err = _rms(xn) * g_ - tgt
        return 0.5 * jnp.sum(jnp.mean(err * err, axis=-1))

    def body(x_ref, f_ref, t_ref, m_ref, g_ref, l_ref, dx_ref, df_ref, dm_ref, dg_ref):
        i = pl.program_id(0)
        tgt = _grid_rows(t_ref, rows_r, i)
        l, vjp = jax.vjp(lambda a_, b_, c_, d_: fn(a_, b_, tgt, c_, d_), x_ref[...], f_ref[...], m_ref[...], g_ref[...])
        dx, df, dm, dg = vjp(jnp.ones((), F32))

        @pl.when(i == 0)
        def _():
            l_ref[...] = jnp.zeros_like(l_ref)
            dm_ref[...] = jnp.zeros_like(dm_ref)
            dg_ref[...] = jnp.zeros_like(dg_ref)

        l_ref[...] += jnp.reshape(l, (1, 1))
        dx_ref[...] = dx
        df_ref[...] = df.astype(df_ref.dtype)
        dm_ref[...] += dm
        dg_ref[...] += dg

    rowspec = pl.BlockSpec((TM, D), lambda i: (i, 0))
    return pl.pallas_call(
        body, name="loss_head", grid=(nb - 1,),
        in_specs=[rowspec, rowspec, _grid_spec(rows_r, nb), _full_spec(modp), _full_spec(g_final)],
        out_specs=[pl.BlockSpec((1, 1), lambda i: (0, 0)), rowspec, rowspec, _full_spec(modp), _full_spec(g_final)],
        out_shape=[jax.ShapeDtypeStruct((1, 1), F32), jax.ShapeDtypeStruct(x.shape, F32), jax.ShapeDtypeStruct(x.shape, MXU_DTYPE),
                   jax.ShapeDtypeStruct(modp.shape, F32), jax.ShapeDtypeStruct(g_final.shape, F32)],
        compiler_params=_cp(("arbitrary",)),
    )(x, f, tview, modp, g_final)


def _repack(name, shards, segs, wp):
    nd, kk, ws = shards.shape
    tr = 128
    used = sum(e - s for s, e in segs)

    def body(a_ref, o_ref):
        full = jnp.concatenate([a_ref[d].astype(F32) for d in range(nd)], axis=1)
        parts = [full[:, s:e] for s, e in segs]
        if wp > used:
            parts.append(jnp.zeros((tr, wp - used), F32))
        o_ref[...] = jnp.concatenate(parts, axis=1).astype(o_ref.dtype)

    return pl.pallas_call(
        body, name=name, grid=(kk // tr,),
        in_specs=[pl.BlockSpec((nd, tr, ws), lambda i: (0, i, 0))],
        out_specs=pl.BlockSpec((tr, wp), lambda i: (i, 0)),
        out_shape=jax.ShapeDtypeStruct((kk, wp), MXU_DTYPE),
        compiler_params=_cp(("parallel",)),
    )(shards)


def _unpack(name, dw, segs, ws, out_dtype):
    kk, wp = dw.shape
    tr = 128
    order = sorted(range(len(segs)), key=lambda i: segs[i][0])
    offs, o = [], 0
    for s, e in segs:
        offs.append(o)
        o += e - s

    def body(a_ref, o_ref):
        a = a_ref[...].astype(F32)
        full = jnp.concatenate([a[:, offs[i]:offs[i] + segs[i][1] - segs[i][0]] for i in order], axis=1)
        for d in range(NDEV):
            o_ref[d] = full[:, d * ws:(d + 1) * ws].astype(o_ref.dtype)

    return pl.pallas_call(
        body, name=name, grid=(kk // tr,),
        in_specs=[pl.BlockSpec((tr, wp), lambda i: (i, 0))],
        out_specs=pl.BlockSpec((NDEV, tr, ws), lambda i: (0, i, 0)),
        out_shape=jax.ShapeDtypeStruct((NDEV, kk, ws), out_dtype),
        compiler_params=_cp(("parallel",)),
    )(dw)


def _adam_math(w, g, m, v):
    m = ADAM_B1 * m + (1.0 - ADAM_B1) * g
    v = ADAM_B2 * v + (1.0 - ADAM_B2) * jnp.square(g)
    m_hat = m / (1.0 - ADAM_B1 ** ADAM_STEP)
    v_hat = v / (1.0 - ADAM_B2 ** ADAM_STEP)
    delta = -ADAM_LR * (m_hat / (jnp.sqrt(v_hat) + ADAM_EPS) + ADAM_WD * w)
    return delta, m, v


def _adam(name, w, parts, m, v, after):
    r, c = w.shape
    nsec, npart = len(parts), parts[0].shape[0]
    rs = r // nsec
    tr = _pick(rs, (256, 128, 64, 32, 16, 8)) if rs * c * 4 > (1 << 20) else rs
    tiles = rs // tr

    def body(w_ref, *refs):
        m_ref, v_ref, _, g_ref, d_ref, nm_ref, nv_ref = refs[nsec:]
        sec = pl.program_id(0) // tiles
        g = None
        for a, p_ref in enumerate(refs[:nsec]):
            ga = p_ref[0].astype(F32)
            for s in range(1, npart):
                ga = ga + p_ref[s].astype(F32)
            g = ga if g is None else jnp.where(sec == a, ga, g)
        delta, nm, nv = _adam_math(w_ref[...], g, m_ref[...], v_ref[...])
        g_ref[...], d_ref[...], nm_ref[...], nv_ref[...] = g, delta, nm, nv

    spec = pl.BlockSpec((tr, c), lambda i: (i, 0))
    part_specs = [pl.BlockSpec((npart, tr, c), lambda i, a=a: (0, jnp.clip(i - a * tiles, 0, tiles - 1), 0)) for a in range(nsec)]
    return pl.pallas_call(
        body, name=name, grid=(r // tr,),
        in_specs=[spec] + part_specs + [spec, spec, ANY],
        out_specs=[spec] * 4, out_shape=[jax.ShapeDtypeStruct((r, c), F32)] * 4,
        compiler_params=_cp(("parallel",)),
    )(w, *parts, m, v, after)


def _mod_fwd(c_all, mod_w):
    nl, _, ws = mod_w.shape

    def body(c_ref, w_ref, o_ref):
        o_ref[0] = _dot(jax.nn.silu(c_ref[...]), w_ref[0])

    return pl.pallas_call(
        body, name="mod_fwd", grid=(nl,),
        in_specs=[_full_spec(c_all), pl.BlockSpec((1, D, ws), lambda i: (i, 0, 0))],
        out_specs=pl.BlockSpec((1, 16, ws), lambda i: (i, 0, 0)),
        out_shape=jax.ShapeDtypeStruct((nl, 16, ws), F32),
        compiler_params=_cp(("parallel",)),
    )(c_all, mod_w)


def _mod_bwd(c_all, mod_w, dm):
    nl, _, ws = mod_w.shape

    def body(c_ref, w_ref, d_ref, dw_ref, dc_ref):
        dw_ref[0] = _dot_tn(jax.nn.silu(c_ref[...]), d_ref[0])
        dc_ref[0] = _dot_nt(d_ref[0], w_ref[0])

    return pl.pallas_call(
        body, name="mod_bwd", grid=(nl,),
        in_specs=[_full_spec(c_all), pl.BlockSpec((1, D, ws), lambda i: (i, 0, 0)), pl.BlockSpec((1, 16, ws), lambda i: (i, 0, 0))],
        out_specs=[pl.BlockSpec((1, D, ws), lambda i: (i, 0, 0)), pl.BlockSpec((1, 16, D), lambda i: (i, 0, 0))],
        out_shape=[jax.ShapeDtypeStruct((nl, D, ws), F32), jax.ShapeDtypeStruct((nl, 16, D), F32)],
        compiler_params=_cp(("parallel",)),
    )(c_all, mod_w, dm)


def _sum_parts(name, parts):
    npart, r, c = parts.shape

    def body(p_ref, o_ref):
        g = p_ref[0].astype(F32)
        for s in range(1, npart):
            g = g + p_ref[s].astype(F32)
        o_ref[...] = g

    return pl.pallas_call(body, name=name, out_shape=jax.ShapeDtypeStruct((r, c), F32), compiler_params=_cp())(parts)


MESH = pl.DeviceIdType.MESH
ANY = pl.BlockSpec(memory_space=pl.ANY)
N_PEERS = NDEV - 1


def _mesh_pos():
    return lax.axis_index("x"), lax.axis_index("y"), lax.axis_index("c")


def _slot(px, py, pc):
    return 4 * px + 2 * py + pc


def _two_level_gather(x_refs, o_refs, send_sems, recv_sems, local_sems):
    x, y, c = _mesh_pos()
    me, sibling = (x, y, c), (x, y, 1 - c)
    chips = [(1 - x, y), (x, 1 - y), (1 - x, 1 - y)]
    n = len(x_refs)

    def copy(a, k, block, to, src=None):
        dst = o_refs[a].at[_slot(*block)]
        return pltpu.make_async_remote_copy(src_ref=dst if src is None else src, dst_ref=dst, send_sem=send_sems.at[a, k],
                                            recv_sem=recv_sems.at[a, k], device_id=to, device_id_type=MESH)

    mine = [pltpu.make_async_copy(x_refs[a], o_refs[a].at[_slot(*me)], local_sems.at[a]) for a in range(n)]
    for cp in mine:
        cp.start()
    first = []
    for a in range(n):
        first.append(copy(a, 0, me, sibling, src=x_refs[a]))
        first += [copy(a, 1 + j, me, (*chip, c), src=x_refs[a]) for j, chip in enumerate(chips)]
    for cp in first:
        cp.start()
    passed = []
    for j, chip in enumerate(chips):
        for a in range(n):
            copy(a, 1 + j, (*chip, c), me).wait_recv()
            fwd = copy(a, 4 + j, (*chip, c), sibling)
            fwd.start()
            passed.append(fwd)
    for a in range(n):
        copy(a, 0, sibling, me).wait_recv()
        for j, chip in enumerate(chips):
            copy(a, 4 + j, (*chip, 1 - c), me).wait_recv()
    for cp in first + passed:
        cp.wait_send()
    for cp in mine:
        cp.wait()


def _ag_small(name, x):
    r, c = x.shape

    def body(x_ref, o_ref, send_sems, recv_sems, local_sems):
        _two_level_gather([x_ref], [o_ref], send_sems, recv_sems, local_sems)

    return pl.pallas_call(
        body, name=name, out_shape=jax.ShapeDtypeStruct((NDEV, r, c), x.dtype),
        in_specs=[pl.BlockSpec(memory_space=pltpu.VMEM)], out_specs=pl.BlockSpec(memory_space=pltpu.VMEM),
        scratch_shapes=[pltpu.SemaphoreType.DMA((1, N_PEERS)), pltpu.SemaphoreType.DMA((1, N_PEERS)), pltpu.SemaphoreType.DMA((1,))],
        compiler_params=pltpu.CompilerParams(vmem_limit_bytes=VMEM_LIMIT),
    )(x)


def _ag_big(name, shards):
    n = len(shards)

    def body(*refs):
        _two_level_gather(refs[:n], refs[n:2 * n], *refs[2 * n:])

    return pl.pallas_call(
        body, name=name, out_shape=[jax.ShapeDtypeStruct((NDEV,) + s.shape, s.dtype) for s in shards],
        in_specs=[ANY] * n, out_specs=[ANY] * n,
        scratch_shapes=[pltpu.SemaphoreType.DMA((n, N_PEERS)), pltpu.SemaphoreType.DMA((n, N_PEERS)), pltpu.SemaphoreType.DMA((n,))],
    )(*shards)


HBM = pl.BlockSpec(memory_space=pltpu.HBM)
SEM = pl.BlockSpec(memory_space=pltpu.SEMAPHORE)
EFFECT = pltpu.SideEffectType.DATAFLOW_SIDE_EFFECTING


def _peers(x, y, c):
    return [(k - 1, ((1 - x) if k & 4 else x, (1 - y) if k & 2 else y, (1 - c) if k & 1 else c)) for k in range(1, NDEV)]


def _xchg_copy(src_refs, land_refs, send_sems, recv_sems, a, k, peer, me, scatter):
    src = src_refs[a].at[_slot(*peer)] if scatter else src_refs[a]
    return pltpu.make_async_remote_copy(src_ref=src, dst_ref=land_refs[a].at[me], send_sem=send_sems.at[a * N_PEERS + k],
                                        recv_sem=recv_sems.at[a * N_PEERS + k], device_id=peer, device_id_type=MESH)


def _xchg_start(name, srcs, lands, deps, scatter):
    n, nd = len(srcs), len(deps)

    def body(*refs):
        src_refs, land_refs = refs[:n], refs[n:2 * n]
        send_sems, recv_sems, token = refs[2 * n + nd], refs[2 * n + nd + 1], refs[-1]
        x, y, c = _mesh_pos()
        me = _slot(x, y, c)
        for k, peer in _peers(x, y, c):
            for a in range(n):
                _xchg_copy(src_refs, land_refs, send_sems, recv_sems, a, k, peer, me, scatter).start()
        token[...] = jnp.zeros_like(token)

    res = pl.pallas_call(
        body, name=name,
        out_shape=(pltpu.SemaphoreType.DMA((n * N_PEERS,)), pltpu.SemaphoreType.DMA((n * N_PEERS,)),
                   *[pltpu.HBM(s.shape, s.dtype) for s in srcs], *[pltpu.HBM(s.shape, s.dtype) for s in lands],
                   jax.ShapeDtypeStruct((8, 128), F32)),
        in_specs=[HBM] * (2 * n) + [ANY] * nd,
        out_specs=(SEM, SEM, *([HBM] * (2 * n)), pl.BlockSpec(memory_space=pltpu.VMEM)),
        input_output_aliases={i: 2 + i for i in range(2 * n)},
        compiler_params=pltpu.CompilerParams(has_side_effects=EFFECT),
    )(*[pltpu.with_memory_space_constraint(s, pltpu.HBM) for s in srcs],
      *[pltpu.with_memory_space_constraint(s, pltpu.HBM) for s in lands], *deps)
    return res[0], res[1], res[2:2 + n], res[2 + n:2 + 2 * n], res[-1]


def _xchg_wait(name, send_sems, recv_sems, srcs, lands, after, scatter):
    n = len(srcs)

    def body(*refs):
        src_refs, land_refs = refs[:n], refs[n:2 * n]
        s_sems, r_sems = refs[2 * n], refs[2 * n + 1]
        x, y, c = _mesh_pos()
        me = _slot(x, y, c)
        for k, peer in _peers(x, y, c):
            for a in range(n):
                cp = _xchg_copy(src_refs, land_refs, s_sems, r_sems, a, k, peer, me, scatter)
                cp.wait_send()
                cp.wait_recv()

    res = pl.pallas_call(
        body, name=name,
        out_shape=[pltpu.HBM(s.shape, s.dtype) for s in srcs] + [pltpu.HBM(s.shape, s.dtype) for s in lands],
        in_specs=[HBM] * (2 * n) + [SEM, SEM, ANY], out_specs=[HBM] * (2 * n),
        input_output_aliases={i: i for i in range(2 * n)},
        compiler_params=pltpu.CompilerParams(has_side_effects=EFFECT),
    )(*srcs, *lands, send_sems, recv_sems, after)
    return res[n:]


def _landing(name, srcs, me, scatter):
    shapes = [s.shape[-2:] for s in srcs]

    def body(me_ref, *refs):
        for s_ref, o_ref in zip(refs[:len(srcs)], refs[len(srcs):]):
            o_ref[...] = s_ref[...].reshape(o_ref.shape)

    def slot_spec(r, c):
        return pl.BlockSpec((1, r, c), lambda i, me_ref: (me_ref[0], 0, 0))

    return pl.pallas_call(
        body, name=name, out_shape=[jax.ShapeDtypeStruct((NDEV, r, c), s.dtype) for s, (r, c) in zip(srcs, shapes)],
        grid_spec=pltpu.PrefetchScalarGridSpec(
            num_scalar_prefetch=1, grid=(1,),
            in_specs=[slot_spec(r, c) if scatter else pl.BlockSpec((r, c), lambda i, me_ref: (0, 0)) for r, c in shapes],
            out_specs=[slot_spec(r, c) for r, c in shapes]),
        compiler_params=_cp(("arbitrary",)),
    )(jnp.reshape(me, (1,)).astype(jnp.int32), *srcs)


STAGES = ("l0_mixer", "l0_ffn", "l1_mixer", "l1_ffn")
STAGE_LAYOUT = {"l0_mixer": (AB_SEGS, AB_P), "l1_mixer": (SSD_SEGS, SSD_P_W)}


class _Exchange:
    def __init__(self, shards, me):
        self.shards, self.me = shards, me
        self.pending, self.pending_grads, self.recv = {}, None, {}

    def _layout(self, stage):
        ws = self.shards[stage][0].shape[-1]
        return STAGE_LAYOUT.get(stage, (((0, NDEV * ws),), NDEV * ws)) + (ws,)

    def _start_gather(self, stage, deps):
        srcs = list(self.shards[stage])
        lands = _landing("own_" + stage, srcs, self.me, False)
        return _xchg_start("gather_start_" + stage, srcs, lands, deps, False)

    def get(self, stage, dep, thread):
        i = STAGES.index(stage)
        if i == 0:
            g_in, g_out = _ag_big("gather_" + stage, list(self.shards[stage]))
            ahead, deps = STAGES[1:3], [g_out, dep]
        else:
            ss, rs, srcs, lands, _ = self.pending.pop(stage)
            g_in, g_out = _xchg_wait("gather_wait_" + stage, ss, rs, srcs, lands, dep, False)
            ahead, deps = STAGES[i + 2:i + 3], [g_out]
        for nxt in ahead:
            self.pending[nxt] = self._start_gather(nxt, deps)
            deps = [self.pending[nxt][4]]
            thread = thread + self.pending[nxt][4][0, 0]
        segs, wp, _ = self._layout(stage)
        return _repack("repack_" + stage, g_in, segs, wp), g_out.reshape(-1, D), thread

    def put(self, stage, d_in, d_out, thread):
        segs, _, ws = self._layout(stage)
        parts = [_unpack("unpack_" + stage, d_in, segs, ws, MXU_DTYPE), d_out.reshape(NDEV, -1, D)]
        deps = [parts[0]]
        if self.pending_grads is not None:
            deps = [self.finish(parts[0])[0]]
        self.staged = (stage, parts)
        return thread if stage == STAGES[0] else thread + self.start_last(deps)[0, 0]

    def start_last(self, deps):
        stage, parts = self.staged
        lands = _landing("own_grad_" + stage, parts, self.me, True)
        self.pending_grads = (stage,) + _xchg_start("scatter_start_" + stage, parts, lands, deps, True)
        return self.pending_grads[5]

    def finish(self, after):
        stage, ss, rs, srcs, lands, _ = self.pending_grads
        self.recv[stage] = _xchg_wait("scatter_wait_" + stage, ss, rs, srcs, lands, after, True)
        self.pending_grads = None
        return self.recv[stage]


def _mm_swiglu(name, h, w_in):
    m, kk = h.shape
    f = w_in.shape[1] // 2
    tm = _pick(m, (272, 256, 128))

    def body(h_ref, wg_ref, wu_ref, pf_ref, act_ref):
        a = h_ref[...].astype(MXU_DTYPE)
        g = jnp.dot(a, wg_ref[...].astype(MXU_DTYPE), preferred_element_type=F32).astype(MXU_DTYPE)
        u = jnp.dot(a, wu_ref[...].astype(MXU_DTYPE), preferred_element_type=F32).astype(MXU_DTYPE)
        pf_ref[0] = g
        pf_ref[1] = u
        act_ref[...] = (jax.nn.silu(g.astype(F32)) * u.astype(F32)).astype(act_ref.dtype)

    return pl.pallas_call(
        body, name=name, grid=(m // tm,),
        in_specs=[pl.BlockSpec((tm, kk), lambda i: (i, 0)), pl.BlockSpec((kk, f), lambda i: (0, 0)), pl.BlockSpec((kk, f), lambda i: (0, 1))],
        out_specs=[pl.BlockSpec((2, tm, f), lambda i: (0, i, 0)), pl.BlockSpec((tm, f), lambda i: (i, 0))],
        out_shape=[jax.ShapeDtypeStruct((2, m, f), MXU_DTYPE), jax.ShapeDtypeStruct((m, f), MXU_DTYPE)],
        compiler_params=_cp(("parallel",)),
    )(h, w_in, w_in)


def _row_gate_up(pf):
    f = pf.shape[2]
    return ([pf], [pl.BlockSpec((2, TM, f), lambda i: (0, i, 0))],
            lambda r: jnp.concatenate([r[0], r[1]], axis=1).astype(F32), 2 * f)


def _ffn_fwd(tag, h, w_in, w_out, nb, cb):
    pf, act = _mm_swiglu(tag + "_ffn_in", h, w_in)
    return pf, act, _mm(tag + "_ffn_out", act, w_out, "nn", F32)


def _ffn_bwd(tag, h, pf, act, df, w_in, w_out, nb, cb):
    dw_out = _mm(tag + "_ffn_out_dw", act, df, "tn", MXU_DTYPE)
    dact = _mm(tag + "_ffn_out_dx", df, w_out, "nt", MXU_DTYPE)
    (dpf,), _ = _rowwise_vjp(tag + "_swiglu_bwd", _fn_swiglu, nb, cb, [_row_gate_up(pf)], [], [_row(dact)], [(0, MXU_DTYPE, None)])
    dw_in = _mm(tag + "_ffn_in_dw", h, dpf, "tn", MXU_DTYPE)
    dh = _mm(tag + "_ffn_in_dx", dpf, w_in, "nt", MXU_DTYPE)
    return dw_out, dw_in, dh


def _local_step(x, ctx, target, mod, P, comm):
    T = x.shape[0]
    N = T + CTX
    nb, cb = N // TM, N // TM - 1
    R = T // GRID_W
    mod0, mod1 = mod[0], mod[1]
    ng = P["norm_g"]
    g00, g01, g10, g11 = ng[0, 0][None], ng[0, 1][None], ng[1, 0][None], ng[1, 1][None]
    pre = functools.partial(_fn_prenorm, a=0, b=1)
    rpre = functools.partial(_fn_resid_prenorm, gi=2, a=3, b=4)
    res5 = functools.partial(_fn_resid, gi=5)
    dirs = (("f", False), ("b", True))

    xc0 = jnp.concatenate([x, ctx], axis=0)
    w_ab_in, w_ab_out, g00 = comm.get("l0_mixer", mod, g00)
    (h0,) = _rowwise("l0_prenorm", pre, nb, cb, [_row(xc0)], [g00, mod0], [(D, MXU_DTYPE)])
    p0 = _mm("l0_in", h0, w_ab_in, "nn", F32)
    gla_rows = [(p0, 512, 0), (p0, 256, 8), (p0, 256, 9), (p0, 128, 20)]
    gla_blk = _multi_chunk(_gla_chunk, GLA_L, TM // GLA_L, len(gla_rows))
    gla_par = {d: [P["ab_gate_w"][int(r)], P["ab_gate_b"][int(r)][None]] for d, r in dirs}
    gla_state = (GLA_H * GLA_DV, GLA_H * GLA_DK)
    o, st0 = None, {}
    for d, rev in dirs:
        o, st0[d] = _scan_fwd("gla_fwd_" + d, gla_blk, TM, nb, cb, rev, gla_rows, gla_par[d], gla_state, GLA_H * GLA_DV, o)
    n128, cb128 = N // GMLP_L, T // GMLP_L
    mix_rows = [_row(o, tm=GMLP_L)] + [_row(p0, 512, j, tm=GMLP_L) for j in (1, 2, 3)]
    mix_par = [P["ab_gla_norm_g"], P["ab_vnorm_g"], P["ab_spatial_w"].reshape(GMLP_G * GMLP_L, GMLP_L), P["ab_spatial_b"].T]
    (cat0,) = _rowwise("l0_mix", _fn_mixpost, n128, cb128, mix_rows, mix_par, [(D, MXU_DTYPE)], tm=GMLP_L)
    y0 = _mm("l0_out", cat0, w_ab_out, "nn", F32)
    w_fi0, w_fo0, g01 = comm.get("l0_ffn", y0, g01)
    x1, h1 = _rowwise("l0_ffn_prenorm", rpre, nb, cb, [_row(xc0), _row(y0)], [g01, mod0, mod0], [(D, F32), (D, MXU_DTYPE)])
    pf0, act0, f0 = _ffn_fwd("l0", h1, w_fi0, w_fo0, nb, cb)
    w_ssd_in, w_ssd_out, g10 = comm.get("l1_mixer", f0, g10)
    x2p, h2 = _rowwise("l0_resid_l1_prenorm", functools.partial(_fn_resid_prenorm, gi=5, a=0, b=1), nb, cb,
                       [_row_grid(x1, R, nb), _row_grid(f0, R, nb)], [g10, mod0, mod1], [(D, F32), (D, MXU_DTYPE)])
    p1 = _mm("l1_in", h2, w_ssd_in, "nn", F32)
    conv_w8 = jnp.concatenate([P["ssd_conv_w"], jnp.zeros((8 - SSD_K, 3 * CONV_W), F32)], axis=0)
    xbc = _conv("l1_conv", p1, conv_w8, P["ssd_conv_b"], nb, permuted_src=True, act=True, flip=False, out_dtype=F32)
    ssd_rows = [(xbc, SSD_INNER, 0), (xbc, 512, 4), (xbc, 512, 5), (p1, 128, 40)]
    ssd_blk = _multi_chunk(_ssd_chunk, SSD_L, TM // SSD_L, len(ssd_rows))
    ssd_par = {d: [P["ssd_dt_bias"][int(r)][None], P["ssd_a_log"][int(r)][None]] for d, r in dirs}
    ssd_state = (SSD_N, SSD_INNER)
    ys, st1 = None, {}
    for d, rev in dirs:
        ys, st1[d] = _scan_fwd("ssd_fwd_" + d, ssd_blk, TM, nb, cb, rev, ssd_rows, ssd_par[d], ssd_state, SSD_INNER, ys)
    fin_rows = [_row(ys), _row(xbc, SSD_INNER, 0), _row(p1, SSD_INNER, 1)]
    fin_par = [P["ssd_d"], P["ssd_norm_g"]]
    (yn,) = _rowwise("l1_finish", _fn_ssd_finish, cb, cb, fin_rows, fin_par, [(SSD_INNER, MXU_DTYPE)])
    y1 = _mm("l1_out", yn, w_ssd_out, "nn", F32)
    w_fi1, w_fo1, g11 = comm.get("l1_ffn", y1, g11)
    x3, h3 = _rowwise("l1_ffn_prenorm", rpre, cb, cb, [_row(x2p), _row(y1)], [g11, mod1, mod1], [(D, F32), (D, MXU_DTYPE)])
    pf1, act1, f1 = _ffn_fwd("l1", h3, w_fi1, w_fo1, cb, cb)
    loss, dx3, df1, dm1_j, d_final_g = _loss_head(x3, f1, target, mod1, P["final_norm_g"], R)

    dP = {"final_norm_g": d_final_g}
    dwo1, dwi1, dh3 = _ffn_bwd("l1", h3, pf1, act1, df1, w_fi1, w_fo1, cb, cb)
    g11 = comm.put("l1_ffn", dwi1, dwo1, g11)
    (dx2p_a, dy1), (dg11, dm1_a, dm1_b) = _rowwise_vjp(
        "l1_ffn_prenorm_bwd", rpre, cb, cb, [_row(x2p), _row(y1)], [g11, mod1, mod1], [_row(dx3), _row(dh3)],
        [(0, F32, None), (1, MXU_DTYPE, None)])
    d_ssd_out = _mm("l1_out_dw", yn, dy1, "tn", MXU_DTYPE)
    dyn = _mm("l1_out_dx", dy1, w_ssd_out, "nt", MXU_DTYPE)
    (dys, dxs, dz), (dP["ssd_d"], dP["ssd_norm_g"]) = _rowwise_vjp(
        "l1_finish_bwd", _fn_ssd_finish, cb, cb, fin_rows, fin_par, [_row(dyn)],
        [(0, F32, None), (1, F32, None), (2, MXU_DTYPE, None)])
    dssd, ddtb, dalog = None, [], []
    for d, rev in dirs:
        dssd, (ddtb_, dalog_) = _scan_bwd("ssd_bwd_" + d, ssd_blk, TM, nb, cb, rev, ssd_rows, ssd_par[d], st1[d], dys, ssd_state,
                                          SSD_INNER, dssd)
        ddtb.append(ddtb_); dalog.append(dalog_)
    dx_s, db_s, dc_s, dtl = dssd
    dP["ssd_dt_bias"] = jnp.concatenate(ddtb, axis=0)
    dP["ssd_a_log"] = jnp.concatenate(dalog, axis=0)
    dacc, dcw8, dP["ssd_conv_b"] = _conv_bwd_pre("l1_conv_bwd", p1, conv_w8, P["ssd_conv_b"], ([dx_s, dxs], [db_s, dc_s]), nb)
    dP["ssd_conv_w"] = dcw8[:SSD_K]
    dpc = _conv("l1_conv_dx", dacc, conv_w8, jnp.zeros((1, 3 * CONV_W), F32), nb, permuted_src=False, act=False, flip=True,
                out_dtype=MXU_DTYPE)
    cat1 = functools.partial(_fn_concat, sums=(1, 1, 1, 1), pad=SSD_P_W - 5248)
    (dp1,) = _rowwise("l1_dp", cat1, nb, cb, [_row(dpc, SSD_INNER, 0), _row(dz, valid=cb), _row(dpc, 1024, 2), _row(dtl)],
                      [], [(SSD_P_W, MXU_DTYPE)])
    g10 = comm.put("l1_mixer", _mm("l1_in_dw", h2, dp1, "tn", F32), d_ssd_out, g10)
    dh2 = _mm("l1_in_dx", dp1, w_ssd_in, "nt", MXU_DTYPE)
    (dx2p,), (dg10, dm1_f) = _rowwise_vjp("l1_prenorm_bwd", pre, nb, cb, [_row(x2p)], [g10, mod1], [_row(dh2)],
                                          [(0, F32, _row(dx2p_a, valid=cb))])

    (dx1_a, df0), (dm0_e,) = _rowwise_vjp("l0_resid_bwd", res5, nb, cb, [_row(x1), _row(f0)], [mod0],
                                          [_row_grid(dx2p, GRID_W, nb)], [(0, F32, None), (1, MXU_DTYPE, None)])
    dwo0, dwi0, dh1 = _ffn_bwd("l0", h1, pf0, act0, df0, w_fi0, w_fo0, nb, cb)
    g01 = comm.put("l0_ffn", dwi0, dwo0, g01)
    (dxc0_a, dy0), (dg01, dm0_a, dm0_b) = _rowwise_vjp(
        "l0_ffn_prenorm_bwd", rpre, nb, cb, [_row(xc0), _row(y0)], [g01, mod0, mod0], [_row(dx1_a), _row(dh1)],
        [(0, F32, None), (1, MXU_DTYPE, None)])
    d_ab_out = _mm("l0_out_dw", cat0, dy0, "tn", MXU_DTYPE)
    dcat0 = _mm("l0_out_dx", dy0, w_ab_out, "nt", MXU_DTYPE)
    (do, dr, du, dgm), (dP["ab_gla_norm_g"], dP["ab_vnorm_g"], dsw, dsb_t) = _rowwise_vjp(
        "l0_mix_bwd", _fn_mixpost, n128, cb128, mix_rows, mix_par, [_row(dcat0, tm=GMLP_L)],
        [(0, F32, None), (1, MXU_DTYPE, None), (2, MXU_DTYPE, None), (3, MXU_DTYPE, None)], tm=GMLP_L)
    dP["ab_spatial_w"] = dsw.reshape(GMLP_G, GMLP_L, GMLP_L)
    dP["ab_spatial_b"] = dsb_t.T
    gl, dgw, dgb = None, [], []
    for d, rev in dirs:
        gl, (dgw_, dgb_) = _scan_bwd("gla_bwd_" + d, gla_blk, TM, nb, cb, rev, gla_rows, gla_par[d], st0[d], do,
                                     gla_state, GLA_H * GLA_DV, gl)
        dgw.append(dgw_[None]); dgb.append(dgb_)
    dP["ab_gate_w"] = jnp.concatenate(dgw, axis=0)
    dP["ab_gate_b"] = jnp.concatenate(dgb, axis=0)
    cat0f = functools.partial(_fn_concat, sums=(1,) * 7, pad=AB_P - 2688)
    (dp0,) = _rowwise("l0_dp", cat0f, nb, cb, [_row(gl[0]), _row(dr), _row(du), _row(dgm), _row(gl[1]), _row(gl[2]), _row(gl[3])],
                      [], [(AB_P, MXU_DTYPE)])
    g00 = comm.put("l0_mixer", _mm("l0_in_dw", h0, dp0, "tn", F32), d_ab_out, g00)
    dh0 = _mm("l0_in_dx", dp0, w_ab_in, "nt", MXU_DTYPE)
    (dxc0,), (dg00, dm0_s) = _rowwise_vjp("l0_prenorm_bwd", pre, nb, cb, [_row(xc0)], [g00, mod0], [_row(dh0)],
                                          [(0, F32, _row(dxc0_a))])
    dP["norm_g"] = jnp.concatenate([dg00, dg01, dg10, dg11], axis=0).reshape(2, 2, D)
    dmod = jnp.stack([dm0_s + dm0_a + dm0_b + dm0_e, dm1_f + dm1_a + dm1_b + dm1_j])
    return loss, dxc0[:T], dmod, dP


WEIGHTS = ("c_ctx", "mod_w", "mod_b", "norm_g", "ffn_w_in", "ffn_w_out", "ab_w_in", "ab_gate_w", "ab_gate_b", "ab_gla_norm_g",
           "ab_vnorm_g", "ab_spatial_w", "ab_spatial_b", "ab_w_out", "ssd_w_in", "ssd_conv_w", "ssd_conv_b", "ssd_dt_bias",
           "ssd_a_log", "ssd_d", "ssd_norm_g", "ssd_w_out", "final_norm_g")
SMALL_SHARDED = ("norm_g", "ab_gate_w", "ab_gate_b", "ssd_conv_w", "ssd_conv_b", "ssd_norm_g")
SMALL = ("c_ctx", "mod_b", "norm_g", "ab_gate_w", "ab_gate_b", "ab_gla_norm_g", "ab_vnorm_g", "ab_spatial_w", "ab_spatial_b",
         "ssd_conv_w", "ssd_conv_b", "ssd_dt_bias", "ssd_a_log", "ssd_d", "ssd_norm_g", "final_norm_g")
LANES = 1024


def _pack(arrs, rows_multiple=8):
    flat = jnp.concatenate([a.reshape(-1).astype(F32) for a in arrs])
    rows = -(-flat.shape[0] // LANES)
    rows = -(-rows // rows_multiple) * rows_multiple
    return jnp.pad(flat, (0, rows * LANES - flat.shape[0])).reshape(rows, LANES)


def _unpack_flat(buf, shapes):
    lead = buf.shape[:-2]
    flat = buf.reshape(lead + (-1,))
    out, o = [], 0
    for s in shapes:
        n = math.prod(s)
        out.append(flat[..., o:o + n].reshape(lead + tuple(s)))
        o += n
    return out


def _unshard(g):
    g = jnp.moveaxis(g, 0, -2)
    return g.reshape(g.shape[:-2] + (g.shape[-2] * g.shape[-1],))


def _my_shard(full, me, ws):
    return lax.dynamic_slice_in_dim(full, me * ws, ws, axis=full.ndim - 1)


def _silu_vjp(cvec, dsc):
    def body(c_ref, d_ref, o_ref):
        _, vjp = jax.vjp(jax.nn.silu, c_ref[...])
        o_ref[...] = vjp(d_ref[...])[0]

    return pl.pallas_call(body, name="c_ctx_bwd", out_shape=jax.ShapeDtypeStruct(cvec.shape, F32), compiler_params=_cp())(cvec, dsc)


def kernel(x, c, ctx, c_ctx, mod_w, mod_b, norm_g, ffn_w_in, ffn_w_out, ab_w_in, ab_gate_w, ab_gate_b, ab_gla_norm_g, ab_vnorm_g, ab_spatial_w, ab_spatial_b, ab_w_out, ssd_w_in, ssd_conv_w, ssd_conv_b, ssd_dt_bias, ssd_a_log, ssd_d, ssd_norm_g, ssd_w_out, final_norm_g, loss_target, m_c_ctx, m_mod_w, m_mod_b, m_norm_g, m_ffn_w_in, m_ffn_w_out, m_ab_w_in, m_ab_gate_w, m_ab_gate_b, m_ab_gla_norm_g, m_ab_vnorm_g, m_ab_spatial_w, m_ab_spatial_b, m_ab_w_out, m_ssd_w_in, m_ssd_conv_w, m_ssd_conv_b, m_ssd_dt_bias, m_ssd_a_log, m_ssd_d, m_ssd_norm_g, m_ssd_w_out, m_final_norm_g, v_c_ctx, v_mod_w, v_mod_b, v_norm_g, v_ffn_w_in, v_ffn_w_out, v_ab_w_in, v_ab_gate_w, v_ab_gate_b, v_ab_gla_norm_g, v_ab_vnorm_g, v_ab_spatial_w, v_ab_spatial_b, v_ab_w_out, v_ssd_w_in, v_ssd_conv_w, v_ssd_conv_b, v_ssd_dt_bias, v_ssd_a_log, v_ssd_d, v_ssd_norm_g, v_ssd_w_out, v_final_norm_g):
    a = dict(locals())
    me = _slot(*_mesh_pos())
    ws_mod = mod_w.shape[-1]

    fwd_small = [c] + [a[k] for k in SMALL_SHARDED]
    g_small = _ag_small("gather_small", _pack(fwd_small))
    parts = _unpack_flat(g_small, [t.shape for t in fwd_small])
    c_rows = parts[0].reshape(NDEV, D)
    full = {k: _unshard(p) for k, p in zip(SMALL_SHARDED, parts[1:])}
    c_all = jnp.concatenate([c_rows, c_ctx[None], jnp.zeros((7, D), F32)], axis=0)
    m_all = _ag_small("gather_mod", _mod_fwd(c_all, mod_w).reshape(2 * 16, ws_mod)).reshape(NDEV, 2, 16, ws_mod)
    m_mine = lax.dynamic_index_in_dim(m_all, me, axis=2, keepdims=False)
    mx = jnp.moveaxis(m_mine, 0, 1).reshape(2, N_MOD, D) + mod_b.reshape(2, N_MOD, D)
    mc = jnp.moveaxis(m_all[:, :, 8, :], 0, 1).reshape(2, N_MOD, D) + mod_b.reshape(2, N_MOD, D)
    pad2 = jnp.zeros((2, 2, D), F32)
    mod = jnp.concatenate([mx, pad2, mc, pad2], axis=1)

    big = {"l0_mixer": (ab_w_in[0], ab_w_out[0]), "l0_ffn": (ffn_w_in[0], ffn_w_out[0]),
           "l1_mixer": (ssd_w_in[0], ssd_w_out[0]), "l1_ffn": (ffn_w_in[1], ffn_w_out[1])}
    comm = _Exchange({k: tuple(w.astype(MXU_DTYPE) for w in v) for k, v in big.items()}, me)
    P = {
        "norm_g": full["norm_g"], "ab_gate_w": full["ab_gate_w"][0], "ab_gate_b": full["ab_gate_b"][0],
        "ab_gla_norm_g": ab_gla_norm_g, "ab_vnorm_g": ab_vnorm_g, "ab_spatial_w": ab_spatial_w[0], "ab_spatial_b": ab_spatial_b[0],
        "ssd_conv_w": full["ssd_conv_w"][0], "ssd_conv_b": full["ssd_conv_b"], "ssd_dt_bias": ssd_dt_bias[0],
        "ssd_a_log": ssd_a_log[0], "ssd_d": ssd_d, "ssd_norm_g": full["ssd_norm_g"], "final_norm_g": final_norm_g[None],
    }

    loss, grad_x, dmod, dP = _local_step(x[0], ctx[0], loss_target[0], mod, P, comm)

    dmx, dmc = dmod[:, 0:N_MOD].reshape(2, N_MOD * D), dmod[:, 8:8 + N_MOD].reshape(2, N_MOD * D)
    small_names = ("ab_gate_w", "ab_gate_b", "ab_gla_norm_g", "ab_vnorm_g", "ab_spatial_w", "ab_spatial_b", "norm_g", "ssd_conv_w",
                   "ssd_conv_b", "ssd_dt_bias", "ssd_a_log", "ssd_d", "ssd_norm_g", "final_norm_g")
    bwd_small = [dP[k] for k in small_names] + [dmc, dmx]
    shapes = [t.shape for t in bwd_small]
    g_bwd = _ag_small("gather_small_grads", _pack(bwd_small))
    summed = _unpack_flat(_sum_parts("sum_small_grads", g_bwd), shapes)
    gfull = dict(zip(small_names, summed[:-2]))
    dmc_sum, dmx_sum = summed[-2], summed[-1]
    dmx_all = _unpack_flat(g_bwd, shapes)[-1]
    dmx_sh = jnp.moveaxis(_my_shard(dmx_all, me, ws_mod), 0, 1)
    dm = jnp.concatenate([dmx_sh, _my_shard(dmc_sum, me, ws_mod)[:, None, :], jnp.zeros((2, 7, ws_mod), F32)], axis=1)
    d_mod_w, dsc = _mod_bwd(c_all, mod_w, dm)
    dsc_ctx = (dsc[0, 8] + dsc[1, 8])[None]
    dsc_all = _ag_small("gather_c_ctx_grad", jnp.concatenate([dsc_ctx, jnp.zeros((7, D), F32)], axis=0))
    d_c_ctx = _silu_vjp(c_ctx[None], _sum_parts("sum_c_ctx_grad", dsc_all)[0:1])[0]

    g_small_w = {
        "c_ctx": d_c_ctx, "mod_b": dmx_sum + dmc_sum, "norm_g": gfull["norm_g"], "ab_gate_w": gfull["ab_gate_w"][None],
        "ab_gate_b": gfull["ab_gate_b"][None], "ab_gla_norm_g": gfull["ab_gla_norm_g"], "ab_vnorm_g": gfull["ab_vnorm_g"],
        "ab_spatial_w": gfull["ab_spatial_w"][None], "ab_spatial_b": gfull["ab_spatial_b"][None], "ssd_conv_w": gfull["ssd_conv_w"][None],
        "ssd_conv_b": gfull["ssd_conv_b"], "ssd_dt_bias": gfull["ssd_dt_bias"][None], "ssd_a_log": gfull["ssd_a_log"][None],
        "ssd_d": gfull["ssd_d"], "ssd_norm_g": gfull["ssd_norm_g"], "final_norm_g": gfull["final_norm_g"][0],
    }
    for k in SMALL_SHARDED:
        g_small_w[k] = _my_shard(g_small_w[k], me, a[k].shape[-1])
    token = comm.start_last([d_c_ctx])
    res = _adam("adam_small", _pack([a[k] for k in SMALL]), [_pack([g_small_w[k] for k in SMALL])[None]],
                _pack([a["m_" + k] for k in SMALL]), _pack([a["v_" + k] for k in SMALL]), token)
    out = {k: vals for k, vals in zip(SMALL, zip(*[_unpack_flat(r, [a[k].shape for k in SMALL]) for r in res]))}

    def adam_big(name, w2d, parts, m2d, v2d, shape):
        return tuple(r.reshape(shape) for r in _adam(name, w2d, parts, m2d, v2d, token))

    def flat2(t):
        return t.reshape(-1, t.shape[-1])

    out["mod_w"] = adam_big("adam_mod_w", flat2(mod_w), [d_mod_w.reshape(1, -1, ws_mod)], flat2(m_mod_w), flat2(v_mod_w), mod_w.shape)

    for j, k in enumerate(("ffn_w_in", "ffn_w_out")):
        out[k] = adam_big("adam_" + k, flat2(a[k]), [comm.recv["l0_ffn"][j], comm.recv["l1_ffn"][j]], flat2(a["m_" + k]),
                          flat2(a["v_" + k]), a[k].shape)
    for j, k in enumerate(("ssd_w_in", "ssd_w_out")):
        out[k] = adam_big("adam_" + k, a[k][0], [comm.recv["l1_mixer"][j]], a["m_" + k][0], a["v_" + k][0], a[k].shape)
    recv_ab = comm.finish(out["ssd_w_out"][3])
    for j, k in enumerate(("ab_w_in", "ab_w_out")):
        out[k] = adam_big("adam_" + k, a[k][0], [recv_ab[j]], a["m_" + k][0], a["v_" + k][0], a[k].shape)

    loss_all = lax.psum(loss[0, 0], ("x", "y", "c"))
    return (loss_all, grad_x[None], *[out[k][0] for k in WEIGHTS], *[out[k][1] for k in WEIGHTS],
            *[out[k][2] for k in WEIGHTS], *[out[k][3] for k in WEIGHTS])
```

```python
import functools
import math

import jax
import jax.numpy as jnp
from jax import lax
from jax.experimental import pallas as pl
from jax.experimental.pallas import tpu as pltpu

F32 = jnp.float32
BF16 = jnp.bfloat16
MXU_DTYPE = jnp.bfloat16
HI = lax.Precision.HIGHEST

D = 1024
NDEV = 8
N_MOD = 6
EPS = 1e-6
GRID_W = 64
CTX = 256
TM = 256
D_FF = 2816
GLA_H, GLA_DK, GLA_DV, GLA_LR, GLA_TAU, GLA_L = 4, 64, 128, 16, 16.0, 64
GMLP_G, GMLP_C, GMLP_L = 4, 128, 128
SSD_H, SSD_P, SSD_G, SSD_N, SSD_L, SSD_K = 32, 64, 4, 128, 128, 5
SSD_INNER = SSD_H * SSD_P
AB_IN = 2592
SSD_IN = 5184
AB_SEGS = ((256, 768), (1056, 1568), (1568, 2080), (2080, 2592), (0, 256), (800, 1056), (768, 800))
AB_P = 2816
SSD_SEGS = ((0, 2048), (3136, 5184), (2048, 2560), (2560, 3072), (3072, 3136))
SSD_P_W = 5376
VMEM_LIMIT = 56 * 1024 * 1024

ADAM_LR, ADAM_B1, ADAM_B2, ADAM_EPS, ADAM_WD, ADAM_STEP = 0.001, 0.9, 0.999, 1e-08, 0.01, 10


def _cp(sem=None, **kw):
    return pltpu.CompilerParams(dimension_semantics=sem, vmem_limit_bytes=VMEM_LIMIT, **kw)


def _dot(a, b, dims=(((1,), (0,)), ((), ()))):
    return lax.dot_general(a.astype(MXU_DTYPE), b.astype(MXU_DTYPE), dims, preferred_element_type=F32)


def _dot_nt(a, b):
    return _dot(a, b, (((1,), (1,)), ((), ())))


def _dot_tn(a, b):
    return _dot(a, b, (((0,), (0,)), ((), ())))


def _dotx(a, b, dims=(((1,), (0,)), ((), ()))):
    return lax.dot_general(a, b, dims, precision=HI, preferred_element_type=F32)


def _rms(x):
    return x * lax.rsqrt(jnp.mean(x * x, axis=-1, keepdims=True) + EPS)


def _pick(n, prefs):
    for p in prefs:
        if n % p == 0:
            return p
    return n


def _row(arr, width=None, colblk=0, tm=TM, valid=None):
    width = arr.shape[1] if width is None else width
    if valid is None:
        return ([arr], [pl.BlockSpec((tm, width), lambda i, c=colblk: (i, c))], lambda r: r[...].astype(F32), width)
    spec = pl.BlockSpec((tm, width), lambda i, c=colblk: (jnp.minimum(i, valid - 1), c))
    return ([arr], [spec], lambda r: jnp.where(pl.program_id(0) < valid, r[...].astype(F32), 0.0), width)


def _row_grid(arr, a, nb):
    n = arr.shape[0]
    b = (n - CTX) // a

    def load(v_ref, c_ref):
        i = pl.program_id(0)
        return jnp.where(i == nb - 1, c_ref[...], _grid_rows(v_ref, a, i))

    return ([arr.reshape(n // b, b, D), arr], [_grid_spec(a, nb), pl.BlockSpec((TM, D), lambda i: (nb - 1, 0))], load, D)


def _row_cat(x, ctx, nb):
    return ([x, ctx], [pl.BlockSpec((TM, D), lambda i: (jnp.minimum(i, nb - 2), 0)), pl.BlockSpec((TM, D), lambda i: (0, 0))],
            lambda x_ref, c_ref: jnp.where(pl.program_id(0) == nb - 1, c_ref[...], x_ref[...]), D)


def _operands(rows):
    return [a for r in rows for a in r[0]], [s for r in rows for s in r[1]]


def _load_rows(refs, rows):
    vals, k = [], 0
    for r in rows:
        vals.append(r[2](*refs[k:k + len(r[0])]))
        k += len(r[0])
    return vals


def _full_spec(p):
    nd = p.ndim
    return pl.BlockSpec(p.shape, lambda i, nd=nd: (0,) * nd)


def _rowwise(name, fn, n_blocks, ctx_blk, rows, params, outs, tm=TM):
    arrs, specs = _operands(rows)
    nr, npar = len(arrs), len(params)

    def body(*refs):
        t = (pl.program_id(0) >= ctx_blk).astype(F32)
        rv = _load_rows(refs[:nr], rows)
        pv = [p[...] for p in refs[nr:nr + npar]]
        res = fn(t, rv, pv)
        for o_ref, o in zip(refs[nr + npar:], res):
            o_ref[...] = o.astype(o_ref.dtype)

    return pl.pallas_call(
        body, name=name, grid=(n_blocks,),
        in_specs=specs + [_full_spec(p) for p in params],
        out_specs=[pl.BlockSpec((tm, w), lambda i: (i, 0)) for w, _ in outs],
        out_shape=[jax.ShapeDtypeStruct((n_blocks * tm, w), dt) for w, dt in outs],
        compiler_params=_cp(("parallel",)),
    )(*arrs, *params)


def _rowwise_vjp(name, fn, n_blocks, ctx_blk, rows, params, douts, row_grads, tm=TM, x_rows_only=False):
    out_blocks = n_blocks - 1 if x_rows_only else n_blocks
    adds = [a for _, _, a in row_grads if a is not None]
    (r_arrs, r_specs), (d_arrs, d_specs), (a_arrs, a_specs) = _operands(rows), _operands(douts), _operands(adds)
    nr, npar, nd, na = len(r_arrs), len(params), len(d_arrs), len(a_arrs)

    def body(*refs):
        i = pl.program_id(0)
        t = (i >= ctx_blk).astype(F32)
        rv = _load_rows(refs[:nr], rows)
        pv = [p[...] for p in refs[nr:nr + npar]]
        dv = _load_rows(refs[nr + npar:nr + npar + nd], douts)
        av = _load_rows(refs[nr + npar + nd:nr + npar + nd + na], adds)
        o_refs = refs[nr + npar + nd + na:]
        _, vjp = jax.vjp(lambda r, p: tuple(fn(t, r, p)), rv, pv)
        d_rows, d_params = vjp(tuple(dv))
        ai, grads = 0, []
        for ri, _, addend in row_grads:
            g = d_rows[ri]
            if addend is not None:
                g = g + av[ai]
                ai += 1
            grads.append(g)

        @pl.when(i < out_blocks)
        def _():
            for o_ref, g in zip(o_refs, grads):
                o_ref[...] = g.astype(o_ref.dtype)

        p_refs = o_refs[len(row_grads):]

        @pl.when(i == 0)
        def _():
            for p_ref in p_refs:
                p_ref[...] = jnp.zeros_like(p_ref)

        for p_ref, g in zip(p_refs, d_params):
            p_ref[...] += g

    widths = [rows[ri][3] for ri, _, _ in row_grads]
    res = pl.pallas_call(
        body, name=name, grid=(n_blocks,),
        in_specs=r_specs + [_full_spec(p) for p in params] + d_specs + a_specs,
        out_specs=[pl.BlockSpec((tm, w), lambda i: (jnp.minimum(i, out_blocks - 1), 0)) for w in widths] + [_full_spec(p) for p in params],
        out_shape=[jax.ShapeDtypeStruct((out_blocks * tm, w), dt) for w, (_, dt, _) in zip(widths, row_grads)]
        + [jax.ShapeDtypeStruct(p.shape, F32) for p in params],
        compiler_params=_cp(("arbitrary",)),
    )(*r_arrs, *params, *d_arrs, *a_arrs)
    return res[:len(row_grads)], res[len(row_grads):]


def _mm(name, a, b, mode, out_dtype):
    if mode == "nn":
        m, kk = a.shape
        n = b.shape[1]
    elif mode == "nt":
        m, kk = a.shape
        n = b.shape[0]
    else:
        kk, m = a.shape
        n = b.shape[1]
    if mode == "tn":
        tm = _pick(m, (1024, 1408, 512, 256, 128))
        tn = _pick(n, (768, 512, 256, 128))
        tk = kk
    else:
        tm = _pick(m, (1088, 1024, 768, 512, 384, 256, 128))
        tn = n if n <= 2816 else _pick(n, (1024, 768, 512, 256, 128))
        tk = kk if kk <= 2816 else _pick(kk, (2816, 1792, 1024, 768, 512, 256, 128))
    nk = kk // tk
    in_place = out_dtype == F32
    if mode == "nn":
        specs = [pl.BlockSpec((tm, tk), lambda i, j, k: (i, k)), pl.BlockSpec((tk, tn), lambda i, j, k: (k, j))]
        dims = (((1,), (0,)), ((), ()))
    elif mode == "nt":
        specs = [pl.BlockSpec((tm, tk), lambda i, j, k: (i, k)), pl.BlockSpec((tn, tk), lambda i, j, k: (j, k))]
        dims = (((1,), (1,)), ((), ()))
    else:
        specs = [pl.BlockSpec((tk, tm), lambda i, j, k: (k, i)), pl.BlockSpec((tk, tn), lambda i, j, k: (k, j))]
        dims = (((0,), (0,)), ((), ()))

    def body(a_ref, b_ref, o_ref, *scratch):
        part = lax.dot_general(a_ref[...].astype(MXU_DTYPE), b_ref[...].astype(MXU_DTYPE), dims, preferred_element_type=F32)
        if nk == 1:
            o_ref[...] = part.astype(o_ref.dtype)
        else:
            k = pl.program_id(2)
            acc = o_ref if in_place else scratch[0]

            @pl.when(k == 0)
            def _():
                acc[...] = part

            @pl.when(k > 0)
            def _():
                acc[...] += part

            if not in_place:
                @pl.when(k == nk - 1)
                def _():
                    o_ref[...] = acc[...].astype(o_ref.dtype)

    return pl.pallas_call(
        body, name=name, grid=(m // tm, n // tn, nk), in_specs=specs,
        out_specs=pl.BlockSpec((tm, tn), lambda i, j, k: (i, j)),
        out_shape=jax.ShapeDtypeStruct((m, n), out_dtype),
        scratch_shapes=[] if nk == 1 or in_place else [pltpu.VMEM((tm, tn), F32)],
        compiler_params=_cp(("parallel", "parallel", "arbitrary")),
    )(a, b)


def _sel_mod(modp, t):
    return modp[0:8] * (1.0 - t) + modp[8:16] * t


def _fn_prenorm(t, rows, params, *, a, b):
    (x,), (g, modp) = rows, params
    m = _sel_mod(modp, t)
    return ((_rms(x) * g) * (1.0 + m[b:b + 1]) + m[a:a + 1],)


def _fn_resid_prenorm(t, rows, params, *, gi, a, b):
    (x, y), (g, mod_a, mod_b) = rows, params
    ma, mb = _sel_mod(mod_a, t), _sel_mod(mod_b, t)
    xn = x + ma[gi:gi + 1] * y
    return xn, (_rms(xn) * g) * (1.0 + mb[b:b + 1]) + mb[a:a + 1]


def _fn_resid(t, rows, params, *, gi):
    (x, y), (mod_a,) = rows, params
    return (x + _sel_mod(mod_a, t)[gi:gi + 1] * y,)


def _fn_swiglu(t, rows, params):
    (pf,) = rows
    return (jax.nn.silu(pf[:, :D_FF]) * pf[:, D_FF:],)


def _fn_mixpost(t, rows, params):
    (o, r, u, g), (gla_g, vn_g, sw, sb_t) = rows, params
    a =jnp.concatenate([_rms(o[:, h * GLA_DV:(h + 1) * GLA_DV]) for h in range(GLA_H)], axis=1) * gla_g * jax.nn.silu(r)
    uu, vv = jax.nn.gelu(u), jax.nn.gelu(g)
    mu = jnp.mean(vv, axis=-1, keepdims=True)
    var = jnp.mean(jnp.square(vv - mu), axis=-1, keepdims=True)
    vn = ((vv - mu) * lax.rsqrt(var + EPS)) * vn_g
    s = jnp.concatenate(
        [_dot(sw[gi * GMLP_L:(gi + 1) * GMLP_L, :], vn[:, gi * GMLP_C:(gi + 1) * GMLP_C]) + sb_t[:, gi:gi + 1]
         for gi in range(GMLP_G)], axis=1)
    return (jnp.concatenate([a, uu * s], axis=1),)


def _head_expand():
    r = lax.broadcasted_iota(jnp.int32, (SSD_H, SSD_INNER), 0)
    c = lax.broadcasted_iota(jnp.int32, (SSD_H, SSD_INNER), 1)
    return (c // SSD_P == r).astype(F32)


def _fn_ssd_finish(t, rows, params):
    (y2, xs, z), (d_skip, norm_g) = rows, params
    d_full = _dotx(jnp.broadcast_to(d_skip, (8, SSD_H)), _head_expand())[0:1]
    y = (y2 + d_full * xs) * jax.nn.silu(z)
    gw = SSD_INNER // SSD_G
    return (jnp.concatenate([_rms(y[:, gi * gw:(gi + 1) * gw]) for gi in range(SSD_G)], axis=1) * norm_g,)


def _fn_concat(t, rows, params, *, sums, pad=0):
    out, i = [], 0
    for n in sums:
        acc = rows[i]
        for j in range(1, n):
            acc = acc + rows[i + j]
        out.append(acc)
        i += n
    if pad:
        out.append(jnp.zeros((out[0].shape[0], pad), F32))
    return (jnp.concatenate(out, axis=1),)


def _tri(n, rev):
    r = lax.broadcasted_iota(jnp.int32, (n, n), 0)
    c = lax.broadcasted_iota(jnp.int32, (n, n), 1)
    return (r <= c) if rev else (r >= c)


def _running_sum(x, rev):
    n, s = x.shape[0], 1
    while s < n:
        z = jnp.zeros((s, x.shape[1]), x.dtype)
        x = x + (jnp.concatenate([x[s:], z], axis=0) if rev else jnp.concatenate([z, x[:n - s]], axis=0))
        s *= 2
    return x


def _gla_chunk(S, v, k, q, tail, gw, gb, *, rev):
    L, H = GLA_L, GLA_H
    lr = tail[:, GLA_LR:2 * GLA_LR] if rev else tail[:, 0:GLA_LR]
    la = jax.nn.log_sigmoid(_dot(lr, gw) + gb) / GLA_TAU
    b = _running_sum(la, rev)
    b_last = b[0:1] if rev else b[L - 1:L]
    kd = k * jnp.exp(b_last - b)
    qd = (q * GLA_DK ** -0.5) * jnp.exp(b)
    ki = k * jnp.exp(-b)

    def same_head(shape, rows_per_head, cols_per_head):
        r = lax.broadcasted_iota(jnp.int32, shape, 0) // rows_per_head
        c = lax.broadcasted_iota(jnp.int32, shape, 1) // cols_per_head
        return r == c

    k_blk = jnp.where(same_head((H * L, H * GLA_DK), L, GLA_DK), jnp.concatenate([ki] * H, axis=0), 0.0)
    v_blk = jnp.where(same_head((H * L, H * GLA_DV), L, GLA_DV), jnp.concatenate([v] * H, axis=0), 0.0)
    row = lax.broadcasted_iota(jnp.int32, (L, H * L), 0)
    src = lax.broadcasted_iota(jnp.int32, (L, H * L), 1) % L
    sc = jnp.where((row <= src) if rev else (row >= src), _dot_nt(qd, k_blk), 0.0)
    o = _dot_nt(qd, S) + _dot(sc, v_blk)
    s_new = S * jnp.exp(b_last) + jnp.where(same_head(S.shape, GLA_DV, GLA_DK), _dot_tn(v, kd), 0.0)
    return s_new, o


def _ssd_chunk(S, x, bm, cm, tail, dtb, alog, *, rev):
    L = SSD_L
    msk = _tri(L, rev)
    raw = tail[:, SSD_H:2 * SSD_H] if rev else tail[:, 0:SSD_H]
    dt = jax.nn.softplus(raw + dtb)
    acum = _running_sum(dt * (-jnp.exp(alog)), rev)
    a_last = acum[0:1] if rev else acum[L - 1:L]
    wst = dt * jnp.exp(a_last - acum)
    eac = jnp.exp(acum)
    dec = jnp.exp(a_last)
    tr = jnp.concatenate([acum, dt, wst, jnp.zeros((L, L - 3 * SSD_H), F32)], axis=1).T
    acum_t, dt_t, wst_t = tr[0:SSD_H], tr[SSD_H:2 * SSD_H], tr[2 * SSD_H:3 * SSD_H]
    lane = lax.broadcasted_iota(jnp.int32, (1, 2 * SSD_P), 1)
    m0 = (lane < SSD_P).astype(F32)
    m1 = 1.0 - m0
    pairs_per_group = SSD_H // SSD_G // 2
    y_parts, s_parts = [], []
    for g in range(SSD_G):
        ns = slice(g * SSD_N, (g + 1) * SSD_N)
        bg, cg = bm[:, ns], cm[:, ns]
        cb = _dot_nt(cg, bg)
        bgt = bg.T
        gs = slice(g * pairs_per_group * 2 * SSD_P, (g + 1) * pairs_per_group * 2 * SSD_P)
        y_carry = _dot(cg, S[:, gs])
        for jj in range(pairs_per_group):
            j = g * pairs_per_group + jj
            ls = slice(j * 2 * SSD_P, (j + 1) * 2 * SSD_P)
            xp, sp = x[:, ls], S[:, ls]
            xm = jnp.concatenate([xp * m0, xp * m1], axis=0)
            lhs, bw = [], []
            for h in (2 * j, 2 * j + 1):
                seg = acum[:, h:h + 1] - acum_t[h:h + 1, :]
                lhs.append(cb * jnp.exp(jnp.where(msk, seg, -jnp.inf)) * dt_t[h:h + 1, :])
                bw.append(bgt * wst_t[h:h + 1, :])
            e_pair = eac[:, 2 * j:2 * j + 1] * m0 + eac[:, 2 * j + 1:2 * j + 2] * m1
            y_parts.append(_dot(jnp.concatenate(lhs, axis=1), xm) + y_carry[:, jj * 2 * SSD_P:(jj + 1) * 2 * SSD_P] * e_pair)
            d_pair = dec[:, 2 * j:2 * j + 1] * m0 + dec[:, 2 * j + 1:2 * j + 2] * m1
            s_parts.append(sp * d_pair + _dot(jnp.concatenate(bw, axis=1), xm))
    return jnp.concatenate(s_parts, axis=1), jnp.concatenate(y_parts, axis=1)


def _multi_chunk(chunk_fn, L, subs, nr):
    def fn(S, *args, rev):
        rows, params = args[:nr], args[nr:]
        ys = [None] * subs
        for j in (range(subs - 1, -1, -1) if rev else range(subs)):
            S, ys[j] = chunk_fn(S, *[r[j * L:(j + 1) * L] for r in rows], *params, rev=rev)
        return S, jnp.concatenate(ys, axis=0)

    return fn


def _scan_order(n, nx, rev, backward):
    nc = n - nx

    def fwd(s):
        return (n - 1 - s) if rev else jnp.where(s < nc, s + nx, s - nc)

    return (lambda s: fwd(n - 1 - s)) if backward else fwd


def _scan_fwd(name, chunk_fn, L, n, nx, rev, rows, params, state_shape, out_w, addend=None):
    order = _scan_order(n, nx, rev, False)
    nr, npar = len(rows), len(params)
    adds = [] if addend is None else [addend]

    def body(*refs):
        s_scr = refs[-1]

        @pl.when(pl.program_id(0) == 0)
        def _():
            s_scr[...] = jnp.zeros_like(s_scr)

        s_in = s_scr[...]
        y_ref, st_ref = refs[nr + npar + len(adds)], refs[nr + npar + len(adds) + 1]
        st_ref[0] = s_in
        s_new, y = chunk_fn(s_in, *[r[...] for r in refs[:nr]], *[p[...] for p in refs[nr:nr + npar]], rev=rev)
        y_ref[...] = y + refs[nr + npar][...] if adds else y
        s_scr[...] = s_new

    return pl.pallas_call(
        body, name=name, grid=(n,),
        in_specs=[pl.BlockSpec((L, w), lambda s, c=c: (order(s), c)) for _, w, c in rows] + [_full_spec(p) for p in params]
        + [pl.BlockSpec((L, out_w), lambda s: (order(s), 0)) for _ in adds],
        out_specs=[pl.BlockSpec((L, out_w), lambda s: (order(s), 0)),
                   pl.BlockSpec((1,) + state_shape, lambda s: (order(s), 0, 0))],
        out_shape=[jax.ShapeDtypeStruct((n * L, out_w), F32), jax.ShapeDtypeStruct((n,) + state_shape, F32)],
        scratch_shapes=[pltpu.VMEM(state_shape, F32)],
        compiler_params=_cp(("arbitrary",)),
    )(*[a for a, _, _ in rows], *params, *adds)


def _scan_bwd(name, chunk_fn, L, n, nx, rev, rows, params, states, dy, state_shape, out_w, addends=None):
    order = _scan_order(n, nx, rev, True)
    dy_blocks = dy.shape[0] // L
    nr, npar = len(rows), len(params)
    adds = [] if addends is None else list(addends)

    def body(*refs):
        i = pl.program_id(0)
        ds_scr = refs[-1]
        rv = [r[...] for r in refs[:nr]]
        pv = [p[...] for p in refs[nr:nr + npar]]
        st_ref, dy_ref = refs[nr + npar], refs[nr + npar + 1]
        a_refs = refs[nr + npar + 2:nr + npar + 2 + len(adds)]
        o_refs = refs[nr + npar + 2 + len(adds):-1]
        p_refs = o_refs[nr:]

        @pl.when(i == 0)
        def _():
            ds_scr[...] = jnp.zeros_like(ds_scr)
            for p_ref in p_refs:
                p_ref[...] = jnp.zeros_like(p_ref)

        _, vjp = jax.vjp(functools.partial(chunk_fn, rev=rev), st_ref[0], *rv, *pv)
        dy_blk = jnp.where(order(i) < dy_blocks, dy_ref[...].astype(F32), 0.0)
        grads = vjp((ds_scr[...], dy_blk))
        ds_scr[...] = grads[0]
        for j, (o_ref, g) in enumerate(zip(o_refs[:nr], grads[1:1 + nr])):
            o_ref[...] = g + a_refs[j][...] if adds else g
        for p_ref, g in zip(p_refs, grads[1 + nr:]):
            p_ref[...] += g

    row_specs = [pl.BlockSpec((L, w), lambda s: (order(s), 0)) for _, w, _ in rows]
    res = pl.pallas_call(
        body, name=name, grid=(n,),
        in_specs=[pl.BlockSpec((L, w), lambda s, c=c: (order(s), c)) for _, w, c in rows] + [_full_spec(p) for p in params]
        + [pl.BlockSpec((1,) + state_shape, lambda s: (order(s), 0, 0)),
           pl.BlockSpec((L, out_w), lambda s: (jnp.minimum(order(s), dy_blocks - 1), 0))]
        + row_specs[:len(adds)],
        out_specs=row_specs + [_full_spec(p) for p in params],
        out_shape=[jax.ShapeDtypeStruct((n * L, w), F32) for _, w, _ in rows] + [jax.ShapeDtypeStruct(p.shape, F32) for p in params],
        scratch_shapes=[pltpu.VMEM(state_shape, F32)],
        compiler_params=_cp(("arbitrary",)),
    )(*[a for a, _, _ in rows], *params, states, dy, *adds)
    return res[:nr], res[nr:]


CONV_W = 1024
CONV_COLBLK = (0, 1, 4)


def _conv_specs(nb, src_blk):
    halo = TM // 8
    return [pl.BlockSpec((TM, CONV_W), lambda j, i: (i, src_blk(j))),
            pl.BlockSpec((8, CONV_W), lambda j, i: (jnp.maximum(i * halo - 1, 0), src_blk(j))),
            pl.BlockSpec((8, CONV_W), lambda j, i: (jnp.minimum(i * halo + halo, nb * halo - 1), src_blk(j)))]


def _conv_ext(i, nb, cur, prev, nxt):
    has_prev = jnp.logical_and(i > 0, i < nb - 1)
    has_next = i < nb - 2
    return jnp.concatenate([jnp.where(has_prev, prev, 0.0), cur, jnp.where(has_next, nxt, 0.0)], axis=0)


def _conv_taps(ext, w, flip):
    acc = None
    for j in range(SSD_K):
        wj = w[SSD_K - 1 - j:SSD_K - j, :] if flip else w[j:j + 1, :]
        term = wj * ext[6 + j:6 + j + TM, :]
        acc = term if acc is None else acc + term
    return acc


def _conv(name, src, w8, b1, nb, *, permuted_src, act, flip, out_dtype):
    src_blk = (lambda j: jnp.where(j == 2, CONV_COLBLK[2], j)) if permuted_src else (lambda j: j)

    def body(cur, prev, nxt, w_ref, b_ref, o_ref):
        ext = _conv_ext(pl.program_id(1), nb, cur[...].astype(F32), prev[...].astype(F32), nxt[...].astype(F32))
        acc = _conv_taps(ext, w_ref[...], flip)
        if act:
            acc = jax.nn.silu(acc + b_ref[...])
        o_ref[...] = acc.astype(o_ref.dtype)

    return pl.pallas_call(
        body, name=name, grid=(3, nb),
        in_specs=_conv_specs(nb, src_blk) + [pl.BlockSpec((8, CONV_W), lambda j, i: (0, j)), pl.BlockSpec((1, CONV_W), lambda j, i: (0, j))],
        out_specs=pl.BlockSpec((TM, CONV_W), lambda j, i: (i, j)),
        out_shape=jax.ShapeDtypeStruct((nb * TM, 3 * CONV_W), out_dtype),
        compiler_params=_cp(("parallel", "parallel")),
    )(src, src, src, w8, b1)


def _conv_bwd_pre(name, p1, w8, b1, dxbc_parts, nb):
    src_blk = lambda j: jnp.where(j == 2, CONV_COLBLK[2], j)
    xs_parts, bc_parts = dxbc_parts
    n_x, n_bc = len(xs_parts), len(bc_parts)
    x_blocks = [p.shape[0] // TM for p in xs_parts]

    def body(*refs):
        cur, prev, nxt, w_ref, b_ref = refs[:5]
        d_refs = refs[5:5 + n_x + n_bc]
        da_ref, dw_ref, db_ref = refs[5 + n_x + n_bc:]
        j, i = pl.program_id(0), pl.program_id(1)
        ext = _conv_ext(i, nb, cur[...], prev[...], nxt[...])
        acc = _conv_taps(ext, w_ref[...], False) + b_ref[...]
        dx = d_refs[0][...]
        for r, blocks in zip(d_refs[1:n_x], x_blocks[1:]):
            dx = dx + jnp.where(i < blocks, r[...], 0.0)
        dbc = jnp.concatenate([d_refs[n_x][...], d_refs[n_x + 1][...]], axis=1)
        dy = jnp.where(j == 2, dbc, dx)
        sg = jax.nn.sigmoid(acc)
        da = dy * (sg + acc * sg * (1.0 - sg))
        da_ref[...] = da

        @pl.when(i == 0)
        def _():
            dw_ref[...] = jnp.zeros_like(dw_ref)
            db_ref[...] = jnp.zeros_like(db_ref)

        rows = [jnp.sum(da * ext[6 + t:6 + t + TM, :], axis=0, keepdims=True) for t in range(SSD_K)]
        dw_ref[...] += jnp.concatenate(rows + [jnp.zeros((8 - SSD_K, CONV_W), F32)], axis=0)
        db_ref[...] += jnp.sum(da, axis=0, keepdims=True)

    x_specs = [pl.BlockSpec((TM, CONV_W), lambda j, i, b=b: (jnp.minimum(i, b - 1), jnp.minimum(j, 1))) for b in x_blocks]
    bc_specs = [pl.BlockSpec((TM, 512), lambda j, i: (i, 0)) for _ in bc_parts]
    return pl.pallas_call(
        body, name=name, grid=(3, nb),
        in_specs=_conv_specs(nb, src_blk) + [pl.BlockSpec((8, CONV_W), lambda j, i: (0, j)), pl.BlockSpec((1, CONV_W), lambda j, i: (0, j))]
        + x_specs + bc_specs,
        out_specs=[pl.BlockSpec((TM, CONV_W), lambda j, i: (i, j)), pl.BlockSpec((8, CONV_W), lambda j, i: (0, j)),
                   pl.BlockSpec((1, CONV_W), lambda j, i: (0, j))],
        out_shape=[jax.ShapeDtypeStruct((nb * TM, 3 * CONV_W), F32), jax.ShapeDtypeStruct((8, 3 * CONV_W), F32),
                   jax.ShapeDtypeStruct((1, 3 * CONV_W), F32)],
        compiler_params=_cp(("arbitrary", "arbitrary")),
    )(p1, p1, p1, w8, b1, *xs_parts, *bc_parts)


def _grid_block(a):
    nbv = TM // a
    blk_b = max(nbv, 8)
    return nbv, blk_b, blk_b // nbv


def _grid_spec(a, nb):
    _, blk_b, per = _grid_block(a)
    return pl.BlockSpec((a, blk_b, D), lambda i: (0, jnp.minimum(i, nb - 2) // per, 0))


def _grid_rows(v_ref, a, i):
    nbv, _, per = _grid_block(a)

    def pick(ph):
        return jnp.concatenate([v_ref[:, ph * nbv + t, :] for t in range(nbv)], axis=0)

    out = pick(0)
    for ph in range(1, per):
        out = jnp.where(i % per == ph, pick(ph), out)
    return out


def _loss_head(x, f, target, modp, g_final, rows_r):
    tview = target.reshape(rows_r, target.shape[0] // rows_r, D)
    nb = x.shape[0] // TM + 1

    def fn(x_, f_, tgt, modp_, g_):
        xn = x_ + _sel_mod(modp_, 0.0)[5:6] * f_
        err = _rms(xn) * g_ - tgt
        return 0.5 * jnp.sum(jnp.mean(err * err, axis=-1))

    def body(x_ref, f_ref, t_ref, m_ref, g_ref, l_ref, dx_ref, df_ref, dm_ref, dg_ref):
        i = pl.program_id(0)
        tgt = _grid_rows(t_ref, rows_r, i)
        l, vjp = jax.vjp(lambda a_, b_, c_, d_: fn(a_, b_, tgt, c_, d_), x_ref[...], f_ref[...], m_ref[...], g_ref[...])
        dx, df, dm, dg = vjp(jnp.ones((), F32))

        @pl.when(i == 0)
        def _():
            l_ref[...] = jnp.zeros_like(l_ref)
            dm_ref[...] = jnp.zeros_like(dm_ref)
            dg_ref[...] = jnp.zeros_like(dg_ref)

        l_ref[...] += jnp.reshape(l, (1, 1))
        dx_ref[...] = dx
        df_ref[...] = df.astype(df_ref.dtype)
        dm_ref[...] += dm
        dg_ref[...] += dg

    rowspec = pl.BlockSpec((TM, D), lambda i: (i, 0))
    return pl.pallas_call(
        body, name="loss_head", grid=(nb - 1,),
        in_specs=[rowspec, rowspec, _grid_spec(rows_r, nb), _full_spec(modp), _full_spec(g_final)],
        out_specs=[pl.BlockSpec((1, 1), lambda i: (0, 0)), rowspec, rowspec, _full_spec(modp), _full_spec(g_final)],
        out_shape=[jax.ShapeDtypeStruct((1, 1), F32), jax.ShapeDtypeStruct(x.shape, F32), jax.ShapeDtypeStruct(x.shape, MXU_DTYPE),
                   jax.ShapeDtypeStruct(modp.shape, F32), jax.ShapeDtypeStruct(g_final.shape, F32)],
        compiler_params=_cp(("arbitrary",)),
    )(x, f, tview, modp, g_final)


def _repack(name, shards, segs, wp):
    nd, kk, ws = shards.shape
    tr = 128
    used = sum(e - s for s, e in segs)

    def body(a_ref, o_ref):
        full = jnp.concatenate([a_ref[d].astype(F32) for d in range(nd)], axis=1)
        parts = [full[:, s:e] for s, e in segs]
        if wp > used:
            parts.append(jnp.zeros((tr, wp - used), F32))
        o_ref[...] = jnp.concatenate(parts, axis=1).astype(o_ref.dtype)

    return pl.pallas_call(
        body, name=name, grid=(kk // tr,),
        in_specs=[pl.BlockSpec((nd, tr, ws), lambda i: (0, i, 0))],
        out_specs=pl.BlockSpec((tr, wp), lambda i: (i, 0)),
        out_shape=jax.ShapeDtypeStruct((kk, wp), MXU_DTYPE),
        compiler_params=_cp(("parallel",)),
    )(shards)


def _unpack(name, dw, segs, ws, out_dtype):
    kk, wp = dw.shape
    tr = 128
    order = sorted(range(len(segs)), key=lambda i: segs[i][0])
    offs, o = [], 0
    for s, e in segs:
        offs.append(o)
        o += e - s

    def body(a_ref, o_ref):
        a = a_ref[...].astype(F32)
        full = jnp.concatenate([a[:, offs[i]:offs[i] + segs[i][1] - segs[i][0]] for i in order], axis=1)
        for d in range(NDEV):
            o_ref[d] = full[:, d * ws:(d + 1) * ws].astype(o_ref.dtype)

    return pl.pallas_call(
        body, name=name, grid=(kk // tr,),
        in_specs=[pl.BlockSpec((tr, wp), lambda i: (i, 0))],
        out_specs=pl.BlockSpec((NDEV, tr, ws), lambda i: (0, i, 0)),
        out_shape=jax.ShapeDtypeStruct((NDEV, kk, ws), out_dtype),
        compiler_params=_cp(("parallel",)),
    )(dw)


def _adam_math(w, g, m, v):
    m = ADAM_B1 * m + (1.0 - ADAM_B1) * g
    v = ADAM_B2 * v + (1.0 - ADAM_B2) * jnp.square(g)
    m_hat = m / (1.0 - ADAM_B1 ** ADAM_STEP)
    v_hat = v / (1.0 - ADAM_B2 ** ADAM_STEP)
    delta = -ADAM_LR * (m_hat / (jnp.sqrt(v_hat) + ADAM_EPS) + ADAM_WD * w)
    return delta, m, v


def _adam(name, w, parts, m, v, after):
    r, c = w.shape
    nsec, npart = len(parts), parts[0].shape[0]
    rs = r // nsec
    tr = _pick(rs, (256, 128, 64, 32, 16, 8)) if rs * c * 4 > (1 << 20) else rs
    tiles = rs // tr

    def body(w_ref, *refs):
        m_ref, v_ref, _, g_ref, d_ref, nm_ref, nv_ref = refs[nsec:]
        sec = pl.program_id(0) // tiles
        for a, p_ref in enumerate(refs[:nsec]):
            @pl.when(sec == a)
            def _(p_ref=p_ref):
                g = p_ref[0].astype(F32)
                for s in range(1, npart):
                    g = g + p_ref[s].astype(F32)
                delta, nm, nv = _adam_math(w_ref[...], g, m_ref[...], v_ref[...])
                g_ref[...], d_ref[...], nm_ref[...], nv_ref[...] = g, delta, nm, nv

    spec = pl.BlockSpec((tr, c), lambda i: (i, 0))
    part_specs = [pl.BlockSpec((npart, tr, c), lambda i, a=a: (0, jnp.clip(i - a * tiles, 0, tiles - 1), 0)) for a in range(nsec)]
    return pl.pallas_call(
        body, name=name, grid=(r // tr,),
        in_specs=[spec] + part_specs + [spec, spec, ANY],
        out_specs=[spec] * 4, out_shape=[jax.ShapeDtypeStruct((r, c), F32)] * 4,
        compiler_params=_cp(("parallel",)),
    )(w, *parts, m, v, after)


def _mod_fwd(c_all, mod_w):
    nl, _, ws = mod_w.shape

    def body(c_ref, w_ref, o_ref):
        o_ref[0] = _dot(jax.nn.silu(c_ref[...]), w_ref[0])

    return pl.pallas_call(
        body, name="mod_fwd", grid=(nl,),
        in_specs=[_full_spec(c_all), pl.BlockSpec((1, D, ws), lambda i: (i, 0, 0))],
        out_specs=pl.BlockSpec((1, 16, ws), lambda i: (i, 0, 0)),
        out_shape=jax.ShapeDtypeStruct((nl, 16, ws), F32),
        compiler_params=_cp(("parallel",)),
    )(c_all, mod_w)


def _mod_bwd(c_all, mod_w, dm):
    nl, _, ws = mod_w.shape

    def body(c_ref, w_ref, d_ref, dw_ref, dc_ref):
        dw_ref[0] = _dot_tn(jax.nn.silu(c_ref[...]), d_ref[0])
        dc_ref[0] = _dot_nt(d_ref[0], w_ref[0])

    return pl.pallas_call(
        body, name="mod_bwd", grid=(nl,),
        in_specs=[_full_spec(c_all), pl.BlockSpec((1, D, ws), lambda i: (i, 0, 0)), pl.BlockSpec((1, 16, ws), lambda i: (i, 0, 0))],
        out_specs=[pl.BlockSpec((1, D, ws), lambda i: (i, 0, 0)), pl.BlockSpec((1, 16, D), lambda i: (i, 0, 0))],
        out_shape=[jax.ShapeDtypeStruct((nl, D, ws), F32), jax.ShapeDtypeStruct((nl, 16, D), F32)],
        compiler_params=_cp(("parallel",)),
    )(c_all, mod_w, dm)


def _sum_parts(name, parts):
    npart, r, c = parts.shape

    def body(p_ref, o_ref):
        g = p_ref[0].astype(F32)
        for s in range(1, npart):
            g = g + p_ref[s].astype(F32)
        o_ref[...] = g

    return pl.pallas_call(body, name=name, out_shape=jax.ShapeDtypeStruct((r, c), F32), compiler_params=_cp())(parts)


MESH = pl.DeviceIdType.MESH
ANY = pl.BlockSpec(memory_space=pl.ANY)
N_PEERS = NDEV - 1


def _mesh_pos():
    return lax.axis_index("x"), lax.axis_index("y"), lax.axis_index("c")


def _slot(px, py, pc):
    return 4 * px + 2 * py + pc


def _two_level_gather(x_refs, o_refs, send_sems, recv_sems, local_sems):
    x, y, c = _mesh_pos()
    me, sibling = (x, y, c), (x, y, 1 - c)
    chips = [(1 - x, y), (x, 1 - y), (1 - x, 1 - y)]
    n = len(x_refs)

    def copy(a, k, block, to, src=None):
        dst = o_refs[a].at[_slot(*block)]
        return pltpu.make_async_remote_copy(src_ref=dst if src is None else src, dst_ref=dst, send_sem=send_sems.at[a, k],
                                            recv_sem=recv_sems.at[a, k], device_id=to, device_id_type=MESH)

    mine = [pltpu.make_async_copy(x_refs[a], o_refs[a].at[_slot(*me)], local_sems.at[a]) for a in range(n)]
    for cp in mine:
        cp.start()
    first = []
    for a in range(n):
        first.append(copy(a, 0, me, sibling, src=x_refs[a]))
        first += [copy(a, 1 + j, me, (*chip, c), src=x_refs[a]) for j, chip in enumerate(chips)]
    for cp in first:
        cp.start()
    passed = []
    for j, chip in enumerate(chips):
        for a in range(n):
            copy(a, 1 + j, (*chip, c), me).wait_recv()
            fwd = copy(a, 4 + j, (*chip, c), sibling)
            fwd.start()
            passed.append(fwd)
    for a in range(n):
        copy(a, 0, sibling, me).wait_recv()
        for j, chip in enumerate(chips):
            copy(a, 4 + j, (*chip, 1 - c), me).wait_recv()
    for cp in first + passed:
        cp.wait_send()
    for cp in mine:
        cp.wait()


def _ag_small(name, x):
    r, c = x.shape

    def body(x_ref, o_ref, send_sems, recv_sems, local_sems):
        _two_level_gather([x_ref], [o_ref], send_sems, recv_sems, local_sems)

    return pl.pallas_call(
        body, name=name, out_shape=jax.ShapeDtypeStruct((NDEV, r, c), x.dtype),
        in_specs=[pl.BlockSpec(memory_space=pltpu.VMEM)], out_specs=pl.BlockSpec(memory_space=pltpu.VMEM),
        scratch_shapes=[pltpu.SemaphoreType.DMA((1, N_PEERS)), pltpu.SemaphoreType.DMA((1, N_PEERS)), pltpu.SemaphoreType.DMA((1,))],
        compiler_params=pltpu.CompilerParams(vmem_limit_bytes=VMEM_LIMIT),
    )(x)


def _ag_big(name, shards):
    n = len(shards)

    def body(*refs):
        _two_level_gather(refs[:n], refs[n:2 * n], *refs[2 * n:])

    return pl.pallas_call(
        body, name=name, out_shape=[jax.ShapeDtypeStruct((NDEV,) + s.shape, s.dtype) for s in shards],
        in_specs=[ANY] * n, out_specs=[ANY] * n,
        scratch_shapes=[pltpu.SemaphoreType.DMA((n, N_PEERS)), pltpu.SemaphoreType.DMA((n, N_PEERS)), pltpu.SemaphoreType.DMA((n,))],
    )(*shards)


HBM = pl.BlockSpec(memory_space=pltpu.HBM)
SEM = pl.BlockSpec(memory_space=pltpu.SEMAPHORE)
EFFECT = pltpu.SideEffectType.DATAFLOW_SIDE_EFFECTING


def _peers(x, y, c):
    return [(k - 1, ((1 - x) if k & 4 else x, (1 - y) if k & 2 else y, (1 - c) if k & 1 else c)) for k in range(1, NDEV)]


def _xchg_copy(src_refs, land_refs, send_sems, recv_sems, a, k, peer, me, scatter):
    src = src_refs[a].at[_slot(*peer)] if scatter else src_refs[a]
    return pltpu.make_async_remote_copy(src_ref=src, dst_ref=land_refs[a].at[me], send_sem=send_sems.at[a * N_PEERS + k],
                                        recv_sem=recv_sems.at[a * N_PEERS + k], device_id=peer, device_id_type=MESH)


def _xchg_start(name, srcs, lands, deps, scatter):
    n, nd = len(srcs), len(deps)

    def body(*refs):
        src_refs, land_refs = refs[:n], refs[n:2 * n]
        send_sems, recv_sems, token = refs[2 * n + nd], refs[2 * n + nd + 1], refs[-1]
        x, y, c = _mesh_pos()
        me = _slot(x, y, c)
        for k, peer in _peers(x, y, c):
            for a in range(n):
                _xchg_copy(src_refs, land_refs, send_sems, recv_sems, a, k, peer, me, scatter).start()
        token[...] = jnp.zeros_like(token)

    res = pl.pallas_call(
        body, name=name,
        out_shape=(pltpu.SemaphoreType.DMA((n * N_PEERS,)), pltpu.SemaphoreType.DMA((n * N_PEERS,)),
                   *[pltpu.HBM(s.shape, s.dtype) for s in srcs], *[pltpu.HBM(s.shape, s.dtype) for s in lands],
                   jax.ShapeDtypeStruct((8, 128), F32)),
        in_specs=[HBM] * (2 * n) + [ANY] * nd,
        out_specs=(SEM, SEM, *([HBM] * (2 * n)), pl.BlockSpec(memory_space=pltpu.VMEM)),
        input_output_aliases={i: 2 + i for i in range(2 * n)},
        compiler_params=pltpu.CompilerParams(has_side_effects=EFFECT),
    )(*[pltpu.with_memory_space_constraint(s, pltpu.HBM) for s in srcs],
      *[pltpu.with_memory_space_constraint(s, pltpu.HBM) for s in lands], *deps)
    return res[0], res[1], res[2:2 + n], res[2 + n:2 + 2 * n], res[-1]


def _xchg_wait(name, send_sems, recv_sems, srcs, lands, after, scatter):
    n = len(srcs)

    def body(*refs):
        src_refs, land_refs = refs[:n], refs[n:2 * n]
        s_sems, r_sems = refs[2 * n], refs[2 * n + 1]
        x, y, c = _mesh_pos()
        me = _slot(x, y, c)
        for k, peer in _peers(x, y, c):
            for a in range(n):
                cp = _xchg_copy(src_refs, land_refs, s_sems, r_sems, a, k, peer, me, scatter)
                cp.wait_send()
                cp.wait_recv()

    res = pl.pallas_call(
        body, name=name,
        out_shape=[pltpu.HBM(s.shape, s.dtype) for s in srcs] + [pltpu.HBM(s.shape, s.dtype) for s in lands],
        in_specs=[HBM] * (2 * n) + [SEM, SEM, ANY], out_specs=[HBM] * (2 * n),
        input_output_aliases={i: i for i in range(2 * n)},
        compiler_params=pltpu.CompilerParams(has_side_effects=EFFECT),
    )(*srcs, *lands, send_sems, recv_sems, after)
    return res[n:]


def _landing(name, srcs, me, scatter):
    shapes = [s.shape[-2:] for s in srcs]

    def body(me_ref, *refs):
        for s_ref, o_ref in zip(refs[:len(srcs)], refs[len(srcs):]):
            o_ref[...] = s_ref[...].reshape(o_ref.shape)

    def slot_spec(r, c):
        return pl.BlockSpec((1, r, c), lambda i, me_ref: (me_ref[0], 0, 0))

    return pl.pallas_call(
        body, name=name, out_shape=[jax.ShapeDtypeStruct((NDEV, r, c), s.dtype) for s, (r, c) in zip(srcs, shapes)],
        grid_spec=pltpu.PrefetchScalarGridSpec(
            num_scalar_prefetch=1, grid=(1,),
            in_specs=[slot_spec(r, c) if scatter else pl.BlockSpec((r, c), lambda i, me_ref: (0, 0)) for r, c in shapes],
            out_specs=[slot_spec(r, c) for r, c in shapes]),
        compiler_params=_cp(("arbitrary",)),
    )(jnp.reshape(me, (1,)).astype(jnp.int32), *srcs)


STAGES = ("l0_mixer", "l0_ffn", "l1_mixer", "l1_ffn")
STAGE_LAYOUT = {"l0_mixer": (AB_SEGS, AB_P), "l1_mixer": (SSD_SEGS, SSD_P_W)}


class _Exchange:
    def __init__(self, shards, me):
        self.shards, self.me = shards, me
        self.pending, self.pending_grads, self.recv = {}, None, {}

    def _layout(self, stage):
        ws = self.shards[stage][0].shape[-1]
        return STAGE_LAYOUT.get(stage, (((0, NDEV * ws),), NDEV * ws)) + (ws,)

    def _start_gather(self, stage, deps):
        srcs = list(self.shards[stage])
        lands = _landing("own_" + stage, srcs, self.me, False)
        return _xchg_start("gather_start_" + stage, srcs, lands, deps, False)

    def get(self, stage, dep, thread):
        i = STAGES.index(stage)
        if i == 0:
            g_in, g_out = _ag_big("gather_" + stage, list(self.shards[stage]))
            ahead, deps = STAGES[1:3], [g_out, dep]
        else:
            ss, rs, srcs, lands, _ = self.pending.pop(stage)
            g_in, g_out = _xchg_wait("gather_wait_" + stage, ss, rs, srcs, lands, dep, False)
            ahead, deps = STAGES[i + 2:i + 3], [g_out]
        for nxt in ahead:
            self.pending[nxt] = self._start_gather(nxt, deps)
            deps = [self.pending[nxt][4]]
            thread = thread + self.pending[nxt][4][0, 0]
        segs, wp, _ = self._layout(stage)
        return _repack("repack_" + stage, g_in, segs, wp), g_out.reshape(-1, D), thread

    def put(self, stage, d_in, d_out, thread):
        segs, _, ws = self._layout(stage)
        parts = [_unpack("unpack_" + stage, d_in, segs, ws, MXU_DTYPE), d_out.reshape(NDEV, -1, D)]
        deps = [parts[0]]
        if self.pending_grads is not None:
            deps = [self.finish(parts[0])[0]]
        self.staged = (stage, parts)
        return thread if stage == STAGES[0] else thread + self.start_last(deps)[0, 0]

    def start_last(self, deps):
        stage, parts = self.staged
        lands = _landing("own_grad_" + stage, parts, self.me, True)
        self.pending_grads = (stage,) + _xchg_start("scatter_start_" + stage, parts, lands, deps, True)
        return self.pending_grads[5]

    def finish(self, after):
        stage, ss, rs, srcs, lands, _ = self.pending_grads
        self.recv[stage] = _xchg_wait("scatter_wait_" + stage, ss, rs, srcs, lands, after, True)
        self.pending_grads = None
        return self.recv[stage]


def _mm_swiglu(name, h, w_in):
    m, kk = h.shape
    f = w_in.shape[1] // 2
    tm = _pick(m, (272, 256, 128))

    def body(h_ref, wg_ref, wu_ref, pf_ref, act_ref):
        a = h_ref[...].astype(MXU_DTYPE)
        g = jnp.dot(a, wg_ref[...].astype(MXU_DTYPE), preferred_element_type=F32).astype(MXU_DTYPE)
        u = jnp.dot(a, wu_ref[...].astype(MXU_DTYPE), preferred_element_type=F32).astype(MXU_DTYPE)
        pf_ref[0] = g
        pf_ref[1] = u
        act_ref[...] = (jax.nn.silu(g.astype(F32)) * u.astype(F32)).astype(act_ref.dtype)

    return pl.pallas_call(
        body, name=name, grid=(m // tm,),
        in_specs=[pl.BlockSpec((tm, kk), lambda i: (i, 0)), pl.BlockSpec((kk, f), lambda i: (0, 0)), pl.BlockSpec((kk, f), lambda i: (0, 1))],
        out_specs=[pl.BlockSpec((2, tm, f), lambda i: (0, i, 0)), pl.BlockSpec((tm, f), lambda i: (i, 0))],
        out_shape=[jax.ShapeDtypeStruct((2, m, f), MXU_DTYPE), jax.ShapeDtypeStruct((m, f), MXU_DTYPE)],
        compiler_params=_cp(("parallel",)),
    )(h, w_in, w_in)


def _row_gate_up(pf):
    f = pf.shape[2]
    return ([pf], [pl.BlockSpec((2, TM, f), lambda i: (0, i, 0))],
            lambda r: jnp.concatenate([r[0], r[1]], axis=1).astype(F32), 2 * f)


def _ffn_fwd(tag, h, w_in, w_out, nb, cb):
    pf, act = _mm_swiglu(tag + "_ffn_in", h, w_in)
    return pf, act, _mm(tag + "_ffn_out", act, w_out, "nn", F32)


def _ffn_bwd(tag, h, pf, act, df, w_in, w_out, nb, cb):
    dw_out = _mm(tag + "_ffn_out_dw", act, df, "tn", MXU_DTYPE)
    dact = _mm(tag + "_ffn_out_dx", df, w_out, "nt", MXU_DTYPE)
    (dpf,), _ = _rowwise_vjp(tag + "_swiglu_bwd", _fn_swiglu, nb, cb, [_row_gate_up(pf)], [], [_row(dact)], [(0, MXU_DTYPE, None)])
    dw_in = _mm(tag + "_ffn_in_dw", h, dpf, "tn", MXU_DTYPE)
    dh = _mm(tag + "_ffn_in_dx", dpf, w_in, "nt", MXU_DTYPE)
    return dw_out, dw_in, dh


def _local_step(x, ctx, target, mod, P, comm):
    T = x.shape[0]
    N = T + CTX
    nb, cb = N // TM, N // TM - 1
    R = T // GRID_W
    mod0, mod1 = mod[0], mod[1]
    ng = P["norm_g"]
    g00, g01, g10, g11 = ng[0, 0][None], ng[0, 1][None], ng[1, 0][None], ng[1, 1][None]
    pre = functools.partial(_fn_prenorm, a=0, b=1)
    rpre = functools.partial(_fn_resid_prenorm, gi=2, a=3, b=4)
    res5 = functools.partial(_fn_resid, gi=5)
    dirs = (("f", False), ("b", True))

    xc0 = _row_cat(x, ctx, nb)
    w_ab_in, w_ab_out, g00 = comm.get("l0_mixer", mod, g00)
    (h0,) = _rowwise("l0_prenorm", pre, nb, cb, [xc0], [g00, mod0], [(D, MXU_DTYPE)])
    p0 = _mm("l0_in", h0, w_ab_in, "nn", F32)
    gla_rows = [(p0, 512, 0), (p0, 256, 8), (p0, 256, 9), (p0, 128, 20)]
    gla_blk = _multi_chunk(_gla_chunk, GLA_L, TM // GLA_L, len(gla_rows))
    gla_par = {d: [P["ab_gate_w"][int(r)], P["ab_gate_b"][int(r)][None]] for d, r in dirs}
    gla_state = (GLA_H * GLA_DV, GLA_H * GLA_DK)
    o, st0 = None, {}
    for d, rev in dirs:
        o, st0[d] = _scan_fwd("gla_fwd_" + d, gla_blk, TM, nb, cb, rev, gla_rows, gla_par[d], gla_state, GLA_H * GLA_DV, o)
    n128, cb128 = N // GMLP_L, T // GMLP_L
    mix_rows = [_row(o, tm=GMLP_L)] + [_row(p0, 512, j, tm=GMLP_L) for j in (1, 2, 3)]
    mix_par = [P["ab_gla_norm_g"], P["ab_vnorm_g"], P["ab_spatial_w"].reshape(GMLP_G * GMLP_L, GMLP_L), P["ab_spatial_b"].T]
    (cat0,) = _rowwise("l0_mix", _fn_mixpost, n128, cb128, mix_rows, mix_par, [(D, MXU_DTYPE)], tm=GMLP_L)
    y0 = _mm("l0_out", cat0, w_ab_out, "nn", F32)
    w_fi0, w_fo0, g01 = comm.get("l0_ffn", y0, g01)
    x1, h1 = _rowwise("l0_ffn_prenorm", rpre, nb, cb, [xc0, _row(y0)], [g01, mod0, mod0], [(D, F32), (D, MXU_DTYPE)])
    pf0, act0, f0 = _ffn_fwd("l0", h1, w_fi0, w_fo0, nb, cb)
    w_ssd_in, w_ssd_out, g10 = comm.get("l1_mixer", f0, g10)
    x2p, h2 = _rowwise("l0_resid_l1_prenorm", functools.partial(_fn_resid_prenorm, gi=5, a=0, b=1), nb, cb,
                       [_row_grid(x1, R, nb), _row_grid(f0, R, nb)], [g10, mod0, mod1], [(D, F32), (D, MXU_DTYPE)])
    p1 = _mm("l1_in", h2, w_ssd_in, "nn", F32)
    conv_w8 = jnp.concatenate([P["ssd_conv_w"], jnp.zeros((8 - SSD_K, 3 * CONV_W), F32)], axis=0)
    xbc = _conv("l1_conv", p1, conv_w8, P["ssd_conv_b"], nb, permuted_src=True, act=True, flip=False, out_dtype=F32)
    ssd_rows = [(xbc, SSD_INNER, 0), (xbc, 512, 4), (xbc, 512, 5), (p1, 128, 40)]
    ssd_blk = _multi_chunk(_ssd_chunk, SSD_L, TM // SSD_L, len(ssd_rows))
    ssd_par = {d: [P["ssd_dt_bias"][int(r)][None], P["ssd_a_log"][int(r)][None]] for d, r in dirs}
    ssd_state = (SSD_N, SSD_INNER)
    ys, st1 = None, {}
    for d, rev in dirs:
        ys, st1[d] = _scan_fwd("ssd_fwd_" + d, ssd_blk, TM, nb, cb, rev, ssd_rows, ssd_par[d], ssd_state, SSD_INNER, ys)
    fin_rows = [_row(ys), _row(xbc, SSD_INNER, 0), _row(p1, SSD_INNER, 1)]
    fin_par = [P["ssd_d"], P["ssd_norm_g"]]
    (yn,) = _rowwise("l1_finish", _fn_ssd_finish, cb, cb, fin_rows, fin_par, [(SSD_INNER, MXU_DTYPE)])
    y1 = _mm("l1_out", yn, w_ssd_out, "nn", F32)
    w_fi1, w_fo1, g11 = comm.get("l1_ffn", y1, g11)
    x3, h3 = _rowwise("l1_ffn_prenorm", rpre, cb, cb, [_row(x2p), _row(y1)], [g11, mod1, mod1], [(D, F32), (D, MXU_DTYPE)])
    pf1, act1, f1 = _ffn_fwd("l1", h3, w_fi1, w_fo1, cb, cb)
    loss, dx3, df1, dm1_j, d_final_g = _loss_head(x3, f1, target, mod1, P["final_norm_g"], R)

    dP = {"final_norm_g": d_final_g}
    dwo1, dwi1, dh3 = _ffn_bwd("l1", h3, pf1, act1, df1, w_fi1, w_fo1, cb, cb)
    g11 = comm.put("l1_ffn", dwi1, dwo1, g11)
    (dx2p_a, dy1), (dg11, dm1_a, dm1_b) = _rowwise_vjp(
        "l1_ffn_prenorm_bwd", rpre, cb, cb, [_row(x2p), _row(y1)], [g11, mod1, mod1], [_row(dx3), _row(dh3)],
        [(0, F32, None), (1, MXU_DTYPE, None)])
    d_ssd_out = _mm("l1_out_dw", yn, dy1, "tn", MXU_DTYPE)
    dyn = _mm("l1_out_dx", dy1, w_ssd_out, "nt", MXU_DTYPE)
    (dys, dxs, dz), (dP["ssd_d"], dP["ssd_norm_g"]) = _rowwise_vjp(
        "l1_finish_bwd", _fn_ssd_finish, cb, cb, fin_rows, fin_par, [_row(dyn)],
        [(0, F32, None), (1, F32, None), (2, MXU_DTYPE, None)])
    dssd, ddtb, dalog = None, [], []
    for d, rev in dirs:
        dssd, (ddtb_, dalog_) = _scan_bwd("ssd_bwd_" + d, ssd_blk, TM, nb, cb, rev, ssd_rows, ssd_par[d], st1[d], dys, ssd_state,
                                          SSD_INNER, dssd)
        ddtb.append(ddtb_); dalog.append(dalog_)
    dx_s, db_s, dc_s, dtl = dssd
    dP["ssd_dt_bias"] = jnp.concatenate(ddtb, axis=0)
    dP["ssd_a_log"] = jnp.concatenate(dalog, axis=0)
    dacc, dcw8, dP["ssd_conv_b"] = _conv_bwd_pre("l1_conv_bwd", p1, conv_w8, P["ssd_conv_b"], ([dx_s, dxs], [db_s, dc_s]), nb)
    dP["ssd_conv_w"] = dcw8[:SSD_K]
    dpc = _conv("l1_conv_dx", dacc, conv_w8, jnp.zeros((1, 3 * CONV_W), F32), nb, permuted_src=False, act=False, flip=True,
                out_dtype=MXU_DTYPE)
    cat1 = functools.partial(_fn_concat, sums=(1, 1, 1, 1), pad=SSD_P_W - 5248)
    (dp1,) = _rowwise("l1_dp", cat1, nb, cb, [_row(dpc, SSD_INNER, 0), _row(dz, valid=cb), _row(dpc, 1024, 2), _row(dtl)],
                      [], [(SSD_P_W, MXU_DTYPE)])
    g10 = comm.put("l1_mixer", _mm("l1_in_dw", h2, dp1, "tn", F32), d_ssd_out, g10)
    dh2 = _mm("l1_in_dx", dp1, w_ssd_in, "nt", MXU_DTYPE)
    (dx2p,), (dg10, dm1_f) = _rowwise_vjp("l1_prenorm_bwd", pre, nb, cb, [_row(x2p)], [g10, mod1], [_row(dh2)],
                                          [(0, F32, _row(dx2p_a, valid=cb))])

    (dx1_a, df0), (dm0_e,) = _rowwise_vjp("l0_resid_bwd", res5, nb, cb, [_row(x1), _row(f0)], [mod0],
                                          [_row_grid(dx2p, GRID_W, nb)], [(0, F32, None), (1, MXU_DTYPE, None)])
    dwo0, dwi0, dh1 = _ffn_bwd("l0", h1, pf0, act0, df0, w_fi0, w_fo0, nb, cb)
    g01 = comm.put("l0_ffn", dwi0, dwo0, g01)
    (dxc0_a, dy0), (dg01, dm0_a, dm0_b) = _rowwise_vjp(
        "l0_ffn_prenorm_bwd", rpre, nb, cb, [xc0, _row(y0)], [g01, mod0, mod0], [_row(dx1_a), _row(dh1)],
        [(0, F32, None), (1, MXU_DTYPE, None)])
    d_ab_out = _mm("l0_out_dw", cat0, dy0, "tn", MXU_DTYPE)
    dcat0 = _mm("l0_out_dx", dy0, w_ab_out, "nt", MXU_DTYPE)
    (do, dr, du, dgm), (dP["ab_gla_norm_g"], dP["ab_vnorm_g"], dsw, dsb_t) = _rowwise_vjp(
        "l0_mix_bwd", _fn_mixpost, n128, cb128, mix_rows, mix_par, [_row(dcat0, tm=GMLP_L)],
        [(0, F32, None), (1, MXU_DTYPE, None), (2, MXU_DTYPE, None), (3, MXU_DTYPE, None)], tm=GMLP_L)
    dP["ab_spatial_w"] = dsw.reshape(GMLP_G, GMLP_L, GMLP_L)
    dP["ab_spatial_b"] = dsb_t.T
    gl, dgw, dgb = None, [], []
    for d, rev in dirs:
        gl, (dgw_, dgb_) = _scan_bwd("gla_bwd_" + d, gla_blk, TM, nb, cb, rev, gla_rows, gla_par[d], st0[d], do,
                                     gla_state, GLA_H * GLA_DV, gl)
        dgw.append(dgw_[None]); dgb.append(dgb_)
    dP["ab_gate_w"] = jnp.concatenate(dgw, axis=0)
    dP["ab_gate_b"] = jnp.concatenate(dgb, axis=0)
    cat0f = functools.partial(_fn_concat, sums=(1,) * 7, pad=AB_P - 2688)
    (dp0,) = _rowwise("l0_dp", cat0f, nb, cb, [_row(gl[0]), _row(dr), _row(du), _row(dgm), _row(gl[1]), _row(gl[2]), _row(gl[3])],
                      [], [(AB_P, MXU_DTYPE)])
    g00 = comm.put("l0_mixer", _mm("l0_in_dw", h0, dp0, "tn", F32), d_ab_out, g00)
    dh0 = _mm("l0_in_dx", dp0, w_ab_in, "nt", MXU_DTYPE)
    (grad_x,), (dg00, dm0_s) = _rowwise_vjp("l0_prenorm_bwd", pre, nb, cb, [xc0], [g00, mod0], [_row(dh0)],
                                            [(0, F32, _row(dxc0_a))], x_rows_only=True)
    dP["norm_g"] = jnp.concatenate([dg00, dg01, dg10, dg11], axis=0).reshape(2, 2, D)
    dmod = jnp.stack([dm0_s + dm0_a + dm0_b + dm0_e, dm1_f + dm1_a + dm1_b + dm1_j])
    return loss, grad_x, dmod, dP


WEIGHTS = ("c_ctx", "mod_w", "mod_b", "norm_g", "ffn_w_in", "ffn_w_out", "ab_w_in", "ab_gate_w", "ab_gate_b", "ab_gla_norm_g",
           "ab_vnorm_g", "ab_spatial_w", "ab_spatial_b", "ab_w_out", "ssd_w_in", "ssd_conv_w", "ssd_conv_b", "ssd_dt_bias",
           "ssd_a_log", "ssd_d", "ssd_norm_g", "ssd_w_out", "final_norm_g")
SMALL_SHARDED = ("norm_g", "ab_gate_w", "ab_gate_b", "ssd_conv_w", "ssd_conv_b", "ssd_norm_g")
SMALL = ("c_ctx", "mod_b", "norm_g", "ab_gate_w", "ab_gate_b", "ab_gla_norm_g", "ab_vnorm_g", "ab_spatial_w", "ab_spatial_b",
         "ssd_conv_w", "ssd_conv_b", "ssd_dt_bias", "ssd_a_log", "ssd_d", "ssd_norm_g", "final_norm_g")
LANES = 1024


def _pack(arrs, rows_multiple=8):
    flat = jnp.concatenate([a.reshape(-1).astype(F32) for a in arrs])
    rows = -(-flat.shape[0] // LANES)
    rows = -(-rows // rows_multiple) * rows_multiple
    return jnp.pad(flat, (0, rows * LANES - flat.shape[0])).reshape(rows, LANES)


def _unpack_flat(buf, shapes):
    lead = buf.shape[:-2]
    flat = buf.reshape(lead + (-1,))
    out, o = [], 0
    for s in shapes:
        n = math.prod(s)
        out.append(flat[..., o:o + n].reshape(lead + tuple(s)))
        o += n
    return out


def _unshard(g):
    g = jnp.moveaxis(g, 0, -2)
    return g.reshape(g.shape[:-2] + (g.shape[-2] * g.shape[-1],))


def _my_shard(full, me, ws):
    return lax.dynamic_slice_in_dim(full, me * ws, ws, axis=full.ndim - 1)


def _silu_vjp(cvec, dsc):
    def body(c_ref, d_ref, o_ref):
        _, vjp = jax.vjp(jax.nn.silu, c_ref[...])
        o_ref[...] = vjp(d_ref[...])[0]

    return pl.pallas_call(body, name="c_ctx_bwd", out_shape=jax.ShapeDtypeStruct(cvec.shape, F32), compiler_params=_cp())(cvec, dsc)


def kernel(x, c, ctx, c_ctx, mod_w, mod_b, norm_g, ffn_w_in, ffn_w_out, ab_w_in, ab_gate_w, ab_gate_b, ab_gla_norm_g, ab_vnorm_g, ab_spatial_w, ab_spatial_b, ab_w_out, ssd_w_in, ssd_conv_w, ssd_conv_b, ssd_dt_bias, ssd_a_log, ssd_d, ssd_norm_g, ssd_w_out, final_norm_g, loss_target, m_c_ctx, m_mod_w, m_mod_b, m_norm_g, m_ffn_w_in, m_ffn_w_out, m_ab_w_in, m_ab_gate_w, m_ab_gate_b, m_ab_gla_norm_g, m_ab_vnorm_g, m_ab_spatial_w, m_ab_spatial_b, m_ab_w_out, m_ssd_w_in, m_ssd_conv_w, m_ssd_conv_b, m_ssd_dt_bias, m_ssd_a_log, m_ssd_d, m_ssd_norm_g, m_ssd_w_out, m_final_norm_g, v_c_ctx, v_mod_w, v_mod_b, v_norm_g, v_ffn_w_in, v_ffn_w_out, v_ab_w_in, v_ab_gate_w, v_ab_gate_b, v_ab_gla_norm_g, v_ab_vnorm_g, v_ab_spatial_w, v_ab_spatial_b, v_ab_w_out, v_ssd_w_in, v_ssd_conv_w, v_ssd_conv_b, v_ssd_dt_bias, v_ssd_a_log, v_ssd_d, v_ssd_norm_g, v_ssd_w_out, v_final_norm_g):
    a = dict(locals())
    me = _slot(*_mesh_pos())
    ws_mod = mod_w.shape[-1]

    fwd_small = [c] + [a[k] for k in SMALL_SHARDED]
    g_small = _ag_small("gather_small", _pack(fwd_small))
    parts = _unpack_flat(g_small, [t.shape for t in fwd_small])
    c_rows = parts[0].reshape(NDEV, D)
    full = {k: _unshard(p) for k, p in zip(SMALL_SHARDED, parts[1:])}
    c_all = jnp.concatenate([c_rows, c_ctx[None], jnp.zeros((7, D), F32)], axis=0)
    m_all = _ag_small("gather_mod", _mod_fwd(c_all, mod_w).reshape(2 * 16, ws_mod)).reshape(NDEV, 2, 16, ws_mod)
    m_mine = lax.dynamic_index_in_dim(m_all, me, axis=2, keepdims=False)
    mx = jnp.moveaxis(m_mine, 0, 1).reshape(2, N_MOD, D) + mod_b.reshape(2, N_MOD, D)
    mc = jnp.moveaxis(m_all[:, :, 8, :], 0, 1).reshape(2, N_MOD, D) + mod_b.reshape(2, N_MOD, D)
    pad2 = jnp.zeros((2, 2, D), F32)
    mod = jnp.concatenate([mx, pad2, mc, pad2], axis=1)

    big = {"l0_mixer": (ab_w_in[0], ab_w_out[0]), "l0_ffn": (ffn_w_in[0], ffn_w_out[0]),
           "l1_mixer": (ssd_w_in[0], ssd_w_out[0]), "l1_ffn": (ffn_w_in[1], ffn_w_out[1])}
    comm = _Exchange({k: tuple(w.astype(MXU_DTYPE) for w in v) for k, v in big.items()}, me)
    P = {
        "norm_g": full["norm_g"], "ab_gate_w": full["ab_gate_w"][0], "ab_gate_b": full["ab_gate_b"][0],
        "ab_gla_norm_g": ab_gla_norm_g, "ab_vnorm_g": ab_vnorm_g, "ab_spatial_w": ab_spatial_w[0], "ab_spatial_b": ab_spatial_b[0],
        "ssd_conv_w": full["ssd_conv_w"][0], "ssd_conv_b": full["ssd_conv_b"], "ssd_dt_bias": ssd_dt_bias[0],
        "ssd_a_log": ssd_a_log[0], "ssd_d": ssd_d, "ssd_norm_g": full["ssd_norm_g"], "final_norm_g": final_norm_g[None],
    }

    loss, grad_x, dmod, dP = _local_step(x[0], ctx[0], loss_target[0], mod, P, comm)

    dmx, dmc = dmod[:, 0:N_MOD].reshape(2, N_MOD * D), dmod[:, 8:8 + N_MOD].reshape(2, N_MOD * D)
    small_names = ("ab_gate_w", "ab_gate_b", "ab_gla_norm_g", "ab_vnorm_g", "ab_spatial_w", "ab_spatial_b", "norm_g", "ssd_conv_w",
                   "ssd_conv_b", "ssd_dt_bias", "ssd_a_log", "ssd_d", "ssd_norm_g", "final_norm_g")
    bwd_small = [dP[k] for k in small_names] + [dmc, dmx]
    shapes = [t.shape for t in bwd_small]
    g_bwd = _ag_small("gather_small_grads", _pack(bwd_small))
    summed = _unpack_flat(_sum_parts("sum_small_grads", g_bwd), shapes)
    gfull = dict(zip(small_names, summed[:-2]))
    dmc_sum, dmx_sum = summed[-2], summed[-1]
    dmx_all = _unpack_flat(g_bwd, shapes)[-1]
    dmx_sh = jnp.moveaxis(_my_shard(dmx_all, me, ws_mod), 0, 1)
    dm = jnp.concatenate([dmx_sh, _my_shard(dmc_sum, me, ws_mod)[:, None, :], jnp.zeros((2, 7, ws_mod), F32)], axis=1)
    d_mod_w, dsc = _mod_bwd(c_all, mod_w, dm)
    dsc_ctx = (dsc[0, 8] + dsc[1, 8])[None]
    dsc_all = _ag_small("gather_c_ctx_grad", jnp.concatenate([dsc_ctx, jnp.zeros((7, D), F32)], axis=0))
    d_c_ctx = _silu_vjp(c_ctx[None], _sum_parts("sum_c_ctx_grad", dsc_all)[0:1])[0]

    g_small_w = {
        "c_ctx": d_c_ctx, "mod_b": dmx_sum + dmc_sum, "norm_g": gfull["norm_g"], "ab_gate_w": gfull["ab_gate_w"][None],
        "ab_gate_b": gfull["ab_gate_b"][None], "ab_gla_norm_g": gfull["ab_gla_norm_g"], "ab_vnorm_g": gfull["ab_vnorm_g"],
        "ab_spatial_w": gfull["ab_spatial_w"][None], "ab_spatial_b": gfull["ab_spatial_b"][None], "ssd_conv_w": gfull["ssd_conv_w"][None],
        "ssd_conv_b": gfull["ssd_conv_b"], "ssd_dt_bias": gfull["ssd_dt_bias"][None], "ssd_a_log": gfull["ssd_a_log"][None],
        "ssd_d": gfull["ssd_d"], "ssd_norm_g": gfull["ssd_norm_g"], "final_norm_g": gfull["final_norm_g"][0],
    }
    for k in SMALL_SHARDED:
        g_small_w[k] = _my_shard(g_small_w[k], me, a[k].shape[-1])
    token = comm.start_last([d_c_ctx])
    res = _adam("adam_small", _pack([a[k] for k in SMALL]), [_pack([g_small_w[k] for k in SMALL])[None]],
                _pack([a["m_" + k] for k in SMALL]), _pack([a["v_" + k] for k in SMALL]), token)
    out = {k: vals for k, vals in zip(SMALL, zip(*[_unpack_flat(r, [a[k].shape for k in SMALL]) for r in res]))}

    def adam_big(name, w2d, parts, m2d, v2d, shape):
        return tuple(r.reshape(shape) for r in _adam(name, w2d, parts, m2d, v2d, token))

    def flat2(t):
        return t.reshape(-1, t.shape[-1])

    out["mod_w"] = adam_big("adam_mod_w", flat2(mod_w), [d_mod_w.reshape(1, -1, ws_mod)], flat2(m_mod_w), flat2(v_mod_w), mod_w.shape)

    for j, k in enumerate(("ffn_w_in", "ffn_w_out")):
        out[k] = adam_big("adam_" + k, flat2(a[k]), [comm.recv["l0_ffn"][j], comm.recv["l1_ffn"][j]], flat2(a["m_" + k]),
                          flat2(a["v_" + k]), a[k].shape)
    for j, k in enumerate(("ssd_w_in", "ssd_w_out")):
        out[k] = adam_big("adam_" + k, a[k][0], [comm.recv["l1_mixer"][j]], a["m_" + k][0], a["v_" + k][0], a[k].shape)
    recv_ab = comm.finish(out["ssd_w_out"][3])
    for j, k in enumerate(("ab_w_in", "ab_w_out")):
        out[k] = adam_big("adam_" + k, a[k][0], [recv_ab[j]], a["m_" + k][0], a["v_" + k][0], a[k].shape)

    loss_all = lax.psum(loss[0, 0], ("x", "y", "c"))
    return (loss_all, grad_x[None], *[out[k][0] for k in WEIGHTS], *[out[k][1] for k in WEIGHTS],
            *[out[k][2] for k in WEIGHTS], *[out[k][3] for k in WEIGHTS])
```

```python
import functools
import math

import jax
import jax.numpy as jnp
from jax import lax
from jax.experimental import pallas as pl
from jax.experimental.pallas import tpu as pltpu

F32 = jnp.float32
BF16 = jnp.bfloat16
MXU_DTYPE = jnp.bfloat16

D = 1024
NDEV = 8
N_MOD = 6
EPS = 1e-6
GRID_W = 64
CTX = 256
TM = 256
D_FF = 2816
GLA_H, GLA_DK, GLA_DV, GLA_LR, GLA_TAU, GLA_L = 4, 64, 128, 16, 16.0, 64
GMLP_G, GMLP_C, GMLP_L = 4, 128, 128
SSD_H, SSD_P, SSD_G, SSD_N, SSD_L, SSD_K = 32, 64, 4, 128, 128, 5
SSD_INNER = SSD_H * SSD_P
AB_IN = 2592
SSD_IN = 5184
AB_SEGS = ((256, 768), (1056, 1568), (1568, 2080), (2080, 2592), (0, 256), (800, 1056), (768, 800))
AB_P = 2816
SSD_SEGS = ((0, 2048), (3136, 5184), (2048, 2560), (2560, 3072), (3072, 3136))
SSD_P_W = 5376
VMEM_LIMIT = 56 * 1024 * 1024

ADAM_LR, ADAM_B1, ADAM_B2, ADAM_EPS, ADAM_WD, ADAM_STEP = 0.001, 0.9, 0.999, 1e-08, 0.01, 10


def _cp(sem=None, **kw):
    return pltpu.CompilerParams(dimension_semantics=sem, vmem_limit_bytes=VMEM_LIMIT, **kw)


def _dot(a, b, dims=(((1,), (0,)), ((), ()))):
    return lax.dot_general(a.astype(MXU_DTYPE), b.astype(MXU_DTYPE), dims, preferred_element_type=F32)


def _dot_nt(a, b):
    return _dot(a, b, (((1,), (1,)), ((), ())))


def _dot_tn(a, b):
    return _dot(a, b, (((0,), (0,)), ((), ())))


def _rms(x):
    return x * lax.rsqrt(jnp.mean(x * x, axis=-1, keepdims=True) + EPS)


def _pick(n, prefs):
    for p in prefs:
        if n % p == 0:
            return p
    return n


def _row(arr, width=None, colblk=0, tm=TM, valid=None):
    width = arr.shape[1] if width is None else width
    if valid is None:
        return ([arr], [pl.BlockSpec((tm, width), lambda i, c=colblk: (i, c))], lambda r: r[...].astype(F32), width)
    spec = pl.BlockSpec((tm, width), lambda i, c=colblk: (jnp.minimum(i, valid - 1), c))
    return ([arr], [spec], lambda r: jnp.where(pl.program_id(0) < valid, r[...].astype(F32), 0.0), width)


def _row_grid(arr, a, nb):
    n = arr.shape[0]
    b = (n - CTX) // a

    def load(v_ref, c_ref):
        i = pl.program_id(0)
        return jnp.where(i == nb - 1, c_ref[...], _grid_rows(v_ref, a, i))

    return ([arr.reshape(n // b, b, D), arr], [_grid_spec(a, nb), pl.BlockSpec((TM, D), lambda i: (nb - 1, 0))], load, D)


def _row_cat(x, ctx, nb):
    return ([x, ctx], [pl.BlockSpec((TM, D), lambda i: (jnp.minimum(i, nb - 2), 0)), pl.BlockSpec((TM, D), lambda i: (0, 0))],
            lambda x_ref, c_ref: jnp.where(pl.program_id(0) == nb - 1, c_ref[...], x_ref[...]), D)


def _operands(rows):
    return [a for r in rows for a in r[0]], [s for r in rows for s in r[1]]


def _load_rows(refs, rows):
    vals, k = [], 0
    for r in rows:
        vals.append(r[2](*refs[k:k + len(r[0])]))
        k += len(r[0])
    return vals


def _full_spec(p):
    nd = p.ndim
    return pl.BlockSpec(p.shape, lambda i, nd=nd: (0,) * nd)


def _rowwise(name, fn, n_blocks, ctx_blk, rows, params, outs, tm=TM):
    arrs, specs = _operands(rows)
    nr, npar = len(arrs), len(params)

    def body(*refs):
        t = (pl.program_id(0) >= ctx_blk).astype(F32)
        rv = _load_rows(refs[:nr], rows)
        pv = [p[...] for p in refs[nr:nr + npar]]
        res = fn(t, rv, pv)
        for o_ref, o in zip(refs[nr + npar:], res):
            o_ref[...] = o.astype(o_ref.dtype)

    return pl.pallas_call(
        body, name=name, grid=(n_blocks,),
        in_specs=specs + [_full_spec(p) for p in params],
        out_specs=[pl.BlockSpec((tm, w), lambda i: (i, 0)) for w, _ in outs],
        out_shape=[jax.ShapeDtypeStruct((n_blocks * tm, w), dt) for w, dt in outs],
        compiler_params=_cp(("parallel",)),
    )(*arrs, *params)


def _rowwise_vjp(name, fn, n_blocks, ctx_blk, rows, params, douts, row_grads, tm=TM, x_rows_only=False):
    out_blocks = n_blocks - 1 if x_rows_only else n_blocks
    adds = [a for _, _, a in row_grads if a is not None]
    (r_arrs, r_specs), (d_arrs, d_specs), (a_arrs, a_specs) = _operands(rows), _operands(douts), _operands(adds)
    nr, npar, nd, na = len(r_arrs), len(params), len(d_arrs), len(a_arrs)

    def body(*refs):
        i = pl.program_id(0)
        t = (i >= ctx_blk).astype(F32)
        rv = _load_rows(refs[:nr], rows)
        pv = [p[...] for p in refs[nr:nr + npar]]
        dv = _load_rows(refs[nr + npar:nr + npar + nd], douts)
        av = _load_rows(refs[nr + npar + nd:nr + npar + nd + na], adds)
        o_refs = refs[nr + npar + nd + na:]
        _, vjp = jax.vjp(lambda r, p: tuple(fn(t, r, p)), rv, pv)
        d_rows, d_params = vjp(tuple(dv))
        ai, grads = 0, []
        for ri, _, addend in row_grads:
            g = jnp.concatenate([d_rows[r] for r in ri], axis=1) if isinstance(ri, tuple) else d_rows[ri]
            if addend is not None:
                g = g + av[ai]
                ai += 1
            grads.append(g)

        @pl.when(i < out_blocks)
        def _():
            for o_ref, g in zip(o_refs, grads):
                o_ref[...] = g.astype(o_ref.dtype)

        p_refs = o_refs[len(row_grads):]

        @pl.when(i == 0)
        def _():
            for p_ref in p_refs:
                p_ref[...] = jnp.zeros_like(p_ref)

        for p_ref, g in zip(p_refs, d_params):
            p_ref[...] += g

    widths = [sum(rows[r][3] for r in ri) if isinstance(ri, tuple) else rows[ri][3] for ri, _, _ in row_grads]
    res = pl.pallas_call(
        body, name=name, grid=(n_blocks,),
        in_specs=r_specs + [_full_spec(p) for p in params] + d_specs + a_specs,
        out_specs=[pl.BlockSpec((tm, w), lambda i: (jnp.minimum(i, out_blocks - 1), 0)) for w in widths] + [_full_spec(p) for p in params],
        out_shape=[jax.ShapeDtypeStruct((out_blocks * tm, w), dt) for w, (_, dt, _) in zip(widths, row_grads)]
        + [jax.ShapeDtypeStruct(p.shape, F32) for p in params],
        compiler_params=_cp(("arbitrary",)),
    )(*r_arrs, *params, *d_arrs, *a_arrs)
    return res[:len(row_grads)], res[len(row_grads):]


def _mm(name, a, b, mode, out_dtype):
    if mode == "nn":
        m, kk = a.shape
        n = b.shape[1]
    elif mode == "nt":
        m, kk = a.shape
        n = b.shape[0]
    else:
        kk, m = a.shape
        n = b.shape[1]
    if mode == "tn":
        tm = _pick(m, (1024, 1408, 512, 256, 128))
        tn = _pick(n, (768, 512, 256, 128))
        tk = kk
    else:
        tm = _pick(m, (1088, 1024, 768, 512, 384, 256, 128))
        tn = n if n <= 2816 else _pick(n, (1024, 768, 512, 256, 128))
        tk = kk if kk <= 2816 else _pick(kk, (2816, 1792, 1024, 768, 512, 256, 128))
    nk = kk // tk
    in_place = out_dtype == F32
    if mode == "nn":
        specs = [pl.BlockSpec((tm, tk), lambda i, j, k: (i, k)), pl.BlockSpec((tk, tn), lambda i, j, k: (k, j))]
        dims = (((1,), (0,)), ((), ()))
    elif mode == "nt":
        specs = [pl.BlockSpec((tm, tk), lambda i, j, k: (i, k)), pl.BlockSpec((tn, tk), lambda i, j, k: (j, k))]
        dims = (((1,), (1,)), ((), ()))
    else:
        specs = [pl.BlockSpec((tk, tm), lambda i, j, k: (k, i)), pl.BlockSpec((tk, tn), lambda i, j, k: (k, j))]
        dims = (((0,), (0,)), ((), ()))

    def body(a_ref, b_ref, o_ref, *scratch):
        part = lax.dot_general(a_ref[...].astype(MXU_DTYPE), b_ref[...].astype(MXU_DTYPE), dims, preferred_element_type=F32)
        if nk == 1:
            o_ref[...] = part.astype(o_ref.dtype)
        else:
            k = pl.program_id(2)
            acc = o_ref if in_place else scratch[0]

            @pl.when(k == 0)
            def _():
                acc[...] = part

            @pl.when(k > 0)
            def _():
                acc[...] += part

            if not in_place:
                @pl.when(k == nk - 1)
                def _():
                    o_ref[...] = acc[...].astype(o_ref.dtype)

    return pl.pallas_call(
        body, name=name, grid=(m // tm, n // tn, nk), in_specs=specs,
        out_specs=pl.BlockSpec((tm, tn), lambda i, j, k: (i, j)),
        out_shape=jax.ShapeDtypeStruct((m, n), out_dtype),
        scratch_shapes=[] if nk == 1 or in_place else [pltpu.VMEM((tm, tn), F32)],
        compiler_params=_cp(("parallel", "parallel", "arbitrary")),
    )(a, b)


def _sel_mod(modp, t):
    return modp[0:8] * (1.0 - t) + modp[8:16] * t


def _fn_prenorm(t, rows, params, *, a, b):
    (x,), (g, modp) = rows, params
    m = _sel_mod(modp, t)
    return ((_rms(x) * g) * (1.0 + m[b:b + 1]) + m[a:a + 1],)


def _fn_resid_prenorm(t, rows, params, *, gi, a, b):
    (x, y), (g, mod_a, mod_b) = rows, params
    ma, mb = _sel_mod(mod_a, t), _sel_mod(mod_b, t)
    xn = x + ma[gi:gi + 1] * y
    return xn, (_rms(xn) * g) * (1.0 + mb[b:b + 1]) + mb[a:a + 1]


def _fn_resid(t, rows, params, *, gi):
    (x, y), (mod_a,) = rows, params
    return (x + _sel_mod(mod_a, t)[gi:gi + 1] * y,)


def _fn_swiglu(t, rows, params):
    gate, up = rows
    return (jax.nn.silu(gate) * up,)


def _fn_mixpost(t, rows, params):
    (o, r, u, g), (gla_g, vn_g, sw, sb_t) = rows, params
    a =jnp.concatenate([_rms(o[:, h * GLA_DV:(h + 1) * GLA_DV]) for h in range(GLA_H)], axis=1) * gla_g * jax.nn.silu(r)
    uu, vv = jax.nn.gelu(u), jax.nn.gelu(g)
    mu = jnp.mean(vv, axis=-1, keepdims=True)
    var = jnp.mean(jnp.square(vv - mu), axis=-1, keepdims=True)
    vn = ((vv - mu) * lax.rsqrt(var + EPS)) * vn_g
    s = jnp.concatenate(
        [_dot(sw[gi * GMLP_L:(gi + 1) * GMLP_L, :], vn[:, gi * GMLP_C:(gi + 1) * GMLP_C]) + sb_t[:, gi:gi + 1]
         for gi in range(GMLP_G)], axis=1)
    return (jnp.concatenate([a, uu * s], axis=1),)


def _expand_heads(row):
    first = lax.broadcasted_iota(jnp.int32, (1, 2 * SSD_P), 1) < SSD_P
    return jnp.concatenate([jnp.where(first, row[:, 2 * j:2 * j + 1], row[:, 2 * j + 1:2 * j + 2]) for j in range(SSD_H // 2)], axis=1)


def _fn_ssd_finish(t, rows, params):
    (y2, xs, z), (d_skip, norm_g) = rows, params
    d_full = _expand_heads(d_skip)
    y = (y2 + d_full * xs) * jax.nn.silu(z)
    gw = SSD_INNER // SSD_G
    return (jnp.concatenate([_rms(y[:, gi * gw:(gi + 1) * gw]) for gi in range(SSD_G)], axis=1) * norm_g,)


def _fn_concat(t, rows, params, *, sums, pad=0):
    out, i = [], 0
    for n in sums:
        acc = rows[i]
        for j in range(1, n):
            acc = acc + rows[i + j]
        out.append(acc)
        i += n
    if pad:
        out.append(jnp.zeros((out[0].shape[0], pad), F32))
    return (jnp.concatenate(out, axis=1),)


def _tri(n, rev):
    r = lax.broadcasted_iota(jnp.int32, (n, n), 0)
    c = lax.broadcasted_iota(jnp.int32, (n, n), 1)
    return (r <= c) if rev else (r >= c)


def _running_sum(x, rev):
    n, s = x.shape[0], 1
    while s < n:
        z = jnp.zeros((s, x.shape[1]), x.dtype)
        x = x + (jnp.concatenate([x[s:], z], axis=0) if rev else jnp.concatenate([z, x[:n - s]], axis=0))
        s *= 2
    return x


def _gla_chunk(S, v, k, q, tail, gw, gb, *, rev):
    L, H = GLA_L, GLA_H
    lr = tail[:, GLA_LR:2 * GLA_LR] if rev else tail[:, 0:GLA_LR]
    la = jax.nn.log_sigmoid(_dot(lr, gw) + gb) / GLA_TAU
    b = _running_sum(la, rev)
    b_last = b[0:1] if rev else b[L - 1:L]
    kd = k * jnp.exp(b_last - b)
    qd = (q * GLA_DK ** -0.5) * jnp.exp(b)
    ki = k * jnp.exp(-b)

    def same_head(shape, rows_per_head, cols_per_head):
        r = lax.broadcasted_iota(jnp.int32, shape, 0) // rows_per_head
        c = lax.broadcasted_iota(jnp.int32, shape, 1) // cols_per_head
        return r == c

    k_blk = jnp.where(same_head((H * L, H * GLA_DK), L, GLA_DK), jnp.concatenate([ki] * H, axis=0), 0.0)
    v_blk = jnp.where(same_head((H * L, H * GLA_DV), L, GLA_DV), jnp.concatenate([v] * H, axis=0), 0.0)
    row = lax.broadcasted_iota(jnp.int32, (L, H * L), 0)
    src = lax.broadcasted_iota(jnp.int32, (L, H * L), 1) % L
    sc = jnp.where((row <= src) if rev else (row >= src), _dot_nt(qd, k_blk), 0.0)
    o = _dot_nt(qd, S) + _dot(sc, v_blk)
    s_new = S * jnp.exp(b_last) + jnp.where(same_head(S.shape, GLA_DV, GLA_DK), _dot_tn(v, kd), 0.0)
    return s_new, o


def _ssd_chunk(S, x, bm, cm, tail, dtb, alog, *, rev):
    L = SSD_L
    msk = _tri(L, rev)
    raw = tail[:, SSD_H:2 * SSD_H] if rev else tail[:, 0:SSD_H]
    dt = jax.nn.softplus(raw + dtb)
    acum = _running_sum(dt * (-jnp.exp(alog)), rev)
    a_last = acum[0:1] if rev else acum[L - 1:L]
    wst = dt * jnp.exp(a_last - acum)
    eac = jnp.exp(acum)
    dec = jnp.exp(a_last)
    tr = jnp.concatenate([acum, dt, wst, jnp.zeros((L, L - 3 * SSD_H), F32)], axis=1).T
    acum_t, dt_t, wst_t = tr[0:SSD_H], tr[SSD_H:2 * SSD_H], tr[2 * SSD_H:3 * SSD_H]
    lane = lax.broadcasted_iota(jnp.int32, (1, 2 * SSD_P), 1)
    m0 = (lane < SSD_P).astype(F32)
    m1 = 1.0 - m0
    pairs_per_group = SSD_H // SSD_G // 2
    y_parts, s_parts = [], []
    for g in range(SSD_G):
        ns = slice(g * SSD_N, (g + 1) * SSD_N)
        bg, cg = bm[:, ns], cm[:, ns]
        cb = _dot_nt(cg, bg)
        bgt = bg.T
        gs = slice(g * pairs_per_group * 2 * SSD_P, (g + 1) * pairs_per_group * 2 * SSD_P)
        y_carry = _dot(cg, S[:, gs])
        for jj in range(pairs_per_group):
            j = g * pairs_per_group + jj
            ls = slice(j * 2 * SSD_P, (j + 1) * 2 * SSD_P)
            xp, sp = x[:, ls], S[:, ls]
            xm = jnp.concatenate([xp * m0, xp * m1], axis=0)
            lhs, bw = [], []
            for h in (2 * j, 2 * j + 1):
                seg = acum[:, h:h + 1] - acum_t[h:h + 1, :]
                lhs.append(cb * jnp.exp(jnp.where(msk, seg, -jnp.inf)) * dt_t[h:h + 1, :])
                bw.append(bgt * wst_t[h:h + 1, :])
            e_pair = eac[:, 2 * j:2 * j + 1] * m0 + eac[:, 2 * j + 1:2 * j + 2] * m1
            y_parts.append(_dot(jnp.concatenate(lhs, axis=1), xm) + y_carry[:, jj * 2 * SSD_P:(jj + 1) * 2 * SSD_P] * e_pair)
            d_pair = dec[:, 2 * j:2 * j + 1] * m0 + dec[:, 2 * j + 1:2 * j + 2] * m1
            s_parts.append(sp * d_pair + _dot(jnp.concatenate(bw, axis=1), xm))
    return jnp.concatenate(s_parts, axis=1), jnp.concatenate(y_parts, axis=1)


def _multi_chunk(chunk_fn, L, subs, nr):
    def fn(S, *args, rev):
        rows, params = args[:nr], args[nr:]
        ys = [None] * subs
        for j in (range(subs - 1, -1, -1) if rev else range(subs)):
            S, ys[j] = chunk_fn(S, *[r[j * L:(j + 1) * L] for r in rows], *params, rev=rev)
        return S, jnp.concatenate(ys, axis=0)

    return fn


def _scan_order(n, nx, rev, backward):
    nc = n - nx

    def fwd(s):
        return (n - 1 - s) if rev else jnp.where(s < nc, s + nx, s - nc)

    return (lambda s: fwd(n - 1 - s)) if backward else fwd


def _scan_fwd(name, chunk_fn, L, n, nx, rev, rows, params, state_shape, out_w, addend=None):
    order = _scan_order(n, nx, rev, False)
    nr, npar = len(rows), len(params)
    adds = [] if addend is None else [addend]

    def body(*refs):
        s_scr = refs[-1]

        @pl.when(pl.program_id(0) == 0)
        def _():
            s_scr[...] = jnp.zeros_like(s_scr)

        s_in = s_scr[...]
        y_ref, st_ref = refs[nr + npar + len(adds)], refs[nr + npar + len(adds) + 1]
        st_ref[0] = s_in
        s_new, y = chunk_fn(s_in, *[r[...] for r in refs[:nr]], *[p[...] for p in refs[nr:nr + npar]], rev=rev)
        y_ref[...] = y + refs[nr + npar][...] if adds else y
        s_scr[...] = s_new

    return pl.pallas_call(
        body, name=name, grid=(n,),
        in_specs=[pl.BlockSpec((L, w), lambda s, c=c: (order(s), c)) for _, w, c in rows] + [_full_spec(p) for p in params]
        + [pl.BlockSpec((L, out_w), lambda s: (order(s), 0)) for _ in adds],
        out_specs=[pl.BlockSpec((L, out_w), lambda s: (order(s), 0)),
                   pl.BlockSpec((1,) + state_shape, lambda s: (order(s), 0, 0))],
        out_shape=[jax.ShapeDtypeStruct((n * L, out_w), F32), jax.ShapeDtypeStruct((n,) + state_shape, F32)],
        scratch_shapes=[pltpu.VMEM(state_shape, F32)],
        compiler_params=_cp(("arbitrary",)),
    )(*[a for a, _, _ in rows], *params, *adds)


def _scan_bwd(name, chunk_fn, L, n, nx, rev, rows, params, states, dy, state_shape, out_w, addends=None):
    order = _scan_order(n, nx, rev, True)
    dy_blocks = dy.shape[0] // L
    nr, npar = len(rows), len(params)
    adds = [] if addends is None else list(addends)

    def body(*refs):
        i = pl.program_id(0)
        ds_scr = refs[-1]
        rv = [r[...] for r in refs[:nr]]
        pv = [p[...] for p in refs[nr:nr + npar]]
        st_ref, dy_ref = refs[nr + npar], refs[nr + npar + 1]
        a_refs = refs[nr + npar + 2:nr + npar + 2 + len(adds)]
        o_refs = refs[nr + npar + 2 + len(adds):-1]
        p_refs = o_refs[nr:]

        @pl.when(i == 0)
        def _():
            ds_scr[...] = jnp.zeros_like(ds_scr)
            for p_ref in p_refs:
                p_ref[...] = jnp.zeros_like(p_ref)

        _, vjp = jax.vjp(functools.partial(chunk_fn, rev=rev), st_ref[0], *rv, *pv)
        dy_blk = jnp.where(order(i) < dy_blocks, dy_ref[...].astype(F32), 0.0)
        grads = vjp((ds_scr[...], dy_blk))
        ds_scr[...] = grads[0]
        for j, (o_ref, g) in enumerate(zip(o_refs[:nr], grads[1:1 + nr])):
            o_ref[...] = g + a_refs[j][...] if adds else g
        for p_ref, g in zip(p_refs, grads[1 + nr:]):
            p_ref[...] += g

    row_specs = [pl.BlockSpec((L, w), lambda s: (order(s), 0)) for _, w, _ in rows]
    res = pl.pallas_call(
        body, name=name, grid=(n,),
        in_specs=[pl.BlockSpec((L, w), lambda s, c=c: (order(s), c)) for _, w, c in rows] + [_full_spec(p) for p in params]
        + [pl.BlockSpec((1,) + state_shape, lambda s: (order(s), 0, 0)),
           pl.BlockSpec((L, out_w), lambda s: (jnp.minimum(order(s), dy_blocks - 1), 0))]
        + row_specs[:len(adds)],
        out_specs=row_specs + [_full_spec(p) for p in params],
        out_shape=[jax.ShapeDtypeStruct((n * L, w), F32) for _, w, _ in rows] + [jax.ShapeDtypeStruct(p.shape, F32) for p in params],
        scratch_shapes=[pltpu.VMEM(state_shape, F32)],
        compiler_params=_cp(("arbitrary",)),
    )(*[a for a, _, _ in rows], *params, states, dy, *adds)
    return res[:nr], res[nr:]


CONV_W = 1024
CONV_COLBLK = (0, 1, 4)


def _conv_specs(nb, src_blk):
    halo = TM // 8
    return [pl.BlockSpec((TM, CONV_W), lambda j, i: (i, src_blk(j))),
            pl.BlockSpec((8, CONV_W), lambda j, i: (jnp.maximum(i * halo - 1, 0), src_blk(j))),
            pl.BlockSpec((8, CONV_W), lambda j, i: (jnp.minimum(i * halo + halo, nb * halo - 1), src_blk(j)))]


def _conv_ext(i, nb, cur, prev, nxt):
    has_prev = jnp.logical_and(i > 0, i < nb - 1)
    has_next = i < nb - 2
    return jnp.concatenate([jnp.where(has_prev, prev, 0.0), cur, jnp.where(has_next, nxt, 0.0)], axis=0)


def _conv_taps(ext, w, flip):
    acc = None
    for j in range(SSD_K):
        wj = w[SSD_K - 1 - j:SSD_K - j, :] if flip else w[j:j + 1, :]
        term = wj * ext[6 + j:6 + j + TM, :]
        acc = term if acc is None else acc + term
    return acc


def _conv(name, src, w8, b1, nb, *, permuted_src, act, flip, out_dtype):
    src_blk = (lambda j: jnp.where(j == 2, CONV_COLBLK[2], j)) if permuted_src else (lambda j: j)

    def body(cur, prev, nxt, w_ref, b_ref, o_ref):
        ext = _conv_ext(pl.program_id(1), nb, cur[...].astype(F32), prev[...].astype(F32), nxt[...].astype(F32))
        acc = _conv_taps(ext, w_ref[...], flip)
        if act:
            acc = jax.nn.silu(acc + b_ref[...])
        o_ref[...] = acc.astype(o_ref.dtype)

    return pl.pallas_call(
        body, name=name, grid=(3, nb),
        in_specs=_conv_specs(nb, src_blk) + [pl.BlockSpec((8, CONV_W), lambda j, i: (0, j)), pl.BlockSpec((1, CONV_W), lambda j, i: (0, j))],
        out_specs=pl.BlockSpec((TM, CONV_W), lambda j, i: (i, j)),
        out_shape=jax.ShapeDtypeStruct((nb * TM, 3 * CONV_W), out_dtype),
        compiler_params=_cp(("parallel", "parallel")),
    )(src, src, src, w8, b1)


def _conv_bwd_pre(name, p1, w8, b1, dxbc_parts, nb):
    src_blk = lambda j: jnp.where(j == 2, CONV_COLBLK[2], j)
    xs_parts, bc_parts = dxbc_parts
    n_x, n_bc = len(xs_parts), len(bc_parts)
    x_blocks = [p.shape[0] // TM for p in xs_parts]

    def body(*refs):
        cur, prev, nxt, w_ref, b_ref = refs[:5]
        d_refs = refs[5:5 + n_x + n_bc]
        da_ref, dw_ref, db_ref = refs[5 + n_x + n_bc:]
        j, i = pl.program_id(0), pl.program_id(1)
        ext = _conv_ext(i, nb, cur[...], prev[...], nxt[...])
        acc = _conv_taps(ext, w_ref[...], False) + b_ref[...]
        dx = d_refs[0][...]
        for r, blocks in zip(d_refs[1:n_x], x_blocks[1:]):
            dx = dx + jnp.where(i < blocks, r[...], 0.0)
        dbc = jnp.concatenate([d_refs[n_x][...], d_refs[n_x + 1][...]], axis=1)
        dy = jnp.where(j == 2, dbc, dx)
        sg = jax.nn.sigmoid(acc)
        da = dy * (sg + acc * sg * (1.0 - sg))
        da_ref[...] = da

        @pl.when(i == 0)
        def _():
            dw_ref[...] = jnp.zeros_like(dw_ref)
            db_ref[...] = jnp.zeros_like(db_ref)

        rows = [jnp.sum(da * ext[6 + t:6 + t + TM, :], axis=0, keepdims=True) for t in range(SSD_K)]
        dw_ref[...] += jnp.concatenate(rows + [jnp.zeros((8 - SSD_K, CONV_W), F32)], axis=0)
        db_ref[...] += jnp.sum(da, axis=0, keepdims=True)

    x_specs = [pl.BlockSpec((TM, CONV_W), lambda j, i, b=b: (jnp.minimum(i, b - 1), jnp.minimum(j, 1))) for b in x_blocks]
    bc_specs = [pl.BlockSpec((TM, 512), lambda j, i: (i, 0)) for _ in bc_parts]
    return pl.pallas_call(
        body, name=name, grid=(3, nb),
        in_specs=_conv_specs(nb, src_blk) + [pl.BlockSpec((8, CONV_W), lambda j, i: (0, j)), pl.BlockSpec((1, CONV_W), lambda j, i: (0, j))]
        + x_specs + bc_specs,
        out_specs=[pl.BlockSpec((TM, CONV_W), lambda j, i: (i, j)), pl.BlockSpec((8, CONV_W), lambda j, i: (0, j)),
                   pl.BlockSpec((1, CONV_W), lambda j, i: (0, j))],
        out_shape=[jax.ShapeDtypeStruct((nb * TM, 3 * CONV_W), F32), jax.ShapeDtypeStruct((8, 3 * CONV_W), F32),
                   jax.ShapeDtypeStruct((1, 3 * CONV_W), F32)],
        compiler_params=_cp(("arbitrary", "arbitrary")),
    )(p1, p1, p1, w8, b1, *xs_parts, *bc_parts)


def _grid_block(a):
    nbv = TM // a
    blk_b = max(nbv, 8)
    return nbv, blk_b, blk_b // nbv


def _grid_spec(a, nb):
    _, blk_b, per = _grid_block(a)
    return pl.BlockSpec((a, blk_b, D), lambda i: (0, jnp.minimum(i, nb - 2) // per, 0))


def _grid_rows(v_ref, a, i):
    nbv, _, per = _grid_block(a)

    def pick(ph):
        return jnp.concatenate([v_ref[:, ph * nbv + t, :] for t in range(nbv)], axis=0)

    out = pick(0)
    for ph in range(1, per):
        out = jnp.where(i % per == ph, pick(ph), out)
    return out


def _loss_head(x, f, target, modp, g_final, rows_r):
    tview = target.reshape(rows_r, target.shape[0] // rows_r, D)
    nb = x.shape[0] // TM + 1

    def fn(x_, f_, tgt, modp_, g_):
        xn = x_ + _sel_mod(modp_, 0.0)[5:6] * f_
        err = _rms(xn) * g_ - tgt
        return 0.5 * jnp.sum(jnp.mean(err * err, axis=-1))

    def body(x_ref, f_ref, t_ref, m_ref, g_ref, l_ref, dx_ref, df_ref, dm_ref, dg_ref):
        i = pl.program_id(0)
        tgt = _grid_rows(t_ref, rows_r, i)
        l, vjp = jax.vjp(lambda a_, b_, c_, d_: fn(a_, b_, tgt, c_, d_), x_ref[...], f_ref[...], m_ref[...], g_ref[...])
        dx, df, dm, dg = vjp(jnp.ones((), F32))

        @pl.when(i == 0)
        def _():
            l_ref[...] = jnp.zeros_like(l_ref)
            dm_ref[...] = jnp.zeros_like(dm_ref)
            dg_ref[...] = jnp.zeros_like(dg_ref)

        l_ref[...] += jnp.reshape(l, (1, 1))
        dx_ref[...] = dx
        df_ref[...] = df.astype(df_ref.dtype)
        dm_ref[...] += dm
        dg_ref[...] += dg

    rowspec = pl.BlockSpec((TM, D), lambda i: (i, 0))
    return pl.pallas_call(
        body, name="loss_head", grid=(nb - 1,),
        in_specs=[rowspec, rowspec, _grid_spec(rows_r, nb), _full_spec(modp), _full_spec(g_final)],
        out_specs=[pl.BlockSpec((1, 1), lambda i: (0, 0)), rowspec, rowspec, _full_spec(modp), _full_spec(g_final)],
        out_shape=[jax.ShapeDtypeStruct((1, 1), F32), jax.ShapeDtypeStruct(x.shape, F32), jax.ShapeDtypeStruct(x.shape, MXU_DTYPE),
                   jax.ShapeDtypeStruct(modp.shape, F32), jax.ShapeDtypeStruct(g_final.shape, F32)],
        compiler_params=_cp(("arbitrary",)),
    )(x, f, tview, modp, g_final)


def _repack(name, shards, segs, wp):
    nd, kk, ws = shards.shape
    tr = 128
    used = sum(e - s for s, e in segs)

    def body(a_ref, o_ref):
        full = jnp.concatenate([a_ref[d].astype(F32) for d in range(nd)], axis=1)
        parts = [full[:, s:e] for s, e in segs]
        if wp > used:
            parts.append(jnp.zeros((tr, wp - used), F32))
        o_ref[...] = jnp.concatenate(parts, axis=1).astype(o_ref.dtype)

    return pl.pallas_call(
        body, name=name, grid=(kk // tr,),
        in_specs=[pl.BlockSpec((nd, tr, ws), lambda i: (0, i, 0))],
        out_specs=pl.BlockSpec((tr, wp), lambda i: (i, 0)),
        out_shape=jax.ShapeDtypeStruct((kk, wp), MXU_DTYPE),
        compiler_params=_cp(("parallel",)),
    )(shards)


def _unpack(name, dw, segs, ws, out_dtype):
    kk, wp = dw.shape
    tr = 128
    order = sorted(range(len(segs)), key=lambda i: segs[i][0])
    offs, o = [], 0
    for s, e in segs:
        offs.append(o)
        o += e - s

    def body(a_ref, o_ref):
        a = a_ref[...].astype(F32)
        full = jnp.concatenate([a[:, offs[i]:offs[i] + segs[i][1] - segs[i][0]] for i in order], axis=1)
        for d in range(NDEV):
            o_ref[d] = full[:, d * ws:(d + 1) * ws].astype(o_ref.dtype)

    return pl.pallas_call(
        body, name=name, grid=(kk // tr,),
        in_specs=[pl.BlockSpec((tr, wp), lambda i: (i, 0))],
        out_specs=pl.BlockSpec((NDEV, tr, ws), lambda i: (0, i, 0)),
        out_shape=jax.ShapeDtypeStruct((NDEV, kk, ws), out_dtype),
        compiler_params=_cp(("parallel",)),
    )(dw)


def _adam_math(w, g, m, v):
    m = ADAM_B1 * m + (1.0 - ADAM_B1) * g
    v = ADAM_B2 * v + (1.0 - ADAM_B2) * jnp.square(g)
    m_hat = m / (1.0 - ADAM_B1 ** ADAM_STEP)
    v_hat = v / (1.0 - ADAM_B2 ** ADAM_STEP)
    delta = -ADAM_LR * (m_hat / (jnp.sqrt(v_hat) + ADAM_EPS) + ADAM_WD * w)
    return delta, m, v


def _adam(name, w, parts, m, v, after):
    r, c = w.shape
    nsec, npart = len(parts), parts[0].shape[0]
    rs = r // nsec
    tr = _pick(rs, (256, 128, 64, 32, 16, 8)) if rs * c * 4 > (1 << 20) else rs
    tiles = rs // tr

    def body(w_ref, *refs):
        m_ref, v_ref, _, g_ref, d_ref, nm_ref, nv_ref = refs[nsec:]
        sec = pl.program_id(0) // tiles
        for a, p_ref in enumerate(refs[:nsec]):
            @pl.when(sec == a)
            def _(p_ref=p_ref):
                g = p_ref[0].astype(F32)
                for s in range(1, npart):
                    g = g + p_ref[s].astype(F32)
                delta, nm, nv = _adam_math(w_ref[...], g, m_ref[...], v_ref[...])
                g_ref[...], d_ref[...], nm_ref[...], nv_ref[...] = g, delta, nm, nv

    spec = pl.BlockSpec((tr, c), lambda i: (i, 0))
    part_specs = [pl.BlockSpec((npart, tr, c), lambda i, a=a: (0, jnp.clip(i - a * tiles, 0, tiles - 1), 0)) for a in range(nsec)]
    return pl.pallas_call(
        body, name=name, grid=(r // tr,),
        in_specs=[spec] + part_specs + [spec, spec, ANY],
        out_specs=[spec] * 4, out_shape=[jax.ShapeDtypeStruct((r, c), F32)] * 4,
        compiler_params=_cp(("parallel",)),
    )(w, *parts, m, v, after)


def _mod_fwd(c_all, mod_w):
    nl, _, ws = mod_w.shape

    def body(c_ref, w_ref, o_ref):
        o_ref[0] = _dot(jax.nn.silu(c_ref[...]), w_ref[0])

    return pl.pallas_call(
        body, name="mod_fwd", grid=(nl,),
        in_specs=[_full_spec(c_all), pl.BlockSpec((1, D, ws), lambda i: (i, 0, 0))],
        out_specs=pl.BlockSpec((1, 16, ws), lambda i: (i, 0, 0)),
        out_shape=jax.ShapeDtypeStruct((nl, 16, ws), F32),
        compiler_params=_cp(("parallel",)),
    )(c_all, mod_w)


def _mod_bwd(c_all, mod_w, dm):
    nl, _, ws = mod_w.shape

    def body(c_ref, w_ref, d_ref, dw_ref, dc_ref):
        dw_ref[0] = _dot_tn(jax.nn.silu(c_ref[...]), d_ref[0])
        dc_ref[0] = _dot_nt(d_ref[0], w_ref[0])

    return pl.pallas_call(
        body, name="mod_bwd", grid=(nl,),
        in_specs=[_full_spec(c_all), pl.BlockSpec((1, D, ws), lambda i: (i, 0, 0)), pl.BlockSpec((1, 16, ws), lambda i: (i, 0, 0))],
        out_specs=[pl.BlockSpec((1, D, ws), lambda i: (i, 0, 0)), pl.BlockSpec((1, 16, D), lambda i: (i, 0, 0))],
        out_shape=[jax.ShapeDtypeStruct((nl, D, ws), F32), jax.ShapeDtypeStruct((nl, 16, D), F32)],
        compiler_params=_cp(("parallel",)),
    )(c_all, mod_w, dm)


def _sum_parts(name, parts):
    npart, r, c = parts.shape

    def body(p_ref, o_ref):
        g = p_ref[0].astype(F32)
        for s in range(1, npart):
            g = g + p_ref[s].astype(F32)
        o_ref[...] = g

    return pl.pallas_call(body, name=name, out_shape=jax.ShapeDtypeStruct((r, c), F32), compiler_params=_cp())(parts)


MESH = pl.DeviceIdType.MESH
ANY = pl.BlockSpec(memory_space=pl.ANY)
N_PEERS = NDEV - 1


def _mesh_pos():
    return lax.axis_index("x"), lax.axis_index("y"), lax.axis_index("c")


def _slot(px, py, pc):
    return 4 * px + 2 * py + pc


def _two_level_gather(x_refs, o_refs, send_sems, recv_sems, local_sems):
    x, y, c = _mesh_pos()
    me, sibling = (x, y, c), (x, y, 1 - c)
    chips = [(1 - x, y), (x, 1 - y), (1 - x, 1 - y)]
    n = len(x_refs)

    def copy(a, k, block, to, src=None):
        dst = o_refs[a].at[_slot(*block)]
        return pltpu.make_async_remote_copy(src_ref=dst if src is None else src, dst_ref=dst, send_sem=send_sems.at[a, k],
                                            recv_sem=recv_sems.at[a, k], device_id=to, device_id_type=MESH)

    mine = [pltpu.make_async_copy(x_refs[a], o_refs[a].at[_slot(*me)], local_sems.at[a]) for a in range(n)]
    for cp in mine:
        cp.start()
    first = []
    for a in range(n):
        first.append(copy(a, 0, me, sibling, src=x_refs[a]))
        first += [copy(a, 1 + j, me, (*chip, c), src=x_refs[a]) for j, chip in enumerate(chips)]
    for cp in first:
        cp.start()
    passed = []
    for j, chip in enumerate(chips):
        for a in range(n):
            copy(a, 1 + j, (*chip, c), me).wait_recv()
            fwd = copy(a, 4 + j, (*chip, c), sibling)
            fwd.start()
            passed.append(fwd)
    for a in range(n):
        copy(a, 0, sibling, me).wait_recv()
        for j, chip in enumerate(chips):
            copy(a, 4 + j, (*chip, 1 - c), me).wait_recv()
    for cp in first + passed:
        cp.wait_send()
    for cp in mine:
        cp.wait()


def _ag_small(name, x):
    r, c = x.shape

    def body(x_ref, o_ref, send_sems, recv_sems, local_sems):
        _two_level_gather([x_ref], [o_ref], send_sems, recv_sems, local_sems)

    return pl.pallas_call(
        body, name=name, out_shape=jax.ShapeDtypeStruct((NDEV, r, c), x.dtype),
        in_specs=[pl.BlockSpec(memory_space=pltpu.VMEM)], out_specs=pl.BlockSpec(memory_space=pltpu.VMEM),
        scratch_shapes=[pltpu.SemaphoreType.DMA((1, N_PEERS)), pltpu.SemaphoreType.DMA((1, N_PEERS)), pltpu.SemaphoreType.DMA((1,))],
        compiler_params=pltpu.CompilerParams(vmem_limit_bytes=VMEM_LIMIT),
    )(x)


def _ag_big(name, shards):
    n = len(shards)

    def body(*refs):
        _two_level_gather(refs[:n], refs[n:2 * n], *refs[2 * n:])

    return pl.pallas_call(
        body, name=name, out_shape=[jax.ShapeDtypeStruct((NDEV,) + s.shape, s.dtype) for s in shards],
        in_specs=[ANY] * n, out_specs=[ANY] * n,
        scratch_shapes=[pltpu.SemaphoreType.DMA((n, N_PEERS)), pltpu.SemaphoreType.DMA((n, N_PEERS)), pltpu.SemaphoreType.DMA((n,))],
    )(*shards)


HBM = pl.BlockSpec(memory_space=pltpu.HBM)
SEM = pl.BlockSpec(memory_space=pltpu.SEMAPHORE)
EFFECT = pltpu.SideEffectType.DATAFLOW_SIDE_EFFECTING


def _peers(x, y, c):
    return [(k - 1, ((1 - x) if k & 4 else x, (1 - y) if k & 2 else y, (1 - c) if k & 1 else c)) for k in range(1, NDEV)]


def _xchg_copy(src_refs, land_refs, send_sems, recv_sems, a, k, peer, me, scatter):
    src = src_refs[a].at[_slot(*peer)] if scatter else src_refs[a]
    return pltpu.make_async_remote_copy(src_ref=src, dst_ref=land_refs[a].at[me], send_sem=send_sems.at[a * N_PEERS + k],
                                        recv_sem=recv_sems.at[a * N_PEERS + k], device_id=peer, device_id_type=MESH)


def _xchg_start(name, srcs, lands, deps, scatter):
    n, nd = len(srcs), len(deps)

    def body(*refs):
        src_refs, land_refs = refs[:n], refs[n:2 * n]
        send_sems, recv_sems, token = refs[2 * n + nd], refs[2 * n + nd + 1], refs[-1]
        x, y, c = _mesh_pos()
        me = _slot(x, y, c)
        for k, peer in _peers(x, y, c):
            for a in range(n):
                _xchg_copy(src_refs, land_refs, send_sems, recv_sems, a, k, peer, me, scatter).start()
        token[...] = jnp.zeros_like(token)

    res = pl.pallas_call(
        body, name=name,
        out_shape=(pltpu.SemaphoreType.DMA((n * N_PEERS,)), pltpu.SemaphoreType.DMA((n * N_PEERS,)),
                   *[pltpu.HBM(s.shape, s.dtype) for s in srcs], *[pltpu.HBM(s.shape, s.dtype) for s in lands],
                   jax.ShapeDtypeStruct((8, 128), F32)),
        in_specs=[HBM] * (2 * n) + [ANY] * nd,
        out_specs=(SEM, SEM, *([HBM] * (2 * n)), pl.BlockSpec(memory_space=pltpu.VMEM)),
        input_output_aliases={i: 2 + i for i in range(2 * n)},
        compiler_params=pltpu.CompilerParams(has_side_effects=EFFECT),
    )(*[pltpu.with_memory_space_constraint(s, pltpu.HBM) for s in srcs],
      *[pltpu.with_memory_space_constraint(s, pltpu.HBM) for s in lands], *deps)
    return res[0], res[1], res[2:2 + n], res[2 + n:2 + 2 * n], res[-1]


def _xchg_wait(name, send_sems, recv_sems, srcs, lands, after, scatter):
    n = len(srcs)

    def body(*refs):
        src_refs, land_refs = refs[:n], refs[n:2 * n]
        s_sems, r_sems = refs[2 * n], refs[2 * n + 1]
        x, y, c = _mesh_pos()
        me = _slot(x, y, c)
        for k, peer in _peers(x, y, c):
            for a in range(n):
                cp = _xchg_copy(src_refs, land_refs, s_sems, r_sems, a, k, peer, me, scatter)
                cp.wait_send()
                cp.wait_recv()

    res = pl.pallas_call(
        body, name=name,
        out_shape=[pltpu.HBM(s.shape, s.dtype) for s in srcs] + [pltpu.HBM(s.shape, s.dtype) for s in lands],
        in_specs=[HBM] * (2 * n) + [SEM, SEM, ANY], out_specs=[HBM] * (2 * n),
        input_output_aliases={i: i for i in range(2 * n)},
        compiler_params=pltpu.CompilerParams(has_side_effects=EFFECT),
    )(*srcs, *lands, send_sems, recv_sems, after)
    return res[n:]


def _landing(name, srcs, me, scatter):
    shapes = [s.shape[-2:] for s in srcs]

    def body(me_ref, *refs):
        for s_ref, o_ref in zip(refs[:len(srcs)], refs[len(srcs):]):
            o_ref[...] = s_ref[...].reshape(o_ref.shape)

    def slot_spec(r, c):
        return pl.BlockSpec((1, r, c), lambda i, me_ref: (me_ref[0], 0, 0))

    return pl.pallas_call(
        body, name=name, out_shape=[jax.ShapeDtypeStruct((NDEV, r, c), s.dtype) for s, (r, c) in zip(srcs, shapes)],
        grid_spec=pltpu.PrefetchScalarGridSpec(
            num_scalar_prefetch=1, grid=(1,),
            in_specs=[slot_spec(r, c) if scatter else pl.BlockSpec((r, c), lambda i, me_ref: (0, 0)) for r, c in shapes],
            out_specs=[slot_spec(r, c) for r, c in shapes]),
        compiler_params=_cp(("arbitrary",)),
    )(jnp.reshape(me, (1,)).astype(jnp.int32), *srcs)


STAGES = ("l0_mixer", "l0_ffn", "l1_mixer", "l1_ffn")
STAGE_LAYOUT = {"l0_mixer": (AB_SEGS, AB_P), "l1_mixer": (SSD_SEGS, SSD_P_W)}


class _Exchange:
    def __init__(self, shards, me):
        self.shards, self.me = shards, me
        self.pending, self.pending_grads, self.recv = {}, None, {}

    def _layout(self, stage):
        ws = self.shards[stage][0].shape[-1]
        return STAGE_LAYOUT.get(stage, (((0, NDEV * ws),), NDEV * ws)) + (ws,)

    def _start_gather(self, stage, deps):
        srcs = list(self.shards[stage])
        lands = _landing("own_" + stage, srcs, self.me, False)
        return _xchg_start("gather_start_" + stage, srcs, lands, deps, False)

    def get(self, stage, dep, thread):
        i = STAGES.index(stage)
        if i == 0:
            g_in, g_out = _ag_big("gather_" + stage, list(self.shards[stage]))
            ahead, deps = STAGES[1:3], [g_out, dep]
        else:
            ss, rs, srcs, lands, _ = self.pending.pop(stage)
            g_in, g_out = _xchg_wait("gather_wait_" + stage, ss, rs, srcs, lands, dep, False)
            ahead, deps = STAGES[i + 2:i + 3], [g_out]
        for nxt in ahead:
            self.pending[nxt] = self._start_gather(nxt, deps)
            deps = [self.pending[nxt][4]]
            thread = thread + self.pending[nxt][4][0, 0]
        segs, wp, _ = self._layout(stage)
        return _repack("repack_" + stage, g_in, segs, wp), g_out.reshape(-1, D), thread

    def put(self, stage, d_in, d_out, thread):
        segs, _, ws = self._layout(stage)
        parts = [_unpack("unpack_" + stage, d_in, segs, ws, MXU_DTYPE), d_out.reshape(NDEV, -1, D)]
        deps = [parts[0]]
        if self.pending_grads is not None:
            deps = [self.finish(parts[0])[0]]
        self.staged = (stage, parts)
        return thread if stage == STAGES[0] else thread + self.start_last(deps)[0, 0]

    def start_last(self, deps):
        stage, parts = self.staged
        lands = _landing("own_grad_" + stage, parts, self.me, True)
        self.pending_grads = (stage,) + _xchg_start("scatter_start_" + stage, parts, lands, deps, True)
        return self.pending_grads[5]

    def finish(self, after):
        stage, ss, rs, srcs, lands, _ = self.pending_grads
        self.recv[stage] = _xchg_wait("scatter_wait_" + stage, ss, rs, srcs, lands, after, True)
        self.pending_grads = None
        return self.recv[stage]


def _mm_swiglu(name, h, w_in):
    m, kk = h.shape
    f = w_in.shape[1] // 2
    tm = _pick(m, (272, 256, 128))

    def body(h_ref, wg_ref, wu_ref, pf_ref, act_ref):
        a = h_ref[...].astype(MXU_DTYPE)
        g = jnp.dot(a, wg_ref[...].astype(MXU_DTYPE), preferred_element_type=F32).astype(MXU_DTYPE)
        u = jnp.dot(a, wu_ref[...].astype(MXU_DTYPE), preferred_element_type=F32).astype(MXU_DTYPE)
        pf_ref[0] = g
        pf_ref[1] = u
        act_ref[...] = (jax.nn.silu(g.astype(F32)) * u.astype(F32)).astype(act_ref.dtype)

    return pl.pallas_call(
        body, name=name, grid=(m // tm,),
        in_specs=[pl.BlockSpec((tm, kk), lambda i: (i, 0)), pl.BlockSpec((kk, f), lambda i: (0, 0)), pl.BlockSpec((kk, f), lambda i: (0, 1))],
        out_specs=[pl.BlockSpec((2, tm, f), lambda i: (0, i, 0)), pl.BlockSpec((tm, f), lambda i: (i, 0))],
        out_shape=[jax.ShapeDtypeStruct((2, m, f), MXU_DTYPE), jax.ShapeDtypeStruct((m, f), MXU_DTYPE)],
        compiler_params=_cp(("parallel",)),
    )(h, w_in, w_in)


def _rows_gate_up(pf):
    f = pf.shape[2]
    return [([pf], [pl.BlockSpec((None, TM, f), lambda i, s=s: (s, i, 0))], lambda r: r[...].astype(F32), f) for s in (0, 1)]


def _ffn_fwd(tag, h, w_in, w_out, nb, cb):
    pf, act = _mm_swiglu(tag + "_ffn_in", h, w_in)
    return pf, act, _mm(tag + "_ffn_out", act, w_out, "nn", F32)


def _ffn_bwd(tag, h, pf, act, df, w_in, w_out, nb, cb):
    dw_out = _mm(tag + "_ffn_out_dw", act, df, "tn", MXU_DTYPE)
    dact = _mm(tag + "_ffn_out_dx", df, w_out, "nt", MXU_DTYPE)
    (dpf,), _ = _rowwise_vjp(tag + "_swiglu_bwd", _fn_swiglu, nb, cb, _rows_gate_up(pf), [], [_row(dact)], [((0, 1), MXU_DTYPE, None)])
    dw_in = _mm(tag + "_ffn_in_dw", h, dpf, "tn", MXU_DTYPE)
    dh = _mm(tag + "_ffn_in_dx", dpf, w_in, "nt", MXU_DTYPE)
    return dw_out, dw_in, dh


def _local_step(x, ctx, target, mod, P, comm):
    T = x.shape[0]
    N = T + CTX
    nb, cb = N // TM, N // TM - 1
    R = T // GRID_W
    mod0, mod1 = mod[0], mod[1]
    ng = P["norm_g"]
    g00, g01, g10, g11 = ng[0, 0][None], ng[0, 1][None], ng[1, 0][None], ng[1, 1][None]
    pre = functools.partial(_fn_prenorm, a=0, b=1)
    rpre = functools.partial(_fn_resid_prenorm, gi=2, a=3, b=4)
    res5 = functools.partial(_fn_resid, gi=5)
    dirs = (("f", False), ("b", True))

    xc0 = _row_cat(x, ctx, nb)
    w_ab_in, w_ab_out, g00 = comm.get("l0_mixer", mod, g00)
    (h0,) = _rowwise("l0_prenorm", pre, nb, cb, [xc0], [g00, mod0], [(D, MXU_DTYPE)])
    p0 = _mm("l0_in", h0, w_ab_in, "nn", F32)
    gla_rows = [(p0, 512, 0), (p0, 256, 8), (p0, 256, 9), (p0, 128, 20)]
    gla_blk = _multi_chunk(_gla_chunk, GLA_L, TM // GLA_L, len(gla_rows))
    gla_par = {d: [P["ab_gate_w"][int(r)], P["ab_gate_b"][int(r)][None]] for d, r in dirs}
    gla_state = (GLA_H * GLA_DV, GLA_H * GLA_DK)
    o, st0 = None, {}
    for d, rev in dirs:
        o, st0[d] = _scan_fwd("gla_fwd_" + d, gla_blk, TM, nb, cb, rev, gla_rows, gla_par[d], gla_state, GLA_H * GLA_DV, o)
    n128, cb128 = N // GMLP_L, T // GMLP_L
    mix_rows = [_row(o, tm=GMLP_L)] + [_row(p0, 512, j, tm=GMLP_L) for j in (1, 2, 3)]
    mix_par = [P["ab_gla_norm_g"], P["ab_vnorm_g"], P["ab_spatial_w"].reshape(GMLP_G * GMLP_L, GMLP_L), P["ab_spatial_b"].T]
    (cat0,) = _rowwise("l0_mix", _fn_mixpost, n128, cb128, mix_rows, mix_par, [(D, MXU_DTYPE)], tm=GMLP_L)
    y0 = _mm("l0_out", cat0, w_ab_out, "nn", F32)
    w_fi0, w_fo0, g01 = comm.get("l0_ffn", y0, g01)
    x1, h1 = _rowwise("l0_ffn_prenorm", rpre, nb, cb, [xc0, _row(y0)], [g01, mod0, mod0], [(D, F32), (D, MXU_DTYPE)])
    pf0, act0, f0 = _ffn_fwd("l0", h1, w_fi0, w_fo0, nb, cb)
    w_ssd_in, w_ssd_out, g10 = comm.get("l1_mixer", f0, g10)
    x2p, h2 = _rowwise("l0_resid_l1_prenorm", functools.partial(_fn_resid_prenorm, gi=5, a=0, b=1), nb, cb,
                       [_row_grid(x1, R, nb), _row_grid(f0, R, nb)], [g10, mod0, mod1], [(D, F32), (D, MXU_DTYPE)])
    p1 = _mm("l1_in", h2, w_ssd_in, "nn", F32)
    conv_w8 = jnp.concatenate([P["ssd_conv_w"], jnp.zeros((8 - SSD_K, 3 * CONV_W), F32)], axis=0)
    xbc = _conv("l1_conv", p1, conv_w8, P["ssd_conv_b"], nb, permuted_src=True, act=True, flip=False, out_dtype=F32)
    ssd_rows = [(xbc, SSD_INNER, 0), (xbc, 512, 4), (xbc, 512, 5), (p1, 128, 40)]
    ssd_blk = _multi_chunk(_ssd_chunk, SSD_L, TM // SSD_L, len(ssd_rows))
    ssd_par = {d: [P["ssd_dt_bias"][int(r)][None], P["ssd_a_log"][int(r)][None]] for d, r in dirs}
    ssd_state = (SSD_N, SSD_INNER)
    ys, st1 = None, {}
    for d, rev in dirs:
        ys, st1[d] = _scan_fwd("ssd_fwd_" + d, ssd_blk, TM, nb, cb, rev, ssd_rows, ssd_par[d], ssd_state, SSD_INNER, ys)
    fin_rows = [_row(ys), _row(xbc, SSD_INNER, 0), _row(p1, SSD_INNER, 1)]
    fin_par = [P["ssd_d"], P["ssd_norm_g"]]
    (yn,) = _rowwise("l1_finish", _fn_ssd_finish, cb, cb, fin_rows, fin_par, [(SSD_INNER, MXU_DTYPE)])
    y1 = _mm("l1_out", yn, w_ssd_out, "nn", F32)
    w_fi1, w_fo1, g11 = comm.get("l1_ffn", y1, g11)
    x3, h3 = _rowwise("l1_ffn_prenorm", rpre, cb, cb, [_row(x2p), _row(y1)], [g11, mod1, mod1], [(D, F32), (D, MXU_DTYPE)])
    pf1, act1, f1 = _ffn_fwd("l1", h3, w_fi1, w_fo1, cb, cb)
    loss, dx3, df1, dm1_j, d_final_g = _loss_head(x3, f1, target, mod1, P["final_norm_g"], R)

    dP = {"final_norm_g": d_final_g}
    dwo1, dwi1, dh3 = _ffn_bwd("l1", h3, pf1, act1, df1, w_fi1, w_fo1, cb, cb)
    g11 = comm.put("l1_ffn", dwi1, dwo1, g11)
    (dx2p_a, dy1), (dg11, dm1_a, dm1_b) = _rowwise_vjp(
        "l1_ffn_prenorm_bwd", rpre, cb, cb, [_row(x2p), _row(y1)], [g11, mod1, mod1], [_row(dx3), _row(dh3)],
        [(0, F32, None), (1, MXU_DTYPE, None)])
    d_ssd_out = _mm("l1_out_dw", yn, dy1, "tn", MXU_DTYPE)
    dyn = _mm("l1_out_dx", dy1, w_ssd_out, "nt", MXU_DTYPE)
    (dys, dxs, dz), (dP["ssd_d"], dP["ssd_norm_g"]) = _rowwise_vjp(
        "l1_finish_bwd", _fn_ssd_finish, cb, cb, fin_rows, fin_par, [_row(dyn)],
        [(0, F32, None), (1, F32, None), (2, MXU_DTYPE, None)])
    dssd, ddtb, dalog = None, [], []
    for d, rev in dirs:
        dssd, (ddtb_, dalog_) = _scan_bwd("ssd_bwd_" + d, ssd_blk, TM, nb, cb, rev, ssd_rows, ssd_par[d], st1[d], dys, ssd_state,
                                          SSD_INNER, dssd)
        ddtb.append(ddtb_); dalog.append(dalog_)
    dx_s, db_s, dc_s, dtl = dssd
    dP["ssd_dt_bias"] = jnp.concatenate(ddtb, axis=0)
    dP["ssd_a_log"] = jnp.concatenate(dalog, axis=0)
    dacc, dcw8, dP["ssd_conv_b"] = _conv_bwd_pre("l1_conv_bwd", p1, conv_w8, P["ssd_conv_b"], ([dx_s, dxs], [db_s, dc_s]), nb)
    dP["ssd_conv_w"] = dcw8[:SSD_K]
    dpc = _conv("l1_conv_dx", dacc, conv_w8, jnp.zeros((1, 3 * CONV_W), F32), nb, permuted_src=False, act=False, flip=True,
                out_dtype=MXU_DTYPE)
    cat1 = functools.partial(_fn_concat, sums=(1, 1, 1, 1), pad=SSD_P_W - 5248)
    (dp1,) = _rowwise("l1_dp", cat1, nb, cb, [_row(dpc, SSD_INNER, 0), _row(dz, valid=cb), _row(dpc, 1024, 2), _row(dtl)],
                      [], [(SSD_P_W, MXU_DTYPE)])
    g10 = comm.put("l1_mixer", _mm("l1_in_dw", h2, dp1, "tn", F32), d_ssd_out, g10)
    dh2 = _mm("l1_in_dx", dp1, w_ssd_in, "nt", MXU_DTYPE)
    (dx2p,), (dg10, dm1_f) = _rowwise_vjp("l1_prenorm_bwd", pre, nb, cb, [_row(x2p)], [g10, mod1], [_row(dh2)],
                                          [(0, F32, _row(dx2p_a, valid=cb))])

    (dx1_a, df0), (dm0_e,) = _rowwise_vjp("l0_resid_bwd", res5, nb, cb, [_row(x1), _row(f0)], [mod0],
                                          [_row_grid(dx2p, GRID_W, nb)], [(0, F32, None), (1, MXU_DTYPE, None)])
    dwo0, dwi0, dh1 = _ffn_bwd("l0", h1, pf0, act0, df0, w_fi0, w_fo0, nb, cb)
    g01 = comm.put("l0_ffn", dwi0, dwo0, g01)
    (dxc0_a, dy0), (dg01, dm0_a, dm0_b) = _rowwise_vjp(
        "l0_ffn_prenorm_bwd", rpre, nb, cb, [xc0, _row(y0)], [g01, mod0, mod0], [_row(dx1_a), _row(dh1)],
        [(0, F32, None), (1, MXU_DTYPE, None)])
    d_ab_out = _mm("l0_out_dw", cat0, dy0, "tn", MXU_DTYPE)
    dcat0 = _mm("l0_out_dx", dy0, w_ab_out, "nt", MXU_DTYPE)
    (do, dr, du, dgm), (dP["ab_gla_norm_g"], dP["ab_vnorm_g"], dsw, dsb_t) = _rowwise_vjp(
        "l0_mix_bwd", _fn_mixpost, n128, cb128, mix_rows, mix_par, [_row(dcat0, tm=GMLP_L)],
        [(0, F32, None), (1, MXU_DTYPE, None), (2, MXU_DTYPE, None), (3, MXU_DTYPE, None)], tm=GMLP_L)
    dP["ab_spatial_w"] = dsw.reshape(GMLP_G, GMLP_L, GMLP_L)
    dP["ab_spatial_b"] = dsb_t.T
    gl, dgw, dgb = None, [], []
    for d, rev in dirs:
        gl, (dgw_, dgb_) = _scan_bwd("gla_bwd_" + d, gla_blk, TM, nb, cb, rev, gla_rows, gla_par[d], st0[d], do,
                                     gla_state, GLA_H * GLA_DV, gl)
        dgw.append(dgw_[None]); dgb.append(dgb_)
    dP["ab_gate_w"] = jnp.concatenate(dgw, axis=0)
    dP["ab_gate_b"] = jnp.concatenate(dgb, axis=0)
    cat0f = functools.partial(_fn_concat, sums=(1,) * 7, pad=AB_P - 2688)
    (dp0,) = _rowwise("l0_dp", cat0f, nb, cb, [_row(gl[0]), _row(dr), _row(du), _row(dgm), _row(gl[1]), _row(gl[2]), _row(gl[3])],
                      [], [(AB_P, MXU_DTYPE)])
    g00 = comm.put("l0_mixer", _mm("l0_in_dw", h0, dp0, "tn", F32), d_ab_out, g00)
    dh0 = _mm("l0_in_dx", dp0, w_ab_in, "nt", MXU_DTYPE)
    (grad_x,), (dg00, dm0_s) = _rowwise_vjp("l0_prenorm_bwd", pre, nb, cb, [xc0], [g00, mod0], [_row(dh0)],
                                            [(0, F32, _row(dxc0_a))], x_rows_only=True)
    dP["norm_g"] = jnp.concatenate([dg00, dg01, dg10, dg11], axis=0).reshape(2, 2, D)
    dmod = jnp.stack([dm0_s + dm0_a + dm0_b + dm0_e, dm1_f + dm1_a + dm1_b + dm1_j])
    return loss, grad_x, dmod, dP


WEIGHTS = ("c_ctx", "mod_w", "mod_b", "norm_g", "ffn_w_in", "ffn_w_out", "ab_w_in", "ab_gate_w", "ab_gate_b", "ab_gla_norm_g",
           "ab_vnorm_g", "ab_spatial_w", "ab_spatial_b", "ab_w_out", "ssd_w_in", "ssd_conv_w", "ssd_conv_b", "ssd_dt_bias",
           "ssd_a_log", "ssd_d", "ssd_norm_g", "ssd_w_out", "final_norm_g")
SMALL_SHARDED = ("norm_g", "ab_gate_w", "ab_gate_b", "ssd_conv_w", "ssd_conv_b", "ssd_norm_g")
SMALL = ("c_ctx", "mod_b", "norm_g", "ab_gate_w", "ab_gate_b", "ab_gla_norm_g", "ab_vnorm_g", "ab_spatial_w", "ab_spatial_b",
         "ssd_conv_w", "ssd_conv_b", "ssd_dt_bias", "ssd_a_log", "ssd_d", "ssd_norm_g", "final_norm_g")
LANES = 1024


def _pack(arrs, rows_multiple=8):
    flat = jnp.concatenate([a.reshape(-1).astype(F32) for a in arrs])
    rows = -(-flat.shape[0] // LANES)
    rows = -(-rows // rows_multiple) * rows_multiple
    return jnp.pad(flat, (0, rows * LANES - flat.shape[0])).reshape(rows, LANES)


def _unpack_flat(buf, shapes):
    lead = buf.shape[:-2]
    flat = buf.reshape(lead + (-1,))
    out, o = [], 0
    for s in shapes:
        n = math.prod(s)
        out.append(flat[..., o:o + n].reshape(lead + tuple(s)))
        o += n
    return out


def _unshard(g):
    g = jnp.moveaxis(g, 0, -2)
    return g.reshape(g.shape[:-2] + (g.shape[-2] * g.shape[-1],))


def _my_shard(full, me, ws):
    return lax.dynamic_slice_in_dim(full, me * ws, ws, axis=full.ndim - 1)


def _silu_vjp(cvec, dsc):
    def body(c_ref, d_ref, o_ref):
        _, vjp = jax.vjp(jax.nn.silu, c_ref[...])
        o_ref[...] = vjp(d_ref[...])[0]

    return pl.pallas_call(body, name="c_ctx_bwd", out_shape=jax.ShapeDtypeStruct(cvec.shape, F32), compiler_params=_cp())(cvec, dsc)


def kernel(x, c, ctx, c_ctx, mod_w, mod_b, norm_g, ffn_w_in, ffn_w_out, ab_w_in, ab_gate_w, ab_gate_b, ab_gla_norm_g, ab_vnorm_g, ab_spatial_w, ab_spatial_b, ab_w_out, ssd_w_in, ssd_conv_w, ssd_conv_b, ssd_dt_bias, ssd_a_log, ssd_d, ssd_norm_g, ssd_w_out, final_norm_g, loss_target, m_c_ctx, m_mod_w, m_mod_b, m_norm_g, m_ffn_w_in, m_ffn_w_out, m_ab_w_in, m_ab_gate_w, m_ab_gate_b, m_ab_gla_norm_g, m_ab_vnorm_g, m_ab_spatial_w, m_ab_spatial_b, m_ab_w_out, m_ssd_w_in, m_ssd_conv_w, m_ssd_conv_b, m_ssd_dt_bias, m_ssd_a_log, m_ssd_d, m_ssd_norm_g, m_ssd_w_out, m_final_norm_g, v_c_ctx, v_mod_w, v_mod_b, v_norm_g, v_ffn_w_in, v_ffn_w_out, v_ab_w_in, v_ab_gate_w, v_ab_gate_b, v_ab_gla_norm_g, v_ab_vnorm_g, v_ab_spatial_w, v_ab_spatial_b, v_ab_w_out, v_ssd_w_in, v_ssd_conv_w, v_ssd_conv_b, v_ssd_dt_bias, v_ssd_a_log, v_ssd_d, v_ssd_norm_g, v_ssd_w_out, v_final_norm_g):
    a = dict(locals())
    me = _slot(*_mesh_pos())
    ws_mod = mod_w.shape[-1]

    fwd_small = [c] + [a[k] for k in SMALL_SHARDED]
    g_small = _ag_small("gather_small", _pack(fwd_small))
    parts = _unpack_flat(g_small, [t.shape for t in fwd_small])
    c_rows = parts[0].reshape(NDEV, D)
    full = {k: _unshard(p) for k, p in zip(SMALL_SHARDED, parts[1:])}
    c_all = jnp.concatenate([c_rows, c_ctx[None], jnp.zeros((7, D), F32)], axis=0)
    m_all = _ag_small("gather_mod", _mod_fwd(c_all, mod_w).reshape(2 * 16, ws_mod)).reshape(NDEV, 2, 16, ws_mod)
    m_mine = lax.dynamic_index_in_dim(m_all, me, axis=2, keepdims=False)
    mx = jnp.moveaxis(m_mine, 0, 1).reshape(2, N_MOD, D) + mod_b.reshape(2, N_MOD, D)
    mc = jnp.moveaxis(m_all[:, :, 8, :], 0, 1).reshape(2, N_MOD, D) + mod_b.reshape(2, N_MOD, D)
    pad2 = jnp.zeros((2, 2, D), F32)
    mod = jnp.concatenate([mx, pad2, mc, pad2], axis=1)

    big = {"l0_mixer": (ab_w_in[0], ab_w_out[0]), "l0_ffn": (ffn_w_in[0], ffn_w_out[0]),
           "l1_mixer": (ssd_w_in[0], ssd_w_out[0]), "l1_ffn": (ffn_w_in[1], ffn_w_out[1])}
    comm = _Exchange({k: tuple(w.astype(MXU_DTYPE) for w in v) for k, v in big.items()}, me)
    P = {
        "norm_g": full["norm_g"], "ab_gate_w": full["ab_gate_w"][0], "ab_gate_b": full["ab_gate_b"][0],
        "ab_gla_norm_g": ab_gla_norm_g, "ab_vnorm_g": ab_vnorm_g, "ab_spatial_w": ab_spatial_w[0], "ab_spatial_b": ab_spatial_b[0],
        "ssd_conv_w": full["ssd_conv_w"][0], "ssd_conv_b": full["ssd_conv_b"], "ssd_dt_bias": ssd_dt_bias[0],
        "ssd_a_log": ssd_a_log[0], "ssd_d": ssd_d, "ssd_norm_g": full["ssd_norm_g"], "final_norm_g": final_norm_g[None],
    }

    loss, grad_x, dmod, dP = _local_step(x[0], ctx[0], loss_target[0], mod, P, comm)

    dmx, dmc = dmod[:, 0:N_MOD].reshape(2, N_MOD * D), dmod[:, 8:8 + N_MOD].reshape(2, N_MOD * D)
    small_names = ("ab_gate_w", "ab_gate_b", "ab_gla_norm_g", "ab_vnorm_g", "ab_spatial_w", "ab_spatial_b", "norm_g", "ssd_conv_w",
                   "ssd_conv_b", "ssd_dt_bias", "ssd_a_log", "ssd_d", "ssd_norm_g", "final_norm_g")
    bwd_small = [dP[k] for k in small_names] + [dmc, dmx]
    shapes = [t.shape for t in bwd_small]
    g_bwd = _ag_small("gather_small_grads", _pack(bwd_small))
    summed = _unpack_flat(_sum_parts("sum_small_grads", g_bwd), shapes)
    gfull = dict(zip(small_names, summed[:-2]))
    dmc_sum, dmx_sum = summed[-2], summed[-1]
    dmx_all = _unpack_flat(g_bwd, shapes)[-1]
    dmx_sh = jnp.moveaxis(_my_shard(dmx_all, me, ws_mod), 0, 1)
    dm = jnp.concatenate([dmx_sh, _my_shard(dmc_sum, me, ws_mod)[:, None, :], jnp.zeros((2, 7, ws_mod), F32)], axis=1)
    d_mod_w, dsc = _mod_bwd(c_all, mod_w, dm)
    dsc_ctx = (dsc[0, 8] + dsc[1, 8])[None]
    dsc_all = _ag_small("gather_c_ctx_grad", jnp.concatenate([dsc_ctx, jnp.zeros((7, D), F32)], axis=0))
    d_c_ctx = _silu_vjp(c_ctx[None], _sum_parts("sum_c_ctx_grad", dsc_all)[0:1])[0]

    g_small_w = {
        "c_ctx": d_c_ctx, "mod_b": dmx_sum + dmc_sum, "norm_g": gfull["norm_g"], "ab_gate_w": gfull["ab_gate_w"][None],
        "ab_gate_b": gfull["ab_gate_b"][None], "ab_gla_norm_g": gfull["ab_gla_norm_g"], "ab_vnorm_g": gfull["ab_vnorm_g"],
        "ab_spatial_w": gfull["ab_spatial_w"][None], "ab_spatial_b": gfull["ab_spatial_b"][None], "ssd_conv_w": gfull["ssd_conv_w"][None],
        "ssd_conv_b": gfull["ssd_conv_b"], "ssd_dt_bias": gfull["ssd_dt_bias"][None], "ssd_a_log": gfull["ssd_a_log"][None],
        "ssd_d": gfull["ssd_d"], "ssd_norm_g": gfull["ssd_norm_g"], "final_norm_g": gfull["final_norm_g"][0],
    }
    for k in SMALL_SHARDED:
        g_small_w[k] = _my_shard(g_small_w[k], me, a[k].shape[-1])
    token = comm.start_last([d_c_ctx])
    res = _adam("adam_small", _pack([a[k] for k in SMALL]), [_pack([g_small_w[k] for k in SMALL])[None]],
                _pack([a["m_" + k] for k in SMALL]), _pack([a["v_" + k] for k in SMALL]), token)
    out = {k: vals for k, vals in zip(SMALL, zip(*[_unpack_flat(r, [a[k].shape for k in SMALL]) for r in res]))}

    def adam_big(name, w2d, parts, m2d, v2d, shape):
        return tuple(r.reshape(shape) for r in _adam(name, w2d, parts, m2d, v2d, token))

    def flat2(t):
        return t.reshape(-1, t.shape[-1])

    out["mod_w"] = adam_big("adam_mod_w", flat2(mod_w), [d_mod_w.reshape(1, -1, ws_mod)], flat2(m_mod_w), flat2(v_mod_w), mod_w.shape)

    for j, k in enumerate(("ffn_w_in", "ffn_w_out")):
        out[k] = adam_big("adam_" + k, flat2(a[k]), [comm.recv["l0_ffn"][j], comm.recv["l1_ffn"][j]], flat2(a["m_" + k]),
                          flat2(a["v_" + k]), a[k].shape)
    for j, k in enumerate(("ssd_w_in", "ssd_w_out")):
        out[k] = adam_big("adam_" + k, a[k][0], [comm.recv["l1_mixer"][j]], a["m_" + k][0], a["v_" + k][0], a[k].shape)
    recv_ab = comm.finish(out["ssd_w_out"][3])
    for j, k in enumerate(("ab_w_in", "ab_w_out")):
        out[k] = adam_big("adam_" + k, a[k][0], [recv_ab[j]], a["m_" + k][0], a["v_" + k][0], a[k].shape)

    loss_all = lax.psum(loss[0, 0], ("x", "y", "c"))
    return (loss_all, grad_x[None], *[out[k][0] for k in WEIGHTS], *[out[k][1] for k in WEIGHTS],
            *[out[k][2] for k in WEIGHTS], *[out[k][3] for k in WEIGHTS])
```

```python
import functools
import math

import jax
import jax.numpy as jnp
from jax import lax
from jax.experimental import pallas as pl
from jax.experimental.pallas import tpu as pltpu

F32 = jnp.float32
BF16 = jnp.bfloat16
MXU_DTYPE = jnp.bfloat16

D = 1024
NDEV = 8
N_MOD = 6
EPS = 1e-6
GRID_W = 64
CTX = 256
TM = 256
D_FF = 2816
GLA_H, GLA_DK, GLA_DV, GLA_LR, GLA_TAU, GLA_L = 4, 64, 128, 16, 16.0, 64
GMLP_G, GMLP_C, GMLP_L = 4, 128, 128
SSD_H, SSD_P, SSD_G, SSD_N, SSD_L, SSD_K = 32, 64, 4, 128, 128, 5
SSD_INNER = SSD_H * SSD_P
AB_IN = 2592
SSD_IN = 5184
AB_SEGS = ((256, 768), (1056, 1568), (1568, 2080), (2080, 2592), (0, 256), (800, 1056), (768, 800))
AB_P = 2816
SSD_SEGS = ((0, 2048), (3136, 5184), (2048, 2560), (2560, 3072), (3072, 3136))
SSD_P_W = 5376
VMEM_LIMIT = 56 * 1024 * 1024

ADAM_LR, ADAM_B1, ADAM_B2, ADAM_EPS, ADAM_WD, ADAM_STEP = 0.001, 0.9, 0.999, 1e-08, 0.01, 10


def _cp(sem=None, **kw):
    return pltpu.CompilerParams(dimension_semantics=sem, vmem_limit_bytes=VMEM_LIMIT, **kw)


def _dot(a, b, dims=(((1,), (0,)), ((), ()))):
    return lax.dot_general(a.astype(MXU_DTYPE), b.astype(MXU_DTYPE), dims, preferred_element_type=F32)


def _dot_nt(a, b):
    return _dot(a, b, (((1,), (1,)), ((), ())))


def _dot_tn(a, b):
    return _dot(a, b, (((0,), (0,)), ((), ())))


def _rms(x):
    return x * lax.rsqrt(jnp.mean(x * x, axis=-1, keepdims=True) + EPS)


def _pick(n, prefs):
    for p in prefs:
        if n % p == 0:
            return p
    return n


def _row(arr, width=None, colblk=0, tm=TM, valid=None):
    width = arr.shape[1] if width is None else width
    if valid is None:
        return ([arr], [pl.BlockSpec((tm, width), lambda i, c=colblk: (i, c))], lambda r: r[...].astype(F32), width)
    spec = pl.BlockSpec((tm, width), lambda i, c=colblk: (jnp.minimum(i, valid - 1), c))
    return ([arr], [spec], lambda r: jnp.where(pl.program_id(0) < valid, r[...].astype(F32), 0.0), width)


def _row_grid(arr, a, nb):
    n = arr.shape[0]
    b = (n - CTX) // a

    def load(v_ref, c_ref):
        i = pl.program_id(0)
        return jnp.where(i == nb - 1, c_ref[...], _grid_rows(v_ref, a, i))

    return ([arr.reshape(n // b, b, D), arr], [_grid_spec(a, nb), pl.BlockSpec((TM, D), lambda i: (nb - 1, 0))], load, D)


def _row_cat(x, ctx, nb):
    return ([x, ctx], [pl.BlockSpec((TM, D), lambda i: (jnp.minimum(i, nb - 2), 0)), pl.BlockSpec((TM, D), lambda i: (0, 0))],
            lambda x_ref, c_ref: jnp.where(pl.program_id(0) == nb - 1, c_ref[...], x_ref[...]), D)


def _operands(rows):
    return [a for r in rows for a in r[0]], [s for r in rows for s in r[1]]


def _load_rows(refs, rows):
    vals, k = [], 0
    for r in rows:
        vals.append(r[2](*refs[k:k + len(r[0])]))
        k += len(r[0])
    return vals


def _full_spec(p):
    nd = p.ndim
    return pl.BlockSpec(p.shape, lambda i, nd=nd: (0,) * nd)


def _rowwise(name, fn, n_blocks, ctx_blk, rows, params, outs, tm=TM):
    arrs, specs = _operands(rows)
    nr, npar = len(arrs), len(params)

    def body(*refs):
        t = (pl.program_id(0) >= ctx_blk).astype(F32)
        rv = _load_rows(refs[:nr], rows)
        pv = [p[...] for p in refs[nr:nr + npar]]
        res = fn(t, rv, pv)
        for o_ref, o in zip(refs[nr + npar:], res):
            o_ref[...] = o.astype(o_ref.dtype)

    return pl.pallas_call(
        body, name=name, grid=(n_blocks,),
        in_specs=specs + [_full_spec(p) for p in params],
        out_specs=[pl.BlockSpec((tm, w), lambda i: (i, 0)) for w, _ in outs],
        out_shape=[jax.ShapeDtypeStruct((n_blocks * tm, w), dt) for w, dt in outs],
        compiler_params=_cp(("parallel",)),
    )(*arrs, *params)


def _rowwise_vjp(name, fn, n_blocks, ctx_blk, rows, params, douts, row_grads, tm=TM, x_rows_only=False):
    out_blocks = n_blocks - 1 if x_rows_only else n_blocks
    adds = [a for _, _, a in row_grads if a is not None]
    (r_arrs, r_specs), (d_arrs, d_specs), (a_arrs, a_specs) = _operands(rows), _operands(douts), _operands(adds)
    nr, npar, nd, na = len(r_arrs), len(params), len(d_arrs), len(a_arrs)

    def body(*refs):
        i = pl.program_id(0)
        t = (i >= ctx_blk).astype(F32)
        rv = _load_rows(refs[:nr], rows)
        pv = [p[...] for p in refs[nr:nr + npar]]
        dv = _load_rows(refs[nr + npar:nr + npar + nd], douts)
        av = _load_rows(refs[nr + npar + nd:nr + npar + nd + na], adds)
        o_refs = refs[nr + npar + nd + na:]
        _, vjp = jax.vjp(lambda r, p: tuple(fn(t, r, p)), rv, pv)
        d_rows, d_params = vjp(tuple(dv))
        ai, grads = 0, []
        for ri, _, addend in row_grads:
            g = jnp.concatenate([d_rows[r] for r in ri], axis=1) if isinstance(ri, tuple) else d_rows[ri]
            if addend is not None:
                g = g + av[ai]
                ai += 1
            grads.append(g)

        @pl.when(i < out_blocks)
        def _():
            for o_ref, g in zip(o_refs, grads):
                o_ref[...] = g.astype(o_ref.dtype)

        p_refs = o_refs[len(row_grads):]

        @pl.when(i == 0)
        def _():
            for p_ref in p_refs:
                p_ref[...] = jnp.zeros_like(p_ref)

        for p_ref, g in zip(p_refs, d_params):
            p_ref[...] += g

    widths = [sum(rows[r][3] for r in ri) if isinstance(ri, tuple) else rows[ri][3] for ri, _, _ in row_grads]
    res = pl.pallas_call(
        body, name=name, grid=(n_blocks,),
        in_specs=r_specs + [_full_spec(p) for p in params] + d_specs + a_specs,
        out_specs=[pl.BlockSpec((tm, w), lambda i: (jnp.minimum(i, out_blocks - 1), 0)) for w in widths] + [_full_spec(p) for p in params],
        out_shape=[jax.ShapeDtypeStruct((out_blocks * tm, w), dt) for w, (_, dt, _) in zip(widths, row_grads)]
        + [jax.ShapeDtypeStruct(p.shape, F32) for p in params],
        compiler_params=_cp(("arbitrary",)),
    )(*r_arrs, *params, *d_arrs, *a_arrs)
    return res[:len(row_grads)], res[len(row_grads):]


def _mm(name, a, b, mode, out_dtype):
    if mode == "nn":
        m, kk = a.shape
        n = b.shape[1]
    elif mode == "nt":
        m, kk = a.shape
        n = b.shape[0]
    else:
        kk, m = a.shape
        n = b.shape[1]
    if mode == "tn":
        tm = _pick(m, (1024, 1408, 512, 256, 128))
        tn = _pick(n, (768, 512, 256, 128))
        tk = kk
    else:
        tm = _pick(m, (1088, 1024, 768, 512, 384, 256, 128))
        tn = n if n <= 2816 else _pick(n, (1024, 768, 512, 256, 128))
        tk = kk if kk <= 2816 else _pick(kk, (2816, 1792, 1024, 768, 512, 256, 128))
    nk = kk // tk
    in_place = out_dtype == F32
    if mode == "nn":
        specs = [pl.BlockSpec((tm, tk), lambda i, j, k: (i, k)), pl.BlockSpec((tk, tn), lambda i, j, k: (k, j))]
        dims = (((1,), (0,)), ((), ()))
    elif mode == "nt":
        specs = [pl.BlockSpec((tm, tk), lambda i, j, k: (i, k)), pl.BlockSpec((tn, tk), lambda i, j, k: (j, k))]
        dims = (((1,), (1,)), ((), ()))
    else:
        specs = [pl.BlockSpec((tk, tm), lambda i, j, k: (k, i)), pl.BlockSpec((tk, tn), lambda i, j, k: (k, j))]
        dims = (((0,), (0,)), ((), ()))

    def body(a_ref, b_ref, o_ref, *scratch):
        part = lax.dot_general(a_ref[...].astype(MXU_DTYPE), b_ref[...].astype(MXU_DTYPE), dims, preferred_element_type=F32)
        if nk == 1:
            o_ref[...] = part.astype(o_ref.dtype)
        else:
            k = pl.program_id(2)
            acc = o_ref if in_place else scratch[0]

            @pl.when(k == 0)
            def _():
                acc[...] = part

            @pl.when(k > 0)
            def _():
                acc[...] += part

            if not in_place:
                @pl.when(k == nk - 1)
                def _():
                    o_ref[...] = acc[...].astype(o_ref.dtype)

    return pl.pallas_call(
        body, name=name, grid=(m // tm, n // tn, nk), in_specs=specs,
        out_specs=pl.BlockSpec((tm, tn), lambda i, j, k: (i, j)),
        out_shape=jax.ShapeDtypeStruct((m, n), out_dtype),
        scratch_shapes=[] if nk == 1 or in_place else [pltpu.VMEM((tm, tn), F32)],
        compiler_params=_cp(("parallel", "parallel", "arbitrary")),
    )(a, b)


def _sel_mod(modp, t):
    return modp[0:8] * (1.0 - t) + modp[8:16] * t


def _fn_prenorm(t, rows, params, *, a, b):
    (x,), (g, modp) = rows, params
    m = _sel_mod(modp, t)
    return ((_rms(x) * g) * (1.0 + m[b:b + 1]) + m[a:a + 1],)


def _fn_resid_prenorm(t, rows, params, *, gi, a, b):
    (x, y), (g, mod_a, mod_b) = rows, params
    ma, mb = _sel_mod(mod_a, t), _sel_mod(mod_b, t)
    xn = x + ma[gi:gi + 1] * y
    return xn, (_rms(xn) * g) * (1.0 + mb[b:b + 1]) + mb[a:a + 1]


def _fn_resid(t, rows, params, *, gi):
    (x, y), (mod_a,) = rows, params
    return (x + _sel_mod(mod_a, t)[gi:gi + 1] * y,)


def _fn_mixpost(t, rows, params):
    (o, r, u, g), (gla_g, vn_g, sw, sb_t) = rows, params
    a =jnp.concatenate([_rms(o[:, h * GLA_DV:(h + 1) * GLA_DV]) for h in range(GLA_H)], axis=1) * gla_g * jax.nn.silu(r)
    uu, vv = jax.nn.gelu(u), jax.nn.gelu(g)
    mu = jnp.mean(vv, axis=-1, keepdims=True)
    var = jnp.mean(jnp.square(vv - mu), axis=-1, keepdims=True)
    vn = ((vv - mu) * lax.rsqrt(var + EPS)) * vn_g
    s = jnp.concatenate(
        [_dot(sw[gi * GMLP_L:(gi + 1) * GMLP_L, :], vn[:, gi * GMLP_C:(gi + 1) * GMLP_C]) + sb_t[:, gi:gi + 1]
         for gi in range(GMLP_G)], axis=1)
    return (jnp.concatenate([a, uu * s], axis=1),)


def _expand_heads(row):
    first = lax.broadcasted_iota(jnp.int32, (1, 2 * SSD_P), 1) < SSD_P
    return jnp.concatenate([jnp.where(first, row[:, 2 * j:2 * j + 1], row[:, 2 * j + 1:2 * j + 2]) for j in range(SSD_H // 2)], axis=1)


def _fn_ssd_finish(t, rows, params):
    (y2, xs, z), (d_skip, norm_g) = rows, params
    d_full = _expand_heads(d_skip)
    y = (y2 + d_full * xs) * jax.nn.silu(z)
    gw = SSD_INNER // SSD_G
    return (jnp.concatenate([_rms(y[:, gi * gw:(gi + 1) * gw]) for gi in range(SSD_G)], axis=1) * norm_g,)


def _fn_concat(t, rows, params, *, sums, pad=0):
    out, i = [], 0
    for n in sums:
        acc = rows[i]
        for j in range(1, n):
            acc = acc + rows[i + j]
        out.append(acc)
        i += n
    if pad:
        out.append(jnp.zeros((out[0].shape[0], pad), F32))
    return (jnp.concatenate(out, axis=1),)


def _tri(n, rev):
    r = lax.broadcasted_iota(jnp.int32, (n, n), 0)
    c = lax.broadcasted_iota(jnp.int32, (n, n), 1)
    return (r <= c) if rev else (r >= c)


def _running_sum(x, rev):
    n, s = x.shape[0], 1
    while s < n:
        z = jnp.zeros((s, x.shape[1]), x.dtype)
        x = x + (jnp.concatenate([x[s:], z], axis=0) if rev else jnp.concatenate([z, x[:n - s]], axis=0))
        s *= 2
    return x


def _gla_chunk(S, v, k, q, tail, gw, gb, *, rev):
    L, H = GLA_L, GLA_H
    lr = tail[:, GLA_LR:2 * GLA_LR] if rev else tail[:, 0:GLA_LR]
    la = jax.nn.log_sigmoid(_dot(lr, gw) + gb) / GLA_TAU
    b = _running_sum(la, rev)
    b_last = b[0:1] if rev else b[L - 1:L]
    kd = k * jnp.exp(b_last - b)
    qd = (q * GLA_DK ** -0.5) * jnp.exp(b)
    ki = k * jnp.exp(-b)

    def same_head(shape, rows_per_head, cols_per_head):
        r = lax.broadcasted_iota(jnp.int32, shape, 0) // rows_per_head
        c = lax.broadcasted_iota(jnp.int32, shape, 1) // cols_per_head
        return r == c

    k_blk = jnp.where(same_head((H * L, H * GLA_DK), L, GLA_DK), jnp.concatenate([ki] * H, axis=0), 0.0)
    v_blk = jnp.where(same_head((H * L, H * GLA_DV), L, GLA_DV), jnp.concatenate([v] * H, axis=0), 0.0)
    row = lax.broadcasted_iota(jnp.int32, (L, H * L), 0)
    src = lax.broadcasted_iota(jnp.int32, (L, H * L), 1) % L
    sc = jnp.where((row <= src) if rev else (row >= src), _dot_nt(qd, k_blk), 0.0)
    o = _dot_nt(qd, S) + _dot(sc, v_blk)
    s_new = S * jnp.exp(b_last) + jnp.where(same_head(S.shape, GLA_DV, GLA_DK), _dot_tn(v, kd), 0.0)
    return s_new, o


def _ssd_chunk(S, x, bm, cm, tail, dtb, alog, *, rev):
    L = SSD_L
    msk = _tri(L, rev)
    raw = tail[:, SSD_H:2 * SSD_H] if rev else tail[:, 0:SSD_H]
    dt = jax.nn.softplus(raw + dtb)
    acum = _running_sum(dt * (-jnp.exp(alog)), rev)
    a_last = acum[0:1] if rev else acum[L - 1:L]
    wst = dt * jnp.exp(a_last - acum)
    eac = jnp.exp(acum)
    dec = jnp.exp(a_last)
    tr = jnp.concatenate([acum, dt, wst, jnp.zeros((L, L - 3 * SSD_H), F32)], axis=1).T
    acum_t, dt_t, wst_t = tr[0:SSD_H], tr[SSD_H:2 * SSD_H], tr[2 * SSD_H:3 * SSD_H]
    lane = lax.broadcasted_iota(jnp.int32, (1, 2 * SSD_P), 1)
    m0 = (lane < SSD_P).astype(F32)
    m1 = 1.0 - m0
    pairs_per_group = SSD_H // SSD_G // 2
    y_parts, s_parts = [], []
    for g in range(SSD_G):
        ns = slice(g * SSD_N, (g + 1) * SSD_N)
        bg, cg = bm[:, ns], cm[:, ns]
        cb = _dot_nt(cg, bg)
        bgt = bg.T
        gs = slice(g * pairs_per_group * 2 * SSD_P, (g + 1) * pairs_per_group * 2 * SSD_P)
        y_carry = _dot(cg, S[:, gs])
        for jj in range(pairs_per_group):
            j = g * pairs_per_group + jj
            ls = slice(j * 2 * SSD_P, (j + 1) * 2 * SSD_P)
            xp, sp = x[:, ls], S[:, ls]
            xm = jnp.concatenate([xp * m0, xp * m1], axis=0)
            lhs, bw = [], []
            for h in (2 * j, 2 * j + 1):
                seg = acum[:, h:h + 1] - acum_t[h:h + 1, :]
                lhs.append(cb * jnp.exp(jnp.where(msk, seg, -jnp.inf)) * dt_t[h:h + 1, :])
                bw.append(bgt * wst_t[h:h + 1, :])
            e_pair = eac[:, 2 * j:2 * j + 1] * m0 + eac[:, 2 * j + 1:2 * j + 2] * m1
            y_parts.append(_dot(jnp.concatenate(lhs, axis=1), xm) + y_carry[:, jj * 2 * SSD_P:(jj + 1) * 2 * SSD_P] * e_pair)
            d_pair = dec[:, 2 * j:2 * j + 1] * m0 + dec[:, 2 * j + 1:2 * j + 2] * m1
            s_parts.append(sp * d_pair + _dot(jnp.concatenate(bw, axis=1), xm))
    return jnp.concatenate(s_parts, axis=1), jnp.concatenate(y_parts, axis=1)


def _multi_chunk(chunk_fn, L, subs, nr):
    def fn(S, *args, rev):
        rows, params = args[:nr], args[nr:]
        ys = [None] * subs
        for j in (range(subs - 1, -1, -1) if rev else range(subs)):
            S, ys[j] = chunk_fn(S, *[r[j * L:(j + 1) * L] for r in rows], *params, rev=rev)
        return S, jnp.concatenate(ys, axis=0)

    return fn


def _scan_order(n, nx, rev, backward):
    nc = n - nx

    def fwd(s):
        return (n - 1 - s) if rev else jnp.where(s < nc, s + nx, s - nc)

    return (lambda s: fwd(n - 1 - s)) if backward else fwd


def _scan_fwd(name, chunk_fn, L, n, nx, rev, rows, params, state_shape, out_w, addend=None):
    order = _scan_order(n, nx, rev, False)
    nr, npar = len(rows), len(params)
    adds = [] if addend is None else [addend]

    def body(*refs):
        s_scr = refs[-1]

        @pl.when(pl.program_id(0) == 0)
        def _():
            s_scr[...] = jnp.zeros_like(s_scr)

        s_in = s_scr[...]
        y_ref, st_ref = refs[nr + npar + len(adds)], refs[nr + npar + len(adds) + 1]
        st_ref[0] = s_in
        s_new, y = chunk_fn(s_in, *[r[...] for r in refs[:nr]], *[p[...] for p in refs[nr:nr + npar]], rev=rev)
        y_ref[...] = y + refs[nr + npar][...] if adds else y
        s_scr[...] = s_new

    return pl.pallas_call(
        body, name=name, grid=(n,),
        in_specs=[pl.BlockSpec((L, w), lambda s, c=c: (order(s), c)) for _, w, c in rows] + [_full_spec(p) for p in params]
        + [pl.BlockSpec((L, out_w), lambda s: (order(s), 0)) for _ in adds],
        out_specs=[pl.BlockSpec((L, out_w), lambda s: (order(s), 0)),
                   pl.BlockSpec((1,) + state_shape, lambda s: (order(s), 0, 0))],
        out_shape=[jax.ShapeDtypeStruct((n * L, out_w), F32), jax.ShapeDtypeStruct((n,) + state_shape, F32)],
        scratch_shapes=[pltpu.VMEM(state_shape, F32)],
        compiler_params=_cp(("arbitrary",)),
    )(*[a for a, _, _ in rows], *params, *adds)


def _scan_bwd(name, chunk_fn, L, n, nx, rev, rows, params, states, dy, state_shape, out_w, addends=None):
    order = _scan_order(n, nx, rev, True)
    dy_blocks = dy.shape[0] // L
    nr, npar = len(rows), len(params)
    adds = [] if addends is None else list(addends)

    def body(*refs):
        i = pl.program_id(0)
        ds_scr = refs[-1]
        rv = [r[...] for r in refs[:nr]]
        pv = [p[...] for p in refs[nr:nr + npar]]
        st_ref, dy_ref = refs[nr + npar], refs[nr + npar + 1]
        a_refs = refs[nr + npar + 2:nr + npar + 2 + len(adds)]
        o_refs = refs[nr + npar + 2 + len(adds):-1]
        p_refs = o_refs[nr:]

        @pl.when(i == 0)
        def _():
            ds_scr[...] = jnp.zeros_like(ds_scr)
            for p_ref in p_refs:
                p_ref[...] = jnp.zeros_like(p_ref)

        _, vjp = jax.vjp(functools.partial(chunk_fn, rev=rev), st_ref[0], *rv, *pv)
        dy_blk = jnp.where(order(i) < dy_blocks, dy_ref[...].astype(F32), 0.0)
        grads = vjp((ds_scr[...], dy_blk))
        ds_scr[...] = grads[0]
        for j, (o_ref, g) in enumerate(zip(o_refs[:nr], grads[1:1 + nr])):
            o_ref[...] = g + a_refs[j][...] if adds else g
        for p_ref, g in zip(p_refs, grads[1 + nr:]):
            p_ref[...] += g

    row_specs = [pl.BlockSpec((L, w), lambda s: (order(s), 0)) for _, w, _ in rows]
    res = pl.pallas_call(
        body, name=name, grid=(n,),
        in_specs=[pl.BlockSpec((L, w), lambda s, c=c: (order(s), c)) for _, w, c in rows] + [_full_spec(p) for p in params]
        + [pl.BlockSpec((1,) + state_shape, lambda s: (order(s), 0, 0)),
           pl.BlockSpec((L, out_w), lambda s: (jnp.minimum(order(s), dy_blocks - 1), 0))]
        + row_specs[:len(adds)],
        out_specs=row_specs + [_full_spec(p) for p in params],
        out_shape=[jax.ShapeDtypeStruct((n * L, w), F32) for _, w, _ in rows] + [jax.ShapeDtypeStruct(p.shape, F32) for p in params],
        scratch_shapes=[pltpu.VMEM(state_shape, F32)],
        compiler_params=_cp(("arbitrary",)),
    )(*[a for a, _, _ in rows], *params, states, dy, *adds)
    return res[:nr], res[nr:]


CONV_W = 1024
CONV_COLBLK = (0, 1, 4)


def _conv_specs(nb, src_blk):
    halo = TM // 8
    return [pl.BlockSpec((TM, CONV_W), lambda j, i: (i, src_blk(j))),
            pl.BlockSpec((8, CONV_W), lambda j, i: (jnp.maximum(i * halo - 1, 0), src_blk(j))),
            pl.BlockSpec((8, CONV_W), lambda j, i: (jnp.minimum(i * halo + halo, nb * halo - 1), src_blk(j)))]


def _conv_ext(i, nb, cur, prev, nxt):
    has_prev = jnp.logical_and(i > 0, i < nb - 1)
    has_next = i < nb - 2
    return jnp.concatenate([jnp.where(has_prev, prev, 0.0), cur, jnp.where(has_next, nxt, 0.0)], axis=0)


def _conv_taps(ext, w, flip):
    acc = None
    for j in range(SSD_K):
        wj = w[SSD_K - 1 - j:SSD_K - j, :] if flip else w[j:j + 1, :]
        term = wj * ext[6 + j:6 + j + TM, :]
        acc = term if acc is None else acc + term
    return acc


def _conv(name, src, w8, b1, nb, *, permuted_src, act, flip, out_dtype):
    src_blk = (lambda j: jnp.where(j == 2, CONV_COLBLK[2], j)) if permuted_src else (lambda j: j)

    def body(cur, prev, nxt, w_ref, b_ref, o_ref):
        ext = _conv_ext(pl.program_id(1), nb, cur[...].astype(F32), prev[...].astype(F32), nxt[...].astype(F32))
        acc = _conv_taps(ext, w_ref[...], flip)
        if act:
            acc = jax.nn.silu(acc + b_ref[...])
        o_ref[...] = acc.astype(o_ref.dtype)

    return pl.pallas_call(
        body, name=name, grid=(3, nb),
        in_specs=_conv_specs(nb, src_blk) + [pl.BlockSpec((8, CONV_W), lambda j, i: (0, j)), pl.BlockSpec((1, CONV_W), lambda j, i: (0, j))],
        out_specs=pl.BlockSpec((TM, CONV_W), lambda j, i: (i, j)),
        out_shape=jax.ShapeDtypeStruct((nb * TM, 3 * CONV_W), out_dtype),
        compiler_params=_cp(("parallel", "parallel")),
    )(src, src, src, w8, b1)


def _conv_bwd_pre(name, p1, w8, b1, dxbc_parts, nb):
    src_blk = lambda j: jnp.where(j == 2, CONV_COLBLK[2], j)
    xs_parts, bc_parts = dxbc_parts
    n_x, n_bc = len(xs_parts), len(bc_parts)
    x_blocks = [p.shape[0] // TM for p in xs_parts]

    def body(*refs):
        cur, prev, nxt, w_ref, b_ref = refs[:5]
        d_refs = refs[5:5 + n_x + n_bc]
        da_ref, dw_ref, db_ref = refs[5 + n_x + n_bc:]
        j, i = pl.program_id(0), pl.program_id(1)
        ext = _conv_ext(i, nb, cur[...], prev[...], nxt[...])
        acc = _conv_taps(ext, w_ref[...], False) + b_ref[...]
        dx = d_refs[0][...]
        for r, blocks in zip(d_refs[1:n_x], x_blocks[1:]):
            dx = dx + jnp.where(i < blocks, r[...], 0.0)
        dbc = jnp.concatenate([d_refs[n_x][...], d_refs[n_x + 1][...]], axis=1)
        dy = jnp.where(j == 2, dbc, dx)
        sg = jax.nn.sigmoid(acc)
        da = dy * (sg + acc * sg * (1.0 - sg))
        da_ref[...] = da

        @pl.when(i == 0)
        def _():
            dw_ref[...] = jnp.zeros_like(dw_ref)
            db_ref[...] = jnp.zeros_like(db_ref)

        rows = [jnp.sum(da * ext[6 + t:6 + t + TM, :], axis=0, keepdims=True) for t in range(SSD_K)]
        dw_ref[...] += jnp.concatenate(rows + [jnp.zeros((8 - SSD_K, CONV_W), F32)], axis=0)
        db_ref[...] += jnp.sum(da, axis=0, keepdims=True)

    x_specs = [pl.BlockSpec((TM, CONV_W), lambda j, i, b=b: (jnp.minimum(i, b - 1), jnp.minimum(j, 1))) for b in x_blocks]
    bc_specs = [pl.BlockSpec((TM, 512), lambda j, i: (i, 0)) for _ in bc_parts]
    return pl.pallas_call(
        body, name=name, grid=(3, nb),
        in_specs=_conv_specs(nb, src_blk) + [pl.BlockSpec((8, CONV_W), lambda j, i: (0, j)), pl.BlockSpec((1, CONV_W), lambda j, i: (0, j))]
        + x_specs + bc_specs,
        out_specs=[pl.BlockSpec((TM, CONV_W), lambda j, i: (i, j)), pl.BlockSpec((8, CONV_W), lambda j, i: (0, j)),
                   pl.BlockSpec((1, CONV_W), lambda j, i: (0, j))],
        out_shape=[jax.ShapeDtypeStruct((nb * TM, 3 * CONV_W), F32), jax.ShapeDtypeStruct((8, 3 * CONV_W), F32),
                   jax.ShapeDtypeStruct((1, 3 * CONV_W), F32)],
        compiler_params=_cp(("arbitrary", "arbitrary")),
    )(p1, p1, p1, w8, b1, *xs_parts, *bc_parts)


def _grid_block(a):
    nbv = TM // a
    blk_b = max(nbv, 8)
    return nbv, blk_b, blk_b // nbv


def _grid_spec(a, nb):
    _, blk_b, per = _grid_block(a)
    return pl.BlockSpec((a, blk_b, D), lambda i: (0, jnp.minimum(i, nb - 2) // per, 0))


def _grid_rows(v_ref, a, i):
    nbv, _, per = _grid_block(a)

    def pick(ph):
        return jnp.concatenate([v_ref[:, ph * nbv + t, :] for t in range(nbv)], axis=0)

    out = pick(0)
    for ph in range(1, per):
        out = jnp.where(i % per == ph, pick(ph), out)
    return out


def _loss_head(x, f, target, modp, g_final, rows_r):
    tview = target.reshape(rows_r, target.shape[0] // rows_r, D)
    nb = x.shape[0] // TM + 1

    def fn(x_, f_, tgt, modp_, g_):
        xn = x_ + _sel_mod(modp_, 0.0)[5:6] * f_
        err = _rms(xn) * g_ - tgt
        return 0.5 * jnp.sum(jnp.mean(err * err, axis=-1))

    def body(x_ref, f_ref, t_ref, m_ref, g_ref, l_ref, dx_ref, df_ref, dm_ref, dg_ref):
        i = pl.program_id(0)
        tgt = _grid_rows(t_ref, rows_r, i)
        l, vjp = jax.vjp(lambda a_, b_, c_, d_: fn(a_, b_, tgt, c_, d_), x_ref[...], f_ref[...], m_ref[...], g_ref[...])
        dx, df, dm, dg = vjp(jnp.ones((), F32))

        @pl.when(i == 0)
        def _():
            l_ref[...] = jnp.zeros_like(l_ref)
            dm_ref[...] = jnp.zeros_like(dm_ref)
            dg_ref[...] = jnp.zeros_like(dg_ref)

        l_ref[...] += jnp.reshape(l, (1, 1))
        dx_ref[...] = dx
        df_ref[...] = df.astype(df_ref.dtype)
        dm_ref[...] += dm
        dg_ref[...] += dg

    rowspec = pl.BlockSpec((TM, D), lambda i: (i, 0))
    return pl.pallas_call(
        body, name="loss_head", grid=(nb - 1,),
        in_specs=[rowspec, rowspec, _grid_spec(rows_r, nb), _full_spec(modp), _full_spec(g_final)],
        out_specs=[pl.BlockSpec((1, 1), lambda i: (0, 0)), rowspec, rowspec, _full_spec(modp), _full_spec(g_final)],
        out_shape=[jax.ShapeDtypeStruct((1, 1), F32), jax.ShapeDtypeStruct(x.shape, F32), jax.ShapeDtypeStruct(x.shape, MXU_DTYPE),
                   jax.ShapeDtypeStruct(modp.shape, F32), jax.ShapeDtypeStruct(g_final.shape, F32)],
        compiler_params=_cp(("arbitrary",)),
    )(x, f, tview, modp, g_final)


def _repack(name, shards, segs, wp):
    nd, kk, ws = shards.shape
    tr = 128
    used = sum(e - s for s, e in segs)

    def body(a_ref, o_ref):
        full = jnp.concatenate([a_ref[d].astype(F32) for d in range(nd)], axis=1)
        parts = [full[:, s:e] for s, e in segs]
        if wp > used:
            parts.append(jnp.zeros((tr, wp - used), F32))
        o_ref[...] = jnp.concatenate(parts, axis=1).astype(o_ref.dtype)

    return pl.pallas_call(
        body, name=name, grid=(kk // tr,),
        in_specs=[pl.BlockSpec((nd, tr, ws), lambda i: (0, i, 0))],
        out_specs=pl.BlockSpec((tr, wp), lambda i: (i, 0)),
        out_shape=jax.ShapeDtypeStruct((kk, wp), MXU_DTYPE),
        compiler_params=_cp(("parallel",)),
    )(shards)


def _unpack(name, dw, segs, ws, out_dtype):
    kk, wp = dw.shape
    tr = 128
    order = sorted(range(len(segs)), key=lambda i: segs[i][0])
    offs, o = [], 0
    for s, e in segs:
        offs.append(o)
        o += e - s

    def body(a_ref, o_ref):
        a = a_ref[...].astype(F32)
        full = jnp.concatenate([a[:, offs[i]:offs[i] + segs[i][1] - segs[i][0]] for i in order], axis=1)
        for d in range(NDEV):
            o_ref[d] = full[:, d * ws:(d + 1) * ws].astype(o_ref.dtype)

    return pl.pallas_call(
        body, name=name, grid=(kk // tr,),
        in_specs=[pl.BlockSpec((tr, wp), lambda i: (i, 0))],
        out_specs=pl.BlockSpec((NDEV, tr, ws), lambda i: (0, i, 0)),
        out_shape=jax.ShapeDtypeStruct((NDEV, kk, ws), out_dtype),
        compiler_params=_cp(("parallel",)),
    )(dw)


def _adam_math(w, g, m, v):
    m = ADAM_B1 * m + (1.0 - ADAM_B1) * g
    v = ADAM_B2 * v + (1.0 - ADAM_B2) * jnp.square(g)
    m_hat = m / (1.0 - ADAM_B1 ** ADAM_STEP)
    v_hat = v / (1.0 - ADAM_B2 ** ADAM_STEP)
    delta = -ADAM_LR * (m_hat / (jnp.sqrt(v_hat) + ADAM_EPS) + ADAM_WD * w)
    return delta, m, v


def _adam(name, w, parts, m, v, after):
    r, c = w.shape
    nsec, npart = len(parts), parts[0].shape[0]
    rs = r // nsec
    tr = _pick(rs, (256, 128, 64, 32, 16, 8)) if rs * c * 4 > (1 << 20) else rs
    tiles = rs // tr

    def body(w_ref, *refs):
        m_ref, v_ref, _, g_ref, d_ref, nm_ref, nv_ref = refs[nsec:]
        sec = pl.program_id(0) // tiles
        for a, p_ref in enumerate(refs[:nsec]):
            @pl.when(sec == a)
            def _(p_ref=p_ref):
                g = p_ref[0].astype(F32)
                for s in range(1, npart):
                    g = g + p_ref[s].astype(F32)
                delta, nm, nv = _adam_math(w_ref[...], g, m_ref[...], v_ref[...])
                g_ref[...], d_ref[...], nm_ref[...], nv_ref[...] = g, delta, nm, nv

    spec = pl.BlockSpec((tr, c), lambda i: (i, 0))
    part_specs = [pl.BlockSpec((npart, tr, c), lambda i, a=a: (0, jnp.clip(i - a * tiles, 0, tiles - 1), 0)) for a in range(nsec)]
    return pl.pallas_call(
        body, name=name, grid=(r // tr,),
        in_specs=[spec] + part_specs + [spec, spec, ANY],
        out_specs=[spec] * 4, out_shape=[jax.ShapeDtypeStruct((r, c), F32)] * 4,
        compiler_params=_cp(("parallel",)),
    )(w, *parts, m, v, after)


def _mod_fwd(c_all, mod_w):
    nl, _, ws = mod_w.shape

    def body(c_ref, w_ref, o_ref):
        o_ref[0] = _dot(jax.nn.silu(c_ref[...]), w_ref[0])

    return pl.pallas_call(
        body, name="mod_fwd", grid=(nl,),
        in_specs=[_full_spec(c_all), pl.BlockSpec((1, D, ws), lambda i: (i, 0, 0))],
        out_specs=pl.BlockSpec((1, 16, ws), lambda i: (i, 0, 0)),
        out_shape=jax.ShapeDtypeStruct((nl, 16, ws), F32),
        compiler_params=_cp(("parallel",)),
    )(c_all, mod_w)


def _mod_bwd(c_all, mod_w, dm):
    nl, _, ws = mod_w.shape

    def body(c_ref, w_ref, d_ref, dw_ref, dc_ref):
        dw_ref[0] = _dot_tn(jax.nn.silu(c_ref[...]), d_ref[0])
        dc_ref[0] = _dot_nt(d_ref[0], w_ref[0])

    return pl.pallas_call(
        body, name="mod_bwd", grid=(nl,),
        in_specs=[_full_spec(c_all), pl.BlockSpec((1, D, ws), lambda i: (i, 0, 0)), pl.BlockSpec((1, 16, ws), lambda i: (i, 0, 0))],
        out_specs=[pl.BlockSpec((1, D, ws), lambda i: (i, 0, 0)), pl.BlockSpec((1, 16, D), lambda i: (i, 0, 0))],
        out_shape=[jax.ShapeDtypeStruct((nl, D, ws), F32), jax.ShapeDtypeStruct((nl, 16, D), F32)],
        compiler_params=_cp(("parallel",)),
    )(c_all, mod_w, dm)


def _sum_parts(name, parts):
    npart, r, c = parts.shape

    def body(p_ref, o_ref):
        g = p_ref[0].astype(F32)
        for s in range(1, npart):
            g = g + p_ref[s].astype(F32)
        o_ref[...] = g

    return pl.pallas_call(body, name=name, out_shape=jax.ShapeDtypeStruct((r, c), F32), compiler_params=_cp())(parts)


MESH = pl.DeviceIdType.MESH
ANY = pl.BlockSpec(memory_space=pl.ANY)
N_PEERS = NDEV - 1


def _mesh_pos():
    return lax.axis_index("x"), lax.axis_index("y"), lax.axis_index("c")


def _slot(px, py, pc):
    return 4 * px + 2 * py + pc


def _two_level_gather(x_refs, o_refs, send_sems, recv_sems, local_sems):
    x, y, c = _mesh_pos()
    me, sibling = (x, y, c), (x, y, 1 - c)
    chips = [(1 - x, y), (x, 1 - y), (1 - x, 1 - y)]
    n = len(x_refs)

    def copy(a, k, block, to, src=None):
        dst = o_refs[a].at[_slot(*block)]
        return pltpu.make_async_remote_copy(src_ref=dst if src is None else src, dst_ref=dst, send_sem=send_sems.at[a, k],
                                            recv_sem=recv_sems.at[a, k], device_id=to, device_id_type=MESH)

    mine = [pltpu.make_async_copy(x_refs[a], o_refs[a].at[_slot(*me)], local_sems.at[a]) for a in range(n)]
    for cp in mine:
        cp.start()
    first = []
    for a in range(n):
        first.append(copy(a, 0, me, sibling, src=x_refs[a]))
        first += [copy(a, 1 + j, me, (*chip, c), src=x_refs[a]) for j, chip in enumerate(chips)]
    for cp in first:
        cp.start()
    passed = []
    for j, chip in enumerate(chips):
        for a in range(n):
            copy(a, 1 + j, (*chip, c), me).wait_recv()
            fwd = copy(a, 4 + j, (*chip, c), sibling)
            fwd.start()
            passed.append(fwd)
    for a in range(n):
        copy(a, 0, sibling, me).wait_recv()
        for j, chip in enumerate(chips):
            copy(a, 4 + j, (*chip, 1 - c), me).wait_recv()
    for cp in first + passed:
        cp.wait_send()
    for cp in mine:
        cp.wait()


def _ag_small(name, x):
    r, c = x.shape

    def body(x_ref, o_ref, send_sems, recv_sems, local_sems):
        _two_level_gather([x_ref], [o_ref], send_sems, recv_sems, local_sems)

    return pl.pallas_call(
        body, name=name, out_shape=jax.ShapeDtypeStruct((NDEV, r, c), x.dtype),
        in_specs=[pl.BlockSpec(memory_space=pltpu.VMEM)], out_specs=pl.BlockSpec(memory_space=pltpu.VMEM),
        scratch_shapes=[pltpu.SemaphoreType.DMA((1, N_PEERS)), pltpu.SemaphoreType.DMA((1, N_PEERS)), pltpu.SemaphoreType.DMA((1,))],
        compiler_params=pltpu.CompilerParams(vmem_limit_bytes=VMEM_LIMIT),
    )(x)


def _ag_big(name, shards):
    n = len(shards)

    def body(*refs):
        _two_level_gather(refs[:n], refs[n:2 * n], *refs[2 * n:])

    return pl.pallas_call(
        body, name=name, out_shape=[jax.ShapeDtypeStruct((NDEV,) + s.shape, s.dtype) for s in shards],
        in_specs=[ANY] * n, out_specs=[ANY] * n,
        scratch_shapes=[pltpu.SemaphoreType.DMA((n, N_PEERS)), pltpu.SemaphoreType.DMA((n, N_PEERS)), pltpu.SemaphoreType.DMA((n,))],
    )(*shards)


HBM = pl.BlockSpec(memory_space=pltpu.HBM)
SEM = pl.BlockSpec(memory_space=pltpu.SEMAPHORE)
EFFECT = pltpu.SideEffectType.DATAFLOW_SIDE_EFFECTING


def _peers(x, y, c):
    return [(k - 1, ((1 - x) if k & 4 else x, (1 - y) if k & 2 else y, (1 - c) if k & 1 else c)) for k in range(1, NDEV)]


def _xchg_copy(src_refs, land_refs, send_sems, recv_sems, a, k, peer, me, scatter):
    src = src_refs[a].at[_slot(*peer)] if scatter else src_refs[a]
    return pltpu.make_async_remote_copy(src_ref=src, dst_ref=land_refs[a].at[me], send_sem=send_sems.at[a * N_PEERS + k],
                                        recv_sem=recv_sems.at[a * N_PEERS + k], device_id=peer, device_id_type=MESH)


def _xchg_start(name, srcs, lands, deps, scatter):
    n, nd = len(srcs), len(deps)

    def body(*refs):
        src_refs, land_refs = refs[:n], refs[n:2 * n]
        send_sems, recv_sems, token = refs[2 * n + nd], refs[2 * n + nd + 1], refs[-1]
        x, y, c = _mesh_pos()
        me = _slot(x, y, c)
        for k, peer in _peers(x, y, c):
            for a in range(n):
                _xchg_copy(src_refs, land_refs, send_sems, recv_sems, a, k, peer, me, scatter).start()
        token[...] = jnp.zeros_like(token)

    res = pl.pallas_call(
        body, name=name,
        out_shape=(pltpu.SemaphoreType.DMA((n * N_PEERS,)), pltpu.SemaphoreType.DMA((n * N_PEERS,)),
                   *[pltpu.HBM(s.shape, s.dtype) for s in srcs], *[pltpu.HBM(s.shape, s.dtype) for s in lands],
                   jax.ShapeDtypeStruct((8, 128), F32)),
        in_specs=[HBM] * (2 * n) + [ANY] * nd,
        out_specs=(SEM, SEM, *([HBM] * (2 * n)), pl.BlockSpec(memory_space=pltpu.VMEM)),
        input_output_aliases={i: 2 + i for i in range(2 * n)},
        compiler_params=pltpu.CompilerParams(has_side_effects=EFFECT),
    )(*[pltpu.with_memory_space_constraint(s, pltpu.HBM) for s in srcs],
      *[pltpu.with_memory_space_constraint(s, pltpu.HBM) for s in lands], *deps)
    return res[0], res[1], res[2:2 + n], res[2 + n:2 + 2 * n], res[-1]


def _xchg_wait(name, send_sems, recv_sems, srcs, lands, after, scatter):
    n = len(srcs)

    def body(*refs):
        src_refs, land_refs = refs[:n], refs[n:2 * n]
        s_sems, r_sems = refs[2 * n], refs[2 * n + 1]
        x, y, c = _mesh_pos()
        me = _slot(x, y, c)
        for k, peer in _peers(x, y, c):
            for a in range(n):
                cp = _xchg_copy(src_refs, land_refs, s_sems, r_sems, a, k, peer, me, scatter)
                cp.wait_send()
                cp.wait_recv()

    res = pl.pallas_call(
        body, name=name,
        out_shape=[pltpu.HBM(s.shape, s.dtype) for s in srcs] + [pltpu.HBM(s.shape, s.dtype) for s in lands],
        in_specs=[HBM] * (2 * n) + [SEM, SEM, ANY], out_specs=[HBM] * (2 * n),
        input_output_aliases={i: i for i in range(2 * n)},
        compiler_params=pltpu.CompilerParams(has_side_effects=EFFECT),
    )(*srcs, *lands, send_sems, recv_sems, after)
    return res[n:]


def _landing(name, srcs, me, scatter):
    shapes = [s.shape[-2:] for s in srcs]

    def body(me_ref, *refs):
        for s_ref, o_ref in zip(refs[:len(srcs)], refs[len(srcs):]):
            o_ref[...] = s_ref[...].reshape(o_ref.shape)

    def slot_spec(r, c):
        return pl.BlockSpec((1, r, c), lambda i, me_ref: (me_ref[0], 0, 0))

    return pl.pallas_call(
        body, name=name, out_shape=[jax.ShapeDtypeStruct((NDEV, r, c), s.dtype) for s, (r, c) in zip(srcs, shapes)],
        grid_spec=pltpu.PrefetchScalarGridSpec(
            num_scalar_prefetch=1, grid=(1,),
            in_specs=[slot_spec(r, c) if scatter else pl.BlockSpec((r, c), lambda i, me_ref: (0, 0)) for r, c in shapes],
            out_specs=[slot_spec(r, c) for r, c in shapes]),
        compiler_params=_cp(("arbitrary",)),
    )(jnp.reshape(me, (1,)).astype(jnp.int32), *srcs)


STAGES = ("l0_mixer", "l0_ffn", "l1_mixer", "l1_ffn")
STAGE_LAYOUT = {"l0_mixer": (AB_SEGS, AB_P), "l1_mixer": (SSD_SEGS, SSD_P_W)}


class _Exchange:
    def __init__(self, shards, me):
        self.shards, self.me = shards, me
        self.pending, self.pending_grads, self.recv = {}, None, {}

    def _layout(self, stage):
        ws = self.shards[stage][0].shape[-1]
        return STAGE_LAYOUT.get(stage, (((0, NDEV * ws),), NDEV * ws)) + (ws,)

    def _start_gather(self, stage, deps):
        srcs = list(self.shards[stage])
        lands = _landing("own_" + stage, srcs, self.me, False)
        return _xchg_start("gather_start_" + stage, srcs, lands, deps, False)

    def get(self, stage, dep, thread):
        i = STAGES.index(stage)
        if i == 0:
            g_in, g_out = _ag_big("gather_" + stage, list(self.shards[stage]))
            ahead, deps = STAGES[1:3], [g_out, dep]
        else:
            ss, rs, srcs, lands, _ = self.pending.pop(stage)
            g_in, g_out = _xchg_wait("gather_wait_" + stage, ss, rs, srcs, lands, dep, False)
            ahead, deps = STAGES[i + 2:i + 3], [g_out]
        for nxt in ahead:
            self.pending[nxt] = self._start_gather(nxt, deps)
            deps = [self.pending[nxt][4]]
            thread = thread + self.pending[nxt][4][0, 0]
        segs, wp, _ = self._layout(stage)
        return _repack("repack_" + stage, g_in, segs, wp), g_out.reshape(-1, D), thread

    def put(self, stage, d_in, d_out, thread):
        segs, _, ws = self._layout(stage)
        parts = [_unpack("unpack_" + stage, d_in, segs, ws, MXU_DTYPE), d_out.reshape(NDEV, -1, D)]
        deps = [parts[0]]
        if self.pending_grads is not None:
            deps = [self.finish(parts[0])[0]]
        self.staged = (stage, parts)
        return thread if stage == STAGES[0] else thread + self.start_last(deps)[0, 0]

    def start_last(self, deps):
        stage, parts = self.staged
        lands = _landing("own_grad_" + stage, parts, self.me, True)
        self.pending_grads = (stage,) + _xchg_start("scatter_start_" + stage, parts, lands, deps, True)
        return self.pending_grads[5]

    def finish(self, after):
        stage, ss, rs, srcs, lands, _ = self.pending_grads
        self.recv[stage] = _xchg_wait("scatter_wait_" + stage, ss, rs, srcs, lands, after, True)
        self.pending_grads = None
        return self.recv[stage]


def _mm_swiglu(name, h, w_in):
    m, kk = h.shape
    f = w_in.shape[1] // 2
    tm = _pick(m, (272, 256, 128))

    def body(h_ref, wg_ref, wu_ref, pf_ref, act_ref):
        a = h_ref[...].astype(MXU_DTYPE)
        g = jnp.dot(a, wg_ref[...].astype(MXU_DTYPE), preferred_element_type=F32).astype(MXU_DTYPE)
        u = jnp.dot(a, wu_ref[...].astype(MXU_DTYPE), preferred_element_type=F32).astype(MXU_DTYPE)
        pf_ref[0] = g
        pf_ref[1] = u
        act_ref[...] = (jax.nn.silu(g.astype(F32)) * u.astype(F32)).astype(act_ref.dtype)

    return pl.pallas_call(
        body, name=name, grid=(m // tm,),
        in_specs=[pl.BlockSpec((tm, kk), lambda i: (i, 0)), pl.BlockSpec((kk, f), lambda i: (0, 0)), pl.BlockSpec((kk, f), lambda i: (0, 1))],
        out_specs=[pl.BlockSpec((2, tm, f), lambda i: (0, i, 0)), pl.BlockSpec((tm, f), lambda i: (i, 0))],
        out_shape=[jax.ShapeDtypeStruct((2, m, f), MXU_DTYPE), jax.ShapeDtypeStruct((m, f), MXU_DTYPE)],
        compiler_params=_cp(("parallel",)),
    )(h, w_in, w_in)


def _ffn_fwd(tag, h, w_in, w_out, nb, cb):
    pf, act = _mm_swiglu(tag + "_ffn_in", h, w_in)
    return pf, act, _mm(tag + "_ffn_out", act, w_out, "nn", F32)


def _mm_swiglu_bwd(name, df, w_out, pf):
    m, kk = df.shape
    f = w_out.shape[0]
    tm = _pick(m, (272, 256, 128))

    def body(d_ref, w_ref, pf_ref, o_ref):
        dact = lax.dot_general(d_ref[...].astype(MXU_DTYPE), w_ref[...].astype(MXU_DTYPE), (((1,), (1,)), ((), ())),
                               preferred_element_type=F32)
        g, u = pf_ref[0].astype(F32), pf_ref[1].astype(F32)
        sg = jax.nn.sigmoid(g)
        o_ref[:, 0:f] = (dact * u * (sg * (1.0 + g * (1.0 - sg)))).astype(o_ref.dtype)
        o_ref[:, f:2 * f] = (dact * (g * sg)).astype(o_ref.dtype)

    return pl.pallas_call(
        body, name=name, grid=(m // tm,),
        in_specs=[pl.BlockSpec((tm, kk), lambda i: (i, 0)), pl.BlockSpec((f, kk), lambda i: (0, 0)), pl.BlockSpec((2, tm, f), lambda i: (0, i, 0))],
        out_specs=pl.BlockSpec((tm, 2 * f), lambda i: (i, 0)),
        out_shape=jax.ShapeDtypeStruct((m, 2 * f), MXU_DTYPE),
        compiler_params=_cp(("parallel",)),
    )(df, w_out, pf)


def _ffn_bwd(tag, h, pf, act, df, w_in, w_out, nb, cb):
    dw_out = _mm(tag + "_ffn_out_dw", act, df, "tn", MXU_DTYPE)
    dpf = _mm_swiglu_bwd(tag + "_ffn_out_dx", df, w_out, pf)
    dw_in = _mm(tag + "_ffn_in_dw", h, dpf, "tn", MXU_DTYPE)
    dh = _mm(tag + "_ffn_in_dx", dpf, w_in, "nt", MXU_DTYPE)
    return dw_out, dw_in, dh


def _local_step(x, ctx, target, mod, P, comm):
    T = x.shape[0]
    N = T + CTX
    nb, cb = N // TM, N // TM - 1
    R = T // GRID_W
    mod0, mod1 = mod[0], mod[1]
    ng = P["norm_g"]
    g00, g01, g10, g11 = ng[0, 0][None], ng[0, 1][None], ng[1, 0][None], ng[1, 1][None]
    pre = functools.partial(_fn_prenorm, a=0, b=1)
    rpre = functools.partial(_fn_resid_prenorm, gi=2, a=3, b=4)
    res5 = functools.partial(_fn_resid, gi=5)
    dirs = (("f", False), ("b", True))

    xc0 = _row_cat(x, ctx, nb)
    w_ab_in, w_ab_out, g00 = comm.get("l0_mixer", mod, g00)
    (h0,) = _rowwise("l0_prenorm", pre, nb, cb, [xc0], [g00, mod0], [(D, MXU_DTYPE)])
    p0 = _mm("l0_in", h0, w_ab_in, "nn", F32)
    gla_rows = [(p0, 512, 0), (p0, 256, 8), (p0, 256, 9), (p0, 128, 20)]
    gla_blk = _multi_chunk(_gla_chunk, GLA_L, TM // GLA_L, len(gla_rows))
    gla_par = {d: [P["ab_gate_w"][int(r)], P["ab_gate_b"][int(r)][None]] for d, r in dirs}
    gla_state = (GLA_H * GLA_DV, GLA_H * GLA_DK)
    o, st0 = None, {}
    for d, rev in dirs:
        o, st0[d] = _scan_fwd("gla_fwd_" + d, gla_blk, TM, nb, cb, rev, gla_rows, gla_par[d], gla_state, GLA_H * GLA_DV, o)
    n128, cb128 = N // GMLP_L, T // GMLP_L
    mix_rows = [_row(o, tm=GMLP_L)] + [_row(p0, 512, j, tm=GMLP_L) for j in (1, 2, 3)]
    mix_par = [P["ab_gla_norm_g"], P["ab_vnorm_g"], P["ab_spatial_w"].reshape(GMLP_G * GMLP_L, GMLP_L), P["ab_spatial_b"].T]
    (cat0,) = _rowwise("l0_mix", _fn_mixpost, n128, cb128, mix_rows, mix_par, [(D, MXU_DTYPE)], tm=GMLP_L)
    y0 = _mm("l0_out", cat0, w_ab_out, "nn", F32)
    w_fi0, w_fo0, g01 = comm.get("l0_ffn", y0, g01)
    x1, h1 = _rowwise("l0_ffn_prenorm", rpre, nb, cb, [xc0, _row(y0)], [g01, mod0, mod0], [(D, F32), (D, MXU_DTYPE)])
    pf0, act0, f0 = _ffn_fwd("l0", h1, w_fi0, w_fo0, nb, cb)
    w_ssd_in, w_ssd_out, g10 = comm.get("l1_mixer", f0, g10)
    x2p, h2 = _rowwise("l0_resid_l1_prenorm", functools.partial(_fn_resid_prenorm, gi=5, a=0, b=1), nb, cb,
                       [_row_grid(x1, R, nb), _row_grid(f0, R, nb)], [g10, mod0, mod1], [(D, F32), (D, MXU_DTYPE)])
    p1 = _mm("l1_in", h2, w_ssd_in, "nn", F32)
    conv_w8 = jnp.concatenate([P["ssd_conv_w"], jnp.zeros((8 - SSD_K, 3 * CONV_W), F32)], axis=0)
    xbc = _conv("l1_conv", p1, conv_w8, P["ssd_conv_b"], nb, permuted_src=True, act=True, flip=False, out_dtype=F32)
    ssd_rows = [(xbc, SSD_INNER, 0), (xbc, 512, 4), (xbc, 512, 5), (p1, 128, 40)]
    ssd_blk = _multi_chunk(_ssd_chunk, SSD_L, TM // SSD_L, len(ssd_rows))
    ssd_par = {d: [P["ssd_dt_bias"][int(r)][None], P["ssd_a_log"][int(r)][None]] for d, r in dirs}
    ssd_state = (SSD_N, SSD_INNER)
    ys, st1 = None, {}
    for d, rev in dirs:
        ys, st1[d] = _scan_fwd("ssd_fwd_" + d, ssd_blk, TM, nb, cb, rev, ssd_rows, ssd_par[d], ssd_state, SSD_INNER, ys)
    fin_rows = [_row(ys), _row(xbc, SSD_INNER, 0), _row(p1, SSD_INNER, 1)]
    fin_par = [P["ssd_d"], P["ssd_norm_g"]]
    (yn,) = _rowwise("l1_finish", _fn_ssd_finish, cb, cb, fin_rows, fin_par, [(SSD_INNER, MXU_DTYPE)])
    y1 = _mm("l1_out", yn, w_ssd_out, "nn", F32)
    w_fi1, w_fo1, g11 = comm.get("l1_ffn", y1, g11)
    x3, h3 = _rowwise("l1_ffn_prenorm", rpre, cb, cb, [_row(x2p), _row(y1)], [g11, mod1, mod1], [(D, F32), (D, MXU_DTYPE)])
    pf1, act1, f1 = _ffn_fwd("l1", h3, w_fi1, w_fo1, cb, cb)
    loss, dx3, df1, dm1_j, d_final_g = _loss_head(x3, f1, target, mod1, P["final_norm_g"], R)

    dP = {"final_norm_g": d_final_g}
    dwo1, dwi1, dh3 = _ffn_bwd("l1", h3, pf1, act1, df1, w_fi1, w_fo1, cb, cb)
    g11 = comm.put("l1_ffn", dwi1, dwo1, g11)
    (dx2p_a, dy1), (dg11, dm1_a, dm1_b) = _rowwise_vjp(
        "l1_ffn_prenorm_bwd", rpre, cb, cb, [_row(x2p), _row(y1)], [g11, mod1, mod1], [_row(dx3), _row(dh3)],
        [(0, F32, None), (1, MXU_DTYPE, None)])
    d_ssd_out = _mm("l1_out_dw", yn, dy1, "tn", MXU_DTYPE)
    dyn = _mm("l1_out_dx", dy1, w_ssd_out, "nt", MXU_DTYPE)
    (dys, dxs, dz), (dP["ssd_d"], dP["ssd_norm_g"]) = _rowwise_vjp(
        "l1_finish_bwd", _fn_ssd_finish, cb, cb, fin_rows, fin_par, [_row(dyn)],
        [(0, F32, None), (1, F32, None), (2, MXU_DTYPE, None)])
    dssd, ddtb, dalog = None, [], []
    for d, rev in dirs:
        dssd, (ddtb_, dalog_) = _scan_bwd("ssd_bwd_" + d, ssd_blk, TM, nb, cb, rev, ssd_rows, ssd_par[d], st1[d], dys, ssd_state,
                                          SSD_INNER, dssd)
        ddtb.append(ddtb_); dalog.append(dalog_)
    dx_s, db_s, dc_s, dtl = dssd
    dP["ssd_dt_bias"] = jnp.concatenate(ddtb, axis=0)
    dP["ssd_a_log"] = jnp.concatenate(dalog, axis=0)
    dacc, dcw8, dP["ssd_conv_b"] = _conv_bwd_pre("l1_conv_bwd", p1, conv_w8, P["ssd_conv_b"], ([dx_s, dxs], [db_s, dc_s]), nb)
    dP["ssd_conv_w"] = dcw8[:SSD_K]
    dpc = _conv("l1_conv_dx", dacc, conv_w8, jnp.zeros((1, 3 * CONV_W), F32), nb, permuted_src=False, act=False, flip=True,
                out_dtype=MXU_DTYPE)
    cat1 = functools.partial(_fn_concat, sums=(1, 1, 1, 1), pad=SSD_P_W - 5248)
    (dp1,) = _rowwise("l1_dp", cat1, nb, cb, [_row(dpc, SSD_INNER, 0), _row(dz, valid=cb), _row(dpc, 1024, 2), _row(dtl)],
                      [], [(SSD_P_W, MXU_DTYPE)])
    g10 = comm.put("l1_mixer", _mm("l1_in_dw", h2, dp1, "tn", F32), d_ssd_out, g10)
    dh2 = _mm("l1_in_dx", dp1, w_ssd_in, "nt", MXU_DTYPE)
    (dx2p,), (dg10, dm1_f) = _rowwise_vjp("l1_prenorm_bwd", pre, nb, cb, [_row(x2p)], [g10, mod1], [_row(dh2)],
                                          [(0, F32, _row(dx2p_a, valid=cb))])

    (dx1_a, df0), (dm0_e,) = _rowwise_vjp("l0_resid_bwd", res5, nb, cb, [_row(x1), _row(f0)], [mod0],
                                          [_row_grid(dx2p, GRID_W, nb)], [(0, F32, None), (1, MXU_DTYPE, None)])
    dwo0, dwi0, dh1 = _ffn_bwd("l0", h1, pf0, act0, df0, w_fi0, w_fo0, nb, cb)
    g01 = comm.put("l0_ffn", dwi0, dwo0, g01)
    (dxc0_a, dy0), (dg01, dm0_a, dm0_b) = _rowwise_vjp(
        "l0_ffn_prenorm_bwd", rpre, nb, cb, [xc0, _row(y0)], [g01, mod0, mod0], [_row(dx1_a), _row(dh1)],
        [(0, F32, None), (1, MXU_DTYPE, None)])
    d_ab_out = _mm("l0_out_dw", cat0, dy0, "tn", MXU_DTYPE)
    dcat0 = _mm("l0_out_dx", dy0, w_ab_out, "nt", MXU_DTYPE)
    (do, dr, du, dgm), (dP["ab_gla_norm_g"], dP["ab_vnorm_g"], dsw, dsb_t) = _rowwise_vjp(
        "l0_mix_bwd", _fn_mixpost, n128, cb128, mix_rows, mix_par, [_row(dcat0, tm=GMLP_L)],
        [(0, F32, None), (1, MXU_DTYPE, None), (2, MXU_DTYPE, None), (3, MXU_DTYPE, None)], tm=GMLP_L)
    dP["ab_spatial_w"] = dsw.reshape(GMLP_G, GMLP_L, GMLP_L)
    dP["ab_spatial_b"] = dsb_t.T
    gl, dgw, dgb = None, [], []
    for d, rev in dirs:
        gl, (dgw_, dgb_) = _scan_bwd("gla_bwd_" + d, gla_blk, TM, nb, cb, rev, gla_rows, gla_par[d], st0[d], do,
                                     gla_state, GLA_H * GLA_DV, gl)
        dgw.append(dgw_[None]); dgb.append(dgb_)
    dP["ab_gate_w"] = jnp.concatenate(dgw, axis=0)
    dP["ab_gate_b"] = jnp.concatenate(dgb, axis=0)
    cat0f = functools.partial(_fn_concat, sums=(1,) * 7, pad=AB_P - 2688)
    (dp0,) = _rowwise("l0_dp", cat0f, nb, cb, [_row(gl[0]), _row(dr), _row(du), _row(dgm), _row(gl[1]), _row(gl[2]), _row(gl[3])],
                      [], [(AB_P, MXU_DTYPE)])
    g00 = comm.put("l0_mixer", _mm("l0_in_dw", h0, dp0, "tn", F32), d_ab_out, g00)
    dh0 = _mm("l0_in_dx", dp0, w_ab_in, "nt", MXU_DTYPE)
    (grad_x,), (dg00, dm0_s) = _rowwise_vjp("l0_prenorm_bwd", pre, nb, cb, [xc0], [g00, mod0], [_row(dh0)],
                                            [(0, F32, _row(dxc0_a))], x_rows_only=True)
    dP["norm_g"] = jnp.concatenate([dg00, dg01, dg10, dg11], axis=0).reshape(2, 2, D)
    dmod = jnp.stack([dm0_s + dm0_a + dm0_b + dm0_e, dm1_f + dm1_a + dm1_b + dm1_j])
    return loss, grad_x, dmod, dP


WEIGHTS = ("c_ctx", "mod_w", "mod_b", "norm_g", "ffn_w_in", "ffn_w_out", "ab_w_in", "ab_gate_w", "ab_gate_b", "ab_gla_norm_g",
           "ab_vnorm_g", "ab_spatial_w", "ab_spatial_b", "ab_w_out", "ssd_w_in", "ssd_conv_w", "ssd_conv_b", "ssd_dt_bias",
           "ssd_a_log", "ssd_d", "ssd_norm_g", "ssd_w_out", "final_norm_g")
SMALL_SHARDED = ("norm_g", "ab_gate_w", "ab_gate_b", "ssd_conv_w", "ssd_conv_b", "ssd_norm_g")
SMALL = ("c_ctx", "mod_b", "norm_g", "ab_gate_w", "ab_gate_b", "ab_gla_norm_g", "ab_vnorm_g", "ab_spatial_w", "ab_spatial_b",
         "ssd_conv_w", "ssd_conv_b", "ssd_dt_bias", "ssd_a_log", "ssd_d", "ssd_norm_g", "final_norm_g")
LANES = 1024


def _pack(arrs, rows_multiple=8):
    flat = jnp.concatenate([a.reshape(-1).astype(F32) for a in arrs])
    rows = -(-flat.shape[0] // LANES)
    rows = -(-rows // rows_multiple) * rows_multiple
    return jnp.pad(flat, (0, rows * LANES - flat.shape[0])).reshape(rows, LANES)


def _unpack_flat(buf, shapes):
    lead = buf.shape[:-2]
    flat = buf.reshape(lead + (-1,))
    out, o = [], 0
    for s in shapes:
        n = math.prod(s)
        out.append(flat[..., o:o + n].reshape(lead + tuple(s)))
        o += n
    return out


def _unshard(g):
    g = jnp.moveaxis(g, 0, -2)
    return g.reshape(g.shape[:-2] + (g.shape[-2] * g.shape[-1],))


def _my_shard(full, me, ws):
    return lax.dynamic_slice_in_dim(full, me * ws, ws, axis=full.ndim - 1)


def _silu_vjp(cvec, dsc):
    def body(c_ref, d_ref, o_ref):
        _, vjp = jax.vjp(jax.nn.silu, c_ref[...])
        o_ref[...] = vjp(d_ref[...])[0]

    return pl.pallas_call(body, name="c_ctx_bwd", out_shape=jax.ShapeDtypeStruct(cvec.shape, F32), compiler_params=_cp())(cvec, dsc)


def kernel(x, c, ctx, c_ctx, mod_w, mod_b, norm_g, ffn_w_in, ffn_w_out, ab_w_in, ab_gate_w, ab_gate_b, ab_gla_norm_g, ab_vnorm_g, ab_spatial_w, ab_spatial_b, ab_w_out, ssd_w_in, ssd_conv_w, ssd_conv_b, ssd_dt_bias, ssd_a_log, ssd_d, ssd_norm_g, ssd_w_out, final_norm_g, loss_target, m_c_ctx, m_mod_w, m_mod_b, m_norm_g, m_ffn_w_in, m_ffn_w_out, m_ab_w_in, m_ab_gate_w, m_ab_gate_b, m_ab_gla_norm_g, m_ab_vnorm_g, m_ab_spatial_w, m_ab_spatial_b, m_ab_w_out, m_ssd_w_in, m_ssd_conv_w, m_ssd_conv_b, m_ssd_dt_bias, m_ssd_a_log, m_ssd_d, m_ssd_norm_g, m_ssd_w_out, m_final_norm_g, v_c_ctx, v_mod_w, v_mod_b, v_norm_g, v_ffn_w_in, v_ffn_w_out, v_ab_w_in, v_ab_gate_w, v_ab_gate_b, v_ab_gla_norm_g, v_ab_vnorm_g, v_ab_spatial_w, v_ab_spatial_b, v_ab_w_out, v_ssd_w_in, v_ssd_conv_w, v_ssd_conv_b, v_ssd_dt_bias, v_ssd_a_log, v_ssd_d, v_ssd_norm_g, v_ssd_w_out, v_final_norm_g):
    a = dict(locals())
    me = _slot(*_mesh_pos())
    ws_mod = mod_w.shape[-1]

    fwd_small = [c] + [a[k] for k in SMALL_SHARDED]
    g_small = _ag_small("gather_small", _pack(fwd_small))
    parts = _unpack_flat(g_small, [t.shape for t in fwd_small])
    c_rows = parts[0].reshape(NDEV, D)
    full = {k: _unshard(p) for k, p in zip(SMALL_SHARDED, parts[1:])}
    c_all = jnp.concatenate([c_rows, c_ctx[None], jnp.zeros((7, D), F32)], axis=0)
    m_all = _ag_small("gather_mod", _mod_fwd(c_all, mod_w).reshape(2 * 16, ws_mod)).reshape(NDEV, 2, 16, ws_mod)
    m_mine = lax.dynamic_index_in_dim(m_all, me, axis=2, keepdims=False)
    mx = jnp.moveaxis(m_mine, 0, 1).reshape(2, N_MOD, D) + mod_b.reshape(2, N_MOD, D)
    mc = jnp.moveaxis(m_all[:, :, 8, :], 0, 1).reshape(2, N_MOD, D) + mod_b.reshape(2, N_MOD, D)
    pad2 = jnp.zeros((2, 2, D), F32)
    mod = jnp.concatenate([mx, pad2, mc, pad2], axis=1)

    big = {"l0_mixer": (ab_w_in[0], ab_w_out[0]), "l0_ffn": (ffn_w_in[0], ffn_w_out[0]),
           "l1_mixer": (ssd_w_in[0], ssd_w_out[0]), "l1_ffn": (ffn_w_in[1], ffn_w_out[1])}
    comm = _Exchange({k: tuple(w.astype(MXU_DTYPE) for w in v) for k, v in big.items()}, me)
    P = {
        "norm_g": full["norm_g"], "ab_gate_w": full["ab_gate_w"][0], "ab_gate_b": full["ab_gate_b"][0],
        "ab_gla_norm_g": ab_gla_norm_g, "ab_vnorm_g": ab_vnorm_g, "ab_spatial_w": ab_spatial_w[0], "ab_spatial_b": ab_spatial_b[0],
        "ssd_conv_w": full["ssd_conv_w"][0], "ssd_conv_b": full["ssd_conv_b"], "ssd_dt_bias": ssd_dt_bias[0],
        "ssd_a_log": ssd_a_log[0], "ssd_d": ssd_d, "ssd_norm_g": full["ssd_norm_g"], "final_norm_g": final_norm_g[None],
    }

    loss, grad_x, dmod, dP = _local_step(x[0], ctx[0], loss_target[0], mod, P, comm)

    dmx, dmc = dmod[:, 0:N_MOD].reshape(2, N_MOD * D), dmod[:, 8:8 + N_MOD].reshape(2, N_MOD * D)
    small_names = ("ab_gate_w", "ab_gate_b", "ab_gla_norm_g", "ab_vnorm_g", "ab_spatial_w", "ab_spatial_b", "norm_g", "ssd_conv_w",
                   "ssd_conv_b", "ssd_dt_bias", "ssd_a_log", "ssd_d", "ssd_norm_g", "final_norm_g")
    bwd_small = [dP[k] for k in small_names] + [dmc, dmx]
    shapes = [t.shape for t in bwd_small]
    g_bwd = _ag_small("gather_small_grads", _pack(bwd_small))
    summed = _unpack_flat(_sum_parts("sum_small_grads", g_bwd), shapes)
    gfull = dict(zip(small_names, summed[:-2]))
    dmc_sum, dmx_sum = summed[-2], summed[-1]
    dmx_all = _unpack_flat(g_bwd, shapes)[-1]
    dmx_sh = jnp.moveaxis(_my_shard(dmx_all, me, ws_mod), 0, 1)
    dm = jnp.concatenate([dmx_sh, _my_shard(dmc_sum, me, ws_mod)[:, None, :], jnp.zeros((2, 7, ws_mod), F32)], axis=1)
    d_mod_w, dsc = _mod_bwd(c_all, mod_w, dm)
    dsc_ctx = (dsc[0, 8] + dsc[1, 8])[None]
    dsc_all = _ag_small("gather_c_ctx_grad", jnp.concatenate([dsc_ctx, jnp.zeros((7, D), F32)], axis=0))
    d_c_ctx = _silu_vjp(c_ctx[None], _sum_parts("sum_c_ctx_grad", dsc_all)[0:1])[0]

    g_small_w = {
        "c_ctx": d_c_ctx, "mod_b": dmx_sum + dmc_sum, "norm_g": gfull["norm_g"], "ab_gate_w": gfull["ab_gate_w"][None],
        "ab_gate_b": gfull["ab_gate_b"][None], "ab_gla_norm_g": gfull["ab_gla_norm_g"], "ab_vnorm_g": gfull["ab_vnorm_g"],
        "ab_spatial_w": gfull["ab_spatial_w"][None], "ab_spatial_b": gfull["ab_spatial_b"][None], "ssd_conv_w": gfull["ssd_conv_w"][None],
        "ssd_conv_b": gfull["ssd_conv_b"], "ssd_dt_bias": gfull["ssd_dt_bias"][None], "ssd_a_log": gfull["ssd_a_log"][None],
        "ssd_d": gfull["ssd_d"], "ssd_norm_g": gfull["ssd_norm_g"], "final_norm_g": gfull["final_norm_g"][0],
    }
    for k in SMALL_SHARDED:
        g_small_w[k] = _my_shard(g_small_w[k], me, a[k].shape[-1])
    token = comm.start_last([d_c_ctx])
    res = _adam("adam_small", _pack([a[k] for k in SMALL]), [_pack([g_small_w[k] for k in SMALL])[None]],
                _pack([a["m_" + k] for k in SMALL]), _pack([a["v_" + k] for k in SMALL]), token)
    out = {k: vals for k, vals in zip(SMALL, zip(*[_unpack_flat(r, [a[k].shape for k in SMALL]) for r in res]))}

    def adam_big(name, w2d, parts, m2d, v2d, shape):
        return tuple(r.reshape(shape) for r in _adam(name, w2d, parts, m2d, v2d, token))

    def flat2(t):
        return t.reshape(-1, t.shape[-1])

    out["mod_w"] = adam_big("adam_mod_w", flat2(mod_w), [d_mod_w.reshape(1, -1, ws_mod)], flat2(m_mod_w), flat2(v_mod_w), mod_w.shape)

    for j, k in enumerate(("ffn_w_in", "ffn_w_out")):
        out[k] = adam_big("adam_" + k, flat2(a[k]), [comm.recv["l0_ffn"][j], comm.recv["l1_ffn"][j]], flat2(a["m_" + k]),
                          flat2(a["v_" + k]), a[k].shape)
    for j, k in enumerate(("ssd_w_in", "ssd_w_out")):
        out[k] = adam_big("adam_" + k, a[k][0], [comm.recv["l1_mixer"][j]], a["m_" + k][0], a["v_" + k][0], a[k].shape)
    recv_ab = comm.finish(out["ssd_w_out"][3])
    for j, k in enumerate(("ab_w_in", "ab_w_out")):
        out[k] = adam_big("adam_" + k, a[k][0], [recv_ab[j]], a["m_" + k][0], a["v_" + k][0], a[k].shape)

    loss_all = lax.psum(loss[0, 0], ("x", "y", "c"))
    return (loss_all, grad_x[None], *[out[k][0] for k in WEIGHTS], *[out[k][1] for k in WEIGHTS],
            *[out[k][2] for k in WEIGHTS], *[out[k][3] for k in WEIGHTS])
```

```python
import functools
import math

import jax
import jax.numpy as jnp
from jax import lax
from jax.experimental import pallas as pl
from jax.experimental.pallas import tpu as pltpu

F32 = jnp.float32
BF16 = jnp.bfloat16
MXU_DTYPE = jnp.bfloat16

D = 1024
NDEV = 8
N_MOD = 6
EPS = 1e-6
GRID_W = 64
CTX = 256
TM = 256
D_FF = 2816
GLA_H, GLA_DK, GLA_DV, GLA_LR, GLA_TAU, GLA_L = 4, 64, 128, 16, 16.0, 64
GMLP_G, GMLP_C, GMLP_L = 4, 128, 128
SSD_H, SSD_P, SSD_G, SSD_N, SSD_L, SSD_K = 32, 64, 4, 128, 128, 5
SSD_INNER = SSD_H * SSD_P
AB_IN = 2592
SSD_IN = 5184
AB_SEGS = ((256, 768), (1056, 1568), (1568, 2080), (2080, 2592), (0, 256), (800, 1056), (768, 800))
AB_P = 2816
SSD_SEGS = ((0, 2048), (3136, 5184), (2048, 2560), (2560, 3072), (3072, 3136))
SSD_P_W = 5376
VMEM_LIMIT = 56 * 1024 * 1024

ADAM_LR, ADAM_B1, ADAM_B2, ADAM_EPS, ADAM_WD, ADAM_STEP = 0.001, 0.9, 0.999, 1e-08, 0.01, 10


def _cp(sem=None, **kw):
    return pltpu.CompilerParams(dimension_semantics=sem, vmem_limit_bytes=VMEM_LIMIT, **kw)


def _dot(a, b, dims=(((1,), (0,)), ((), ()))):
    return lax.dot_general(a.astype(MXU_DTYPE), b.astype(MXU_DTYPE), dims, preferred_element_type=F32)


def _dot_nt(a, b):
    return _dot(a, b, (((1,), (1,)), ((), ())))


def _dot_tn(a, b):
    return _dot(a, b, (((0,), (0,)), ((), ())))


def _rms(x):
    return x * lax.rsqrt(jnp.mean(x * x, axis=-1, keepdims=True) + EPS)


def _pick(n, prefs):
    for p in prefs:
        if n % p == 0:
            return p
    return n


def _row(arr, width=None, colblk=0, tm=TM, valid=None):
    width = arr.shape[1] if width is None else width
    if valid is None:
        return ([arr], [pl.BlockSpec((tm, width), lambda i, c=colblk: (i, c))], lambda r: r[...].astype(F32), width)
    spec = pl.BlockSpec((tm, width), lambda i, c=colblk: (jnp.minimum(i, valid - 1), c))
    return ([arr], [spec], lambda r: jnp.where(pl.program_id(0) < valid, r[...].astype(F32), 0.0), width)


def _row_grid(arr, a, nb):
    n = arr.shape[0]
    b = (n - CTX) // a

    def load(v_ref, c_ref):
        i = pl.program_id(0)
        return jnp.where(i == nb - 1, c_ref[...], _grid_rows(v_ref, a, i))

    return ([arr.reshape(n // b, b, D), arr], [_grid_spec(a, nb), pl.BlockSpec((TM, D), lambda i: (nb - 1, 0))], load, D)


def _row_cat(x, ctx, nb):
    return ([x, ctx], [pl.BlockSpec((TM, D), lambda i: (jnp.minimum(i, nb - 2), 0)), pl.BlockSpec((TM, D), lambda i: (0, 0))],
            lambda x_ref, c_ref: jnp.where(pl.program_id(0) == nb - 1, c_ref[...], x_ref[...]), D)


def _operands(rows):
    return [a for r in rows for a in r[0]], [s for r in rows for s in r[1]]


def _load_rows(refs, rows):
    vals, k = [], 0
    for r in rows:
        vals.append(r[2](*refs[k:k + len(r[0])]))
        k += len(r[0])
    return vals


def _full_spec(p):
    nd = p.ndim
    return pl.BlockSpec(p.shape, lambda i, nd=nd: (0,) * nd)


def _rowwise(name, fn, n_blocks, ctx_blk, rows, params, outs, tm=TM, post_mm=None):
    arrs, specs = _operands(rows)
    nr, npar = len(arrs), len(params)
    extra = [] if post_mm is None else [post_mm]
    outs = list(outs) + [(w.shape[1], F32) for w in extra]

    def body(*refs):
        t = (pl.program_id(0) >= ctx_blk).astype(F32)
        rv = _load_rows(refs[:nr], rows)
        pv = [p[...] for p in refs[nr:nr + npar]]
        res = list(fn(t, rv, pv))
        o_refs = refs[nr + npar + len(extra):]
        if extra:
            res.append(_dot(res[0].astype(o_refs[0].dtype), refs[nr + npar][...]))
        for o_ref, o in zip(o_refs, res):
            o_ref[...] = o.astype(o_ref.dtype)

    return pl.pallas_call(
        body, name=name, grid=(n_blocks,),
        in_specs=specs + [_full_spec(p) for p in params + extra],
        out_specs=[pl.BlockSpec((tm, w), lambda i: (i, 0)) for w, _ in outs],
        out_shape=[jax.ShapeDtypeStruct((n_blocks * tm, w), dt) for w, dt in outs],
        compiler_params=_cp(("parallel",)),
    )(*arrs, *params, *extra)


def _rowwise_vjp(name, fn, n_blocks, ctx_blk, rows, params, douts, row_grads, tm=TM, x_rows_only=False, cot_mm=None):
    out_blocks = n_blocks - 1 if x_rows_only else n_blocks
    adds = [a for _, _, a in row_grads if a is not None]
    (r_arrs, r_specs), (d_arrs, d_specs), (a_arrs, a_specs) = _operands(rows), _operands(douts), _operands(adds)
    nr, npar, nd, na = len(r_arrs), len(params), len(d_arrs), len(a_arrs)
    extra = [] if cot_mm is None else [cot_mm]

    def body(*refs):
        i = pl.program_id(0)
        t = (i >= ctx_blk).astype(F32)
        rv = _load_rows(refs[:nr], rows)
        pv = [p[...] for p in refs[nr:nr + npar]]
        dv = _load_rows(refs[nr + npar:nr + npar + nd], douts)
        av = _load_rows(refs[nr + npar + nd:nr + npar + nd + na], adds)
        o_refs = refs[nr + npar + nd + na + len(extra):]
        if extra:
            dv[0] = _dot_nt(dv[0], refs[nr + npar + nd + na][...])
        _, vjp = jax.vjp(lambda r, p: tuple(fn(t, r, p)), rv, pv)
        d_rows, d_params = vjp(tuple(dv))
        ai, grads = 0, []
        for ri, _, addend in row_grads:
            g = jnp.concatenate([d_rows[r] for r in ri], axis=1) if isinstance(ri, tuple) else d_rows[ri]
            if addend is not None:
                g = g + av[ai]
                ai += 1
            grads.append(g)

        @pl.when(i < out_blocks)
        def _():
            for o_ref, g in zip(o_refs, grads):
                o_ref[...] = g.astype(o_ref.dtype)

        p_refs = o_refs[len(row_grads):]

        @pl.when(i == 0)
        def _():
            for p_ref in p_refs:
                p_ref[...] = jnp.zeros_like(p_ref)

        for p_ref, g in zip(p_refs, d_params):
            p_ref[...] += g

    widths = [sum(rows[r][3] for r in ri) if isinstance(ri, tuple) else rows[ri][3] for ri, _, _ in row_grads]
    res = pl.pallas_call(
        body, name=name, grid=(n_blocks,),
        in_specs=r_specs + [_full_spec(p) for p in params] + d_specs + a_specs + [_full_spec(p) for p in extra],
        out_specs=[pl.BlockSpec((tm, w), lambda i: (jnp.minimum(i, out_blocks - 1), 0)) for w in widths] + [_full_spec(p) for p in params],
        out_shape=[jax.ShapeDtypeStruct((out_blocks * tm, w), dt) for w, (_, dt, _) in zip(widths, row_grads)]
        + [jax.ShapeDtypeStruct(p.shape, F32) for p in params],
        compiler_params=_cp(("arbitrary",)),
    )(*r_arrs, *params, *d_arrs, *a_arrs, *extra)
    return res[:len(row_grads)], res[len(row_grads):]


def _mm(name, a, b, mode, out_dtype):
    if mode == "nn":
        m, kk = a.shape
        n = b.shape[1]
    elif mode == "nt":
        m, kk = a.shape
        n = b.shape[0]
    else:
        kk, m = a.shape
        n = b.shape[1]
    if mode == "tn":
        tm = _pick(m, (1024, 1408, 512, 256, 128))
        tn = _pick(n, (768, 512, 256, 128))
        tk = kk
    else:
        tm = _pick(m, (1088, 1024, 768, 512, 384, 256, 128))
        tn = n if n <= 2816 else _pick(n, (1024, 768, 512, 256, 128))
        tk = kk if kk <= 2816 else _pick(kk, (2816, 1792, 1024, 768, 512, 256, 128))
    nk = kk // tk
    in_place = out_dtype == F32
    if mode == "nn":
        specs = [pl.BlockSpec((tm, tk), lambda i, j, k: (i, k)), pl.BlockSpec((tk, tn), lambda i, j, k: (k, j))]
        dims = (((1,), (0,)), ((), ()))
    elif mode == "nt":
        specs = [pl.BlockSpec((tm, tk), lambda i, j, k: (i, k)), pl.BlockSpec((tn, tk), lambda i, j, k: (j, k))]
        dims = (((1,), (1,)), ((), ()))
    else:
        specs = [pl.BlockSpec((tk, tm), lambda i, j, k: (k, i)), pl.BlockSpec((tk, tn), lambda i, j, k: (k, j))]
        dims = (((0,), (0,)), ((), ()))

    def body(a_ref, b_ref, o_ref, *scratch):
        part = lax.dot_general(a_ref[...].astype(MXU_DTYPE), b_ref[...].astype(MXU_DTYPE), dims, preferred_element_type=F32)
        if nk == 1:
            o_ref[...] = part.astype(o_ref.dtype)
        else:
            k = pl.program_id(2)
            acc = o_ref if in_place else scratch[0]

            @pl.when(k == 0)
            def _():
                acc[...] = part

            @pl.when(k > 0)
            def _():
                acc[...] += part

            if not in_place:
                @pl.when(k == nk - 1)
                def _():
                    o_ref[...] = acc[...].astype(o_ref.dtype)

    return pl.pallas_call(
        body, name=name, grid=(m // tm, n // tn, nk), in_specs=specs,
        out_specs=pl.BlockSpec((tm, tn), lambda i, j, k: (i, j)),
        out_shape=jax.ShapeDtypeStruct((m, n), out_dtype),
        scratch_shapes=[] if nk == 1 or in_place else [pltpu.VMEM((tm, tn), F32)],
        compiler_params=_cp(("parallel", "parallel", "arbitrary")),
    )(a, b)


def _sel_mod(modp, t):
    return modp[0:8] * (1.0 - t) + modp[8:16] * t


def _fn_prenorm(t, rows, params, *, a, b):
    (x,), (g, modp) = rows, params
    m = _sel_mod(modp, t)
    return ((_rms(x) * g) * (1.0 + m[b:b + 1]) + m[a:a + 1],)


def _fn_resid_prenorm(t, rows, params, *, gi, a, b):
    (x, y), (g, mod_a, mod_b) = rows, params
    ma, mb = _sel_mod(mod_a, t), _sel_mod(mod_b, t)
    xn = x + ma[gi:gi + 1] * y
    return xn, (_rms(xn) * g) * (1.0 + mb[b:b + 1]) + mb[a:a + 1]


def _fn_resid(t, rows, params, *, gi):
    (x, y), (mod_a,) = rows, params
    return (x + _sel_mod(mod_a, t)[gi:gi + 1] * y,)


def _fn_mixpost(t, rows, params):
    (o, r, u, g), (gla_g, vn_g, sw, sb_t) = rows, params
    a =jnp.concatenate([_rms(o[:, h * GLA_DV:(h + 1) * GLA_DV]) for h in range(GLA_H)], axis=1) * gla_g * jax.nn.silu(r)
    uu, vv = jax.nn.gelu(u), jax.nn.gelu(g)
    mu = jnp.mean(vv, axis=-1, keepdims=True)
    var = jnp.mean(jnp.square(vv - mu), axis=-1, keepdims=True)
    vn = ((vv - mu) * lax.rsqrt(var + EPS)) * vn_g
    s = jnp.concatenate(
        [_dot(sw[gi * GMLP_L:(gi + 1) * GMLP_L, :], vn[:, gi * GMLP_C:(gi + 1) * GMLP_C]) + sb_t[:, gi:gi + 1]
         for gi in range(GMLP_G)], axis=1)
    return (jnp.concatenate([a, uu * s], axis=1),)


def _expand_heads(row):
    first = lax.broadcasted_iota(jnp.int32, (1, 2 * SSD_P), 1) < SSD_P
    return jnp.concatenate([jnp.where(first, row[:, 2 * j:2 * j + 1], row[:, 2 * j + 1:2 * j + 2]) for j in range(SSD_H // 2)], axis=1)


def _fn_ssd_finish(t, rows, params):
    (y2, xs, z), (d_skip, norm_g) = rows, params
    d_full = _expand_heads(d_skip)
    y = (y2 + d_full * xs) * jax.nn.silu(z)
    gw = SSD_INNER // SSD_G
    return (jnp.concatenate([_rms(y[:, gi * gw:(gi + 1) * gw]) for gi in range(SSD_G)], axis=1) * norm_g,)


def _fn_concat(t, rows, params, *, sums, pad=0):
    out, i = [], 0
    for n in sums:
        acc = rows[i]
        for j in range(1, n):
            acc = acc + rows[i + j]
        out.append(acc)
        i += n
    if pad:
        out.append(jnp.zeros((out[0].shape[0], pad), F32))
    return (jnp.concatenate(out, axis=1),)


def _tri(n, rev):
    r = lax.broadcasted_iota(jnp.int32, (n, n), 0)
    c = lax.broadcasted_iota(jnp.int32, (n, n), 1)
    return (r <= c) if rev else (r >= c)


def _running_sum(x, rev):
    n, s = x.shape[0], 1
    while s < n:
        z = jnp.zeros((s, x.shape[1]), x.dtype)
        x = x + (jnp.concatenate([x[s:], z], axis=0) if rev else jnp.concatenate([z, x[:n - s]], axis=0))
        s *= 2
    return x


def _gla_chunk(S, v, k, q, tail, gw, gb, *, rev):
    L, H = GLA_L, GLA_H
    lr = tail[:, GLA_LR:2 * GLA_LR] if rev else tail[:, 0:GLA_LR]
    la = jax.nn.log_sigmoid(_dot(lr, gw) + gb) / GLA_TAU
    b = _running_sum(la, rev)
    b_last = b[0:1] if rev else b[L - 1:L]
    kd = k * jnp.exp(b_last - b)
    qd = (q * GLA_DK ** -0.5) * jnp.exp(b)
    ki = k * jnp.exp(-b)

    def same_head(shape, rows_per_head, cols_per_head):
        r = lax.broadcasted_iota(jnp.int32, shape, 0) // rows_per_head
        c = lax.broadcasted_iota(jnp.int32, shape, 1) // cols_per_head
        return r == c

    k_blk = jnp.where(same_head((H * L, H * GLA_DK), L, GLA_DK), jnp.concatenate([ki] * H, axis=0), 0.0)
    v_blk = jnp.where(same_head((H * L, H * GLA_DV), L, GLA_DV), jnp.concatenate([v] * H, axis=0), 0.0)
    row = lax.broadcasted_iota(jnp.int32, (L, H * L), 0)
    src = lax.broadcasted_iota(jnp.int32, (L, H * L), 1) % L
    sc = jnp.where((row <= src) if rev else (row >= src), _dot_nt(qd, k_blk), 0.0)
    o = _dot_nt(qd, S) + _dot(sc, v_blk)
    s_new = S * jnp.exp(b_last) + jnp.where(same_head(S.shape, GLA_DV, GLA_DK), _dot_tn(v, kd), 0.0)
    return s_new, o


def _ssd_chunk(S, x, bm, cm, tail, dtb, alog, *, rev):
    L = SSD_L
    msk = _tri(L, rev)
    raw = tail[:, SSD_H:2 * SSD_H] if rev else tail[:, 0:SSD_H]
    dt = jax.nn.softplus(raw + dtb)
    acum = _running_sum(dt * (-jnp.exp(alog)), rev)
    a_last = acum[0:1] if rev else acum[L - 1:L]
    wst = dt * jnp.exp(a_last - acum)
    eac = jnp.exp(acum)
    dec = jnp.exp(a_last)
    tr = jnp.concatenate([acum, dt, wst, jnp.zeros((L, L - 3 * SSD_H), F32)], axis=1).T
    acum_t, dt_t, wst_t = tr[0:SSD_H], tr[SSD_H:2 * SSD_H], tr[2 * SSD_H:3 * SSD_H]
    lane = lax.broadcasted_iota(jnp.int32, (1, 2 * SSD_P), 1)
    m0 = (lane < SSD_P).astype(F32)
    m1 = 1.0 - m0
    pairs_per_group = SSD_H // SSD_G // 2
    y_parts, s_parts = [], []
    for g in range(SSD_G):
        ns = slice(g * SSD_N, (g + 1) * SSD_N)
        bg, cg = bm[:, ns], cm[:, ns]
        cb = _dot_nt(cg, bg)
        bgt = bg.T
        gs = slice(g * pairs_per_group * 2 * SSD_P, (g + 1) * pairs_per_group * 2 * SSD_P)
        y_carry = _dot(cg, S[:, gs])
        for jj in range(pairs_per_group):
            j = g * pairs_per_group + jj
            ls = slice(j * 2 * SSD_P, (j + 1) * 2 * SSD_P)
            xp, sp = x[:, ls], S[:, ls]
            xm = jnp.concatenate([xp * m0, xp * m1], axis=0)
            lhs, bw = [], []
            for h in (2 * j, 2 * j + 1):
                seg = acum[:, h:h + 1] - acum_t[h:h + 1, :]
                lhs.append(cb * jnp.exp(jnp.where(msk, seg, -jnp.inf)) * dt_t[h:h + 1, :])
                bw.append(bgt * wst_t[h:h + 1, :])
            e_pair = eac[:, 2 * j:2 * j + 1] * m0 + eac[:, 2 * j + 1:2 * j + 2] * m1
            y_parts.append(_dot(jnp.concatenate(lhs, axis=1), xm) + y_carry[:, jj * 2 * SSD_P:(jj + 1) * 2 * SSD_P] * e_pair)
            d_pair = dec[:, 2 * j:2 * j + 1] * m0 + dec[:, 2 * j + 1:2 * j + 2] * m1
            s_parts.append(sp * d_pair + _dot(jnp.concatenate(bw, axis=1), xm))
    return jnp.concatenate(s_parts, axis=1), jnp.concatenate(y_parts, axis=1)


def _multi_chunk(chunk_fn, L, subs, nr):
    def fn(S, *args, rev):
        rows, params = args[:nr], args[nr:]
        ys = [None] * subs
        for j in (range(subs - 1, -1, -1) if rev else range(subs)):
            S, ys[j] = chunk_fn(S, *[r[j * L:(j + 1) * L] for r in rows], *params, rev=rev)
        return S, jnp.concatenate(ys, axis=0)

    return fn


def _scan_order(n, nx, rev, backward):
    nc = n - nx

    def fwd(s):
        return (n - 1 - s) if rev else jnp.where(s < nc, s + nx, s - nc)

    return (lambda s: fwd(n - 1 - s)) if backward else fwd


def _scan_fwd(name, chunk_fn, L, n, nx, rev, rows, params, state_shape, out_w, addend=None):
    order = _scan_order(n, nx, rev, False)
    nr, npar = len(rows), len(params)
    adds = [] if addend is None else [addend]

    def body(*refs):
        s_scr = refs[-1]

        @pl.when(pl.program_id(0) == 0)
        def _():
            s_scr[...] = jnp.zeros_like(s_scr)

        s_in = s_scr[...]
        y_ref, st_ref = refs[nr + npar + len(adds)], refs[nr + npar + len(adds) + 1]
        st_ref[0] = s_in
        s_new, y = chunk_fn(s_in, *[r[...] for r in refs[:nr]], *[p[...] for p in refs[nr:nr + npar]], rev=rev)
        y_ref[...] = y + refs[nr + npar][...] if adds else y
        s_scr[...] = s_new

    return pl.pallas_call(
        body, name=name, grid=(n,),
        in_specs=[pl.BlockSpec((L, w), lambda s, c=c: (order(s), c)) for _, w, c in rows] + [_full_spec(p) for p in params]
        + [pl.BlockSpec((L, out_w), lambda s: (order(s), 0)) for _ in adds],
        out_specs=[pl.BlockSpec((L, out_w), lambda s: (order(s), 0)),
                   pl.BlockSpec((1,) + state_shape, lambda s: (order(s), 0, 0))],
        out_shape=[jax.ShapeDtypeStruct((n * L, out_w), F32), jax.ShapeDtypeStruct((n,) + state_shape, F32)],
        scratch_shapes=[pltpu.VMEM(state_shape, F32)],
        compiler_params=_cp(("arbitrary",)),
    )(*[a for a, _, _ in rows], *params, *adds)


def _scan_bwd(name, chunk_fn, L, n, nx, rev, rows, params, states, dy, state_shape, out_w, addends=None):
    order = _scan_order(n, nx, rev, True)
    dy_blocks = dy.shape[0] // L
    nr, npar = len(rows), len(params)
    adds = [] if addends is None else list(addends)

    def body(*refs):
        i = pl.program_id(0)
        ds_scr = refs[-1]
        rv = [r[...] for r in refs[:nr]]
        pv = [p[...] for p in refs[nr:nr + npar]]
        st_ref, dy_ref = refs[nr + npar], refs[nr + npar + 1]
        a_refs = refs[nr + npar + 2:nr + npar + 2 + len(adds)]
        o_refs = refs[nr + npar + 2 + len(adds):-1]
        p_refs = o_refs[nr:]

        @pl.when(i == 0)
        def _():
            ds_scr[...] = jnp.zeros_like(ds_scr)
            for p_ref in p_refs:
                p_ref[...] = jnp.zeros_like(p_ref)

        _, vjp = jax.vjp(functools.partial(chunk_fn, rev=rev), st_ref[0], *rv, *pv)
        dy_blk = jnp.where(order(i) < dy_blocks, dy_ref[...].astype(F32), 0.0)
        grads = vjp((ds_scr[...], dy_blk))
        ds_scr[...] = grads[0]
        for j, (o_ref, g) in enumerate(zip(o_refs[:nr], grads[1:1 + nr])):
            o_ref[...] = g + a_refs[j][...] if adds else g
        for p_ref, g in zip(p_refs, grads[1 + nr:]):
            p_ref[...] += g

    row_specs = [pl.BlockSpec((L, w), lambda s: (order(s), 0)) for _, w, _ in rows]
    res = pl.pallas_call(
        body, name=name, grid=(n,),
        in_specs=[pl.BlockSpec((L, w), lambda s, c=c: (order(s), c)) for _, w, c in rows] + [_full_spec(p) for p in params]
        + [pl.BlockSpec((1,) + state_shape, lambda s: (order(s), 0, 0)),
           pl.BlockSpec((L, out_w), lambda s: (jnp.minimum(order(s), dy_blocks - 1), 0))]
        + row_specs[:len(adds)],
        out_specs=row_specs + [_full_spec(p) for p in params],
        out_shape=[jax.ShapeDtypeStruct((n * L, w), F32) for _, w, _ in rows] + [jax.ShapeDtypeStruct(p.shape, F32) for p in params],
        scratch_shapes=[pltpu.VMEM(state_shape, F32)],
        compiler_params=_cp(("arbitrary",)),
    )(*[a for a, _, _ in rows], *params, states, dy, *adds)
    return res[:nr], res[nr:]


CONV_W = 1024
CONV_COLBLK = (0, 1, 4)


def _conv_specs(nb, src_blk):
    halo = TM // 8
    return [pl.BlockSpec((TM, CONV_W), lambda j, i: (i, src_blk(j))),
            pl.BlockSpec((8, CONV_W), lambda j, i: (jnp.maximum(i * halo - 1, 0), src_blk(j))),
            pl.BlockSpec((8, CONV_W), lambda j, i: (jnp.minimum(i * halo + halo, nb * halo - 1), src_blk(j)))]


def _conv_ext(i, nb, cur, prev, nxt):
    has_prev = jnp.logical_and(i > 0, i < nb - 1)
    has_next = i < nb - 2
    return jnp.concatenate([jnp.where(has_prev, prev, 0.0), cur, jnp.where(has_next, nxt, 0.0)], axis=0)


def _conv_taps(ext, w, flip):
    acc = None
    for j in range(SSD_K):
        wj = w[SSD_K - 1 - j:SSD_K - j, :] if flip else w[j:j + 1, :]
        term = wj * ext[6 + j:6 + j + TM, :]
        acc = term if acc is None else acc + term
    return acc


def _conv(name, src, w8, b1, nb, *, permuted_src, act, flip, out_dtype):
    src_blk = (lambda j: jnp.where(j == 2, CONV_COLBLK[2], j)) if permuted_src else (lambda j: j)

    def body(cur, prev, nxt, w_ref, b_ref, o_ref):
        ext = _conv_ext(pl.program_id(1), nb, cur[...].astype(F32), prev[...].astype(F32), nxt[...].astype(F32))
        acc = _conv_taps(ext, w_ref[...], flip)
        if act:
            acc = jax.nn.silu(acc + b_ref[...])
        o_ref[...] = acc.astype(o_ref.dtype)

    return pl.pallas_call(
        body, name=name, grid=(3, nb),
        in_specs=_conv_specs(nb, src_blk) + [pl.BlockSpec((8, CONV_W), lambda j, i: (0, j)), pl.BlockSpec((1, CONV_W), lambda j, i: (0, j))],
        out_specs=pl.BlockSpec((TM, CONV_W), lambda j, i: (i, j)),
        out_shape=jax.ShapeDtypeStruct((nb * TM, 3 * CONV_W), out_dtype),
        compiler_params=_cp(("parallel", "parallel")),
    )(src, src, src, w8, b1)


def _conv_bwd_pre(name, p1, w8, b1, dxbc_parts, nb):
    src_blk = lambda j: jnp.where(j == 2, CONV_COLBLK[2], j)
    xs_parts, bc_parts = dxbc_parts
    n_x, n_bc = len(xs_parts), len(bc_parts)
    x_blocks = [p.shape[0] // TM for p in xs_parts]

    def body(*refs):
        cur, prev, nxt, w_ref, b_ref = refs[:5]
        d_refs = refs[5:5 + n_x + n_bc]
        da_ref, dw_ref, db_ref = refs[5 + n_x + n_bc:]
        j, i = pl.program_id(0), pl.program_id(1)
        ext = _conv_ext(i, nb, cur[...], prev[...], nxt[...])
        acc = _conv_taps(ext, w_ref[...], False) + b_ref[...]
        dx = d_refs[0][...]
        for r, blocks in zip(d_refs[1:n_x], x_blocks[1:]):
            dx = dx + jnp.where(i < blocks, r[...], 0.0)
        dbc = jnp.concatenate([d_refs[n_x][...], d_refs[n_x + 1][...]], axis=1)
        dy = jnp.where(j == 2, dbc, dx)
        sg = jax.nn.sigmoid(acc)
        da = dy * (sg + acc * sg * (1.0 - sg))
        da_ref[...] = da

        @pl.when(i == 0)
        def _():
            dw_ref[...] = jnp.zeros_like(dw_ref)
            db_ref[...] = jnp.zeros_like(db_ref)

        rows = [jnp.sum(da * ext[6 + t:6 + t + TM, :], axis=0, keepdims=True) for t in range(SSD_K)]
        dw_ref[...] += jnp.concatenate(rows + [jnp.zeros((8 - SSD_K, CONV_W), F32)], axis=0)
        db_ref[...] += jnp.sum(da, axis=0, keepdims=True)

    x_specs = [pl.BlockSpec((TM, CONV_W), lambda j, i, b=b: (jnp.minimum(i, b - 1), jnp.minimum(j, 1))) for b in x_blocks]
    bc_specs = [pl.BlockSpec((TM, 512), lambda j, i: (i, 0)) for _ in bc_parts]
    return pl.pallas_call(
        body, name=name, grid=(3, nb),
        in_specs=_conv_specs(nb, src_blk) + [pl.BlockSpec((8, CONV_W), lambda j, i: (0, j)), pl.BlockSpec((1, CONV_W), lambda j, i: (0, j))]
        + x_specs + bc_specs,
        out_specs=[pl.BlockSpec((TM, CONV_W), lambda j, i: (i, j)), pl.BlockSpec((8, CONV_W), lambda j, i: (0, j)),
                   pl.BlockSpec((1, CONV_W), lambda j, i: (0, j))],
        out_shape=[jax.ShapeDtypeStruct((nb * TM, 3 * CONV_W), F32), jax.ShapeDtypeStruct((8, 3 * CONV_W), F32),
                   jax.ShapeDtypeStruct((1, 3 * CONV_W), F32)],
        compiler_params=_cp(("arbitrary", "arbitrary")),
    )(p1, p1, p1, w8, b1, *xs_parts, *bc_parts)


def _grid_block(a):
    nbv = TM // a
    blk_b = max(nbv, 8)
    return nbv, blk_b, blk_b // nbv


def _grid_spec(a, nb):
    _, blk_b, per = _grid_block(a)
    return pl.BlockSpec((a, blk_b, D), lambda i: (0, jnp.minimum(i, nb - 2) // per, 0))


def _grid_rows(v_ref, a, i):
    nbv, _, per = _grid_block(a)

    def pick(ph):
        return jnp.concatenate([v_ref[:, ph * nbv + t, :] for t in range(nbv)], axis=0)

    out = pick(0)
    for ph in range(1, per):
        out = jnp.where(i % per == ph, pick(ph), out)
    return out


def _loss_head(x, f, target, modp, g_final, rows_r):
    tview = target.reshape(rows_r, target.shape[0] // rows_r, D)
    nb = x.shape[0] // TM + 1

    def fn(x_, f_, tgt, modp_, g_):
        xn = x_ + _sel_mod(modp_, 0.0)[5:6] * f_
        err = _rms(xn) * g_ - tgt
        return 0.5 * jnp.sum(jnp.mean(err * err, axis=-1))

    def body(x_ref, f_ref, t_ref, m_ref, g_ref, l_ref, dx_ref, df_ref, dm_ref, dg_ref):
        i = pl.program_id(0)
        tgt = _grid_rows(t_ref, rows_r, i)
        l, vjp = jax.vjp(lambda a_, b_, c_, d_: fn(a_, b_, tgt, c_, d_), x_ref[...], f_ref[...], m_ref[...], g_ref[...])
        dx, df, dm, dg = vjp(jnp.ones((), F32))

        @pl.when(i == 0)
        def _():
            l_ref[...] = jnp.zeros_like(l_ref)
            dm_ref[...] = jnp.zeros_like(dm_ref)
            dg_ref[...] = jnp.zeros_like(dg_ref)

        l_ref[...] += jnp.reshape(l, (1, 1))
        dx_ref[...] = dx
        df_ref[...] = df.astype(df_ref.dtype)
        dm_ref[...] += dm
        dg_ref[...] += dg

    rowspec = pl.BlockSpec((TM, D), lambda i: (i, 0))
    return pl.pallas_call(
        body, name="loss_head", grid=(nb - 1,),
        in_specs=[rowspec, rowspec, _grid_spec(rows_r, nb), _full_spec(modp), _full_spec(g_final)],
        out_specs=[pl.BlockSpec((1, 1), lambda i: (0, 0)), rowspec, rowspec, _full_spec(modp), _full_spec(g_final)],
        out_shape=[jax.ShapeDtypeStruct((1, 1), F32), jax.ShapeDtypeStruct(x.shape, F32), jax.ShapeDtypeStruct(x.shape, MXU_DTYPE),
                   jax.ShapeDtypeStruct(modp.shape, F32), jax.ShapeDtypeStruct(g_final.shape, F32)],
        compiler_params=_cp(("arbitrary",)),
    )(x, f, tview, modp, g_final)


def _repack(name, shards, segs, wp):
    nd, kk, ws = shards.shape
    tr = 128
    used = sum(e - s for s, e in segs)

    def body(a_ref, o_ref):
        full = jnp.concatenate([a_ref[d].astype(F32) for d in range(nd)], axis=1)
        parts = [full[:, s:e] for s, e in segs]
        if wp > used:
            parts.append(jnp.zeros((tr, wp - used), F32))
        o_ref[...] = jnp.concatenate(parts, axis=1).astype(o_ref.dtype)

    return pl.pallas_call(
        body, name=name, grid=(kk // tr,),
        in_specs=[pl.BlockSpec((nd, tr, ws), lambda i: (0, i, 0))],
        out_specs=pl.BlockSpec((tr, wp), lambda i: (i, 0)),
        out_shape=jax.ShapeDtypeStruct((kk, wp), MXU_DTYPE),
        compiler_params=_cp(("parallel",)),
    )(shards)


def _unpack(name, dw, segs, ws, out_dtype):
    kk, wp = dw.shape
    tr = 128
    order = sorted(range(len(segs)), key=lambda i: segs[i][0])
    offs, o = [], 0
    for s, e in segs:
        offs.append(o)
        o += e - s

    def body(a_ref, o_ref):
        a = a_ref[...].astype(F32)
        full = jnp.concatenate([a[:, offs[i]:offs[i] + segs[i][1] - segs[i][0]] for i in order], axis=1)
        for d in range(NDEV):
            o_ref[d] = full[:, d * ws:(d + 1) * ws].astype(o_ref.dtype)

    return pl.pallas_call(
        body, name=name, grid=(kk // tr,),
        in_specs=[pl.BlockSpec((tr, wp), lambda i: (i, 0))],
        out_specs=pl.BlockSpec((NDEV, tr, ws), lambda i: (0, i, 0)),
        out_shape=jax.ShapeDtypeStruct((NDEV, kk, ws), out_dtype),
        compiler_params=_cp(("parallel",)),
    )(dw)


def _adam_math(w, g, m, v):
    m = ADAM_B1 * m + (1.0 - ADAM_B1) * g
    v = ADAM_B2 * v + (1.0 - ADAM_B2) * jnp.square(g)
    m_hat = m / (1.0 - ADAM_B1 ** ADAM_STEP)
    v_hat = v / (1.0 - ADAM_B2 ** ADAM_STEP)
    delta = -ADAM_LR * (m_hat / (jnp.sqrt(v_hat) + ADAM_EPS) + ADAM_WD * w)
    return delta, m, v


def _adam(name, w, parts, m, v, after):
    r, c = w.shape
    nsec, npart = len(parts), parts[0].shape[0]
    rs = r // nsec
    tr = _pick(rs, (256, 128, 64, 32, 16, 8)) if rs * c * 4 > (1 << 20) else rs
    tiles = rs // tr

    def body(w_ref, *refs):
        m_ref, v_ref, _, g_ref, d_ref, nm_ref, nv_ref = refs[nsec:]
        sec = pl.program_id(0) // tiles
        for a, p_ref in enumerate(refs[:nsec]):
            @pl.when(sec == a)
            def _(p_ref=p_ref):
                g = p_ref[0].astype(F32)
                for s in range(1, npart):
                    g = g + p_ref[s].astype(F32)
                delta, nm, nv = _adam_math(w_ref[...], g, m_ref[...], v_ref[...])
                g_ref[...], d_ref[...], nm_ref[...], nv_ref[...] = g, delta, nm, nv

    spec = pl.BlockSpec((tr, c), lambda i: (i, 0))
    part_specs = [pl.BlockSpec((npart, tr, c), lambda i, a=a: (0, jnp.clip(i - a * tiles, 0, tiles - 1), 0)) for a in range(nsec)]
    return pl.pallas_call(
        body, name=name, grid=(r // tr,),
        in_specs=[spec] + part_specs + [spec, spec, ANY],
        out_specs=[spec] * 4, out_shape=[jax.ShapeDtypeStruct((r, c), F32)] * 4,
        compiler_params=_cp(("parallel",)),
    )(w, *parts, m, v, after)


def _mod_fwd(c_all, mod_w):
    nl, _, ws = mod_w.shape

    def body(c_ref, w_ref, o_ref):
        o_ref[0] = _dot(jax.nn.silu(c_ref[...]), w_ref[0])

    return pl.pallas_call(
        body, name="mod_fwd", grid=(nl,),
        in_specs=[_full_spec(c_all), pl.BlockSpec((1, D, ws), lambda i: (i, 0, 0))],
        out_specs=pl.BlockSpec((1, 16, ws), lambda i: (i, 0, 0)),
        out_shape=jax.ShapeDtypeStruct((nl, 16, ws), F32),
        compiler_params=_cp(("parallel",)),
    )(c_all, mod_w)


def _mod_bwd(c_all, mod_w, dm):
    nl, _, ws = mod_w.shape

    def body(c_ref, w_ref, d_ref, dw_ref, dc_ref):
        dw_ref[0] = _dot_tn(jax.nn.silu(c_ref[...]), d_ref[0])
        dc_ref[0] = _dot_nt(d_ref[0], w_ref[0])

    return pl.pallas_call(
        body, name="mod_bwd", grid=(nl,),
        in_specs=[_full_spec(c_all), pl.BlockSpec((1, D, ws), lambda i: (i, 0, 0)), pl.BlockSpec((1, 16, ws), lambda i: (i, 0, 0))],
        out_specs=[pl.BlockSpec((1, D, ws), lambda i: (i, 0, 0)), pl.BlockSpec((1, 16, D), lambda i: (i, 0, 0))],
        out_shape=[jax.ShapeDtypeStruct((nl, D, ws), F32), jax.ShapeDtypeStruct((nl, 16, D), F32)],
        compiler_params=_cp(("parallel",)),
    )(c_all, mod_w, dm)


def _sum_parts(name, parts):
    npart, r, c = parts.shape

    def body(p_ref, o_ref):
        g = p_ref[0].astype(F32)
        for s in range(1, npart):
            g = g + p_ref[s].astype(F32)
        o_ref[...] = g

    return pl.pallas_call(body, name=name, out_shape=jax.ShapeDtypeStruct((r, c), F32), compiler_params=_cp())(parts)


MESH = pl.DeviceIdType.MESH
ANY = pl.BlockSpec(memory_space=pl.ANY)
N_PEERS = NDEV - 1


def _mesh_pos():
    return lax.axis_index("x"), lax.axis_index("y"), lax.axis_index("c")


def _slot(px, py, pc):
    return 4 * px + 2 * py + pc


def _two_level_gather(x_refs, o_refs, send_sems, recv_sems, local_sems):
    x, y, c = _mesh_pos()
    me, sibling = (x, y, c), (x, y, 1 - c)
    chips = [(1 - x, y), (x, 1 - y), (1 - x, 1 - y)]
    n = len(x_refs)

    def copy(a, k, block, to, src=None):
        dst = o_refs[a].at[_slot(*block)]
        return pltpu.make_async_remote_copy(src_ref=dst if src is None else src, dst_ref=dst, send_sem=send_sems.at[a, k],
                                            recv_sem=recv_sems.at[a, k], device_id=to, device_id_type=MESH)

    mine = [pltpu.make_async_copy(x_refs[a], o_refs[a].at[_slot(*me)], local_sems.at[a]) for a in range(n)]
    for cp in mine:
        cp.start()
    first = []
    for a in range(n):
        first.append(copy(a, 0, me, sibling, src=x_refs[a]))
        first += [copy(a, 1 + j, me, (*chip, c), src=x_refs[a]) for j, chip in enumerate(chips)]
    for cp in first:
        cp.start()
    passed = []
    for j, chip in enumerate(chips):
        for a in range(n):
            copy(a, 1 + j, (*chip, c), me).wait_recv()
            fwd = copy(a, 4 + j, (*chip, c), sibling)
            fwd.start()
            passed.append(fwd)
    for a in range(n):
        copy(a, 0, sibling, me).wait_recv()
        for j, chip in enumerate(chips):
            copy(a, 4 + j, (*chip, 1 - c), me).wait_recv()
    for cp in first + passed:
        cp.wait_send()
    for cp in mine:
        cp.wait()


def _ag_small(name, x):
    r, c = x.shape

    def body(x_ref, o_ref, send_sems, recv_sems, local_sems):
        _two_level_gather([x_ref], [o_ref], send_sems, recv_sems, local_sems)

    return pl.pallas_call(
        body, name=name, out_shape=jax.ShapeDtypeStruct((NDEV, r, c), x.dtype),
        in_specs=[pl.BlockSpec(memory_space=pltpu.VMEM)], out_specs=pl.BlockSpec(memory_space=pltpu.VMEM),
        scratch_shapes=[pltpu.SemaphoreType.DMA((1, N_PEERS)), pltpu.SemaphoreType.DMA((1, N_PEERS)), pltpu.SemaphoreType.DMA((1,))],
        compiler_params=pltpu.CompilerParams(vmem_limit_bytes=VMEM_LIMIT),
    )(x)


def _ag_big(name, shards):
    n = len(shards)

    def body(*refs):
        _two_level_gather(refs[:n], refs[n:2 * n], *refs[2 * n:])

    return pl.pallas_call(
        body, name=name, out_shape=[jax.ShapeDtypeStruct((NDEV,) + s.shape, s.dtype) for s in shards],
        in_specs=[ANY] * n, out_specs=[ANY] * n,
        scratch_shapes=[pltpu.SemaphoreType.DMA((n, N_PEERS)), pltpu.SemaphoreType.DMA((n, N_PEERS)), pltpu.SemaphoreType.DMA((n,))],
    )(*shards)


HBM = pl.BlockSpec(memory_space=pltpu.HBM)
SEM = pl.BlockSpec(memory_space=pltpu.SEMAPHORE)
EFFECT = pltpu.SideEffectType.DATAFLOW_SIDE_EFFECTING


def _peers(x, y, c):
    return [(k - 1, ((1 - x) if k & 4 else x, (1 - y) if k & 2 else y, (1 - c) if k & 1 else c)) for k in range(1, NDEV)]


def _xchg_copy(src_refs, land_refs, send_sems, recv_sems, a, k, peer, me, scatter):
    src = src_refs[a].at[_slot(*peer)] if scatter else src_refs[a]
    return pltpu.make_async_remote_copy(src_ref=src, dst_ref=land_refs[a].at[me], send_sem=send_sems.at[a * N_PEERS + k],
                                        recv_sem=recv_sems.at[a * N_PEERS + k], device_id=peer, device_id_type=MESH)


def _xchg_start(name, srcs, lands, deps, scatter):
    n, nd = len(srcs), len(deps)

    def body(*refs):
        src_refs, land_refs = refs[:n], refs[n:2 * n]
        send_sems, recv_sems, token = refs[2 * n + nd], refs[2 * n + nd + 1], refs[-1]
        x, y, c = _mesh_pos()
        me = _slot(x, y, c)
        for k, peer in _peers(x, y, c):
            for a in range(n):
                _xchg_copy(src_refs, land_refs, send_sems, recv_sems, a, k, peer, me, scatter).start()
        token[...] = jnp.zeros_like(token)

    res = pl.pallas_call(
        body, name=name,
        out_shape=(pltpu.SemaphoreType.DMA((n * N_PEERS,)), pltpu.SemaphoreType.DMA((n * N_PEERS,)),
                   *[pltpu.HBM(s.shape, s.dtype) for s in srcs], *[pltpu.HBM(s.shape, s.dtype) for s in lands],
                   jax.ShapeDtypeStruct((8, 128), F32)),
        in_specs=[HBM] * (2 * n) + [ANY] * nd,
        out_specs=(SEM, SEM, *([HBM] * (2 * n)), pl.BlockSpec(memory_space=pltpu.VMEM)),
        input_output_aliases={i: 2 + i for i in range(2 * n)},
        compiler_params=pltpu.CompilerParams(has_side_effects=EFFECT),
    )(*[pltpu.with_memory_space_constraint(s, pltpu.HBM) for s in srcs],
      *[pltpu.with_memory_space_constraint(s, pltpu.HBM) for s in lands], *deps)
    return res[0], res[1], res[2:2 + n], res[2 + n:2 + 2 * n], res[-1]


def _xchg_wait(name, send_sems, recv_sems, srcs, lands, after, scatter):
    n = len(srcs)

    def body(*refs):
        src_refs, land_refs = refs[:n], refs[n:2 * n]
        s_sems, r_sems = refs[2 * n], refs[2 * n + 1]
        x, y, c = _mesh_pos()
        me = _slot(x, y, c)
        for k, peer in _peers(x, y, c):
            for a in range(n):
                cp = _xchg_copy(src_refs, land_refs, s_sems, r_sems, a, k, peer, me, scatter)
                cp.wait_send()
                cp.wait_recv()

    res = pl.pallas_call(
        body, name=name,
        out_shape=[pltpu.HBM(s.shape, s.dtype) for s in srcs] + [pltpu.HBM(s.shape, s.dtype) for s in lands],
        in_specs=[HBM] * (2 * n) + [SEM, SEM, ANY], out_specs=[HBM] * (2 * n),
        input_output_aliases={i: i for i in range(2 * n)},
        compiler_params=pltpu.CompilerParams(has_side_effects=EFFECT),
    )(*srcs, *lands, send_sems, recv_sems, after)
    return res[n:]


def _landing(name, srcs, me, scatter):
    shapes = [s.shape[-2:] for s in srcs]

    def body(me_ref, *refs):
        for s_ref, o_ref in zip(refs[:len(srcs)], refs[len(srcs):]):
            o_ref[...] = s_ref[...].reshape(o_ref.shape)

    def slot_spec(r, c):
        return pl.BlockSpec((1, r, c), lambda i, me_ref: (me_ref[0], 0, 0))

    return pl.pallas_call(
        body, name=name, out_shape=[jax.ShapeDtypeStruct((NDEV, r, c), s.dtype) for s, (r, c) in zip(srcs, shapes)],
        grid_spec=pltpu.PrefetchScalarGridSpec(
            num_scalar_prefetch=1, grid=(1,),
            in_specs=[slot_spec(r, c) if scatter else pl.BlockSpec((r, c), lambda i, me_ref: (0, 0)) for r, c in shapes],
            out_specs=[slot_spec(r, c) for r, c in shapes]),
        compiler_params=_cp(("arbitrary",)),
    )(jnp.reshape(me, (1,)).astype(jnp.int32), *srcs)


STAGES = ("l0_mixer", "l0_ffn", "l1_mixer", "l1_ffn")
STAGE_LAYOUT = {"l0_mixer": (AB_SEGS, AB_P), "l1_mixer": (SSD_SEGS, SSD_P_W)}


class _Exchange:
    def __init__(self, shards, me):
        self.shards, self.me = shards, me
        self.pending, self.pending_grads, self.recv = {}, None, {}

    def _layout(self, stage):
        ws = self.shards[stage][0].shape[-1]
        return STAGE_LAYOUT.get(stage, (((0, NDEV * ws),), NDEV * ws)) + (ws,)

    def _start_gather(self, stage, deps):
        srcs = list(self.shards[stage])
        lands = _landing("own_" + stage, srcs, self.me, False)
        return _xchg_start("gather_start_" + stage, srcs, lands, deps, False)

    def get(self, stage, dep, thread):
        i = STAGES.index(stage)
        if i == 0:
            g_in, g_out = _ag_big("gather_" + stage, list(self.shards[stage]))
            ahead, deps = STAGES[1:3], [g_out, dep]
        else:
            ss, rs, srcs, lands, _ = self.pending.pop(stage)
            g_in, g_out = _xchg_wait("gather_wait_" + stage, ss, rs, srcs, lands, dep, False)
            ahead, deps = STAGES[i + 2:i + 3], [g_out]
        for nxt in ahead:
            self.pending[nxt] = self._start_gather(nxt, deps)
            deps = [self.pending[nxt][4]]
            thread = thread + self.pending[nxt][4][0, 0]
        segs, wp, _ = self._layout(stage)
        return _repack("repack_" + stage, g_in, segs, wp), g_out.reshape(-1, D), thread

    def put(self, stage, d_in, d_out, thread):
        segs, _, ws = self._layout(stage)
        parts = [_unpack("unpack_" + stage, d_in, segs, ws, MXU_DTYPE), d_out.reshape(NDEV, -1, D)]
        deps = [parts[0]]
        if self.pending_grads is not None:
            deps = [self.finish(parts[0])[0]]
        self.staged = (stage, parts)
        return thread if stage == STAGES[0] else thread + self.start_last(deps)[0, 0]

    def start_last(self, deps):
        stage, parts = self.staged
        lands = _landing("own_grad_" + stage, parts, self.me, True)
        self.pending_grads = (stage,) + _xchg_start("scatter_start_" + stage, parts, lands, deps, True)
        return self.pending_grads[5]

    def finish(self, after):
        stage, ss, rs, srcs, lands, _ = self.pending_grads
        self.recv[stage] = _xchg_wait("scatter_wait_" + stage, ss, rs, srcs, lands, after, True)
        self.pending_grads = None
        return self.recv[stage]


def _mm_swiglu(name, h, w_in):
    m, kk = h.shape
    f = w_in.shape[1] // 2
    tm = _pick(m, (272, 256, 128))

    def body(h_ref, wg_ref, wu_ref, pf_ref, act_ref):
        a = h_ref[...].astype(MXU_DTYPE)
        g = jnp.dot(a, wg_ref[...].astype(MXU_DTYPE), preferred_element_type=F32).astype(MXU_DTYPE)
        u = jnp.dot(a, wu_ref[...].astype(MXU_DTYPE), preferred_element_type=F32).astype(MXU_DTYPE)
        pf_ref[0] = g
        pf_ref[1] = u
        act_ref[...] = (jax.nn.silu(g.astype(F32)) * u.astype(F32)).astype(act_ref.dtype)

    return pl.pallas_call(
        body, name=name, grid=(m // tm,),
        in_specs=[pl.BlockSpec((tm, kk), lambda i: (i, 0)), pl.BlockSpec((kk, f), lambda i: (0, 0)), pl.BlockSpec((kk, f), lambda i: (0, 1))],
        out_specs=[pl.BlockSpec((2, tm, f), lambda i: (0, i, 0)), pl.BlockSpec((tm, f), lambda i: (i, 0))],
        out_shape=[jax.ShapeDtypeStruct((2, m, f), MXU_DTYPE), jax.ShapeDtypeStruct((m, f), MXU_DTYPE)],
        compiler_params=_cp(("parallel",)),
    )(h, w_in, w_in)


def _ffn_fwd(tag, h, w_in, w_out, nb, cb):
    pf, act = _mm_swiglu(tag + "_ffn_in", h, w_in)
    return pf, act, _mm(tag + "_ffn_out", act, w_out, "nn", F32)


def _mm_swiglu_bwd(name, df, w_out, pf):
    m, kk = df.shape
    f = w_out.shape[0]
    tm = _pick(m, (272, 256, 128))

    def body(d_ref, w_ref, pf_ref, o_ref):
        dact = lax.dot_general(d_ref[...].astype(MXU_DTYPE), w_ref[...].astype(MXU_DTYPE), (((1,), (1,)), ((), ())),
                               preferred_element_type=F32)
        g, u = pf_ref[0].astype(F32), pf_ref[1].astype(F32)
        sg = jax.nn.sigmoid(g)
        o_ref[:, 0:f] = (dact * u * (sg * (1.0 + g * (1.0 - sg)))).astype(o_ref.dtype)
        o_ref[:, f:2 * f] = (dact * (g * sg)).astype(o_ref.dtype)

    return pl.pallas_call(
        body, name=name, grid=(m // tm,),
        in_specs=[pl.BlockSpec((tm, kk), lambda i: (i, 0)), pl.BlockSpec((f, kk), lambda i: (0, 0)), pl.BlockSpec((2, tm, f), lambda i: (0, i, 0))],
        out_specs=pl.BlockSpec((tm, 2 * f), lambda i: (i, 0)),
        out_shape=jax.ShapeDtypeStruct((m, 2 * f), MXU_DTYPE),
        compiler_params=_cp(("parallel",)),
    )(df, w_out, pf)


def _ffn_bwd(tag, h, pf, act, df, w_in, w_out, nb, cb):
    dw_out = _mm(tag + "_ffn_out_dw", act, df, "tn", MXU_DTYPE)
    dpf = _mm_swiglu_bwd(tag + "_ffn_out_dx", df, w_out, pf)
    dw_in = _mm(tag + "_ffn_in_dw", h, dpf, "tn", MXU_DTYPE)
    dh = _mm(tag + "_ffn_in_dx", dpf, w_in, "nt", MXU_DTYPE)
    return dw_out, dw_in, dh


def _local_step(x, ctx, target, mod, P, comm):
    T = x.shape[0]
    N = T + CTX
    nb, cb = N // TM, N // TM - 1
    R = T // GRID_W
    mod0, mod1 = mod[0], mod[1]
    ng = P["norm_g"]
    g00, g01, g10, g11 = ng[0, 0][None], ng[0, 1][None], ng[1, 0][None], ng[1, 1][None]
    pre = functools.partial(_fn_prenorm, a=0, b=1)
    rpre = functools.partial(_fn_resid_prenorm, gi=2, a=3, b=4)
    res5 = functools.partial(_fn_resid, gi=5)
    dirs = (("f", False), ("b", True))

    xc0 = _row_cat(x, ctx, nb)
    w_ab_in, w_ab_out, g00 = comm.get("l0_mixer", mod, g00)
    (h0,) = _rowwise("l0_prenorm", pre, nb, cb, [xc0], [g00, mod0], [(D, MXU_DTYPE)])
    p0 = _mm("l0_in", h0, w_ab_in, "nn", F32)
    gla_rows = [(p0, 512, 0), (p0, 256, 8), (p0, 256, 9), (p0, 128, 20)]
    gla_blk = _multi_chunk(_gla_chunk, GLA_L, TM // GLA_L, len(gla_rows))
    gla_par = {d: [P["ab_gate_w"][int(r)], P["ab_gate_b"][int(r)][None]] for d, r in dirs}
    gla_state = (GLA_H * GLA_DV, GLA_H * GLA_DK)
    o, st0 = None, {}
    for d, rev in dirs:
        o, st0[d] = _scan_fwd("gla_fwd_" + d, gla_blk, TM, nb, cb, rev, gla_rows, gla_par[d], gla_state, GLA_H * GLA_DV, o)
    n128, cb128 = N // GMLP_L, T // GMLP_L
    mix_rows = [_row(o, tm=GMLP_L)] + [_row(p0, 512, j, tm=GMLP_L) for j in (1, 2, 3)]
    mix_par = [P["ab_gla_norm_g"], P["ab_vnorm_g"], P["ab_spatial_w"].reshape(GMLP_G * GMLP_L, GMLP_L), P["ab_spatial_b"].T]
    (cat0,) = _rowwise("l0_mix", _fn_mixpost, n128, cb128, mix_rows, mix_par, [(D, MXU_DTYPE)], tm=GMLP_L)
    y0 = _mm("l0_out", cat0, w_ab_out, "nn", F32)
    w_fi0, w_fo0, g01 = comm.get("l0_ffn", y0, g01)
    x1, h1 = _rowwise("l0_ffn_prenorm", rpre, nb, cb, [xc0, _row(y0)], [g01, mod0, mod0], [(D, F32), (D, MXU_DTYPE)])
    pf0, act0, f0 = _ffn_fwd("l0", h1, w_fi0, w_fo0, nb, cb)
    w_ssd_in, w_ssd_out, g10 = comm.get("l1_mixer", f0, g10)
    x2p, h2 = _rowwise("l0_resid_l1_prenorm", functools.partial(_fn_resid_prenorm, gi=5, a=0, b=1), nb, cb,
                       [_row_grid(x1, R, nb), _row_grid(f0, R, nb)], [g10, mod0, mod1], [(D, F32), (D, MXU_DTYPE)])
    p1 = _mm("l1_in", h2, w_ssd_in, "nn", F32)
    conv_w8 = jnp.concatenate([P["ssd_conv_w"], jnp.zeros((8 - SSD_K, 3 * CONV_W), F32)], axis=0)
    xbc = _conv("l1_conv", p1, conv_w8, P["ssd_conv_b"], nb, permuted_src=True, act=True, flip=False, out_dtype=F32)
    ssd_rows = [(xbc, SSD_INNER, 0), (xbc, 512, 4), (xbc, 512, 5), (p1, 128, 40)]
    ssd_blk = _multi_chunk(_ssd_chunk, SSD_L, TM // SSD_L, len(ssd_rows))
    ssd_par = {d: [P["ssd_dt_bias"][int(r)][None], P["ssd_a_log"][int(r)][None]] for d, r in dirs}
    ssd_state = (SSD_N, SSD_INNER)
    ys, st1 = None, {}
    for d, rev in dirs:
        ys, st1[d] = _scan_fwd("ssd_fwd_" + d, ssd_blk, TM, nb, cb, rev, ssd_rows, ssd_par[d], ssd_state, SSD_INNER, ys)
    fin_rows = [_row(ys), _row(xbc, SSD_INNER, 0), _row(p1, SSD_INNER, 1)]
    fin_par = [P["ssd_d"], P["ssd_norm_g"]]
    yn, y1 = _rowwise("l1_finish_out", _fn_ssd_finish, cb, cb, fin_rows, fin_par, [(SSD_INNER, MXU_DTYPE)], post_mm=w_ssd_out)
    w_fi1, w_fo1, g11 = comm.get("l1_ffn", y1, g11)
    x3, h3 = _rowwise("l1_ffn_prenorm", rpre, cb, cb, [_row(x2p), _row(y1)], [g11, mod1, mod1], [(D, F32), (D, MXU_DTYPE)])
    pf1, act1, f1 = _ffn_fwd("l1", h3, w_fi1, w_fo1, cb, cb)
    loss, dx3, df1, dm1_j, d_final_g = _loss_head(x3, f1, target, mod1, P["final_norm_g"], R)

    dP = {"final_norm_g": d_final_g}
    dwo1, dwi1, dh3 = _ffn_bwd("l1", h3, pf1, act1, df1, w_fi1, w_fo1, cb, cb)
    g11 = comm.put("l1_ffn", dwi1, dwo1, g11)
    (dx2p_a, dy1), (dg11, dm1_a, dm1_b) = _rowwise_vjp(
        "l1_ffn_prenorm_bwd", rpre, cb, cb, [_row(x2p), _row(y1)], [g11, mod1, mod1], [_row(dx3), _row(dh3)],
        [(0, F32, None), (1, MXU_DTYPE, None)])
    d_ssd_out = _mm("l1_out_dw", yn, dy1, "tn", MXU_DTYPE)
    (dys, dxs, dz), (dP["ssd_d"], dP["ssd_norm_g"]) = _rowwise_vjp(
        "l1_out_dx_finish_bwd", _fn_ssd_finish, cb, cb, fin_rows, fin_par, [_row(dy1)],
        [(0, F32, None), (1, F32, None), (2, MXU_DTYPE, None)], cot_mm=w_ssd_out)
    dssd, ddtb, dalog = None, [], []
    for d, rev in dirs:
        dssd, (ddtb_, dalog_) = _scan_bwd("ssd_bwd_" + d, ssd_blk, TM, nb, cb, rev, ssd_rows, ssd_par[d], st1[d], dys, ssd_state,
                                          SSD_INNER, dssd)
        ddtb.append(ddtb_); dalog.append(dalog_)
    dx_s, db_s, dc_s, dtl = dssd
    dP["ssd_dt_bias"] = jnp.concatenate(ddtb, axis=0)
    dP["ssd_a_log"] = jnp.concatenate(dalog, axis=0)
    dacc, dcw8, dP["ssd_conv_b"] = _conv_bwd_pre("l1_conv_bwd", p1, conv_w8, P["ssd_conv_b"], ([dx_s, dxs], [db_s, dc_s]), nb)
    dP["ssd_conv_w"] = dcw8[:SSD_K]
    dpc = _conv("l1_conv_dx", dacc, conv_w8, jnp.zeros((1, 3 * CONV_W), F32), nb, permuted_src=False, act=False, flip=True,
                out_dtype=MXU_DTYPE)
    cat1 = functools.partial(_fn_concat, sums=(1, 1, 1, 1), pad=SSD_P_W - 5248)
    (dp1,) = _rowwise("l1_dp", cat1, nb, cb, [_row(dpc, SSD_INNER, 0), _row(dz, valid=cb), _row(dpc, 1024, 2), _row(dtl)],
                      [], [(SSD_P_W, MXU_DTYPE)])
    g10 = comm.put("l1_mixer", _mm("l1_in_dw", h2, dp1, "tn", F32), d_ssd_out, g10)
    dh2 = _mm("l1_in_dx", dp1, w_ssd_in, "nt", MXU_DTYPE)
    (dx2p,), (dg10, dm1_f) = _rowwise_vjp("l1_prenorm_bwd", pre, nb, cb, [_row(x2p)], [g10, mod1], [_row(dh2)],
                                          [(0, F32, _row(dx2p_a, valid=cb))])

    (dx1_a, df0), (dm0_e,) = _rowwise_vjp("l0_resid_bwd", res5, nb, cb, [_row(x1), _row(f0)], [mod0],
                                          [_row_grid(dx2p, GRID_W, nb)], [(0, F32, None), (1, MXU_DTYPE, None)])
    dwo0, dwi0, dh1 = _ffn_bwd("l0", h1, pf0, act0, df0, w_fi0, w_fo0, nb, cb)
    g01 = comm.put("l0_ffn", dwi0, dwo0, g01)
    (dxc0_a, dy0), (dg01, dm0_a, dm0_b) = _rowwise_vjp(
        "l0_ffn_prenorm_bwd", rpre, nb, cb, [xc0, _row(y0)], [g01, mod0, mod0], [_row(dx1_a), _row(dh1)],
        [(0, F32, None), (1, MXU_DTYPE, None)])
    d_ab_out = _mm("l0_out_dw", cat0, dy0, "tn", MXU_DTYPE)
    dcat0 = _mm("l0_out_dx", dy0, w_ab_out, "nt", MXU_DTYPE)
    (do, dr, du, dgm), (dP["ab_gla_norm_g"], dP["ab_vnorm_g"], dsw, dsb_t) = _rowwise_vjp(
        "l0_mix_bwd", _fn_mixpost, n128, cb128, mix_rows, mix_par, [_row(dcat0, tm=GMLP_L)],
        [(0, F32, None), (1, MXU_DTYPE, None), (2, MXU_DTYPE, None), (3, MXU_DTYPE, None)], tm=GMLP_L)
    dP["ab_spatial_w"] = dsw.reshape(GMLP_G, GMLP_L, GMLP_L)
    dP["ab_spatial_b"] = dsb_t.T
    gl, dgw, dgb = None, [], []
    for d, rev in dirs:
        gl, (dgw_, dgb_) = _scan_bwd("gla_bwd_" + d, gla_blk, TM, nb, cb, rev, gla_rows, gla_par[d], st0[d], do,
                                     gla_state, GLA_H * GLA_DV, gl)
        dgw.append(dgw_[None]); dgb.append(dgb_)
    dP["ab_gate_w"] = jnp.concatenate(dgw, axis=0)
    dP["ab_gate_b"] = jnp.concatenate(dgb, axis=0)
    cat0f = functools.partial(_fn_concat, sums=(1,) * 7, pad=AB_P - 2688)
    (dp0,) = _rowwise("l0_dp", cat0f, nb, cb, [_row(gl[0]), _row(dr), _row(du), _row(dgm), _row(gl[1]), _row(gl[2]), _row(gl[3])],
                      [], [(AB_P, MXU_DTYPE)])
    g00 = comm.put("l0_mixer", _mm("l0_in_dw", h0, dp0, "tn", F32), d_ab_out, g00)
    dh0 = _mm("l0_in_dx", dp0, w_ab_in, "nt", MXU_DTYPE)
    (grad_x,), (dg00, dm0_s) = _rowwise_vjp("l0_prenorm_bwd", pre, nb, cb, [xc0], [g00, mod0], [_row(dh0)],
                                            [(0, F32, _row(dxc0_a))], x_rows_only=True)
    dP["norm_g"] = jnp.concatenate([dg00, dg01, dg10, dg11], axis=0).reshape(2, 2, D)
    dmod = jnp.stack([dm0_s + dm0_a + dm0_b + dm0_e, dm1_f + dm1_a + dm1_b + dm1_j])
    return loss, grad_x, dmod, dP


WEIGHTS = ("c_ctx", "mod_w", "mod_b", "norm_g", "ffn_w_in", "ffn_w_out", "ab_w_in", "ab_gate_w", "ab_gate_b", "ab_gla_norm_g",
           "ab_vnorm_g", "ab_spatial_w", "ab_spatial_b", "ab_w_out", "ssd_w_in", "ssd_conv_w", "ssd_conv_b", "ssd_dt_bias",
           "ssd_a_log", "ssd_d", "ssd_norm_g", "ssd_w_out", "final_norm_g")
SMALL_SHARDED = ("norm_g", "ab_gate_w", "ab_gate_b", "ssd_conv_w", "ssd_conv_b", "ssd_norm_g")
SMALL = ("c_ctx", "mod_b", "norm_g", "ab_gate_w", "ab_gate_b", "ab_gla_norm_g", "ab_vnorm_g", "ab_spatial_w", "ab_spatial_b",
         "ssd_conv_w", "ssd_conv_b", "ssd_dt_bias", "ssd_a_log", "ssd_d", "ssd_norm_g", "final_norm_g")
LANES = 1024


def _pack(arrs, rows_multiple=8):
    flat = jnp.concatenate([a.reshape(-1).astype(F32) for a in arrs])
    rows = -(-flat.shape[0] // LANES)
    rows = -(-rows // rows_multiple) * rows_multiple
    return jnp.pad(flat, (0, rows * LANES - flat.shape[0])).reshape(rows, LANES)


def _unpack_flat(buf, shapes):
    lead = buf.shape[:-2]
    flat = buf.reshape(lead + (-1,))
    out, o = [], 0
    for s in shapes:
        n = math.prod(s)
        out.append(flat[..., o:o + n].reshape(lead + tuple(s)))
        o += n
    return out


def _unshard(g):
    g = jnp.moveaxis(g, 0, -2)
    return g.reshape(g.shape[:-2] + (g.shape[-2] * g.shape[-1],))


def _my_shard(full, me, ws):
    return lax.dynamic_slice_in_dim(full, me * ws, ws, axis=full.ndim - 1)


def _silu_vjp(cvec, dsc):
    def body(c_ref, d_ref, o_ref):
        _, vjp = jax.vjp(jax.nn.silu, c_ref[...])
        o_ref[...] = vjp(d_ref[...])[0]

    return pl.pallas_call(body, name="c_ctx_bwd", out_shape=jax.ShapeDtypeStruct(cvec.shape, F32), compiler_params=_cp())(cvec, dsc)


def kernel(x, c, ctx, c_ctx, mod_w, mod_b, norm_g, ffn_w_in, ffn_w_out, ab_w_in, ab_gate_w, ab_gate_b, ab_gla_norm_g, ab_vnorm_g, ab_spatial_w, ab_spatial_b, ab_w_out, ssd_w_in, ssd_conv_w, ssd_conv_b, ssd_dt_bias, ssd_a_log, ssd_d, ssd_norm_g, ssd_w_out, final_norm_g, loss_target, m_c_ctx, m_mod_w, m_mod_b, m_norm_g, m_ffn_w_in, m_ffn_w_out, m_ab_w_in, m_ab_gate_w, m_ab_gate_b, m_ab_gla_norm_g, m_ab_vnorm_g, m_ab_spatial_w, m_ab_spatial_b, m_ab_w_out, m_ssd_w_in, m_ssd_conv_w, m_ssd_conv_b, m_ssd_dt_bias, m_ssd_a_log, m_ssd_d, m_ssd_norm_g, m_ssd_w_out, m_final_norm_g, v_c_ctx, v_mod_w, v_mod_b, v_norm_g, v_ffn_w_in, v_ffn_w_out, v_ab_w_in, v_ab_gate_w, v_ab_gate_b, v_ab_gla_norm_g, v_ab_vnorm_g, v_ab_spatial_w, v_ab_spatial_b, v_ab_w_out, v_ssd_w_in, v_ssd_conv_w, v_ssd_conv_b, v_ssd_dt_bias, v_ssd_a_log, v_ssd_d, v_ssd_norm_g, v_ssd_w_out, v_final_norm_g):
    a = dict(locals())
    me = _slot(*_mesh_pos())
    ws_mod = mod_w.shape[-1]

    fwd_small = [c] + [a[k] for k in SMALL_SHARDED]
    g_small = _ag_small("gather_small", _pack(fwd_small))
    parts = _unpack_flat(g_small, [t.shape for t in fwd_small])
    c_rows = parts[0].reshape(NDEV, D)
    full = {k: _unshard(p) for k, p in zip(SMALL_SHARDED, parts[1:])}
    c_all = jnp.concatenate([c_rows, c_ctx[None], jnp.zeros((7, D), F32)], axis=0)
    m_all = _ag_small("gather_mod", _mod_fwd(c_all, mod_w).reshape(2 * 16, ws_mod)).reshape(NDEV, 2, 16, ws_mod)
    m_mine = lax.dynamic_index_in_dim(m_all, me, axis=2, keepdims=False)
    mx = jnp.moveaxis(m_mine, 0, 1).reshape(2, N_MOD, D) + mod_b.reshape(2, N_MOD, D)
    mc = jnp.moveaxis(m_all[:, :, 8, :], 0, 1).reshape(2, N_MOD, D) + mod_b.reshape(2, N_MOD, D)
    pad2 = jnp.zeros((2, 2, D), F32)
    mod = jnp.concatenate([mx, pad2, mc, pad2], axis=1)

    big = {"l0_mixer": (ab_w_in[0], ab_w_out[0]), "l0_ffn": (ffn_w_in[0], ffn_w_out[0]),
           "l1_mixer": (ssd_w_in[0], ssd_w_out[0]), "l1_ffn": (ffn_w_in[1], ffn_w_out[1])}
    comm = _Exchange({k: tuple(w.astype(MXU_DTYPE) for w in v) for k, v in big.items()}, me)
    P = {
        "norm_g": full["norm_g"], "ab_gate_w": full["ab_gate_w"][0], "ab_gate_b": full["ab_gate_b"][0],
        "ab_gla_norm_g": ab_gla_norm_g, "ab_vnorm_g": ab_vnorm_g, "ab_spatial_w": ab_spatial_w[0], "ab_spatial_b": ab_spatial_b[0],
        "ssd_conv_w": full["ssd_conv_w"][0], "ssd_conv_b": full["ssd_conv_b"], "ssd_dt_bias": ssd_dt_bias[0],
        "ssd_a_log": ssd_a_log[0], "ssd_d": ssd_d, "ssd_norm_g": full["ssd_norm_g"], "final_norm_g": final_norm_g[None],
    }

    loss, grad_x, dmod, dP = _local_step(x[0], ctx[0], loss_target[0], mod, P, comm)

    dmx, dmc = dmod[:, 0:N_MOD].reshape(2, N_MOD * D), dmod[:, 8:8 + N_MOD].reshape(2, N_MOD * D)
    small_names = ("ab_gate_w", "ab_gate_b", "ab_gla_norm_g", "ab_vnorm_g", "ab_spatial_w", "ab_spatial_b", "norm_g", "ssd_conv_w",
                   "ssd_conv_b", "ssd_dt_bias", "ssd_a_log", "ssd_d", "ssd_norm_g", "final_norm_g")
    bwd_small = [dP[k] for k in small_names] + [dmc, dmx]
    shapes = [t.shape for t in bwd_small]
    g_bwd = _ag_small("gather_small_grads", _pack(bwd_small))
    summed = _unpack_flat(_sum_parts("sum_small_grads", g_bwd), shapes)
    gfull = dict(zip(small_names, summed[:-2]))
    dmc_sum, dmx_sum = summed[-2], summed[-1]
    dmx_all = _unpack_flat(g_bwd, shapes)[-1]
    dmx_sh = jnp.moveaxis(_my_shard(dmx_all, me, ws_mod), 0, 1)
    dm = jnp.concatenate([dmx_sh, _my_shard(dmc_sum, me, ws_mod)[:, None, :], jnp.zeros((2, 7, ws_mod), F32)], axis=1)
    d_mod_w, dsc = _mod_bwd(c_all, mod_w, dm)
    dsc_ctx = (dsc[0, 8] + dsc[1, 8])[None]
    dsc_all = _ag_small("gather_c_ctx_grad", jnp.concatenate([dsc_ctx, jnp.zeros((7, D), F32)], axis=0))
    d_c_ctx = _silu_vjp(c_ctx[None], _sum_parts("sum_c_ctx_grad", dsc_all)[0:1])[0]

    g_small_w = {
        "c_ctx": d_c_ctx, "mod_b": dmx_sum + dmc_sum, "norm_g": gfull["norm_g"], "ab_gate_w": gfull["ab_gate_w"][None],
        "ab_gate_b": gfull["ab_gate_b"][None], "ab_gla_norm_g": gfull["ab_gla_norm_g"], "ab_vnorm_g": gfull["ab_vnorm_g"],
        "ab_spatial_w": gfull["ab_spatial_w"][None], "ab_spatial_b": gfull["ab_spatial_b"][None], "ssd_conv_w": gfull["ssd_conv_w"][None],
        "ssd_conv_b": gfull["ssd_conv_b"], "ssd_dt_bias": gfull["ssd_dt_bias"][None], "ssd_a_log": gfull["ssd_a_log"][None],
        "ssd_d": gfull["ssd_d"], "ssd_norm_g": gfull["ssd_norm_g"], "final_norm_g": gfull["final_norm_g"][0],
    }
    for k in SMALL_SHARDED:
        g_small_w[k] = _my_shard(g_small_w[k], me, a[k].shape[-1])
    token = comm.start_last([d_c_ctx])
    res = _adam("adam_small", _pack([a[k] for k in SMALL]), [_pack([g_small_w[k] for k in SMALL])[None]],
                _pack([a["m_" + k] for k in SMALL]), _pack([a["v_" + k] for k in SMALL]), token)
    out = {k: vals for k, vals in zip(SMALL, zip(*[_unpack_flat(r, [a[k].shape for k in SMALL]) for r in res]))}

    def adam_big(name, w2d, parts, m2d, v2d, shape):
        return tuple(r.reshape(shape) for r in _adam(name, w2d, parts, m2d, v2d, token))

    def flat2(t):
        return t.reshape(-1, t.shape[-1])

    out["mod_w"] = adam_big("adam_mod_w", flat2(mod_w), [d_mod_w.reshape(1, -1, ws_mod)], flat2(m_mod_w), flat2(v_mod_w), mod_w.shape)

    for j, k in enumerate(("ffn_w_in", "ffn_w_out")):
        out[k] = adam_big("adam_" + k, flat2(a[k]), [comm.recv["l0_ffn"][j], comm.recv["l1_ffn"][j]], flat2(a["m_" + k]),
                          flat2(a["v_" + k]), a[k].shape)
    for j, k in enumerate(("ssd_w_in", "ssd_w_out")):
        out[k] = adam_big("adam_" + k, a[k][0], [comm.recv["l1_mixer"][j]], a["m_" + k][0], a["v_" + k][0], a[k].shape)
    recv_ab = comm.finish(out["ssd_w_out"][3])
    for j, k in enumerate(("ab_w_in", "ab_w_out")):
        out[k] = adam_big("adam_" + k, a[k][0], [recv_ab[j]], a["m_" + k][0], a["v_" + k][0], a[k].shape)

    loss_all = lax.psum(loss[0, 0], ("x", "y", "c"))
    return (loss_all, grad_x[None], *[out[k][0] for k in WEIGHTS], *[out[k][1] for k in WEIGHTS],
            *[out[k][2] for k in WEIGHTS], *[out[k][3] for k in WEIGHTS])
```

```python
import functools
import math

import jax
import jax.numpy as jnp
from jax import lax
from jax.experimental import pallas as pl
from jax.experimental.pallas import tpu as pltpu

F32 = jnp.float32
BF16 = jnp.bfloat16
MXU_DTYPE = jnp.bfloat16

D = 1024
NDEV = 8
N_MOD = 6
EPS = 1e-6
GRID_W = 64
CTX = 256
TM = 256
D_FF = 2816
GLA_H, GLA_DK, GLA_DV, GLA_LR, GLA_TAU, GLA_L = 4, 64, 128, 16, 16.0, 64
GMLP_G, GMLP_C, GMLP_L = 4, 128, 128
SSD_H, SSD_P, SSD_G, SSD_N, SSD_L, SSD_K = 32, 64, 4, 128, 128, 5
SSD_INNER = SSD_H * SSD_P
AB_IN = 2592
SSD_IN = 5184
AB_SEGS = ((256, 768), (1056, 1568), (1568, 2080), (2080, 2592), (0, 256), (800, 1056), (768, 800))
AB_P = 2816
SSD_SEGS = ((0, 2048), (3136, 5184), (2048, 2560), (2560, 3072), (3072, 3136))
SSD_P_W = 5376
VMEM_LIMIT = 56 * 1024 * 1024

ADAM_LR, ADAM_B1, ADAM_B2, ADAM_EPS, ADAM_WD, ADAM_STEP = 0.001, 0.9, 0.999, 1e-08, 0.01, 10


def _cp(sem=None, **kw):
    return pltpu.CompilerParams(dimension_semantics=sem, vmem_limit_bytes=VMEM_LIMIT, **kw)


def _dot(a, b, dims=(((1,), (0,)), ((), ()))):
    return lax.dot_general(a.astype(MXU_DTYPE), b.astype(MXU_DTYPE), dims, preferred_element_type=F32)


def _dot_nt(a, b):
    return _dot(a, b, (((1,), (1,)), ((), ())))


def _dot_tn(a, b):
    return _dot(a, b, (((0,), (0,)), ((), ())))


def _rms(x):
    return x * lax.rsqrt(jnp.mean(x * x, axis=-1, keepdims=True) + EPS)


def _pick(n, prefs):
    for p in prefs:
        if n % p == 0:
            return p
    return n


def _row(arr, width=None, colblk=0, tm=TM, valid=None):
    width = arr.shape[1] if width is None else width
    if valid is None:
        return ([arr], [pl.BlockSpec((tm, width), lambda i, c=colblk: (i, c))], lambda r: r[...].astype(F32), width)
    spec = pl.BlockSpec((tm, width), lambda i, c=colblk: (jnp.minimum(i, valid - 1), c))
    return ([arr], [spec], lambda r: jnp.where(pl.program_id(0) < valid, r[...].astype(F32), 0.0), width)


def _row_grid(arr, a, nb):
    n = arr.shape[0]
    b = (n - CTX) // a

    def load(v_ref, c_ref):
        i = pl.program_id(0)
        return jnp.where(i == nb - 1, c_ref[...], _grid_rows(v_ref, a, i))

    return ([arr.reshape(n // b, b, D), arr], [_grid_spec(a, nb), pl.BlockSpec((TM, D), lambda i: (nb - 1, 0))], load, D)


def _row_cat(x, ctx, nb):
    return ([x, ctx], [pl.BlockSpec((TM, D), lambda i: (jnp.minimum(i, nb - 2), 0)), pl.BlockSpec((TM, D), lambda i: (0, 0))],
            lambda x_ref, c_ref: jnp.where(pl.program_id(0) == nb - 1, c_ref[...], x_ref[...]), D)


def _operands(rows):
    return [a for r in rows for a in r[0]], [s for r in rows for s in r[1]]


def _load_rows(refs, rows):
    vals, k = [], 0
    for r in rows:
        vals.append(r[2](*refs[k:k + len(r[0])]))
        k += len(r[0])
    return vals


def _full_spec(p):
    nd = p.ndim
    return pl.BlockSpec(p.shape, lambda i, nd=nd: (0,) * nd)


def _rowwise(name, fn, n_blocks, ctx_blk, rows, params, outs, tm=TM, post_mm=None):
    arrs, specs = _operands(rows)
    nr, npar = len(arrs), len(params)
    extra = [] if post_mm is None else [post_mm]
    outs = list(outs) + [(w.shape[1], F32) for w in extra]

    def body(*refs):
        t = (pl.program_id(0) >= ctx_blk).astype(F32)
        rv = _load_rows(refs[:nr], rows)
        pv = [p[...] for p in refs[nr:nr + npar]]
        res = list(fn(t, rv, pv))
        o_refs = refs[nr + npar + len(extra):]
        if extra:
            res.append(_dot(res[0].astype(o_refs[0].dtype), refs[nr + npar][...]))
        for o_ref, o in zip(o_refs, res):
            o_ref[...] = o.astype(o_ref.dtype)

    return pl.pallas_call(
        body, name=name, grid=(n_blocks,),
        in_specs=specs + [_full_spec(p) for p in params + extra],
        out_specs=[pl.BlockSpec((tm, w), lambda i: (i, 0)) for w, _ in outs],
        out_shape=[jax.ShapeDtypeStruct((n_blocks * tm, w), dt) for w, dt in outs],
        compiler_params=_cp(("parallel",)),
    )(*arrs, *params, *extra)


def _rowwise_vjp(name, fn, n_blocks, ctx_blk, rows, params, douts, row_grads, tm=TM, x_rows_only=False, cot_mm=None):
    out_blocks = n_blocks - 1 if x_rows_only else n_blocks
    adds = [a for _, _, a in row_grads if a is not None]
    (r_arrs, r_specs), (d_arrs, d_specs), (a_arrs, a_specs) = _operands(rows), _operands(douts), _operands(adds)
    nr, npar, nd, na = len(r_arrs), len(params), len(d_arrs), len(a_arrs)
    extra = [] if cot_mm is None else [cot_mm[1]]

    def body(*refs):
        i = pl.program_id(0)
        t = (i >= ctx_blk).astype(F32)
        rv = _load_rows(refs[:nr], rows)
        pv = [p[...] for p in refs[nr:nr + npar]]
        dv = _load_rows(refs[nr + npar:nr + npar + nd], douts)
        av = _load_rows(refs[nr + npar + nd:nr + npar + nd + na], adds)
        o_refs = refs[nr + npar + nd + na + len(extra):]
        if extra:
            dv[cot_mm[0]] = _dot_nt(dv[cot_mm[0]], refs[nr + npar + nd + na][...])
        _, vjp = jax.vjp(lambda r, p: tuple(fn(t, r, p)), rv, pv)
        d_rows, d_params = vjp(tuple(dv))
        ai, grads = 0, []
        for ri, _, addend in row_grads:
            g = jnp.concatenate([d_rows[r] for r in ri], axis=1) if isinstance(ri, tuple) else d_rows[ri]
            if addend is not None:
                g = g + av[ai]
                ai += 1
            grads.append(g)

        @pl.when(i < out_blocks)
        def _():
            for o_ref, g in zip(o_refs, grads):
                o_ref[...] = g.astype(o_ref.dtype)

        p_refs = o_refs[len(row_grads):]

        @pl.when(i == 0)
        def _():
            for p_ref in p_refs:
                p_ref[...] = jnp.zeros_like(p_ref)

        for p_ref, g in zip(p_refs, d_params):
            p_ref[...] += g

    widths = [sum(rows[r][3] for r in ri) if isinstance(ri, tuple) else rows[ri][3] for ri, _, _ in row_grads]
    res = pl.pallas_call(
        body, name=name, grid=(n_blocks,),
        in_specs=r_specs + [_full_spec(p) for p in params] + d_specs + a_specs + [_full_spec(p) for p in extra],
        out_specs=[pl.BlockSpec((tm, w), lambda i: (jnp.minimum(i, out_blocks - 1), 0)) for w in widths] + [_full_spec(p) for p in params],
        out_shape=[jax.ShapeDtypeStruct((out_blocks * tm, w), dt) for w, (_, dt, _) in zip(widths, row_grads)]
        + [jax.ShapeDtypeStruct(p.shape, F32) for p in params],
        compiler_params=_cp(("arbitrary",)),
    )(*r_arrs, *params, *d_arrs, *a_arrs, *extra)
    return res[:len(row_grads)], res[len(row_grads):]


def _mm(name, a, b, mode, out_dtype):
    if mode == "nn":
        m, kk = a.shape
        n = b.shape[1]
    elif mode == "nt":
        m, kk = a.shape
        n = b.shape[0]
    else:
        kk, m = a.shape
        n = b.shape[1]
    if mode == "tn":
        tm = _pick(m, (1024, 1408, 512, 256, 128))
        tn = _pick(n, (768, 512, 256, 128))
        tk = kk
    else:
        tm = _pick(m, (1088, 1024, 768, 512, 384, 256, 128))
        tn = n if n <= 2816 else _pick(n, (1024, 768, 512, 256, 128))
        tk = kk if kk <= 2816 else _pick(kk, (2816, 1792, 1024, 768, 512, 256, 128))
    nk = kk // tk
    in_place = out_dtype == F32
    if mode == "nn":
        specs = [pl.BlockSpec((tm, tk), lambda i, j, k: (i, k)), pl.BlockSpec((tk, tn), lambda i, j, k: (k, j))]
        dims = (((1,), (0,)), ((), ()))
    elif mode == "nt":
        specs = [pl.BlockSpec((tm, tk), lambda i, j, k: (i, k)), pl.BlockSpec((tn, tk), lambda i, j, k: (j, k))]
        dims = (((1,), (1,)), ((), ()))
    else:
        specs = [pl.BlockSpec((tk, tm), lambda i, j, k: (k, i)), pl.BlockSpec((tk, tn), lambda i, j, k: (k, j))]
        dims = (((0,), (0,)), ((), ()))

    def body(a_ref, b_ref, o_ref, *scratch):
        part = lax.dot_general(a_ref[...].astype(MXU_DTYPE), b_ref[...].astype(MXU_DTYPE), dims, preferred_element_type=F32)
        if nk == 1:
            o_ref[...] = part.astype(o_ref.dtype)
        else:
            k = pl.program_id(2)
            acc = o_ref if in_place else scratch[0]

            @pl.when(k == 0)
            def _():
                acc[...] = part

            @pl.when(k > 0)
            def _():
                acc[...] += part

            if not in_place:
                @pl.when(k == nk - 1)
                def _():
                    o_ref[...] = acc[...].astype(o_ref.dtype)

    return pl.pallas_call(
        body, name=name, grid=(m // tm, n // tn, nk), in_specs=specs,
        out_specs=pl.BlockSpec((tm, tn), lambda i, j, k: (i, j)),
        out_shape=jax.ShapeDtypeStruct((m, n), out_dtype),
        scratch_shapes=[] if nk == 1 or in_place else [pltpu.VMEM((tm, tn), F32)],
        compiler_params=_cp(("parallel", "parallel", "arbitrary")),
    )(a, b)


def _sel_mod(modp, t):
    return modp[0:8] * (1.0 - t) + modp[8:16] * t


def _fn_prenorm(t, rows, params, *, a, b):
    (x,), (g, modp) = rows, params
    m = _sel_mod(modp, t)
    return ((_rms(x) * g) * (1.0 + m[b:b + 1]) + m[a:a + 1],)


def _fn_resid_prenorm(t, rows, params, *, gi, a, b):
    (x, y), (g, mod_a, mod_b) = rows, params
    ma, mb = _sel_mod(mod_a, t), _sel_mod(mod_b, t)
    xn = x + ma[gi:gi + 1] * y
    return xn, (_rms(xn) * g) * (1.0 + mb[b:b + 1]) + mb[a:a + 1]


def _fn_resid(t, rows, params, *, gi):
    (x, y), (mod_a,) = rows, params
    return (x + _sel_mod(mod_a, t)[gi:gi + 1] * y,)


def _fn_mixpost(t, rows, params):
    (o, r, u, g), (gla_g, vn_g, sw, sb_t) = rows, params
    a =jnp.concatenate([_rms(o[:, h * GLA_DV:(h + 1) * GLA_DV]) for h in range(GLA_H)], axis=1) * gla_g * jax.nn.silu(r)
    uu, vv = jax.nn.gelu(u), jax.nn.gelu(g)
    mu = jnp.mean(vv, axis=-1, keepdims=True)
    var = jnp.mean(jnp.square(vv - mu), axis=-1, keepdims=True)
    vn = ((vv - mu) * lax.rsqrt(var + EPS)) * vn_g
    s = jnp.concatenate(
        [_dot(sw[gi * GMLP_L:(gi + 1) * GMLP_L, :], vn[:, gi * GMLP_C:(gi + 1) * GMLP_C]) + sb_t[:, gi:gi + 1]
         for gi in range(GMLP_G)], axis=1)
    return (jnp.concatenate([a, uu * s], axis=1),)


def _expand_heads(row):
    first = lax.broadcasted_iota(jnp.int32, (1, 2 * SSD_P), 1) < SSD_P
    return jnp.concatenate([jnp.where(first, row[:, 2 * j:2 * j + 1], row[:, 2 * j + 1:2 * j + 2]) for j in range(SSD_H // 2)], axis=1)


def _fn_ssd_finish(t, rows, params):
    (y2, xs, z), (d_skip, norm_g) = rows, params
    d_full = _expand_heads(d_skip)
    y = (y2 + d_full * xs) * jax.nn.silu(z)
    gw = SSD_INNER // SSD_G
    return (jnp.concatenate([_rms(y[:, gi * gw:(gi + 1) * gw]) for gi in range(SSD_G)], axis=1) * norm_g,)


def _fn_concat(t, rows, params, *, sums, pad=0):
    out, i = [], 0
    for n in sums:
        acc = rows[i]
        for j in range(1, n):
            acc = acc + rows[i + j]
        out.append(acc)
        i += n
    if pad:
        out.append(jnp.zeros((out[0].shape[0], pad), F32))
    return (jnp.concatenate(out, axis=1),)


def _tri(n, rev):
    r = lax.broadcasted_iota(jnp.int32, (n, n), 0)
    c = lax.broadcasted_iota(jnp.int32, (n, n), 1)
    return (r <= c) if rev else (r >= c)


def _running_sum(x, rev):
    n, s = x.shape[0], 1
    while s < n:
        z = jnp.zeros((s, x.shape[1]), x.dtype)
        x = x + (jnp.concatenate([x[s:], z], axis=0) if rev else jnp.concatenate([z, x[:n - s]], axis=0))
        s *= 2
    return x


def _gla_chunk(S, v, k, q, tail, gw, gb, *, rev):
    L, H = GLA_L, GLA_H
    lr = tail[:, GLA_LR:2 * GLA_LR] if rev else tail[:, 0:GLA_LR]
    la = jax.nn.log_sigmoid(_dot(lr, gw) + gb) / GLA_TAU
    b = _running_sum(la, rev)
    b_last = b[0:1] if rev else b[L - 1:L]
    kd = k * jnp.exp(b_last - b)
    qd = (q * GLA_DK ** -0.5) * jnp.exp(b)
    ki = k * jnp.exp(-b)

    def same_head(shape, rows_per_head, cols_per_head):
        r = lax.broadcasted_iota(jnp.int32, shape, 0) // rows_per_head
        c = lax.broadcasted_iota(jnp.int32, shape, 1) // cols_per_head
        return r == c

    k_blk = jnp.where(same_head((H * L, H * GLA_DK), L, GLA_DK), jnp.concatenate([ki] * H, axis=0), 0.0)
    v_blk = jnp.where(same_head((H * L, H * GLA_DV), L, GLA_DV), jnp.concatenate([v] * H, axis=0), 0.0)
    row = lax.broadcasted_iota(jnp.int32, (L, H * L), 0)
    src = lax.broadcasted_iota(jnp.int32, (L, H * L), 1) % L
    sc = jnp.where((row <= src) if rev else (row >= src), _dot_nt(qd, k_blk), 0.0)
    o = _dot_nt(qd, S) + _dot(sc, v_blk)
    s_new = S * jnp.exp(b_last) + jnp.where(same_head(S.shape, GLA_DV, GLA_DK), _dot_tn(v, kd), 0.0)
    return s_new, o


def _ssd_chunk(S, x, bm, cm, tail, dtb, alog, *, rev):
    L = SSD_L
    msk = _tri(L, rev)
    raw = tail[:, SSD_H:2 * SSD_H] if rev else tail[:, 0:SSD_H]
    dt = jax.nn.softplus(raw + dtb)
    acum = _running_sum(dt * (-jnp.exp(alog)), rev)
    a_last = acum[0:1] if rev else acum[L - 1:L]
    wst = dt * jnp.exp(a_last - acum)
    eac = jnp.exp(acum)
    dec = jnp.exp(a_last)
    tr = jnp.concatenate([acum, dt, wst, jnp.zeros((L, L - 3 * SSD_H), F32)], axis=1).T
    acum_t, dt_t, wst_t = tr[0:SSD_H], tr[SSD_H:2 * SSD_H], tr[2 * SSD_H:3 * SSD_H]
    lane = lax.broadcasted_iota(jnp.int32, (1, 2 * SSD_P), 1)
    m0 = (lane < SSD_P).astype(F32)
    m1 = 1.0 - m0
    pairs_per_group = SSD_H // SSD_G // 2
    y_parts, s_parts = [], []
    for g in range(SSD_G):
        ns = slice(g * SSD_N, (g + 1) * SSD_N)
        bg, cg = bm[:, ns], cm[:, ns]
        cb = _dot_nt(cg, bg)
        bgt = bg.T
        gs = slice(g * pairs_per_group * 2 * SSD_P, (g + 1) * pairs_per_group * 2 * SSD_P)
        y_carry = _dot(cg, S[:, gs])
        for jj in range(pairs_per_group):
            j = g * pairs_per_group + jj
            ls = slice(j * 2 * SSD_P, (j + 1) * 2 * SSD_P)
            xp, sp = x[:, ls], S[:, ls]
            xm = jnp.concatenate([xp * m0, xp * m1], axis=0)
            lhs, bw = [], []
            for h in (2 * j, 2 * j + 1):
                seg = acum[:, h:h + 1] - acum_t[h:h + 1, :]
                lhs.append(cb * jnp.exp(jnp.where(msk, seg, -jnp.inf)) * dt_t[h:h + 1, :])
                bw.append(bgt * wst_t[h:h + 1, :])
            e_pair = eac[:, 2 * j:2 * j + 1] * m0 + eac[:, 2 * j + 1:2 * j + 2] * m1
            y_parts.append(_dot(jnp.concatenate(lhs, axis=1), xm) + y_carry[:, jj * 2 * SSD_P:(jj + 1) * 2 * SSD_P] * e_pair)
            d_pair = dec[:, 2 * j:2 * j + 1] * m0 + dec[:, 2 * j + 1:2 * j + 2] * m1
            s_parts.append(sp * d_pair + _dot(jnp.concatenate(bw, axis=1), xm))
    return jnp.concatenate(s_parts, axis=1), jnp.concatenate(y_parts, axis=1)


def _multi_chunk(chunk_fn, L, subs, nr):
    def fn(S, *args, rev):
        rows, params = args[:nr], args[nr:]
        ys = [None] * subs
        for j in (range(subs - 1, -1, -1) if rev else range(subs)):
            S, ys[j] = chunk_fn(S, *[r[j * L:(j + 1) * L] for r in rows], *params, rev=rev)
        return S, jnp.concatenate(ys, axis=0)

    return fn


def _scan_order(n, nx, rev, backward):
    nc = n - nx

    def fwd(s):
        return (n - 1 - s) if rev else jnp.where(s < nc, s + nx, s - nc)

    return (lambda s: fwd(n - 1 - s)) if backward else fwd


def _scan_fwd(name, chunk_fn, L, n, nx, rev, rows, params, state_shape, out_w, addend=None):
    order = _scan_order(n, nx, rev, False)
    nr, npar = len(rows), len(params)
    adds = [] if addend is None else [addend]

    def body(*refs):
        s_scr = refs[-1]

        @pl.when(pl.program_id(0) == 0)
        def _():
            s_scr[...] = jnp.zeros_like(s_scr)

        s_in = s_scr[...]
        y_ref, st_ref = refs[nr + npar + len(adds)], refs[nr + npar + len(adds) + 1]
        st_ref[0] = s_in
        s_new, y = chunk_fn(s_in, *[r[...] for r in refs[:nr]], *[p[...] for p in refs[nr:nr + npar]], rev=rev)
        y_ref[...] = y + refs[nr + npar][...] if adds else y
        s_scr[...] = s_new

    return pl.pallas_call(
        body, name=name, grid=(n,),
        in_specs=[pl.BlockSpec((L, w), lambda s, c=c: (order(s), c)) for _, w, c in rows] + [_full_spec(p) for p in params]
        + [pl.BlockSpec((L, out_w), lambda s: (order(s), 0)) for _ in adds],
        out_specs=[pl.BlockSpec((L, out_w), lambda s: (order(s), 0)),
                   pl.BlockSpec((1,) + state_shape, lambda s: (order(s), 0, 0))],
        out_shape=[jax.ShapeDtypeStruct((n * L, out_w), F32), jax.ShapeDtypeStruct((n,) + state_shape, F32)],
        scratch_shapes=[pltpu.VMEM(state_shape, F32)],
        compiler_params=_cp(("arbitrary",)),
    )(*[a for a, _, _ in rows], *params, *adds)


def _scan_bwd(name, chunk_fn, L, n, nx, rev, rows, params, states, dy, state_shape, out_w, addends=None):
    order = _scan_order(n, nx, rev, True)
    dy_blocks = dy.shape[0] // L
    nr, npar = len(rows), len(params)
    adds = [] if addends is None else list(addends)

    def body(*refs):
        i = pl.program_id(0)
        ds_scr = refs[-1]
        rv = [r[...] for r in refs[:nr]]
        pv = [p[...] for p in refs[nr:nr + npar]]
        st_ref, dy_ref = refs[nr + npar], refs[nr + npar + 1]
        a_refs = refs[nr + npar + 2:nr + npar + 2 + len(adds)]
        o_refs = refs[nr + npar + 2 + len(adds):-1]
        p_refs = o_refs[nr:]

        @pl.when(i == 0)
        def _():
            ds_scr[...] = jnp.zeros_like(ds_scr)
            for p_ref in p_refs:
                p_ref[...] = jnp.zeros_like(p_ref)

        _, vjp = jax.vjp(functools.partial(chunk_fn, rev=rev), st_ref[0], *rv, *pv)
        dy_blk = jnp.where(order(i) < dy_blocks, dy_ref[...].astype(F32), 0.0)
        grads = vjp((ds_scr[...], dy_blk))
        ds_scr[...] = grads[0]
        for j, (o_ref, g) in enumerate(zip(o_refs[:nr], grads[1:1 + nr])):
            o_ref[...] = g + a_refs[j][...] if adds else g
        for p_ref, g in zip(p_refs, grads[1 + nr:]):
            p_ref[...] += g

    row_specs = [pl.BlockSpec((L, w), lambda s: (order(s), 0)) for _, w, _ in rows]
    res = pl.pallas_call(
        body, name=name, grid=(n,),
        in_specs=[pl.BlockSpec((L, w), lambda s, c=c: (order(s), c)) for _, w, c in rows] + [_full_spec(p) for p in params]
        + [pl.BlockSpec((1,) + state_shape, lambda s: (order(s), 0, 0)),
           pl.BlockSpec((L, out_w), lambda s: (jnp.minimum(order(s), dy_blocks - 1), 0))]
        + row_specs[:len(adds)],
        out_specs=row_specs + [_full_spec(p) for p in params],
        out_shape=[jax.ShapeDtypeStruct((n * L, w), F32) for _, w, _ in rows] + [jax.ShapeDtypeStruct(p.shape, F32) for p in params],
        scratch_shapes=[pltpu.VMEM(state_shape, F32)],
        compiler_params=_cp(("arbitrary",)),
    )(*[a for a, _, _ in rows], *params, states, dy, *adds)
    return res[:nr], res[nr:]


CONV_W = 1024
CONV_COLBLK = (0, 1, 4)


def _conv_specs(nb, src_blk):
    halo = TM // 8
    return [pl.BlockSpec((TM, CONV_W), lambda j, i: (i, src_blk(j))),
            pl.BlockSpec((8, CONV_W), lambda j, i: (jnp.maximum(i * halo - 1, 0), src_blk(j))),
            pl.BlockSpec((8, CONV_W), lambda j, i: (jnp.minimum(i * halo + halo, nb * halo - 1), src_blk(j)))]


def _conv_ext(i, nb, cur, prev, nxt):
    has_prev = jnp.logical_and(i > 0, i < nb - 1)
    has_next = i < nb - 2
    return jnp.concatenate([jnp.where(has_prev, prev, 0.0), cur, jnp.where(has_next, nxt, 0.0)], axis=0)


def _conv_taps(ext, w, flip):
    acc = None
    for j in range(SSD_K):
        wj = w[SSD_K - 1 - j:SSD_K - j, :] if flip else w[j:j + 1, :]
        term = wj * ext[6 + j:6 + j + TM, :]
        acc = term if acc is None else acc + term
    return acc


def _conv(name, src, w8, b1, nb, *, permuted_src, act, flip, out_dtype):
    src_blk = (lambda j: jnp.where(j == 2, CONV_COLBLK[2], j)) if permuted_src else (lambda j: j)

    def body(cur, prev, nxt, w_ref, b_ref, o_ref):
        ext = _conv_ext(pl.program_id(1), nb, cur[...].astype(F32), prev[...].astype(F32), nxt[...].astype(F32))
        acc = _conv_taps(ext, w_ref[...], flip)
        if act:
            acc = jax.nn.silu(acc + b_ref[...])
        o_ref[...] = acc.astype(o_ref.dtype)

    return pl.pallas_call(
        body, name=name, grid=(3, nb),
        in_specs=_conv_specs(nb, src_blk) + [pl.BlockSpec((8, CONV_W), lambda j, i: (0, j)), pl.BlockSpec((1, CONV_W), lambda j, i: (0, j))],
        out_specs=pl.BlockSpec((TM, CONV_W), lambda j, i: (i, j)),
        out_shape=jax.ShapeDtypeStruct((nb * TM, 3 * CONV_W), out_dtype),
        compiler_params=_cp(("parallel", "parallel")),
    )(src, src, src, w8, b1)


def _conv_bwd_pre(name, p1, w8, b1, dxbc_parts, nb):
    src_blk = lambda j: jnp.where(j == 2, CONV_COLBLK[2], j)
    xs_parts, bc_parts = dxbc_parts
    n_x, n_bc = len(xs_parts), len(bc_parts)
    x_blocks = [p.shape[0] // TM for p in xs_parts]

    def body(*refs):
        cur, prev, nxt, w_ref, b_ref = refs[:5]
        d_refs = refs[5:5 + n_x + n_bc]
        da_ref, dw_ref, db_ref = refs[5 + n_x + n_bc:]
        j, i = pl.program_id(0), pl.program_id(1)
        ext = _conv_ext(i, nb, cur[...], prev[...], nxt[...])
        acc = _conv_taps(ext, w_ref[...], False) + b_ref[...]
        dx = d_refs[0][...]
        for r, blocks in zip(d_refs[1:n_x], x_blocks[1:]):
            dx = dx + jnp.where(i < blocks, r[...], 0.0)
        dbc = jnp.concatenate([d_refs[n_x][...], d_refs[n_x + 1][...]], axis=1)
        dy = jnp.where(j == 2, dbc, dx)
        sg = jax.nn.sigmoid(acc)
        da = dy * (sg + acc * sg * (1.0 - sg))
        da_ref[...] = da

        @pl.when(i == 0)
        def _():
            dw_ref[...] = jnp.zeros_like(dw_ref)
            db_ref[...] = jnp.zeros_like(db_ref)

        rows = [jnp.sum(da * ext[6 + t:6 + t + TM, :], axis=0, keepdims=True) for t in range(SSD_K)]
        dw_ref[...] += jnp.concatenate(rows + [jnp.zeros((8 - SSD_K, CONV_W), F32)], axis=0)
        db_ref[...] += jnp.sum(da, axis=0, keepdims=True)

    x_specs = [pl.BlockSpec((TM, CONV_W), lambda j, i, b=b: (jnp.minimum(i, b - 1), jnp.minimum(j, 1))) for b in x_blocks]
    bc_specs = [pl.BlockSpec((TM, 512), lambda j, i: (i, 0)) for _ in bc_parts]
    return pl.pallas_call(
        body, name=name, grid=(3, nb),
        in_specs=_conv_specs(nb, src_blk) + [pl.BlockSpec((8, CONV_W), lambda j, i: (0, j)), pl.BlockSpec((1, CONV_W), lambda j, i: (0, j))]
        + x_specs + bc_specs,
        out_specs=[pl.BlockSpec((TM, CONV_W), lambda j, i: (i, j)), pl.BlockSpec((8, CONV_W), lambda j, i: (0, j)),
                   pl.BlockSpec((1, CONV_W), lambda j, i: (0, j))],
        out_shape=[jax.ShapeDtypeStruct((nb * TM, 3 * CONV_W), F32), jax.ShapeDtypeStruct((8, 3 * CONV_W), F32),
                   jax.ShapeDtypeStruct((1, 3 * CONV_W), F32)],
        compiler_params=_cp(("arbitrary", "arbitrary")),
    )(p1, p1, p1, w8, b1, *xs_parts, *bc_parts)


def _grid_block(a):
    nbv = TM // a
    blk_b = max(nbv, 8)
    return nbv, blk_b, blk_b // nbv


def _grid_spec(a, nb):
    _, blk_b, per = _grid_block(a)
    return pl.BlockSpec((a, blk_b, D), lambda i: (0, jnp.minimum(i, nb - 2) // per, 0))


def _grid_rows(v_ref, a, i):
    nbv, _, per = _grid_block(a)

    def pick(ph):
        return jnp.concatenate([v_ref[:, ph * nbv + t, :] for t in range(nbv)], axis=0)

    out = pick(0)
    for ph in range(1, per):
        out = jnp.where(i % per == ph, pick(ph), out)
    return out


def _loss_head(x, f, target, modp, g_final, rows_r):
    tview = target.reshape(rows_r, target.shape[0] // rows_r, D)
    nb = x.shape[0] // TM + 1

    def fn(x_, f_, tgt, modp_, g_):
        xn = x_ + _sel_mod(modp_, 0.0)[5:6] * f_
        err = _rms(xn) * g_ - tgt
        return 0.5 * jnp.sum(jnp.mean(err * err, axis=-1))

    def body(x_ref, f_ref, t_ref, m_ref, g_ref, l_ref, dx_ref, df_ref, dm_ref, dg_ref):
        i = pl.program_id(0)
        tgt = _grid_rows(t_ref, rows_r, i)
        l, vjp = jax.vjp(lambda a_, b_, c_, d_: fn(a_, b_, tgt, c_, d_), x_ref[...], f_ref[...], m_ref[...], g_ref[...])
        dx, df, dm, dg = vjp(jnp.ones((), F32))

        @pl.when(i == 0)
        def _():
            l_ref[...] = jnp.zeros_like(l_ref)
            dm_ref[...] = jnp.zeros_like(dm_ref)
            dg_ref[...] = jnp.zeros_like(dg_ref)

        l_ref[...] += jnp.reshape(l, (1, 1))
        dx_ref[...] = dx
        df_ref[...] = df.astype(df_ref.dtype)
        dm_ref[...] += dm
        dg_ref[...] += dg

    rowspec = pl.BlockSpec((TM, D), lambda i: (i, 0))
    return pl.pallas_call(
        body, name="loss_head", grid=(nb - 1,),
        in_specs=[rowspec, rowspec, _grid_spec(rows_r, nb), _full_spec(modp), _full_spec(g_final)],
        out_specs=[pl.BlockSpec((1, 1), lambda i: (0, 0)), rowspec, rowspec, _full_spec(modp), _full_spec(g_final)],
        out_shape=[jax.ShapeDtypeStruct((1, 1), F32), jax.ShapeDtypeStruct(x.shape, F32), jax.ShapeDtypeStruct(x.shape, MXU_DTYPE),
                   jax.ShapeDtypeStruct(modp.shape, F32), jax.ShapeDtypeStruct(g_final.shape, F32)],
        compiler_params=_cp(("arbitrary",)),
    )(x, f, tview, modp, g_final)


def _repack(name, shards, segs, wp):
    nd, kk, ws = shards.shape
    tr = 128
    used = sum(e - s for s, e in segs)

    def body(a_ref, o_ref):
        full = jnp.concatenate([a_ref[d].astype(F32) for d in range(nd)], axis=1)
        parts = [full[:, s:e] for s, e in segs]
        if wp > used:
            parts.append(jnp.zeros((tr, wp - used), F32))
        o_ref[...] = jnp.concatenate(parts, axis=1).astype(o_ref.dtype)

    return pl.pallas_call(
        body, name=name, grid=(kk // tr,),
        in_specs=[pl.BlockSpec((nd, tr, ws), lambda i: (0, i, 0))],
        out_specs=pl.BlockSpec((tr, wp), lambda i: (i, 0)),
        out_shape=jax.ShapeDtypeStruct((kk, wp), MXU_DTYPE),
        compiler_params=_cp(("parallel",)),
    )(shards)


def _unpack(name, dw, segs, ws, out_dtype):
    kk, wp = dw.shape
    tr = 128
    order = sorted(range(len(segs)), key=lambda i: segs[i][0])
    offs, o = [], 0
    for s, e in segs:
        offs.append(o)
        o += e - s

    def body(a_ref, o_ref):
        a = a_ref[...].astype(F32)
        full = jnp.concatenate([a[:, offs[i]:offs[i] + segs[i][1] - segs[i][0]] for i in order], axis=1)
        for d in range(NDEV):
            o_ref[d] = full[:, d * ws:(d + 1) * ws].astype(o_ref.dtype)

    return pl.pallas_call(
        body, name=name, grid=(kk // tr,),
        in_specs=[pl.BlockSpec((tr, wp), lambda i: (i, 0))],
        out_specs=pl.BlockSpec((NDEV, tr, ws), lambda i: (0, i, 0)),
        out_shape=jax.ShapeDtypeStruct((NDEV, kk, ws), out_dtype),
        compiler_params=_cp(("parallel",)),
    )(dw)


def _adam_math(w, g, m, v):
    m = ADAM_B1 * m + (1.0 - ADAM_B1) * g
    v = ADAM_B2 * v + (1.0 - ADAM_B2) * jnp.square(g)
    m_hat = m / (1.0 - ADAM_B1 ** ADAM_STEP)
    v_hat = v / (1.0 - ADAM_B2 ** ADAM_STEP)
    delta = -ADAM_LR * (m_hat / (jnp.sqrt(v_hat) + ADAM_EPS) + ADAM_WD * w)
    return delta, m, v


def _adam(name, w, parts, m, v, after):
    r, c = w.shape
    nsec, npart = len(parts), parts[0].shape[0]
    rs = r // nsec
    tr = _pick(rs, (256, 128, 64, 32, 16, 8)) if rs * c * 4 > (1 << 20) else rs
    tiles = rs // tr

    def body(w_ref, *refs):
        m_ref, v_ref, _, g_ref, d_ref, nm_ref, nv_ref = refs[nsec:]
        sec = pl.program_id(0) // tiles
        for a, p_ref in enumerate(refs[:nsec]):
            @pl.when(sec == a)
            def _(p_ref=p_ref):
                g = p_ref[0].astype(F32)
                for s in range(1, npart):
                    g = g + p_ref[s].astype(F32)
                delta, nm, nv = _adam_math(w_ref[...], g, m_ref[...], v_ref[...])
                g_ref[...], d_ref[...], nm_ref[...], nv_ref[...] = g, delta, nm, nv

    spec = pl.BlockSpec((tr, c), lambda i: (i, 0))
    part_specs = [pl.BlockSpec((npart, tr, c), lambda i, a=a: (0, jnp.clip(i - a * tiles, 0, tiles - 1), 0)) for a in range(nsec)]
    return pl.pallas_call(
        body, name=name, grid=(r // tr,),
        in_specs=[spec] + part_specs + [spec, spec, ANY],
        out_specs=[spec] * 4, out_shape=[jax.ShapeDtypeStruct((r, c), F32)] * 4,
        compiler_params=_cp(("parallel",)),
    )(w, *parts, m, v, after)


def _mod_fwd(c_all, mod_w):
    nl, _, ws = mod_w.shape

    def body(c_ref, w_ref, o_ref):
        o_ref[0] = _dot(jax.nn.silu(c_ref[...]), w_ref[0])

    return pl.pallas_call(
        body, name="mod_fwd", grid=(nl,),
        in_specs=[_full_spec(c_all), pl.BlockSpec((1, D, ws), lambda i: (i, 0, 0))],
        out_specs=pl.BlockSpec((1, 16, ws), lambda i: (i, 0, 0)),
        out_shape=jax.ShapeDtypeStruct((nl, 16, ws), F32),
        compiler_params=_cp(("parallel",)),
    )(c_all, mod_w)


def _mod_bwd(c_all, mod_w, dm):
    nl, _, ws = mod_w.shape

    def body(c_ref, w_ref, d_ref, dw_ref, dc_ref):
        dw_ref[0] = _dot_tn(jax.nn.silu(c_ref[...]), d_ref[0])
        dc_ref[0] = _dot_nt(d_ref[0], w_ref[0])

    return pl.pallas_call(
        body, name="mod_bwd", grid=(nl,),
        in_specs=[_full_spec(c_all), pl.BlockSpec((1, D, ws), lambda i: (i, 0, 0)), pl.BlockSpec((1, 16, ws), lambda i: (i, 0, 0))],
        out_specs=[pl.BlockSpec((1, D, ws), lambda i: (i, 0, 0)), pl.BlockSpec((1, 16, D), lambda i: (i, 0, 0))],
        out_shape=[jax.ShapeDtypeStruct((nl, D, ws), F32), jax.ShapeDtypeStruct((nl, 16, D), F32)],
        compiler_params=_cp(("parallel",)),
    )(c_all, mod_w, dm)


def _sum_parts(name, parts):
    npart, r, c = parts.shape

    def body(p_ref, o_ref):
        g = p_ref[0].astype(F32)
        for s in range(1, npart):
            g = g + p_ref[s].astype(F32)
        o_ref[...] = g

    return pl.pallas_call(body, name=name, out_shape=jax.ShapeDtypeStruct((r, c), F32), compiler_params=_cp())(parts)


MESH = pl.DeviceIdType.MESH
ANY = pl.BlockSpec(memory_space=pl.ANY)
N_PEERS = NDEV - 1


def _mesh_pos():
    return lax.axis_index("x"), lax.axis_index("y"), lax.axis_index("c")


def _slot(px, py, pc):
    return 4 * px + 2 * py + pc


def _two_level_gather(x_refs, o_refs, send_sems, recv_sems, local_sems):
    x, y, c = _mesh_pos()
    me, sibling = (x, y, c), (x, y, 1 - c)
    chips = [(1 - x, y), (x, 1 - y), (1 - x, 1 - y)]
    n = len(x_refs)

    def copy(a, k, block, to, src=None):
        dst = o_refs[a].at[_slot(*block)]
        return pltpu.make_async_remote_copy(src_ref=dst if src is None else src, dst_ref=dst, send_sem=send_sems.at[a, k],
                                            recv_sem=recv_sems.at[a, k], device_id=to, device_id_type=MESH)

    mine = [pltpu.make_async_copy(x_refs[a], o_refs[a].at[_slot(*me)], local_sems.at[a]) for a in range(n)]
    for cp in mine:
        cp.start()
    first = []
    for a in range(n):
        first.append(copy(a, 0, me, sibling, src=x_refs[a]))
        first += [copy(a, 1 + j, me, (*chip, c), src=x_refs[a]) for j, chip in enumerate(chips)]
    for cp in first:
        cp.start()
    passed = []
    for j, chip in enumerate(chips):
        for a in range(n):
            copy(a, 1 + j, (*chip, c), me).wait_recv()
            fwd = copy(a, 4 + j, (*chip, c), sibling)
            fwd.start()
            passed.append(fwd)
    for a in range(n):
        copy(a, 0, sibling, me).wait_recv()
        for j, chip in enumerate(chips):
            copy(a, 4 + j, (*chip, 1 - c), me).wait_recv()
    for cp in first + passed:
        cp.wait_send()
    for cp in mine:
        cp.wait()


def _ag_small(name, x):
    r, c = x.shape

    def body(x_ref, o_ref, send_sems, recv_sems, local_sems):
        _two_level_gather([x_ref], [o_ref], send_sems, recv_sems, local_sems)

    return pl.pallas_call(
        body, name=name, out_shape=jax.ShapeDtypeStruct((NDEV, r, c), x.dtype),
        in_specs=[pl.BlockSpec(memory_space=pltpu.VMEM)], out_specs=pl.BlockSpec(memory_space=pltpu.VMEM),
        scratch_shapes=[pltpu.SemaphoreType.DMA((1, N_PEERS)), pltpu.SemaphoreType.DMA((1, N_PEERS)), pltpu.SemaphoreType.DMA((1,))],
        compiler_params=pltpu.CompilerParams(vmem_limit_bytes=VMEM_LIMIT),
    )(x)


def _ag_big(name, shards):
    n = len(shards)

    def body(*refs):
        _two_level_gather(refs[:n], refs[n:2 * n], *refs[2 * n:])

    return pl.pallas_call(
        body, name=name, out_shape=[jax.ShapeDtypeStruct((NDEV,) + s.shape, s.dtype) for s in shards],
        in_specs=[ANY] * n, out_specs=[ANY] * n,
        scratch_shapes=[pltpu.SemaphoreType.DMA((n, N_PEERS)), pltpu.SemaphoreType.DMA((n, N_PEERS)), pltpu.SemaphoreType.DMA((n,))],
    )(*shards)


HBM = pl.BlockSpec(memory_space=pltpu.HBM)
SEM = pl.BlockSpec(memory_space=pltpu.SEMAPHORE)
EFFECT = pltpu.SideEffectType.DATAFLOW_SIDE_EFFECTING


def _peers(x, y, c):
    return [(k - 1, ((1 - x) if k & 4 else x, (1 - y) if k & 2 else y, (1 - c) if k & 1 else c)) for k in range(1, NDEV)]


def _xchg_copy(src_refs, land_refs, send_sems, recv_sems, a, k, peer, me, scatter):
    src = src_refs[a].at[_slot(*peer)] if scatter else src_refs[a]
    return pltpu.make_async_remote_copy(src_ref=src, dst_ref=land_refs[a].at[me], send_sem=send_sems.at[a * N_PEERS + k],
                                        recv_sem=recv_sems.at[a * N_PEERS + k], device_id=peer, device_id_type=MESH)


def _xchg_start(name, srcs, lands, deps, scatter):
    n, nd = len(srcs), len(deps)

    def body(*refs):
        src_refs, land_refs = refs[:n], refs[n:2 * n]
        send_sems, recv_sems, token = refs[2 * n + nd], refs[2 * n + nd + 1], refs[-1]
        x, y, c = _mesh_pos()
        me = _slot(x, y, c)
        for k, peer in _peers(x, y, c):
            for a in range(n):
                _xchg_copy(src_refs, land_refs, send_sems, recv_sems, a, k, peer, me, scatter).start()
        token[...] = jnp.zeros_like(token)

    res = pl.pallas_call(
        body, name=name,
        out_shape=(pltpu.SemaphoreType.DMA((n * N_PEERS,)), pltpu.SemaphoreType.DMA((n * N_PEERS,)),
                   *[pltpu.HBM(s.shape, s.dtype) for s in srcs], *[pltpu.HBM(s.shape, s.dtype) for s in lands],
                   jax.ShapeDtypeStruct((8, 128), F32)),
        in_specs=[HBM] * (2 * n) + [ANY] * nd,
        out_specs=(SEM, SEM, *([HBM] * (2 * n)), pl.BlockSpec(memory_space=pltpu.VMEM)),
        input_output_aliases={i: 2 + i for i in range(2 * n)},
        compiler_params=pltpu.CompilerParams(has_side_effects=EFFECT),
    )(*[pltpu.with_memory_space_constraint(s, pltpu.HBM) for s in srcs],
      *[pltpu.with_memory_space_constraint(s, pltpu.HBM) for s in lands], *deps)
    return res[0], res[1], res[2:2 + n], res[2 + n:2 + 2 * n], res[-1]


def _xchg_wait(name, send_sems, recv_sems, srcs, lands, after, scatter):
    n = len(srcs)

    def body(*refs):
        src_refs, land_refs = refs[:n], refs[n:2 * n]
        s_sems, r_sems = refs[2 * n], refs[2 * n + 1]
        x, y, c = _mesh_pos()
        me = _slot(x, y, c)
        for k, peer in _peers(x, y, c):
            for a in range(n):
                cp = _xchg_copy(src_refs, land_refs, s_sems, r_sems, a, k, peer, me, scatter)
                cp.wait_send()
                cp.wait_recv()

    res = pl.pallas_call(
        body, name=name,
        out_shape=[pltpu.HBM(s.shape, s.dtype) for s in srcs] + [pltpu.HBM(s.shape, s.dtype) for s in lands],
        in_specs=[HBM] * (2 * n) + [SEM, SEM, ANY], out_specs=[HBM] * (2 * n),
        input_output_aliases={i: i for i in range(2 * n)},
        compiler_params=pltpu.CompilerParams(has_side_effects=EFFECT),
    )(*srcs, *lands, send_sems, recv_sems, after)
    return res[n:]


def _landing(name, srcs, me, scatter):
    shapes = [s.shape[-2:] for s in srcs]

    def body(me_ref, *refs):
        for s_ref, o_ref in zip(refs[:len(srcs)], refs[len(srcs):]):
            o_ref[...] = s_ref[...].reshape(o_ref.shape)

    def slot_spec(r, c):
        return pl.BlockSpec((1, r, c), lambda i, me_ref: (me_ref[0], 0, 0))

    return pl.pallas_call(
        body, name=name, out_shape=[jax.ShapeDtypeStruct((NDEV, r, c), s.dtype) for s, (r, c) in zip(srcs, shapes)],
        grid_spec=pltpu.PrefetchScalarGridSpec(
            num_scalar_prefetch=1, grid=(1,),
            in_specs=[slot_spec(r, c) if scatter else pl.BlockSpec((r, c), lambda i, me_ref: (0, 0)) for r, c in shapes],
            out_specs=[slot_spec(r, c) for r, c in shapes]),
        compiler_params=_cp(("arbitrary",)),
    )(jnp.reshape(me, (1,)).astype(jnp.int32), *srcs)


STAGES = ("l0_mixer", "l0_ffn", "l1_mixer", "l1_ffn")
STAGE_LAYOUT = {"l0_mixer": (AB_SEGS, AB_P), "l1_mixer": (SSD_SEGS, SSD_P_W)}


class _Exchange:
    def __init__(self, shards, me):
        self.shards, self.me = shards, me
        self.pending, self.pending_grads, self.recv = {}, None, {}

    def _layout(self, stage):
        ws = self.shards[stage][0].shape[-1]
        return STAGE_LAYOUT.get(stage, (((0, NDEV * ws),), NDEV * ws)) + (ws,)

    def _start_gather(self, stage, deps):
        srcs = list(self.shards[stage])
        lands = _landing("own_" + stage, srcs, self.me, False)
        return _xchg_start("gather_start_" + stage, srcs, lands, deps, False)

    def get(self, stage, dep, thread):
        i = STAGES.index(stage)
        if i == 0:
            g_in, g_out = _ag_big("gather_" + stage, list(self.shards[stage]))
            ahead, deps = STAGES[1:3], [g_out, dep]
        else:
            ss, rs, srcs, lands, _ = self.pending.pop(stage)
            g_in, g_out = _xchg_wait("gather_wait_" + stage, ss, rs, srcs, lands, dep, False)
            ahead, deps = STAGES[i + 2:i + 3], [g_out]
        for nxt in ahead:
            self.pending[nxt] = self._start_gather(nxt, deps)
            deps = [self.pending[nxt][4]]
            thread = thread + self.pending[nxt][4][0, 0]
        segs, wp, _ = self._layout(stage)
        return _repack("repack_" + stage, g_in, segs, wp), g_out.reshape(-1, D), thread

    def put(self, stage, d_in, d_out, thread):
        segs, _, ws = self._layout(stage)
        parts = [_unpack("unpack_" + stage, d_in, segs, ws, MXU_DTYPE), d_out.reshape(NDEV, -1, D)]
        deps = [parts[0]]
        if self.pending_grads is not None:
            deps = [self.finish(parts[0])[0]]
        self.staged = (stage, parts)
        return thread if stage == STAGES[0] else thread + self.start_last(deps)[0, 0]

    def start_last(self, deps):
        stage, parts = self.staged
        lands = _landing("own_grad_" + stage, parts, self.me, True)
        self.pending_grads = (stage,) + _xchg_start("scatter_start_" + stage, parts, lands, deps, True)
        return self.pending_grads[5]

    def finish(self, after):
        stage, ss, rs, srcs, lands, _ = self.pending_grads
        self.recv[stage] = _xchg_wait("scatter_wait_" + stage, ss, rs, srcs, lands, after, True)
        self.pending_grads = None
        return self.recv[stage]


def _mm_swiglu(name, h, w_in):
    m, kk = h.shape
    f = w_in.shape[1] // 2
    tm = _pick(m, (272, 256, 128))

    def body(h_ref, wg_ref, wu_ref, pf_ref, act_ref):
        a = h_ref[...].astype(MXU_DTYPE)
        g = jnp.dot(a, wg_ref[...].astype(MXU_DTYPE), preferred_element_type=F32).astype(MXU_DTYPE)
        u = jnp.dot(a, wu_ref[...].astype(MXU_DTYPE), preferred_element_type=F32).astype(MXU_DTYPE)
        pf_ref[0] = g
        pf_ref[1] = u
        act_ref[...] = (jax.nn.silu(g.astype(F32)) * u.astype(F32)).astype(act_ref.dtype)

    return pl.pallas_call(
        body, name=name, grid=(m // tm,),
        in_specs=[pl.BlockSpec((tm, kk), lambda i: (i, 0)), pl.BlockSpec((kk, f), lambda i: (0, 0)), pl.BlockSpec((kk, f), lambda i: (0, 1))],
        out_specs=[pl.BlockSpec((2, tm, f), lambda i: (0, i, 0)), pl.BlockSpec((tm, f), lambda i: (i, 0))],
        out_shape=[jax.ShapeDtypeStruct((2, m, f), MXU_DTYPE), jax.ShapeDtypeStruct((m, f), MXU_DTYPE)],
        compiler_params=_cp(("parallel",)),
    )(h, w_in, w_in)


def _ffn_fwd(tag, h, w_in, w_out, nb, cb):
    pf, act = _mm_swiglu(tag + "_ffn_in", h, w_in)
    return pf, act, _mm(tag + "_ffn_out", act, w_out, "nn", F32)


def _mm_swiglu_bwd(name, df, w_out, pf):
    m, kk = df.shape
    f = w_out.shape[0]
    tm = _pick(m, (272, 256, 128))

    def body(d_ref, w_ref, pf_ref, o_ref):
        dact = lax.dot_general(d_ref[...].astype(MXU_DTYPE), w_ref[...].astype(MXU_DTYPE), (((1,), (1,)), ((), ())),
                               preferred_element_type=F32)
        g, u = pf_ref[0].astype(F32), pf_ref[1].astype(F32)
        sg = jax.nn.sigmoid(g)
        o_ref[:, 0:f] = (dact * u * (sg * (1.0 + g * (1.0 - sg)))).astype(o_ref.dtype)
        o_ref[:, f:2 * f] = (dact * (g * sg)).astype(o_ref.dtype)

    return pl.pallas_call(
        body, name=name, grid=(m // tm,),
        in_specs=[pl.BlockSpec((tm, kk), lambda i: (i, 0)), pl.BlockSpec((f, kk), lambda i: (0, 0)), pl.BlockSpec((2, tm, f), lambda i: (0, i, 0))],
        out_specs=pl.BlockSpec((tm, 2 * f), lambda i: (i, 0)),
        out_shape=jax.ShapeDtypeStruct((m, 2 * f), MXU_DTYPE),
        compiler_params=_cp(("parallel",)),
    )(df, w_out, pf)


def _ffn_bwd(tag, h, pf, act, df, w_in, w_out, nb, cb):
    dw_out = _mm(tag + "_ffn_out_dw", act, df, "tn", MXU_DTYPE)
    dpf = _mm_swiglu_bwd(tag + "_ffn_out_dx", df, w_out, pf)
    dw_in = _mm(tag + "_ffn_in_dw", h, dpf, "tn", MXU_DTYPE)
    return dw_out, dw_in, dpf


def _local_step(x, ctx, target, mod, P, comm):
    T = x.shape[0]
    N = T + CTX
    nb, cb = N // TM, N // TM - 1
    R = T // GRID_W
    mod0, mod1 = mod[0], mod[1]
    ng = P["norm_g"]
    g00, g01, g10, g11 = ng[0, 0][None], ng[0, 1][None], ng[1, 0][None], ng[1, 1][None]
    pre = functools.partial(_fn_prenorm, a=0, b=1)
    rpre = functools.partial(_fn_resid_prenorm, gi=2, a=3, b=4)
    res5 = functools.partial(_fn_resid, gi=5)
    dirs = (("f", False), ("b", True))

    xc0 = _row_cat(x, ctx, nb)
    w_ab_in, w_ab_out, g00 = comm.get("l0_mixer", mod, g00)
    (h0,) = _rowwise("l0_prenorm", pre, nb, cb, [xc0], [g00, mod0], [(D, MXU_DTYPE)])
    p0 = _mm("l0_in", h0, w_ab_in, "nn", F32)
    gla_rows = [(p0, 512, 0), (p0, 256, 8), (p0, 256, 9), (p0, 128, 20)]
    gla_blk = _multi_chunk(_gla_chunk, GLA_L, TM // GLA_L, len(gla_rows))
    gla_par = {d: [P["ab_gate_w"][int(r)], P["ab_gate_b"][int(r)][None]] for d, r in dirs}
    gla_state = (GLA_H * GLA_DV, GLA_H * GLA_DK)
    o, st0 = None, {}
    for d, rev in dirs:
        o, st0[d] = _scan_fwd("gla_fwd_" + d, gla_blk, TM, nb, cb, rev, gla_rows, gla_par[d], gla_state, GLA_H * GLA_DV, o)
    n128, cb128 = N // GMLP_L, T // GMLP_L
    mix_rows = [_row(o, tm=GMLP_L)] + [_row(p0, 512, j, tm=GMLP_L) for j in (1, 2, 3)]
    mix_par = [P["ab_gla_norm_g"], P["ab_vnorm_g"], P["ab_spatial_w"].reshape(GMLP_G * GMLP_L, GMLP_L), P["ab_spatial_b"].T]
    (cat0,) = _rowwise("l0_mix", _fn_mixpost, n128, cb128, mix_rows, mix_par, [(D, MXU_DTYPE)], tm=GMLP_L)
    y0 = _mm("l0_out", cat0, w_ab_out, "nn", F32)
    w_fi0, w_fo0, g01 = comm.get("l0_ffn", y0, g01)
    x1, h1 = _rowwise("l0_ffn_prenorm", rpre, nb, cb, [xc0, _row(y0)], [g01, mod0, mod0], [(D, F32), (D, MXU_DTYPE)])
    pf0, act0, f0 = _ffn_fwd("l0", h1, w_fi0, w_fo0, nb, cb)
    w_ssd_in, w_ssd_out, g10 = comm.get("l1_mixer", f0, g10)
    x2p, h2 = _rowwise("l0_resid_l1_prenorm", functools.partial(_fn_resid_prenorm, gi=5, a=0, b=1), nb, cb,
                       [_row_grid(x1, R, nb), _row_grid(f0, R, nb)], [g10, mod0, mod1], [(D, F32), (D, MXU_DTYPE)])
    p1 = _mm("l1_in", h2, w_ssd_in, "nn", F32)
    conv_w8 = jnp.concatenate([P["ssd_conv_w"], jnp.zeros((8 - SSD_K, 3 * CONV_W), F32)], axis=0)
    xbc = _conv("l1_conv", p1, conv_w8, P["ssd_conv_b"], nb, permuted_src=True, act=True, flip=False, out_dtype=F32)
    ssd_rows = [(xbc, SSD_INNER, 0), (xbc, 512, 4), (xbc, 512, 5), (p1, 128, 40)]
    ssd_blk = _multi_chunk(_ssd_chunk, SSD_L, TM // SSD_L, len(ssd_rows))
    ssd_par = {d: [P["ssd_dt_bias"][int(r)][None], P["ssd_a_log"][int(r)][None]] for d, r in dirs}
    ssd_state = (SSD_N, SSD_INNER)
    ys, st1 = None, {}
    for d, rev in dirs:
        ys, st1[d] = _scan_fwd("ssd_fwd_" + d, ssd_blk, TM, nb, cb, rev, ssd_rows, ssd_par[d], ssd_state, SSD_INNER, ys)
    fin_rows = [_row(ys), _row(xbc, SSD_INNER, 0), _row(p1, SSD_INNER, 1)]
    fin_par = [P["ssd_d"], P["ssd_norm_g"]]
    yn, y1 = _rowwise("l1_finish_out", _fn_ssd_finish, cb, cb, fin_rows, fin_par, [(SSD_INNER, MXU_DTYPE)], post_mm=w_ssd_out)
    w_fi1, w_fo1, g11 = comm.get("l1_ffn", y1, g11)
    x3, h3 = _rowwise("l1_ffn_prenorm", rpre, cb, cb, [_row(x2p), _row(y1)], [g11, mod1, mod1], [(D, F32), (D, MXU_DTYPE)])
    pf1, act1, f1 = _ffn_fwd("l1", h3, w_fi1, w_fo1, cb, cb)
    loss, dx3, df1, dm1_j, d_final_g = _loss_head(x3, f1, target, mod1, P["final_norm_g"], R)

    dP = {"final_norm_g": d_final_g}
    dwo1, dwi1, dpf1 = _ffn_bwd("l1", h3, pf1, act1, df1, w_fi1, w_fo1, cb, cb)
    g11 = comm.put("l1_ffn", dwi1, dwo1, g11)
    (dx2p_a, dy1), (dg11, dm1_a, dm1_b) = _rowwise_vjp(
        "l1_ffn_in_dx_prenorm_bwd", rpre, cb, cb, [_row(x2p), _row(y1)], [g11, mod1, mod1], [_row(dx3), _row(dpf1)],
        [(0, F32, None), (1, MXU_DTYPE, None)], cot_mm=(1, w_fi1))
    d_ssd_out = _mm("l1_out_dw", yn, dy1, "tn", MXU_DTYPE)
    (dys, dxs, dz), (dP["ssd_d"], dP["ssd_norm_g"]) = _rowwise_vjp(
        "l1_out_dx_finish_bwd", _fn_ssd_finish, cb, cb, fin_rows, fin_par, [_row(dy1)],
        [(0, F32, None), (1, F32, None), (2, MXU_DTYPE, None)], cot_mm=(0, w_ssd_out))
    dssd, ddtb, dalog = None, [], []
    for d, rev in dirs:
        dssd, (ddtb_, dalog_) = _scan_bwd("ssd_bwd_" + d, ssd_blk, TM, nb, cb, rev, ssd_rows, ssd_par[d], st1[d], dys, ssd_state,
                                          SSD_INNER, dssd)
        ddtb.append(ddtb_); dalog.append(dalog_)
    dx_s, db_s, dc_s, dtl = dssd
    dP["ssd_dt_bias"] = jnp.concatenate(ddtb, axis=0)
    dP["ssd_a_log"] = jnp.concatenate(dalog, axis=0)
    dacc, dcw8, dP["ssd_conv_b"] = _conv_bwd_pre("l1_conv_bwd", p1, conv_w8, P["ssd_conv_b"], ([dx_s, dxs], [db_s, dc_s]), nb)
    dP["ssd_conv_w"] = dcw8[:SSD_K]
    dpc = _conv("l1_conv_dx", dacc, conv_w8, jnp.zeros((1, 3 * CONV_W), F32), nb, permuted_src=False, act=False, flip=True,
                out_dtype=MXU_DTYPE)
    cat1 = functools.partial(_fn_concat, sums=(1, 1, 1, 1), pad=SSD_P_W - 5248)
    (dp1,) = _rowwise("l1_dp", cat1, nb, cb, [_row(dpc, SSD_INNER, 0), _row(dz, valid=cb), _row(dpc, 1024, 2), _row(dtl)],
                      [], [(SSD_P_W, MXU_DTYPE)])
    g10 = comm.put("l1_mixer", _mm("l1_in_dw", h2, dp1, "tn", F32), d_ssd_out, g10)
    (dx2p,), (dg10, dm1_f) = _rowwise_vjp("l1_in_dx_prenorm_bwd", pre, nb, cb, [_row(x2p)], [g10, mod1], [_row(dp1)],
                                          [(0, F32, _row(dx2p_a, valid=cb))], cot_mm=(0, w_ssd_in))

    (dx1_a, df0), (dm0_e,) = _rowwise_vjp("l0_resid_bwd", res5, nb, cb, [_row(x1), _row(f0)], [mod0],
                                          [_row_grid(dx2p, GRID_W, nb)], [(0, F32, None), (1, MXU_DTYPE, None)])
    dwo0, dwi0, dpf0 = _ffn_bwd("l0", h1, pf0, act0, df0, w_fi0, w_fo0, nb, cb)
    g01 = comm.put("l0_ffn", dwi0, dwo0, g01)
    (dxc0_a, dy0), (dg01, dm0_a, dm0_b) = _rowwise_vjp(
        "l0_ffn_in_dx_prenorm_bwd", rpre, nb, cb, [xc0, _row(y0)], [g01, mod0, mod0], [_row(dx1_a), _row(dpf0)],
        [(0, F32, None), (1, MXU_DTYPE, None)], cot_mm=(1, w_fi0))
    d_ab_out = _mm("l0_out_dw", cat0, dy0, "tn", MXU_DTYPE)
    dcat0 = _mm("l0_out_dx", dy0, w_ab_out, "nt", MXU_DTYPE)
    (do, dr, du, dgm), (dP["ab_gla_norm_g"], dP["ab_vnorm_g"], dsw, dsb_t) = _rowwise_vjp(
        "l0_mix_bwd", _fn_mixpost, n128, cb128, mix_rows, mix_par, [_row(dcat0, tm=GMLP_L)],
        [(0, F32, None), (1, MXU_DTYPE, None), (2, MXU_DTYPE, None), (3, MXU_DTYPE, None)], tm=GMLP_L)
    dP["ab_spatial_w"] = dsw.reshape(GMLP_G, GMLP_L, GMLP_L)
    dP["ab_spatial_b"] = dsb_t.T
    gl, dgw, dgb = None, [], []
    for d, rev in dirs:
        gl, (dgw_, dgb_) = _scan_bwd("gla_bwd_" + d, gla_blk, TM, nb, cb, rev, gla_rows, gla_par[d], st0[d], do,
                                     gla_state, GLA_H * GLA_DV, gl)
        dgw.append(dgw_[None]); dgb.append(dgb_)
    dP["ab_gate_w"] = jnp.concatenate(dgw, axis=0)
    dP["ab_gate_b"] = jnp.concatenate(dgb, axis=0)
    cat0f = functools.partial(_fn_concat, sums=(1,) * 7, pad=AB_P - 2688)
    (dp0,) = _rowwise("l0_dp", cat0f, nb, cb, [_row(gl[0]), _row(dr), _row(du), _row(dgm), _row(gl[1]), _row(gl[2]), _row(gl[3])],
                      [], [(AB_P, MXU_DTYPE)])
    g00 = comm.put("l0_mixer", _mm("l0_in_dw", h0, dp0, "tn", F32), d_ab_out, g00)
    (grad_x,), (dg00, dm0_s) = _rowwise_vjp("l0_in_dx_prenorm_bwd", pre, nb, cb, [xc0], [g00, mod0], [_row(dp0)],
                                            [(0, F32, _row(dxc0_a))], x_rows_only=True, cot_mm=(0, w_ab_in))
    dP["norm_g"] = jnp.concatenate([dg00, dg01, dg10, dg11], axis=0).reshape(2, 2, D)
    dmod = jnp.stack([dm0_s + dm0_a + dm0_b + dm0_e, dm1_f + dm1_a + dm1_b + dm1_j])
    return loss, grad_x, dmod, dP


WEIGHTS = ("c_ctx", "mod_w", "mod_b", "norm_g", "ffn_w_in", "ffn_w_out", "ab_w_in", "ab_gate_w", "ab_gate_b", "ab_gla_norm_g",
           "ab_vnorm_g", "ab_spatial_w", "ab_spatial_b", "ab_w_out", "ssd_w_in", "ssd_conv_w", "ssd_conv_b", "ssd_dt_bias",
           "ssd_a_log", "ssd_d", "ssd_norm_g", "ssd_w_out", "final_norm_g")
SMALL_SHARDED = ("norm_g", "ab_gate_w", "ab_gate_b", "ssd_conv_w", "ssd_conv_b", "ssd_norm_g")
SMALL = ("c_ctx", "mod_b", "norm_g", "ab_gate_w", "ab_gate_b", "ab_gla_norm_g", "ab_vnorm_g", "ab_spatial_w", "ab_spatial_b",
         "ssd_conv_w", "ssd_conv_b", "ssd_dt_bias", "ssd_a_log", "ssd_d", "ssd_norm_g", "final_norm_g")
LANES = 1024


def _pack(arrs, rows_multiple=8):
    flat = jnp.concatenate([a.reshape(-1).astype(F32) for a in arrs])
    rows = -(-flat.shape[0] // LANES)
    rows = -(-rows // rows_multiple) * rows_multiple
    return jnp.pad(flat, (0, rows * LANES - flat.shape[0])).reshape(rows, LANES)


def _unpack_flat(buf, shapes):
    lead = buf.shape[:-2]
    flat = buf.reshape(lead + (-1,))
    out, o = [], 0
    for s in shapes:
        n = math.prod(s)
        out.append(flat[..., o:o + n].reshape(lead + tuple(s)))
        o += n
    return out


def _unshard(g):
    g = jnp.moveaxis(g, 0, -2)
    return g.reshape(g.shape[:-2] + (g.shape[-2] * g.shape[-1],))


def _my_shard(full, me, ws):
    return lax.dynamic_slice_in_dim(full, me * ws, ws, axis=full.ndim - 1)


def _silu_vjp(cvec, dsc):
    def body(c_ref, d_ref, o_ref):
        _, vjp = jax.vjp(jax.nn.silu, c_ref[...])
        o_ref[...] = vjp(d_ref[...])[0]

    return pl.pallas_call(body, name="c_ctx_bwd", out_shape=jax.ShapeDtypeStruct(cvec.shape, F32), compiler_params=_cp())(cvec, dsc)


def kernel(x, c, ctx, c_ctx, mod_w, mod_b, norm_g, ffn_w_in, ffn_w_out, ab_w_in, ab_gate_w, ab_gate_b, ab_gla_norm_g, ab_vnorm_g, ab_spatial_w, ab_spatial_b, ab_w_out, ssd_w_in, ssd_conv_w, ssd_conv_b, ssd_dt_bias, ssd_a_log, ssd_d, ssd_norm_g, ssd_w_out, final_norm_g, loss_target, m_c_ctx, m_mod_w, m_mod_b, m_norm_g, m_ffn_w_in, m_ffn_w_out, m_ab_w_in, m_ab_gate_w, m_ab_gate_b, m_ab_gla_norm_g, m_ab_vnorm_g, m_ab_spatial_w, m_ab_spatial_b, m_ab_w_out, m_ssd_w_in, m_ssd_conv_w, m_ssd_conv_b, m_ssd_dt_bias, m_ssd_a_log, m_ssd_d, m_ssd_norm_g, m_ssd_w_out, m_final_norm_g, v_c_ctx, v_mod_w, v_mod_b, v_norm_g, v_ffn_w_in, v_ffn_w_out, v_ab_w_in, v_ab_gate_w, v_ab_gate_b, v_ab_gla_norm_g, v_ab_vnorm_g, v_ab_spatial_w, v_ab_spatial_b, v_ab_w_out, v_ssd_w_in, v_ssd_conv_w, v_ssd_conv_b, v_ssd_dt_bias, v_ssd_a_log, v_ssd_d, v_ssd_norm_g, v_ssd_w_out, v_final_norm_g):
    a = dict(locals())
    me = _slot(*_mesh_pos())
    ws_mod = mod_w.shape[-1]

    fwd_small = [c] + [a[k] for k in SMALL_SHARDED]
    g_small = _ag_small("gather_small", _pack(fwd_small))
    parts = _unpack_flat(g_small, [t.shape for t in fwd_small])
    c_rows = parts[0].reshape(NDEV, D)
    full = {k: _unshard(p) for k, p in zip(SMALL_SHARDED, parts[1:])}
    c_all = jnp.concatenate([c_rows, c_ctx[None], jnp.zeros((7, D), F32)], axis=0)
    m_all = _ag_small("gather_mod", _mod_fwd(c_all, mod_w).reshape(2 * 16, ws_mod)).reshape(NDEV, 2, 16, ws_mod)
    m_mine = lax.dynamic_index_in_dim(m_all, me, axis=2, keepdims=False)
    mx = jnp.moveaxis(m_mine, 0, 1).reshape(2, N_MOD, D) + mod_b.reshape(2, N_MOD, D)
    mc = jnp.moveaxis(m_all[:, :, 8, :], 0, 1).reshape(2, N_MOD, D) + mod_b.reshape(2, N_MOD, D)
    pad2 = jnp.zeros((2, 2, D), F32)
    mod = jnp.concatenate([mx, pad2, mc, pad2], axis=1)

    big = {"l0_mixer": (ab_w_in[0], ab_w_out[0]), "l0_ffn": (ffn_w_in[0], ffn_w_out[0]),
           "l1_mixer": (ssd_w_in[0], ssd_w_out[0]), "l1_ffn": (ffn_w_in[1], ffn_w_out[1])}
    comm = _Exchange({k: tuple(w.astype(MXU_DTYPE) for w in v) for k, v in big.items()}, me)
    P = {
        "norm_g": full["norm_g"], "ab_gate_w": full["ab_gate_w"][0], "ab_gate_b": full["ab_gate_b"][0],
        "ab_gla_norm_g": ab_gla_norm_g, "ab_vnorm_g": ab_vnorm_g, "ab_spatial_w": ab_spatial_w[0], "ab_spatial_b": ab_spatial_b[0],
        "ssd_conv_w": full["ssd_conv_w"][0], "ssd_conv_b": full["ssd_conv_b"], "ssd_dt_bias": ssd_dt_bias[0],
        "ssd_a_log": ssd_a_log[0], "ssd_d": ssd_d, "ssd_norm_g": full["ssd_norm_g"], "final_norm_g": final_norm_g[None],
    }

    loss, grad_x, dmod, dP = _local_step(x[0], ctx[0], loss_target[0], mod, P, comm)

    dmx, dmc = dmod[:, 0:N_MOD].reshape(2, N_MOD * D), dmod[:, 8:8 + N_MOD].reshape(2, N_MOD * D)
    small_names = ("ab_gate_w", "ab_gate_b", "ab_gla_norm_g", "ab_vnorm_g", "ab_spatial_w", "ab_spatial_b", "norm_g", "ssd_conv_w",
                   "ssd_conv_b", "ssd_dt_bias", "ssd_a_log", "ssd_d", "ssd_norm_g", "final_norm_g")
    bwd_small = [dP[k] for k in small_names] + [dmc, dmx]
    shapes = [t.shape for t in bwd_small]
    g_bwd = _ag_small("gather_small_grads", _pack(bwd_small))
    summed = _unpack_flat(_sum_parts("sum_small_grads", g_bwd), shapes)
    gfull = dict(zip(small_names, summed[:-2]))
    dmc_sum, dmx_sum = summed[-2], summed[-1]
    dmx_all = _unpack_flat(g_bwd, shapes)[-1]
    dmx_sh = jnp.moveaxis(_my_shard(dmx_all, me, ws_mod), 0, 1)
    dm = jnp.concatenate([dmx_sh, _my_shard(dmc_sum, me, ws_mod)[:, None, :], jnp.zeros((2, 7, ws_mod), F32)], axis=1)
    d_mod_w, dsc = _mod_bwd(c_all, mod_w, dm)
    dsc_ctx = (dsc[0, 8] + dsc[1, 8])[None]
    dsc_all = _ag_small("gather_c_ctx_grad", jnp.concatenate([dsc_ctx, jnp.zeros((7, D), F32)], axis=0))
    d_c_ctx = _silu_vjp(c_ctx[None], _sum_parts("sum_c_ctx_grad", dsc_all)[0:1])[0]

    g_small_w = {
        "c_ctx": d_c_ctx, "mod_b": dmx_sum + dmc_sum, "norm_g": gfull["norm_g"], "ab_gate_w": gfull["ab_gate_w"][None],
        "ab_gate_b": gfull["ab_gate_b"][None], "ab_gla_norm_g": gfull["ab_gla_norm_g"], "ab_vnorm_g": gfull["ab_vnorm_g"],
        "ab_spatial_w": gfull["ab_spatial_w"][None], "ab_spatial_b": gfull["ab_spatial_b"][None], "ssd_conv_w": gfull["ssd_conv_w"][None],
        "ssd_conv_b": gfull["ssd_conv_b"], "ssd_dt_bias": gfull["ssd_dt_bias"][None], "ssd_a_log": gfull["ssd_a_log"][None],
        "ssd_d": gfull["ssd_d"], "ssd_norm_g": gfull["ssd_norm_g"], "final_norm_g": gfull["final_norm_g"][0],
    }
    for k in SMALL_SHARDED:
        g_small_w[k] = _my_shard(g_small_w[k], me, a[k].shape[-1])
    token = comm.start_last([d_c_ctx])
    res = _adam("adam_small", _pack([a[k] for k in SMALL]), [_pack([g_small_w[k] for k in SMALL])[None]],
                _pack([a["m_" + k] for k in SMALL]), _pack([a["v_" + k] for k in SMALL]), token)
    out = {k: vals for k, vals in zip(SMALL, zip(*[_unpack_flat(r, [a[k].shape for k in SMALL]) for r in res]))}

    def adam_big(name, w2d, parts, m2d, v2d, shape):
        return tuple(r.reshape(shape) for r in _adam(name, w2d, parts, m2d, v2d, token))

    def flat2(t):
        return t.reshape(-1, t.shape[-1])

    out["mod_w"] = adam_big("adam_mod_w", flat2(mod_w), [d_mod_w.reshape(1, -1, ws_mod)], flat2(m_mod_w), flat2(v_mod_w), mod_w.shape)

    for j, k in enumerate(("ffn_w_in", "ffn_w_out")):
        out[k] = adam_big("adam_" + k, flat2(a[k]), [comm.recv["l0_ffn"][j], comm.recv["l1_ffn"][j]], flat2(a["m_" + k]),
                          flat2(a["v_" + k]), a[k].shape)
    for j, k in enumerate(("ssd_w_in", "ssd_w_out")):
        out[k] = adam_big("adam_" + k, a[k][0], [comm.recv["l1_mixer"][j]], a["m_" + k][0], a["v_" + k][0], a[k].shape)
    recv_ab = comm.finish(out["ssd_w_out"][3])
    for j, k in enumerate(("ab_w_in", "ab_w_out")):
        out[k] = adam_big("adam_" + k, a[k][0], [recv_ab[j]], a["m_" + k][0], a["v_" + k][0], a[k].shape)

    loss_all = lax.psum(loss[0, 0], ("x", "y", "c"))
    return (loss_all, grad_x[None], *[out[k][0] for k in WEIGHTS], *[out[k][1] for k in WEIGHTS],
            *[out[k][2] for k in WEIGHTS], *[out[k][3] for k in WEIGHTS])
```

```python
import functools
import math

import jax
import jax.numpy as jnp
from jax import lax
from jax.experimental import pallas as pl
from jax.experimental.pallas import tpu as pltpu

F32 = jnp.float32
BF16 = jnp.bfloat16
MXU_DTYPE = jnp.bfloat16

D = 1024
NDEV = 8
N_MOD = 6
EPS = 1e-6
GRID_W = 64
CTX = 256
TM = 256
D_FF = 2816
GLA_H, GLA_DK, GLA_DV, GLA_LR, GLA_TAU, GLA_L = 4, 64, 128, 16, 16.0, 64
GMLP_G, GMLP_C, GMLP_L = 4, 128, 128
SSD_H, SSD_P, SSD_G, SSD_N, SSD_L, SSD_K = 32, 64, 4, 128, 128, 5
SSD_INNER = SSD_H * SSD_P
AB_IN = 2592
SSD_IN = 5184
AB_SEGS = ((256, 768), (1056, 1568), (1568, 2080), (2080, 2592), (0, 256), (800, 1056), (768, 800))
AB_P = 2816
SSD_SEGS = ((0, 2048), (3136, 5184), (2048, 2560), (2560, 3072), (3072, 3136))
SSD_P_W = 5376
VMEM_LIMIT = 56 * 1024 * 1024

ADAM_LR, ADAM_B1, ADAM_B2, ADAM_EPS, ADAM_WD, ADAM_STEP = 0.001, 0.9, 0.999, 1e-08, 0.01, 10


def _cp(sem=None, **kw):
    return pltpu.CompilerParams(dimension_semantics=sem, vmem_limit_bytes=VMEM_LIMIT, **kw)


def _dot(a, b, dims=(((1,), (0,)), ((), ()))):
    return lax.dot_general(a.astype(MXU_DTYPE), b.astype(MXU_DTYPE), dims, preferred_element_type=F32)


def _dot_nt(a, b):
    return _dot(a, b, (((1,), (1,)), ((), ())))


def _dot_tn(a, b):
    return _dot(a, b, (((0,), (0,)), ((), ())))


def _rms(x):
    return x * lax.rsqrt(jnp.mean(x * x, axis=-1, keepdims=True) + EPS)


def _pick(n, prefs):
    for p in prefs:
        if n % p == 0:
            return p
    return n


def _row(arr, width=None, colblk=0, tm=TM, valid=None):
    width = arr.shape[1] if width is None else width
    if valid is None:
        return ([arr], [pl.BlockSpec((tm, width), lambda i, c=colblk: (i, c))], lambda r: r[...].astype(F32), width)
    spec = pl.BlockSpec((tm, width), lambda i, c=colblk: (jnp.minimum(i, valid - 1), c))
    return ([arr], [spec], lambda r: jnp.where(pl.program_id(0) < valid, r[...].astype(F32), 0.0), width)


def _row_grid(arr, a, nb):
    n = arr.shape[0]
    b = (n - CTX) // a

    def load(v_ref, c_ref):
        i = pl.program_id(0)
        return jnp.where(i == nb - 1, c_ref[...], _grid_rows(v_ref, a, i))

    return ([arr.reshape(n // b, b, D), arr], [_grid_spec(a, nb), pl.BlockSpec((TM, D), lambda i: (nb - 1, 0))], load, D)


def _row_cat(x, ctx, nb):
    return ([x, ctx], [pl.BlockSpec((TM, D), lambda i: (jnp.minimum(i, nb - 2), 0)), pl.BlockSpec((TM, D), lambda i: (0, 0))],
            lambda x_ref, c_ref: jnp.where(pl.program_id(0) == nb - 1, c_ref[...], x_ref[...]), D)


def _operands(rows):
    return [a for r in rows for a in r[0]], [s for r in rows for s in r[1]]


def _load_rows(refs, rows):
    vals, k = [], 0
    for r in rows:
        vals.append(r[2](*refs[k:k + len(r[0])]))
        k += len(r[0])
    return vals


def _full_spec(p):
    nd = p.ndim
    return pl.BlockSpec(p.shape, lambda i, nd=nd: (0,) * nd)


def _rowwise(name, fn, n_blocks, ctx_blk, rows, params, outs, tm=TM, post_mm=None):
    arrs, specs = _operands(rows)
    nr, npar = len(arrs), len(params)
    extra = [] if post_mm is None else [post_mm[1]]
    outs = list(outs) + [(w.shape[1], F32) for w in extra]

    def body(*refs):
        t = (pl.program_id(0) >= ctx_blk).astype(F32)
        rv = _load_rows(refs[:nr], rows)
        pv = [p[...] for p in refs[nr:nr + npar]]
        res = list(fn(t, rv, pv))
        o_refs = refs[nr + npar + len(extra):]
        if extra:
            res.append(_dot(res[post_mm[0]].astype(o_refs[post_mm[0]].dtype), refs[nr + npar][...]))
        for o_ref, o in zip(o_refs, res):
            o_ref[...] = o.astype(o_ref.dtype)

    return pl.pallas_call(
        body, name=name, grid=(n_blocks,),
        in_specs=specs + [_full_spec(p) for p in params + extra],
        out_specs=[pl.BlockSpec((tm, w), lambda i: (i, 0)) for w, _ in outs],
        out_shape=[jax.ShapeDtypeStruct((n_blocks * tm, w), dt) for w, dt in outs],
        compiler_params=_cp(("parallel",)),
    )(*arrs, *params, *extra)


def _rowwise_vjp(name, fn, n_blocks, ctx_blk, rows, params, douts, row_grads, tm=TM, x_rows_only=False, cot_mm=None):
    out_blocks = n_blocks - 1 if x_rows_only else n_blocks
    adds = [a for _, _, a in row_grads if a is not None]
    (r_arrs, r_specs), (d_arrs, d_specs), (a_arrs, a_specs) = _operands(rows), _operands(douts), _operands(adds)
    nr, npar, nd, na = len(r_arrs), len(params), len(d_arrs), len(a_arrs)
    extra = [] if cot_mm is None else [cot_mm[1]]

    def body(*refs):
        i = pl.program_id(0)
        t = (i >= ctx_blk).astype(F32)
        rv = _load_rows(refs[:nr], rows)
        pv = [p[...] for p in refs[nr:nr + npar]]
        dv = _load_rows(refs[nr + npar:nr + npar + nd], douts)
        av = _load_rows(refs[nr + npar + nd:nr + npar + nd + na], adds)
        o_refs = refs[nr + npar + nd + na + len(extra):]
        if extra:
            dv[cot_mm[0]] = _dot_nt(dv[cot_mm[0]], refs[nr + npar + nd + na][...])
        _, vjp = jax.vjp(lambda r, p: tuple(fn(t, r, p)), rv, pv)
        d_rows, d_params = vjp(tuple(dv))
        ai, grads = 0, []
        for ri, _, addend in row_grads:
            g = jnp.concatenate([d_rows[r] for r in ri], axis=1) if isinstance(ri, tuple) else d_rows[ri]
            if addend is not None:
                g = g + av[ai]
                ai += 1
            grads.append(g)

        @pl.when(i < out_blocks)
        def _():
            for o_ref, g in zip(o_refs, grads):
                o_ref[...] = g.astype(o_ref.dtype)

        p_refs = o_refs[len(row_grads):]

        @pl.when(i == 0)
        def _():
            for p_ref in p_refs:
                p_ref[...] = jnp.zeros_like(p_ref)

        for p_ref, g in zip(p_refs, d_params):
            p_ref[...] += g

    widths = [sum(rows[r][3] for r in ri) if isinstance(ri, tuple) else rows[ri][3] for ri, _, _ in row_grads]
    res = pl.pallas_call(
        body, name=name, grid=(n_blocks,),
        in_specs=r_specs + [_full_spec(p) for p in params] + d_specs + a_specs + [_full_spec(p) for p in extra],
        out_specs=[pl.BlockSpec((tm, w), lambda i: (jnp.minimum(i, out_blocks - 1), 0)) for w in widths] + [_full_spec(p) for p in params],
        out_shape=[jax.ShapeDtypeStruct((out_blocks * tm, w), dt) for w, (_, dt, _) in zip(widths, row_grads)]
        + [jax.ShapeDtypeStruct(p.shape, F32) for p in params],
        compiler_params=_cp(("arbitrary",)),
    )(*r_arrs, *params, *d_arrs, *a_arrs, *extra)
    return res[:len(row_grads)], res[len(row_grads):]


def _mm(name, a, b, mode, out_dtype):
    if mode == "nn":
        m, kk = a.shape
        n = b.shape[1]
    elif mode == "nt":
        m, kk = a.shape
        n = b.shape[0]
    else:
        kk, m = a.shape
        n = b.shape[1]
    if mode == "tn":
        tm = _pick(m, (1024, 1408, 512, 256, 128))
        tn = _pick(n, (768, 512, 256, 128))
        tk = kk
    else:
        tm = _pick(m, (1088, 1024, 768, 512, 384, 256, 128))
        tn = n if n <= 2816 else _pick(n, (1024, 768, 512, 256, 128))
        tk = kk if kk <= 2816 else _pick(kk, (2816, 1792, 1024, 768, 512, 256, 128))
    nk = kk // tk
    in_place = out_dtype == F32
    if mode == "nn":
        specs = [pl.BlockSpec((tm, tk), lambda i, j, k: (i, k)), pl.BlockSpec((tk, tn), lambda i, j, k: (k, j))]
        dims = (((1,), (0,)), ((), ()))
    elif mode == "nt":
        specs = [pl.BlockSpec((tm, tk), lambda i, j, k: (i, k)), pl.BlockSpec((tn, tk), lambda i, j, k: (j, k))]
        dims = (((1,), (1,)), ((), ()))
    else:
        specs = [pl.BlockSpec((tk, tm), lambda i, j, k: (k, i)), pl.BlockSpec((tk, tn), lambda i, j, k: (k, j))]
        dims = (((0,), (0,)), ((), ()))

    def body(a_ref, b_ref, o_ref, *scratch):
        part = lax.dot_general(a_ref[...].astype(MXU_DTYPE), b_ref[...].astype(MXU_DTYPE), dims, preferred_element_type=F32)
        if nk == 1:
            o_ref[...] = part.astype(o_ref.dtype)
        else:
            k = pl.program_id(2)
            acc = o_ref if in_place else scratch[0]

            @pl.when(k == 0)
            def _():
                acc[...] = part

            @pl.when(k > 0)
            def _():
                acc[...] += part

            if not in_place:
                @pl.when(k == nk - 1)
                def _():
                    o_ref[...] = acc[...].astype(o_ref.dtype)

    return pl.pallas_call(
        body, name=name, grid=(m // tm, n // tn, nk), in_specs=specs,
        out_specs=pl.BlockSpec((tm, tn), lambda i, j, k: (i, j)),
        out_shape=jax.ShapeDtypeStruct((m, n), out_dtype),
        scratch_shapes=[] if nk == 1 or in_place else [pltpu.VMEM((tm, tn), F32)],
        compiler_params=_cp(("parallel", "parallel", "arbitrary")),
    )(a, b)


def _sel_mod(modp, t):
    return modp[0:8] * (1.0 - t) + modp[8:16] * t


def _fn_prenorm(t, rows, params, *, a, b):
    (x,), (g, modp) = rows, params
    m = _sel_mod(modp, t)
    return ((_rms(x) * g) * (1.0 + m[b:b + 1]) + m[a:a + 1],)


def _fn_resid_prenorm(t, rows, params, *, gi, a, b):
    (x, y), (g, mod_a, mod_b) = rows, params
    ma, mb = _sel_mod(mod_a, t), _sel_mod(mod_b, t)
    xn = x + ma[gi:gi + 1] * y
    return xn, (_rms(xn) * g) * (1.0 + mb[b:b + 1]) + mb[a:a + 1]


def _fn_resid(t, rows, params, *, gi):
    (x, y), (mod_a,) = rows, params
    return (x + _sel_mod(mod_a, t)[gi:gi + 1] * y,)


def _fn_mixpost(t, rows, params):
    (o, r, u, g), (gla_g, vn_g, sw, sb_t) = rows, params
    a =jnp.concatenate([_rms(o[:, h * GLA_DV:(h + 1) * GLA_DV]) for h in range(GLA_H)], axis=1) * gla_g * jax.nn.silu(r)
    uu, vv = jax.nn.gelu(u), jax.nn.gelu(g)
    mu = jnp.mean(vv, axis=-1, keepdims=True)
    var = jnp.mean(jnp.square(vv - mu), axis=-1, keepdims=True)
    vn = ((vv - mu) * lax.rsqrt(var + EPS)) * vn_g
    s = jnp.concatenate(
        [_dot(sw[gi * GMLP_L:(gi + 1) * GMLP_L, :], vn[:, gi * GMLP_C:(gi + 1) * GMLP_C]) + sb_t[:, gi:gi + 1]
         for gi in range(GMLP_G)], axis=1)
    return (jnp.concatenate([a, uu * s], axis=1),)


def _expand_heads(row):
    first = lax.broadcasted_iota(jnp.int32, (1, 2 * SSD_P), 1) < SSD_P
    return jnp.concatenate([jnp.where(first, row[:, 2 * j:2 * j + 1], row[:, 2 * j + 1:2 * j + 2]) for j in range(SSD_H // 2)], axis=1)


def _fn_ssd_finish(t, rows, params):
    (y2, xs, z), (d_skip, norm_g) = rows, params
    d_full = _expand_heads(d_skip)
    y = (y2 + d_full * xs) * jax.nn.silu(z)
    gw = SSD_INNER // SSD_G
    return (jnp.concatenate([_rms(y[:, gi * gw:(gi + 1) * gw]) for gi in range(SSD_G)], axis=1) * norm_g,)


def _fn_concat(t, rows, params, *, sums, pad=0):
    out, i = [], 0
    for n in sums:
        acc = rows[i]
        for j in range(1, n):
            acc = acc + rows[i + j]
        out.append(acc)
        i += n
    if pad:
        out.append(jnp.zeros((out[0].shape[0], pad), F32))
    return (jnp.concatenate(out, axis=1),)


def _tri(n, rev):
    r = lax.broadcasted_iota(jnp.int32, (n, n), 0)
    c = lax.broadcasted_iota(jnp.int32, (n, n), 1)
    return (r <= c) if rev else (r >= c)


def _running_sum(x, rev):
    n, s = x.shape[0], 1
    while s < n:
        z = jnp.zeros((s, x.shape[1]), x.dtype)
        x = x + (jnp.concatenate([x[s:], z], axis=0) if rev else jnp.concatenate([z, x[:n - s]], axis=0))
        s *= 2
    return x


def _gla_chunk(S, v, k, q, tail, gw, gb, *, rev):
    L, H = GLA_L, GLA_H
    lr = tail[:, GLA_LR:2 * GLA_LR] if rev else tail[:, 0:GLA_LR]
    la = jax.nn.log_sigmoid(_dot(lr, gw) + gb) / GLA_TAU
    b = _running_sum(la, rev)
    b_last = b[0:1] if rev else b[L - 1:L]
    kd = k * jnp.exp(b_last - b)
    qd = (q * GLA_DK ** -0.5) * jnp.exp(b)
    ki = k * jnp.exp(-b)

    def same_head(shape, rows_per_head, cols_per_head):
        r = lax.broadcasted_iota(jnp.int32, shape, 0) // rows_per_head
        c = lax.broadcasted_iota(jnp.int32, shape, 1) // cols_per_head
        return r == c

    k_blk = jnp.where(same_head((H * L, H * GLA_DK), L, GLA_DK), jnp.concatenate([ki] * H, axis=0), 0.0)
    v_blk = jnp.where(same_head((H * L, H * GLA_DV), L, GLA_DV), jnp.concatenate([v] * H, axis=0), 0.0)
    row = lax.broadcasted_iota(jnp.int32, (L, H * L), 0)
    src = lax.broadcasted_iota(jnp.int32, (L, H * L), 1) % L
    sc = jnp.where((row <= src) if rev else (row >= src), _dot_nt(qd, k_blk), 0.0)
    o = _dot_nt(qd, S) + _dot(sc, v_blk)
    s_new = S * jnp.exp(b_last) + jnp.where(same_head(S.shape, GLA_DV, GLA_DK), _dot_tn(v, kd), 0.0)
    return s_new, o


def _ssd_chunk(S, x, bm, cm, tail, dtb, alog, *, rev):
    L = SSD_L
    msk = _tri(L, rev)
    raw = tail[:, SSD_H:2 * SSD_H] if rev else tail[:, 0:SSD_H]
    dt = jax.nn.softplus(raw + dtb)
    acum = _running_sum(dt * (-jnp.exp(alog)), rev)
    a_last = acum[0:1] if rev else acum[L - 1:L]
    wst = dt * jnp.exp(a_last - acum)
    eac = jnp.exp(acum)
    dec = jnp.exp(a_last)
    tr = jnp.concatenate([acum, dt, wst, jnp.zeros((L, L - 3 * SSD_H), F32)], axis=1).T
    acum_t, dt_t, wst_t = tr[0:SSD_H], tr[SSD_H:2 * SSD_H], tr[2 * SSD_H:3 * SSD_H]
    lane = lax.broadcasted_iota(jnp.int32, (1, 2 * SSD_P), 1)
    m0 = (lane < SSD_P).astype(F32)
    m1 = 1.0 - m0
    pairs_per_group = SSD_H // SSD_G // 2
    y_parts, s_parts = [], []
    for g in range(SSD_G):
        ns = slice(g * SSD_N, (g + 1) * SSD_N)
        bg, cg = bm[:, ns], cm[:, ns]
        cb = _dot_nt(cg, bg)
        bgt = bg.T
        gs = slice(g * pairs_per_group * 2 * SSD_P, (g + 1) * pairs_per_group * 2 * SSD_P)
        y_carry = _dot(cg, S[:, gs])
        for jj in range(pairs_per_group):
            j = g * pairs_per_group + jj
            ls = slice(j * 2 * SSD_P, (j + 1) * 2 * SSD_P)
            xp, sp = x[:, ls], S[:, ls]
            xm = jnp.concatenate([xp * m0, xp * m1], axis=0)
            lhs, bw = [], []
            for h in (2 * j, 2 * j + 1):
                seg = acum[:, h:h + 1] - acum_t[h:h + 1, :]
                lhs.append(cb * jnp.exp(jnp.where(msk, seg, -jnp.inf)) * dt_t[h:h + 1, :])
                bw.append(bgt * wst_t[h:h + 1, :])
            e_pair = eac[:, 2 * j:2 * j + 1] * m0 + eac[:, 2 * j + 1:2 * j + 2] * m1
            y_parts.append(_dot(jnp.concatenate(lhs, axis=1), xm) + y_carry[:, jj * 2 * SSD_P:(jj + 1) * 2 * SSD_P] * e_pair)
            d_pair = dec[:, 2 * j:2 * j + 1] * m0 + dec[:, 2 * j + 1:2 * j + 2] * m1
            s_parts.append(sp * d_pair + _dot(jnp.concatenate(bw, axis=1), xm))
    return jnp.concatenate(s_parts, axis=1), jnp.concatenate(y_parts, axis=1)


def _multi_chunk(chunk_fn, L, subs, nr):
    def fn(S, *args, rev):
        rows, params = args[:nr], args[nr:]
        ys = [None] * subs
        for j in (range(subs - 1, -1, -1) if rev else range(subs)):
            S, ys[j] = chunk_fn(S, *[r[j * L:(j + 1) * L] for r in rows], *params, rev=rev)
        return S, jnp.concatenate(ys, axis=0)

    return fn


def _scan_order(n, nx, rev, backward):
    nc = n - nx

    def fwd(s):
        return (n - 1 - s) if rev else jnp.where(s < nc, s + nx, s - nc)

    return (lambda s: fwd(n - 1 - s)) if backward else fwd


def _scan_fwd(name, chunk_fn, L, n, nx, rev, rows, params, state_shape, out_w, addend=None):
    order = _scan_order(n, nx, rev, False)
    nr, npar = len(rows), len(params)
    adds = [] if addend is None else [addend]

    def body(*refs):
        s_scr = refs[-1]

        @pl.when(pl.program_id(0) == 0)
        def _():
            s_scr[...] = jnp.zeros_like(s_scr)

        s_in = s_scr[...]
        y_ref, st_ref = refs[nr + npar + len(adds)], refs[nr + npar + len(adds) + 1]
        st_ref[0] = s_in
        s_new, y = chunk_fn(s_in, *[r[...] for r in refs[:nr]], *[p[...] for p in refs[nr:nr + npar]], rev=rev)
        y_ref[...] = y + refs[nr + npar][...] if adds else y
        s_scr[...] = s_new

    return pl.pallas_call(
        body, name=name, grid=(n,),
        in_specs=[pl.BlockSpec((L, w), lambda s, c=c: (order(s), c)) for _, w, c in rows] + [_full_spec(p) for p in params]
        + [pl.BlockSpec((L, out_w), lambda s: (order(s), 0)) for _ in adds],
        out_specs=[pl.BlockSpec((L, out_w), lambda s: (order(s), 0)),
                   pl.BlockSpec((1,) + state_shape, lambda s: (order(s), 0, 0))],
        out_shape=[jax.ShapeDtypeStruct((n * L, out_w), F32), jax.ShapeDtypeStruct((n,) + state_shape, F32)],
        scratch_shapes=[pltpu.VMEM(state_shape, F32)],
        compiler_params=_cp(("arbitrary",)),
    )(*[a for a, _, _ in rows], *params, *adds)


def _scan_bwd(name, chunk_fn, L, n, nx, rev, rows, params, states, dy, state_shape, out_w, addends=None):
    order = _scan_order(n, nx, rev, True)
    dy_blocks = dy.shape[0] // L
    nr, npar = len(rows), len(params)
    adds = [] if addends is None else list(addends)

    def body(*refs):
        i = pl.program_id(0)
        ds_scr = refs[-1]
        rv = [r[...] for r in refs[:nr]]
        pv = [p[...] for p in refs[nr:nr + npar]]
        st_ref, dy_ref = refs[nr + npar], refs[nr + npar + 1]
        a_refs = refs[nr + npar + 2:nr + npar + 2 + len(adds)]
        o_refs = refs[nr + npar + 2 + len(adds):-1]
        p_refs = o_refs[nr:]

        @pl.when(i == 0)
        def _():
            ds_scr[...] = jnp.zeros_like(ds_scr)
            for p_ref in p_refs:
                p_ref[...] = jnp.zeros_like(p_ref)

        _, vjp = jax.vjp(functools.partial(chunk_fn, rev=rev), st_ref[0], *rv, *pv)
        dy_blk = jnp.where(order(i) < dy_blocks, dy_ref[...].astype(F32), 0.0)
        grads = vjp((ds_scr[...], dy_blk))
        ds_scr[...] = grads[0]
        for j, (o_ref, g) in enumerate(zip(o_refs[:nr], grads[1:1 + nr])):
            o_ref[...] = g + a_refs[j][...] if adds else g
        for p_ref, g in zip(p_refs, grads[1 + nr:]):
            p_ref[...] += g

    row_specs = [pl.BlockSpec((L, w), lambda s: (order(s), 0)) for _, w, _ in rows]
    res = pl.pallas_call(
        body, name=name, grid=(n,),
        in_specs=[pl.BlockSpec((L, w), lambda s, c=c: (order(s), c)) for _, w, c in rows] + [_full_spec(p) for p in params]
        + [pl.BlockSpec((1,) + state_shape, lambda s: (order(s), 0, 0)),
           pl.BlockSpec((L, out_w), lambda s: (jnp.minimum(order(s), dy_blocks - 1), 0))]
        + row_specs[:len(adds)],
        out_specs=row_specs + [_full_spec(p) for p in params],
        out_shape=[jax.ShapeDtypeStruct((n * L, w), F32) for _, w, _ in rows] + [jax.ShapeDtypeStruct(p.shape, F32) for p in params],
        scratch_shapes=[pltpu.VMEM(state_shape, F32)],
        compiler_params=_cp(("arbitrary",)),
    )(*[a for a, _, _ in rows], *params, states, dy, *adds)
    return res[:nr], res[nr:]


CONV_W = 1024
CONV_COLBLK = (0, 1, 4)


def _conv_specs(nb, src_blk):
    halo = TM // 8
    return [pl.BlockSpec((TM, CONV_W), lambda j, i: (i, src_blk(j))),
            pl.BlockSpec((8, CONV_W), lambda j, i: (jnp.maximum(i * halo - 1, 0), src_blk(j))),
            pl.BlockSpec((8, CONV_W), lambda j, i: (jnp.minimum(i * halo + halo, nb * halo - 1), src_blk(j)))]


def _conv_ext(i, nb, cur, prev, nxt):
    has_prev = jnp.logical_and(i > 0, i < nb - 1)
    has_next = i < nb - 2
    return jnp.concatenate([jnp.where(has_prev, prev, 0.0), cur, jnp.where(has_next, nxt, 0.0)], axis=0)


def _conv_taps(ext, w, flip):
    acc = None
    for j in range(SSD_K):
        wj = w[SSD_K - 1 - j:SSD_K - j, :] if flip else w[j:j + 1, :]
        term = wj * ext[6 + j:6 + j + TM, :]
        acc = term if acc is None else acc + term
    return acc


def _conv(name, src, w8, b1, nb, *, permuted_src, act, flip, out_dtype):
    src_blk = (lambda j: jnp.where(j == 2, CONV_COLBLK[2], j)) if permuted_src else (lambda j: j)

    def body(cur, prev, nxt, w_ref, b_ref, o_ref):
        ext = _conv_ext(pl.program_id(1), nb, cur[...].astype(F32), prev[...].astype(F32), nxt[...].astype(F32))
        acc = _conv_taps(ext, w_ref[...], flip)
        if act:
            acc = jax.nn.silu(acc + b_ref[...])
        o_ref[...] = acc.astype(o_ref.dtype)

    return pl.pallas_call(
        body, name=name, grid=(3, nb),
        in_specs=_conv_specs(nb, src_blk) + [pl.BlockSpec((8, CONV_W), lambda j, i: (0, j)), pl.BlockSpec((1, CONV_W), lambda j, i: (0, j))],
        out_specs=pl.BlockSpec((TM, CONV_W), lambda j, i: (i, j)),
        out_shape=jax.ShapeDtypeStruct((nb * TM, 3 * CONV_W), out_dtype),
        compiler_params=_cp(("parallel", "parallel")),
    )(src, src, src, w8, b1)


def _conv_bwd_pre(name, p1, w8, b1, dxbc_parts, nb):
    src_blk = lambda j: jnp.where(j == 2, CONV_COLBLK[2], j)
    xs_parts, bc_parts = dxbc_parts
    n_x, n_bc = len(xs_parts), len(bc_parts)
    x_blocks = [p.shape[0] // TM for p in xs_parts]

    def body(*refs):
        cur, prev, nxt, w_ref, b_ref = refs[:5]
        d_refs = refs[5:5 + n_x + n_bc]
        da_ref, dw_ref, db_ref = refs[5 + n_x + n_bc:]
        j, i = pl.program_id(0), pl.program_id(1)
        ext = _conv_ext(i, nb, cur[...], prev[...], nxt[...])
        acc = _conv_taps(ext, w_ref[...], False) + b_ref[...]
        dx = d_refs[0][...]
        for r, blocks in zip(d_refs[1:n_x], x_blocks[1:]):
            dx = dx + jnp.where(i < blocks, r[...], 0.0)
        dbc = jnp.concatenate([d_refs[n_x][...], d_refs[n_x + 1][...]], axis=1)
        dy = jnp.where(j == 2, dbc, dx)
        sg = jax.nn.sigmoid(acc)
        da = dy * (sg + acc * sg * (1.0 - sg))
        da_ref[...] = da

        @pl.when(i == 0)
        def _():
            dw_ref[...] = jnp.zeros_like(dw_ref)
            db_ref[...] = jnp.zeros_like(db_ref)

        rows = [jnp.sum(da * ext[6 + t:6 + t + TM, :], axis=0, keepdims=True) for t in range(SSD_K)]
        dw_ref[...] += jnp.concatenate(rows + [jnp.zeros((8 - SSD_K, CONV_W), F32)], axis=0)
        db_ref[...] += jnp.sum(da, axis=0, keepdims=True)

    x_specs = [pl.BlockSpec((TM, CONV_W), lambda j, i, b=b: (jnp.minimum(i, b - 1), jnp.minimum(j, 1))) for b in x_blocks]
    bc_specs = [pl.BlockSpec((TM, 512), lambda j, i: (i, 0)) for _ in bc_parts]
    return pl.pallas_call(
        body, name=name, grid=(3, nb),
        in_specs=_conv_specs(nb, src_blk) + [pl.BlockSpec((8, CONV_W), lambda j, i: (0, j)), pl.BlockSpec((1, CONV_W), lambda j, i: (0, j))]
        + x_specs + bc_specs,
        out_specs=[pl.BlockSpec((TM, CONV_W), lambda j, i: (i, j)), pl.BlockSpec((8, CONV_W), lambda j, i: (0, j)),
                   pl.BlockSpec((1, CONV_W), lambda j, i: (0, j))],
        out_shape=[jax.ShapeDtypeStruct((nb * TM, 3 * CONV_W), F32), jax.ShapeDtypeStruct((8, 3 * CONV_W), F32),
                   jax.ShapeDtypeStruct((1, 3 * CONV_W), F32)],
        compiler_params=_cp(("arbitrary", "arbitrary")),
    )(p1, p1, p1, w8, b1, *xs_parts, *bc_parts)


def _grid_block(a):
    nbv = TM // a
    blk_b = max(nbv, 8)
    return nbv, blk_b, blk_b // nbv


def _grid_spec(a, nb):
    _, blk_b, per = _grid_block(a)
    return pl.BlockSpec((a, blk_b, D), lambda i: (0, jnp.minimum(i, nb - 2) // per, 0))


def _grid_rows(v_ref, a, i):
    nbv, _, per = _grid_block(a)

    def pick(ph):
        return jnp.concatenate([v_ref[:, ph * nbv + t, :] for t in range(nbv)], axis=0)

    out = pick(0)
    for ph in range(1, per):
        out = jnp.where(i % per == ph, pick(ph), out)
    return out


def _loss_head(x, f, target, modp, g_final, rows_r):
    tview = target.reshape(rows_r, target.shape[0] // rows_r, D)
    nb = x.shape[0] // TM + 1

    def fn(x_, f_, tgt, modp_, g_):
        xn = x_ + _sel_mod(modp_, 0.0)[5:6] * f_
        err = _rms(xn) * g_ - tgt
        return 0.5 * jnp.sum(jnp.mean(err * err, axis=-1))

    def body(x_ref, f_ref, t_ref, m_ref, g_ref, l_ref, dx_ref, df_ref, dm_ref, dg_ref):
        i = pl.program_id(0)
        tgt = _grid_rows(t_ref, rows_r, i)
        l, vjp = jax.vjp(lambda a_, b_, c_, d_: fn(a_, b_, tgt, c_, d_), x_ref[...], f_ref[...], m_ref[...], g_ref[...])
        dx, df, dm, dg = vjp(jnp.ones((), F32))

        @pl.when(i == 0)
        def _():
            l_ref[...] = jnp.zeros_like(l_ref)
            dm_ref[...] = jnp.zeros_like(dm_ref)
            dg_ref[...] = jnp.zeros_like(dg_ref)

        l_ref[...] += jnp.reshape(l, (1, 1))
        dx_ref[...] = dx
        df_ref[...] = df.astype(df_ref.dtype)
        dm_ref[...] += dm
        dg_ref[...] += dg

    rowspec = pl.BlockSpec((TM, D), lambda i: (i, 0))
    return pl.pallas_call(
        body, name="loss_head", grid=(nb - 1,),
        in_specs=[rowspec, rowspec, _grid_spec(rows_r, nb), _full_spec(modp), _full_spec(g_final)],
        out_specs=[pl.BlockSpec((1, 1), lambda i: (0, 0)), rowspec, rowspec, _full_spec(modp), _full_spec(g_final)],
        out_shape=[jax.ShapeDtypeStruct((1, 1), F32), jax.ShapeDtypeStruct(x.shape, F32), jax.ShapeDtypeStruct(x.shape, MXU_DTYPE),
                   jax.ShapeDtypeStruct(modp.shape, F32), jax.ShapeDtypeStruct(g_final.shape, F32)],
        compiler_params=_cp(("arbitrary",)),
    )(x, f, tview, modp, g_final)


def _repack(name, shards, segs, wp):
    nd, kk, ws = shards.shape
    tr = 128
    used = sum(e - s for s, e in segs)

    def body(a_ref, o_ref):
        full = jnp.concatenate([a_ref[d].astype(F32) for d in range(nd)], axis=1)
        parts = [full[:, s:e] for s, e in segs]
        if wp > used:
            parts.append(jnp.zeros((tr, wp - used), F32))
        o_ref[...] = jnp.concatenate(parts, axis=1).astype(o_ref.dtype)

    return pl.pallas_call(
        body, name=name, grid=(kk // tr,),
        in_specs=[pl.BlockSpec((nd, tr, ws), lambda i: (0, i, 0))],
        out_specs=pl.BlockSpec((tr, wp), lambda i: (i, 0)),
        out_shape=jax.ShapeDtypeStruct((kk, wp), MXU_DTYPE),
        compiler_params=_cp(("parallel",)),
    )(shards)


def _unpack(name, dw, segs, ws, out_dtype):
    kk, wp = dw.shape
    tr = 128
    order = sorted(range(len(segs)), key=lambda i: segs[i][0])
    offs, o = [], 0
    for s, e in segs:
        offs.append(o)
        o += e - s

    def body(a_ref, o_ref):
        a = a_ref[...].astype(F32)
        full = jnp.concatenate([a[:, offs[i]:offs[i] + segs[i][1] - segs[i][0]] for i in order], axis=1)
        for d in range(NDEV):
            o_ref[d] = full[:, d * ws:(d + 1) * ws].astype(o_ref.dtype)

    return pl.pallas_call(
        body, name=name, grid=(kk // tr,),
        in_specs=[pl.BlockSpec((tr, wp), lambda i: (i, 0))],
        out_specs=pl.BlockSpec((NDEV, tr, ws), lambda i: (0, i, 0)),
        out_shape=jax.ShapeDtypeStruct((NDEV, kk, ws), out_dtype),
        compiler_params=_cp(("parallel",)),
    )(dw)


def _adam_math(w, g, m, v):
    m = ADAM_B1 * m + (1.0 - ADAM_B1) * g
    v = ADAM_B2 * v + (1.0 - ADAM_B2) * jnp.square(g)
    m_hat = m / (1.0 - ADAM_B1 ** ADAM_STEP)
    v_hat = v / (1.0 - ADAM_B2 ** ADAM_STEP)
    delta = -ADAM_LR * (m_hat / (jnp.sqrt(v_hat) + ADAM_EPS) + ADAM_WD * w)
    return delta, m, v


def _adam(name, w, parts, m, v, after):
    r, c = w.shape
    nsec, npart = len(parts), parts[0].shape[0]
    rs = r // nsec
    tr = _pick(rs, (256, 128, 64, 32, 16, 8)) if rs * c * 4 > (1 << 20) else rs
    tiles = rs // tr

    def body(w_ref, *refs):
        m_ref, v_ref, _, g_ref, d_ref, nm_ref, nv_ref = refs[nsec:]
        sec = pl.program_id(0) // tiles
        for a, p_ref in enumerate(refs[:nsec]):
            @pl.when(sec == a)
            def _(p_ref=p_ref):
                g = p_ref[0].astype(F32)
                for s in range(1, npart):
                    g = g + p_ref[s].astype(F32)
                delta, nm, nv = _adam_math(w_ref[...], g, m_ref[...], v_ref[...])
                g_ref[...], d_ref[...], nm_ref[...], nv_ref[...] = g, delta, nm, nv

    spec = pl.BlockSpec((tr, c), lambda i: (i, 0))
    part_specs = [pl.BlockSpec((npart, tr, c), lambda i, a=a: (0, jnp.clip(i - a * tiles, 0, tiles - 1), 0)) for a in range(nsec)]
    return pl.pallas_call(
        body, name=name, grid=(r // tr,),
        in_specs=[spec] + part_specs + [spec, spec, ANY],
        out_specs=[spec] * 4, out_shape=[jax.ShapeDtypeStruct((r, c), F32)] * 4,
        compiler_params=_cp(("parallel",)),
    )(w, *parts, m, v, after)


def _mod_fwd(c_all, mod_w):
    nl, _, ws = mod_w.shape

    def body(c_ref, w_ref, o_ref):
        o_ref[0] = _dot(jax.nn.silu(c_ref[...]), w_ref[0])

    return pl.pallas_call(
        body, name="mod_fwd", grid=(nl,),
        in_specs=[_full_spec(c_all), pl.BlockSpec((1, D, ws), lambda i: (i, 0, 0))],
        out_specs=pl.BlockSpec((1, 16, ws), lambda i: (i, 0, 0)),
        out_shape=jax.ShapeDtypeStruct((nl, 16, ws), F32),
        compiler_params=_cp(("parallel",)),
    )(c_all, mod_w)


def _mod_bwd(c_all, mod_w, dm):
    nl, _, ws = mod_w.shape

    def body(c_ref, w_ref, d_ref, dw_ref, dc_ref):
        dw_ref[0] = _dot_tn(jax.nn.silu(c_ref[...]), d_ref[0])
        dc_ref[0] = _dot_nt(d_ref[0], w_ref[0])

    return pl.pallas_call(
        body, name="mod_bwd", grid=(nl,),
        in_specs=[_full_spec(c_all), pl.BlockSpec((1, D, ws), lambda i: (i, 0, 0)), pl.BlockSpec((1, 16, ws), lambda i: (i, 0, 0))],
        out_specs=[pl.BlockSpec((1, D, ws), lambda i: (i, 0, 0)), pl.BlockSpec((1, 16, D), lambda i: (i, 0, 0))],
        out_shape=[jax.ShapeDtypeStruct((nl, D, ws), F32), jax.ShapeDtypeStruct((nl, 16, D), F32)],
        compiler_params=_cp(("parallel",)),
    )(c_all, mod_w, dm)


def _sum_parts(name, parts):
    npart, r, c = parts.shape

    def body(p_ref, o_ref):
        g = p_ref[0].astype(F32)
        for s in range(1, npart):
            g = g + p_ref[s].astype(F32)
        o_ref[...] = g

    return pl.pallas_call(body, name=name, out_shape=jax.ShapeDtypeStruct((r, c), F32), compiler_params=_cp())(parts)


MESH = pl.DeviceIdType.MESH
ANY = pl.BlockSpec(memory_space=pl.ANY)
N_PEERS = NDEV - 1


def _mesh_pos():
    return lax.axis_index("x"), lax.axis_index("y"), lax.axis_index("c")


def _slot(px, py, pc):
    return 4 * px + 2 * py + pc


def _two_level_gather(x_refs, o_refs, send_sems, recv_sems, local_sems):
    x, y, c = _mesh_pos()
    me, sibling = (x, y, c), (x, y, 1 - c)
    chips = [(1 - x, y), (x, 1 - y), (1 - x, 1 - y)]
    n = len(x_refs)

    def copy(a, k, block, to, src=None):
        dst = o_refs[a].at[_slot(*block)]
        return pltpu.make_async_remote_copy(src_ref=dst if src is None else src, dst_ref=dst, send_sem=send_sems.at[a, k],
                                            recv_sem=recv_sems.at[a, k], device_id=to, device_id_type=MESH)

    mine = [pltpu.make_async_copy(x_refs[a], o_refs[a].at[_slot(*me)], local_sems.at[a]) for a in range(n)]
    for cp in mine:
        cp.start()
    first = []
    for a in range(n):
        first.append(copy(a, 0, me, sibling, src=x_refs[a]))
        first += [copy(a, 1 + j, me, (*chip, c), src=x_refs[a]) for j, chip in enumerate(chips)]
    for cp in first:
        cp.start()
    passed = []
    for j, chip in enumerate(chips):
        for a in range(n):
            copy(a, 1 + j, (*chip, c), me).wait_recv()
            fwd = copy(a, 4 + j, (*chip, c), sibling)
            fwd.start()
            passed.append(fwd)
    for a in range(n):
        copy(a, 0, sibling, me).wait_recv()
        for j, chip in enumerate(chips):
            copy(a, 4 + j, (*chip, 1 - c), me).wait_recv()
    for cp in first + passed:
        cp.wait_send()
    for cp in mine:
        cp.wait()


def _ag_small(name, x):
    r, c = x.shape

    def body(x_ref, o_ref, send_sems, recv_sems, local_sems):
        _two_level_gather([x_ref], [o_ref], send_sems, recv_sems, local_sems)

    return pl.pallas_call(
        body, name=name, out_shape=jax.ShapeDtypeStruct((NDEV, r, c), x.dtype),
        in_specs=[pl.BlockSpec(memory_space=pltpu.VMEM)], out_specs=pl.BlockSpec(memory_space=pltpu.VMEM),
        scratch_shapes=[pltpu.SemaphoreType.DMA((1, N_PEERS)), pltpu.SemaphoreType.DMA((1, N_PEERS)), pltpu.SemaphoreType.DMA((1,))],
        compiler_params=pltpu.CompilerParams(vmem_limit_bytes=VMEM_LIMIT),
    )(x)


def _ag_big(name, shards):
    n = len(shards)

    def body(*refs):
        _two_level_gather(refs[:n], refs[n:2 * n], *refs[2 * n:])

    return pl.pallas_call(
        body, name=name, out_shape=[jax.ShapeDtypeStruct((NDEV,) + s.shape, s.dtype) for s in shards],
        in_specs=[ANY] * n, out_specs=[ANY] * n,
        scratch_shapes=[pltpu.SemaphoreType.DMA((n, N_PEERS)), pltpu.SemaphoreType.DMA((n, N_PEERS)), pltpu.SemaphoreType.DMA((n,))],
    )(*shards)


HBM = pl.BlockSpec(memory_space=pltpu.HBM)
SEM = pl.BlockSpec(memory_space=pltpu.SEMAPHORE)
EFFECT = pltpu.SideEffectType.DATAFLOW_SIDE_EFFECTING


def _peers(x, y, c):
    return [(k - 1, ((1 - x) if k & 4 else x, (1 - y) if k & 2 else y, (1 - c) if k & 1 else c)) for k in range(1, NDEV)]


def _xchg_copy(src_refs, land_refs, send_sems, recv_sems, a, k, peer, me, scatter):
    src = src_refs[a].at[_slot(*peer)] if scatter else src_refs[a]
    return pltpu.make_async_remote_copy(src_ref=src, dst_ref=land_refs[a].at[me], send_sem=send_sems.at[a * N_PEERS + k],
                                        recv_sem=recv_sems.at[a * N_PEERS + k], device_id=peer, device_id_type=MESH)


def _xchg_start(name, srcs, lands, deps, scatter):
    n, nd = len(srcs), len(deps)

    def body(*refs):
        src_refs, land_refs = refs[:n], refs[n:2 * n]
        send_sems, recv_sems, token = refs[2 * n + nd], refs[2 * n + nd + 1], refs[-1]
        x, y, c = _mesh_pos()
        me = _slot(x, y, c)
        for k, peer in _peers(x, y, c):
            for a in range(n):
                _xchg_copy(src_refs, land_refs, send_sems, recv_sems, a, k, peer, me, scatter).start()
        token[...] = jnp.zeros_like(token)

    res = pl.pallas_call(
        body, name=name,
        out_shape=(pltpu.SemaphoreType.DMA((n * N_PEERS,)), pltpu.SemaphoreType.DMA((n * N_PEERS,)),
                   *[pltpu.HBM(s.shape, s.dtype) for s in srcs], *[pltpu.HBM(s.shape, s.dtype) for s in lands],
                   jax.ShapeDtypeStruct((8, 128), F32)),
        in_specs=[HBM] * (2 * n) + [ANY] * nd,
        out_specs=(SEM, SEM, *([HBM] * (2 * n)), pl.BlockSpec(memory_space=pltpu.VMEM)),
        input_output_aliases={i: 2 + i for i in range(2 * n)},
        compiler_params=pltpu.CompilerParams(has_side_effects=EFFECT),
    )(*[pltpu.with_memory_space_constraint(s, pltpu.HBM) for s in srcs],
      *[pltpu.with_memory_space_constraint(s, pltpu.HBM) for s in lands], *deps)
    return res[0], res[1], res[2:2 + n], res[2 + n:2 + 2 * n], res[-1]


def _xchg_wait(name, send_sems, recv_sems, srcs, lands, after, scatter):
    n = len(srcs)

    def body(*refs):
        src_refs, land_refs = refs[:n], refs[n:2 * n]
        s_sems, r_sems = refs[2 * n], refs[2 * n + 1]
        x, y, c = _mesh_pos()
        me = _slot(x, y, c)
        for k, peer in _peers(x, y, c):
            for a in range(n):
                cp = _xchg_copy(src_refs, land_refs, s_sems, r_sems, a, k, peer, me, scatter)
                cp.wait_send()
                cp.wait_recv()

    res = pl.pallas_call(
        body, name=name,
        out_shape=[pltpu.HBM(s.shape, s.dtype) for s in srcs] + [pltpu.HBM(s.shape, s.dtype) for s in lands],
        in_specs=[HBM] * (2 * n) + [SEM, SEM, ANY], out_specs=[HBM] * (2 * n),
        input_output_aliases={i: i for i in range(2 * n)},
        compiler_params=pltpu.CompilerParams(has_side_effects=EFFECT),
    )(*srcs, *lands, send_sems, recv_sems, after)
    return res[n:]


def _landing(name, srcs, me, scatter):
    shapes = [s.shape[-2:] for s in srcs]

    def body(me_ref, *refs):
        for s_ref, o_ref in zip(refs[:len(srcs)], refs[len(srcs):]):
            o_ref[...] = s_ref[...].reshape(o_ref.shape)

    def slot_spec(r, c):
        return pl.BlockSpec((1, r, c), lambda i, me_ref: (me_ref[0], 0, 0))

    return pl.pallas_call(
        body, name=name, out_shape=[jax.ShapeDtypeStruct((NDEV, r, c), s.dtype) for s, (r, c) in zip(srcs, shapes)],
        grid_spec=pltpu.PrefetchScalarGridSpec(
            num_scalar_prefetch=1, grid=(1,),
            in_specs=[slot_spec(r, c) if scatter else pl.BlockSpec((r, c), lambda i, me_ref: (0, 0)) for r, c in shapes],
            out_specs=[slot_spec(r, c) for r, c in shapes]),
        compiler_params=_cp(("arbitrary",)),
    )(jnp.reshape(me, (1,)).astype(jnp.int32), *srcs)


STAGES = ("l0_mixer", "l0_ffn", "l1_mixer", "l1_ffn")
STAGE_LAYOUT = {"l0_mixer": (AB_SEGS, AB_P), "l1_mixer": (SSD_SEGS, SSD_P_W)}


class _Exchange:
    def __init__(self, shards, me):
        self.shards, self.me = shards, me
        self.pending, self.pending_grads, self.recv = {}, None, {}

    def _layout(self, stage):
        ws = self.shards[stage][0].shape[-1]
        return STAGE_LAYOUT.get(stage, (((0, NDEV * ws),), NDEV * ws)) + (ws,)

    def _start_gather(self, stage, deps):
        srcs = list(self.shards[stage])
        lands = _landing("own_" + stage, srcs, self.me, False)
        return _xchg_start("gather_start_" + stage, srcs, lands, deps, False)

    def get(self, stage, dep, thread):
        i = STAGES.index(stage)
        if i == 0:
            g_in, g_out = _ag_big("gather_" + stage, list(self.shards[stage]))
            ahead, deps = STAGES[1:3], [g_out, dep]
        else:
            ss, rs, srcs, lands, _ = self.pending.pop(stage)
            g_in, g_out = _xchg_wait("gather_wait_" + stage, ss, rs, srcs, lands, dep, False)
            ahead, deps = STAGES[i + 2:i + 3], [g_out]
        for nxt in ahead:
            self.pending[nxt] = self._start_gather(nxt, deps)
            deps = [self.pending[nxt][4]]
            thread = thread + self.pending[nxt][4][0, 0]
        segs, wp, _ = self._layout(stage)
        return _repack("repack_" + stage, g_in, segs, wp), g_out.reshape(-1, D), thread

    def put(self, stage, d_in, d_out, thread):
        segs, _, ws = self._layout(stage)
        parts = [_unpack("unpack_" + stage, d_in, segs, ws, MXU_DTYPE), d_out.reshape(NDEV, -1, D)]
        deps = [parts[0]]
        if self.pending_grads is not None:
            deps = [self.finish(parts[0])[0]]
        self.staged = (stage, parts)
        return thread if stage == STAGES[0] else thread + self.start_last(deps)[0, 0]

    def start_last(self, deps):
        stage, parts = self.staged
        lands = _landing("own_grad_" + stage, parts, self.me, True)
        self.pending_grads = (stage,) + _xchg_start("scatter_start_" + stage, parts, lands, deps, True)
        return self.pending_grads[5]

    def finish(self, after):
        stage, ss, rs, srcs, lands, _ = self.pending_grads
        self.recv[stage] = _xchg_wait("scatter_wait_" + stage, ss, rs, srcs, lands, after, True)
        self.pending_grads = None
        return self.recv[stage]


def _mm_swiglu(name, h, w_in):
    m, kk = h.shape
    f = w_in.shape[1] // 2
    tm = _pick(m, (272, 256, 128))

    def body(h_ref, wg_ref, wu_ref, pf_ref, act_ref):
        a = h_ref[...].astype(MXU_DTYPE)
        g = jnp.dot(a, wg_ref[...].astype(MXU_DTYPE), preferred_element_type=F32).astype(MXU_DTYPE)
        u = jnp.dot(a, wu_ref[...].astype(MXU_DTYPE), preferred_element_type=F32).astype(MXU_DTYPE)
        pf_ref[0] = g
        pf_ref[1] = u
        act_ref[...] = (jax.nn.silu(g.astype(F32)) * u.astype(F32)).astype(act_ref.dtype)

    return pl.pallas_call(
        body, name=name, grid=(m // tm,),
        in_specs=[pl.BlockSpec((tm, kk), lambda i: (i, 0)), pl.BlockSpec((kk, f), lambda i: (0, 0)), pl.BlockSpec((kk, f), lambda i: (0, 1))],
        out_specs=[pl.BlockSpec((2, tm, f), lambda i: (0, i, 0)), pl.BlockSpec((tm, f), lambda i: (i, 0))],
        out_shape=[jax.ShapeDtypeStruct((2, m, f), MXU_DTYPE), jax.ShapeDtypeStruct((m, f), MXU_DTYPE)],
        compiler_params=_cp(("parallel",)),
    )(h, w_in, w_in)


def _ffn_fwd(tag, h, w_in, w_out, nb, cb):
    pf, act = _mm_swiglu(tag + "_ffn_in", h, w_in)
    return pf, act, _mm(tag + "_ffn_out", act, w_out, "nn", F32)


def _mm_swiglu_bwd(name, df, w_out, pf):
    m, kk = df.shape
    f = w_out.shape[0]
    tm = _pick(m, (272, 256, 128))

    def body(d_ref, w_ref, pf_ref, o_ref):
        dact = lax.dot_general(d_ref[...].astype(MXU_DTYPE), w_ref[...].astype(MXU_DTYPE), (((1,), (1,)), ((), ())),
                               preferred_element_type=F32)
        g, u = pf_ref[0].astype(F32), pf_ref[1].astype(F32)
        sg = jax.nn.sigmoid(g)
        o_ref[:, 0:f] = (dact * u * (sg * (1.0 + g * (1.0 - sg)))).astype(o_ref.dtype)
        o_ref[:, f:2 * f] = (dact * (g * sg)).astype(o_ref.dtype)

    return pl.pallas_call(
        body, name=name, grid=(m // tm,),
        in_specs=[pl.BlockSpec((tm, kk), lambda i: (i, 0)), pl.BlockSpec((f, kk), lambda i: (0, 0)), pl.BlockSpec((2, tm, f), lambda i: (0, i, 0))],
        out_specs=pl.BlockSpec((tm, 2 * f), lambda i: (i, 0)),
        out_shape=jax.ShapeDtypeStruct((m, 2 * f), MXU_DTYPE),
        compiler_params=_cp(("parallel",)),
    )(df, w_out, pf)


def _ffn_bwd(tag, h, pf, act, df, w_in, w_out, nb, cb):
    dw_out = _mm(tag + "_ffn_out_dw", act, df, "tn", MXU_DTYPE)
    dpf = _mm_swiglu_bwd(tag + "_ffn_out_dx", df, w_out, pf)
    dw_in = _mm(tag + "_ffn_in_dw", h, dpf, "tn", MXU_DTYPE)
    return dw_out, dw_in, dpf


def _local_step(x, ctx, target, mod, P, comm):
    T = x.shape[0]
    N = T + CTX
    nb, cb = N // TM, N // TM - 1
    R = T // GRID_W
    mod0, mod1 = mod[0], mod[1]
    ng = P["norm_g"]
    g00, g01, g10, g11 = ng[0, 0][None], ng[0, 1][None], ng[1, 0][None], ng[1, 1][None]
    pre = functools.partial(_fn_prenorm, a=0, b=1)
    rpre = functools.partial(_fn_resid_prenorm, gi=2, a=3, b=4)
    res5 = functools.partial(_fn_resid, gi=5)
    dirs = (("f", False), ("b", True))

    xc0 = _row_cat(x, ctx, nb)
    w_ab_in, w_ab_out, g00 = comm.get("l0_mixer", mod, g00)
    h0, p0 = _rowwise("l0_prenorm_in", pre, nb, cb, [xc0], [g00, mod0], [(D, MXU_DTYPE)], post_mm=(0, w_ab_in))
    gla_rows = [(p0, 512, 0), (p0, 256, 8), (p0, 256, 9), (p0, 128, 20)]
    gla_blk = _multi_chunk(_gla_chunk, GLA_L, TM // GLA_L, len(gla_rows))
    gla_par = {d: [P["ab_gate_w"][int(r)], P["ab_gate_b"][int(r)][None]] for d, r in dirs}
    gla_state = (GLA_H * GLA_DV, GLA_H * GLA_DK)
    o, st0 = None, {}
    for d, rev in dirs:
        o, st0[d] = _scan_fwd("gla_fwd_" + d, gla_blk, TM, nb, cb, rev, gla_rows, gla_par[d], gla_state, GLA_H * GLA_DV, o)
    n128, cb128 = N // GMLP_L, T // GMLP_L
    mix_rows = [_row(o, tm=GMLP_L)] + [_row(p0, 512, j, tm=GMLP_L) for j in (1, 2, 3)]
    mix_par = [P["ab_gla_norm_g"], P["ab_vnorm_g"], P["ab_spatial_w"].reshape(GMLP_G * GMLP_L, GMLP_L), P["ab_spatial_b"].T]
    (cat0,) = _rowwise("l0_mix", _fn_mixpost, n128, cb128, mix_rows, mix_par, [(D, MXU_DTYPE)], tm=GMLP_L)
    y0 = _mm("l0_out", cat0, w_ab_out, "nn", F32)
    w_fi0, w_fo0, g01 = comm.get("l0_ffn", y0, g01)
    x1, h1 = _rowwise("l0_ffn_prenorm", rpre, nb, cb, [xc0, _row(y0)], [g01, mod0, mod0], [(D, F32), (D, MXU_DTYPE)])
    pf0, act0, f0 = _ffn_fwd("l0", h1, w_fi0, w_fo0, nb, cb)
    w_ssd_in, w_ssd_out, g10 = comm.get("l1_mixer", f0, g10)
    x2p, h2, p1 = _rowwise("l0_resid_l1_prenorm_in", functools.partial(_fn_resid_prenorm, gi=5, a=0, b=1), nb, cb,
                           [_row_grid(x1, R, nb), _row_grid(f0, R, nb)], [g10, mod0, mod1], [(D, F32), (D, MXU_DTYPE)],
                           post_mm=(1, w_ssd_in))
    conv_w8 = jnp.concatenate([P["ssd_conv_w"], jnp.zeros((8 - SSD_K, 3 * CONV_W), F32)], axis=0)
    xbc = _conv("l1_conv", p1, conv_w8, P["ssd_conv_b"], nb, permuted_src=True, act=True, flip=False, out_dtype=F32)
    ssd_rows = [(xbc, SSD_INNER, 0), (xbc, 512, 4), (xbc, 512, 5), (p1, 128, 40)]
    ssd_blk = _multi_chunk(_ssd_chunk, SSD_L, TM // SSD_L, len(ssd_rows))
    ssd_par = {d: [P["ssd_dt_bias"][int(r)][None], P["ssd_a_log"][int(r)][None]] for d, r in dirs}
    ssd_state = (SSD_N, SSD_INNER)
    ys, st1 = None, {}
    for d, rev in dirs:
        ys, st1[d] = _scan_fwd("ssd_fwd_" + d, ssd_blk, TM, nb, cb, rev, ssd_rows, ssd_par[d], ssd_state, SSD_INNER, ys)
    fin_rows = [_row(ys), _row(xbc, SSD_INNER, 0), _row(p1, SSD_INNER, 1)]
    fin_par = [P["ssd_d"], P["ssd_norm_g"]]
    yn, y1 = _rowwise("l1_finish_out", _fn_ssd_finish, cb, cb, fin_rows, fin_par, [(SSD_INNER, MXU_DTYPE)], post_mm=(0, w_ssd_out))
    w_fi1, w_fo1, g11 = comm.get("l1_ffn", y1, g11)
    x3, h3 = _rowwise("l1_ffn_prenorm", rpre, cb, cb, [_row(x2p), _row(y1)], [g11, mod1, mod1], [(D, F32), (D, MXU_DTYPE)])
    pf1, act1, f1 = _ffn_fwd("l1", h3, w_fi1, w_fo1, cb, cb)
    loss, dx3, df1, dm1_j, d_final_g = _loss_head(x3, f1, target, mod1, P["final_norm_g"], R)

    dP = {"final_norm_g": d_final_g}
    dwo1, dwi1, dpf1 = _ffn_bwd("l1", h3, pf1, act1, df1, w_fi1, w_fo1, cb, cb)
    g11 = comm.put("l1_ffn", dwi1, dwo1, g11)
    (dx2p_a, dy1), (dg11, dm1_a, dm1_b) = _rowwise_vjp(
        "l1_ffn_in_dx_prenorm_bwd", rpre, cb, cb, [_row(x2p), _row(y1)], [g11, mod1, mod1], [_row(dx3), _row(dpf1)],
        [(0, F32, None), (1, MXU_DTYPE, None)], cot_mm=(1, w_fi1))
    d_ssd_out = _mm("l1_out_dw", yn, dy1, "tn", MXU_DTYPE)
    (dys, dxs, dz), (dP["ssd_d"], dP["ssd_norm_g"]) = _rowwise_vjp(
        "l1_out_dx_finish_bwd", _fn_ssd_finish, cb, cb, fin_rows, fin_par, [_row(dy1)],
        [(0, F32, None), (1, F32, None), (2, MXU_DTYPE, None)], cot_mm=(0, w_ssd_out))
    dssd, ddtb, dalog = None, [], []
    for d, rev in dirs:
        dssd, (ddtb_, dalog_) = _scan_bwd("ssd_bwd_" + d, ssd_blk, TM, nb, cb, rev, ssd_rows, ssd_par[d], st1[d], dys, ssd_state,
                                          SSD_INNER, dssd)
        ddtb.append(ddtb_); dalog.append(dalog_)
    dx_s, db_s, dc_s, dtl = dssd
    dP["ssd_dt_bias"] = jnp.concatenate(ddtb, axis=0)
    dP["ssd_a_log"] = jnp.concatenate(dalog, axis=0)
    dacc, dcw8, dP["ssd_conv_b"] = _conv_bwd_pre("l1_conv_bwd", p1, conv_w8, P["ssd_conv_b"], ([dx_s, dxs], [db_s, dc_s]), nb)
    dP["ssd_conv_w"] = dcw8[:SSD_K]
    dpc = _conv("l1_conv_dx", dacc, conv_w8, jnp.zeros((1, 3 * CONV_W), F32), nb, permuted_src=False, act=False, flip=True,
                out_dtype=MXU_DTYPE)
    cat1 = functools.partial(_fn_concat, sums=(1, 1, 1, 1), pad=SSD_P_W - 5248)
    (dp1,) = _rowwise("l1_dp", cat1, nb, cb, [_row(dpc, SSD_INNER, 0), _row(dz, valid=cb), _row(dpc, 1024, 2), _row(dtl)],
                      [], [(SSD_P_W, MXU_DTYPE)])
    g10 = comm.put("l1_mixer", _mm("l1_in_dw", h2, dp1, "tn", F32), d_ssd_out, g10)
    (dx2p,), (dg10, dm1_f) = _rowwise_vjp("l1_in_dx_prenorm_bwd", pre, nb, cb, [_row(x2p)], [g10, mod1], [_row(dp1)],
                                          [(0, F32, _row(dx2p_a, valid=cb))], cot_mm=(0, w_ssd_in))

    (dx1_a, df0), (dm0_e,) = _rowwise_vjp("l0_resid_bwd", res5, nb, cb, [_row(x1), _row(f0)], [mod0],
                                          [_row_grid(dx2p, GRID_W, nb)], [(0, F32, None), (1, MXU_DTYPE, None)])
    dwo0, dwi0, dpf0 = _ffn_bwd("l0", h1, pf0, act0, df0, w_fi0, w_fo0, nb, cb)
    g01 = comm.put("l0_ffn", dwi0, dwo0, g01)
    (dxc0_a, dy0), (dg01, dm0_a, dm0_b) = _rowwise_vjp(
        "l0_ffn_in_dx_prenorm_bwd", rpre, nb, cb, [xc0, _row(y0)], [g01, mod0, mod0], [_row(dx1_a), _row(dpf0)],
        [(0, F32, None), (1, MXU_DTYPE, None)], cot_mm=(1, w_fi0))
    d_ab_out = _mm("l0_out_dw", cat0, dy0, "tn", MXU_DTYPE)
    dcat0 = _mm("l0_out_dx", dy0, w_ab_out, "nt", MXU_DTYPE)
    (do, dr, du, dgm), (dP["ab_gla_norm_g"], dP["ab_vnorm_g"], dsw, dsb_t) = _rowwise_vjp(
        "l0_mix_bwd", _fn_mixpost, n128, cb128, mix_rows, mix_par, [_row(dcat0, tm=GMLP_L)],
        [(0, F32, None), (1, MXU_DTYPE, None), (2, MXU_DTYPE, None), (3, MXU_DTYPE, None)], tm=GMLP_L)
    dP["ab_spatial_w"] = dsw.reshape(GMLP_G, GMLP_L, GMLP_L)
    dP["ab_spatial_b"] = dsb_t.T
    gl, dgw, dgb = None, [], []
    for d, rev in dirs:
        gl, (dgw_, dgb_) = _scan_bwd("gla_bwd_" + d, gla_blk, TM, nb, cb, rev, gla_rows, gla_par[d], st0[d], do,
                                     gla_state, GLA_H * GLA_DV, gl)
        dgw.append(dgw_[None]); dgb.append(dgb_)
    dP["ab_gate_w"] = jnp.concatenate(dgw, axis=0)
    dP["ab_gate_b"] = jnp.concatenate(dgb, axis=0)
    cat0f = functools.partial(_fn_concat, sums=(1,) * 7, pad=AB_P - 2688)
    (dp0,) = _rowwise("l0_dp", cat0f, nb, cb, [_row(gl[0]), _row(dr), _row(du), _row(dgm), _row(gl[1]), _row(gl[2]), _row(gl[3])],
                      [], [(AB_P, MXU_DTYPE)])
    g00 = comm.put("l0_mixer", _mm("l0_in_dw", h0, dp0, "tn", F32), d_ab_out, g00)
    (grad_x,), (dg00, dm0_s) = _rowwise_vjp("l0_in_dx_prenorm_bwd", pre, nb, cb, [xc0], [g00, mod0], [_row(dp0)],
                                            [(0, F32, _row(dxc0_a))], x_rows_only=True, cot_mm=(0, w_ab_in))
    dP["norm_g"] = jnp.concatenate([dg00, dg01, dg10, dg11], axis=0).reshape(2, 2, D)
    dmod = jnp.stack([dm0_s + dm0_a + dm0_b + dm0_e, dm1_f + dm1_a + dm1_b + dm1_j])
    return loss, grad_x, dmod, dP


WEIGHTS = ("c_ctx", "mod_w", "mod_b", "norm_g", "ffn_w_in", "ffn_w_out", "ab_w_in", "ab_gate_w", "ab_gate_b", "ab_gla_norm_g",
           "ab_vnorm_g", "ab_spatial_w", "ab_spatial_b", "ab_w_out", "ssd_w_in", "ssd_conv_w", "ssd_conv_b", "ssd_dt_bias",
           "ssd_a_log", "ssd_d", "ssd_norm_g", "ssd_w_out", "final_norm_g")
SMALL_SHARDED = ("norm_g", "ab_gate_w", "ab_gate_b", "ssd_conv_w", "ssd_conv_b", "ssd_norm_g")
SMALL = ("c_ctx", "mod_b", "norm_g", "ab_gate_w", "ab_gate_b", "ab_gla_norm_g", "ab_vnorm_g", "ab_spatial_w", "ab_spatial_b",
         "ssd_conv_w", "ssd_conv_b", "ssd_dt_bias", "ssd_a_log", "ssd_d", "ssd_norm_g", "final_norm_g")
LANES = 1024


def _pack(arrs, rows_multiple=8):
    flat = jnp.concatenate([a.reshape(-1).astype(F32) for a in arrs])
    rows = -(-flat.shape[0] // LANES)
    rows = -(-rows // rows_multiple) * rows_multiple
    return jnp.pad(flat, (0, rows * LANES - flat.shape[0])).reshape(rows, LANES)


def _unpack_flat(buf, shapes):
    lead = buf.shape[:-2]
    flat = buf.reshape(lead + (-1,))
    out, o = [], 0
    for s in shapes:
        n = math.prod(s)
        out.append(flat[..., o:o + n].reshape(lead + tuple(s)))
        o += n
    return out


def _unshard(g):
    g = jnp.moveaxis(g, 0, -2)
    return g.reshape(g.shape[:-2] + (g.shape[-2] * g.shape[-1],))


def _my_shard(full, me, ws):
    return lax.dynamic_slice_in_dim(full, me * ws, ws, axis=full.ndim - 1)


def _silu_vjp(cvec, dsc):
    def body(c_ref, d_ref, o_ref):
        _, vjp = jax.vjp(jax.nn.silu, c_ref[...])
        o_ref[...] = vjp(d_ref[...])[0]

    return pl.pallas_call(body, name="c_ctx_bwd", out_shape=jax.ShapeDtypeStruct(cvec.shape, F32), compiler_params=_cp())(cvec, dsc)


def kernel(x, c, ctx, c_ctx, mod_w, mod_b, norm_g, ffn_w_in, ffn_w_out, ab_w_in, ab_gate_w, ab_gate_b, ab_gla_norm_g, ab_vnorm_g, ab_spatial_w, ab_spatial_b, ab_w_out, ssd_w_in, ssd_conv_w, ssd_conv_b, ssd_dt_bias, ssd_a_log, ssd_d, ssd_norm_g, ssd_w_out, final_norm_g, loss_target, m_c_ctx, m_mod_w, m_mod_b, m_norm_g, m_ffn_w_in, m_ffn_w_out, m_ab_w_in, m_ab_gate_w, m_ab_gate_b, m_ab_gla_norm_g, m_ab_vnorm_g, m_ab_spatial_w, m_ab_spatial_b, m_ab_w_out, m_ssd_w_in, m_ssd_conv_w, m_ssd_conv_b, m_ssd_dt_bias, m_ssd_a_log, m_ssd_d, m_ssd_norm_g, m_ssd_w_out, m_final_norm_g, v_c_ctx, v_mod_w, v_mod_b, v_norm_g, v_ffn_w_in, v_ffn_w_out, v_ab_w_in, v_ab_gate_w, v_ab_gate_b, v_ab_gla_norm_g, v_ab_vnorm_g, v_ab_spatial_w, v_ab_spatial_b, v_ab_w_out, v_ssd_w_in, v_ssd_conv_w, v_ssd_conv_b, v_ssd_dt_bias, v_ssd_a_log, v_ssd_d, v_ssd_norm_g, v_ssd_w_out, v_final_norm_g):
    a = dict(locals())
    me = _slot(*_mesh_pos())
    ws_mod = mod_w.shape[-1]

    fwd_small = [c] + [a[k] for k in SMALL_SHARDED]
    g_small = _ag_small("gather_small", _pack(fwd_small))
    parts = _unpack_flat(g_small, [t.shape for t in fwd_small])
    c_rows = parts[0].reshape(NDEV, D)
    full = {k: _unshard(p) for k, p in zip(SMALL_SHARDED, parts[1:])}
    c_all = jnp.concatenate([c_rows, c_ctx[None], jnp.zeros((7, D), F32)], axis=0)
    m_all = _ag_small("gather_mod", _mod_fwd(c_all, mod_w).reshape(2 * 16, ws_mod)).reshape(NDEV, 2, 16, ws_mod)
    m_mine = lax.dynamic_index_in_dim(m_all, me, axis=2, keepdims=False)
    mx = jnp.moveaxis(m_mine, 0, 1).reshape(2, N_MOD, D) + mod_b.reshape(2, N_MOD, D)
    mc = jnp.moveaxis(m_all[:, :, 8, :], 0, 1).reshape(2, N_MOD, D) + mod_b.reshape(2, N_MOD, D)
    pad2 = jnp.zeros((2, 2, D), F32)
    mod = jnp.concatenate([mx, pad2, mc, pad2], axis=1)

    big = {"l0_mixer": (ab_w_in[0], ab_w_out[0]), "l0_ffn": (ffn_w_in[0], ffn_w_out[0]),
           "l1_mixer": (ssd_w_in[0], ssd_w_out[0]), "l1_ffn": (ffn_w_in[1], ffn_w_out[1])}
    comm = _Exchange({k: tuple(w.astype(MXU_DTYPE) for w in v) for k, v in big.items()}, me)
    P = {
        "norm_g": full["norm_g"], "ab_gate_w": full["ab_gate_w"][0], "ab_gate_b": full["ab_gate_b"][0],
        "ab_gla_norm_g": ab_gla_norm_g, "ab_vnorm_g": ab_vnorm_g, "ab_spatial_w": ab_spatial_w[0], "ab_spatial_b": ab_spatial_b[0],
        "ssd_conv_w": full["ssd_conv_w"][0], "ssd_conv_b": full["ssd_conv_b"], "ssd_dt_bias": ssd_dt_bias[0],
        "ssd_a_log": ssd_a_log[0], "ssd_d": ssd_d, "ssd_norm_g": full["ssd_norm_g"], "final_norm_g": final_norm_g[None],
    }

    loss, grad_x, dmod, dP = _local_step(x[0], ctx[0], loss_target[0], mod, P, comm)

    dmx, dmc = dmod[:, 0:N_MOD].reshape(2, N_MOD * D), dmod[:, 8:8 + N_MOD].reshape(2, N_MOD * D)
    small_names = ("ab_gate_w", "ab_gate_b", "ab_gla_norm_g", "ab_vnorm_g", "ab_spatial_w", "ab_spatial_b", "norm_g", "ssd_conv_w",
                   "ssd_conv_b", "ssd_dt_bias", "ssd_a_log", "ssd_d", "ssd_norm_g", "final_norm_g")
    bwd_small = [dP[k] for k in small_names] + [dmc, dmx]
    shapes = [t.shape for t in bwd_small]
    g_bwd = _ag_small("gather_small_grads", _pack(bwd_small))
    summed = _unpack_flat(_sum_parts("sum_small_grads", g_bwd), shapes)
    gfull = dict(zip(small_names, summed[:-2]))
    dmc_sum, dmx_sum = summed[-2], summed[-1]
    dmx_all = _unpack_flat(g_bwd, shapes)[-1]
    dmx_sh = jnp.moveaxis(_my_shard(dmx_all, me, ws_mod), 0, 1)
    dm = jnp.concatenate([dmx_sh, _my_shard(dmc_sum, me, ws_mod)[:, None, :], jnp.zeros((2, 7, ws_mod), F32)], axis=1)
    d_mod_w, dsc = _mod_bwd(c_all, mod_w, dm)
    dsc_ctx = (dsc[0, 8] + dsc[1, 8])[None]
    dsc_all = _ag_small("gather_c_ctx_grad", jnp.concatenate([dsc_ctx, jnp.zeros((7, D), F32)], axis=0))
    d_c_ctx = _silu_vjp(c_ctx[None], _sum_parts("sum_c_ctx_grad", dsc_all)[0:1])[0]

    g_small_w = {
        "c_ctx": d_c_ctx, "mod_b": dmx_sum + dmc_sum, "norm_g": gfull["norm_g"], "ab_gate_w": gfull["ab_gate_w"][None],
        "ab_gate_b": gfull["ab_gate_b"][None], "ab_gla_norm_g": gfull["ab_gla_norm_g"], "ab_vnorm_g": gfull["ab_vnorm_g"],
        "ab_spatial_w": gfull["ab_spatial_w"][None], "ab_spatial_b": gfull["ab_spatial_b"][None], "ssd_conv_w": gfull["ssd_conv_w"][None],
        "ssd_conv_b": gfull["ssd_conv_b"], "ssd_dt_bias": gfull["ssd_dt_bias"][None], "ssd_a_log": gfull["ssd_a_log"][None],
        "ssd_d": gfull["ssd_d"], "ssd_norm_g": gfull["ssd_norm_g"], "final_norm_g": gfull["final_norm_g"][0],
    }
    for k in SMALL_SHARDED:
        g_small_w[k] = _my_shard(g_small_w[k], me, a[k].shape[-1])
    token = comm.start_last([d_c_ctx])
    res = _adam("adam_small", _pack([a[k] for k in SMALL]), [_pack([g_small_w[k] for k in SMALL])[None]],
                _pack([a["m_" + k] for k in SMALL]), _pack([a["v_" + k] for k in SMALL]), token)
    out = {k: vals for k, vals in zip(SMALL, zip(*[_unpack_flat(r, [a[k].shape for k in SMALL]) for r in res]))}

    def adam_big(name, w2d, parts, m2d, v2d, shape):
        return tuple(r.reshape(shape) for r in _adam(name, w2d, parts, m2d, v2d, token))

    def flat2(t):
        return t.reshape(-1, t.shape[-1])

    out["mod_w"] = adam_big("adam_mod_w", flat2(mod_w), [d_mod_w.reshape(1, -1, ws_mod)], flat2(m_mod_w), flat2(v_mod_w), mod_w.shape)

    for j, k in enumerate(("ffn_w_in", "ffn_w_out")):
        out[k] = adam_big("adam_" + k, flat2(a[k]), [comm.recv["l0_ffn"][j], comm.recv["l1_ffn"][j]], flat2(a["m_" + k]),
                          flat2(a["v_" + k]), a[k].shape)
    for j, k in enumerate(("ssd_w_in", "ssd_w_out")):
        out[k] = adam_big("adam_" + k, a[k][0], [comm.recv["l1_mixer"][j]], a["m_" + k][0], a["v_" + k][0], a[k].shape)
    recv_ab = comm.finish(out["ssd_w_out"][3])
    for j, k in enumerate(("ab_w_in", "ab_w_out")):
        out[k] = adam_big("adam_" + k, a[k][0], [recv_ab[j]], a["m_" + k][0], a["v_" + k][0], a[k].shape)

    loss_all = lax.psum(loss[0, 0], ("x", "y", "c"))
    return (loss_all, grad_x[None], *[out[k][0] for k in WEIGHTS], *[out[k][1] for k in WEIGHTS],
            *[out[k][2] for k in WEIGHTS], *[out[k][3] for k in WEIGHTS])
```

```python
import functools
import math

import jax
import jax.numpy as jnp
from jax import lax
from jax.experimental import pallas as pl
from jax.experimental.pallas import tpu as pltpu

F32 = jnp.float32
BF16 = jnp.bfloat16
MXU_DTYPE = jnp.bfloat16

D = 1024
NDEV = 8
N_MOD = 6
EPS = 1e-6
GRID_W = 64
CTX = 256
TM = 256
D_FF = 2816
GLA_H, GLA_DK, GLA_DV, GLA_LR, GLA_TAU, GLA_L = 4, 64, 128, 16, 16.0, 64
GMLP_G, GMLP_C, GMLP_L = 4, 128, 128
SSD_H, SSD_P, SSD_G, SSD_N, SSD_L, SSD_K = 32, 64, 4, 128, 128, 5
SSD_INNER = SSD_H * SSD_P
AB_IN = 2592
SSD_IN = 5184
AB_SEGS = ((256, 768), (1056, 1568), (1568, 2080), (2080, 2592), (0, 256), (800, 1056), (768, 800))
AB_P = 2816
SSD_SEGS = ((0, 2048), (3136, 5184), (2048, 2560), (2560, 3072), (3072, 3136))
SSD_P_W = 5376
VMEM_LIMIT = 56 * 1024 * 1024

ADAM_LR, ADAM_B1, ADAM_B2, ADAM_EPS, ADAM_WD, ADAM_STEP = 0.001, 0.9, 0.999, 1e-08, 0.01, 10


def _cp(sem=None, **kw):
    return pltpu.CompilerParams(dimension_semantics=sem, vmem_limit_bytes=VMEM_LIMIT, **kw)


def _dot(a, b, dims=(((1,), (0,)), ((), ()))):
    return lax.dot_general(a.astype(MXU_DTYPE), b.astype(MXU_DTYPE), dims, preferred_element_type=F32)


def _dot_nt(a, b):
    return _dot(a, b, (((1,), (1,)), ((), ())))


def _dot_tn(a, b):
    return _dot(a, b, (((0,), (0,)), ((), ())))


def _rms(x):
    return x * lax.rsqrt(jnp.mean(x * x, axis=-1, keepdims=True) + EPS)


def _pick(n, prefs):
    for p in prefs:
        if n % p == 0:
            return p
    return n


def _row(arr, width=None, colblk=0, tm=TM, valid=None):
    width = arr.shape[1] if width is None else width
    if valid is None:
        return ([arr], [pl.BlockSpec((tm, width), lambda i, c=colblk: (i, c))], lambda r: r[...].astype(F32), width)
    spec = pl.BlockSpec((tm, width), lambda i, c=colblk: (jnp.minimum(i, valid - 1), c))
    return ([arr], [spec], lambda r: jnp.where(pl.program_id(0) < valid, r[...].astype(F32), 0.0), width)


def _row_grid(arr, a, nb):
    n = arr.shape[0]
    b = (n - CTX) // a

    def load(v_ref, c_ref):
        i = pl.program_id(0)
        return jnp.where(i == nb - 1, c_ref[...], _grid_rows(v_ref, a, i))

    return ([arr.reshape(n // b, b, D), arr], [_grid_spec(a, nb), pl.BlockSpec((TM, D), lambda i: (nb - 1, 0))], load, D)


def _row_cat(x, ctx, nb):
    return ([x, ctx], [pl.BlockSpec((TM, D), lambda i: (jnp.minimum(i, nb - 2), 0)), pl.BlockSpec((TM, D), lambda i: (0, 0))],
            lambda x_ref, c_ref: jnp.where(pl.program_id(0) == nb - 1, c_ref[...], x_ref[...]), D)


def _row_mm(arr, w):
    return ([arr, w], [pl.BlockSpec((TM, arr.shape[1]), lambda i: (i, 0)), _full_spec(w)],
            lambda a_ref, w_ref: _dot(a_ref[...], w_ref[...]), w.shape[1])


def _operands(rows):
    return [a for r in rows for a in r[0]], [s for r in rows for s in r[1]]


def _load_rows(refs, rows):
    vals, k = [], 0
    for r in rows:
        vals.append(r[2](*refs[k:k + len(r[0])]))
        k += len(r[0])
    return vals


def _full_spec(p):
    nd = p.ndim
    return pl.BlockSpec(p.shape, lambda i, nd=nd: (0,) * nd)


def _rowwise(name, fn, n_blocks, ctx_blk, rows, params, outs, tm=TM, post_mm=None):
    arrs, specs = _operands(rows)
    nr, npar = len(arrs), len(params)
    extra = [] if post_mm is None else [post_mm[1]]
    outs = list(outs) + [(w.shape[1], F32) for w in extra]

    def body(*refs):
        t = (pl.program_id(0) >= ctx_blk).astype(F32)
        rv = _load_rows(refs[:nr], rows)
        pv = [p[...] for p in refs[nr:nr + npar]]
        res = list(fn(t, rv, pv))
        o_refs = refs[nr + npar + len(extra):]
        if extra:
            res.append(_dot(res[post_mm[0]].astype(o_refs[post_mm[0]].dtype), refs[nr + npar][...]))
        for o_ref, o in zip(o_refs, res):
            o_ref[...] = o.astype(o_ref.dtype)

    return pl.pallas_call(
        body, name=name, grid=(n_blocks,),
        in_specs=specs + [_full_spec(p) for p in params + extra],
        out_specs=[pl.BlockSpec((tm, w), lambda i: (i, 0)) for w, _ in outs],
        out_shape=[jax.ShapeDtypeStruct((n_blocks * tm, w), dt) for w, dt in outs],
        compiler_params=_cp(("parallel",)),
    )(*arrs, *params, *extra)


def _rowwise_vjp(name, fn, n_blocks, ctx_blk, rows, params, douts, row_grads, tm=TM, x_rows_only=False, cot_mm=None):
    out_blocks = n_blocks - 1 if x_rows_only else n_blocks
    adds = [a for _, _, a in row_grads if a is not None]
    (r_arrs, r_specs), (d_arrs, d_specs), (a_arrs, a_specs) = _operands(rows), _operands(douts), _operands(adds)
    nr, npar, nd, na = len(r_arrs), len(params), len(d_arrs), len(a_arrs)
    extra = [] if cot_mm is None else [cot_mm[1]]

    def body(*refs):
        i = pl.program_id(0)
        t = (i >= ctx_blk).astype(F32)
        rv = _load_rows(refs[:nr], rows)
        pv = [p[...] for p in refs[nr:nr + npar]]
        dv = _load_rows(refs[nr + npar:nr + npar + nd], douts)
        av = _load_rows(refs[nr + npar + nd:nr + npar + nd + na], adds)
        o_refs = refs[nr + npar + nd + na + len(extra):]
        if extra:
            dv[cot_mm[0]] = _dot_nt(dv[cot_mm[0]], refs[nr + npar + nd + na][...])
        _, vjp = jax.vjp(lambda r, p: tuple(fn(t, r, p)), rv, pv)
        d_rows, d_params = vjp(tuple(dv))
        ai, grads = 0, []
        for ri, _, addend in row_grads:
            g = jnp.concatenate([d_rows[r] for r in ri], axis=1) if isinstance(ri, tuple) else d_rows[ri]
            if addend is not None:
                g = g + av[ai]
                ai += 1
            grads.append(g)

        @pl.when(i < out_blocks)
        def _():
            for o_ref, g in zip(o_refs, grads):
                o_ref[...] = g.astype(o_ref.dtype)

        p_refs = o_refs[len(row_grads):]

        @pl.when(i == 0)
        def _():
            for p_ref in p_refs:
                p_ref[...] = jnp.zeros_like(p_ref)

        for p_ref, g in zip(p_refs, d_params):
            p_ref[...] += g

    widths = [sum(rows[r][3] for r in ri) if isinstance(ri, tuple) else rows[ri][3] for ri, _, _ in row_grads]
    res = pl.pallas_call(
        body, name=name, grid=(n_blocks,),
        in_specs=r_specs + [_full_spec(p) for p in params] + d_specs + a_specs + [_full_spec(p) for p in extra],
        out_specs=[pl.BlockSpec((tm, w), lambda i: (jnp.minimum(i, out_blocks - 1), 0)) for w in widths] + [_full_spec(p) for p in params],
        out_shape=[jax.ShapeDtypeStruct((out_blocks * tm, w), dt) for w, (_, dt, _) in zip(widths, row_grads)]
        + [jax.ShapeDtypeStruct(p.shape, F32) for p in params],
        compiler_params=_cp(("arbitrary",)),
    )(*r_arrs, *params, *d_arrs, *a_arrs, *extra)
    return res[:len(row_grads)], res[len(row_grads):]


def _mm(name, a, b, mode, out_dtype):
    if mode == "nn":
        m, kk = a.shape
        n = b.shape[1]
    elif mode == "nt":
        m, kk = a.shape
        n = b.shape[0]
    else:
        kk, m = a.shape
        n = b.shape[1]
    if mode == "tn":
        tm = _pick(m, (1024, 1408, 512, 256, 128))
        tn = _pick(n, (768, 512, 256, 128))
        tk = kk
    else:
        tm = _pick(m, (1088, 1024, 768, 512, 384, 256, 128))
        tn = n if n <= 2816 else _pick(n, (1024, 768, 512, 256, 128))
        tk = kk if kk <= 2816 else _pick(kk, (2816, 1792, 1024, 768, 512, 256, 128))
    nk = kk // tk
    in_place = out_dtype == F32
    if mode == "nn":
        specs = [pl.BlockSpec((tm, tk), lambda i, j, k: (i, k)), pl.BlockSpec((tk, tn), lambda i, j, k: (k, j))]
        dims = (((1,), (0,)), ((), ()))
    elif mode == "nt":
        specs = [pl.BlockSpec((tm, tk), lambda i, j, k: (i, k)), pl.BlockSpec((tn, tk), lambda i, j, k: (j, k))]
        dims = (((1,), (1,)), ((), ()))
    else:
        specs = [pl.BlockSpec((tk, tm), lambda i, j, k: (k, i)), pl.BlockSpec((tk, tn), lambda i, j, k: (k, j))]
        dims = (((0,), (0,)), ((), ()))

    def body(a_ref, b_ref, o_ref, *scratch):
        part = lax.dot_general(a_ref[...].astype(MXU_DTYPE), b_ref[...].astype(MXU_DTYPE), dims, preferred_element_type=F32)
        if nk == 1:
            o_ref[...] = part.astype(o_ref.dtype)
        else:
            k = pl.program_id(2)
            acc = o_ref if in_place else scratch[0]

            @pl.when(k == 0)
            def _():
                acc[...] = part

            @pl.when(k > 0)
            def _():
                acc[...] += part

            if not in_place:
                @pl.when(k == nk - 1)
                def _():
                    o_ref[...] = acc[...].astype(o_ref.dtype)

    return pl.pallas_call(
        body, name=name, grid=(m // tm, n // tn, nk), in_specs=specs,
        out_specs=pl.BlockSpec((tm, tn), lambda i, j, k: (i, j)),
        out_shape=jax.ShapeDtypeStruct((m, n), out_dtype),
        scratch_shapes=[] if nk == 1 or in_place else [pltpu.VMEM((tm, tn), F32)],
        compiler_params=_cp(("parallel", "parallel", "arbitrary")),
    )(a, b)


def _sel_mod(modp, t):
    return modp[0:8] * (1.0 - t) + modp[8:16] * t


def _fn_prenorm(t, rows, params, *, a, b):
    (x,), (g, modp) = rows, params
    m = _sel_mod(modp, t)
    return ((_rms(x) * g) * (1.0 + m[b:b + 1]) + m[a:a + 1],)


def _fn_resid_prenorm(t, rows, params, *, gi, a, b):
    (x, y), (g, mod_a, mod_b) = rows, params
    ma, mb = _sel_mod(mod_a, t), _sel_mod(mod_b, t)
    xn = x + ma[gi:gi + 1] * y
    return xn, (_rms(xn) * g) * (1.0 + mb[b:b + 1]) + mb[a:a + 1]


def _fn_resid(t, rows, params, *, gi):
    (x, y), (mod_a,) = rows, params
    return (x + _sel_mod(mod_a, t)[gi:gi + 1] * y,)


def _fn_mixpost(t, rows, params):
    (o, r, u, g), (gla_g, vn_g, sw, sb_t) = rows, params
    a =jnp.concatenate([_rms(o[:, h * GLA_DV:(h + 1) * GLA_DV]) for h in range(GLA_H)], axis=1) * gla_g * jax.nn.silu(r)
    uu, vv = jax.nn.gelu(u), jax.nn.gelu(g)
    mu = jnp.mean(vv, axis=-1, keepdims=True)
    var = jnp.mean(jnp.square(vv - mu), axis=-1, keepdims=True)
    vn = ((vv - mu) * lax.rsqrt(var + EPS)) * vn_g
    s = jnp.concatenate(
        [_dot(sw[gi * GMLP_L:(gi + 1) * GMLP_L, :], vn[:, gi * GMLP_C:(gi + 1) * GMLP_C]) + sb_t[:, gi:gi + 1]
         for gi in range(GMLP_G)], axis=1)
    return (jnp.concatenate([a, uu * s], axis=1),)


def _expand_heads(row):
    first = lax.broadcasted_iota(jnp.int32, (1, 2 * SSD_P), 1) < SSD_P
    return jnp.concatenate([jnp.where(first, row[:, 2 * j:2 * j + 1], row[:, 2 * j + 1:2 * j + 2]) for j in range(SSD_H // 2)], axis=1)


def _fn_ssd_finish(t, rows, params):
    (y2, xs, z), (d_skip, norm_g) = rows, params
    d_full = _expand_heads(d_skip)
    y = (y2 + d_full * xs) * jax.nn.silu(z)
    gw = SSD_INNER // SSD_G
    return (jnp.concatenate([_rms(y[:, gi * gw:(gi + 1) * gw]) for gi in range(SSD_G)], axis=1) * norm_g,)


def _fn_concat(t, rows, params, *, sums, pad=0):
    out, i = [], 0
    for n in sums:
        acc = rows[i]
        for j in range(1, n):
            acc = acc + rows[i + j]
        out.append(acc)
        i += n
    if pad:
        out.append(jnp.zeros((out[0].shape[0], pad), F32))
    return (jnp.concatenate(out, axis=1),)


def _tri(n, rev):
    r = lax.broadcasted_iota(jnp.int32, (n, n), 0)
    c = lax.broadcasted_iota(jnp.int32, (n, n), 1)
    return (r <= c) if rev else (r >= c)


def _running_sum(x, rev):
    n, s = x.shape[0], 1
    while s < n:
        z = jnp.zeros((s, x.shape[1]), x.dtype)
        x = x + (jnp.concatenate([x[s:], z], axis=0) if rev else jnp.concatenate([z, x[:n - s]], axis=0))
        s *= 2
    return x


def _gla_chunk(S, v, k, q, tail, gw, gb, *, rev):
    L, H = GLA_L, GLA_H
    lr = tail[:, GLA_LR:2 * GLA_LR] if rev else tail[:, 0:GLA_LR]
    la = jax.nn.log_sigmoid(_dot(lr, gw) + gb) / GLA_TAU
    b = _running_sum(la, rev)
    b_last = b[0:1] if rev else b[L - 1:L]
    kd = k * jnp.exp(b_last - b)
    qd = (q * GLA_DK ** -0.5) * jnp.exp(b)
    ki = k * jnp.exp(-b)

    def same_head(shape, rows_per_head, cols_per_head):
        r = lax.broadcasted_iota(jnp.int32, shape, 0) // rows_per_head
        c = lax.broadcasted_iota(jnp.int32, shape, 1) // cols_per_head
        return r == c

    k_blk = jnp.where(same_head((H * L, H * GLA_DK), L, GLA_DK), jnp.concatenate([ki] * H, axis=0), 0.0)
    v_blk = jnp.where(same_head((H * L, H * GLA_DV), L, GLA_DV), jnp.concatenate([v] * H, axis=0), 0.0)
    row = lax.broadcasted_iota(jnp.int32, (L, H * L), 0)
    src = lax.broadcasted_iota(jnp.int32, (L, H * L), 1) % L
    sc = jnp.where((row <= src) if rev else (row >= src), _dot_nt(qd, k_blk), 0.0)
    o = _dot_nt(qd, S) + _dot(sc, v_blk)
    s_new = S * jnp.exp(b_last) + jnp.where(same_head(S.shape, GLA_DV, GLA_DK), _dot_tn(v, kd), 0.0)
    return s_new, o


def _ssd_chunk(S, x, bm, cm, tail, dtb, alog, *, rev):
    L = SSD_L
    msk = _tri(L, rev)
    raw = tail[:, SSD_H:2 * SSD_H] if rev else tail[:, 0:SSD_H]
    dt = jax.nn.softplus(raw + dtb)
    acum = _running_sum(dt * (-jnp.exp(alog)), rev)
    a_last = acum[0:1] if rev else acum[L - 1:L]
    wst = dt * jnp.exp(a_last - acum)
    eac = jnp.exp(acum)
    dec = jnp.exp(a_last)
    tr = jnp.concatenate([acum, dt, wst, jnp.zeros((L, L - 3 * SSD_H), F32)], axis=1).T
    acum_t, dt_t, wst_t = tr[0:SSD_H], tr[SSD_H:2 * SSD_H], tr[2 * SSD_H:3 * SSD_H]
    lane = lax.broadcasted_iota(jnp.int32, (1, 2 * SSD_P), 1)
    m0 = (lane < SSD_P).astype(F32)
    m1 = 1.0 - m0
    pairs_per_group = SSD_H // SSD_G // 2
    y_parts, s_parts = [], []
    for g in range(SSD_G):
        ns = slice(g * SSD_N, (g + 1) * SSD_N)
        bg, cg = bm[:, ns], cm[:, ns]
        cb = _dot_nt(cg, bg)
        bgt = bg.T
        gs = slice(g * pairs_per_group * 2 * SSD_P, (g + 1) * pairs_per_group * 2 * SSD_P)
        y_carry = _dot(cg, S[:, gs])
        for jj in range(pairs_per_group):
            j = g * pairs_per_group + jj
            ls = slice(j * 2 * SSD_P, (j + 1) * 2 * SSD_P)
            xp, sp = x[:, ls], S[:, ls]
            xm = jnp.concatenate([xp * m0, xp * m1], axis=0)
            lhs, bw = [], []
            for h in (2 * j, 2 * j + 1):
                seg = acum[:, h:h + 1] - acum_t[h:h + 1, :]
                lhs.append(cb * jnp.exp(jnp.where(msk, seg, -jnp.inf)) * dt_t[h:h + 1, :])
                bw.append(bgt * wst_t[h:h + 1, :])
            e_pair = eac[:, 2 * j:2 * j + 1] * m0 + eac[:, 2 * j + 1:2 * j + 2] * m1
            y_parts.append(_dot(jnp.concatenate(lhs, axis=1), xm) + y_carry[:, jj * 2 * SSD_P:(jj + 1) * 2 * SSD_P] * e_pair)
            d_pair = dec[:, 2 * j:2 * j + 1] * m0 + dec[:, 2 * j + 1:2 * j + 2] * m1
            s_parts.append(sp * d_pair + _dot(jnp.concatenate(bw, axis=1), xm))
    return jnp.concatenate(s_parts, axis=1), jnp.concatenate(y_parts, axis=1)


def _multi_chunk(chunk_fn, L, subs, nr):
    def fn(S, *args, rev):
        rows, params = args[:nr], args[nr:]
        ys = [None] * subs
        for j in (range(subs - 1, -1, -1) if rev else range(subs)):
            S, ys[j] = chunk_fn(S, *[r[j * L:(j + 1) * L] for r in rows], *params, rev=rev)
        return S, jnp.concatenate(ys, axis=0)

    return fn


def _scan_order(n, nx, rev, backward):
    nc = n - nx

    def fwd(s):
        return (n - 1 - s) if rev else jnp.where(s < nc, s + nx, s - nc)

    return (lambda s: fwd(n - 1 - s)) if backward else fwd


def _scan_fwd(name, chunk_fn, L, n, nx, rev, rows, params, state_shape, out_w, addend=None):
    order = _scan_order(n, nx, rev, False)
    nr, npar = len(rows), len(params)
    adds = [] if addend is None else [addend]

    def body(*refs):
        s_scr = refs[-1]

        @pl.when(pl.program_id(0) == 0)
        def _():
            s_scr[...] = jnp.zeros_like(s_scr)

        s_in = s_scr[...]
        y_ref, st_ref = refs[nr + npar + len(adds)], refs[nr + npar + len(adds) + 1]
        st_ref[0] = s_in
        s_new, y = chunk_fn(s_in, *[r[...] for r in refs[:nr]], *[p[...] for p in refs[nr:nr + npar]], rev=rev)
        y_ref[...] = y + refs[nr + npar][...] if adds else y
        s_scr[...] = s_new

    return pl.pallas_call(
        body, name=name, grid=(n,),
        in_specs=[pl.BlockSpec((L, w), lambda s, c=c: (order(s), c)) for _, w, c in rows] + [_full_spec(p) for p in params]
        + [pl.BlockSpec((L, out_w), lambda s: (order(s), 0)) for _ in adds],
        out_specs=[pl.BlockSpec((L, out_w), lambda s: (order(s), 0)),
                   pl.BlockSpec((1,) + state_shape, lambda s: (order(s), 0, 0))],
        out_shape=[jax.ShapeDtypeStruct((n * L, out_w), F32), jax.ShapeDtypeStruct((n,) + state_shape, F32)],
        scratch_shapes=[pltpu.VMEM(state_shape, F32)],
        compiler_params=_cp(("arbitrary",)),
    )(*[a for a, _, _ in rows], *params, *adds)


def _scan_bwd(name, chunk_fn, L, n, nx, rev, rows, params, states, dy, state_shape, out_w, addends=None):
    order = _scan_order(n, nx, rev, True)
    dy_blocks = dy.shape[0] // L
    nr, npar = len(rows), len(params)
    adds = [] if addends is None else list(addends)

    def body(*refs):
        i = pl.program_id(0)
        ds_scr = refs[-1]
        rv = [r[...] for r in refs[:nr]]
        pv = [p[...] for p in refs[nr:nr + npar]]
        st_ref, dy_ref = refs[nr + npar], refs[nr + npar + 1]
        a_refs = refs[nr + npar + 2:nr + npar + 2 + len(adds)]
        o_refs = refs[nr + npar + 2 + len(adds):-1]
        p_refs = o_refs[nr:]

        @pl.when(i == 0)
        def _():
            ds_scr[...] = jnp.zeros_like(ds_scr)
            for p_ref in p_refs:
                p_ref[...] = jnp.zeros_like(p_ref)

        _, vjp = jax.vjp(functools.partial(chunk_fn, rev=rev), st_ref[0], *rv, *pv)
        dy_blk = jnp.where(order(i) < dy_blocks, dy_ref[...].astype(F32), 0.0)
        grads = vjp((ds_scr[...], dy_blk))
        ds_scr[...] = grads[0]
        for j, (o_ref, g) in enumerate(zip(o_refs[:nr], grads[1:1 + nr])):
            o_ref[...] = g + a_refs[j][...] if adds else g
        for p_ref, g in zip(p_refs, grads[1 + nr:]):
            p_ref[...] += g

    row_specs = [pl.BlockSpec((L, w), lambda s: (order(s), 0)) for _, w, _ in rows]
    res = pl.pallas_call(
        body, name=name, grid=(n,),
        in_specs=[pl.BlockSpec((L, w), lambda s, c=c: (order(s), c)) for _, w, c in rows] + [_full_spec(p) for p in params]
        + [pl.BlockSpec((1,) + state_shape, lambda s: (order(s), 0, 0)),
           pl.BlockSpec((L, out_w), lambda s: (jnp.minimum(order(s), dy_blocks - 1), 0))]
        + row_specs[:len(adds)],
        out_specs=row_specs + [_full_spec(p) for p in params],
        out_shape=[jax.ShapeDtypeStruct((n * L, w), F32) for _, w, _ in rows] + [jax.ShapeDtypeStruct(p.shape, F32) for p in params],
        scratch_shapes=[pltpu.VMEM(state_shape, F32)],
        compiler_params=_cp(("arbitrary",)),
    )(*[a for a, _, _ in rows], *params, states, dy, *adds)
    return res[:nr], res[nr:]


CONV_W = 1024
CONV_COLBLK = (0, 1, 4)


def _conv_specs(nb, src_blk):
    halo = TM // 8
    return [pl.BlockSpec((TM, CONV_W), lambda j, i: (i, src_blk(j))),
            pl.BlockSpec((8, CONV_W), lambda j, i: (jnp.maximum(i * halo - 1, 0), src_blk(j))),
            pl.BlockSpec((8, CONV_W), lambda j, i: (jnp.minimum(i * halo + halo, nb * halo - 1), src_blk(j)))]


def _conv_ext(i, nb, cur, prev, nxt):
    has_prev = jnp.logical_and(i > 0, i < nb - 1)
    has_next = i < nb - 2
    return jnp.concatenate([jnp.where(has_prev, prev, 0.0), cur, jnp.where(has_next, nxt, 0.0)], axis=0)


def _conv_taps(ext, w, flip):
    acc = None
    for j in range(SSD_K):
        wj = w[SSD_K - 1 - j:SSD_K - j, :] if flip else w[j:j + 1, :]
        term = wj * ext[6 + j:6 + j + TM, :]
        acc = term if acc is None else acc + term
    return acc


def _conv(name, src, w8, b1, nb, *, permuted_src, act, flip, out_dtype):
    src_blk = (lambda j: jnp.where(j == 2, CONV_COLBLK[2], j)) if permuted_src else (lambda j: j)

    def body(cur, prev, nxt, w_ref, b_ref, o_ref):
        ext = _conv_ext(pl.program_id(1), nb, cur[...].astype(F32), prev[...].astype(F32), nxt[...].astype(F32))
        acc = _conv_taps(ext, w_ref[...], flip)
        if act:
            acc = jax.nn.silu(acc + b_ref[...])
        o_ref[...] = acc.astype(o_ref.dtype)

    return pl.pallas_call(
        body, name=name, grid=(3, nb),
        in_specs=_conv_specs(nb, src_blk) + [pl.BlockSpec((8, CONV_W), lambda j, i: (0, j)), pl.BlockSpec((1, CONV_W), lambda j, i: (0, j))],
        out_specs=pl.BlockSpec((TM, CONV_W), lambda j, i: (i, j)),
        out_shape=jax.ShapeDtypeStruct((nb * TM, 3 * CONV_W), out_dtype),
        compiler_params=_cp(("parallel", "parallel")),
    )(src, src, src, w8, b1)


def _conv_bwd_pre(name, p1, w8, b1, dxbc_parts, nb):
    src_blk = lambda j: jnp.where(j == 2, CONV_COLBLK[2], j)
    xs_parts, bc_parts = dxbc_parts
    n_x, n_bc = len(xs_parts), len(bc_parts)
    x_blocks = [p.shape[0] // TM for p in xs_parts]

    def body(*refs):
        cur, prev, nxt, w_ref, b_ref = refs[:5]
        d_refs = refs[5:5 + n_x + n_bc]
        da_ref, dw_ref, db_ref = refs[5 + n_x + n_bc:]
        j, i = pl.program_id(0), pl.program_id(1)
        ext = _conv_ext(i, nb, cur[...], prev[...], nxt[...])
        acc = _conv_taps(ext, w_ref[...], False) + b_ref[...]
        dx = d_refs[0][...]
        for r, blocks in zip(d_refs[1:n_x], x_blocks[1:]):
            dx = dx + jnp.where(i < blocks, r[...], 0.0)
        dbc = jnp.concatenate([d_refs[n_x][...], d_refs[n_x + 1][...]], axis=1)
        dy = jnp.where(j == 2, dbc, dx)
        sg = jax.nn.sigmoid(acc)
        da = dy * (sg + acc * sg * (1.0 - sg))
        da_ref[...] = da

        @pl.when(i == 0)
        def _():
            dw_ref[...] = jnp.zeros_like(dw_ref)
            db_ref[...] = jnp.zeros_like(db_ref)

        rows = [jnp.sum(da * ext[6 + t:6 + t + TM, :], axis=0, keepdims=True) for t in range(SSD_K)]
        dw_ref[...] += jnp.concatenate(rows + [jnp.zeros((8 - SSD_K, CONV_W), F32)], axis=0)
        db_ref[...] += jnp.sum(da, axis=0, keepdims=True)

    x_specs = [pl.BlockSpec((TM, CONV_W), lambda j, i, b=b: (jnp.minimum(i, b - 1), jnp.minimum(j, 1))) for b in x_blocks]
    bc_specs = [pl.BlockSpec((TM, 512), lambda j, i: (i, 0)) for _ in bc_parts]
    return pl.pallas_call(
        body, name=name, grid=(3, nb),
        in_specs=_conv_specs(nb, src_blk) + [pl.BlockSpec((8, CONV_W), lambda j, i: (0, j)), pl.BlockSpec((1, CONV_W), lambda j, i: (0, j))]
        + x_specs + bc_specs,
        out_specs=[pl.BlockSpec((TM, CONV_W), lambda j, i: (i, j)), pl.BlockSpec((8, CONV_W), lambda j, i: (0, j)),
                   pl.BlockSpec((1, CONV_W), lambda j, i: (0, j))],
        out_shape=[jax.ShapeDtypeStruct((nb * TM, 3 * CONV_W), F32), jax.ShapeDtypeStruct((8, 3 * CONV_W), F32),
                   jax.ShapeDtypeStruct((1, 3 * CONV_W), F32)],
        compiler_params=_cp(("arbitrary", "arbitrary")),
    )(p1, p1, p1, w8, b1, *xs_parts, *bc_parts)


def _grid_block(a):
    nbv = TM // a
    blk_b = max(nbv, 8)
    return nbv, blk_b, blk_b // nbv


def _grid_spec(a, nb):
    _, blk_b, per = _grid_block(a)
    return pl.BlockSpec((a, blk_b, D), lambda i: (0, jnp.minimum(i, nb - 2) // per, 0))


def _grid_rows(v_ref, a, i):
    nbv, _, per = _grid_block(a)

    def pick(ph):
        return jnp.concatenate([v_ref[:, ph * nbv + t, :] for t in range(nbv)], axis=0)

    out = pick(0)
    for ph in range(1, per):
        out = jnp.where(i % per == ph, pick(ph), out)
    return out


def _loss_head(x, f, target, modp, g_final, rows_r):
    tview = target.reshape(rows_r, target.shape[0] // rows_r, D)
    nb = x.shape[0] // TM + 1

    def fn(x_, f_, tgt, modp_, g_):
        xn = x_ + _sel_mod(modp_, 0.0)[5:6] * f_
        err = _rms(xn) * g_ - tgt
        return 0.5 * jnp.sum(jnp.mean(err * err, axis=-1))

    def body(x_ref, f_ref, t_ref, m_ref, g_ref, l_ref, dx_ref, df_ref, dm_ref, dg_ref):
        i = pl.program_id(0)
        tgt = _grid_rows(t_ref, rows_r, i)
        l, vjp = jax.vjp(lambda a_, b_, c_, d_: fn(a_, b_, tgt, c_, d_), x_ref[...], f_ref[...], m_ref[...], g_ref[...])
        dx, df, dm, dg = vjp(jnp.ones((), F32))

        @pl.when(i == 0)
        def _():
            l_ref[...] = jnp.zeros_like(l_ref)
            dm_ref[...] = jnp.zeros_like(dm_ref)
            dg_ref[...] = jnp.zeros_like(dg_ref)

        l_ref[...] += jnp.reshape(l, (1, 1))
        dx_ref[...] = dx
        df_ref[...] = df.astype(df_ref.dtype)
        dm_ref[...] += dm
        dg_ref[...] += dg

    rowspec = pl.BlockSpec((TM, D), lambda i: (i, 0))
    return pl.pallas_call(
        body, name="loss_head", grid=(nb - 1,),
        in_specs=[rowspec, rowspec, _grid_spec(rows_r, nb), _full_spec(modp), _full_spec(g_final)],
        out_specs=[pl.BlockSpec((1, 1), lambda i: (0, 0)), rowspec, rowspec, _full_spec(modp), _full_spec(g_final)],
        out_shape=[jax.ShapeDtypeStruct((1, 1), F32), jax.ShapeDtypeStruct(x.shape, F32), jax.ShapeDtypeStruct(x.shape, MXU_DTYPE),
                   jax.ShapeDtypeStruct(modp.shape, F32), jax.ShapeDtypeStruct(g_final.shape, F32)],
        compiler_params=_cp(("arbitrary",)),
    )(x, f, tview, modp, g_final)


def _repack(name, shards, segs, wp):
    nd, kk, ws = shards.shape
    tr = 128
    used = sum(e - s for s, e in segs)

    def body(a_ref, o_ref):
        full = jnp.concatenate([a_ref[d].astype(F32) for d in range(nd)], axis=1)
        parts = [full[:, s:e] for s, e in segs]
        if wp > used:
            parts.append(jnp.zeros((tr, wp - used), F32))
        o_ref[...] = jnp.concatenate(parts, axis=1).astype(o_ref.dtype)

    return pl.pallas_call(
        body, name=name, grid=(kk // tr,),
        in_specs=[pl.BlockSpec((nd, tr, ws), lambda i: (0, i, 0))],
        out_specs=pl.BlockSpec((tr, wp), lambda i: (i, 0)),
        out_shape=jax.ShapeDtypeStruct((kk, wp), MXU_DTYPE),
        compiler_params=_cp(("parallel",)),
    )(shards)


def _unpack(name, dw, segs, ws, out_dtype):
    kk, wp = dw.shape
    tr = 128
    order = sorted(range(len(segs)), key=lambda i: segs[i][0])
    offs, o = [], 0
    for s, e in segs:
        offs.append(o)
        o += e - s

    def body(a_ref, o_ref):
        a = a_ref[...].astype(F32)
        full = jnp.concatenate([a[:, offs[i]:offs[i] + segs[i][1] - segs[i][0]] for i in order], axis=1)
        for d in range(NDEV):
            o_ref[d] = full[:, d * ws:(d + 1) * ws].astype(o_ref.dtype)

    return pl.pallas_call(
        body, name=name, grid=(kk // tr,),
        in_specs=[pl.BlockSpec((tr, wp), lambda i: (i, 0))],
        out_specs=pl.BlockSpec((NDEV, tr, ws), lambda i: (0, i, 0)),
        out_shape=jax.ShapeDtypeStruct((NDEV, kk, ws), out_dtype),
        compiler_params=_cp(("parallel",)),
    )(dw)


def _adam_math(w, g, m, v):
    m = ADAM_B1 * m + (1.0 - ADAM_B1) * g
    v = ADAM_B2 * v + (1.0 - ADAM_B2) * jnp.square(g)
    m_hat = m / (1.0 - ADAM_B1 ** ADAM_STEP)
    v_hat = v / (1.0 - ADAM_B2 ** ADAM_STEP)
    delta = -ADAM_LR * (m_hat / (jnp.sqrt(v_hat) + ADAM_EPS) + ADAM_WD * w)
    return delta, m, v


def _adam(name, w, parts, m, v, after):
    r, c = w.shape
    nsec, npart = len(parts), parts[0].shape[0]
    rs = r // nsec
    tr = _pick(rs, (256, 128, 64, 32, 16, 8)) if rs * c * 4 > (1 << 20) else rs
    tiles = rs // tr

    def body(w_ref, *refs):
        m_ref, v_ref, _, g_ref, d_ref, nm_ref, nv_ref = refs[nsec:]
        sec = pl.program_id(0) // tiles
        for a, p_ref in enumerate(refs[:nsec]):
            @pl.when(sec == a)
            def _(p_ref=p_ref):
                g = p_ref[0].astype(F32)
                for s in range(1, npart):
                    g = g + p_ref[s].astype(F32)
                delta, nm, nv = _adam_math(w_ref[...], g, m_ref[...], v_ref[...])
                g_ref[...], d_ref[...], nm_ref[...], nv_ref[...] = g, delta, nm, nv

    spec = pl.BlockSpec((tr, c), lambda i: (i, 0))
    part_specs = [pl.BlockSpec((npart, tr, c), lambda i, a=a: (0, jnp.clip(i - a * tiles, 0, tiles - 1), 0)) for a in range(nsec)]
    return pl.pallas_call(
        body, name=name, grid=(r // tr,),
        in_specs=[spec] + part_specs + [spec, spec, ANY],
        out_specs=[spec] * 4, out_shape=[jax.ShapeDtypeStruct((r, c), F32)] * 4,
        compiler_params=_cp(("parallel",)),
    )(w, *parts, m, v, after)


def _mod_fwd(c_all, mod_w):
    nl, _, ws = mod_w.shape

    def body(c_ref, w_ref, o_ref):
        o_ref[0] = _dot(jax.nn.silu(c_ref[...]), w_ref[0])

    return pl.pallas_call(
        body, name="mod_fwd", grid=(nl,),
        in_specs=[_full_spec(c_all), pl.BlockSpec((1, D, ws), lambda i: (i, 0, 0))],
        out_specs=pl.BlockSpec((1, 16, ws), lambda i: (i, 0, 0)),
        out_shape=jax.ShapeDtypeStruct((nl, 16, ws), F32),
        compiler_params=_cp(("parallel",)),
    )(c_all, mod_w)


def _mod_bwd(c_all, mod_w, dm):
    nl, _, ws = mod_w.shape

    def body(c_ref, w_ref, d_ref, dw_ref, dc_ref):
        dw_ref[0] = _dot_tn(jax.nn.silu(c_ref[...]), d_ref[0])
        dc_ref[0] = _dot_nt(d_ref[0], w_ref[0])

    return pl.pallas_call(
        body, name="mod_bwd", grid=(nl,),
        in_specs=[_full_spec(c_all), pl.BlockSpec((1, D, ws), lambda i: (i, 0, 0)), pl.BlockSpec((1, 16, ws), lambda i: (i, 0, 0))],
        out_specs=[pl.BlockSpec((1, D, ws), lambda i: (i, 0, 0)), pl.BlockSpec((1, 16, D), lambda i: (i, 0, 0))],
        out_shape=[jax.ShapeDtypeStruct((nl, D, ws), F32), jax.ShapeDtypeStruct((nl, 16, D), F32)],
        compiler_params=_cp(("parallel",)),
    )(c_all, mod_w, dm)


def _sum_parts(name, parts):
    npart, r, c = parts.shape

    def body(p_ref, o_ref):
        g = p_ref[0].astype(F32)
        for s in range(1, npart):
            g = g + p_ref[s].astype(F32)
        o_ref[...] = g

    return pl.pallas_call(body, name=name, out_shape=jax.ShapeDtypeStruct((r, c), F32), compiler_params=_cp())(parts)


MESH = pl.DeviceIdType.MESH
ANY = pl.BlockSpec(memory_space=pl.ANY)
N_PEERS = NDEV - 1


def _mesh_pos():
    return lax.axis_index("x"), lax.axis_index("y"), lax.axis_index("c")


def _slot(px, py, pc):
    return 4 * px + 2 * py + pc


def _two_level_gather(x_refs, o_refs, send_sems, recv_sems, local_sems):
    x, y, c = _mesh_pos()
    me, sibling = (x, y, c), (x, y, 1 - c)
    chips = [(1 - x, y), (x, 1 - y), (1 - x, 1 - y)]
    n = len(x_refs)

    def copy(a, k, block, to, src=None):
        dst = o_refs[a].at[_slot(*block)]
        return pltpu.make_async_remote_copy(src_ref=dst if src is None else src, dst_ref=dst, send_sem=send_sems.at[a, k],
                                            recv_sem=recv_sems.at[a, k], device_id=to, device_id_type=MESH)

    mine = [pltpu.make_async_copy(x_refs[a], o_refs[a].at[_slot(*me)], local_sems.at[a]) for a in range(n)]
    for cp in mine:
        cp.start()
    first = []
    for a in range(n):
        first.append(copy(a, 0, me, sibling, src=x_refs[a]))
        first += [copy(a, 1 + j, me, (*chip, c), src=x_refs[a]) for j, chip in enumerate(chips)]
    for cp in first:
        cp.start()
    passed = []
    for j, chip in enumerate(chips):
        for a in range(n):
            copy(a, 1 + j, (*chip, c), me).wait_recv()
            fwd = copy(a, 4 + j, (*chip, c), sibling)
            fwd.start()
            passed.append(fwd)
    for a in range(n):
        copy(a, 0, sibling, me).wait_recv()
        for j, chip in enumerate(chips):
            copy(a, 4 + j, (*chip, 1 - c), me).wait_recv()
    for cp in first + passed:
        cp.wait_send()
    for cp in mine:
        cp.wait()


def _ag_small(name, x):
    r, c = x.shape

    def body(x_ref, o_ref, send_sems, recv_sems, local_sems):
        _two_level_gather([x_ref], [o_ref], send_sems, recv_sems, local_sems)

    return pl.pallas_call(
        body, name=name, out_shape=jax.ShapeDtypeStruct((NDEV, r, c), x.dtype),
        in_specs=[pl.BlockSpec(memory_space=pltpu.VMEM)], out_specs=pl.BlockSpec(memory_space=pltpu.VMEM),
        scratch_shapes=[pltpu.SemaphoreType.DMA((1, N_PEERS)), pltpu.SemaphoreType.DMA((1, N_PEERS)), pltpu.SemaphoreType.DMA((1,))],
        compiler_params=pltpu.CompilerParams(vmem_limit_bytes=VMEM_LIMIT),
    )(x)


def _ag_big(name, shards):
    n = len(shards)

    def body(*refs):
        _two_level_gather(refs[:n], refs[n:2 * n], *refs[2 * n:])

    return pl.pallas_call(
        body, name=name, out_shape=[jax.ShapeDtypeStruct((NDEV,) + s.shape, s.dtype) for s in shards],
        in_specs=[ANY] * n, out_specs=[ANY] * n,
        scratch_shapes=[pltpu.SemaphoreType.DMA((n, N_PEERS)), pltpu.SemaphoreType.DMA((n, N_PEERS)), pltpu.SemaphoreType.DMA((n,))],
    )(*shards)


HBM = pl.BlockSpec(memory_space=pltpu.HBM)
SEM = pl.BlockSpec(memory_space=pltpu.SEMAPHORE)
EFFECT = pltpu.SideEffectType.DATAFLOW_SIDE_EFFECTING


def _peers(x, y, c):
    return [(k - 1, ((1 - x) if k & 4 else x, (1 - y) if k & 2 else y, (1 - c) if k & 1 else c)) for k in range(1, NDEV)]


def _xchg_copy(src_refs, land_refs, send_sems, recv_sems, a, k, peer, me, scatter):
    src = src_refs[a].at[_slot(*peer)] if scatter else src_refs[a]
    return pltpu.make_async_remote_copy(src_ref=src, dst_ref=land_refs[a].at[me], send_sem=send_sems.at[a * N_PEERS + k],
                                        recv_sem=recv_sems.at[a * N_PEERS + k], device_id=peer, device_id_type=MESH)


def _xchg_start(name, srcs, lands, deps, scatter):
    n, nd = len(srcs), len(deps)

    def body(*refs):
        src_refs, land_refs = refs[:n], refs[n:2 * n]
        send_sems, recv_sems, token = refs[2 * n + nd], refs[2 * n + nd + 1], refs[-1]
        x, y, c = _mesh_pos()
        me = _slot(x, y, c)
        for k, peer in _peers(x, y, c):
            for a in range(n):
                _xchg_copy(src_refs, land_refs, send_sems, recv_sems, a, k, peer, me, scatter).start()
        token[...] = jnp.zeros_like(token)

    res = pl.pallas_call(
        body, name=name,
        out_shape=(pltpu.SemaphoreType.DMA((n * N_PEERS,)), pltpu.SemaphoreType.DMA((n * N_PEERS,)),
                   *[pltpu.HBM(s.shape, s.dtype) for s in srcs], *[pltpu.HBM(s.shape, s.dtype) for s in lands],
                   jax.ShapeDtypeStruct((8, 128), F32)),
        in_specs=[HBM] * (2 * n) + [ANY] * nd,
        out_specs=(SEM, SEM, *([HBM] * (2 * n)), pl.BlockSpec(memory_space=pltpu.VMEM)),
        input_output_aliases={i: 2 + i for i in range(2 * n)},
        compiler_params=pltpu.CompilerParams(has_side_effects=EFFECT),
    )(*[pltpu.with_memory_space_constraint(s, pltpu.HBM) for s in srcs],
      *[pltpu.with_memory_space_constraint(s, pltpu.HBM) for s in lands], *deps)
    return res[0], res[1], res[2:2 + n], res[2 + n:2 + 2 * n], res[-1]


def _xchg_wait(name, send_sems, recv_sems, srcs, lands, after, scatter):
    n = len(srcs)

    def body(*refs):
        src_refs, land_refs = refs[:n], refs[n:2 * n]
        s_sems, r_sems = refs[2 * n], refs[2 * n + 1]
        x, y, c = _mesh_pos()
        me = _slot(x, y, c)
        for k, peer in _peers(x, y, c):
            for a in range(n):
                cp = _xchg_copy(src_refs, land_refs, s_sems, r_sems, a, k, peer, me, scatter)
                cp.wait_send()
                cp.wait_recv()

    res = pl.pallas_call(
        body, name=name,
        out_shape=[pltpu.HBM(s.shape, s.dtype) for s in srcs] + [pltpu.HBM(s.shape, s.dtype) for s in lands],
        in_specs=[HBM] * (2 * n) + [SEM, SEM, ANY], out_specs=[HBM] * (2 * n),
        input_output_aliases={i: i for i in range(2 * n)},
        compiler_params=pltpu.CompilerParams(has_side_effects=EFFECT),
    )(*srcs, *lands, send_sems, recv_sems, after)
    return res[n:]


def _landing(name, srcs, me, scatter):
    shapes = [s.shape[-2:] for s in srcs]

    def body(me_ref, *refs):
        for s_ref, o_ref in zip(refs[:len(srcs)], refs[len(srcs):]):
            o_ref[...] = s_ref[...].reshape(o_ref.shape)

    def slot_spec(r, c):
        return pl.BlockSpec((1, r, c), lambda i, me_ref: (me_ref[0], 0, 0))

    return pl.pallas_call(
        body, name=name, out_shape=[jax.ShapeDtypeStruct((NDEV, r, c), s.dtype) for s, (r, c) in zip(srcs, shapes)],
        grid_spec=pltpu.PrefetchScalarGridSpec(
            num_scalar_prefetch=1, grid=(1,),
            in_specs=[slot_spec(r, c) if scatter else pl.BlockSpec((r, c), lambda i, me_ref: (0, 0)) for r, c in shapes],
            out_specs=[slot_spec(r, c) for r, c in shapes]),
        compiler_params=_cp(("arbitrary",)),
    )(jnp.reshape(me, (1,)).astype(jnp.int32), *srcs)


STAGES = ("l0_mixer", "l0_ffn", "l1_mixer", "l1_ffn")
STAGE_LAYOUT = {"l0_mixer": (AB_SEGS, AB_P), "l1_mixer": (SSD_SEGS, SSD_P_W)}


class _Exchange:
    def __init__(self, shards, me):
        self.shards, self.me = shards, me
        self.pending, self.pending_grads, self.recv = {}, None, {}

    def _layout(self, stage):
        ws = self.shards[stage][0].shape[-1]
        return STAGE_LAYOUT.get(stage, (((0, NDEV * ws),), NDEV * ws)) + (ws,)

    def _start_gather(self, stage, deps):
        srcs = list(self.shards[stage])
        lands = _landing("own_" + stage, srcs, self.me, False)
        return _xchg_start("gather_start_" + stage, srcs, lands, deps, False)

    def get(self, stage, dep, thread):
        i = STAGES.index(stage)
        if i == 0:
            g_in, g_out = _ag_big("gather_" + stage, list(self.shards[stage]))
            ahead, deps = STAGES[1:3], [g_out, dep]
        else:
            ss, rs, srcs, lands, _ = self.pending.pop(stage)
            g_in, g_out = _xchg_wait("gather_wait_" + stage, ss, rs, srcs, lands, dep, False)
            ahead, deps = STAGES[i + 2:i + 3], [g_out]
        for nxt in ahead:
            self.pending[nxt] = self._start_gather(nxt, deps)
            deps = [self.pending[nxt][4]]
            thread = thread + self.pending[nxt][4][0, 0]
        segs, wp, _ = self._layout(stage)
        return _repack("repack_" + stage, g_in, segs, wp), g_out.reshape(-1, D), thread

    def put(self, stage, d_in, d_out, thread):
        segs, _, ws = self._layout(stage)
        parts = [_unpack("unpack_" + stage, d_in, segs, ws, MXU_DTYPE), d_out.reshape(NDEV, -1, D)]
        deps = [parts[0]]
        if self.pending_grads is not None:
            deps = [self.finish(parts[0])[0]]
        self.staged = (stage, parts)
        return thread if stage == STAGES[0] else thread + self.start_last(deps)[0, 0]

    def start_last(self, deps):
        stage, parts = self.staged
        lands = _landing("own_grad_" + stage, parts, self.me, True)
        self.pending_grads = (stage,) + _xchg_start("scatter_start_" + stage, parts, lands, deps, True)
        return self.pending_grads[5]

    def finish(self, after):
        stage, ss, rs, srcs, lands, _ = self.pending_grads
        self.recv[stage] = _xchg_wait("scatter_wait_" + stage, ss, rs, srcs, lands, after, True)
        self.pending_grads = None
        return self.recv[stage]


def _mm_swiglu(name, h, w_in):
    m, kk = h.shape
    f = w_in.shape[1] // 2
    tm = _pick(m, (272, 256, 128))

    def body(h_ref, wg_ref, wu_ref, pf_ref, act_ref):
        a = h_ref[...].astype(MXU_DTYPE)
        g = jnp.dot(a, wg_ref[...].astype(MXU_DTYPE), preferred_element_type=F32).astype(MXU_DTYPE)
        u = jnp.dot(a, wu_ref[...].astype(MXU_DTYPE), preferred_element_type=F32).astype(MXU_DTYPE)
        pf_ref[0] = g
        pf_ref[1] = u
        act_ref[...] = (jax.nn.silu(g.astype(F32)) * u.astype(F32)).astype(act_ref.dtype)

    return pl.pallas_call(
        body, name=name, grid=(m // tm,),
        in_specs=[pl.BlockSpec((tm, kk), lambda i: (i, 0)), pl.BlockSpec((kk, f), lambda i: (0, 0)), pl.BlockSpec((kk, f), lambda i: (0, 1))],
        out_specs=[pl.BlockSpec((2, tm, f), lambda i: (0, i, 0)), pl.BlockSpec((tm, f), lambda i: (i, 0))],
        out_shape=[jax.ShapeDtypeStruct((2, m, f), MXU_DTYPE), jax.ShapeDtypeStruct((m, f), MXU_DTYPE)],
        compiler_params=_cp(("parallel",)),
    )(h, w_in, w_in)


def _ffn_fwd(tag, h, w_in, w_out, nb, cb):
    pf, act = _mm_swiglu(tag + "_ffn_in", h, w_in)
    return pf, act, _mm(tag + "_ffn_out", act, w_out, "nn", F32)


def _mm_swiglu_bwd(name, df, w_out, pf):
    m, kk = df.shape
    f = w_out.shape[0]
    tm = _pick(m, (272, 256, 128))

    def body(d_ref, w_ref, pf_ref, o_ref):
        dact = lax.dot_general(d_ref[...].astype(MXU_DTYPE), w_ref[...].astype(MXU_DTYPE), (((1,), (1,)), ((), ())),
                               preferred_element_type=F32)
        g, u = pf_ref[0].astype(F32), pf_ref[1].astype(F32)
        sg = jax.nn.sigmoid(g)
        o_ref[:, 0:f] = (dact * u * (sg * (1.0 + g * (1.0 - sg)))).astype(o_ref.dtype)
        o_ref[:, f:2 * f] = (dact * (g * sg)).astype(o_ref.dtype)

    return pl.pallas_call(
        body, name=name, grid=(m // tm,),
        in_specs=[pl.BlockSpec((tm, kk), lambda i: (i, 0)), pl.BlockSpec((f, kk), lambda i: (0, 0)), pl.BlockSpec((2, tm, f), lambda i: (0, i, 0))],
        out_specs=pl.BlockSpec((tm, 2 * f), lambda i: (i, 0)),
        out_shape=jax.ShapeDtypeStruct((m, 2 * f), MXU_DTYPE),
        compiler_params=_cp(("parallel",)),
    )(df, w_out, pf)


def _ffn_bwd(tag, h, pf, act, df, w_in, w_out, nb, cb):
    dw_out = _mm(tag + "_ffn_out_dw", act, df, "tn", MXU_DTYPE)
    dpf = _mm_swiglu_bwd(tag + "_ffn_out_dx", df, w_out, pf)
    dw_in = _mm(tag + "_ffn_in_dw", h, dpf, "tn", MXU_DTYPE)
    return dw_out, dw_in, dpf


def _local_step(x, ctx, target, mod, P, comm):
    T = x.shape[0]
    N = T + CTX
    nb, cb = N // TM, N // TM - 1
    R = T // GRID_W
    mod0, mod1 = mod[0], mod[1]
    ng = P["norm_g"]
    g00, g01, g10, g11 = ng[0, 0][None], ng[0, 1][None], ng[1, 0][None], ng[1, 1][None]
    pre = functools.partial(_fn_prenorm, a=0, b=1)
    rpre = functools.partial(_fn_resid_prenorm, gi=2, a=3, b=4)
    res5 = functools.partial(_fn_resid, gi=5)
    dirs = (("f", False), ("b", True))

    xc0 = _row_cat(x, ctx, nb)
    w_ab_in, w_ab_out, g00 = comm.get("l0_mixer", mod, g00)
    h0, p0 = _rowwise("l0_prenorm_in", pre, nb, cb, [xc0], [g00, mod0], [(D, MXU_DTYPE)], post_mm=(0, w_ab_in))
    gla_rows = [(p0, 512, 0), (p0, 256, 8), (p0, 256, 9), (p0, 128, 20)]
    gla_blk = _multi_chunk(_gla_chunk, GLA_L, TM // GLA_L, len(gla_rows))
    gla_par = {d: [P["ab_gate_w"][int(r)], P["ab_gate_b"][int(r)][None]] for d, r in dirs}
    gla_state = (GLA_H * GLA_DV, GLA_H * GLA_DK)
    o, st0 = None, {}
    for d, rev in dirs:
        o, st0[d] = _scan_fwd("gla_fwd_" + d, gla_blk, TM, nb, cb, rev, gla_rows, gla_par[d], gla_state, GLA_H * GLA_DV, o)
    n128, cb128 = N // GMLP_L, T // GMLP_L
    mix_rows = [_row(o, tm=GMLP_L)] + [_row(p0, 512, j, tm=GMLP_L) for j in (1, 2, 3)]
    mix_par = [P["ab_gla_norm_g"], P["ab_vnorm_g"], P["ab_spatial_w"].reshape(GMLP_G * GMLP_L, GMLP_L), P["ab_spatial_b"].T]
    (cat0,) = _rowwise("l0_mix", _fn_mixpost, n128, cb128, mix_rows, mix_par, [(D, MXU_DTYPE)], tm=GMLP_L)
    w_fi0, w_fo0, g01 = comm.get("l0_ffn", cat0, g01)
    y0 = _row_mm(cat0, w_ab_out)
    x1, h1 = _rowwise("l0_out_ffn_prenorm", rpre, nb, cb, [xc0, y0], [g01, mod0, mod0], [(D, F32), (D, MXU_DTYPE)])
    pf0, act0, f0 = _ffn_fwd("l0", h1, w_fi0, w_fo0, nb, cb)
    w_ssd_in, w_ssd_out, g10 = comm.get("l1_mixer", f0, g10)
    x2p, h2, p1 = _rowwise("l0_resid_l1_prenorm_in", functools.partial(_fn_resid_prenorm, gi=5, a=0, b=1), nb, cb,
                           [_row_grid(x1, R, nb), _row_grid(f0, R, nb)], [g10, mod0, mod1], [(D, F32), (D, MXU_DTYPE)],
                           post_mm=(1, w_ssd_in))
    conv_w8 = jnp.concatenate([P["ssd_conv_w"], jnp.zeros((8 - SSD_K, 3 * CONV_W), F32)], axis=0)
    xbc = _conv("l1_conv", p1, conv_w8, P["ssd_conv_b"], nb, permuted_src=True, act=True, flip=False, out_dtype=F32)
    ssd_rows = [(xbc, SSD_INNER, 0), (xbc, 512, 4), (xbc, 512, 5), (p1, 128, 40)]
    ssd_blk = _multi_chunk(_ssd_chunk, SSD_L, TM // SSD_L, len(ssd_rows))
    ssd_par = {d: [P["ssd_dt_bias"][int(r)][None], P["ssd_a_log"][int(r)][None]] for d, r in dirs}
    ssd_state = (SSD_N, SSD_INNER)
    ys, st1 = None, {}
    for d, rev in dirs:
        ys, st1[d] = _scan_fwd("ssd_fwd_" + d, ssd_blk, TM, nb, cb, rev, ssd_rows, ssd_par[d], ssd_state, SSD_INNER, ys)
    fin_rows = [_row(ys), _row(xbc, SSD_INNER, 0), _row(p1, SSD_INNER, 1)]
    fin_par = [P["ssd_d"], P["ssd_norm_g"]]
    yn, y1 = _rowwise("l1_finish_out", _fn_ssd_finish, cb, cb, fin_rows, fin_par, [(SSD_INNER, MXU_DTYPE)], post_mm=(0, w_ssd_out))
    w_fi1, w_fo1, g11 = comm.get("l1_ffn", y1, g11)
    x3, h3 = _rowwise("l1_ffn_prenorm", rpre, cb, cb, [_row(x2p), _row(y1)], [g11, mod1, mod1], [(D, F32), (D, MXU_DTYPE)])
    pf1, act1, f1 = _ffn_fwd("l1", h3, w_fi1, w_fo1, cb, cb)
    loss, dx3, df1, dm1_j, d_final_g = _loss_head(x3, f1, target, mod1, P["final_norm_g"], R)

    dP = {"final_norm_g": d_final_g}
    dwo1, dwi1, dpf1 = _ffn_bwd("l1", h3, pf1, act1, df1, w_fi1, w_fo1, cb, cb)
    g11 = comm.put("l1_ffn", dwi1, dwo1, g11)
    (dx2p_a, dy1), (dg11, dm1_a, dm1_b) = _rowwise_vjp(
        "l1_ffn_in_dx_prenorm_bwd", rpre, cb, cb, [_row(x2p), _row(y1)], [g11, mod1, mod1], [_row(dx3), _row(dpf1)],
        [(0, F32, None), (1, MXU_DTYPE, None)], cot_mm=(1, w_fi1))
    d_ssd_out = _mm("l1_out_dw", yn, dy1, "tn", MXU_DTYPE)
    (dys, dxs, dz), (dP["ssd_d"], dP["ssd_norm_g"]) = _rowwise_vjp(
        "l1_out_dx_finish_bwd", _fn_ssd_finish, cb, cb, fin_rows, fin_par, [_row(dy1)],
        [(0, F32, None), (1, F32, None), (2, MXU_DTYPE, None)], cot_mm=(0, w_ssd_out))
    dssd, ddtb, dalog = None, [], []
    for d, rev in dirs:
        dssd, (ddtb_, dalog_) = _scan_bwd("ssd_bwd_" + d, ssd_blk, TM, nb, cb, rev, ssd_rows, ssd_par[d], st1[d], dys, ssd_state,
                                          SSD_INNER, dssd)
        ddtb.append(ddtb_); dalog.append(dalog_)
    dx_s, db_s, dc_s, dtl = dssd
    dP["ssd_dt_bias"] = jnp.concatenate(ddtb, axis=0)
    dP["ssd_a_log"] = jnp.concatenate(dalog, axis=0)
    dacc, dcw8, dP["ssd_conv_b"] = _conv_bwd_pre("l1_conv_bwd", p1, conv_w8, P["ssd_conv_b"], ([dx_s, dxs], [db_s, dc_s]), nb)
    dP["ssd_conv_w"] = dcw8[:SSD_K]
    dpc = _conv("l1_conv_dx", dacc, conv_w8, jnp.zeros((1, 3 * CONV_W), F32), nb, permuted_src=False, act=False, flip=True,
                out_dtype=MXU_DTYPE)
    cat1 = functools.partial(_fn_concat, sums=(1, 1, 1, 1), pad=SSD_P_W - 5248)
    (dp1,) = _rowwise("l1_dp", cat1, nb, cb, [_row(dpc, SSD_INNER, 0), _row(dz, valid=cb), _row(dpc, 1024, 2), _row(dtl)],
                      [], [(SSD_P_W, MXU_DTYPE)])
    g10 = comm.put("l1_mixer", _mm("l1_in_dw", h2, dp1, "tn", F32), d_ssd_out, g10)
    (dx2p,), (dg10, dm1_f) = _rowwise_vjp("l1_in_dx_prenorm_bwd", pre, nb, cb, [_row(x2p)], [g10, mod1], [_row(dp1)],
                                          [(0, F32, _row(dx2p_a, valid=cb))], cot_mm=(0, w_ssd_in))

    (dx1_a, df0), (dm0_e,) = _rowwise_vjp("l0_resid_bwd", res5, nb, cb, [_row(x1), _row(f0)], [mod0],
                                          [_row_grid(dx2p, GRID_W, nb)], [(0, F32, None), (1, MXU_DTYPE, None)])
    dwo0, dwi0, dpf0 = _ffn_bwd("l0", h1, pf0, act0, df0, w_fi0, w_fo0, nb, cb)
    g01 = comm.put("l0_ffn", dwi0, dwo0, g01)
    (dxc0_a, dy0), (dg01, dm0_a, dm0_b) = _rowwise_vjp(
        "l0_ffn_in_dx_prenorm_bwd", rpre, nb, cb, [xc0, y0], [g01, mod0, mod0], [_row(dx1_a), _row(dpf0)],
        [(0, F32, None), (1, MXU_DTYPE, None)], cot_mm=(1, w_fi0))
    d_ab_out = _mm("l0_out_dw", cat0, dy0, "tn", MXU_DTYPE)
    dcat0 = _mm("l0_out_dx", dy0, w_ab_out, "nt", MXU_DTYPE)
    (do, dr, du, dgm), (dP["ab_gla_norm_g"], dP["ab_vnorm_g"], dsw, dsb_t) = _rowwise_vjp(
        "l0_mix_bwd", _fn_mixpost, n128, cb128, mix_rows, mix_par, [_row(dcat0, tm=GMLP_L)],
        [(0, F32, None), (1, MXU_DTYPE, None), (2, MXU_DTYPE, None), (3, MXU_DTYPE, None)], tm=GMLP_L)
    dP["ab_spatial_w"] = dsw.reshape(GMLP_G, GMLP_L, GMLP_L)
    dP["ab_spatial_b"] = dsb_t.T
    gl, dgw, dgb = None, [], []
    for d, rev in dirs:
        gl, (dgw_, dgb_) = _scan_bwd("gla_bwd_" + d, gla_blk, TM, nb, cb, rev, gla_rows, gla_par[d], st0[d], do,
                                     gla_state, GLA_H * GLA_DV, gl)
        dgw.append(dgw_[None]); dgb.append(dgb_)
    dP["ab_gate_w"] = jnp.concatenate(dgw, axis=0)
    dP["ab_gate_b"] = jnp.concatenate(dgb, axis=0)
    cat0f = functools.partial(_fn_concat, sums=(1,) * 7, pad=AB_P - 2688)
    (dp0,) = _rowwise("l0_dp", cat0f, nb, cb, [_row(gl[0]), _row(dr), _row(du), _row(dgm), _row(gl[1]), _row(gl[2]), _row(gl[3])],
                      [], [(AB_P, MXU_DTYPE)])
    g00 = comm.put("l0_mixer", _mm("l0_in_dw", h0, dp0, "tn", F32), d_ab_out, g00)
    (grad_x,), (dg00, dm0_s) = _rowwise_vjp("l0_in_dx_prenorm_bwd", pre, nb, cb, [xc0], [g00, mod0], [_row(dp0)],
                                            [(0, F32, _row(dxc0_a))], x_rows_only=True, cot_mm=(0, w_ab_in))
    dP["norm_g"] = jnp.concatenate([dg00, dg01, dg10, dg11], axis=0).reshape(2, 2, D)
    dmod = jnp.stack([dm0_s + dm0_a + dm0_b + dm0_e, dm1_f + dm1_a + dm1_b + dm1_j])
    return loss, grad_x, dmod, dP


WEIGHTS = ("c_ctx", "mod_w", "mod_b", "norm_g", "ffn_w_in", "ffn_w_out", "ab_w_in", "ab_gate_w", "ab_gate_b", "ab_gla_norm_g",
           "ab_vnorm_g", "ab_spatial_w", "ab_spatial_b", "ab_w_out", "ssd_w_in", "ssd_conv_w", "ssd_conv_b", "ssd_dt_bias",
           "ssd_a_log", "ssd_d", "ssd_norm_g", "ssd_w_out", "final_norm_g")
SMALL_SHARDED = ("norm_g", "ab_gate_w", "ab_gate_b", "ssd_conv_w", "ssd_conv_b", "ssd_norm_g")
SMALL = ("c_ctx", "mod_b", "norm_g", "ab_gate_w", "ab_gate_b", "ab_gla_norm_g", "ab_vnorm_g", "ab_spatial_w", "ab_spatial_b",
         "ssd_conv_w", "ssd_conv_b", "ssd_dt_bias", "ssd_a_log", "ssd_d", "ssd_norm_g", "final_norm_g")
LANES = 1024


def _pack(arrs, rows_multiple=8):
    flat = jnp.concatenate([a.reshape(-1).astype(F32) for a in arrs])
    rows = -(-flat.shape[0] // LANES)
    rows = -(-rows // rows_multiple) * rows_multiple
    return jnp.pad(flat, (0, rows * LANES - flat.shape[0])).reshape(rows, LANES)


def _unpack_flat(buf, shapes):
    lead = buf.shape[:-2]
    flat = buf.reshape(lead + (-1,))
    out, o = [], 0
    for s in shapes:
        n = math.prod(s)
        out.append(flat[..., o:o + n].reshape(lead + tuple(s)))
        o += n
    return out


def _unshard(g):
    g = jnp.moveaxis(g, 0, -2)
    return g.reshape(g.shape[:-2] + (g.shape[-2] * g.shape[-1],))


def _my_shard(full, me, ws):
    return lax.dynamic_slice_in_dim(full, me * ws, ws, axis=full.ndim - 1)


def _silu_vjp(cvec, dsc):
    def body(c_ref, d_ref, o_ref):
        _, vjp = jax.vjp(jax.nn.silu, c_ref[...])
        o_ref[...] = vjp(d_ref[...])[0]

    return pl.pallas_call(body, name="c_ctx_bwd", out_shape=jax.ShapeDtypeStruct(cvec.shape, F32), compiler_params=_cp())(cvec, dsc)


def kernel(x, c, ctx, c_ctx, mod_w, mod_b, norm_g, ffn_w_in, ffn_w_out, ab_w_in, ab_gate_w, ab_gate_b, ab_gla_norm_g, ab_vnorm_g, ab_spatial_w, ab_spatial_b, ab_w_out, ssd_w_in, ssd_conv_w, ssd_conv_b, ssd_dt_bias, ssd_a_log, ssd_d, ssd_norm_g, ssd_w_out, final_norm_g, loss_target, m_c_ctx, m_mod_w, m_mod_b, m_norm_g, m_ffn_w_in, m_ffn_w_out, m_ab_w_in, m_ab_gate_w, m_ab_gate_b, m_ab_gla_norm_g, m_ab_vnorm_g, m_ab_spatial_w, m_ab_spatial_b, m_ab_w_out, m_ssd_w_in, m_ssd_conv_w, m_ssd_conv_b, m_ssd_dt_bias, m_ssd_a_log, m_ssd_d, m_ssd_norm_g, m_ssd_w_out, m_final_norm_g, v_c_ctx, v_mod_w, v_mod_b, v_norm_g, v_ffn_w_in, v_ffn_w_out, v_ab_w_in, v_ab_gate_w, v_ab_gate_b, v_ab_gla_norm_g, v_ab_vnorm_g, v_ab_spatial_w, v_ab_spatial_b, v_ab_w_out, v_ssd_w_in, v_ssd_conv_w, v_ssd_conv_b, v_ssd_dt_bias, v_ssd_a_log, v_ssd_d, v_ssd_norm_g, v_ssd_w_out, v_final_norm_g):
    a = dict(locals())
    me = _slot(*_mesh_pos())
    ws_mod = mod_w.shape[-1]

    fwd_small = [c] + [a[k] for k in SMALL_SHARDED]
    g_small = _ag_small("gather_small", _pack(fwd_small))
    parts = _unpack_flat(g_small, [t.shape for t in fwd_small])
    c_rows = parts[0].reshape(NDEV, D)
    full = {k: _unshard(p) for k, p in zip(SMALL_SHARDED, parts[1:])}
    c_all = jnp.concatenate([c_rows, c_ctx[None], jnp.zeros((7, D), F32)], axis=0)
    m_all = _ag_small("gather_mod", _mod_fwd(c_all, mod_w).reshape(2 * 16, ws_mod)).reshape(NDEV, 2, 16, ws_mod)
    m_mine = lax.dynamic_index_in_dim(m_all, me, axis=2, keepdims=False)
    mx = jnp.moveaxis(m_mine, 0, 1).reshape(2, N_MOD, D) + mod_b.reshape(2, N_MOD, D)
    mc = jnp.moveaxis(m_all[:, :, 8, :], 0, 1).reshape(2, N_MOD, D) + mod_b.reshape(2, N_MOD, D)
    pad2 = jnp.zeros((2, 2, D), F32)
    mod = jnp.concatenate([mx, pad2, mc, pad2], axis=1)

    big = {"l0_mixer": (ab_w_in[0], ab_w_out[0]), "l0_ffn": (ffn_w_in[0], ffn_w_out[0]),
           "l1_mixer": (ssd_w_in[0], ssd_w_out[0]), "l1_ffn": (ffn_w_in[1], ffn_w_out[1])}
    comm = _Exchange({k: tuple(w.astype(MXU_DTYPE) for w in v) for k, v in big.items()}, me)
    P = {
        "norm_g": full["norm_g"], "ab_gate_w": full["ab_gate_w"][0], "ab_gate_b": full["ab_gate_b"][0],
        "ab_gla_norm_g": ab_gla_norm_g, "ab_vnorm_g": ab_vnorm_g, "ab_spatial_w": ab_spatial_w[0], "ab_spatial_b": ab_spatial_b[0],
        "ssd_conv_w": full["ssd_conv_w"][0], "ssd_conv_b": full["ssd_conv_b"], "ssd_dt_bias": ssd_dt_bias[0],
        "ssd_a_log": ssd_a_log[0], "ssd_d": ssd_d, "ssd_norm_g": full["ssd_norm_g"], "final_norm_g": final_norm_g[None],
    }

    loss, grad_x, dmod, dP = _local_step(x[0], ctx[0], loss_target[0], mod, P, comm)

    dmx, dmc = dmod[:, 0:N_MOD].reshape(2, N_MOD * D), dmod[:, 8:8 + N_MOD].reshape(2, N_MOD * D)
    small_names = ("ab_gate_w", "ab_gate_b", "ab_gla_norm_g", "ab_vnorm_g", "ab_spatial_w", "ab_spatial_b", "norm_g", "ssd_conv_w",
                   "ssd_conv_b", "ssd_dt_bias", "ssd_a_log", "ssd_d", "ssd_norm_g", "final_norm_g")
    bwd_small = [dP[k] for k in small_names] + [dmc, dmx]
    shapes = [t.shape for t in bwd_small]
    g_bwd = _ag_small("gather_small_grads", _pack(bwd_small))
    summed = _unpack_flat(_sum_parts("sum_small_grads", g_bwd), shapes)
    gfull = dict(zip(small_names, summed[:-2]))
    dmc_sum, dmx_sum = summed[-2], summed[-1]
    dmx_all = _unpack_flat(g_bwd, shapes)[-1]
    dmx_sh = jnp.moveaxis(_my_shard(dmx_all, me, ws_mod), 0, 1)
    dm = jnp.concatenate([dmx_sh, _my_shard(dmc_sum, me, ws_mod)[:, None, :], jnp.zeros((2, 7, ws_mod), F32)], axis=1)
    d_mod_w, dsc = _mod_bwd(c_all, mod_w, dm)
    dsc_ctx = (dsc[0, 8] + dsc[1, 8])[None]
    dsc_all = _ag_small("gather_c_ctx_grad", jnp.concatenate([dsc_ctx, jnp.zeros((7, D), F32)], axis=0))
    d_c_ctx = _silu_vjp(c_ctx[None], _sum_parts("sum_c_ctx_grad", dsc_all)[0:1])[0]

    g_small_w = {
        "c_ctx": d_c_ctx, "mod_b": dmx_sum + dmc_sum, "norm_g": gfull["norm_g"], "ab_gate_w": gfull["ab_gate_w"][None],
        "ab_gate_b": gfull["ab_gate_b"][None], "ab_gla_norm_g": gfull["ab_gla_norm_g"], "ab_vnorm_g": gfull["ab_vnorm_g"],
        "ab_spatial_w": gfull["ab_spatial_w"][None], "ab_spatial_b": gfull["ab_spatial_b"][None], "ssd_conv_w": gfull["ssd_conv_w"][None],
        "ssd_conv_b": gfull["ssd_conv_b"], "ssd_dt_bias": gfull["ssd_dt_bias"][None], "ssd_a_log": gfull["ssd_a_log"][None],
        "ssd_d": gfull["ssd_d"], "ssd_norm_g": gfull["ssd_norm_g"], "final_norm_g": gfull["final_norm_g"][0],
    }
    for k in SMALL_SHARDED:
        g_small_w[k] = _my_shard(g_small_w[k], me, a[k].shape[-1])
    token = comm.start_last([d_c_ctx])
    res = _adam("adam_small", _pack([a[k] for k in SMALL]), [_pack([g_small_w[k] for k in SMALL])[None]],
                _pack([a["m_" + k] for k in SMALL]), _pack([a["v_" + k] for k in SMALL]), token)
    out = {k: vals for k, vals in zip(SMALL, zip(*[_unpack_flat(r, [a[k].shape for k in SMALL]) for r in res]))}

    def adam_big(name, w2d, parts, m2d, v2d, shape):
        return tuple(r.reshape(shape) for r in _adam(name, w2d, parts, m2d, v2d, token))

    def flat2(t):
        return t.reshape(-1, t.shape[-1])

    out["mod_w"] = adam_big("adam_mod_w", flat2(mod_w), [d_mod_w.reshape(1, -1, ws_mod)], flat2(m_mod_w), flat2(v_mod_w), mod_w.shape)

    for j, k in enumerate(("ffn_w_in", "ffn_w_out")):
        out[k] = adam_big("adam_" + k, flat2(a[k]), [comm.recv["l0_ffn"][j], comm.recv["l1_ffn"][j]], flat2(a["m_" + k]),
                          flat2(a["v_" + k]), a[k].shape)
    for j, k in enumerate(("ssd_w_in", "ssd_w_out")):
        out[k] = adam_big("adam_" + k, a[k][0], [comm.recv["l1_mixer"][j]], a["m_" + k][0], a["v_" + k][0], a[k].shape)
    recv_ab = comm.finish(out["ssd_w_out"][3])
    for j, k in enumerate(("ab_w_in", "ab_w_out")):
        out[k] = adam_big("adam_" + k, a[k][0], [recv_ab[j]], a["m_" + k][0], a["v_" + k][0], a[k].shape)

    loss_all = lax.psum(loss[0, 0], ("x", "y", "c"))
    return (loss_all, grad_x[None], *[out[k][0] for k in WEIGHTS], *[out[k][1] for k in WEIGHTS],
            *[out[k][2] for k in WEIGHTS], *[out[k][3] for k in WEIGHTS])
```

```python
import functools
import math

import jax
import jax.numpy as jnp
from jax import lax
from jax.experimental import pallas as pl
from jax.experimental.pallas import tpu as pltpu

F32 = jnp.float32
BF16 = jnp.bfloat16
MXU_DTYPE = jnp.bfloat16

D = 1024
NDEV = 8
N_MOD = 6
EPS = 1e-6
GRID_W = 64
CTX = 256
TM = 256
D_FF = 2816
GLA_H, GLA_DK, GLA_DV, GLA_LR, GLA_TAU, GLA_L = 4, 64, 128, 16, 16.0, 64
GMLP_G, GMLP_C, GMLP_L = 4, 128, 128
SSD_H, SSD_P, SSD_G, SSD_N, SSD_L, SSD_K = 32, 64, 4, 128, 128, 5
SSD_INNER = SSD_H * SSD_P
AB_IN = 2592
SSD_IN = 5184
AB_SEGS = ((256, 768), (1056, 1568), (1568, 2080), (2080, 2592), (0, 256), (800, 1056), (768, 800))
AB_P = 2816
SSD_SEGS = ((0, 2048), (3136, 5184), (2048, 2560), (2560, 3072), (3072, 3136))
SSD_P_W = 5376
VMEM_LIMIT = 56 * 1024 * 1024

ADAM_LR, ADAM_B1, ADAM_B2, ADAM_EPS, ADAM_WD, ADAM_STEP = 0.001, 0.9, 0.999, 1e-08, 0.01, 10


def _cp(sem=None, **kw):
    return pltpu.CompilerParams(dimension_semantics=sem, vmem_limit_bytes=VMEM_LIMIT, **kw)


def _dot(a, b, dims=(((1,), (0,)), ((), ()))):
    return lax.dot_general(a.astype(MXU_DTYPE), b.astype(MXU_DTYPE), dims, preferred_element_type=F32)


def _dot_nt(a, b):
    return _dot(a, b, (((1,), (1,)), ((), ())))


def _dot_tn(a, b):
    return _dot(a, b, (((0,), (0,)), ((), ())))


def _rms(x):
    return x * lax.rsqrt(jnp.mean(x * x, axis=-1, keepdims=True) + EPS)


def _pick(n, prefs):
    for p in prefs:
        if n % p == 0:
            return p
    return n


def _row(arr, width=None, colblk=0, tm=TM, valid=None):
    width = arr.shape[1] if width is None else width
    if valid is None:
        return ([arr], [pl.BlockSpec((tm, width), lambda i, c=colblk: (i, c))], lambda r: r[...].astype(F32), width)
    spec = pl.BlockSpec((tm, width), lambda i, c=colblk: (jnp.minimum(i, valid - 1), c))
    return ([arr], [spec], lambda r: jnp.where(pl.program_id(0) < valid, r[...].astype(F32), 0.0), width)


def _row_grid(arr, a, nb):
    n = arr.shape[0]
    b = (n - CTX) // a

    def load(v_ref, c_ref):
        i = pl.program_id(0)
        return jnp.where(i == nb - 1, c_ref[...], _grid_rows(v_ref, a, i))

    return ([arr.reshape(n // b, b, D), arr], [_grid_spec(a, nb), pl.BlockSpec((TM, D), lambda i: (nb - 1, 0))], load, D)


def _row_cat(x, ctx, nb):
    return ([x, ctx], [pl.BlockSpec((TM, D), lambda i: (jnp.minimum(i, nb - 2), 0)), pl.BlockSpec((TM, D), lambda i: (0, 0))],
            lambda x_ref, c_ref: jnp.where(pl.program_id(0) == nb - 1, c_ref[...], x_ref[...]), D)


def _operands(rows):
    return [a for r in rows for a in r[0]], [s for r in rows for s in r[1]]


def _load_rows(refs, rows):
    vals, k = [], 0
    for r in rows:
        vals.append(r[2](*refs[k:k + len(r[0])]))
        k += len(r[0])
    return vals


def _full_spec(p):
    nd = p.ndim
    return pl.BlockSpec(p.shape, lambda i, nd=nd: (0,) * nd)


def _rowwise(name, fn, n_blocks, ctx_blk, rows, params, outs, tm=TM, post_mm=None):
    arrs, specs = _operands(rows)
    nr, npar = len(arrs), len(params)
    extra = [] if post_mm is None else [post_mm[1]]
    outs = list(outs) + [(w.shape[1], F32) for w in extra]

    def body(*refs):
        t = (pl.program_id(0) >= ctx_blk).astype(F32)
        rv = _load_rows(refs[:nr], rows)
        pv = [p[...] for p in refs[nr:nr + npar]]
        res = list(fn(t, rv, pv))
        o_refs = refs[nr + npar + len(extra):]
        if extra:
            res.append(_dot(res[post_mm[0]].astype(o_refs[post_mm[0]].dtype), refs[nr + npar][...]))
        for o_ref, o in zip(o_refs, res):
            o_ref[...] = o.astype(o_ref.dtype)

    return pl.pallas_call(
        body, name=name, grid=(n_blocks,),
        in_specs=specs + [_full_spec(p) for p in params + extra],
        out_specs=[pl.BlockSpec((tm, w), lambda i: (i, 0)) for w, _ in outs],
        out_shape=[jax.ShapeDtypeStruct((n_blocks * tm, w), dt) for w, dt in outs],
        compiler_params=_cp(("parallel",)),
    )(*arrs, *params, *extra)


def _rowwise_vjp(name, fn, n_blocks, ctx_blk, rows, params, douts, row_grads, tm=TM, x_rows_only=False, cot_mm=None):
    out_blocks = n_blocks - 1 if x_rows_only else n_blocks
    adds = [a for _, _, a in row_grads if a is not None]
    (r_arrs, r_specs), (d_arrs, d_specs), (a_arrs, a_specs) = _operands(rows), _operands(douts), _operands(adds)
    nr, npar, nd, na = len(r_arrs), len(params), len(d_arrs), len(a_arrs)
    extra = [] if cot_mm is None else [cot_mm[1]]

    def body(*refs):
        i = pl.program_id(0)
        t = (i >= ctx_blk).astype(F32)
        rv = _load_rows(refs[:nr], rows)
        pv = [p[...] for p in refs[nr:nr + npar]]
        dv = _load_rows(refs[nr + npar:nr + npar + nd], douts)
        av = _load_rows(refs[nr + npar + nd:nr + npar + nd + na], adds)
        o_refs = refs[nr + npar + nd + na + len(extra):]
        if extra:
            dv[cot_mm[0]] = _dot_nt(dv[cot_mm[0]], refs[nr + npar + nd + na][...])
        _, vjp = jax.vjp(lambda r, p: tuple(fn(t, r, p)), rv, pv)
        d_rows, d_params = vjp(tuple(dv))
        ai, grads = 0, []
        for ri, _, addend in row_grads:
            g = jnp.concatenate([d_rows[r] for r in ri], axis=1) if isinstance(ri, tuple) else d_rows[ri]
            if addend is not None:
                g = g + av[ai]
                ai += 1
            grads.append(g)

        @pl.when(i < out_blocks)
        def _():
            for o_ref, g in zip(o_refs, grads):
                o_ref[...] = g.astype(o_ref.dtype)

        p_refs = o_refs[len(row_grads):]

        @pl.when(i == 0)
        def _():
            for p_ref in p_refs:
                p_ref[...] = jnp.zeros_like(p_ref)

        for p_ref, g in zip(p_refs, d_params):
            p_ref[...] += g

    widths = [sum(rows[r][3] for r in ri) if isinstance(ri, tuple) else rows[ri][3] for ri, _, _ in row_grads]
    res = pl.pallas_call(
        body, name=name, grid=(n_blocks,),
        in_specs=r_specs + [_full_spec(p) for p in params] + d_specs + a_specs + [_full_spec(p) for p in extra],
        out_specs=[pl.BlockSpec((tm, w), lambda i: (jnp.minimum(i, out_blocks - 1), 0)) for w in widths] + [_full_spec(p) for p in params],
        out_shape=[jax.ShapeDtypeStruct((out_blocks * tm, w), dt) for w, (_, dt, _) in zip(widths, row_grads)]
        + [jax.ShapeDtypeStruct(p.shape, F32) for p in params],
        compiler_params=_cp(("arbitrary",)),
    )(*r_arrs, *params, *d_arrs, *a_arrs, *extra)
    return res[:len(row_grads)], res[len(row_grads):]


def _mm(name, a, b, mode, out_dtype):
    if mode == "nn":
        m, kk = a.shape
        n = b.shape[1]
    elif mode == "nt":
        m, kk = a.shape
        n = b.shape[0]
    else:
        kk, m = a.shape
        n = b.shape[1]
    if mode == "tn":
        tm = _pick(m, (1024, 1408, 512, 256, 128))
        tn = _pick(n, (768, 512, 256, 128))
        tk = kk
    else:
        tm = _pick(m, (1088, 1024, 768, 512, 384, 256, 128))
        tn = n if n <= 2816 else _pick(n, (1024, 768, 512, 256, 128))
        tk = kk if kk <= 2816 else _pick(kk, (2816, 1792, 1024, 768, 512, 256, 128))
    nk = kk // tk
    in_place = out_dtype == F32
    if mode == "nn":
        specs = [pl.BlockSpec((tm, tk), lambda i, j, k: (i, k)), pl.BlockSpec((tk, tn), lambda i, j, k: (k, j))]
        dims = (((1,), (0,)), ((), ()))
    elif mode == "nt":
        specs = [pl.BlockSpec((tm, tk), lambda i, j, k: (i, k)), pl.BlockSpec((tn, tk), lambda i, j, k: (j, k))]
        dims = (((1,), (1,)), ((), ()))
    else:
        specs = [pl.BlockSpec((tk, tm), lambda i, j, k: (k, i)), pl.BlockSpec((tk, tn), lambda i, j, k: (k, j))]
        dims = (((0,), (0,)), ((), ()))

    def body(a_ref, b_ref, o_ref, *scratch):
        part = lax.dot_general(a_ref[...].astype(MXU_DTYPE), b_ref[...].astype(MXU_DTYPE), dims, preferred_element_type=F32)
        if nk == 1:
            o_ref[...] = part.astype(o_ref.dtype)
        else:
            k = pl.program_id(2)
            acc = o_ref if in_place else scratch[0]

            @pl.when(k == 0)
            def _():
                acc[...] = part

            @pl.when(k > 0)
            def _():
                acc[...] += part

            if not in_place:
                @pl.when(k == nk - 1)
                def _():
                    o_ref[...] = acc[...].astype(o_ref.dtype)

    return pl.pallas_call(
        body, name=name, grid=(m // tm, n // tn, nk), in_specs=specs,
        out_specs=pl.BlockSpec((tm, tn), lambda i, j, k: (i, j)),
        out_shape=jax.ShapeDtypeStruct((m, n), out_dtype),
        scratch_shapes=[] if nk == 1 or in_place else [pltpu.VMEM((tm, tn), F32)],
        compiler_params=_cp(("parallel", "parallel", "arbitrary")),
    )(a, b)


def _sel_mod(modp, t):
    return modp[0:8] * (1.0 - t) + modp[8:16] * t


def _fn_prenorm(t, rows, params, *, a, b):
    (x,), (g, modp) = rows, params
    m = _sel_mod(modp, t)
    return ((_rms(x) * g) * (1.0 + m[b:b + 1]) + m[a:a + 1],)


def _fn_resid_prenorm(t, rows, params, *, gi, a, b):
    (x, y), (g, mod_a, mod_b) = rows, params
    ma, mb = _sel_mod(mod_a, t), _sel_mod(mod_b, t)
    xn = x + ma[gi:gi + 1] * y
    return xn, (_rms(xn) * g) * (1.0 + mb[b:b + 1]) + mb[a:a + 1]


def _fn_resid(t, rows, params, *, gi):
    (x, y), (mod_a,) = rows, params
    return (x + _sel_mod(mod_a, t)[gi:gi + 1] * y,)


def _fn_mixpost(t, rows, params):
    (o, r, u, g), (gla_g, vn_g, sw, sb_t) = rows, params
    a =jnp.concatenate([_rms(o[:, h * GLA_DV:(h + 1) * GLA_DV]) for h in range(GLA_H)], axis=1) * gla_g * jax.nn.silu(r)
    uu, vv = jax.nn.gelu(u), jax.nn.gelu(g)
    mu = jnp.mean(vv, axis=-1, keepdims=True)
    var = jnp.mean(jnp.square(vv - mu), axis=-1, keepdims=True)
    vn = ((vv - mu) * lax.rsqrt(var + EPS)) * vn_g
    s = jnp.concatenate(
        [_dot(sw[gi * GMLP_L:(gi + 1) * GMLP_L, :], vn[:, gi * GMLP_C:(gi + 1) * GMLP_C]) + sb_t[:, gi:gi + 1]
         for gi in range(GMLP_G)], axis=1)
    return (jnp.concatenate([a, uu * s], axis=1),)


def _expand_heads(row):
    first = lax.broadcasted_iota(jnp.int32, (1, 2 * SSD_P), 1) < SSD_P
    return jnp.concatenate([jnp.where(first, row[:, 2 * j:2 * j + 1], row[:, 2 * j + 1:2 * j + 2]) for j in range(SSD_H // 2)], axis=1)


def _fn_ssd_finish(t, rows, params):
    (y2, xs, z), (d_skip, norm_g) = rows, params
    d_full = _expand_heads(d_skip)
    y = (y2 + d_full * xs) * jax.nn.silu(z)
    gw = SSD_INNER // SSD_G
    return (jnp.concatenate([_rms(y[:, gi * gw:(gi + 1) * gw]) for gi in range(SSD_G)], axis=1) * norm_g,)


def _fn_concat(t, rows, params, *, sums, pad=0):
    out, i = [], 0
    for n in sums:
        acc = rows[i]
        for j in range(1, n):
            acc = acc + rows[i + j]
        out.append(acc)
        i += n
    if pad:
        out.append(jnp.zeros((out[0].shape[0], pad), F32))
    return (jnp.concatenate(out, axis=1),)


def _tri(n, rev):
    r = lax.broadcasted_iota(jnp.int32, (n, n), 0)
    c = lax.broadcasted_iota(jnp.int32, (n, n), 1)
    return (r <= c) if rev else (r >= c)


def _running_sum(x, rev):
    n, s = x.shape[0], 1
    while s < n:
        z = jnp.zeros((s, x.shape[1]), x.dtype)
        x = x + (jnp.concatenate([x[s:], z], axis=0) if rev else jnp.concatenate([z, x[:n - s]], axis=0))
        s *= 2
    return x


def _gla_chunk(S, v, k, q, tail, gw, gb, *, rev):
    L, H = GLA_L, GLA_H
    lr = tail[:, GLA_LR:2 * GLA_LR] if rev else tail[:, 0:GLA_LR]
    la = jax.nn.log_sigmoid(_dot(lr, gw) + gb) / GLA_TAU
    b = _running_sum(la, rev)
    b_last = b[0:1] if rev else b[L - 1:L]
    kd = k * jnp.exp(b_last - b)
    qd = (q * GLA_DK ** -0.5) * jnp.exp(b)
    ki = k * jnp.exp(-b)

    def same_head(shape, rows_per_head, cols_per_head):
        r = lax.broadcasted_iota(jnp.int32, shape, 0) // rows_per_head
        c = lax.broadcasted_iota(jnp.int32, shape, 1) // cols_per_head
        return r == c

    k_blk = jnp.where(same_head((H * L, H * GLA_DK), L, GLA_DK), jnp.concatenate([ki] * H, axis=0), 0.0)
    v_blk = jnp.where(same_head((H * L, H * GLA_DV), L, GLA_DV), jnp.concatenate([v] * H, axis=0), 0.0)
    row = lax.broadcasted_iota(jnp.int32, (L, H * L), 0)
    src = lax.broadcasted_iota(jnp.int32, (L, H * L), 1) % L
    sc = jnp.where((row <= src) if rev else (row >= src), _dot_nt(qd, k_blk), 0.0)
    o = _dot_nt(qd, S) + _dot(sc, v_blk)
    s_new = S * jnp.exp(b_last) + jnp.where(same_head(S.shape, GLA_DV, GLA_DK), _dot_tn(v, kd), 0.0)
    return s_new, o


def _ssd_chunk(S, x, bm, cm, tail, dtb, alog, *, rev):
    L = SSD_L
    msk = _tri(L, rev)
    raw = tail[:, SSD_H:2 * SSD_H] if rev else tail[:, 0:SSD_H]
    dt = jax.nn.softplus(raw + dtb)
    acum = _running_sum(dt * (-jnp.exp(alog)), rev)
    a_last = acum[0:1] if rev else acum[L - 1:L]
    wst = dt * jnp.exp(a_last - acum)
    eac = jnp.exp(acum)
    dec = jnp.exp(a_last)
    tr = jnp.concatenate([acum, dt, wst, jnp.zeros((L, L - 3 * SSD_H), F32)], axis=1).T
    acum_t, dt_t, wst_t = tr[0:SSD_H], tr[SSD_H:2 * SSD_H], tr[2 * SSD_H:3 * SSD_H]
    lane = lax.broadcasted_iota(jnp.int32, (1, 2 * SSD_P), 1)
    m0 = (lane < SSD_P).astype(F32)
    m1 = 1.0 - m0
    pairs_per_group = SSD_H // SSD_G // 2
    y_parts, s_parts = [], []
    for g in range(SSD_G):
        ns = slice(g * SSD_N, (g + 1) * SSD_N)
        bg, cg = bm[:, ns], cm[:, ns]
        cb = _dot_nt(cg, bg)
        bgt = bg.T
        gs = slice(g * pairs_per_group * 2 * SSD_P, (g + 1) * pairs_per_group * 2 * SSD_P)
        y_carry = _dot(cg, S[:, gs])
        for jj in range(pairs_per_group):
            j = g * pairs_per_group + jj
            ls = slice(j * 2 * SSD_P, (j + 1) * 2 * SSD_P)
            xp, sp = x[:, ls], S[:, ls]
            xm = jnp.concatenate([xp * m0, xp * m1], axis=0)
            lhs, bw = [], []
            for h in (2 * j, 2 * j + 1):
                seg = acum[:, h:h + 1] - acum_t[h:h + 1, :]
                lhs.append(cb * jnp.exp(jnp.where(msk, seg, -jnp.inf)) * dt_t[h:h + 1, :])
                bw.append(bgt * wst_t[h:h + 1, :])
            e_pair = eac[:, 2 * j:2 * j + 1] * m0 + eac[:, 2 * j + 1:2 * j + 2] * m1
            y_parts.append(_dot(jnp.concatenate(lhs, axis=1), xm) + y_carry[:, jj * 2 * SSD_P:(jj + 1) * 2 * SSD_P] * e_pair)
            d_pair = dec[:, 2 * j:2 * j + 1] * m0 + dec[:, 2 * j + 1:2 * j + 2] * m1
            s_parts.append(sp * d_pair + _dot(jnp.concatenate(bw, axis=1), xm))
    return jnp.concatenate(s_parts, axis=1), jnp.concatenate(y_parts, axis=1)


def _multi_chunk(chunk_fn, L, subs, nr):
    def fn(S, *args, rev):
        rows, params = args[:nr], args[nr:]
        ys = [None] * subs
        for j in (range(subs - 1, -1, -1) if rev else range(subs)):
            S, ys[j] = chunk_fn(S, *[r[j * L:(j + 1) * L] for r in rows], *params, rev=rev)
        return S, jnp.concatenate(ys, axis=0)

    return fn


def _scan_order(n, nx, rev, backward):
    nc = n - nx

    def fwd(s):
        return (n - 1 - s) if rev else jnp.where(s < nc, s + nx, s - nc)

    return (lambda s: fwd(n - 1 - s)) if backward else fwd


def _scan_fwd(name, chunk_fn, L, n, nx, rev, rows, params, state_shape, out_w, addend=None):
    order = _scan_order(n, nx, rev, False)
    nr, npar = len(rows), len(params)
    adds = [] if addend is None else [addend]

    def body(*refs):
        s_scr = refs[-1]

        @pl.when(pl.program_id(0) == 0)
        def _():
            s_scr[...] = jnp.zeros_like(s_scr)

        s_in = s_scr[...]
        y_ref, st_ref = refs[nr + npar + len(adds)], refs[nr + npar + len(adds) + 1]
        st_ref[0] = s_in
        s_new, y = chunk_fn(s_in, *[r[...] for r in refs[:nr]], *[p[...] for p in refs[nr:nr + npar]], rev=rev)
        y_ref[...] = y + refs[nr + npar][...] if adds else y
        s_scr[...] = s_new

    return pl.pallas_call(
        body, name=name, grid=(n,),
        in_specs=[pl.BlockSpec((L, w), lambda s, c=c: (order(s), c)) for _, w, c in rows] + [_full_spec(p) for p in params]
        + [pl.BlockSpec((L, out_w), lambda s: (order(s), 0)) for _ in adds],
        out_specs=[pl.BlockSpec((L, out_w), lambda s: (order(s), 0)),
                   pl.BlockSpec((1,) + state_shape, lambda s: (order(s), 0, 0))],
        out_shape=[jax.ShapeDtypeStruct((n * L, out_w), F32), jax.ShapeDtypeStruct((n,) + state_shape, F32)],
        scratch_shapes=[pltpu.VMEM(state_shape, F32)],
        compiler_params=_cp(("arbitrary",)),
    )(*[a for a, _, _ in rows], *params, *adds)


def _scan_bwd(name, chunk_fn, L, n, nx, rev, rows, params, states, dy, state_shape, out_w, addends=None):
    order = _scan_order(n, nx, rev, True)
    dy_blocks = dy.shape[0] // L
    nr, npar = len(rows), len(params)
    adds = [] if addends is None else list(addends)

    def body(*refs):
        i = pl.program_id(0)
        ds_scr = refs[-1]
        rv = [r[...] for r in refs[:nr]]
        pv = [p[...] for p in refs[nr:nr + npar]]
        st_ref, dy_ref = refs[nr + npar], refs[nr + npar + 1]
        a_refs = refs[nr + npar + 2:nr + npar + 2 + len(adds)]
        o_refs = refs[nr + npar + 2 + len(adds):-1]
        p_refs = o_refs[nr:]

        @pl.when(i == 0)
        def _():
            ds_scr[...] = jnp.zeros_like(ds_scr)
            for p_ref in p_refs:
                p_ref[...] = jnp.zeros_like(p_ref)

        _, vjp = jax.vjp(functools.partial(chunk_fn, rev=rev), st_ref[0], *rv, *pv)
        dy_blk = jnp.where(order(i) < dy_blocks, dy_ref[...].astype(F32), 0.0)
        grads = vjp((ds_scr[...], dy_blk))
        ds_scr[...] = grads[0]
        for j, (o_ref, g) in enumerate(zip(o_refs[:nr], grads[1:1 + nr])):
            o_ref[...] = g + a_refs[j][...] if adds else g
        for p_ref, g in zip(p_refs, grads[1 + nr:]):
            p_ref[...] += g

    row_specs = [pl.BlockSpec((L, w), lambda s: (order(s), 0)) for _, w, _ in rows]
    res = pl.pallas_call(
        body, name=name, grid=(n,),
        in_specs=[pl.BlockSpec((L, w), lambda s, c=c: (order(s), c)) for _, w, c in rows] + [_full_spec(p) for p in params]
        + [pl.BlockSpec((1,) + state_shape, lambda s: (order(s), 0, 0)),
           pl.BlockSpec((L, out_w), lambda s: (jnp.minimum(order(s), dy_blocks - 1), 0))]
        + row_specs[:len(adds)],
        out_specs=row_specs + [_full_spec(p) for p in params],
        out_shape=[jax.ShapeDtypeStruct((n * L, w), F32) for _, w, _ in rows] + [jax.ShapeDtypeStruct(p.shape, F32) for p in params],
        scratch_shapes=[pltpu.VMEM(state_shape, F32)],
        compiler_params=_cp(("arbitrary",)),
    )(*[a for a, _, _ in rows], *params, states, dy, *adds)
    return res[:nr], res[nr:]


CONV_W = 1024
CONV_COLBLK = (0, 1, 4)


def _conv_specs(nb, src_blk):
    halo = TM // 8
    return [pl.BlockSpec((TM, CONV_W), lambda j, i: (i, src_blk(j))),
            pl.BlockSpec((8, CONV_W), lambda j, i: (jnp.maximum(i * halo - 1, 0), src_blk(j))),
            pl.BlockSpec((8, CONV_W), lambda j, i: (jnp.minimum(i * halo + halo, nb * halo - 1), src_blk(j)))]


def _conv_ext(i, nb, cur, prev, nxt):
    has_prev = jnp.logical_and(i > 0, i < nb - 1)
    has_next = i < nb - 2
    return jnp.concatenate([jnp.where(has_prev, prev, 0.0), cur, jnp.where(has_next, nxt, 0.0)], axis=0)


def _conv_taps(ext, w, flip):
    acc = None
    for j in range(SSD_K):
        wj = w[SSD_K - 1 - j:SSD_K - j, :] if flip else w[j:j + 1, :]
        win = ext if j == 2 else pltpu.roll(ext, (2 - j) % ext.shape[0], axis=0)
        term = wj * win[8:8 + TM, :]
        acc = term if acc is None else acc + term
    return acc


def _conv(name, src, w8, b1, nb, *, permuted_src, act, flip, out_dtype):
    src_blk = (lambda j: jnp.where(j == 2, CONV_COLBLK[2], j)) if permuted_src else (lambda j: j)

    def body(cur, prev, nxt, w_ref, b_ref, o_ref):
        ext = _conv_ext(pl.program_id(1), nb, cur[...].astype(F32), prev[...].astype(F32), nxt[...].astype(F32))
        acc = _conv_taps(ext, w_ref[...], flip)
        if act:
            acc = jax.nn.silu(acc + b_ref[...])
        o_ref[...] = acc.astype(o_ref.dtype)

    return pl.pallas_call(
        body, name=name, grid=(3, nb),
        in_specs=_conv_specs(nb, src_blk) + [pl.BlockSpec((8, CONV_W), lambda j, i: (0, j)), pl.BlockSpec((1, CONV_W), lambda j, i: (0, j))],
        out_specs=pl.BlockSpec((TM, CONV_W), lambda j, i: (i, j)),
        out_shape=jax.ShapeDtypeStruct((nb * TM, 3 * CONV_W), out_dtype),
        compiler_params=_cp(("parallel", "parallel")),
    )(src, src, src, w8, b1)


def _conv_bwd_pre(name, p1, w8, b1, dxbc_parts, nb):
    src_blk = lambda j: jnp.where(j == 2, CONV_COLBLK[2], j)
    xs_parts, bc_parts = dxbc_parts
    n_x, n_bc = len(xs_parts), len(bc_parts)
    x_blocks = [p.shape[0] // TM for p in xs_parts]

    def body(*refs):
        cur, prev, nxt, w_ref, b_ref = refs[:5]
        d_refs = refs[5:5 + n_x + n_bc]
        da_ref, dw_ref, db_ref = refs[5 + n_x + n_bc:]
        j, i = pl.program_id(0), pl.program_id(1)
        ext = _conv_ext(i, nb, cur[...], prev[...], nxt[...])
        acc = _conv_taps(ext, w_ref[...], False) + b_ref[...]
        dx = d_refs[0][...]
        for r, blocks in zip(d_refs[1:n_x], x_blocks[1:]):
            dx = dx + jnp.where(i < blocks, r[...], 0.0)
        dbc = jnp.concatenate([d_refs[n_x][...], d_refs[n_x + 1][...]], axis=1)
        dy = jnp.where(j == 2, dbc, dx)
        sg = jax.nn.sigmoid(acc)
        da = dy * (sg + acc * sg * (1.0 - sg))
        da_ref[...] = da

        @pl.when(i == 0)
        def _():
            dw_ref[...] = jnp.zeros_like(dw_ref)
            db_ref[...] = jnp.zeros_like(db_ref)

        rows = [jnp.sum(da * (ext if t == 2 else pltpu.roll(ext, (2 - t) % ext.shape[0], axis=0))[8:8 + TM, :], axis=0, keepdims=True)
                for t in range(SSD_K)]
        dw_ref[...] += jnp.concatenate(rows + [jnp.zeros((8 - SSD_K, CONV_W), F32)], axis=0)
        db_ref[...] += jnp.sum(da, axis=0, keepdims=True)

    x_specs = [pl.BlockSpec((TM, CONV_W), lambda j, i, b=b: (jnp.minimum(i, b - 1), jnp.minimum(j, 1))) for b in x_blocks]
    bc_specs = [pl.BlockSpec((TM, 512), lambda j, i: (i, 0)) for _ in bc_parts]
    return pl.pallas_call(
        body, name=name, grid=(3, nb),
        in_specs=_conv_specs(nb, src_blk) + [pl.BlockSpec((8, CONV_W), lambda j, i: (0, j)), pl.BlockSpec((1, CONV_W), lambda j, i: (0, j))]
        + x_specs + bc_specs,
        out_specs=[pl.BlockSpec((TM, CONV_W), lambda j, i: (i, j)), pl.BlockSpec((8, CONV_W), lambda j, i: (0, j)),
                   pl.BlockSpec((1, CONV_W), lambda j, i: (0, j))],
        out_shape=[jax.ShapeDtypeStruct((nb * TM, 3 * CONV_W), F32), jax.ShapeDtypeStruct((8, 3 * CONV_W), F32),
                   jax.ShapeDtypeStruct((1, 3 * CONV_W), F32)],
        compiler_params=_cp(("arbitrary", "arbitrary")),
    )(p1, p1, p1, w8, b1, *xs_parts, *bc_parts)


def _grid_block(a):
    nbv = TM // a
    blk_b = max(nbv, 8)
    return nbv, blk_b, blk_b // nbv


def _grid_spec(a, nb):
    _, blk_b, per = _grid_block(a)
    return pl.BlockSpec((a, blk_b, D), lambda i: (0, jnp.minimum(i, nb - 2) // per, 0))


def _grid_rows(v_ref, a, i):
    nbv, _, per = _grid_block(a)

    def pick(ph):
        return jnp.concatenate([v_ref[:, ph * nbv + t, :] for t in range(nbv)], axis=0)

    out = pick(0)
    for ph in range(1, per):
        out = jnp.where(i % per == ph, pick(ph), out)
    return out


def _loss_head(x, f, target, modp, g_final, rows_r):
    tview = target.reshape(rows_r, target.shape[0] // rows_r, D)
    nb = x.shape[0] // TM + 1

    def fn(x_, f_, tgt, modp_, g_):
        xn = x_ + _sel_mod(modp_, 0.0)[5:6] * f_
        err = _rms(xn) * g_ - tgt
        return 0.5 * jnp.sum(jnp.mean(err * err, axis=-1))

    def body(x_ref, f_ref, t_ref, m_ref, g_ref, l_ref, dx_ref, df_ref, dm_ref, dg_ref):
        i = pl.program_id(0)
        tgt = _grid_rows(t_ref, rows_r, i)
        l, vjp = jax.vjp(lambda a_, b_, c_, d_: fn(a_, b_, tgt, c_, d_), x_ref[...], f_ref[...], m_ref[...], g_ref[...])
        dx, df, dm, dg = vjp(jnp.ones((), F32))

        @pl.when(i == 0)
        def _():
            l_ref[...] = jnp.zeros_like(l_ref)
            dm_ref[...] = jnp.zeros_like(dm_ref)
            dg_ref[...] = jnp.zeros_like(dg_ref)

        l_ref[...] += jnp.reshape(l, (1, 1))
        dx_ref[...] = dx
        df_ref[...] = df.astype(df_ref.dtype)
        dm_ref[...] += dm
        dg_ref[...] += dg

    rowspec = pl.BlockSpec((TM, D), lambda i: (i, 0))
    return pl.pallas_call(
        body, name="loss_head", grid=(nb - 1,),
        in_specs=[rowspec, rowspec, _grid_spec(rows_r, nb), _full_spec(modp), _full_spec(g_final)],
        out_specs=[pl.BlockSpec((1, 1), lambda i: (0, 0)), rowspec, rowspec, _full_spec(modp), _full_spec(g_final)],
        out_shape=[jax.ShapeDtypeStruct((1, 1), F32), jax.ShapeDtypeStruct(x.shape, F32), jax.ShapeDtypeStruct(x.shape, MXU_DTYPE),
                   jax.ShapeDtypeStruct(modp.shape, F32), jax.ShapeDtypeStruct(g_final.shape, F32)],
        compiler_params=_cp(("arbitrary",)),
    )(x, f, tview, modp, g_final)


def _repack(name, shards, segs, wp):
    nd, kk, ws = shards.shape
    tr = 128
    used = sum(e - s for s, e in segs)

    def body(a_ref, o_ref):
        full = jnp.concatenate([a_ref[d].astype(F32) for d in range(nd)], axis=1)
        parts = [full[:, s:e] for s, e in segs]
        if wp > used:
            parts.append(jnp.zeros((tr, wp - used), F32))
        o_ref[...] = jnp.concatenate(parts, axis=1).astype(o_ref.dtype)

    return pl.pallas_call(
        body, name=name, grid=(kk // tr,),
        in_specs=[pl.BlockSpec((nd, tr, ws), lambda i: (0, i, 0))],
        out_specs=pl.BlockSpec((tr, wp), lambda i: (i, 0)),
        out_shape=jax.ShapeDtypeStruct((kk, wp), MXU_DTYPE),
        compiler_params=_cp(("parallel",)),
    )(shards)


def _unpack(name, dw, segs, ws, out_dtype):
    kk, wp = dw.shape
    tr = 128
    order = sorted(range(len(segs)), key=lambda i: segs[i][0])
    offs, o = [], 0
    for s, e in segs:
        offs.append(o)
        o += e - s

    def body(a_ref, o_ref):
        a = a_ref[...].astype(F32)
        full = jnp.concatenate([a[:, offs[i]:offs[i] + segs[i][1] - segs[i][0]] for i in order], axis=1)
        for d in range(NDEV):
            o_ref[d] = full[:, d * ws:(d + 1) * ws].astype(o_ref.dtype)

    return pl.pallas_call(
        body, name=name, grid=(kk // tr,),
        in_specs=[pl.BlockSpec((tr, wp), lambda i: (i, 0))],
        out_specs=pl.BlockSpec((NDEV, tr, ws), lambda i: (0, i, 0)),
        out_shape=jax.ShapeDtypeStruct((NDEV, kk, ws), out_dtype),
        compiler_params=_cp(("parallel",)),
    )(dw)


def _adam_math(w, g, m, v):
    m = ADAM_B1 * m + (1.0 - ADAM_B1) * g
    v = ADAM_B2 * v + (1.0 - ADAM_B2) * jnp.square(g)
    m_hat = m / (1.0 - ADAM_B1 ** ADAM_STEP)
    v_hat = v / (1.0 - ADAM_B2 ** ADAM_STEP)
    delta = -ADAM_LR * (m_hat / (jnp.sqrt(v_hat) + ADAM_EPS) + ADAM_WD * w)
    return delta, m, v


def _adam(name, w, parts, m, v, after):
    r, c = w.shape
    nsec, npart = len(parts), parts[0].shape[0]
    rs = r // nsec
    tr = _pick(rs, (256, 128, 64, 32, 16, 8)) if rs * c * 4 > (1 << 20) else rs
    tiles = rs // tr

    def body(w_ref, *refs):
        m_ref, v_ref, _, g_ref, d_ref, nm_ref, nv_ref = refs[nsec:]
        sec = pl.program_id(0) // tiles
        for a, p_ref in enumerate(refs[:nsec]):
            @pl.when(sec == a)
            def _(p_ref=p_ref):
                g = p_ref[0].astype(F32)
                for s in range(1, npart):
                    g = g + p_ref[s].astype(F32)
                delta, nm, nv = _adam_math(w_ref[...], g, m_ref[...], v_ref[...])
                g_ref[...], d_ref[...], nm_ref[...], nv_ref[...] = g, delta, nm, nv

    spec = pl.BlockSpec((tr, c), lambda i: (i, 0))
    part_specs = [pl.BlockSpec((npart, tr, c), lambda i, a=a: (0, jnp.clip(i - a * tiles, 0, tiles - 1), 0)) for a in range(nsec)]
    return pl.pallas_call(
        body, name=name, grid=(r // tr,),
        in_specs=[spec] + part_specs + [spec, spec, ANY],
        out_specs=[spec] * 4, out_shape=[jax.ShapeDtypeStruct((r, c), F32)] * 4,
        compiler_params=_cp(("parallel",)),
    )(w, *parts, m, v, after)


def _mod_fwd(c_all, mod_w):
    nl, _, ws = mod_w.shape

    def body(c_ref, w_ref, o_ref):
        o_ref[0] = _dot(jax.nn.silu(c_ref[...]), w_ref[0])

    return pl.pallas_call(
        body, name="mod_fwd", grid=(nl,),
        in_specs=[_full_spec(c_all), pl.BlockSpec((1, D, ws), lambda i: (i, 0, 0))],
        out_specs=pl.BlockSpec((1, 16, ws), lambda i: (i, 0, 0)),
        out_shape=jax.ShapeDtypeStruct((nl, 16, ws), F32),
        compiler_params=_cp(("parallel",)),
    )(c_all, mod_w)


def _mod_bwd(c_all, mod_w, dm):
    nl, _, ws = mod_w.shape

    def body(c_ref, w_ref, d_ref, dw_ref, dc_ref):
        dw_ref[0] = _dot_tn(jax.nn.silu(c_ref[...]), d_ref[0])
        dc_ref[0] = _dot_nt(d_ref[0], w_ref[0])

    return pl.pallas_call(
        body, name="mod_bwd", grid=(nl,),
        in_specs=[_full_spec(c_all), pl.BlockSpec((1, D, ws), lambda i: (i, 0, 0)), pl.BlockSpec((1, 16, ws), lambda i: (i, 0, 0))],
        out_specs=[pl.BlockSpec((1, D, ws), lambda i: (i, 0, 0)), pl.BlockSpec((1, 16, D), lambda i: (i, 0, 0))],
        out_shape=[jax.ShapeDtypeStruct((nl, D, ws), F32), jax.ShapeDtypeStruct((nl, 16, D), F32)],
        compiler_params=_cp(("parallel",)),
    )(c_all, mod_w, dm)


def _sum_parts(name, parts):
    npart, r, c = parts.shape

    def body(p_ref, o_ref):
        g = p_ref[0].astype(F32)
        for s in range(1, npart):
            g = g + p_ref[s].astype(F32)
        o_ref[...] = g

    return pl.pallas_call(body, name=name, out_shape=jax.ShapeDtypeStruct((r, c), F32), compiler_params=_cp())(parts)


MESH = pl.DeviceIdType.MESH
ANY = pl.BlockSpec(memory_space=pl.ANY)
N_PEERS = NDEV - 1


def _mesh_pos():
    return lax.axis_index("x"), lax.axis_index("y"), lax.axis_index("c")


def _slot(px, py, pc):
    return 4 * px + 2 * py + pc


def _two_level_gather(x_refs, o_refs, send_sems, recv_sems, local_sems):
    x, y, c = _mesh_pos()
    me, sibling = (x, y, c), (x, y, 1 - c)
    chips = [(1 - x, y), (x, 1 - y), (1 - x, 1 - y)]
    n = len(x_refs)

    def copy(a, k, block, to, src=None):
        dst = o_refs[a].at[_slot(*block)]
        return pltpu.make_async_remote_copy(src_ref=dst if src is None else src, dst_ref=dst, send_sem=send_sems.at[a, k],
                                            recv_sem=recv_sems.at[a, k], device_id=to, device_id_type=MESH)

    mine = [pltpu.make_async_copy(x_refs[a], o_refs[a].at[_slot(*me)], local_sems.at[a]) for a in range(n)]
    for cp in mine:
        cp.start()
    first = []
    for a in range(n):
        first.append(copy(a, 0, me, sibling, src=x_refs[a]))
        first += [copy(a, 1 + j, me, (*chip, c), src=x_refs[a]) for j, chip in enumerate(chips)]
    for cp in first:
        cp.start()
    passed = []
    for j, chip in enumerate(chips):
        for a in range(n):
            copy(a, 1 + j, (*chip, c), me).wait_recv()
            fwd = copy(a, 4 + j, (*chip, c), sibling)
            fwd.start()
            passed.append(fwd)
    for a in range(n):
        copy(a, 0, sibling, me).wait_recv()
        for j, chip in enumerate(chips):
            copy(a, 4 + j, (*chip, 1 - c), me).wait_recv()
    for cp in first + passed:
        cp.wait_send()
    for cp in mine:
        cp.wait()


def _ag_small(name, x):
    r, c = x.shape

    def body(x_ref, o_ref, send_sems, recv_sems, local_sems):
        _two_level_gather([x_ref], [o_ref], send_sems, recv_sems, local_sems)

    return pl.pallas_call(
        body, name=name, out_shape=jax.ShapeDtypeStruct((NDEV, r, c), x.dtype),
        in_specs=[pl.BlockSpec(memory_space=pltpu.VMEM)], out_specs=pl.BlockSpec(memory_space=pltpu.VMEM),
        scratch_shapes=[pltpu.SemaphoreType.DMA((1, N_PEERS)), pltpu.SemaphoreType.DMA((1, N_PEERS)), pltpu.SemaphoreType.DMA((1,))],
        compiler_params=pltpu.CompilerParams(vmem_limit_bytes=VMEM_LIMIT),
    )(x)


def _ag_big(name, shards):
    n = len(shards)

    def body(*refs):
        _two_level_gather(refs[:n], refs[n:2 * n], *refs[2 * n:])

    return pl.pallas_call(
        body, name=name, out_shape=[jax.ShapeDtypeStruct((NDEV,) + s.shape, s.dtype) for s in shards],
        in_specs=[ANY] * n, out_specs=[ANY] * n,
        scratch_shapes=[pltpu.SemaphoreType.DMA((n, N_PEERS)), pltpu.SemaphoreType.DMA((n, N_PEERS)), pltpu.SemaphoreType.DMA((n,))],
    )(*shards)


HBM = pl.BlockSpec(memory_space=pltpu.HBM)
SEM = pl.BlockSpec(memory_space=pltpu.SEMAPHORE)
EFFECT = pltpu.SideEffectType.DATAFLOW_SIDE_EFFECTING


def _peers(x, y, c):
    return [(k - 1, ((1 - x) if k & 4 else x, (1 - y) if k & 2 else y, (1 - c) if k & 1 else c)) for k in range(1, NDEV)]


def _xchg_copy(src_refs, land_refs, send_sems, recv_sems, a, k, peer, me, scatter):
    src = src_refs[a].at[_slot(*peer)] if scatter else src_refs[a]
    return pltpu.make_async_remote_copy(src_ref=src, dst_ref=land_refs[a].at[me], send_sem=send_sems.at[a * N_PEERS + k],
                                        recv_sem=recv_sems.at[a * N_PEERS + k], device_id=peer, device_id_type=MESH)


def _xchg_start(name, srcs, lands, deps, scatter):
    n, nd = len(srcs), len(deps)

    def body(*refs):
        src_refs, land_refs = refs[:n], refs[n:2 * n]
        send_sems, recv_sems, token = refs[2 * n + nd], refs[2 * n + nd + 1], refs[-1]
        x, y, c = _mesh_pos()
        me = _slot(x, y, c)
        for k, peer in _peers(x, y, c):
            for a in range(n):
                _xchg_copy(src_refs, land_refs, send_sems, recv_sems, a, k, peer, me, scatter).start()
        token[...] = jnp.zeros_like(token)

    res = pl.pallas_call(
        body, name=name,
        out_shape=(pltpu.SemaphoreType.DMA((n * N_PEERS,)), pltpu.SemaphoreType.DMA((n * N_PEERS,)),
                   *[pltpu.HBM(s.shape, s.dtype) for s in srcs], *[pltpu.HBM(s.shape, s.dtype) for s in lands],
                   jax.ShapeDtypeStruct((8, 128), F32)),
        in_specs=[HBM] * (2 * n) + [ANY] * nd,
        out_specs=(SEM, SEM, *([HBM] * (2 * n)), pl.BlockSpec(memory_space=pltpu.VMEM)),
        input_output_aliases={i: 2 + i for i in range(2 * n)},
        compiler_params=pltpu.CompilerParams(has_side_effects=EFFECT),
    )(*[pltpu.with_memory_space_constraint(s, pltpu.HBM) for s in srcs],
      *[pltpu.with_memory_space_constraint(s, pltpu.HBM) for s in lands], *deps)
    return res[0], res[1], res[2:2 + n], res[2 + n:2 + 2 * n], res[-1]


def _xchg_wait(name, send_sems, recv_sems, srcs, lands, after, scatter):
    n = len(srcs)

    def body(*refs):
        src_refs, land_refs = refs[:n], refs[n:2 * n]
        s_sems, r_sems = refs[2 * n], refs[2 * n + 1]
        x, y, c = _mesh_pos()
        me = _slot(x, y, c)
        for k, peer in _peers(x, y, c):
            for a in range(n):
                cp = _xchg_copy(src_refs, land_refs, s_sems, r_sems, a, k, peer, me, scatter)
                cp.wait_send()
                cp.wait_recv()

    res = pl.pallas_call(
        body, name=name,
        out_shape=[pltpu.HBM(s.shape, s.dtype) for s in srcs] + [pltpu.HBM(s.shape, s.dtype) for s in lands],
        in_specs=[HBM] * (2 * n) + [SEM, SEM, ANY], out_specs=[HBM] * (2 * n),
        input_output_aliases={i: i for i in range(2 * n)},
        compiler_params=pltpu.CompilerParams(has_side_effects=EFFECT),
    )(*srcs, *lands, send_sems, recv_sems, after)
    return res[n:]


def _landing(name, srcs, me, scatter):
    shapes = [s.shape[-2:] for s in srcs]

    def body(me_ref, *refs):
        for s_ref, o_ref in zip(refs[:len(srcs)], refs[len(srcs):]):
            o_ref[...] = s_ref[...].reshape(o_ref.shape)

    def slot_spec(r, c):
        return pl.BlockSpec((1, r, c), lambda i, me_ref: (me_ref[0], 0, 0))

    return pl.pallas_call(
        body, name=name, out_shape=[jax.ShapeDtypeStruct((NDEV, r, c), s.dtype) for s, (r, c) in zip(srcs, shapes)],
        grid_spec=pltpu.PrefetchScalarGridSpec(
            num_scalar_prefetch=1, grid=(1,),
            in_specs=[slot_spec(r, c) if scatter else pl.BlockSpec((r, c), lambda i, me_ref: (0, 0)) for r, c in shapes],
            out_specs=[slot_spec(r, c) for r, c in shapes]),
        compiler_params=_cp(("arbitrary",)),
    )(jnp.reshape(me, (1,)).astype(jnp.int32), *srcs)


STAGES = ("l0_mixer", "l0_ffn", "l1_mixer", "l1_ffn")
STAGE_LAYOUT = {"l0_mixer": (AB_SEGS, AB_P), "l1_mixer": (SSD_SEGS, SSD_P_W)}


class _Exchange:
    def __init__(self, shards, me):
        self.shards, self.me = shards, me
        self.pending, self.pending_grads, self.recv = {}, None, {}

    def _layout(self, stage):
        ws = self.shards[stage][0].shape[-1]
        return STAGE_LAYOUT.get(stage, (((0, NDEV * ws),), NDEV * ws)) + (ws,)

    def _start_gather(self, stage, deps):
        srcs = list(self.shards[stage])
        lands = _landing("own_" + stage, srcs, self.me, False)
        return _xchg_start("gather_start_" + stage, srcs, lands, deps, False)

    def get(self, stage, dep, thread):
        i = STAGES.index(stage)
        if i == 0:
            g_in, g_out = _ag_big("gather_" + stage, list(self.shards[stage]))
            ahead, deps = STAGES[1:3], [g_out, dep]
        else:
            ss, rs, srcs, lands, _ = self.pending.pop(stage)
            g_in, g_out = _xchg_wait("gather_wait_" + stage, ss, rs, srcs, lands, dep, False)
            ahead, deps = STAGES[i + 2:i + 3], [g_out]
        for nxt in ahead:
            self.pending[nxt] = self._start_gather(nxt, deps)
            deps = [self.pending[nxt][4]]
            thread = thread + self.pending[nxt][4][0, 0]
        segs, wp, _ = self._layout(stage)
        return _repack("repack_" + stage, g_in, segs, wp), g_out.reshape(-1, D), thread

    def put(self, stage, d_in, d_out, thread):
        segs, _, ws = self._layout(stage)
        parts = [_unpack("unpack_" + stage, d_in, segs, ws, MXU_DTYPE), d_out.reshape(NDEV, -1, D)]
        deps = [parts[0]]
        if self.pending_grads is not None:
            deps = [self.finish(parts[0])[0]]
        self.staged = (stage, parts)
        return thread if stage == STAGES[0] else thread + self.start_last(deps)[0, 0]

    def start_last(self, deps):
        stage, parts = self.staged
        lands = _landing("own_grad_" + stage, parts, self.me, True)
        self.pending_grads = (stage,) + _xchg_start("scatter_start_" + stage, parts, lands, deps, True)
        return self.pending_grads[5]

    def finish(self, after):
        stage, ss, rs, srcs, lands, _ = self.pending_grads
        self.recv[stage] = _xchg_wait("scatter_wait_" + stage, ss, rs, srcs, lands, after, True)
        self.pending_grads = None
        return self.recv[stage]


def _mm_swiglu(name, h, w_in):
    m, kk = h.shape
    f = w_in.shape[1] // 2
    tm = _pick(m, (272, 256, 128))

    def body(h_ref, wg_ref, wu_ref, pf_ref, act_ref):
        a = h_ref[...].astype(MXU_DTYPE)
        g = jnp.dot(a, wg_ref[...].astype(MXU_DTYPE), preferred_element_type=F32).astype(MXU_DTYPE)
        u = jnp.dot(a, wu_ref[...].astype(MXU_DTYPE), preferred_element_type=F32).astype(MXU_DTYPE)
        pf_ref[0] = g
        pf_ref[1] = u
        act_ref[...] = (jax.nn.silu(g.astype(F32)) * u.astype(F32)).astype(act_ref.dtype)

    return pl.pallas_call(
        body, name=name, grid=(m // tm,),
        in_specs=[pl.BlockSpec((tm, kk), lambda i: (i, 0)), pl.BlockSpec((kk, f), lambda i: (0, 0)), pl.BlockSpec((kk, f), lambda i: (0, 1))],
        out_specs=[pl.BlockSpec((2, tm, f), lambda i: (0, i, 0)), pl.BlockSpec((tm, f), lambda i: (i, 0))],
        out_shape=[jax.ShapeDtypeStruct((2, m, f), MXU_DTYPE), jax.ShapeDtypeStruct((m, f), MXU_DTYPE)],
        compiler_params=_cp(("parallel",)),
    )(h, w_in, w_in)


def _ffn_fwd(tag, h, w_in, w_out, nb, cb):
    pf, act = _mm_swiglu(tag + "_ffn_in", h, w_in)
    return pf, act, _mm(tag + "_ffn_out", act, w_out, "nn", F32)


def _mm_swiglu_bwd(name, df, w_out, pf):
    m, kk = df.shape
    f = w_out.shape[0]
    tm = _pick(m, (272, 256, 128))

    def body(d_ref, w_ref, pf_ref, o_ref):
        dact = lax.dot_general(d_ref[...].astype(MXU_DTYPE), w_ref[...].astype(MXU_DTYPE), (((1,), (1,)), ((), ())),
                               preferred_element_type=F32)
        g, u = pf_ref[0].astype(F32), pf_ref[1].astype(F32)
        sg = jax.nn.sigmoid(g)
        o_ref[:, 0:f] = (dact * u * (sg * (1.0 + g * (1.0 - sg)))).astype(o_ref.dtype)
        o_ref[:, f:2 * f] = (dact * (g * sg)).astype(o_ref.dtype)

    return pl.pallas_call(
        body, name=name, grid=(m // tm,),
        in_specs=[pl.BlockSpec((tm, kk), lambda i: (i, 0)), pl.BlockSpec((f, kk), lambda i: (0, 0)), pl.BlockSpec((2, tm, f), lambda i: (0, i, 0))],
        out_specs=pl.BlockSpec((tm, 2 * f), lambda i: (i, 0)),
        out_shape=jax.ShapeDtypeStruct((m, 2 * f), MXU_DTYPE),
        compiler_params=_cp(("parallel",)),
    )(df, w_out, pf)


def _ffn_bwd(tag, h, pf, act, df, w_in, w_out, nb, cb):
    dw_out = _mm(tag + "_ffn_out_dw", act, df, "tn", MXU_DTYPE)
    dpf = _mm_swiglu_bwd(tag + "_ffn_out_dx", df, w_out, pf)
    dw_in = _mm(tag + "_ffn_in_dw", h, dpf, "tn", MXU_DTYPE)
    return dw_out, dw_in, dpf


def _local_step(x, ctx, target, mod, P, comm):
    T = x.shape[0]
    N = T + CTX
    nb, cb = N // TM, N // TM - 1
    R = T // GRID_W
    mod0, mod1 = mod[0], mod[1]
    ng = P["norm_g"]
    g00, g01, g10, g11 = ng[0, 0][None], ng[0, 1][None], ng[1, 0][None], ng[1, 1][None]
    pre = functools.partial(_fn_prenorm, a=0, b=1)
    rpre = functools.partial(_fn_resid_prenorm, gi=2, a=3, b=4)
    res5 = functools.partial(_fn_resid, gi=5)
    dirs = (("f", False), ("b", True))

    xc0 = _row_cat(x, ctx, nb)
    w_ab_in, w_ab_out, g00 = comm.get("l0_mixer", mod, g00)
    h0, p0 = _rowwise("l0_prenorm_in", pre, nb, cb, [xc0], [g00, mod0], [(D, MXU_DTYPE)], post_mm=(0, w_ab_in))
    gla_rows = [(p0, 512, 0), (p0, 256, 8), (p0, 256, 9), (p0, 128, 20)]
    gla_blk = _multi_chunk(_gla_chunk, GLA_L, TM // GLA_L, len(gla_rows))
    gla_par = {d: [P["ab_gate_w"][int(r)], P["ab_gate_b"][int(r)][None]] for d, r in dirs}
    gla_state = (GLA_H * GLA_DV, GLA_H * GLA_DK)
    o, st0 = None, {}
    for d, rev in dirs:
        o, st0[d] = _scan_fwd("gla_fwd_" + d, gla_blk, TM, nb, cb, rev, gla_rows, gla_par[d], gla_state, GLA_H * GLA_DV, o)
    n128, cb128 = N // GMLP_L, T // GMLP_L
    mix_rows = [_row(o, tm=GMLP_L)] + [_row(p0, 512, j, tm=GMLP_L) for j in (1, 2, 3)]
    mix_par = [P["ab_gla_norm_g"], P["ab_vnorm_g"], P["ab_spatial_w"].reshape(GMLP_G * GMLP_L, GMLP_L), P["ab_spatial_b"].T]
    (cat0,) = _rowwise("l0_mix", _fn_mixpost, n128, cb128, mix_rows, mix_par, [(D, MXU_DTYPE)], tm=GMLP_L)
    y0 = _mm("l0_out", cat0, w_ab_out, "nn", F32)
    w_fi0, w_fo0, g01 = comm.get("l0_ffn", y0, g01)
    x1, h1 = _rowwise("l0_ffn_prenorm", rpre, nb, cb, [xc0, _row(y0)], [g01, mod0, mod0], [(D, F32), (D, MXU_DTYPE)])
    pf0, act0, f0 = _ffn_fwd("l0", h1, w_fi0, w_fo0, nb, cb)
    w_ssd_in, w_ssd_out, g10 = comm.get("l1_mixer", f0, g10)
    x2p, h2, p1 = _rowwise("l0_resid_l1_prenorm_in", functools.partial(_fn_resid_prenorm, gi=5, a=0, b=1), nb, cb,
                           [_row_grid(x1, R, nb), _row_grid(f0, R, nb)], [g10, mod0, mod1], [(D, F32), (D, MXU_DTYPE)],
                           post_mm=(1, w_ssd_in))
    conv_w8 = jnp.concatenate([P["ssd_conv_w"], jnp.zeros((8 - SSD_K, 3 * CONV_W), F32)], axis=0)
    xbc = _conv("l1_conv", p1, conv_w8, P["ssd_conv_b"], nb, permuted_src=True, act=True, flip=False, out_dtype=F32)
    ssd_rows = [(xbc, SSD_INNER, 0), (xbc, 512, 4), (xbc, 512, 5), (p1, 128, 40)]
    ssd_blk = _multi_chunk(_ssd_chunk, SSD_L, TM // SSD_L, len(ssd_rows))
    ssd_par = {d: [P["ssd_dt_bias"][int(r)][None], P["ssd_a_log"][int(r)][None]] for d, r in dirs}
    ssd_state = (SSD_N, SSD_INNER)
    ys, st1 = None, {}
    for d, rev in dirs:
        ys, st1[d] = _scan_fwd("ssd_fwd_" + d, ssd_blk, TM, nb, cb, rev, ssd_rows, ssd_par[d], ssd_state, SSD_INNER, ys)
    fin_rows = [_row(ys), _row(xbc, SSD_INNER, 0), _row(p1, SSD_INNER, 1)]
    fin_par = [P["ssd_d"], P["ssd_norm_g"]]
    yn, y1 = _rowwise("l1_finish_out", _fn_ssd_finish, cb, cb, fin_rows, fin_par, [(SSD_INNER, MXU_DTYPE)], post_mm=(0, w_ssd_out))
    w_fi1, w_fo1, g11 = comm.get("l1_ffn", y1, g11)
    x3, h3 = _rowwise("l1_ffn_prenorm", rpre, cb, cb, [_row(x2p), _row(y1)], [g11, mod1, mod1], [(D, F32), (D, MXU_DTYPE)])
    pf1, act1, f1 = _ffn_fwd("l1", h3, w_fi1, w_fo1, cb, cb)
    loss, dx3, df1, dm1_j, d_final_g = _loss_head(x3, f1, target, mod1, P["final_norm_g"], R)

    dP = {"final_norm_g": d_final_g}
    dwo1, dwi1, dpf1 = _ffn_bwd("l1", h3, pf1, act1, df1, w_fi1, w_fo1, cb, cb)
    g11 = comm.put("l1_ffn", dwi1, dwo1, g11)
    (dx2p_a, dy1), (dg11, dm1_a, dm1_b) = _rowwise_vjp(
        "l1_ffn_in_dx_prenorm_bwd", rpre, cb, cb, [_row(x2p), _row(y1)], [g11, mod1, mod1], [_row(dx3), _row(dpf1)],
        [(0, F32, None), (1, MXU_DTYPE, None)], cot_mm=(1, w_fi1))
    d_ssd_out = _mm("l1_out_dw", yn, dy1, "tn", MXU_DTYPE)
    (dys, dxs, dz), (dP["ssd_d"], dP["ssd_norm_g"]) = _rowwise_vjp(
        "l1_out_dx_finish_bwd", _fn_ssd_finish, cb, cb, fin_rows, fin_par, [_row(dy1)],
        [(0, F32, None), (1, F32, None), (2, MXU_DTYPE, None)], cot_mm=(0, w_ssd_out))
    dssd, ddtb, dalog = None, [], []
    for d, rev in dirs:
        dssd, (ddtb_, dalog_) = _scan_bwd("ssd_bwd_" + d, ssd_blk, TM, nb, cb, rev, ssd_rows, ssd_par[d], st1[d], dys, ssd_state,
                                          SSD_INNER, dssd)
        ddtb.append(ddtb_); dalog.append(dalog_)
    dx_s, db_s, dc_s, dtl = dssd
    dP["ssd_dt_bias"] = jnp.concatenate(ddtb, axis=0)
    dP["ssd_a_log"] = jnp.concatenate(dalog, axis=0)
    dacc, dcw8, dP["ssd_conv_b"] = _conv_bwd_pre("l1_conv_bwd", p1, conv_w8, P["ssd_conv_b"], ([dx_s, dxs], [db_s, dc_s]), nb)
    dP["ssd_conv_w"] = dcw8[:SSD_K]
    dpc = _conv("l1_conv_dx", dacc, conv_w8, jnp.zeros((1, 3 * CONV_W), F32), nb, permuted_src=False, act=False, flip=True,
                out_dtype=MXU_DTYPE)
    cat1 = functools.partial(_fn_concat, sums=(1, 1, 1, 1), pad=SSD_P_W - 5248)
    (dp1,) = _rowwise("l1_dp", cat1, nb, cb, [_row(dpc, SSD_INNER, 0), _row(dz, valid=cb), _row(dpc, 1024, 2), _row(dtl)],
                      [], [(SSD_P_W, MXU_DTYPE)])
    g10 = comm.put("l1_mixer", _mm("l1_in_dw", h2, dp1, "tn", F32), d_ssd_out, g10)
    (dx2p,), (dg10, dm1_f) = _rowwise_vjp("l1_in_dx_prenorm_bwd", pre, nb, cb, [_row(x2p)], [g10, mod1], [_row(dp1)],
                                          [(0, F32, _row(dx2p_a, valid=cb))], cot_mm=(0, w_ssd_in))

    (dx1_a, df0), (dm0_e,) = _rowwise_vjp("l0_resid_bwd", res5, nb, cb, [_row(x1), _row(f0)], [mod0],
                                          [_row_grid(dx2p, GRID_W, nb)], [(0, F32, None), (1, MXU_DTYPE, None)])
    dwo0, dwi0, dpf0 = _ffn_bwd("l0", h1, pf0, act0, df0, w_fi0, w_fo0, nb, cb)
    g01 = comm.put("l0_ffn", dwi0, dwo0, g01)
    (dxc0_a, dy0), (dg01, dm0_a, dm0_b) = _rowwise_vjp(
        "l0_ffn_in_dx_prenorm_bwd", rpre, nb, cb, [xc0, _row(y0)], [g01, mod0, mod0], [_row(dx1_a), _row(dpf0)],
        [(0, F32, None), (1, MXU_DTYPE, None)], cot_mm=(1, w_fi0))
    d_ab_out = _mm("l0_out_dw", cat0, dy0, "tn", MXU_DTYPE)
    dcat0 = _mm("l0_out_dx", dy0, w_ab_out, "nt", MXU_DTYPE)
    (do, dr, du, dgm), (dP["ab_gla_norm_g"], dP["ab_vnorm_g"], dsw, dsb_t) = _rowwise_vjp(
        "l0_mix_bwd", _fn_mixpost, n128, cb128, mix_rows, mix_par, [_row(dcat0, tm=GMLP_L)],
        [(0, F32, None), (1, MXU_DTYPE, None), (2, MXU_DTYPE, None), (3, MXU_DTYPE, None)], tm=GMLP_L)
    dP["ab_spatial_w"] = dsw.reshape(GMLP_G, GMLP_L, GMLP_L)
    dP["ab_spatial_b"] = dsb_t.T
    gl, dgw, dgb = None, [], []
    for d, rev in dirs:
        gl, (dgw_, dgb_) = _scan_bwd("gla_bwd_" + d, gla_blk, TM, nb, cb, rev, gla_rows, gla_par[d], st0[d], do,
                                     gla_state, GLA_H * GLA_DV, gl)
        dgw.append(dgw_[None]); dgb.append(dgb_)
    dP["ab_gate_w"] = jnp.concatenate(dgw, axis=0)
    dP["ab_gate_b"] = jnp.concatenate(dgb, axis=0)
    cat0f = functools.partial(_fn_concat, sums=(1,) * 7, pad=AB_P - 2688)
    (dp0,) = _rowwise("l0_dp", cat0f, nb, cb, [_row(gl[0]), _row(dr), _row(du), _row(dgm), _row(gl[1]), _row(gl[2]), _row(gl[3])],
                      [], [(AB_P, MXU_DTYPE)])
    g00 = comm.put("l0_mixer", _mm("l0_in_dw", h0, dp0, "tn", F32), d_ab_out, g00)
    (grad_x,), (dg00, dm0_s) = _rowwise_vjp("l0_in_dx_prenorm_bwd", pre, nb, cb, [xc0], [g00, mod0], [_row(dp0)],
                                            [(0, F32, _row(dxc0_a))], x_rows_only=True, cot_mm=(0, w_ab_in))
    dP["norm_g"] = jnp.concatenate([dg00, dg01, dg10, dg11], axis=0).reshape(2, 2, D)
    dmod = jnp.stack([dm0_s + dm0_a + dm0_b + dm0_e, dm1_f + dm1_a + dm1_b + dm1_j])
    return loss, grad_x, dmod, dP


WEIGHTS = ("c_ctx", "mod_w", "mod_b", "norm_g", "ffn_w_in", "ffn_w_out", "ab_w_in", "ab_gate_w", "ab_gate_b", "ab_gla_norm_g",
           "ab_vnorm_g", "ab_spatial_w", "ab_spatial_b", "ab_w_out", "ssd_w_in", "ssd_conv_w", "ssd_conv_b", "ssd_dt_bias",
           "ssd_a_log", "ssd_d", "ssd_norm_g", "ssd_w_out", "final_norm_g")
SMALL_SHARDED = ("norm_g", "ab_gate_w", "ab_gate_b", "ssd_conv_w", "ssd_conv_b", "ssd_norm_g")
SMALL = ("c_ctx", "mod_b", "norm_g", "ab_gate_w", "ab_gate_b", "ab_gla_norm_g", "ab_vnorm_g", "ab_spatial_w", "ab_spatial_b",
         "ssd_conv_w", "ssd_conv_b", "ssd_dt_bias", "ssd_a_log", "ssd_d", "ssd_norm_g", "final_norm_g")
LANES = 1024


def _pack(arrs, rows_multiple=8):
    flat = jnp.concatenate([a.reshape(-1).astype(F32) for a in arrs])
    rows = -(-flat.shape[0] // LANES)
    rows = -(-rows // rows_multiple) * rows_multiple
    return jnp.pad(flat, (0, rows * LANES - flat.shape[0])).reshape(rows, LANES)


def _unpack_flat(buf, shapes):
    lead = buf.shape[:-2]
    flat = buf.reshape(lead + (-1,))
    out, o = [], 0
    for s in shapes:
        n = math.prod(s)
        out.append(flat[..., o:o + n].reshape(lead + tuple(s)))
        o += n
    return out


def _unshard(g):
    g = jnp.moveaxis(g, 0, -2)
    return g.reshape(g.shape[:-2] + (g.shape[-2] * g.shape[-1],))


def _my_shard(full, me, ws):
    return lax.dynamic_slice_in_dim(full, me * ws, ws, axis=full.ndim - 1)


def _silu_vjp(cvec, dsc):
    def body(c_ref, d_ref, o_ref):
        _, vjp = jax.vjp(jax.nn.silu, c_ref[...])
        o_ref[...] = vjp(d_ref[...])[0]

    return pl.pallas_call(body, name="c_ctx_bwd", out_shape=jax.ShapeDtypeStruct(cvec.shape, F32), compiler_params=_cp())(cvec, dsc)


def kernel(x, c, ctx, c_ctx, mod_w, mod_b, norm_g, ffn_w_in, ffn_w_out, ab_w_in, ab_gate_w, ab_gate_b, ab_gla_norm_g, ab_vnorm_g, ab_spatial_w, ab_spatial_b, ab_w_out, ssd_w_in, ssd_conv_w, ssd_conv_b, ssd_dt_bias, ssd_a_log, ssd_d, ssd_norm_g, ssd_w_out, final_norm_g, loss_target, m_c_ctx, m_mod_w, m_mod_b, m_norm_g, m_ffn_w_in, m_ffn_w_out, m_ab_w_in, m_ab_gate_w, m_ab_gate_b, m_ab_gla_norm_g, m_ab_vnorm_g, m_ab_spatial_w, m_ab_spatial_b, m_ab_w_out, m_ssd_w_in, m_ssd_conv_w, m_ssd_conv_b, m_ssd_dt_bias, m_ssd_a_log, m_ssd_d, m_ssd_norm_g, m_ssd_w_out, m_final_norm_g, v_c_ctx, v_mod_w, v_mod_b, v_norm_g, v_ffn_w_in, v_ffn_w_out, v_ab_w_in, v_ab_gate_w, v_ab_gate_b, v_ab_gla_norm_g, v_ab_vnorm_g, v_ab_spatial_w, v_ab_spatial_b, v_ab_w_out, v_ssd_w_in, v_ssd_conv_w, v_ssd_conv_b, v_ssd_dt_bias, v_ssd_a_log, v_ssd_d, v_ssd_norm_g, v_ssd_w_out, v_final_norm_g):
    a = dict(locals())
    me = _slot(*_mesh_pos())
    ws_mod = mod_w.shape[-1]

    fwd_small = [c] + [a[k] for k in SMALL_SHARDED]
    g_small = _ag_small("gather_small", _pack(fwd_small))
    parts = _unpack_flat(g_small, [t.shape for t in fwd_small])
    c_rows = parts[0].reshape(NDEV, D)
    full = {k: _unshard(p) for k, p in zip(SMALL_SHARDED, parts[1:])}
    c_all = jnp.concatenate([c_rows, c_ctx[None], jnp.zeros((7, D), F32)], axis=0)
    m_all = _ag_small("gather_mod", _mod_fwd(c_all, mod_w).reshape(2 * 16, ws_mod)).reshape(NDEV, 2, 16, ws_mod)
    m_mine = lax.dynamic_index_in_dim(m_all, me, axis=2, keepdims=False)
    mx = jnp.moveaxis(m_mine, 0, 1).reshape(2, N_MOD, D) + mod_b.reshape(2, N_MOD, D)
    mc = jnp.moveaxis(m_all[:, :, 8, :], 0, 1).reshape(2, N_MOD, D) + mod_b.reshape(2, N_MOD, D)
    pad2 = jnp.zeros((2, 2, D), F32)
    mod = jnp.concatenate([mx, pad2, mc, pad2], axis=1)

    big = {"l0_mixer": (ab_w_in[0], ab_w_out[0]), "l0_ffn": (ffn_w_in[0], ffn_w_out[0]),
           "l1_mixer": (ssd_w_in[0], ssd_w_out[0]), "l1_ffn": (ffn_w_in[1], ffn_w_out[1])}
    comm = _Exchange({k: tuple(w.astype(MXU_DTYPE) for w in v) for k, v in big.items()}, me)
    P = {
        "norm_g": full["norm_g"], "ab_gate_w": full["ab_gate_w"][0], "ab_gate_b": full["ab_gate_b"][0],
        "ab_gla_norm_g": ab_gla_norm_g, "ab_vnorm_g": ab_vnorm_g, "ab_spatial_w": ab_spatial_w[0], "ab_spatial_b": ab_spatial_b[0],
        "ssd_conv_w": full["ssd_conv_w"][0], "ssd_conv_b": full["ssd_conv_b"], "ssd_dt_bias": ssd_dt_bias[0],
        "ssd_a_log": ssd_a_log[0], "ssd_d": ssd_d, "ssd_norm_g": full["ssd_norm_g"], "final_norm_g": final_norm_g[None],
    }

    loss, grad_x, dmod, dP = _local_step(x[0], ctx[0], loss_target[0], mod, P, comm)

    dmx, dmc = dmod[:, 0:N_MOD].reshape(2, N_MOD * D), dmod[:, 8:8 + N_MOD].reshape(2, N_MOD * D)
    small_names = ("ab_gate_w", "ab_gate_b", "ab_gla_norm_g", "ab_vnorm_g", "ab_spatial_w", "ab_spatial_b", "norm_g", "ssd_conv_w",
                   "ssd_conv_b", "ssd_dt_bias", "ssd_a_log", "ssd_d", "ssd_norm_g", "final_norm_g")
    bwd_small = [dP[k] for k in small_names] + [dmc, dmx]
    shapes = [t.shape for t in bwd_small]
    g_bwd = _ag_small("gather_small_grads", _pack(bwd_small))
    summed = _unpack_flat(_sum_parts("sum_small_grads", g_bwd), shapes)
    gfull = dict(zip(small_names, summed[:-2]))
    dmc_sum, dmx_sum = summed[-2], summed[-1]
    dmx_all = _unpack_flat(g_bwd, shapes)[-1]
    dmx_sh = jnp.moveaxis(_my_shard(dmx_all, me, ws_mod), 0, 1)
    dm = jnp.concatenate([dmx_sh, _my_shard(dmc_sum, me, ws_mod)[:, None, :], jnp.zeros((2, 7, ws_mod), F32)], axis=1)
    d_mod_w, dsc = _mod_bwd(c_all, mod_w, dm)
    dsc_ctx = (dsc[0, 8] + dsc[1, 8])[None]
    dsc_all = _ag_small("gather_c_ctx_grad", jnp.concatenate([dsc_ctx, jnp.zeros((7, D), F32)], axis=0))
    d_c_ctx = _silu_vjp(c_ctx[None], _sum_parts("sum_c_ctx_grad", dsc_all)[0:1])[0]

    g_small_w = {
        "c_ctx": d_c_ctx, "mod_b": dmx_sum + dmc_sum, "norm_g": gfull["norm_g"], "ab_gate_w": gfull["ab_gate_w"][None],
        "ab_gate_b": gfull["ab_gate_b"][None], "ab_gla_norm_g": gfull["ab_gla_norm_g"], "ab_vnorm_g": gfull["ab_vnorm_g"],
        "ab_spatial_w": gfull["ab_spatial_w"][None], "ab_spatial_b": gfull["ab_spatial_b"][None], "ssd_conv_w": gfull["ssd_conv_w"][None],
        "ssd_conv_b": gfull["ssd_conv_b"], "ssd_dt_bias": gfull["ssd_dt_bias"][None], "ssd_a_log": gfull["ssd_a_log"][None],
        "ssd_d": gfull["ssd_d"], "ssd_norm_g": gfull["ssd_norm_g"], "final_norm_g": gfull["final_norm_g"][0],
    }
    for k in SMALL_SHARDED:
        g_small_w[k] = _my_shard(g_small_w[k], me, a[k].shape[-1])
    token = comm.start_last([d_c_ctx])
    res = _adam("adam_small", _pack([a[k] for k in SMALL]), [_pack([g_small_w[k] for k in SMALL])[None]],
                _pack([a["m_" + k] for k in SMALL]), _pack([a["v_" + k] for k in SMALL]), token)
    out = {k: vals for k, vals in zip(SMALL, zip(*[_unpack_flat(r, [a[k].shape for k in SMALL]) for r in res]))}

    def adam_big(name, w2d, parts, m2d, v2d, shape):
        return tuple(r.reshape(shape) for r in _adam(name, w2d, parts, m2d, v2d, token))

    def flat2(t):
        return t.reshape(-1, t.shape[-1])

    out["mod_w"] = adam_big("adam_mod_w", flat2(mod_w), [d_mod_w.reshape(1, -1, ws_mod)], flat2(m_mod_w), flat2(v_mod_w), mod_w.shape)

    for j, k in enumerate(("ffn_w_in", "ffn_w_out")):
        out[k] = adam_big("adam_" + k, flat2(a[k]), [comm.recv["l0_ffn"][j], comm.recv["l1_ffn"][j]], flat2(a["m_" + k]),
                          flat2(a["v_" + k]), a[k].shape)
    for j, k in enumerate(("ssd_w_in", "ssd_w_out")):
        out[k] = adam_big("adam_" + k, a[k][0], [comm.recv["l1_mixer"][j]], a["m_" + k][0], a["v_" + k][0], a[k].shape)
    recv_ab = comm.finish(out["ssd_w_out"][3])
    for j, k in enumerate(("ab_w_in", "ab_w_out")):
        out[k] = adam_big("adam_" + k, a[k][0], [recv_ab[j]], a["m_" + k][0], a["v_" + k][0], a[k].shape)

    loss_all = lax.psum(loss[0, 0], ("x", "y", "c"))
    return (loss_all, grad_x[None], *[out[k][0] for k in WEIGHTS], *[out[k][1] for k in WEIGHTS],
            *[out[k][2] for k in WEIGHTS], *[out[k][3] for k in WEIGHTS])
```
